```python
import jax, jax.numpy as jnp
from jax import lax
import numpy as np

D_MODEL = 1024
BATCH = 8
SEQ = 4096
DEPTH = 1
DEC_BATCH = 8
DEC_SEQ = 2048
PAST_LEN = 128

A_GROUPS = 8
A_WIDTH = 512
A_CHUNK = 128
A_GROUP_DIM = A_WIDTH // A_GROUPS
HG_HEADS = 8
HG_DK = 128
HG_DV = 128
HG_WIDTH = HG_HEADS * HG_DK
HG_CHUNK = 32
N_IN = 2 * A_WIDTH + 5 * HG_WIDTH + 2 * D_MODEL
N_EXPERTS = 64
TOP_K = 6
N_GROUPS = 8
TOPK_GROUPS = 4
EXPERT_DIM = 256
SHARED_DIM = 256
ROUTED_SCALE = 2.5
MOE_BLOCK = 128
ALPHA = (2.0 * DEPTH) ** 0.25
BETA = (8.0 * DEPTH) ** -0.25
LN_EPS = 1e-5
RMS_EPS = 1e-6

kernel_name = "hybrid_gmlp_hgrn2_moe_encoder"


def layer_norm(x, g, b):
    xf = x.astype(jnp.float32)
    mu = xf.mean(-1, keepdims=True)
    var = jnp.square(xf - mu).mean(-1, keepdims=True)
    return ((xf - mu) * lax.rsqrt(var + LN_EPS) * g + b).astype(x.dtype)


def split_in(h):
    sizes = (A_WIDTH, A_WIDTH, HG_WIDTH, HG_WIDTH, HG_WIDTH, HG_WIDTH, HG_WIDTH, D_MODEL, D_MODEL)
    cuts = [int(c) for c in np.cumsum(sizes)[:-1]]
    return jnp.split(h, cuts, axis=-1)


def gmlp_branch(u, v, ln_g, ln_b, ws, sb):
    u = jax.nn.gelu(u)
    v = layer_norm(jax.nn.gelu(v), ln_g, ln_b)
    B, S, _ = v.shape
    vc = v.reshape(B, S // A_CHUNK, A_CHUNK, A_GROUPS, A_GROUP_DIM)
    mixed = jnp.einsum('gts,bnsgc->bntgc', ws, vc) + sb[:, :, None]
    return u * mixed.reshape(B, S, A_WIDTH)


def hgrn2_scan(q, k, logf, v):
    B, S, H, DK = q.shape
    DV = v.shape[-1]
    n = S // HG_CHUNK

    def chunks(t):
        return t.reshape(B, n, HG_CHUNK, H, t.shape[-1]).transpose(1, 0, 3, 2, 4)

    tri = jnp.tril(jnp.ones((HG_CHUNK, HG_CHUNK), dtype=bool))

    def step(state, inp):
        qc, kc, gc, vc = inp
        b = jnp.cumsum(gc, axis=2)
        diff = b[:, :, :, None, :] - b[:, :, None, :, :]
        dec = jnp.where(tri[:, :, None], jnp.exp(jnp.minimum(diff, 0.0)), 0.0)
        scores = jnp.einsum('bhtk,bhsk,bhtsk->bhts', qc, kc, dec)
        o = (jnp.einsum('bhts,bhsv->bhtv', scores, vc)
             + jnp.einsum('bhtk,bhkv->bhtv', qc * jnp.exp(b), state))
        b_last = b[:, :, -1:, :]
        state = (jnp.exp(b_last[:, :, 0, :, None]) * state
                 + jnp.einsum('bhsk,bhsv->bhkv', kc * jnp.exp(b_last - b), vc))
        return state, o

    init = jnp.zeros((B, H, DK, DV), jnp.float32)
    _, o = lax.scan(step, init, (chunks(q), chunks(k), chunks(logf), chunks(v)))
    return o.transpose(1, 0, 3, 2, 4).reshape(B, S, H, DV)


def hgrn2_branch(q, f_fwd, f_bwd, i_in, g_out, lb, norm_g):
    B, S, _ = q.shape
    f32 = jnp.float32

    def heads(t):
        return t.astype(f32).reshape(B, S, HG_HEADS, HG_DK)

    qh = heads(jax.nn.silu(q))
    vh = heads(i_in)

    def gates(fpre, lb_dir):
        f = lb_dir + (1.0 - lb_dir) * jax.nn.sigmoid(fpre.astype(f32))
        return heads(1.0 - f), heads(jnp.log(f))

    kf, gf = gates(f_fwd, lb[0])
    kb, gb = gates(f_bwd, lb[1])
    o_f = hgrn2_scan(qh, kf, gf, vh)
    o_b = jnp.flip(hgrn2_scan(jnp.flip(qh, 1), jnp.flip(kb, 1), jnp.flip(gb, 1), jnp.flip(vh, 1)), 1)
    o = o_f + o_b
    o = o * lax.rsqrt(jnp.mean(jnp.square(o), -1, keepdims=True) + RMS_EPS)
    o = o.reshape(B, S, HG_WIDTH) * norm_g * jax.nn.silu(g_out.astype(f32))
    return o.astype(q.dtype)


def moe_ffn(x, w_router, router_bias, w_e_gate, w_e_up, w_e_down, w_sh_gate, w_sh_up, w_sh_down):
    B, S, D = x.shape
    f32 = jnp.float32
    xt = x.reshape(-1, D)
    T = xt.shape[0]
    scores = jax.nn.sigmoid((xt @ w_router).astype(f32))
    biased = scores + router_bias.astype(f32)
    grp_score = lax.top_k(biased.reshape(T, N_GROUPS, N_EXPERTS // N_GROUPS), 2)[0].sum(-1)
    _, grp_idx = lax.top_k(grp_score, TOPK_GROUPS)
    grp_mask = jnp.any(grp_idx[..., None] == jnp.arange(N_GROUPS), axis=1)
    allowed = jnp.where(jnp.repeat(grp_mask, N_EXPERTS // N_GROUPS, axis=1), biased, -jnp.inf)
    _, idx = lax.top_k(allowed, TOP_K)
    w = jnp.take_along_axis(scores, idx, axis=1)
    w = w / w.sum(-1, keepdims=True) * ROUTED_SCALE
    TK = T * TOP_K
    flat_e = idx.reshape(-1).astype(jnp.int32)
    order = jnp.argsort(flat_e)
    sorted_e = flat_e[order]
    tok = (order // TOP_K).astype(jnp.int32)
    gate = w.reshape(-1)[order]
    counts = jnp.bincount(flat_e, length=N_EXPERTS).astype(jnp.int32)
    padded = (counts + MOE_BLOCK - 1) // MOE_BLOCK * MOE_BLOCK
    pad_end = jnp.cumsum(padded)
    pad_start = pad_end - padded
    start = jnp.cumsum(counts) - counts
    dest = pad_start[sorted_e] + jnp.arange(TK, dtype=jnp.int32) - start[sorted_e]
    n_blocks = -(-TK // MOE_BLOCK) + N_EXPERTS
    P = n_blocks * MOE_BLOCK
    buf_tok = jnp.full((P,), T, jnp.int32).at[dest].set(tok)
    buf_gate = jnp.zeros((P,), f32).at[dest].set(gate)
    blk_exp = jnp.minimum(jnp.searchsorted(pad_end, jnp.arange(n_blocks, dtype=jnp.int32) * MOE_BLOCK, side='right'), N_EXPERTS - 1)
    x_pad = jnp.concatenate([xt, jnp.zeros((1, D), xt.dtype)], axis=0)

    def run_block(args):
        t_idx, g, e = args
        xb = x_pad[t_idx]
        hb = jax.nn.silu(xb @ w_e_gate[e]) * (xb @ w_e_up[e])
        return (hb @ w_e_down[e]).astype(f32) * g[:, None]

    yb = lax.map(run_block, (buf_tok.reshape(n_blocks, MOE_BLOCK), buf_gate.reshape(n_blocks, MOE_BLOCK), blk_exp))
    routed = jnp.zeros((T + 1, D), f32).at[buf_tok].add(yb.reshape(P, D))[:T]
    shared = (jax.nn.silu(xt @ w_sh_gate) * (xt @ w_sh_up)) @ w_sh_down
    return (routed + shared.astype(f32)).astype(x.dtype).reshape(B, S, D)


def encoder_layer(x, lb, w_in, a_ln_g, a_ln_b, a_ws, a_sb, hg_norm_g, w_pa, w_pb, w_o,
                  ln1_g, ln1_b, w_router, router_bias, w_e_gate, w_e_up, w_e_down,
                  w_sh_gate, w_sh_up, w_sh_down, ln2_g, ln2_b):
    h = x @ w_in
    u_a, v_a, q, f_fwd, f_bwd, i_in, g_out, gate_a, gate_b = split_in(h)
    a = gmlp_branch(u_a, v_a, a_ln_g, a_ln_b, a_ws, a_sb) @ w_pa
    r = hgrn2_branch(q, f_fwd, f_bwd, i_in, g_out, lb, hg_norm_g) @ w_pb
    mixed = jax.nn.sigmoid(gate_a) * a + jax.nn.sigmoid(gate_b) * r
    x = layer_norm(ALPHA * x + mixed @ w_o, ln1_g, ln1_b)
    y = moe_ffn(x, w_router, router_bias, w_e_gate, w_e_up, w_e_down, w_sh_gate, w_sh_up, w_sh_down)
    return layer_norm(ALPHA * x + y, ln2_g, ln2_b)


def encode(x, p):
    (ln_in_g, ln_in_b, w_in, a_ln_g, a_ln_b, a_ws, a_sb, hg_lb_logits, hg_norm_g, w_pa, w_pb, w_o,
     ln1_g, ln1_b, w_router, router_bias, w_e_gate, w_e_up, w_e_down, w_sh_gate, w_sh_up, w_sh_down,
     ln2_g, ln2_b) = p
    x = layer_norm(x, ln_in_g, ln_in_b)
    lb_all = jnp.cumsum(jax.nn.softmax(hg_lb_logits.astype(jnp.float32), axis=1), axis=1)
    for l in range(DEPTH):
        x = encoder_layer(x, lb_all[:, l], w_in[l], a_ln_g[l], a_ln_b[l], a_ws[l], a_sb[l], hg_norm_g[l],
                          w_pa[l], w_pb[l], w_o[l], ln1_g[l], ln1_b[l], w_router[l], router_bias[l],
                          w_e_gate[l], w_e_up[l], w_e_down[l], w_sh_gate[l], w_sh_up[l], w_sh_down[l],
                          ln2_g[l], ln2_b[l])
    return x


def setup_inputs(seed: int = 0) -> dict:
    key = jax.random.key(seed)
    ks = iter(jax.random.split(key, 40))
    nrm = lambda shape, s: jax.random.normal(next(ks), shape, jnp.float32) * s
    gain = lambda shape: 1.0 + nrm(shape, 0.02)
    L, D, E = DEPTH, D_MODEL, N_EXPERTS
    return {
        "x_prompt": nrm((BATCH, SEQ, D), 1.0),
        "x_sample": nrm((DEC_BATCH, DEC_SEQ, D), 1.0),
        "ln_in_g": gain((D,)),
        "ln_in_b": nrm((D,), 0.02),
        "w_in": nrm((L, D, N_IN), D ** -0.5),
        "a_ln_g": gain((L, A_WIDTH)),
        "a_ln_b": nrm((L, A_WIDTH), 0.02),
        "a_ws": nrm((L, A_GROUPS, A_CHUNK, A_CHUNK), A_CHUNK ** -0.5),
        "a_sb": gain((L, A_CHUNK, A_GROUPS)),
        "hg_lb_logits": nrm((2, L + 1, HG_WIDTH), 0.5),
        "hg_norm_g": gain((L, HG_WIDTH)),
        "w_pa": nrm((L, A_WIDTH, D), A_WIDTH ** -0.5 * BETA),
        "w_pb": nrm((L, HG_WIDTH, D), HG_WIDTH ** -0.5 * BETA),
        "w_o": nrm((L, D, D), D ** -0.5 * BETA),
        "ln1_g": gain((L, D)),
        "ln1_b": nrm((L, D), 0.02),
        "w_router": nrm((L, D, E), D ** -0.5),
        "router_bias": nrm((L, E), 0.01),
        "w_e_gate": nrm((L, E, D, EXPERT_DIM), D ** -0.5),
        "w_e_up": nrm((L, E, D, EXPERT_DIM), D ** -0.5),
        "w_e_down": nrm((L, E, EXPERT_DIM, D), EXPERT_DIM ** -0.5 * BETA),
        "w_sh_gate": nrm((L, D, SHARED_DIM), D ** -0.5),
        "w_sh_up": nrm((L, D, SHARED_DIM), D ** -0.5),
        "w_sh_down": nrm((L, SHARED_DIM, D), SHARED_DIM ** -0.5 * BETA),
        "ln2_g": gain((L, D)),
        "ln2_b": nrm((L, D), 0.02),
    }


def reference(x_prompt, x_sample, ln_in_g, ln_in_b, w_in, a_ln_g, a_ln_b, a_ws, a_sb, hg_lb_logits,
              hg_norm_g, w_pa, w_pb, w_o, ln1_g, ln1_b, w_router, router_bias, w_e_gate, w_e_up,
              w_e_down, w_sh_gate, w_sh_up, w_sh_down, ln2_g, ln2_b):
    params = (ln_in_g, ln_in_b, w_in, a_ln_g, a_ln_b, a_ws, a_sb, hg_lb_logits, hg_norm_g, w_pa, w_pb, w_o,
              ln1_g, ln1_b, w_router, router_bias, w_e_gate, w_e_up, w_e_down, w_sh_gate, w_sh_up,
              w_sh_down, ln2_g, ln2_b)
    y_prompt = encode(x_prompt, params)
    y_sample = encode(x_sample, params)
    return (y_prompt, y_sample)
```

```python
import functools

import jax
import jax.numpy as jnp
from jax import lax
from jax.experimental import pallas as pl
from jax.experimental.pallas import tpu as pltpu

F32 = jnp.float32
BF16 = jnp.bfloat16

D_MODEL = 1024
A_GROUPS = 8
A_WIDTH = 512
A_CHUNK = 128
HG_HEADS = 8
HG_DK = 128
HG_WIDTH = HG_HEADS * HG_DK
N_IN = 2 * A_WIDTH + 5 * HG_WIDTH + 2 * D_MODEL
N_EXPERTS = 64
TOP_K = 6
N_GROUPS = 8
TOPK_GROUPS = 4
GROUP_SIZE = N_EXPERTS // N_GROUPS
EXPERT_DIM = 256
SHARED_DIM = 256
ROUTED_SCALE = 2.5
DEPTH = 1
ALPHA = (2.0 * DEPTH) ** 0.25
LN_EPS = 1e-5
RMS_EPS = 1e-6

LANES = 128
TOKEN_TILE = 256
HG_CHUNK = 128
EXPERT_BLOCK = 256
VMEM_LIMIT = 56 * 1024 * 1024

_O_U, _O_V, _O_Q, _O_FF, _O_FB, _O_I, _O_G, _O_GA, _O_GB = (
    0, 512, 1024, 2048, 3072, 4096, 5120, 6144, 7168)


def _layer_norm(x, g, b):
    mu = jnp.mean(x, axis=-1, keepdims=True)
    xc = x - mu
    var = jnp.mean(xc * xc, axis=-1, keepdims=True)
    return xc * lax.rsqrt(var + LN_EPS) * g + b


def _bdot(a, b):
    return jnp.dot(a.astype(BF16), b.astype(BF16), preferred_element_type=F32)


def _const_spec(shape):
    nd = len(shape)
    return pl.BlockSpec(shape, lambda *_: (0,) * nd, pipeline_mode=pl.Buffered(1))


def _inproj_kernel(x_ref, lng_ref, lnb_ref, win_ref, alng_ref, alnb_ref, wsp_ref, sbf_ref, lb_ref,
                   wpa_ref, ag_ref, q_ref, gf_ref, gb_ref, iv_ref, so_ref, sgb_ref, gmin_ref):
    tm = x_ref.shape[0]
    xb = _layer_norm(x_ref[...], lng_ref[...], lnb_ref[...]).astype(BF16)

    def sec(lo, width):
        return jnp.dot(xb, win_ref[:, lo:lo + width], preferred_element_type=F32)

    u = jax.nn.gelu(sec(_O_U, A_WIDTH))
    v = _layer_norm(jax.nn.gelu(sec(_O_V, A_WIDTH)), alng_ref[...], alnb_ref[...]).astype(BF16)
    lane = lax.broadcasted_iota(jnp.int32, (A_CHUNK, LANES), 1)
    left = lane < (A_WIDTH // A_GROUPS)
    zero = jnp.zeros((A_CHUNK, LANES), BF16)
    chunks = []
    for c in range(tm // A_CHUNK):
        vc = v[c * A_CHUNK:(c + 1) * A_CHUNK]
        cols = []
        for p in range(A_GROUPS // 2):
            vp = vc[:, p * LANES:(p + 1) * LANES]
            rhs = jnp.concatenate([jnp.where(left, vp, zero), jnp.where(left, zero, vp)], axis=0)
            cols.append(jnp.dot(wsp_ref[p], rhs, preferred_element_type=F32))
        chunks.append(jnp.concatenate(cols, axis=1) + sbf_ref[...])
    mixed = jnp.concatenate(chunks, axis=0)
    a = _bdot(u * mixed, wpa_ref[...])
    ag_ref[...] = (jax.nn.sigmoid(sec(_O_GA, D_MODEL)) * a).astype(BF16)

    q_ref[...] = jax.nn.silu(sec(_O_Q, HG_WIDTH)).astype(BF16)
    mins = []
    for d, (off, g_ref) in enumerate(((_O_FF, gf_ref), (_O_FB, gb_ref))):
        lb = lb_ref[d:d + 1, :]
        f = lb + (1.0 - lb) * jax.nn.sigmoid(sec(off, HG_WIDTH))
        g = jnp.log(f)
        g_ref[...] = g
        half = jnp.sum(g.reshape(tm // (HG_CHUNK // 2), HG_CHUNK // 2, HG_WIDTH), axis=1)
        mins.append(jnp.min(half, axis=-1, keepdims=True))
    gmin_ref[0] = jnp.broadcast_to(jnp.concatenate(mins, axis=0), gmin_ref.shape[1:])
    iv_ref[...] = sec(_O_I, HG_WIDTH).astype(BF16)
    so_ref[...] = jax.nn.silu(sec(_O_G, HG_WIDTH)).astype(BF16)
    sgb_ref[...] = jax.nn.sigmoid(sec(_O_GB, D_MODEL)).astype(BF16)


def _inproj(x, p):
    T = x.shape[0]
    tm = TOKEN_TILE
    nt = T // tm
    nh = 2 * tm // (HG_CHUNK // 2)
    tok = lambda w: pl.BlockSpec((tm, w), lambda i: (i, 0))
    outs = (
        jax.ShapeDtypeStruct((T, D_MODEL), BF16),
        jax.ShapeDtypeStruct((T, HG_WIDTH), BF16),
        jax.ShapeDtypeStruct((T, HG_WIDTH), F32),
        jax.ShapeDtypeStruct((T, HG_WIDTH), F32),
        jax.ShapeDtypeStruct((T, HG_WIDTH), BF16),
        jax.ShapeDtypeStruct((T, HG_WIDTH), BF16),
        jax.ShapeDtypeStruct((T, D_MODEL), BF16),
        jax.ShapeDtypeStruct((nt, nh, LANES), F32),
    )
    return pl.pallas_call(
        _inproj_kernel,
        grid=(nt,),
        in_specs=[
            tok(D_MODEL),
            _const_spec((1, D_MODEL)), _const_spec((1, D_MODEL)),
            _const_spec((D_MODEL, N_IN)),
            _const_spec((1, A_WIDTH)), _const_spec((1, A_WIDTH)),
            _const_spec((A_GROUPS // 2, A_CHUNK, 2 * A_CHUNK)),
            _const_spec((A_CHUNK, A_WIDTH)),
            _const_spec((2, HG_WIDTH)),
            _const_spec((A_WIDTH, D_MODEL)),
        ],
        out_specs=(tok(D_MODEL), tok(HG_WIDTH), tok(HG_WIDTH), tok(HG_WIDTH), tok(HG_WIDTH),
                   tok(HG_WIDTH), tok(D_MODEL), pl.BlockSpec((1, nh, LANES), lambda i: (i, 0, 0))),
        out_shape=outs,
        compiler_params=pltpu.CompilerParams(
            dimension_semantics=("parallel",), vmem_limit_bytes=VMEM_LIMIT),
        name="inproj",
    )(x, p["ln_in_g"], p["ln_in_b"], p["w_in"], p["a_ln_g"], p["a_ln_b"], p["wsp"], p["sbf"],
      p["lb"], p["w_pa"])


def _hgrn_chunk(q, g, v, st_ref, h, tri, mask, fwd):
    C = q.shape[0]
    ghi = g.astype(BF16)
    glo = (g - ghi.astype(F32)).astype(BF16)
    b2 = jnp.dot(tri, jnp.concatenate([ghi, glo], axis=1), preferred_element_type=F32)
    b = b2[:, :HG_DK] + b2[:, HG_DK:]
    mid = C // 2 - 1 if fwd else C // 2
    end = C - 1 if fwd else 0
    r = b[mid:mid + 1, :]
    b_end = b[end:end + 1, :]
    qt = q.astype(F32) * jnp.exp(b - r)
    kt = (1.0 - jnp.exp(g)) * jnp.exp(r - b)
    scores = lax.dot_general(qt.astype(BF16), kt.astype(BF16), (((1,), (1,)), ((), ())),
                             preferred_element_type=F32)
    scores = jnp.where(mask, scores, 0.0)
    st = st_ref[h]
    o = jnp.dot(scores.astype(BF16), v, preferred_element_type=F32)
    o = o + lax.dot_general((qt * jnp.exp(r)).astype(BF16), st.astype(BF16),
                            (((1,), (1,)), ((), ())), preferred_element_type=F32)
    kh = (kt * jnp.exp(b_end - r)).astype(BF16)
    st_ref[h] = st * jnp.exp(b_end) + lax.dot_general(
        v, kh, (((0,), (0,)), ((), ())), preferred_element_type=F32)
    return o


def _hgrn_kernel(qf_ref, qb_ref, gf_ref, gb_ref, vf_ref, vb_ref, tril_ref, triu_ref,
                 of_ref, ob_ref, sf_ref, sb_ref):
    @pl.when(pl.program_id(1) == 0)
    def _():
        sf_ref[...] = jnp.zeros_like(sf_ref)
        sb_ref[...] = jnp.zeros_like(sb_ref)

    C = qf_ref.shape[0]
    row = lax.broadcasted_iota(jnp.int32, (C, C), 0)
    col = lax.broadcasted_iota(jnp.int32, (C, C), 1)
    for h in range(HG_HEADS):
        sl = slice(h * HG_DK, (h + 1) * HG_DK)
        of_ref[:, sl] = _hgrn_chunk(qf_ref[:, sl], gf_ref[:, sl], vf_ref[:, sl], sf_ref, h,
                                    tril_ref[...], row >= col, True)
        ob_ref[:, sl] = _hgrn_chunk(qb_ref[:, sl], gb_ref[:, sl], vb_ref[:, sl], sb_ref, h,
                                    triu_ref[...], row <= col, False)


def _hgrn(q, gf, gb, iv, batch):
    T = q.shape[0]
    C = HG_CHUNK
    nc = T // batch // C
    fwd = pl.BlockSpec((C, HG_WIDTH), lambda b, j: (b * nc + j, 0))
    bwd = pl.BlockSpec((C, HG_WIDTH), lambda b, j: (b * nc + nc - 1 - j, 0))
    row = lax.broadcasted_iota(jnp.int32, (C, C), 0)
    col = lax.broadcasted_iota(jnp.int32, (C, C), 1)
    tril = (row >= col).astype(BF16)
    triu = (row <= col).astype(BF16)
    return pl.pallas_call(
        _hgrn_kernel,
        grid=(batch, nc),
        in_specs=[fwd, bwd, fwd, bwd, fwd, bwd, _const_spec((C, C)), _const_spec((C, C))],
        out_specs=(fwd, bwd),
        out_shape=(jax.ShapeDtypeStruct((T, HG_WIDTH), F32), jax.ShapeDtypeStruct((T, HG_WIDTH), F32)),
        scratch_shapes=[pltpu.VMEM((HG_HEADS, HG_DK, HG_DK), F32),
                        pltpu.VMEM((HG_HEADS, HG_DK, HG_DK), F32)],
        compiler_params=pltpu.CompilerParams(
            dimension_semantics=("parallel", "arbitrary"), vmem_limit_bytes=VMEM_LIMIT),
        name="hgrn",
    )(q, q, gf, gb, iv, iv, tril, triu)


def _xor_partner(x, lane, s):
    return jnp.where((lane & s) != 0, pltpu.roll(x, s, 1), pltpu.roll(x, LANES - s, 1))


def _mix_kernel(x_ref, lng_ref, lnb_ref, of_ref, ob_ref, so_ref, sgb_ref, ag_ref, ng_ref, wpb_ref,
                wo_ref, l1g_ref, l1b_ref, wr_ref, rb_ref, tri_ref, x1_ref, route_ref, cnt_ref,
                carry_ref):
    tm = x_ref.shape[0]

    @pl.when(pl.program_id(0) == 0)
    def _():
        carry_ref[...] = jnp.zeros_like(carry_ref)

    o = of_ref[...] + ob_ref[...]
    heads = []
    for h in range(HG_HEADS):
        oh = o[:, h * HG_DK:(h + 1) * HG_DK]
        heads.append(oh * lax.rsqrt(jnp.mean(oh * oh, axis=-1, keepdims=True) + RMS_EPS))
    rn = jnp.concatenate(heads, axis=1) * ng_ref[...] * so_ref[...].astype(F32)
    r = _bdot(rn, wpb_ref[...])
    mixed = ag_ref[...].astype(F32) + sgb_ref[...].astype(F32) * r
    y = _bdot(mixed, wo_ref[...])
    xn = _layer_norm(x_ref[...], lng_ref[...], lnb_ref[...])
    x1 = _layer_norm(ALPHA * xn + y, l1g_ref[...], l1b_ref[...])
    x1_ref[...] = x1

    scores = jax.nn.sigmoid(_bdot(x1, wr_ref[...]))
    lane = lax.broadcasted_iota(jnp.int32, (tm, LANES), 1)
    valid = lane < N_EXPERTS
    neg = jnp.float32(-jnp.inf)
    biased = jnp.where(valid, scores + rb_ref[...], neg)
    m1, m2 = biased, jnp.full_like(biased, neg)
    s = 1
    while s < GROUP_SIZE:
        p1, p2 = _xor_partner(m1, lane, s), _xor_partner(m2, lane, s)
        m1, m2 = jnp.maximum(m1, p1), jnp.maximum(jnp.minimum(m1, p1), jnp.maximum(m2, p2))
        s *= 2
    gs = m1 + m2
    ahead = jnp.zeros((tm, LANES), F32)
    for d in range(1, N_GROUPS):
        lower = pltpu.roll(gs, d * GROUP_SIZE, 1)
        higher = pltpu.roll(gs, LANES - d * GROUP_SIZE, 1)
        ahead = ahead + jnp.where(lower >= gs, 1.0, 0.0) + jnp.where(higher > gs, 1.0, 0.0)
    allowed = jnp.where(valid & (ahead < TOPK_GROUPS), biased, neg)
    sel = jnp.zeros((tm, LANES), F32)
    lane_f = lane.astype(F32)
    picks = []
    for _ in range(TOP_K):
        m = jnp.max(allowed, axis=-1, keepdims=True)
        first = jnp.min(jnp.where(allowed == m, lane_f, float(LANES)), axis=-1, keepdims=True)
        hit = lane_f == first
        picks.append((first, hit, jnp.sum(jnp.where(hit, scores, 0.0), axis=-1, keepdims=True)))
        sel = jnp.where(hit, 1.0, sel)
        allowed = jnp.where(hit, neg, allowed)
    wsum = picks[0][2]
    for pk in picks[1:]:
        wsum = wsum + pk[2]
    before = jnp.dot(tri_ref[...], sel.astype(BF16), preferred_element_type=F32) + carry_ref[0:1, :]
    total = carry_ref[0:1, :] + jnp.sum(sel, axis=0, keepdims=True)
    carry_ref[...] = jnp.broadcast_to(total, carry_ref.shape)
    cnt_ref[...] = jnp.broadcast_to(total, cnt_ref.shape)
    route = jnp.zeros((tm, LANES), F32)
    for k, (first, hit, w) in enumerate(picks):
        rank = jnp.sum(jnp.where(hit, before, 0.0), axis=-1, keepdims=True)
        route = jnp.where(lane == k, first, route)
        route = jnp.where(lane == 8 + k, rank, route)
        route = jnp.where(lane == 16 + k, w / wsum * ROUTED_SCALE, route)
    route_ref[...] = route


def _mix(x, of, ob, so, sgb, ag, p):
    T = x.shape[0]
    tm = TOKEN_TILE
    tok = lambda w: pl.BlockSpec((tm, w), lambda i: (i, 0))
    row = lax.broadcasted_iota(jnp.int32, (tm, tm), 0)
    col = lax.broadcasted_iota(jnp.int32, (tm, tm), 1)
    tri = (row > col).astype(BF16)
    return pl.pallas_call(
        _mix_kernel,
        grid=(T // tm,),
        in_specs=[
            tok(D_MODEL), _const_spec((1, D_MODEL)), _const_spec((1, D_MODEL)),
            tok(HG_WIDTH), tok(HG_WIDTH), tok(HG_WIDTH), tok(D_MODEL), tok(D_MODEL),
            _const_spec((1, HG_WIDTH)),
            _const_spec((HG_WIDTH, D_MODEL)), _const_spec((D_MODEL, D_MODEL)),
            _const_spec((1, D_MODEL)), _const_spec((1, D_MODEL)),
            _const_spec((D_MODEL, LANES)), _const_spec((1, LANES)),
            _const_spec((tm, tm)),
        ],
        out_specs=(tok(D_MODEL), tok(LANES), pl.BlockSpec((8, LANES), lambda i: (0, 0))),
        out_shape=(jax.ShapeDtypeStruct((T, D_MODEL), F32),
                   jax.ShapeDtypeStruct((T, LANES), F32),
                   jax.ShapeDtypeStruct((8, LANES), F32)),
        scratch_shapes=[pltpu.VMEM((8, LANES), F32)],
        compiler_params=pltpu.CompilerParams(
            dimension_semantics=("arbitrary",), vmem_limit_bytes=VMEM_LIMIT),
        name="mix",
    )(x, p["ln_in_g"], p["ln_in_b"], of, ob, so, sgb, ag, p["hg_norm_g"], p["w_pb"], p["w_o"],
      p["ln1_g"], p["ln1_b"], p["w_router"], p["router_bias"], tri)


def _dispatch_kernel(dest_ref, x1_ref, xs_in_ref, xs_ref, sem):
    del xs_in_ref
    tm = x1_ref.shape[0]

    def row_copy(t, k):
        return pltpu.make_async_copy(
            x1_ref.at[pl.ds(t, 1)], xs_ref.at[pl.ds(dest_ref[0, 0, k * tm + t], 1)], sem)

    def start(t, c):
        for k in range(TOP_K):
            row_copy(t, k).start()
        return c

    def wait(t, c):
        for k in range(TOP_K):
            row_copy(t, k).wait()
        return c

    lax.fori_loop(0, tm, start, 0)
    lax.fori_loop(0, tm, wait, 0)


def _dispatch(x1, dest3, n_rows):
    T = x1.shape[0]
    tm = TOKEN_TILE
    zeros = jnp.zeros((n_rows, D_MODEL), F32)
    return pl.pallas_call(
        _dispatch_kernel,
        grid=(T // tm,),
        in_specs=[
            pl.BlockSpec((1, 1, TOP_K * tm), lambda i: (i, 0, 0), memory_space=pltpu.SMEM),
            pl.BlockSpec((tm, D_MODEL), lambda i: (i, 0)),
            pl.BlockSpec(memory_space=pl.ANY),
        ],
        out_specs=pl.BlockSpec(memory_space=pl.ANY),
        out_shape=jax.ShapeDtypeStruct((n_rows, D_MODEL), F32),
        scratch_shapes=[pltpu.SemaphoreType.DMA],
        input_output_aliases={2: 0},
        compiler_params=pltpu.CompilerParams(
            dimension_semantics=("arbitrary",), vmem_limit_bytes=VMEM_LIMIT),
        name="dispatch",
    )(dest3, x1, zeros)


def _experts_kernel(be_ref, nu_ref, xs_ref, wg_ref, wu_ref, wd_ref, ys_ref):
    del be_ref

    @pl.when(pl.program_id(0) < nu_ref[0])
    def _():
        xb = xs_ref[...].astype(BF16)
        hb = (jax.nn.silu(jnp.dot(xb, wg_ref[0], preferred_element_type=F32))
              * jnp.dot(xb, wu_ref[0], preferred_element_type=F32))
        ys_ref[...] = jnp.dot(hb.astype(BF16), wd_ref[0], preferred_element_type=F32)

    @pl.when(pl.program_id(0) >= nu_ref[0])
    def _():
        ys_ref[...] = jnp.zeros_like(ys_ref)


def _experts(xs, blk_exp, n_used, p):
    n_rows = xs.shape[0]
    mb = EXPERT_BLOCK
    grid_spec = pltpu.PrefetchScalarGridSpec(
        num_scalar_prefetch=2,
        grid=(n_rows // mb,),
        in_specs=[
            pl.BlockSpec((mb, D_MODEL), lambda i, be, nu: (i, 0)),
            pl.BlockSpec((1, D_MODEL, EXPERT_DIM), lambda i, be, nu: (be[i], 0, 0)),
            pl.BlockSpec((1, D_MODEL, EXPERT_DIM), lambda i, be, nu: (be[i], 0, 0)),
            pl.BlockSpec((1, EXPERT_DIM, D_MODEL), lambda i, be, nu: (be[i], 0, 0)),
        ],
        out_specs=pl.BlockSpec((mb, D_MODEL), lambda i, be, nu: (i, 0)),
    )
    return pl.pallas_call(
        _experts_kernel,
        grid_spec=grid_spec,
        out_shape=jax.ShapeDtypeStruct((n_rows, D_MODEL), F32),
        compiler_params=pltpu.CompilerParams(
            dimension_semantics=("arbitrary",), vmem_limit_bytes=VMEM_LIMIT),
        name="experts",
    )(blk_exp, n_used, xs, p["w_e_gate"], p["w_e_up"], p["w_e_down"])


def _final_kernel(dest_ref, x1_ref, route_ref, ys_ref, wsg_ref, wsu_ref, wsd_ref, l2g_ref, l2b_ref,
                  out_ref, yg_ref, sem):
    tm = x1_ref.shape[0]

    def row_copy(t, k):
        return pltpu.make_async_copy(
            ys_ref.at[pl.ds(dest_ref[0, 0, k * tm + t], 1)], yg_ref.at[k, pl.ds(t, 1)], sem)

    def start(t, c):
        for k in range(TOP_K):
            row_copy(t, k).start()
        return c

    def wait(t, c):
        for k in range(TOP_K):
            row_copy(t, k).wait()
        return c

    lax.fori_loop(0, tm, start, 0)
    x1 = x1_ref[...]
    xb = x1.astype(BF16)
    hs = (jax.nn.silu(jnp.dot(xb, wsg_ref[...], preferred_element_type=F32))
          * jnp.dot(xb, wsu_ref[...], preferred_element_type=F32))
    shared = jnp.dot(hs.astype(BF16), wsd_ref[...], preferred_element_type=F32)
    lax.fori_loop(0, tm, wait, 0)
    route = route_ref[...]
    routed = yg_ref[0] * route[:, 16:17]
    for k in range(1, TOP_K):
        routed = routed + yg_ref[k] * route[:, 16 + k:17 + k]
    out_ref[...] = _layer_norm(ALPHA * x1 + (routed + shared), l2g_ref[...], l2b_ref[...])


def _final(x1, route, dest3, ys, p):
    T = x1.shape[0]
    tm = TOKEN_TILE
    tok = lambda w: pl.BlockSpec((tm, w), lambda i: (i, 0))
    return pl.pallas_call(
        _final_kernel,
        grid=(T // tm,),
        in_specs=[
            pl.BlockSpec((1, 1, TOP_K * tm), lambda i: (i, 0, 0), memory_space=pltpu.SMEM),
            tok(D_MODEL), tok(LANES),
            pl.BlockSpec(memory_space=pl.ANY),
            _const_spec((D_MODEL, SHARED_DIM)), _const_spec((D_MODEL, SHARED_DIM)),
            _const_spec((SHARED_DIM, D_MODEL)),
            _const_spec((1, D_MODEL)), _const_spec((1, D_MODEL)),
        ],
        out_specs=tok(D_MODEL),
        out_shape=jax.ShapeDtypeStruct((T, D_MODEL), F32),
        scratch_shapes=[pltpu.VMEM((TOP_K, tm, D_MODEL), F32), pltpu.SemaphoreType.DMA],
        compiler_params=pltpu.CompilerParams(
            dimension_semantics=("arbitrary",), vmem_limit_bytes=VMEM_LIMIT),
        name="final",
    )(dest3, x1, route, ys, p["w_sh_gate"], p["w_sh_up"], p["w_sh_down"], p["ln2_g"], p["ln2_b"])


def _routing_layout(route, counts_row, n_tokens):
    tm, mb = TOKEN_TILE, EXPERT_BLOCK
    n_blocks = -(-n_tokens * TOP_K // mb) + N_EXPERTS
    idx = route[:, 0:TOP_K].astype(jnp.int32)
    rank = route[:, 8:8 + TOP_K].astype(jnp.int32)
    counts = counts_row[:N_EXPERTS].astype(jnp.int32)
    padded = (counts + mb - 1) // mb * mb
    pad_end = jnp.cumsum(padded)
    pad_start = pad_end - padded
    dest = pad_start[idx] + rank
    dest3 = dest.reshape(n_tokens // tm, tm, TOP_K).transpose(0, 2, 1).reshape(
        n_tokens // tm, 1, TOP_K * tm)
    blk_exp = jnp.minimum(
        jnp.searchsorted(pad_end, jnp.arange(n_blocks, dtype=jnp.int32) * mb, side="right"),
        N_EXPERTS - 1).astype(jnp.int32)
    n_used = (pad_end[-1:] // mb).astype(jnp.int32)
    return dest3, blk_exp, n_used, n_blocks * mb


def _encode(x, p):
    batch, seq, _ = x.shape
    T = batch * seq
    xt = x.reshape(T, D_MODEL)
    ag, q, gf, gb, iv, so, sgb, _ = _inproj(xt, p)
    of, ob = _hgrn(q, gf, gb, iv, batch)
    x1, route, cnt = _mix(xt, of, ob, so, sgb, ag, p)
    dest3, blk_exp, n_used, n_rows = _routing_layout(route, cnt[0], T)
    xs = _dispatch(x1, dest3, n_rows)
    ys = _experts(xs, blk_exp, n_used, p)
    out = _final(x1, route, dest3, ys, p)
    return out.reshape(batch, seq, D_MODEL)


def _prepare_params(ln_in_g, ln_in_b, w_in, a_ln_g, a_ln_b, a_ws, a_sb, hg_lb_logits, hg_norm_g,
                    w_pa, w_pb, w_o, ln1_g, ln1_b, w_router, router_bias, w_e_gate, w_e_up,
                    w_e_down, w_sh_gate, w_sh_up, w_sh_down, ln2_g, ln2_b):
    l = 0
    row = lambda v: v.reshape(1, -1).astype(F32)
    ws = a_ws[l].astype(BF16)
    wsp = jnp.concatenate([ws[0::2], ws[1::2]], axis=2)
    sbf = jnp.repeat(a_sb[l].astype(F32), A_WIDTH // A_GROUPS, axis=1)
    lb = jnp.cumsum(jax.nn.softmax(hg_lb_logits.astype(F32), axis=1), axis=1)[:, l]
    pad = LANES - N_EXPERTS
    return dict(
        ln_in_g=row(ln_in_g), ln_in_b=row(ln_in_b), w_in=w_in[l].astype(BF16),
        a_ln_g=row(a_ln_g[l]), a_ln_b=row(a_ln_b[l]), wsp=wsp, sbf=sbf, lb=lb,
        w_pa=w_pa[l].astype(BF16), hg_norm_g=row(hg_norm_g[l]),
        w_pb=w_pb[l].astype(BF16), w_o=w_o[l].astype(BF16),
        ln1_g=row(ln1_g[l]), ln1_b=row(ln1_b[l]),
        w_router=jnp.pad(w_router[l], ((0, 0), (0, pad))).astype(BF16),
        router_bias=jnp.pad(router_bias[l].astype(F32), (0, pad)).reshape(1, LANES),
        w_e_gate=w_e_gate[l].astype(BF16), w_e_up=w_e_up[l].astype(BF16),
        w_e_down=w_e_down[l].astype(BF16),
        w_sh_gate=w_sh_gate[l].astype(BF16), w_sh_up=w_sh_up[l].astype(BF16),
        w_sh_down=w_sh_down[l].astype(BF16),
        ln2_g=row(ln2_g[l]), ln2_b=row(ln2_b[l]),
    )


def kernel(x_prompt, x_sample, ln_in_g, ln_in_b, w_in, a_ln_g, a_ln_b, a_ws, a_sb, hg_lb_logits,
           hg_norm_g, w_pa, w_pb, w_o, ln1_g, ln1_b, w_router, router_bias, w_e_gate, w_e_up,
           w_e_down, w_sh_gate, w_sh_up, w_sh_down, ln2_g, ln2_b):
    p = _prepare_params(ln_in_g, ln_in_b, w_in, a_ln_g, a_ln_b, a_ws, a_sb, hg_lb_logits, hg_norm_g,
                        w_pa, w_pb, w_o, ln1_g, ln1_b, w_router, router_bias, w_e_gate, w_e_up,
                        w_e_down, w_sh_gate, w_sh_up, w_sh_down, ln2_g, ln2_b)
    return _encode(x_prompt, p), _encode(x_sample, p)
```

```python
import functools

import jax
import jax.numpy as jnp
from jax import lax
from jax.experimental import pallas as pl
from jax.experimental.pallas import tpu as pltpu

F32 = jnp.float32
BF16 = jnp.bfloat16

D_MODEL = 1024
A_GROUPS = 8
A_WIDTH = 512
A_CHUNK = 128
HG_HEADS = 8
HG_DK = 128
HG_WIDTH = HG_HEADS * HG_DK
N_IN = 2 * A_WIDTH + 5 * HG_WIDTH + 2 * D_MODEL
N_EXPERTS = 64
TOP_K = 6
N_GROUPS = 8
TOPK_GROUPS = 4
GROUP_SIZE = N_EXPERTS // N_GROUPS
EXPERT_DIM = 256
SHARED_DIM = 256
ROUTED_SCALE = 2.5
DEPTH = 1
ALPHA = (2.0 * DEPTH) ** 0.25
LN_EPS = 1e-5
RMS_EPS = 1e-6

LANES = 128
TOKEN_TILE = 256
HG_CHUNK = 128
EXPERT_BLOCK = 256
VMEM_LIMIT = 56 * 1024 * 1024

_O_U, _O_V, _O_Q, _O_FF, _O_FB, _O_I, _O_G, _O_GA, _O_GB = (
    0, 512, 1024, 2048, 3072, 4096, 5120, 6144, 7168)


def _layer_norm(x, g, b):
    mu = jnp.mean(x, axis=-1, keepdims=True)
    xc = x - mu
    var = jnp.mean(xc * xc, axis=-1, keepdims=True)
    return xc * lax.rsqrt(var + LN_EPS) * g + b


def _bdot(a, b):
    return jnp.dot(a.astype(BF16), b.astype(BF16), preferred_element_type=F32)


def _pack_rows(x):
    w = x.shape[1] // 2
    lo = lax.bitcast_convert_type(x[:, :w].astype(BF16).astype(F32), jnp.uint32)
    hi = lax.bitcast_convert_type(x[:, w:].astype(BF16).astype(F32), jnp.uint32)
    return hi | (lo >> 16)


def _unpack_rows(p):
    lo = lax.bitcast_convert_type(p << 16, F32)
    hi = lax.bitcast_convert_type(p & jnp.uint32(0xFFFF0000), F32)
    return lo, hi


def _const_spec(shape):
    nd = len(shape)
    return pl.BlockSpec(shape, lambda *_: (0,) * nd, pipeline_mode=pl.Buffered(1))


def _inproj_kernel(x_ref, lng_ref, lnb_ref, win_ref, alng_ref, alnb_ref, wsp_ref, sbf_ref, lb_ref,
                   wpa_ref, ag_ref, q_ref, gf_ref, gb_ref, iv_ref, so_ref, sgb_ref, gmin_ref):
    tm = x_ref.shape[0]
    xb = _layer_norm(x_ref[...], lng_ref[...], lnb_ref[...]).astype(BF16)

    def sec(lo, width):
        return jnp.dot(xb, win_ref[:, lo:lo + width], preferred_element_type=F32)

    u = jax.nn.gelu(sec(_O_U, A_WIDTH))
    v = _layer_norm(jax.nn.gelu(sec(_O_V, A_WIDTH)), alng_ref[...], alnb_ref[...]).astype(BF16)
    lane = lax.broadcasted_iota(jnp.int32, (A_CHUNK, LANES), 1)
    left = lane < (A_WIDTH // A_GROUPS)
    zero = jnp.zeros((A_CHUNK, LANES), BF16)
    chunks = []
    for c in range(tm // A_CHUNK):
        vc = v[c * A_CHUNK:(c + 1) * A_CHUNK]
        cols = []
        for p in range(A_GROUPS // 2):
            vp = vc[:, p * LANES:(p + 1) * LANES]
            rhs = jnp.concatenate([jnp.where(left, vp, zero), jnp.where(left, zero, vp)], axis=0)
            cols.append(jnp.dot(wsp_ref[p], rhs, preferred_element_type=F32))
        chunks.append(jnp.concatenate(cols, axis=1) + sbf_ref[...])
    mixed = jnp.concatenate(chunks, axis=0)
    a = _bdot(u * mixed, wpa_ref[...])
    ag_ref[...] = (jax.nn.sigmoid(sec(_O_GA, D_MODEL)) * a).astype(BF16)

    q_ref[...] = jax.nn.silu(sec(_O_Q, HG_WIDTH)).astype(BF16)
    mins = []
    for d, (off, g_ref) in enumerate(((_O_FF, gf_ref), (_O_FB, gb_ref))):
        lb = lb_ref[d:d + 1, :]
        f = lb + (1.0 - lb) * jax.nn.sigmoid(sec(off, HG_WIDTH))
        g = jnp.log(f)
        g_ref[...] = g
        half = jnp.sum(g.reshape(tm // (HG_CHUNK // 2), HG_CHUNK // 2, HG_WIDTH), axis=1)
        mins.append(jnp.min(half, axis=-1, keepdims=True))
    gmin_ref[0] = jnp.broadcast_to(jnp.concatenate(mins, axis=0), gmin_ref.shape[1:])
    iv_ref[...] = sec(_O_I, HG_WIDTH).astype(BF16)
    so_ref[...] = jax.nn.silu(sec(_O_G, HG_WIDTH)).astype(BF16)
    sgb_ref[...] = jax.nn.sigmoid(sec(_O_GB, D_MODEL)).astype(BF16)


def _inproj(x, p):
    T = x.shape[0]
    tm = TOKEN_TILE
    nt = T // tm
    nh = 2 * tm // (HG_CHUNK // 2)
    tok = lambda w: pl.BlockSpec((tm, w), lambda i: (i, 0))
    outs = (
        jax.ShapeDtypeStruct((T, D_MODEL), BF16),
        jax.ShapeDtypeStruct((T, HG_WIDTH), BF16),
        jax.ShapeDtypeStruct((T, HG_WIDTH), F32),
        jax.ShapeDtypeStruct((T, HG_WIDTH), F32),
        jax.ShapeDtypeStruct((T, HG_WIDTH), BF16),
        jax.ShapeDtypeStruct((T, HG_WIDTH), BF16),
        jax.ShapeDtypeStruct((T, D_MODEL), BF16),
        jax.ShapeDtypeStruct((nt, nh, LANES), F32),
    )
    return pl.pallas_call(
        _inproj_kernel,
        grid=(nt,),
        in_specs=[
            tok(D_MODEL),
            _const_spec((1, D_MODEL)), _const_spec((1, D_MODEL)),
            _const_spec((D_MODEL, N_IN)),
            _const_spec((1, A_WIDTH)), _const_spec((1, A_WIDTH)),
            _const_spec((A_GROUPS // 2, A_CHUNK, 2 * A_CHUNK)),
            _const_spec((A_CHUNK, A_WIDTH)),
            _const_spec((2, HG_WIDTH)),
            _const_spec((A_WIDTH, D_MODEL)),
        ],
        out_specs=(tok(D_MODEL), tok(HG_WIDTH), tok(HG_WIDTH), tok(HG_WIDTH), tok(HG_WIDTH),
                   tok(HG_WIDTH), tok(D_MODEL), pl.BlockSpec((1, nh, LANES), lambda i: (i, 0, 0))),
        out_shape=outs,
        compiler_params=pltpu.CompilerParams(
            dimension_semantics=("parallel",), vmem_limit_bytes=VMEM_LIMIT),
        name="inproj",
    )(x, p["ln_in_g"], p["ln_in_b"], p["w_in"], p["a_ln_g"], p["a_ln_b"], p["wsp"], p["sbf"],
      p["lb"], p["w_pa"])


_NT = (((1,), (1,)), ((), ()))
_TN = (((0,), (0,)), ((), ()))


def _hgrn_direction(q_ref, g_ref, v_ref, st_ref, o_ref, tri, mask, fwd):
    C = q_ref.shape[0]
    g = g_ref[...]
    ghi = g.astype(BF16)
    glo = (g - ghi.astype(F32)).astype(BF16)
    b = (jnp.dot(tri, ghi, preferred_element_type=F32)
         + jnp.dot(tri, glo, preferred_element_type=F32))
    mid = C // 2 - 1 if fwd else C // 2
    end = C - 1 if fwd else 0
    r = b[mid:mid + 1, :]
    b_end = b[end:end + 1, :]
    qt = q_ref[...].astype(F32) * jnp.exp(b - r)
    kt = (1.0 - jnp.exp(g)) * jnp.exp(r - b)
    qtb = qt.astype(BF16)
    ktb = kt.astype(BF16)
    qhb = (qt * jnp.exp(r)).astype(BF16)
    khb = (kt * jnp.exp(b_end - r)).astype(BF16)
    decay = jnp.exp(b_end)
    v = v_ref[...]
    heads = [slice(h * HG_DK, (h + 1) * HG_DK) for h in range(HG_HEADS)]
    scores = [jnp.where(mask, lax.dot_general(qtb[:, sl], ktb[:, sl], _NT,
                                              preferred_element_type=F32), 0.0).astype(BF16)
              for sl in heads]
    for h, sl in enumerate(heads):
        st = st_ref[h]
        o_ref[:, sl] = (jnp.dot(scores[h], v[:, sl], preferred_element_type=F32)
                        + lax.dot_general(qhb[:, sl], st.astype(BF16), _NT,
                                          preferred_element_type=F32))
        st_ref[h] = st * decay[:, sl] + lax.dot_general(v[:, sl], khb[:, sl], _TN,
                                                        preferred_element_type=F32)


def _hgrn_kernel(qf_ref, qb_ref, gf_ref, gb_ref, vf_ref, vb_ref, tril_ref, triu_ref,
                 of_ref, ob_ref, sf_ref, sb_ref):
    @pl.when(pl.program_id(1) == 0)
    def _():
        sf_ref[...] = jnp.zeros_like(sf_ref)
        sb_ref[...] = jnp.zeros_like(sb_ref)

    C = qf_ref.shape[0]
    row = lax.broadcasted_iota(jnp.int32, (C, C), 0)
    col = lax.broadcasted_iota(jnp.int32, (C, C), 1)
    _hgrn_direction(qf_ref, gf_ref, vf_ref, sf_ref, of_ref, tril_ref[...], row >= col, True)
    _hgrn_direction(qb_ref, gb_ref, vb_ref, sb_ref, ob_ref, triu_ref[...], row <= col, False)


def _hgrn(q, gf, gb, iv, batch):
    T = q.shape[0]
    C = HG_CHUNK
    nc = T // batch // C
    fwd = pl.BlockSpec((C, HG_WIDTH), lambda b, j: (b * nc + j, 0))
    bwd = pl.BlockSpec((C, HG_WIDTH), lambda b, j: (b * nc + nc - 1 - j, 0))
    row = lax.broadcasted_iota(jnp.int32, (C, C), 0)
    col = lax.broadcasted_iota(jnp.int32, (C, C), 1)
    tril = (row >= col).astype(BF16)
    triu = (row <= col).astype(BF16)
    return pl.pallas_call(
        _hgrn_kernel,
        grid=(batch, nc),
        in_specs=[fwd, bwd, fwd, bwd, fwd, bwd, _const_spec((C, C)), _const_spec((C, C))],
        out_specs=(fwd, bwd),
        out_shape=(jax.ShapeDtypeStruct((T, HG_WIDTH), F32), jax.ShapeDtypeStruct((T, HG_WIDTH), F32)),
        scratch_shapes=[pltpu.VMEM((HG_HEADS, HG_DK, HG_DK), F32),
                        pltpu.VMEM((HG_HEADS, HG_DK, HG_DK), F32)],
        compiler_params=pltpu.CompilerParams(
            dimension_semantics=("parallel", "arbitrary"), vmem_limit_bytes=VMEM_LIMIT),
        name="hgrn",
    )(q, q, gf, gb, iv, iv, tril, triu)


def _xor_partner(x, lane, s):
    return jnp.where((lane & s) != 0, pltpu.roll(x, s, 1), pltpu.roll(x, LANES - s, 1))


def _mix_kernel(x_ref, lng_ref, lnb_ref, of_ref, ob_ref, so_ref, sgb_ref, ag_ref, ng_ref, wpb_ref,
                wo_ref, l1g_ref, l1b_ref, wr_ref, rb_ref, tri_ref, x1_ref, x1p_ref, route_ref,
                cnt_ref, carry_ref):
    tm = x_ref.shape[0]

    @pl.when(pl.program_id(0) == 0)
    def _():
        carry_ref[...] = jnp.zeros_like(carry_ref)

    o = of_ref[...] + ob_ref[...]
    heads = []
    for h in range(HG_HEADS):
        oh = o[:, h * HG_DK:(h + 1) * HG_DK]
        heads.append(oh * lax.rsqrt(jnp.mean(oh * oh, axis=-1, keepdims=True) + RMS_EPS))
    rn = jnp.concatenate(heads, axis=1) * ng_ref[...] * so_ref[...].astype(F32)
    r = _bdot(rn, wpb_ref[...])
    mixed = ag_ref[...].astype(F32) + sgb_ref[...].astype(F32) * r
    y = _bdot(mixed, wo_ref[...])
    xn = _layer_norm(x_ref[...], lng_ref[...], lnb_ref[...])
    x1 = _layer_norm(ALPHA * xn + y, l1g_ref[...], l1b_ref[...])
    x1_ref[...] = x1
    x1p_ref[...] = _pack_rows(x1)

    scores = jax.nn.sigmoid(_bdot(x1, wr_ref[...]))
    lane = lax.broadcasted_iota(jnp.int32, (tm, LANES), 1)
    valid = lane < N_EXPERTS
    neg = jnp.float32(-jnp.inf)
    biased = jnp.where(valid, scores + rb_ref[...], neg)
    m1, m2 = biased, jnp.full_like(biased, neg)
    s = 1
    while s < GROUP_SIZE:
        p1, p2 = _xor_partner(m1, lane, s), _xor_partner(m2, lane, s)
        m1, m2 = jnp.maximum(m1, p1), jnp.maximum(jnp.minimum(m1, p1), jnp.maximum(m2, p2))
        s *= 2
    gs = m1 + m2
    ahead = jnp.zeros((tm, LANES), F32)
    for d in range(1, N_GROUPS):
        lower = pltpu.roll(gs, d * GROUP_SIZE, 1)
        higher = pltpu.roll(gs, LANES - d * GROUP_SIZE, 1)
        ahead = ahead + jnp.where(lower >= gs, 1.0, 0.0) + jnp.where(higher > gs, 1.0, 0.0)
    allowed = jnp.where(valid & (ahead < TOPK_GROUPS), biased, neg)
    sel = jnp.zeros((tm, LANES), F32)
    lane_f = lane.astype(F32)
    picks = []
    for _ in range(TOP_K):
        m = jnp.max(allowed, axis=-1, keepdims=True)
        first = jnp.min(jnp.where(allowed == m, lane_f, float(LANES)), axis=-1, keepdims=True)
        hit = lane_f == first
        picks.append((first, hit, jnp.sum(jnp.where(hit, scores, 0.0), axis=-1, keepdims=True)))
        sel = jnp.where(hit, 1.0, sel)
        allowed = jnp.where(hit, neg, allowed)
    wsum = picks[0][2]
    for pk in picks[1:]:
        wsum = wsum + pk[2]
    before = jnp.dot(tri_ref[...], sel.astype(BF16), preferred_element_type=F32) + carry_ref[0:1, :]
    total = carry_ref[0:1, :] + jnp.sum(sel, axis=0, keepdims=True)
    carry_ref[...] = jnp.broadcast_to(total, carry_ref.shape)
    cnt_ref[...] = jnp.broadcast_to(total, cnt_ref.shape)
    route = jnp.zeros((tm, LANES), F32)
    for k, (first, hit, w) in enumerate(picks):
        rank = jnp.sum(jnp.where(hit, before, 0.0), axis=-1, keepdims=True)
        route = jnp.where(lane == k, first, route)
        route = jnp.where(lane == 8 + k, rank, route)
        route = jnp.where(lane == 16 + k, w / wsum * ROUTED_SCALE, route)
    route_ref[...] = route


def _mix(x, of, ob, so, sgb, ag, p):
    T = x.shape[0]
    tm = TOKEN_TILE
    tok = lambda w: pl.BlockSpec((tm, w), lambda i: (i, 0))
    row = lax.broadcasted_iota(jnp.int32, (tm, tm), 0)
    col = lax.broadcasted_iota(jnp.int32, (tm, tm), 1)
    tri = (row > col).astype(BF16)
    return pl.pallas_call(
        _mix_kernel,
        grid=(T // tm,),
        in_specs=[
            tok(D_MODEL), _const_spec((1, D_MODEL)), _const_spec((1, D_MODEL)),
            tok(HG_WIDTH), tok(HG_WIDTH), tok(HG_WIDTH), tok(D_MODEL), tok(D_MODEL),
            _const_spec((1, HG_WIDTH)),
            _const_spec((HG_WIDTH, D_MODEL)), _const_spec((D_MODEL, D_MODEL)),
            _const_spec((1, D_MODEL)), _const_spec((1, D_MODEL)),
            _const_spec((D_MODEL, LANES)), _const_spec((1, LANES)),
            _const_spec((tm, tm)),
        ],
        out_specs=(tok(D_MODEL), tok(D_MODEL // 2), tok(LANES),
                   pl.BlockSpec((8, LANES), lambda i: (0, 0))),
        out_shape=(jax.ShapeDtypeStruct((T, D_MODEL), F32),
                   jax.ShapeDtypeStruct((T, D_MODEL // 2), jnp.uint32),
                   jax.ShapeDtypeStruct((T, LANES), F32),
                   jax.ShapeDtypeStruct((8, LANES), F32)),
        scratch_shapes=[pltpu.VMEM((8, LANES), F32)],
        compiler_params=pltpu.CompilerParams(
            dimension_semantics=("arbitrary",), vmem_limit_bytes=VMEM_LIMIT),
        name="mix",
    )(x, p["ln_in_g"], p["ln_in_b"], of, ob, so, sgb, ag, p["hg_norm_g"], p["w_pb"], p["w_o"],
      p["ln1_g"], p["ln1_b"], p["w_router"], p["router_bias"], tri)


def _dispatch_kernel(dest_ref, x1p_ref, xs_ref, sem):
    tm = x1p_ref.shape[0]

    def row_copy(t, k):
        return pltpu.make_async_copy(
            x1p_ref.at[pl.ds(t, 1)], xs_ref.at[pl.ds(dest_ref[0, 0, k * tm + t], 1)], sem)

    def start(t, c):
        for k in range(TOP_K):
            row_copy(t, k).start()
        return c

    def wait(t, c):
        for k in range(TOP_K):
            row_copy(t, k).wait()
        return c

    lax.fori_loop(0, tm, start, 0)
    lax.fori_loop(0, tm, wait, 0)


def _dispatch(x1p, dest3, n_rows):
    T, w = x1p.shape
    tm = TOKEN_TILE
    return pl.pallas_call(
        _dispatch_kernel,
        grid=(T // tm,),
        in_specs=[
            pl.BlockSpec((1, 1, TOP_K * tm), lambda i: (i, 0, 0), memory_space=pltpu.SMEM),
            pl.BlockSpec((tm, w), lambda i: (i, 0)),
        ],
        out_specs=pl.BlockSpec(memory_space=pl.ANY),
        out_shape=jax.ShapeDtypeStruct((n_rows, w), jnp.uint32),
        scratch_shapes=[pltpu.SemaphoreType.DMA],
        compiler_params=pltpu.CompilerParams(
            dimension_semantics=("arbitrary",), vmem_limit_bytes=VMEM_LIMIT),
        name="dispatch",
    )(dest3, x1p)


def _experts_kernel(be_ref, nv_ref, xs_ref, wg_ref, wu_ref, wd_ref, ys_ref):
    del be_ref
    n_valid = nv_ref[pl.program_id(0)]
    mb, w = xs_ref.shape

    @pl.when(n_valid > 0)
    def _():
        keep = lax.broadcasted_iota(jnp.int32, (mb, w), 0) < n_valid
        lo, hi = _unpack_rows(jnp.where(keep, xs_ref[...], jnp.uint32(0)))
        lo, hi = lo.astype(BF16), hi.astype(BF16)

        def proj(w_ref):
            return (jnp.dot(lo, w_ref[0, :w, :], preferred_element_type=F32)
                    + jnp.dot(hi, w_ref[0, w:, :], preferred_element_type=F32))

        hb = jax.nn.silu(proj(wg_ref)) * proj(wu_ref)
        ys_ref[...] = _pack_rows(jnp.dot(hb.astype(BF16), wd_ref[0], preferred_element_type=F32))

    @pl.when(n_valid <= 0)
    def _():
        ys_ref[...] = jnp.zeros_like(ys_ref)


def _experts(xs, blk_exp, n_valid, p):
    n_rows, w = xs.shape
    mb = EXPERT_BLOCK
    grid_spec = pltpu.PrefetchScalarGridSpec(
        num_scalar_prefetch=2,
        grid=(n_rows // mb,),
        in_specs=[
            pl.BlockSpec((mb, w), lambda i, be, nv: (i, 0)),
            pl.BlockSpec((1, D_MODEL, EXPERT_DIM), lambda i, be, nv: (be[i], 0, 0)),
            pl.BlockSpec((1, D_MODEL, EXPERT_DIM), lambda i, be, nv: (be[i], 0, 0)),
            pl.BlockSpec((1, EXPERT_DIM, D_MODEL), lambda i, be, nv: (be[i], 0, 0)),
        ],
        out_specs=pl.BlockSpec((mb, w), lambda i, be, nv: (i, 0)),
    )
    return pl.pallas_call(
        _experts_kernel,
        grid_spec=grid_spec,
        out_shape=jax.ShapeDtypeStruct((n_rows, w), jnp.uint32),
        compiler_params=pltpu.CompilerParams(
            dimension_semantics=("arbitrary",), vmem_limit_bytes=VMEM_LIMIT),
        name="experts",
    )(blk_exp, n_valid, xs, p["w_e_gate"], p["w_e_up"], p["w_e_down"])


def _final_kernel(dest_ref, x1_ref, route_ref, ys_ref, wsg_ref, wsu_ref, wsd_ref, l2g_ref, l2b_ref,
                  out_ref, yg_ref, sem):
    tm = x1_ref.shape[0]

    def row_copy(t, k):
        return pltpu.make_async_copy(
            ys_ref.at[pl.ds(dest_ref[0, 0, k * tm + t], 1)], yg_ref.at[k, pl.ds(t, 1)], sem)

    def start(t, c):
        for k in range(TOP_K):
            row_copy(t, k).start()
        return c

    def wait(t, c):
        for k in range(TOP_K):
            row_copy(t, k).wait()
        return c

    lax.fori_loop(0, tm, start, 0)
    x1 = x1_ref[...]
    xb = x1.astype(BF16)
    hs = (jax.nn.silu(jnp.dot(xb, wsg_ref[...], preferred_element_type=F32))
          * jnp.dot(xb, wsu_ref[...], preferred_element_type=F32))
    shared = jnp.dot(hs.astype(BF16), wsd_ref[...], preferred_element_type=F32)
    lax.fori_loop(0, tm, wait, 0)
    route = route_ref[...]
    lo, hi = None, None
    for k in range(TOP_K):
        wk = route[:, 16 + k:17 + k]
        lk, hk = _unpack_rows(yg_ref[k])
        lo = lk * wk if lo is None else lo + lk * wk
        hi = hk * wk if hi is None else hi + hk * wk
    routed = jnp.concatenate([lo, hi], axis=1)
    out_ref[...] = _layer_norm(ALPHA * x1 + (routed + shared), l2g_ref[...], l2b_ref[...])


def _final(x1, route, dest3, ys, p):
    T = x1.shape[0]
    tm = TOKEN_TILE
    tok = lambda w: pl.BlockSpec((tm, w), lambda i: (i, 0))
    return pl.pallas_call(
        _final_kernel,
        grid=(T // tm,),
        in_specs=[
            pl.BlockSpec((1, 1, TOP_K * tm), lambda i: (i, 0, 0), memory_space=pltpu.SMEM),
            tok(D_MODEL), tok(LANES),
            pl.BlockSpec(memory_space=pl.ANY),
            _const_spec((D_MODEL, SHARED_DIM)), _const_spec((D_MODEL, SHARED_DIM)),
            _const_spec((SHARED_DIM, D_MODEL)),
            _const_spec((1, D_MODEL)), _const_spec((1, D_MODEL)),
        ],
        out_specs=tok(D_MODEL),
        out_shape=jax.ShapeDtypeStruct((T, D_MODEL), F32),
        scratch_shapes=[pltpu.VMEM((TOP_K, tm, ys.shape[1]), jnp.uint32), pltpu.SemaphoreType.DMA],
        compiler_params=pltpu.CompilerParams(
            dimension_semantics=("arbitrary",), vmem_limit_bytes=VMEM_LIMIT),
        name="final",
    )(dest3, x1, route, ys, p["w_sh_gate"], p["w_sh_up"], p["w_sh_down"], p["ln2_g"], p["ln2_b"])


def _routing_layout(route, counts_row, n_tokens):
    tm, mb = TOKEN_TILE, EXPERT_BLOCK
    n_blocks = -(-n_tokens * TOP_K // mb) + N_EXPERTS
    idx = route[:, 0:TOP_K].astype(jnp.int32)
    rank = route[:, 8:8 + TOP_K].astype(jnp.int32)
    counts = counts_row[:N_EXPERTS].astype(jnp.int32)
    padded = (counts + mb - 1) // mb * mb
    pad_end = jnp.cumsum(padded)
    pad_start = pad_end - padded
    experts = jnp.arange(N_EXPERTS, dtype=jnp.int32)
    dest = rank + jnp.sum(jnp.where(idx[:, :, None] == experts, pad_start, 0), axis=-1)
    dest3 = dest.reshape(n_tokens // tm, tm, TOP_K).transpose(0, 2, 1).reshape(
        n_tokens // tm, 1, TOP_K * tm)
    blk_start = jnp.arange(n_blocks, dtype=jnp.int32) * mb
    blk_exp = jnp.minimum(
        jnp.sum((pad_end[None, :] <= blk_start[:, None]).astype(jnp.int32), axis=1), N_EXPERTS - 1)
    valid_end = jnp.sum(jnp.where(blk_exp[:, None] == experts, pad_start + counts, 0), axis=-1)
    n_valid = jnp.clip(valid_end - blk_start, 0, mb).astype(jnp.int32)
    return dest3, blk_exp, n_valid, n_blocks * mb


def _encode(x, p):
    batch, seq, _ = x.shape
    T = batch * seq
    xt = x.reshape(T, D_MODEL)
    ag, q, gf, gb, iv, so, sgb, _ = _inproj(xt, p)
    of, ob = _hgrn(q, gf, gb, iv, batch)
    x1, x1p, route, cnt = _mix(xt, of, ob, so, sgb, ag, p)
    dest3, blk_exp, n_valid, n_rows = _routing_layout(route, cnt[0], T)
    xs = _dispatch(x1p, dest3, n_rows)
    ys = _experts(xs, blk_exp, n_valid, p)
    out = _final(x1, route, dest3, ys, p)
    return out.reshape(batch, seq, D_MODEL)


def _prepare_params(ln_in_g, ln_in_b, w_in, a_ln_g, a_ln_b, a_ws, a_sb, hg_lb_logits, hg_norm_g,
                    w_pa, w_pb, w_o, ln1_g, ln1_b, w_router, router_bias, w_e_gate, w_e_up,
                    w_e_down, w_sh_gate, w_sh_up, w_sh_down, ln2_g, ln2_b):
    l = 0
    row = lambda v: v.reshape(1, -1).astype(F32)
    ws = a_ws[l].astype(BF16)
    wsp = jnp.concatenate([ws[0::2], ws[1::2]], axis=2)
    sbf = jnp.repeat(a_sb[l].astype(F32), A_WIDTH // A_GROUPS, axis=1)
    lb = jnp.cumsum(jax.nn.softmax(hg_lb_logits.astype(F32), axis=1), axis=1)[:, l]
    pad = LANES - N_EXPERTS
    return dict(
        ln_in_g=row(ln_in_g), ln_in_b=row(ln_in_b), w_in=w_in[l].astype(BF16),
        a_ln_g=row(a_ln_g[l]), a_ln_b=row(a_ln_b[l]), wsp=wsp, sbf=sbf, lb=lb,
        w_pa=w_pa[l].astype(BF16), hg_norm_g=row(hg_norm_g[l]),
        w_pb=w_pb[l].astype(BF16), w_o=w_o[l].astype(BF16),
        ln1_g=row(ln1_g[l]), ln1_b=row(ln1_b[l]),
        w_router=jnp.pad(w_router[l], ((0, 0), (0, pad))).astype(BF16),
        router_bias=jnp.pad(router_bias[l].astype(F32), (0, pad)).reshape(1, LANES),
        w_e_gate=w_e_gate[l].astype(BF16), w_e_up=w_e_up[l].astype(BF16),
        w_e_down=w_e_down[l].astype(BF16),
        w_sh_gate=w_sh_gate[l].astype(BF16), w_sh_up=w_sh_up[l].astype(BF16),
        w_sh_down=w_sh_down[l].astype(BF16),
        ln2_g=row(ln2_g[l]), ln2_b=row(ln2_b[l]),
    )


def kernel(x_prompt, x_sample, ln_in_g, ln_in_b, w_in, a_ln_g, a_ln_b, a_ws, a_sb, hg_lb_logits,
           hg_norm_g, w_pa, w_pb, w_o, ln1_g, ln1_b, w_router, router_bias, w_e_gate, w_e_up,
           w_e_down, w_sh_gate, w_sh_up, w_sh_down, ln2_g, ln2_b):
    p = _prepare_params(ln_in_g, ln_in_b, w_in, a_ln_g, a_ln_b, a_ws, a_sb, hg_lb_logits, hg_norm_g,
                        w_pa, w_pb, w_o, ln1_g, ln1_b, w_router, router_bias, w_e_gate, w_e_up,
                        w_e_down, w_sh_gate, w_sh_up, w_sh_down, ln2_g, ln2_b)
    return _encode(x_prompt, p), _encode(x_sample, p)
```

```python
import functools

import jax
import jax.numpy as jnp
from jax import lax
from jax.experimental import pallas as pl
from jax.experimental.pallas import tpu as pltpu
from jax.experimental.pallas import tpu_sc as plsc

F32 = jnp.float32
BF16 = jnp.bfloat16

D_MODEL = 1024
A_GROUPS = 8
A_WIDTH = 512
A_CHUNK = 128
HG_HEADS = 8
HG_DK = 128
HG_WIDTH = HG_HEADS * HG_DK
N_IN = 2 * A_WIDTH + 5 * HG_WIDTH + 2 * D_MODEL
N_EXPERTS = 64
TOP_K = 6
N_GROUPS = 8
TOPK_GROUPS = 4
GROUP_SIZE = N_EXPERTS // N_GROUPS
EXPERT_DIM = 256
SHARED_DIM = 256
ROUTED_SCALE = 2.5
DEPTH = 1
ALPHA = (2.0 * DEPTH) ** 0.25
LN_EPS = 1e-5
RMS_EPS = 1e-6

LANES = 128
TOKEN_TILE = 256
HG_CHUNK = 128
EXPERT_BLOCK = 256
SC_CORES = 2
SC_SUBCORES = 16
SC_WINDOW = 32
VMEM_LIMIT = 56 * 1024 * 1024

_O_U, _O_V, _O_Q, _O_FF, _O_FB, _O_I, _O_G, _O_GA, _O_GB = (
    0, 512, 1024, 2048, 3072, 4096, 5120, 6144, 7168)


def _layer_norm(x, g, b):
    mu = jnp.mean(x, axis=-1, keepdims=True)
    xc = x - mu
    var = jnp.mean(xc * xc, axis=-1, keepdims=True)
    return xc * lax.rsqrt(var + LN_EPS) * g + b


def _bdot(a, b):
    return jnp.dot(a.astype(BF16), b.astype(BF16), preferred_element_type=F32)


def _pack_rows(x):
    w = x.shape[1] // 2
    lo = lax.bitcast_convert_type(x[:, :w].astype(BF16).astype(F32), jnp.uint32)
    hi = lax.bitcast_convert_type(x[:, w:].astype(BF16).astype(F32), jnp.uint32)
    return hi | (lo >> 16)


def _unpack_rows(p):
    lo = lax.bitcast_convert_type(p << 16, F32)
    hi = lax.bitcast_convert_type(p & jnp.uint32(0xFFFF0000), F32)
    return lo, hi


def _const_spec(shape):
    nd = len(shape)
    return pl.BlockSpec(shape, lambda *_: (0,) * nd, pipeline_mode=pl.Buffered(1))


def _inproj_kernel(x_ref, lng_ref, lnb_ref, win_ref, alng_ref, alnb_ref, wsp_ref, sbf_ref, lb_ref,
                   wpa_ref, ag_ref, q_ref, gf_ref, gb_ref, iv_ref, so_ref, sgb_ref, gmin_ref):
    tm = x_ref.shape[0]
    xb = _layer_norm(x_ref[...], lng_ref[...], lnb_ref[...]).astype(BF16)

    def sec(lo, width):
        return jnp.dot(xb, win_ref[:, lo:lo + width], preferred_element_type=F32)

    u = jax.nn.gelu(sec(_O_U, A_WIDTH))
    v = _layer_norm(jax.nn.gelu(sec(_O_V, A_WIDTH)), alng_ref[...], alnb_ref[...]).astype(BF16)
    lane = lax.broadcasted_iota(jnp.int32, (A_CHUNK, LANES), 1)
    left = lane < (A_WIDTH // A_GROUPS)
    zero = jnp.zeros((A_CHUNK, LANES), BF16)
    chunks = []
    for c in range(tm // A_CHUNK):
        vc = v[c * A_CHUNK:(c + 1) * A_CHUNK]
        cols = []
        for p in range(A_GROUPS // 2):
            vp = vc[:, p * LANES:(p + 1) * LANES]
            rhs = jnp.concatenate([jnp.where(left, vp, zero), jnp.where(left, zero, vp)], axis=0)
            cols.append(jnp.dot(wsp_ref[p], rhs, preferred_element_type=F32))
        chunks.append(jnp.concatenate(cols, axis=1) + sbf_ref[...])
    mixed = jnp.concatenate(chunks, axis=0)
    a = _bdot(u * mixed, wpa_ref[...])
    ag_ref[...] = (jax.nn.sigmoid(sec(_O_GA, D_MODEL)) * a).astype(BF16)

    q_ref[...] = jax.nn.silu(sec(_O_Q, HG_WIDTH)).astype(BF16)
    mins = []
    for d, (off, g_ref) in enumerate(((_O_FF, gf_ref), (_O_FB, gb_ref))):
        lb = lb_ref[d:d + 1, :]
        f = lb + (1.0 - lb) * jax.nn.sigmoid(sec(off, HG_WIDTH))
        g = jnp.log(f)
        g_ref[...] = g
        half = jnp.sum(g.reshape(tm // (HG_CHUNK // 2), HG_CHUNK // 2, HG_WIDTH), axis=1)
        mins.append(jnp.min(half, axis=-1, keepdims=True))
    gmin_ref[0] = jnp.broadcast_to(jnp.concatenate(mins, axis=0), gmin_ref.shape[1:])
    iv_ref[...] = sec(_O_I, HG_WIDTH).astype(BF16)
    so_ref[...] = jax.nn.silu(sec(_O_G, HG_WIDTH)).astype(BF16)
    sgb_ref[...] = jax.nn.sigmoid(sec(_O_GB, D_MODEL)).astype(BF16)


def _inproj(x, p):
    T = x.shape[0]
    tm = TOKEN_TILE
    nt = T // tm
    nh = 2 * tm // (HG_CHUNK // 2)
    tok = lambda w: pl.BlockSpec((tm, w), lambda i: (i, 0))
    outs = (
        jax.ShapeDtypeStruct((T, D_MODEL), BF16),
        jax.ShapeDtypeStruct((T, HG_WIDTH), BF16),
        jax.ShapeDtypeStruct((T, HG_WIDTH), F32),
        jax.ShapeDtypeStruct((T, HG_WIDTH), F32),
        jax.ShapeDtypeStruct((T, HG_WIDTH), BF16),
        jax.ShapeDtypeStruct((T, HG_WIDTH), BF16),
        jax.ShapeDtypeStruct((T, D_MODEL), BF16),
        jax.ShapeDtypeStruct((nt, nh, LANES), F32),
    )
    return pl.pallas_call(
        _inproj_kernel,
        grid=(nt,),
        in_specs=[
            tok(D_MODEL),
            _const_spec((1, D_MODEL)), _const_spec((1, D_MODEL)),
            _const_spec((D_MODEL, N_IN)),
            _const_spec((1, A_WIDTH)), _const_spec((1, A_WIDTH)),
            _const_spec((A_GROUPS // 2, A_CHUNK, 2 * A_CHUNK)),
            _const_spec((A_CHUNK, A_WIDTH)),
            _const_spec((2, HG_WIDTH)),
            _const_spec((A_WIDTH, D_MODEL)),
        ],
        out_specs=(tok(D_MODEL), tok(HG_WIDTH), tok(HG_WIDTH), tok(HG_WIDTH), tok(HG_WIDTH),
                   tok(HG_WIDTH), tok(D_MODEL), pl.BlockSpec((1, nh, LANES), lambda i: (i, 0, 0))),
        out_shape=outs,
        compiler_params=pltpu.CompilerParams(
            dimension_semantics=("parallel",), vmem_limit_bytes=VMEM_LIMIT),
        name="inproj",
    )(x, p["ln_in_g"], p["ln_in_b"], p["w_in"], p["a_ln_g"], p["a_ln_b"], p["wsp"], p["sbf"],
      p["lb"], p["w_pa"])


_NT = (((1,), (1,)), ((), ()))
_TN = (((0,), (0,)), ((), ()))


def _hgrn_direction(q_ref, g_ref, v_ref, st_ref, o_ref, tri, mask, fwd):
    C = q_ref.shape[0]
    g = g_ref[...]
    ghi = g.astype(BF16)
    glo = (g - ghi.astype(F32)).astype(BF16)
    b = (jnp.dot(tri, ghi, preferred_element_type=F32)
         + jnp.dot(tri, glo, preferred_element_type=F32))
    mid = C // 2 - 1 if fwd else C // 2
    end = C - 1 if fwd else 0
    r = b[mid:mid + 1, :]
    b_end = b[end:end + 1, :]
    qt = q_ref[...].astype(F32) * jnp.exp(b - r)
    kt = (1.0 - jnp.exp(g)) * jnp.exp(r - b)
    qtb = qt.astype(BF16)
    ktb = kt.astype(BF16)
    qhb = (qt * jnp.exp(r)).astype(BF16)
    khb = (kt * jnp.exp(b_end - r)).astype(BF16)
    decay = jnp.exp(b_end)
    v = v_ref[...]
    heads = [slice(h * HG_DK, (h + 1) * HG_DK) for h in range(HG_HEADS)]
    scores = [jnp.where(mask, lax.dot_general(qtb[:, sl], ktb[:, sl], _NT,
                                              preferred_element_type=F32), 0.0).astype(BF16)
              for sl in heads]
    for h, sl in enumerate(heads):
        st = st_ref[h]
        o_ref[:, sl] = (jnp.dot(scores[h], v[:, sl], preferred_element_type=F32)
                        + lax.dot_general(qhb[:, sl], st.astype(BF16), _NT,
                                          preferred_element_type=F32))
        st_ref[h] = st * decay[:, sl] + lax.dot_general(v[:, sl], khb[:, sl], _TN,
                                                        preferred_element_type=F32)


def _hgrn_kernel(qf_ref, qb_ref, gf_ref, gb_ref, vf_ref, vb_ref, tril_ref, triu_ref,
                 of_ref, ob_ref, sf_ref, sb_ref):
    @pl.when(pl.program_id(1) == 0)
    def _():
        sf_ref[...] = jnp.zeros_like(sf_ref)
        sb_ref[...] = jnp.zeros_like(sb_ref)

    C = qf_ref.shape[0]
    row = lax.broadcasted_iota(jnp.int32, (C, C), 0)
    col = lax.broadcasted_iota(jnp.int32, (C, C), 1)
    _hgrn_direction(qf_ref, gf_ref, vf_ref, sf_ref, of_ref, tril_ref[...], row >= col, True)
    _hgrn_direction(qb_ref, gb_ref, vb_ref, sb_ref, ob_ref, triu_ref[...], row <= col, False)


def _hgrn(q, gf, gb, iv, batch):
    T = q.shape[0]
    C = HG_CHUNK
    nc = T // batch // C
    fwd = pl.BlockSpec((C, HG_WIDTH), lambda b, j: (b * nc + j, 0))
    bwd = pl.BlockSpec((C, HG_WIDTH), lambda b, j: (b * nc + nc - 1 - j, 0))
    row = lax.broadcasted_iota(jnp.int32, (C, C), 0)
    col = lax.broadcasted_iota(jnp.int32, (C, C), 1)
    tril = (row >= col).astype(BF16)
    triu = (row <= col).astype(BF16)
    return pl.pallas_call(
        _hgrn_kernel,
        grid=(batch, nc),
        in_specs=[fwd, bwd, fwd, bwd, fwd, bwd, _const_spec((C, C)), _const_spec((C, C))],
        out_specs=(fwd, bwd),
        out_shape=(jax.ShapeDtypeStruct((T, HG_WIDTH), F32), jax.ShapeDtypeStruct((T, HG_WIDTH), F32)),
        scratch_shapes=[pltpu.VMEM((HG_HEADS, HG_DK, HG_DK), F32),
                        pltpu.VMEM((HG_HEADS, HG_DK, HG_DK), F32)],
        compiler_params=pltpu.CompilerParams(
            dimension_semantics=("parallel", "arbitrary"), vmem_limit_bytes=VMEM_LIMIT),
        name="hgrn",
    )(q, q, gf, gb, iv, iv, tril, triu)


def _xor_partner(x, lane, s):
    return jnp.where((lane & s) != 0, pltpu.roll(x, s, 1), pltpu.roll(x, LANES - s, 1))


def _mix_kernel(x_ref, lng_ref, lnb_ref, of_ref, ob_ref, so_ref, sgb_ref, ag_ref, ng_ref, wpb_ref,
                wo_ref, l1g_ref, l1b_ref, wr_ref, rb_ref, tri_ref, x1_ref, x1p_ref, route_ref,
                cnt_ref, carry_ref):
    tm = x_ref.shape[0]

    @pl.when(pl.program_id(0) == 0)
    def _():
        carry_ref[...] = jnp.zeros_like(carry_ref)

    o = of_ref[...] + ob_ref[...]
    heads = []
    for h in range(HG_HEADS):
        oh = o[:, h * HG_DK:(h + 1) * HG_DK]
        heads.append(oh * lax.rsqrt(jnp.mean(oh * oh, axis=-1, keepdims=True) + RMS_EPS))
    rn = jnp.concatenate(heads, axis=1) * ng_ref[...] * so_ref[...].astype(F32)
    r = _bdot(rn, wpb_ref[...])
    mixed = ag_ref[...].astype(F32) + sgb_ref[...].astype(F32) * r
    y = _bdot(mixed, wo_ref[...])
    xn = _layer_norm(x_ref[...], lng_ref[...], lnb_ref[...])
    x1 = _layer_norm(ALPHA * xn + y, l1g_ref[...], l1b_ref[...])
    x1_ref[...] = x1
    x1p_ref[...] = _pack_rows(x1)

    scores = jax.nn.sigmoid(_bdot(x1, wr_ref[...]))
    lane = lax.broadcasted_iota(jnp.int32, (tm, LANES), 1)
    valid = lane < N_EXPERTS
    neg = jnp.float32(-jnp.inf)
    biased = jnp.where(valid, scores + rb_ref[...], neg)
    m1, m2 = biased, jnp.full_like(biased, neg)
    s = 1
    while s < GROUP_SIZE:
        p1, p2 = _xor_partner(m1, lane, s), _xor_partner(m2, lane, s)
        m1, m2 = jnp.maximum(m1, p1), jnp.maximum(jnp.minimum(m1, p1), jnp.maximum(m2, p2))
        s *= 2
    gs = m1 + m2
    ahead = jnp.zeros((tm, LANES), F32)
    for d in range(1, N_GROUPS):
        lower = pltpu.roll(gs, d * GROUP_SIZE, 1)
        higher = pltpu.roll(gs, LANES - d * GROUP_SIZE, 1)
        ahead = ahead + jnp.where(lower >= gs, 1.0, 0.0) + jnp.where(higher > gs, 1.0, 0.0)
    allowed = jnp.where(valid & (ahead < TOPK_GROUPS), biased, neg)
    sel = jnp.zeros((tm, LANES), F32)
    lane_f = lane.astype(F32)
    picks = []
    for _ in range(TOP_K):
        m = jnp.max(allowed, axis=-1, keepdims=True)
        first = jnp.min(jnp.where(allowed == m, lane_f, float(LANES)), axis=-1, keepdims=True)
        hit = lane_f == first
        picks.append((first, hit, jnp.sum(jnp.where(hit, scores, 0.0), axis=-1, keepdims=True)))
        sel = jnp.where(hit, 1.0, sel)
        allowed = jnp.where(hit, neg, allowed)
    wsum = picks[0][2]
    for pk in picks[1:]:
        wsum = wsum + pk[2]
    before = jnp.dot(tri_ref[...], sel.astype(BF16), preferred_element_type=F32) + carry_ref[0:1, :]
    total = carry_ref[0:1, :] + jnp.sum(sel, axis=0, keepdims=True)
    carry_ref[...] = jnp.broadcast_to(total, carry_ref.shape)
    cnt_ref[...] = jnp.broadcast_to(total, cnt_ref.shape)
    route = jnp.zeros((tm, LANES), F32)
    for k, (first, hit, w) in enumerate(picks):
        rank = jnp.sum(jnp.where(hit, before, 0.0), axis=-1, keepdims=True)
        route = jnp.where(lane == k, first, route)
        route = jnp.where(lane == 8 + k, rank, route)
        route = jnp.where(lane == 16 + k, w / wsum * ROUTED_SCALE, route)
    route_ref[...] = route


def _mix(x, of, ob, so, sgb, ag, p):
    T = x.shape[0]
    tm = TOKEN_TILE
    tok = lambda w: pl.BlockSpec((tm, w), lambda i: (i, 0))
    row = lax.broadcasted_iota(jnp.int32, (tm, tm), 0)
    col = lax.broadcasted_iota(jnp.int32, (tm, tm), 1)
    tri = (row > col).astype(BF16)
    return pl.pallas_call(
        _mix_kernel,
        grid=(T // tm,),
        in_specs=[
            tok(D_MODEL), _const_spec((1, D_MODEL)), _const_spec((1, D_MODEL)),
            tok(HG_WIDTH), tok(HG_WIDTH), tok(HG_WIDTH), tok(D_MODEL), tok(D_MODEL),
            _const_spec((1, HG_WIDTH)),
            _const_spec((HG_WIDTH, D_MODEL)), _const_spec((D_MODEL, D_MODEL)),
            _const_spec((1, D_MODEL)), _const_spec((1, D_MODEL)),
            _const_spec((D_MODEL, LANES)), _const_spec((1, LANES)),
            _const_spec((tm, tm)),
        ],
        out_specs=(tok(D_MODEL), tok(D_MODEL // 2), tok(LANES),
                   pl.BlockSpec((8, LANES), lambda i: (0, 0))),
        out_shape=(jax.ShapeDtypeStruct((T, D_MODEL), F32),
                   jax.ShapeDtypeStruct((T, D_MODEL // 2), jnp.uint32),
                   jax.ShapeDtypeStruct((T, LANES), F32),
                   jax.ShapeDtypeStruct((8, LANES), F32)),
        scratch_shapes=[pltpu.VMEM((8, LANES), F32)],
        compiler_params=pltpu.CompilerParams(
            dimension_semantics=("arbitrary",), vmem_limit_bytes=VMEM_LIMIT),
        name="mix",
    )(x, p["ln_in_g"], p["ln_in_b"], of, ob, so, sgb, ag, p["hg_norm_g"], p["w_pb"], p["w_o"],
      p["ln1_g"], p["ln1_b"], p["w_router"], p["router_bias"], tri)


def _sc_mesh():
    return plsc.VectorSubcoreMesh(core_axis_name="c", subcore_axis_name="s",
                                  num_cores=SC_CORES, num_subcores=SC_SUBCORES)


def _sc_worker():
    return lax.axis_index("s") * SC_CORES + lax.axis_index("c")


def _sc_dispatch(x1p, destw, n_rows):
    T, w = x1p.shape
    n_win, _, W = destw.shape
    per_worker = n_win // (SC_CORES * SC_SUBCORES)

    def body(x_hbm, d_hbm, o_hbm, rows_v, idx_v, sem):
        first = _sc_worker() * per_worker

        @pl.loop(0, per_worker)
        def _(j):
            win = first + j
            pltpu.sync_copy(x_hbm.at[pl.ds(win * W, W)], rows_v)
            pltpu.sync_copy(d_hbm.at[win], idx_v)
            copies = [pltpu.async_copy(rows_v, o_hbm.at[idx_v.at[k]], sem) for k in range(TOP_K)]
            for c in copies:
                c.wait()

    return pl.kernel(
        body,
        out_type=jax.ShapeDtypeStruct((n_rows, w), jnp.uint32),
        mesh=_sc_mesh(),
        scratch_types=[pltpu.VMEM((W, w), jnp.uint32), pltpu.VMEM((TOP_K, W), jnp.int32),
                       pltpu.SemaphoreType.DMA],
        name="sc_dispatch",
    )(x1p, destw)


def _sc_combine(ys, destw):
    n_win, _, W = destw.shape
    w = ys.shape[1]
    per_worker = n_win // (SC_CORES * SC_SUBCORES)

    def body(y_hbm, d_hbm, o_hbm, rows_v, idx_v, sem):
        first = _sc_worker() * per_worker

        @pl.loop(0, per_worker)
        def _(j):
            win = first + j
            pltpu.sync_copy(d_hbm.at[win], idx_v)
            gathers = [pltpu.async_copy(y_hbm.at[idx_v.at[k]], rows_v.at[k], sem)
                       for k in range(TOP_K)]
            for c in gathers:
                c.wait()
            writes = [pltpu.async_copy(rows_v.at[k], o_hbm.at[k, pl.ds(win * W, W)], sem)
                      for k in range(TOP_K)]
            for c in writes:
                c.wait()

    return pl.kernel(
        body,
        out_type=jax.ShapeDtypeStruct((TOP_K, n_win * W, w), jnp.uint32),
        mesh=_sc_mesh(),
        scratch_types=[pltpu.VMEM((TOP_K, W, w), jnp.uint32), pltpu.VMEM((TOP_K, W), jnp.int32),
                       pltpu.SemaphoreType.DMA],
        name="sc_combine",
    )(ys, destw)


def _experts_kernel(be_ref, nv_ref, xs_ref, wg_ref, wu_ref, wd_ref, ys_ref):
    del be_ref
    n_valid = nv_ref[pl.program_id(0)]
    mb, w = xs_ref.shape

    @pl.when(n_valid > 0)
    def _():
        keep = lax.broadcasted_iota(jnp.int32, (mb, w), 0) < n_valid
        lo, hi = _unpack_rows(jnp.where(keep, xs_ref[...], jnp.uint32(0)))
        lo, hi = lo.astype(BF16), hi.astype(BF16)

        def proj(w_ref):
            return (jnp.dot(lo, w_ref[0, :w, :], preferred_element_type=F32)
                    + jnp.dot(hi, w_ref[0, w:, :], preferred_element_type=F32))

        hb = jax.nn.silu(proj(wg_ref)) * proj(wu_ref)
        ys_ref[...] = _pack_rows(jnp.dot(hb.astype(BF16), wd_ref[0], preferred_element_type=F32))

    @pl.when(n_valid <= 0)
    def _():
        ys_ref[...] = jnp.zeros_like(ys_ref)


def _experts(xs, blk_exp, n_valid, p):
    n_rows, w = xs.shape
    mb = EXPERT_BLOCK
    grid_spec = pltpu.PrefetchScalarGridSpec(
        num_scalar_prefetch=2,
        grid=(n_rows // mb,),
        in_specs=[
            pl.BlockSpec((mb, w), lambda i, be, nv: (i, 0)),
            pl.BlockSpec((1, D_MODEL, EXPERT_DIM), lambda i, be, nv: (be[i], 0, 0)),
            pl.BlockSpec((1, D_MODEL, EXPERT_DIM), lambda i, be, nv: (be[i], 0, 0)),
            pl.BlockSpec((1, EXPERT_DIM, D_MODEL), lambda i, be, nv: (be[i], 0, 0)),
        ],
        out_specs=pl.BlockSpec((mb, w), lambda i, be, nv: (i, 0)),
    )
    return pl.pallas_call(
        _experts_kernel,
        grid_spec=grid_spec,
        out_shape=jax.ShapeDtypeStruct((n_rows, w), jnp.uint32),
        compiler_params=pltpu.CompilerParams(
            dimension_semantics=("arbitrary",), vmem_limit_bytes=VMEM_LIMIT),
        name="experts",
    )(blk_exp, n_valid, xs, p["w_e_gate"], p["w_e_up"], p["w_e_down"])


def _final_kernel(x1_ref, route_ref, yg_ref, wsg_ref, wsu_ref, wsd_ref, l2g_ref, l2b_ref, out_ref):
    x1 = x1_ref[...]
    xb = x1.astype(BF16)
    hs = (jax.nn.silu(jnp.dot(xb, wsg_ref[...], preferred_element_type=F32))
          * jnp.dot(xb, wsu_ref[...], preferred_element_type=F32))
    shared = jnp.dot(hs.astype(BF16), wsd_ref[...], preferred_element_type=F32)
    route = route_ref[...]
    lo, hi = None, None
    for k in range(TOP_K):
        wk = route[:, 16 + k:17 + k]
        lk, hk = _unpack_rows(yg_ref[k])
        lo = lk * wk if lo is None else lo + lk * wk
        hi = hk * wk if hi is None else hi + hk * wk
    routed = jnp.concatenate([lo, hi], axis=1)
    out_ref[...] = _layer_norm(ALPHA * x1 + (routed + shared), l2g_ref[...], l2b_ref[...])


def _final(x1, route, yg, p):
    T = x1.shape[0]
    tm = TOKEN_TILE
    tok = lambda w: pl.BlockSpec((tm, w), lambda i: (i, 0))
    return pl.pallas_call(
        _final_kernel,
        grid=(T // tm,),
        in_specs=[
            tok(D_MODEL), tok(LANES),
            pl.BlockSpec((TOP_K, tm, yg.shape[2]), lambda i: (0, i, 0)),
            _const_spec((D_MODEL, SHARED_DIM)), _const_spec((D_MODEL, SHARED_DIM)),
            _const_spec((SHARED_DIM, D_MODEL)),
            _const_spec((1, D_MODEL)), _const_spec((1, D_MODEL)),
        ],
        out_specs=tok(D_MODEL),
        out_shape=jax.ShapeDtypeStruct((T, D_MODEL), F32),
        compiler_params=pltpu.CompilerParams(
            dimension_semantics=("parallel",), vmem_limit_bytes=VMEM_LIMIT),
        name="final",
    )(x1, route, yg, p["w_sh_gate"], p["w_sh_up"], p["w_sh_down"], p["ln2_g"], p["ln2_b"])


def _routing_layout(route, counts_row, n_tokens):
    tm, mb = TOKEN_TILE, EXPERT_BLOCK
    n_blocks = -(-n_tokens * TOP_K // mb) + N_EXPERTS
    idx = route[:, 0:TOP_K].astype(jnp.int32)
    rank = route[:, 8:8 + TOP_K].astype(jnp.int32)
    counts = counts_row[:N_EXPERTS].astype(jnp.int32)
    padded = (counts + mb - 1) // mb * mb
    pad_end = jnp.cumsum(padded)
    pad_start = pad_end - padded
    experts = jnp.arange(N_EXPERTS, dtype=jnp.int32)
    dest = rank + jnp.sum(jnp.where(idx[:, :, None] == experts, pad_start, 0), axis=-1)
    destw = dest.reshape(n_tokens // SC_WINDOW, SC_WINDOW, TOP_K).transpose(0, 2, 1)
    blk_start = jnp.arange(n_blocks, dtype=jnp.int32) * mb
    blk_exp = jnp.minimum(
        jnp.sum((pad_end[None, :] <= blk_start[:, None]).astype(jnp.int32), axis=1), N_EXPERTS - 1)
    valid_end = jnp.sum(jnp.where(blk_exp[:, None] == experts, pad_start + counts, 0), axis=-1)
    n_valid = jnp.clip(valid_end - blk_start, 0, mb).astype(jnp.int32)
    return destw, blk_exp, n_valid, n_blocks * mb


def _encode(x, p):
    batch, seq, _ = x.shape
    T = batch * seq
    xt = x.reshape(T, D_MODEL)
    ag, q, gf, gb, iv, so, sgb, _ = _inproj(xt, p)
    of, ob = _hgrn(q, gf, gb, iv, batch)
    x1, x1p, route, cnt = _mix(xt, of, ob, so, sgb, ag, p)
    destw, blk_exp, n_valid, n_rows = _routing_layout(route, cnt[0], T)
    xs = _sc_dispatch(x1p, destw, n_rows)
    ys = _experts(xs, blk_exp, n_valid, p)
    out = _final(x1, route, _sc_combine(ys, destw), p)
    return out.reshape(batch, seq, D_MODEL)


def _prepare_params(ln_in_g, ln_in_b, w_in, a_ln_g, a_ln_b, a_ws, a_sb, hg_lb_logits, hg_norm_g,
                    w_pa, w_pb, w_o, ln1_g, ln1_b, w_router, router_bias, w_e_gate, w_e_up,
                    w_e_down, w_sh_gate, w_sh_up, w_sh_down, ln2_g, ln2_b):
    l = 0
    row = lambda v: v.reshape(1, -1).astype(F32)
    ws = a_ws[l].astype(BF16)
    wsp = jnp.concatenate([ws[0::2], ws[1::2]], axis=2)
    sbf = jnp.repeat(a_sb[l].astype(F32), A_WIDTH // A_GROUPS, axis=1)
    lb = jnp.cumsum(jax.nn.softmax(hg_lb_logits.astype(F32), axis=1), axis=1)[:, l]
    pad = LANES - N_EXPERTS
    return dict(
        ln_in_g=row(ln_in_g), ln_in_b=row(ln_in_b), w_in=w_in[l].astype(BF16),
        a_ln_g=row(a_ln_g[l]), a_ln_b=row(a_ln_b[l]), wsp=wsp, sbf=sbf, lb=lb,
        w_pa=w_pa[l].astype(BF16), hg_norm_g=row(hg_norm_g[l]),
        w_pb=w_pb[l].astype(BF16), w_o=w_o[l].astype(BF16),
        ln1_g=row(ln1_g[l]), ln1_b=row(ln1_b[l]),
        w_router=jnp.pad(w_router[l], ((0, 0), (0, pad))).astype(BF16),
        router_bias=jnp.pad(router_bias[l].astype(F32), (0, pad)).reshape(1, LANES),
        w_e_gate=w_e_gate[l].astype(BF16), w_e_up=w_e_up[l].astype(BF16),
        w_e_down=w_e_down[l].astype(BF16),
        w_sh_gate=w_sh_gate[l].astype(BF16), w_sh_up=w_sh_up[l].astype(BF16),
        w_sh_down=w_sh_down[l].astype(BF16),
        ln2_g=row(ln2_g[l]), ln2_b=row(ln2_b[l]),
    )


def kernel(x_prompt, x_sample, ln_in_g, ln_in_b, w_in, a_ln_g, a_ln_b, a_ws, a_sb, hg_lb_logits,
           hg_norm_g, w_pa, w_pb, w_o, ln1_g, ln1_b, w_router, router_bias, w_e_gate, w_e_up,
           w_e_down, w_sh_gate, w_sh_up, w_sh_down, ln2_g, ln2_b):
    p = _prepare_params(ln_in_g, ln_in_b, w_in, a_ln_g, a_ln_b, a_ws, a_sb, hg_lb_logits, hg_norm_g,
                        w_pa, w_pb, w_o, ln1_g, ln1_b, w_router, router_bias, w_e_gate, w_e_up,
                        w_e_down, w_sh_gate, w_sh_up, w_sh_down, ln2_g, ln2_b)
    return _encode(x_prompt, p), _encode(x_sample, p)
```

```python
import functools

import jax
import jax.numpy as jnp
from jax import lax
from jax.experimental import pallas as pl
from jax.experimental.pallas import tpu as pltpu
from jax.experimental.pallas import tpu_sc as plsc

F32 = jnp.float32
BF16 = jnp.bfloat16

D_MODEL = 1024
A_GROUPS = 8
A_WIDTH = 512
A_CHUNK = 128
HG_HEADS = 8
HG_DK = 128
HG_WIDTH = HG_HEADS * HG_DK
N_IN = 2 * A_WIDTH + 5 * HG_WIDTH + 2 * D_MODEL
N_EXPERTS = 64
TOP_K = 6
N_GROUPS = 8
TOPK_GROUPS = 4
GROUP_SIZE = N_EXPERTS // N_GROUPS
EXPERT_DIM = 256
SHARED_DIM = 256
ROUTED_SCALE = 2.5
DEPTH = 1
ALPHA = (2.0 * DEPTH) ** 0.25
LN_EPS = 1e-5
RMS_EPS = 1e-6

LANES = 128
TOKEN_TILE = 256
INPROJ_TILE = 512
HG_CHUNK = 128
HG_SAFE_LOGDECAY = 80.0
EXPERT_BLOCK = 512
SC_CORES = 2
SC_SUBCORES = 16
SC_WINDOW = 32
VMEM_LIMIT = 56 * 1024 * 1024

_O_U, _O_V, _O_Q, _O_FF, _O_FB, _O_I, _O_G, _O_GA, _O_GB = (
    0, 512, 1024, 2048, 3072, 4096, 5120, 6144, 7168)


def _layer_norm(x, g, b):
    mu = jnp.mean(x, axis=-1, keepdims=True)
    xc = x - mu
    var = jnp.mean(xc * xc, axis=-1, keepdims=True)
    return xc * lax.rsqrt(var + LN_EPS) * g + b


def _bdot(a, b):
    return jnp.dot(a.astype(BF16), b.astype(BF16), preferred_element_type=F32)


def _pack_rows(x):
    w = x.shape[1] // 2
    lo = lax.bitcast_convert_type(x[:, :w].astype(BF16).astype(F32), jnp.uint32)
    hi = lax.bitcast_convert_type(x[:, w:].astype(BF16).astype(F32), jnp.uint32)
    return hi | (lo >> 16)


def _unpack_rows(p):
    lo = lax.bitcast_convert_type(p << 16, F32)
    hi = lax.bitcast_convert_type(p & jnp.uint32(0xFFFF0000), F32)
    return lo, hi


def _const_spec(shape):
    nd = len(shape)
    return pl.BlockSpec(shape, lambda *_: (0,) * nd, pipeline_mode=pl.Buffered(1))


def _inproj_kernel(x_ref, lng_ref, lnb_ref, win_ref, alng_ref, alnb_ref, wsp_ref, sbf_ref, lb_ref,
                   wpa_ref, ag_ref, q_ref, gf_ref, gb_ref, iv_ref, so_ref, sgb_ref, gmin_ref):
    tm = x_ref.shape[0]
    xb = _layer_norm(x_ref[...], lng_ref[...], lnb_ref[...]).astype(BF16)

    def sec(lo, width):
        return jnp.dot(xb, win_ref[:, lo:lo + width], preferred_element_type=F32)

    u = jax.nn.gelu(sec(_O_U, A_WIDTH))
    v = _layer_norm(jax.nn.gelu(sec(_O_V, A_WIDTH)), alng_ref[...], alnb_ref[...]).astype(BF16)
    lane = lax.broadcasted_iota(jnp.int32, (A_CHUNK, LANES), 1)
    left = lane < (A_WIDTH // A_GROUPS)
    zero = jnp.zeros((A_CHUNK, LANES), BF16)
    chunks = []
    for c in range(tm // A_CHUNK):
        vc = v[c * A_CHUNK:(c + 1) * A_CHUNK]
        cols = []
        for p in range(A_GROUPS // 2):
            vp = vc[:, p * LANES:(p + 1) * LANES]
            rhs = jnp.concatenate([jnp.where(left, vp, zero), jnp.where(left, zero, vp)], axis=0)
            cols.append(jnp.dot(wsp_ref[p], rhs, preferred_element_type=F32))
        chunks.append(jnp.concatenate(cols, axis=1) + sbf_ref[...])
    mixed = jnp.concatenate(chunks, axis=0)
    a = _bdot(u * mixed, wpa_ref[...])
    ag_ref[...] = (jax.nn.sigmoid(sec(_O_GA, D_MODEL)) * a).astype(BF16)

    q_ref[...] = jax.nn.silu(sec(_O_Q, HG_WIDTH)).astype(BF16)
    mins = []
    for d, (off, g_ref) in enumerate(((_O_FF, gf_ref), (_O_FB, gb_ref))):
        lb = lb_ref[d:d + 1, :]
        f = lb + (1.0 - lb) * jax.nn.sigmoid(sec(off, HG_WIDTH))
        g = jnp.log(f)
        g_ref[...] = g
        half = jnp.sum(g.reshape(tm // (HG_CHUNK // 2), HG_CHUNK // 2, HG_WIDTH), axis=1)
        mins.append(jnp.min(half, axis=-1, keepdims=True))
    gmin_ref[0] = jnp.broadcast_to(jnp.concatenate(mins, axis=0), gmin_ref.shape[1:])
    iv_ref[...] = sec(_O_I, HG_WIDTH).astype(BF16)
    so_ref[...] = jax.nn.silu(sec(_O_G, HG_WIDTH)).astype(BF16)
    sgb_ref[...] = jax.nn.sigmoid(sec(_O_GB, D_MODEL)).astype(BF16)


def _inproj(x, p):
    T = x.shape[0]
    tm = INPROJ_TILE
    nt = T // tm
    nh = 2 * tm // (HG_CHUNK // 2)
    tok = lambda w: pl.BlockSpec((tm, w), lambda i: (i, 0))
    outs = (
        jax.ShapeDtypeStruct((T, D_MODEL), BF16),
        jax.ShapeDtypeStruct((T, HG_WIDTH), BF16),
        jax.ShapeDtypeStruct((T, HG_WIDTH), F32),
        jax.ShapeDtypeStruct((T, HG_WIDTH), F32),
        jax.ShapeDtypeStruct((T, HG_WIDTH), BF16),
        jax.ShapeDtypeStruct((T, HG_WIDTH), BF16),
        jax.ShapeDtypeStruct((T, D_MODEL), BF16),
        jax.ShapeDtypeStruct((nt, nh, LANES), F32),
    )
    return pl.pallas_call(
        _inproj_kernel,
        grid=(nt,),
        in_specs=[
            tok(D_MODEL),
            _const_spec((1, D_MODEL)), _const_spec((1, D_MODEL)),
            _const_spec((D_MODEL, N_IN)),
            _const_spec((1, A_WIDTH)), _const_spec((1, A_WIDTH)),
            _const_spec((A_GROUPS // 2, A_CHUNK, 2 * A_CHUNK)),
            _const_spec((A_CHUNK, A_WIDTH)),
            _const_spec((2, HG_WIDTH)),
            _const_spec((A_WIDTH, D_MODEL)),
        ],
        out_specs=(tok(D_MODEL), tok(HG_WIDTH), tok(HG_WIDTH), tok(HG_WIDTH), tok(HG_WIDTH),
                   tok(HG_WIDTH), tok(D_MODEL), pl.BlockSpec((1, nh, LANES), lambda i: (i, 0, 0))),
        out_shape=outs,
        compiler_params=pltpu.CompilerParams(
            dimension_semantics=("parallel",), vmem_limit_bytes=VMEM_LIMIT),
        name="inproj",
    )(x, p["ln_in_g"], p["ln_in_b"], p["w_in"], p["a_ln_g"], p["a_ln_b"], p["wsp"], p["sbf"],
      p["lb"], p["w_pa"])


_NT = (((1,), (1,)), ((), ()))
_TN = (((0,), (0,)), ((), ()))


def _hgrn_direction(q_ref, g_ref, v_ref, st_ref, o_ref, tri, mask, fwd):
    C = q_ref.shape[0]
    g = g_ref[...]
    ghi = g.astype(BF16)
    glo = (g - ghi.astype(F32)).astype(BF16)
    b = (jnp.dot(tri, ghi, preferred_element_type=F32)
         + jnp.dot(tri, glo, preferred_element_type=F32))
    mid = C // 2 - 1 if fwd else C // 2
    end = C - 1 if fwd else 0
    r = b[mid:mid + 1, :]
    b_end = b[end:end + 1, :]
    qt = q_ref[...].astype(F32) * jnp.exp(b - r)
    kt = (1.0 - jnp.exp(g)) * jnp.exp(r - b)
    qtb = qt.astype(BF16)
    ktb = kt.astype(BF16)
    qhb = (qt * jnp.exp(r)).astype(BF16)
    khb = (kt * jnp.exp(b_end - r)).astype(BF16)
    decay = jnp.exp(b_end)
    v = v_ref[...]
    heads = [slice(h * HG_DK, (h + 1) * HG_DK) for h in range(HG_HEADS)]
    scores = [jnp.where(mask, lax.dot_general(qtb[:, sl], ktb[:, sl], _NT,
                                              preferred_element_type=F32), 0.0).astype(BF16)
              for sl in heads]
    for h, sl in enumerate(heads):
        st = st_ref[h]
        o_ref[:, sl] = (jnp.dot(scores[h], v[:, sl], preferred_element_type=F32)
                        + lax.dot_general(qhb[:, sl], st.astype(BF16), _NT,
                                          preferred_element_type=F32))
        st_ref[h] = st * decay[:, sl] + lax.dot_general(v[:, sl], khb[:, sl], _TN,
                                                        preferred_element_type=F32)


def _hgrn_direction_stepwise(q_ref, g_ref, v_ref, st_ref, o_ref, q32_ref, v32_ref, fwd):
    C = q_ref.shape[0]
    q32_ref[...] = q_ref[...].astype(F32)
    v32_ref[...] = v_ref[...].astype(F32)
    sub = 8
    rows = lax.broadcasted_iota(jnp.int32, (sub, HG_DK), 0)

    def group(i, carry):
        base = pl.multiple_of((i if fwd else C // sub - 1 - i) * sub, sub)
        f = jnp.exp(g_ref[pl.ds(base, sub), :])
        k = 1.0 - f
        q = q32_ref[pl.ds(base, sub), :]
        v = v32_ref[pl.ds(base, sub), :]
        for h in range(HG_HEADS):
            sl = slice(h * HG_DK, (h + 1) * HG_DK)
            st = st_ref[h]
            out = jnp.zeros((sub, HG_DK), F32)
            for r in (range(sub) if fwd else range(sub - 1, -1, -1)):
                v_t = jnp.where(rows == 0, v[r:r + 1, sl], 0.0).astype(BF16)
                k_t = jnp.broadcast_to(k[r:r + 1, sl], (sub, HG_DK)).astype(BF16)
                st = st * f[r:r + 1, sl] + lax.dot_general(v_t, k_t, _TN, preferred_element_type=F32)
                q_t = jnp.broadcast_to(q[r:r + 1, sl], (sub, HG_DK)).astype(BF16)
                o_t = lax.dot_general(q_t, st.astype(BF16), _NT, preferred_element_type=F32)
                out = jnp.where(rows == r, o_t, out)
            st_ref[h] = st
            o_ref[pl.ds(base, sub), sl] = out
        return carry

    lax.fori_loop(0, C // sub, group, 0)


def _hgrn_kernel(safe_ref, qf_ref, qb_ref, gf_ref, gb_ref, vf_ref, vb_ref, tril_ref, triu_ref,
                 of_ref, ob_ref, sf_ref, sb_ref, q32_ref, v32_ref):
    b, j = pl.program_id(0), pl.program_id(1)
    nc = pl.num_programs(1)

    @pl.when(j == 0)
    def _():
        sf_ref[...] = jnp.zeros_like(sf_ref)
        sb_ref[...] = jnp.zeros_like(sb_ref)

    C = qf_ref.shape[0]
    row = lax.broadcasted_iota(jnp.int32, (C, C), 0)
    col = lax.broadcasted_iota(jnp.int32, (C, C), 1)
    safe_f = safe_ref[0, b * nc + j] != 0
    safe_b = safe_ref[1, b * nc + nc - 1 - j] != 0

    @pl.when(safe_f)
    def _():
        _hgrn_direction(qf_ref, gf_ref, vf_ref, sf_ref, of_ref, tril_ref[...], row >= col, True)

    @pl.when(jnp.logical_not(safe_f))
    def _():
        _hgrn_direction_stepwise(qf_ref, gf_ref, vf_ref, sf_ref, of_ref, q32_ref, v32_ref, True)

    @pl.when(safe_b)
    def _():
        _hgrn_direction(qb_ref, gb_ref, vb_ref, sb_ref, ob_ref, triu_ref[...], row <= col, False)

    @pl.when(jnp.logical_not(safe_b))
    def _():
        _hgrn_direction_stepwise(qb_ref, gb_ref, vb_ref, sb_ref, ob_ref, q32_ref, v32_ref, False)


def _hgrn(q, gf, gb, iv, gmin, batch):
    T = q.shape[0]
    C = HG_CHUNK
    nc = T // batch // C
    nt, nh, _ = gmin.shape
    halves = gmin[:, :, 0].reshape(nt, 2, nh // 4, 2)
    safe = (jnp.min(halves, axis=-1) > -HG_SAFE_LOGDECAY).astype(jnp.int32)
    safe = safe.transpose(1, 0, 2).reshape(2, T // C)
    fwd = pl.BlockSpec((C, HG_WIDTH), lambda b, j, s: (b * nc + j, 0))
    bwd = pl.BlockSpec((C, HG_WIDTH), lambda b, j, s: (b * nc + nc - 1 - j, 0))
    const = lambda shape: pl.BlockSpec(shape, lambda b, j, s: (0,) * len(shape),
                                       pipeline_mode=pl.Buffered(1))
    row = lax.broadcasted_iota(jnp.int32, (C, C), 0)
    col = lax.broadcasted_iota(jnp.int32, (C, C), 1)
    tril = (row >= col).astype(BF16)
    triu = (row <= col).astype(BF16)
    grid_spec = pltpu.PrefetchScalarGridSpec(
        num_scalar_prefetch=1,
        grid=(batch, nc),
        in_specs=[fwd, bwd, fwd, bwd, fwd, bwd, const((C, C)), const((C, C))],
        out_specs=(fwd, bwd),
        scratch_shapes=[pltpu.VMEM((HG_HEADS, HG_DK, HG_DK), F32),
                        pltpu.VMEM((HG_HEADS, HG_DK, HG_DK), F32),
                        pltpu.VMEM((C, HG_WIDTH), F32), pltpu.VMEM((C, HG_WIDTH), F32)],
    )
    return pl.pallas_call(
        _hgrn_kernel,
        grid_spec=grid_spec,
        out_shape=(jax.ShapeDtypeStruct((T, HG_WIDTH), F32), jax.ShapeDtypeStruct((T, HG_WIDTH), F32)),
        compiler_params=pltpu.CompilerParams(
            dimension_semantics=("parallel", "arbitrary"), vmem_limit_bytes=VMEM_LIMIT),
        name="hgrn",
    )(safe, q, q, gf, gb, iv, iv, tril, triu)


def _xor_partner(x, lane, s):
    return jnp.where((lane & s) != 0, pltpu.roll(x, s, 1), pltpu.roll(x, LANES - s, 1))


def _mix_kernel(x_ref, lng_ref, lnb_ref, of_ref, ob_ref, so_ref, sgb_ref, ag_ref, ng_ref, wpb_ref,
                wo_ref, l1g_ref, l1b_ref, wr_ref, rb_ref, tri_ref, x1_ref, x1p_ref, route_ref,
                cnt_ref, carry_ref):
    tm = x_ref.shape[0]

    @pl.when(pl.program_id(0) == 0)
    def _():
        carry_ref[...] = jnp.zeros_like(carry_ref)

    o = of_ref[...] + ob_ref[...]
    heads = []
    for h in range(HG_HEADS):
        oh = o[:, h * HG_DK:(h + 1) * HG_DK]
        heads.append(oh * lax.rsqrt(jnp.mean(oh * oh, axis=-1, keepdims=True) + RMS_EPS))
    rn = jnp.concatenate(heads, axis=1) * ng_ref[...] * so_ref[...].astype(F32)
    r = _bdot(rn, wpb_ref[...])
    mixed = ag_ref[...].astype(F32) + sgb_ref[...].astype(F32) * r
    y = _bdot(mixed, wo_ref[...])
    xn = _layer_norm(x_ref[...], lng_ref[...], lnb_ref[...])
    x1 = _layer_norm(ALPHA * xn + y, l1g_ref[...], l1b_ref[...])
    x1_ref[...] = x1
    x1p_ref[...] = _pack_rows(x1)

    scores = jax.nn.sigmoid(_bdot(x1, wr_ref[...]))
    lane = lax.broadcasted_iota(jnp.int32, (tm, LANES), 1)
    valid = lane < N_EXPERTS
    neg = jnp.float32(-jnp.inf)
    biased = jnp.where(valid, scores + rb_ref[...], neg)
    m1, m2 = biased, jnp.full_like(biased, neg)
    s = 1
    while s < GROUP_SIZE:
        p1, p2 = _xor_partner(m1, lane, s), _xor_partner(m2, lane, s)
        m1, m2 = jnp.maximum(m1, p1), jnp.maximum(jnp.minimum(m1, p1), jnp.maximum(m2, p2))
        s *= 2
    gs = m1 + m2
    ahead = jnp.zeros((tm, LANES), F32)
    for d in range(1, N_GROUPS):
        lower = pltpu.roll(gs, d * GROUP_SIZE, 1)
        higher = pltpu.roll(gs, LANES - d * GROUP_SIZE, 1)
        ahead = ahead + jnp.where(lower >= gs, 1.0, 0.0) + jnp.where(higher > gs, 1.0, 0.0)
    allowed = jnp.where(valid & (ahead < TOPK_GROUPS), biased, neg)
    sel = jnp.zeros((tm, LANES), F32)
    lane_f = lane.astype(F32)
    picks = []
    for _ in range(TOP_K):
        m = jnp.max(allowed, axis=-1, keepdims=True)
        first = jnp.min(jnp.where(allowed == m, lane_f, float(LANES)), axis=-1, keepdims=True)
        hit = lane_f == first
        picks.append((first, hit, jnp.sum(jnp.where(hit, scores, 0.0), axis=-1, keepdims=True)))
        sel = jnp.where(hit, 1.0, sel)
        allowed = jnp.where(hit, neg, allowed)
    wsum = picks[0][2]
    for pk in picks[1:]:
        wsum = wsum + pk[2]
    before = jnp.dot(tri_ref[...], sel.astype(BF16), preferred_element_type=F32) + carry_ref[0:1, :]
    total = carry_ref[0:1, :] + jnp.sum(sel, axis=0, keepdims=True)
    carry_ref[...] = jnp.broadcast_to(total, carry_ref.shape)
    cnt_ref[...] = jnp.broadcast_to(total, cnt_ref.shape)
    route = jnp.zeros((tm, LANES), F32)
    for k, (first, hit, w) in enumerate(picks):
        rank = jnp.sum(jnp.where(hit, before, 0.0), axis=-1, keepdims=True)
        route = jnp.where(lane == k, first, route)
        route = jnp.where(lane == 8 + k, rank, route)
        route = jnp.where(lane == 16 + k, w / wsum * ROUTED_SCALE, route)
    route_ref[...] = route


def _mix(x, of, ob, so, sgb, ag, p):
    T = x.shape[0]
    tm = TOKEN_TILE
    tok = lambda w: pl.BlockSpec((tm, w), lambda i: (i, 0))
    row = lax.broadcasted_iota(jnp.int32, (tm, tm), 0)
    col = lax.broadcasted_iota(jnp.int32, (tm, tm), 1)
    tri = (row > col).astype(BF16)
    return pl.pallas_call(
        _mix_kernel,
        grid=(T // tm,),
        in_specs=[
            tok(D_MODEL), _const_spec((1, D_MODEL)), _const_spec((1, D_MODEL)),
            tok(HG_WIDTH), tok(HG_WIDTH), tok(HG_WIDTH), tok(D_MODEL), tok(D_MODEL),
            _const_spec((1, HG_WIDTH)),
            _const_spec((HG_WIDTH, D_MODEL)), _const_spec((D_MODEL, D_MODEL)),
            _const_spec((1, D_MODEL)), _const_spec((1, D_MODEL)),
            _const_spec((D_MODEL, LANES)), _const_spec((1, LANES)),
            _const_spec((tm, tm)),
        ],
        out_specs=(tok(D_MODEL), tok(D_MODEL // 2), tok(LANES),
                   pl.BlockSpec((8, LANES), lambda i: (0, 0))),
        out_shape=(jax.ShapeDtypeStruct((T, D_MODEL), F32),
                   jax.ShapeDtypeStruct((T, D_MODEL // 2), jnp.uint32),
                   jax.ShapeDtypeStruct((T, LANES), F32),
                   jax.ShapeDtypeStruct((8, LANES), F32)),
        scratch_shapes=[pltpu.VMEM((8, LANES), F32)],
        compiler_params=pltpu.CompilerParams(
            dimension_semantics=("arbitrary",), vmem_limit_bytes=VMEM_LIMIT),
        name="mix",
    )(x, p["ln_in_g"], p["ln_in_b"], of, ob, so, sgb, ag, p["hg_norm_g"], p["w_pb"], p["w_o"],
      p["ln1_g"], p["ln1_b"], p["w_router"], p["router_bias"], tri)


def _sc_mesh():
    return plsc.VectorSubcoreMesh(core_axis_name="c", subcore_axis_name="s",
                                  num_cores=SC_CORES, num_subcores=SC_SUBCORES)


def _sc_worker():
    return lax.axis_index("s") * SC_CORES + lax.axis_index("c")


def _sc_dispatch(x1p, destw, n_rows):
    T, w = x1p.shape
    n_win, _, W = destw.shape
    per_worker = n_win // (SC_CORES * SC_SUBCORES)

    def body(x_hbm, d_hbm, o_hbm, rows_v, idx_v, sem):
        first = _sc_worker() * per_worker

        @pl.loop(0, per_worker)
        def _(j):
            win = first + j
            pltpu.sync_copy(x_hbm.at[pl.ds(win * W, W)], rows_v)
            pltpu.sync_copy(d_hbm.at[win], idx_v)
            copies = [pltpu.async_copy(rows_v, o_hbm.at[idx_v.at[k]], sem) for k in range(TOP_K)]
            for c in copies:
                c.wait()

    return pl.kernel(
        body,
        out_type=jax.ShapeDtypeStruct((n_rows, w), jnp.uint32),
        mesh=_sc_mesh(),
        scratch_types=[pltpu.VMEM((W, w), jnp.uint32), pltpu.VMEM((TOP_K, W), jnp.int32),
                       pltpu.SemaphoreType.DMA],
        name="sc_dispatch",
    )(x1p, destw)


def _sc_combine(ys, destw):
    n_win, _, W = destw.shape
    w = ys.shape[1]
    per_worker = n_win // (SC_CORES * SC_SUBCORES)

    def body(y_hbm, d_hbm, o_hbm, rows_v, idx_v, sem):
        first = _sc_worker() * per_worker

        @pl.loop(0, per_worker)
        def _(j):
            win = first + j
            pltpu.sync_copy(d_hbm.at[win], idx_v)
            gathers = [pltpu.async_copy(y_hbm.at[idx_v.at[k]], rows_v.at[k], sem)
                       for k in range(TOP_K)]
            for c in gathers:
                c.wait()
            writes = [pltpu.async_copy(rows_v.at[k], o_hbm.at[k, pl.ds(win * W, W)], sem)
                      for k in range(TOP_K)]
            for c in writes:
                c.wait()

    return pl.kernel(
        body,
        out_type=jax.ShapeDtypeStruct((TOP_K, n_win * W, w), jnp.uint32),
        mesh=_sc_mesh(),
        scratch_types=[pltpu.VMEM((TOP_K, W, w), jnp.uint32), pltpu.VMEM((TOP_K, W), jnp.int32),
                       pltpu.SemaphoreType.DMA],
        name="sc_combine",
    )(ys, destw)


def _experts_kernel(be_ref, nv_ref, xs_ref, wg_ref, wu_ref, wd_ref, ys_ref):
    del be_ref
    n_valid = nv_ref[pl.program_id(0)]
    mb, w = xs_ref.shape

    @pl.when(n_valid > 0)
    def _():
        keep = lax.broadcasted_iota(jnp.int32, (mb, w), 0) < n_valid
        lo, hi = _unpack_rows(jnp.where(keep, xs_ref[...], jnp.uint32(0)))
        lo, hi = lo.astype(BF16), hi.astype(BF16)

        def proj(w_ref):
            return (jnp.dot(lo, w_ref[0, :w, :], preferred_element_type=F32)
                    + jnp.dot(hi, w_ref[0, w:, :], preferred_element_type=F32))

        hb = jax.nn.silu(proj(wg_ref)) * proj(wu_ref)
        ys_ref[...] = _pack_rows(jnp.dot(hb.astype(BF16), wd_ref[0], preferred_element_type=F32))

    @pl.when(n_valid <= 0)
    def _():
        ys_ref[...] = jnp.zeros_like(ys_ref)


def _experts(xs, blk_exp, n_valid, p):
    n_rows, w = xs.shape
    mb = EXPERT_BLOCK
    grid_spec = pltpu.PrefetchScalarGridSpec(
        num_scalar_prefetch=2,
        grid=(n_rows // mb,),
        in_specs=[
            pl.BlockSpec((mb, w), lambda i, be, nv: (i, 0)),
            pl.BlockSpec((1, D_MODEL, EXPERT_DIM), lambda i, be, nv: (be[i], 0, 0)),
            pl.BlockSpec((1, D_MODEL, EXPERT_DIM), lambda i, be, nv: (be[i], 0, 0)),
            pl.BlockSpec((1, EXPERT_DIM, D_MODEL), lambda i, be, nv: (be[i], 0, 0)),
        ],
        out_specs=pl.BlockSpec((mb, w), lambda i, be, nv: (i, 0)),
    )
    return pl.pallas_call(
        _experts_kernel,
        grid_spec=grid_spec,
        out_shape=jax.ShapeDtypeStruct((n_rows, w), jnp.uint32),
        compiler_params=pltpu.CompilerParams(
            dimension_semantics=("arbitrary",), vmem_limit_bytes=VMEM_LIMIT),
        name="experts",
    )(blk_exp, n_valid, xs, p["w_e_gate"], p["w_e_up"], p["w_e_down"])


def _final_kernel(x1_ref, route_ref, yg_ref, wsg_ref, wsu_ref, wsd_ref, l2g_ref, l2b_ref, out_ref):
    x1 = x1_ref[...]
    xb = x1.astype(BF16)
    hs = (jax.nn.silu(jnp.dot(xb, wsg_ref[...], preferred_element_type=F32))
          * jnp.dot(xb, wsu_ref[...], preferred_element_type=F32))
    shared = jnp.dot(hs.astype(BF16), wsd_ref[...], preferred_element_type=F32)
    route = route_ref[...]
    lo, hi = None, None
    for k in range(TOP_K):
        wk = route[:, 16 + k:17 + k]
        lk, hk = _unpack_rows(yg_ref[k])
        lo = lk * wk if lo is None else lo + lk * wk
        hi = hk * wk if hi is None else hi + hk * wk
    routed = jnp.concatenate([lo, hi], axis=1)
    out_ref[...] = _layer_norm(ALPHA * x1 + (routed + shared), l2g_ref[...], l2b_ref[...])


def _final(x1, route, yg, p):
    T = x1.shape[0]
    tm = TOKEN_TILE
    tok = lambda w: pl.BlockSpec((tm, w), lambda i: (i, 0))
    return pl.pallas_call(
        _final_kernel,
        grid=(T // tm,),
        in_specs=[
            tok(D_MODEL), tok(LANES),
            pl.BlockSpec((TOP_K, tm, yg.shape[2]), lambda i: (0, i, 0)),
            _const_spec((D_MODEL, SHARED_DIM)), _const_spec((D_MODEL, SHARED_DIM)),
            _const_spec((SHARED_DIM, D_MODEL)),
            _const_spec((1, D_MODEL)), _const_spec((1, D_MODEL)),
        ],
        out_specs=tok(D_MODEL),
        out_shape=jax.ShapeDtypeStruct((T, D_MODEL), F32),
        compiler_params=pltpu.CompilerParams(
            dimension_semantics=("parallel",), vmem_limit_bytes=VMEM_LIMIT),
        name="final",
    )(x1, route, yg, p["w_sh_gate"], p["w_sh_up"], p["w_sh_down"], p["ln2_g"], p["ln2_b"])


def _routing_layout(route, counts_row, n_tokens):
    tm, mb = TOKEN_TILE, EXPERT_BLOCK
    n_blocks = -(-n_tokens * TOP_K // mb) + N_EXPERTS
    idx = route[:, 0:TOP_K].astype(jnp.int32)
    rank = route[:, 8:8 + TOP_K].astype(jnp.int32)
    counts = counts_row[:N_EXPERTS].astype(jnp.int32)
    padded = (counts + mb - 1) // mb * mb
    pad_end = jnp.cumsum(padded)
    pad_start = pad_end - padded
    experts = jnp.arange(N_EXPERTS, dtype=jnp.int32)
    dest = rank + jnp.sum(jnp.where(idx[:, :, None] == experts, pad_start, 0), axis=-1)
    destw = dest.reshape(n_tokens // SC_WINDOW, SC_WINDOW, TOP_K).transpose(0, 2, 1)
    blk_start = jnp.arange(n_blocks, dtype=jnp.int32) * mb
    blk_exp = jnp.minimum(
        jnp.sum((pad_end[None, :] <= blk_start[:, None]).astype(jnp.int32), axis=1), N_EXPERTS - 1)
    valid_end = jnp.sum(jnp.where(blk_exp[:, None] == experts, pad_start + counts, 0), axis=-1)
    n_valid = jnp.clip(valid_end - blk_start, 0, mb).astype(jnp.int32)
    return destw, blk_exp, n_valid, n_blocks * mb


def _encode(x, p):
    batch, seq, _ = x.shape
    T = batch * seq
    xt = x.reshape(T, D_MODEL)
    ag, q, gf, gb, iv, so, sgb, gmin = _inproj(xt, p)
    of, ob = _hgrn(q, gf, gb, iv, gmin, batch)
    x1, x1p, route, cnt = _mix(xt, of, ob, so, sgb, ag, p)
    destw, blk_exp, n_valid, n_rows = _routing_layout(route, cnt[0], T)
    xs = _sc_dispatch(x1p, destw, n_rows)
    ys = _experts(xs, blk_exp, n_valid, p)
    out = _final(x1, route, _sc_combine(ys, destw), p)
    return out.reshape(batch, seq, D_MODEL)


def _prepare_params(ln_in_g, ln_in_b, w_in, a_ln_g, a_ln_b, a_ws, a_sb, hg_lb_logits, hg_norm_g,
                    w_pa, w_pb, w_o, ln1_g, ln1_b, w_router, router_bias, w_e_gate, w_e_up,
                    w_e_down, w_sh_gate, w_sh_up, w_sh_down, ln2_g, ln2_b):
    l = 0
    row = lambda v: v.reshape(1, -1).astype(F32)
    ws = a_ws[l].astype(BF16)
    wsp = jnp.concatenate([ws[0::2], ws[1::2]], axis=2)
    sbf = jnp.repeat(a_sb[l].astype(F32), A_WIDTH // A_GROUPS, axis=1)
    lb = jnp.cumsum(jax.nn.softmax(hg_lb_logits.astype(F32), axis=1), axis=1)[:, l]
    pad = LANES - N_EXPERTS
    return dict(
        ln_in_g=row(ln_in_g), ln_in_b=row(ln_in_b), w_in=w_in[l].astype(BF16),
        a_ln_g=row(a_ln_g[l]), a_ln_b=row(a_ln_b[l]), wsp=wsp, sbf=sbf, lb=lb,
        w_pa=w_pa[l].astype(BF16), hg_norm_g=row(hg_norm_g[l]),
        w_pb=w_pb[l].astype(BF16), w_o=w_o[l].astype(BF16),
        ln1_g=row(ln1_g[l]), ln1_b=row(ln1_b[l]),
        w_router=jnp.pad(w_router[l], ((0, 0), (0, pad))).astype(BF16),
        router_bias=jnp.pad(router_bias[l].astype(F32), (0, pad)).reshape(1, LANES),
        w_e_gate=w_e_gate[l].astype(BF16), w_e_up=w_e_up[l].astype(BF16),
        w_e_down=w_e_down[l].astype(BF16),
        w_sh_gate=w_sh_gate[l].astype(BF16), w_sh_up=w_sh_up[l].astype(BF16),
        w_sh_down=w_sh_down[l].astype(BF16),
        ln2_g=row(ln2_g[l]), ln2_b=row(ln2_b[l]),
    )


def kernel(x_prompt, x_sample, ln_in_g, ln_in_b, w_in, a_ln_g, a_ln_b, a_ws, a_sb, hg_lb_logits,
           hg_norm_g, w_pa, w_pb, w_o, ln1_g, ln1_b, w_router, router_bias, w_e_gate, w_e_up,
           w_e_down, w_sh_gate, w_sh_up, w_sh_down, ln2_g, ln2_b):
    p = _prepare_params(ln_in_g, ln_in_b, w_in, a_ln_g, a_ln_b, a_ws, a_sb, hg_lb_logits, hg_norm_g,
                        w_pa, w_pb, w_o, ln1_g, ln1_b, w_router, router_bias, w_e_gate, w_e_up,
                        w_e_down, w_sh_gate, w_sh_up, w_sh_down, ln2_g, ln2_b)
    return _encode(x_prompt, p), _encode(x_sample, p)
```

```python
import functools

import jax
import jax.numpy as jnp
from jax import lax
from jax.experimental import pallas as pl
from jax.experimental.pallas import tpu as pltpu
from jax.experimental.pallas import tpu_sc as plsc

F32 = jnp.float32
BF16 = jnp.bfloat16

D_MODEL = 1024
A_GROUPS = 8
A_WIDTH = 512
A_CHUNK = 128
HG_HEADS = 8
HG_DK = 128
HG_WIDTH = HG_HEADS * HG_DK
N_IN = 2 * A_WIDTH + 5 * HG_WIDTH + 2 * D_MODEL
N_EXPERTS = 64
TOP_K = 6
N_GROUPS = 8
TOPK_GROUPS = 4
GROUP_SIZE = N_EXPERTS // N_GROUPS
EXPERT_DIM = 256
SHARED_DIM = 256
ROUTED_SCALE = 2.5
DEPTH = 1
ALPHA = (2.0 * DEPTH) ** 0.25
LN_EPS = 1e-5
RMS_EPS = 1e-6

LANES = 128
TOKEN_TILE = 256
INPROJ_TILE = 256
HG_CHUNK = 128
HG_SAFE_LOGDECAY = 80.0
EXPERT_BLOCK = 512
SC_CORES = 2
SC_SUBCORES = 16
SC_WINDOW = 32
VMEM_LIMIT = 56 * 1024 * 1024

_O_U, _O_V, _O_Q, _O_FF, _O_FB, _O_I, _O_G, _O_GA, _O_GB = (
    0, 512, 1024, 2048, 3072, 4096, 5120, 6144, 7168)


def _layer_norm(x, g, b):
    mu = jnp.mean(x, axis=-1, keepdims=True)
    xc = x - mu
    var = jnp.mean(xc * xc, axis=-1, keepdims=True)
    return xc * lax.rsqrt(var + LN_EPS) * g + b


def _bdot(a, b):
    return jnp.dot(a.astype(BF16), b.astype(BF16), preferred_element_type=F32)


def _pack_rows(x):
    w = x.shape[1] // 2
    lo = lax.bitcast_convert_type(x[:, :w].astype(BF16).astype(F32), jnp.uint32)
    hi = lax.bitcast_convert_type(x[:, w:].astype(BF16).astype(F32), jnp.uint32)
    return hi | (lo >> 16)


def _unpack_rows(p):
    lo = lax.bitcast_convert_type(p << 16, F32)
    hi = lax.bitcast_convert_type(p & jnp.uint32(0xFFFF0000), F32)
    return lo, hi


def _const_spec(shape):
    nd = len(shape)
    return pl.BlockSpec(shape, lambda *_: (0,) * nd, pipeline_mode=pl.Buffered(1))


def _inproj_kernel(x_ref, lng_ref, lnb_ref, win_ref, alng_ref, alnb_ref, wsp_ref, sbf_ref, lb_ref,
                   wpa_ref, ag_ref, q_ref, gf_ref, gb_ref, iv_ref, so_ref, sgb_ref, gmin_ref):
    tm = x_ref.shape[0]
    xb = _layer_norm(x_ref[...], lng_ref[...], lnb_ref[...]).astype(BF16)

    def sec(lo, width):
        return jnp.dot(xb, win_ref[:, lo:lo + width], preferred_element_type=F32)

    u = jax.nn.gelu(sec(_O_U, A_WIDTH))
    v = _layer_norm(jax.nn.gelu(sec(_O_V, A_WIDTH)), alng_ref[...], alnb_ref[...]).astype(BF16)
    lane = lax.broadcasted_iota(jnp.int32, (A_CHUNK, LANES), 1)
    left = lane < (A_WIDTH // A_GROUPS)
    zero = jnp.zeros((A_CHUNK, LANES), BF16)
    chunks = []
    for c in range(tm // A_CHUNK):
        vc = v[c * A_CHUNK:(c + 1) * A_CHUNK]
        cols = []
        for p in range(A_GROUPS // 2):
            vp = vc[:, p * LANES:(p + 1) * LANES]
            rhs = jnp.concatenate([jnp.where(left, vp, zero), jnp.where(left, zero, vp)], axis=0)
            cols.append(jnp.dot(wsp_ref[p], rhs, preferred_element_type=F32))
        chunks.append(jnp.concatenate(cols, axis=1) + sbf_ref[...])
    mixed = jnp.concatenate(chunks, axis=0)
    a = _bdot(u * mixed, wpa_ref[...])
    ag_ref[...] = (jax.nn.sigmoid(sec(_O_GA, D_MODEL)) * a).astype(BF16)

    q_ref[...] = jax.nn.silu(sec(_O_Q, HG_WIDTH)).astype(BF16)
    mins = []
    for d, (off, g_ref) in enumerate(((_O_FF, gf_ref), (_O_FB, gb_ref))):
        lb = lb_ref[d:d + 1, :]
        f = lb + (1.0 - lb) * jax.nn.sigmoid(sec(off, HG_WIDTH))
        g = jnp.log(f)
        g_ref[...] = g
        half = jnp.sum(g.reshape(tm // (HG_CHUNK // 2), HG_CHUNK // 2, HG_WIDTH), axis=1)
        mins.append(jnp.min(half, axis=-1, keepdims=True))
    gmin_ref[0] = jnp.broadcast_to(jnp.concatenate(mins, axis=0), gmin_ref.shape[1:])
    iv_ref[...] = sec(_O_I, HG_WIDTH).astype(BF16)
    so_ref[...] = jax.nn.silu(sec(_O_G, HG_WIDTH)).astype(BF16)
    sgb_ref[...] = jax.nn.sigmoid(sec(_O_GB, D_MODEL)).astype(BF16)


def _inproj(x, p):
    T = x.shape[0]
    tm = INPROJ_TILE
    nt = T // tm
    nh = 2 * tm // (HG_CHUNK // 2)
    tok = lambda w: pl.BlockSpec((tm, w), lambda i: (i, 0))
    outs = (
        jax.ShapeDtypeStruct((T, D_MODEL), BF16),
        jax.ShapeDtypeStruct((T, HG_WIDTH), BF16),
        jax.ShapeDtypeStruct((T, HG_WIDTH), F32),
        jax.ShapeDtypeStruct((T, HG_WIDTH), F32),
        jax.ShapeDtypeStruct((T, HG_WIDTH), BF16),
        jax.ShapeDtypeStruct((T, HG_WIDTH), BF16),
        jax.ShapeDtypeStruct((T, D_MODEL), BF16),
        jax.ShapeDtypeStruct((nt, nh, LANES), F32),
    )
    return pl.pallas_call(
        _inproj_kernel,
        grid=(nt,),
        in_specs=[
            tok(D_MODEL),
            _const_spec((1, D_MODEL)), _const_spec((1, D_MODEL)),
            _const_spec((D_MODEL, N_IN)),
            _const_spec((1, A_WIDTH)), _const_spec((1, A_WIDTH)),
            _const_spec((A_GROUPS // 2, A_CHUNK, 2 * A_CHUNK)),
            _const_spec((A_CHUNK, A_WIDTH)),
            _const_spec((2, HG_WIDTH)),
            _const_spec((A_WIDTH, D_MODEL)),
        ],
        out_specs=(tok(D_MODEL), tok(HG_WIDTH), tok(HG_WIDTH), tok(HG_WIDTH), tok(HG_WIDTH),
                   tok(HG_WIDTH), tok(D_MODEL), pl.BlockSpec((1, nh, LANES), lambda i: (i, 0, 0))),
        out_shape=outs,
        compiler_params=pltpu.CompilerParams(
            dimension_semantics=("parallel",), vmem_limit_bytes=VMEM_LIMIT),
        name="inproj",
    )(x, p["ln_in_g"], p["ln_in_b"], p["w_in"], p["a_ln_g"], p["a_ln_b"], p["wsp"], p["sbf"],
      p["lb"], p["w_pa"])


_NT = (((1,), (1,)), ((), ()))
_TN = (((0,), (0,)), ((), ()))


def _hgrn_direction(q_ref, g_ref, v_ref, st_ref, o_ref, tri, mask, fwd):
    C = q_ref.shape[0]
    g = g_ref[...]
    ghi = g.astype(BF16)
    glo = (g - ghi.astype(F32)).astype(BF16)
    b = jnp.dot(jnp.concatenate([tri, tri], axis=1), jnp.concatenate([ghi, glo], axis=0),
                preferred_element_type=F32)
    mid = C // 2 - 1 if fwd else C // 2
    end = C - 1 if fwd else 0
    r = b[mid:mid + 1, :]
    b_end = b[end:end + 1, :]
    qt = q_ref[...].astype(F32) * jnp.exp(b - r)
    kt = (1.0 - jnp.exp(g)) * jnp.exp(r - b)
    qtb = qt.astype(BF16)
    ktb = kt.astype(BF16)
    qhb = (qt * jnp.exp(r)).astype(BF16)
    khb = (kt * jnp.exp(b_end - r)).astype(BF16)
    decay = jnp.exp(b_end)
    v32 = v_ref[...].astype(F32)
    heads = [slice(h * HG_DK, (h + 1) * HG_DK) for h in range(HG_HEADS)]
    zero = jnp.zeros((C, HG_DK), BF16)
    mask2 = jnp.concatenate([mask, mask], axis=1)
    scores = []
    for p in range(HG_HEADS // 2):
        k1, k2 = ktb[:, heads[2 * p]], ktb[:, heads[2 * p + 1]]
        kk = jnp.concatenate([jnp.concatenate([k1, zero], axis=1),
                              jnp.concatenate([zero, k2], axis=1)], axis=0)
        s2 = lax.dot_general(qtb[:, 2 * p * HG_DK:(2 * p + 2) * HG_DK], kk, _NT,
                             preferred_element_type=F32)
        s2 = jnp.where(mask2, s2, 0.0).astype(BF16)
        scores += [s2[:, :C], s2[:, C:]]
    for h, sl in enumerate(heads):
        st = st_ref[h]
        vt = v32[:, sl].T.astype(BF16)
        o_ref[:, sl] = lax.dot_general(
            jnp.concatenate([scores[h], qhb[:, sl]], axis=1),
            jnp.concatenate([vt, st.astype(BF16)], axis=1), _NT, preferred_element_type=F32)
        st_ref[h] = st * decay[:, sl] + jnp.dot(vt, khb[:, sl], preferred_element_type=F32)


def _hgrn_direction_stepwise(q_ref, g_ref, v_ref, st_ref, o_ref, q32_ref, v32_ref, fwd):
    C = q_ref.shape[0]
    q32_ref[...] = q_ref[...].astype(F32)
    v32_ref[...] = v_ref[...].astype(F32)
    sub = 8
    rows = lax.broadcasted_iota(jnp.int32, (sub, HG_DK), 0)

    def group(i, carry):
        base = pl.multiple_of((i if fwd else C // sub - 1 - i) * sub, sub)
        f = jnp.exp(g_ref[pl.ds(base, sub), :])
        k = 1.0 - f
        q = q32_ref[pl.ds(base, sub), :]
        v = v32_ref[pl.ds(base, sub), :]
        for h in range(HG_HEADS):
            sl = slice(h * HG_DK, (h + 1) * HG_DK)
            st = st_ref[h]
            out = jnp.zeros((sub, HG_DK), F32)
            for r in (range(sub) if fwd else range(sub - 1, -1, -1)):
                v_t = jnp.where(rows == 0, v[r:r + 1, sl], 0.0).astype(BF16)
                k_t = jnp.broadcast_to(k[r:r + 1, sl], (sub, HG_DK)).astype(BF16)
                st = st * f[r:r + 1, sl] + lax.dot_general(v_t, k_t, _TN, preferred_element_type=F32)
                q_t = jnp.broadcast_to(q[r:r + 1, sl], (sub, HG_DK)).astype(BF16)
                o_t = lax.dot_general(q_t, st.astype(BF16), _NT, preferred_element_type=F32)
                out = jnp.where(rows == r, o_t, out)
            st_ref[h] = st
            o_ref[pl.ds(base, sub), sl] = out
        return carry

    lax.fori_loop(0, C // sub, group, 0)


def _hgrn_kernel(safe_ref, qf_ref, qb_ref, gf_ref, gb_ref, vf_ref, vb_ref, tril_ref, triu_ref,
                 of_ref, ob_ref, sf_ref, sb_ref, q32_ref, v32_ref):
    b, j = pl.program_id(0), pl.program_id(1)
    nc = pl.num_programs(1)

    @pl.when(j == 0)
    def _():
        sf_ref[...] = jnp.zeros_like(sf_ref)
        sb_ref[...] = jnp.zeros_like(sb_ref)

    C = qf_ref.shape[0]
    row = lax.broadcasted_iota(jnp.int32, (C, C), 0)
    col = lax.broadcasted_iota(jnp.int32, (C, C), 1)
    safe_f = safe_ref[0, b * nc + j] != 0
    safe_b = safe_ref[1, b * nc + nc - 1 - j] != 0

    def forward(stepwise):
        if stepwise:
            _hgrn_direction_stepwise(qf_ref, gf_ref, vf_ref, sf_ref, of_ref, q32_ref, v32_ref, True)
        else:
            _hgrn_direction(qf_ref, gf_ref, vf_ref, sf_ref, of_ref, tril_ref[...], row >= col, True)

    def backward(stepwise):
        if stepwise:
            _hgrn_direction_stepwise(qb_ref, gb_ref, vb_ref, sb_ref, ob_ref, q32_ref, v32_ref, False)
        else:
            _hgrn_direction(qb_ref, gb_ref, vb_ref, sb_ref, ob_ref, triu_ref[...], row <= col, False)

    @pl.when(safe_f & safe_b)
    def _():
        forward(False)
        backward(False)

    @pl.when(jnp.logical_not(safe_f & safe_b))
    def _():
        @pl.when(safe_f)
        def _():
            forward(False)

        @pl.when(jnp.logical_not(safe_f))
        def _():
            forward(True)

        @pl.when(safe_b)
        def _():
            backward(False)

        @pl.when(jnp.logical_not(safe_b))
        def _():
            backward(True)


def _hgrn(q, gf, gb, iv, gmin, batch):
    T = q.shape[0]
    C = HG_CHUNK
    nc = T // batch // C
    nt, nh, _ = gmin.shape
    halves = gmin[:, :, 0].reshape(nt, 2, nh // 4, 2)
    safe = (jnp.min(halves, axis=-1) > -HG_SAFE_LOGDECAY).astype(jnp.int32)
    safe = safe.transpose(1, 0, 2).reshape(2, T // C)
    fwd = pl.BlockSpec((C, HG_WIDTH), lambda b, j, s: (b * nc + j, 0))
    bwd = pl.BlockSpec((C, HG_WIDTH), lambda b, j, s: (b * nc + nc - 1 - j, 0))
    const = lambda shape: pl.BlockSpec(shape, lambda b, j, s: (0,) * len(shape),
                                       pipeline_mode=pl.Buffered(1))
    row = lax.broadcasted_iota(jnp.int32, (C, C), 0)
    col = lax.broadcasted_iota(jnp.int32, (C, C), 1)
    tril = (row >= col).astype(BF16)
    triu = (row <= col).astype(BF16)
    grid_spec = pltpu.PrefetchScalarGridSpec(
        num_scalar_prefetch=1,
        grid=(batch, nc),
        in_specs=[fwd, bwd, fwd, bwd, fwd, bwd, const((C, C)), const((C, C))],
        out_specs=(fwd, bwd),
        scratch_shapes=[pltpu.VMEM((HG_HEADS, HG_DK, HG_DK), F32),
                        pltpu.VMEM((HG_HEADS, HG_DK, HG_DK), F32),
                        pltpu.VMEM((C, HG_WIDTH), F32), pltpu.VMEM((C, HG_WIDTH), F32)],
    )
    return pl.pallas_call(
        _hgrn_kernel,
        grid_spec=grid_spec,
        out_shape=(jax.ShapeDtypeStruct((T, HG_WIDTH), F32), jax.ShapeDtypeStruct((T, HG_WIDTH), F32)),
        compiler_params=pltpu.CompilerParams(
            dimension_semantics=("parallel", "arbitrary"), vmem_limit_bytes=VMEM_LIMIT),
        name="hgrn",
    )(safe, q, q, gf, gb, iv, iv, tril, triu)


def _mix_kernel(x_ref, lng_ref, lnb_ref, of_ref, ob_ref, so_ref, sgb_ref, ag_ref, ng_ref, wpb_ref,
                wo_ref, l1g_ref, l1b_ref, wr_ref, rb_ref, tri_ref, x1_ref, x1p_ref, route_ref,
                cnt_ref, carry_ref):
    tm = x_ref.shape[0]

    @pl.when(pl.program_id(0) == 0)
    def _():
        carry_ref[...] = jnp.zeros_like(carry_ref)

    o = of_ref[...] + ob_ref[...]
    heads = []
    for h in range(HG_HEADS):
        oh = o[:, h * HG_DK:(h + 1) * HG_DK]
        heads.append(oh * lax.rsqrt(jnp.mean(oh * oh, axis=-1, keepdims=True) + RMS_EPS))
    rn = jnp.concatenate(heads, axis=1) * ng_ref[...] * so_ref[...].astype(F32)
    r = _bdot(rn, wpb_ref[...])
    mixed = ag_ref[...].astype(F32) + sgb_ref[...].astype(F32) * r
    y = _bdot(mixed, wo_ref[...])
    xn = _layer_norm(x_ref[...], lng_ref[...], lnb_ref[...])
    x1 = _layer_norm(ALPHA * xn + y, l1g_ref[...], l1b_ref[...])
    x1_ref[...] = x1
    x1p_ref[...] = _pack_rows(x1)

    neg = jnp.float32(-jnp.inf)
    reps = tm // LANES
    scores = jax.nn.sigmoid(lax.dot_general(wr_ref[...], x1.astype(BF16), _NT,
                                            preferred_element_type=F32))
    biased = (scores + jnp.concatenate([rb_ref[...]] * reps, axis=1)).reshape(
        N_GROUPS, GROUP_SIZE, tm)
    sub = lax.broadcasted_iota(jnp.int32, biased.shape, 1).astype(F32)
    m1 = jnp.max(biased, axis=1, keepdims=True)
    first = jnp.min(jnp.where(biased == m1, sub, float(GROUP_SIZE)), axis=1, keepdims=True)
    m2 = jnp.max(jnp.where(sub == first, neg, biased), axis=1, keepdims=True)
    gs = (m1 + m2).reshape(N_GROUPS, tm)
    grp = lax.broadcasted_iota(jnp.int32, (N_GROUPS, tm), 0)
    ahead = jnp.zeros((N_GROUPS, tm), F32)
    for d in range(1, N_GROUPS):
        other = pltpu.roll(gs, d, 0)
        tie = jnp.where(grp >= d, 1.0, 0.0)
        ahead = ahead + jnp.where(other > gs, 1.0, jnp.where(other == gs, tie, 0.0))
    keep = (ahead < TOPK_GROUPS).reshape(N_GROUPS, 1, tm)
    allowed = jnp.where(keep, biased, neg).reshape(N_EXPERTS, tm)
    row = lax.broadcasted_iota(jnp.int32, (N_EXPERTS, tm), 0).astype(F32)
    sel = jnp.zeros((N_EXPERTS, tm), F32)
    picks = []
    for _ in range(TOP_K):
        m = jnp.max(allowed, axis=0, keepdims=True)
        first = jnp.min(jnp.where(allowed == m, row, float(N_EXPERTS)), axis=0, keepdims=True)
        hit = row == first
        picks.append((first, hit, jnp.sum(jnp.where(hit, scores, 0.0), axis=0, keepdims=True)))
        sel = jnp.where(hit, 1.0, sel)
        allowed = jnp.where(hit, neg, allowed)
    wsum = picks[0][2]
    for pk in picks[1:]:
        wsum = wsum + pk[2]
    selb = sel.astype(BF16)
    carry = carry_ref[...]
    before = (jnp.dot(selb, tri_ref[...], preferred_element_type=F32)
              + jnp.concatenate([carry] * reps, axis=1))
    total = carry + jnp.dot(selb, jnp.ones((tm, LANES), BF16), preferred_element_type=F32)
    carry_ref[...] = total
    cnt_ref[...] = total
    blank = [jnp.zeros((1, tm), F32)] * (8 - TOP_K)
    route_ref[...] = jnp.concatenate(
        [pk[0] for pk in picks] + blank
        + [jnp.sum(jnp.where(pk[1], before, 0.0), axis=0, keepdims=True) for pk in picks] + blank
        + [pk[2] / wsum * ROUTED_SCALE for pk in picks] + blank, axis=0)


def _mix(x, of, ob, so, sgb, ag, p):
    T = x.shape[0]
    tm = TOKEN_TILE
    tok = lambda w: pl.BlockSpec((tm, w), lambda i: (i, 0))
    row = lax.broadcasted_iota(jnp.int32, (tm, tm), 0)
    col = lax.broadcasted_iota(jnp.int32, (tm, tm), 1)
    tri = (row < col).astype(BF16)
    return pl.pallas_call(
        _mix_kernel,
        grid=(T // tm,),
        in_specs=[
            tok(D_MODEL), _const_spec((1, D_MODEL)), _const_spec((1, D_MODEL)),
            tok(HG_WIDTH), tok(HG_WIDTH), tok(HG_WIDTH), tok(D_MODEL), tok(D_MODEL),
            _const_spec((1, HG_WIDTH)),
            _const_spec((HG_WIDTH, D_MODEL)), _const_spec((D_MODEL, D_MODEL)),
            _const_spec((1, D_MODEL)), _const_spec((1, D_MODEL)),
            _const_spec((N_EXPERTS, D_MODEL)), _const_spec((N_EXPERTS, LANES)),
            _const_spec((tm, tm)),
        ],
        out_specs=(tok(D_MODEL), tok(D_MODEL // 2), pl.BlockSpec((24, tm), lambda i: (0, i)),
                   pl.BlockSpec((N_EXPERTS, LANES), lambda i: (0, 0))),
        out_shape=(jax.ShapeDtypeStruct((T, D_MODEL), F32),
                   jax.ShapeDtypeStruct((T, D_MODEL // 2), jnp.uint32),
                   jax.ShapeDtypeStruct((24, T), F32),
                   jax.ShapeDtypeStruct((N_EXPERTS, LANES), F32)),
        scratch_shapes=[pltpu.VMEM((N_EXPERTS, LANES), F32)],
        compiler_params=pltpu.CompilerParams(
            dimension_semantics=("arbitrary",), vmem_limit_bytes=VMEM_LIMIT),
        name="mix",
    )(x, p["ln_in_g"], p["ln_in_b"], of, ob, so, sgb, ag, p["hg_norm_g"], p["w_pb"], p["w_o"],
      p["ln1_g"], p["ln1_b"], p["w_router"], p["router_bias"], tri)


def _sc_mesh():
    return plsc.VectorSubcoreMesh(core_axis_name="c", subcore_axis_name="s",
                                  num_cores=SC_CORES, num_subcores=SC_SUBCORES)


def _sc_worker():
    return lax.axis_index("s") * SC_CORES + lax.axis_index("c")


def _sc_dispatch(x1p, destw, n_rows):
    T, w = x1p.shape
    n_win, _, W = destw.shape
    per_worker = n_win // (SC_CORES * SC_SUBCORES)

    def body(x_hbm, d_hbm, o_hbm, rows_v, idx_v, sem):
        first = _sc_worker() * per_worker

        @pl.loop(0, per_worker)
        def _(j):
            win = first + j
            pltpu.sync_copy(x_hbm.at[pl.ds(win * W, W)], rows_v)
            pltpu.sync_copy(d_hbm.at[win], idx_v)
            copies = [pltpu.async_copy(rows_v, o_hbm.at[idx_v.at[k]], sem) for k in range(TOP_K)]
            for c in copies:
                c.wait()

    return pl.kernel(
        body,
        out_type=jax.ShapeDtypeStruct((n_rows, w), jnp.uint32),
        mesh=_sc_mesh(),
        scratch_types=[pltpu.VMEM((W, w), jnp.uint32), pltpu.VMEM((TOP_K, W), jnp.int32),
                       pltpu.SemaphoreType.DMA],
        name="sc_dispatch",
    )(x1p, destw)


def _sc_combine(ys, destw):
    n_win, _, W = destw.shape
    w = ys.shape[1]
    per_worker = n_win // (SC_CORES * SC_SUBCORES)

    def body(y_hbm, d_hbm, o_hbm, rows_v, idx_v, sem):
        first = _sc_worker() * per_worker

        @pl.loop(0, per_worker)
        def _(j):
            win = first + j
            pltpu.sync_copy(d_hbm.at[win], idx_v)
            gathers = [pltpu.async_copy(y_hbm.at[idx_v.at[k]], rows_v.at[k], sem)
                       for k in range(TOP_K)]
            for c in gathers:
                c.wait()
            writes = [pltpu.async_copy(rows_v.at[k], o_hbm.at[k, pl.ds(win * W, W)], sem)
                      for k in range(TOP_K)]
            for c in writes:
                c.wait()

    return pl.kernel(
        body,
        out_type=jax.ShapeDtypeStruct((TOP_K, n_win * W, w), jnp.uint32),
        mesh=_sc_mesh(),
        scratch_types=[pltpu.VMEM((TOP_K, W, w), jnp.uint32), pltpu.VMEM((TOP_K, W), jnp.int32),
                       pltpu.SemaphoreType.DMA],
        name="sc_combine",
    )(ys, destw)


def _experts_kernel(be_ref, nv_ref, xs_ref, wg_ref, wu_ref, wd_ref, ys_ref, wgb_ref, wub_ref, wdb_ref):
    i = pl.program_id(0)
    n_valid = nv_ref[i]
    mb, w = xs_ref.shape

    @pl.when((i == 0) | (be_ref[i] != be_ref[jnp.maximum(i, 1) - 1]))
    def _():
        wgb_ref[...] = wg_ref[0].astype(BF16)
        wub_ref[...] = wu_ref[0].astype(BF16)
        wdb_ref[...] = wd_ref[0].astype(BF16)

    @pl.when(n_valid > 0)
    def _():
        keep = lax.broadcasted_iota(jnp.int32, (mb, w), 0) < n_valid
        lo, hi = _unpack_rows(jnp.where(keep, xs_ref[...], jnp.uint32(0)))
        lo, hi = lo.astype(BF16), hi.astype(BF16)

        def proj(w_ref):
            return (jnp.dot(lo, w_ref[:w, :], preferred_element_type=F32)
                    + jnp.dot(hi, w_ref[w:, :], preferred_element_type=F32))

        hb = jax.nn.silu(proj(wgb_ref)) * proj(wub_ref)
        ys_ref[...] = _pack_rows(jnp.dot(hb.astype(BF16), wdb_ref[...], preferred_element_type=F32))

    @pl.when(n_valid <= 0)
    def _():
        ys_ref[...] = jnp.zeros_like(ys_ref)


def _experts(xs, blk_exp, n_valid, p):
    n_rows, w = xs.shape
    mb = EXPERT_BLOCK
    grid_spec = pltpu.PrefetchScalarGridSpec(
        num_scalar_prefetch=2,
        grid=(n_rows // mb,),
        in_specs=[
            pl.BlockSpec((mb, w), lambda i, be, nv: (i, 0)),
            pl.BlockSpec((1, D_MODEL, EXPERT_DIM), lambda i, be, nv: (be[i], 0, 0)),
            pl.BlockSpec((1, D_MODEL, EXPERT_DIM), lambda i, be, nv: (be[i], 0, 0)),
            pl.BlockSpec((1, EXPERT_DIM, D_MODEL), lambda i, be, nv: (be[i], 0, 0)),
        ],
        out_specs=pl.BlockSpec((mb, w), lambda i, be, nv: (i, 0)),
        scratch_shapes=[pltpu.VMEM((D_MODEL, EXPERT_DIM), BF16), pltpu.VMEM((D_MODEL, EXPERT_DIM), BF16),
                        pltpu.VMEM((EXPERT_DIM, D_MODEL), BF16)],
    )
    return pl.pallas_call(
        _experts_kernel,
        grid_spec=grid_spec,
        out_shape=jax.ShapeDtypeStruct((n_rows, w), jnp.uint32),
        compiler_params=pltpu.CompilerParams(
            dimension_semantics=("arbitrary",), vmem_limit_bytes=VMEM_LIMIT),
        name="experts",
    )(blk_exp, n_valid, xs, p["w_e_gate"], p["w_e_up"], p["w_e_down"])


def _final_kernel(x1_ref, gate_ref, yg_ref, wsg_ref, wsu_ref, wsd_ref, l2g_ref, l2b_ref, out_ref):
    x1 = x1_ref[...]
    xb = x1.astype(BF16)
    hs = (jax.nn.silu(jnp.dot(xb, wsg_ref[...], preferred_element_type=F32))
          * jnp.dot(xb, wsu_ref[...], preferred_element_type=F32))
    shared = jnp.dot(hs.astype(BF16), wsd_ref[...], preferred_element_type=F32)
    gate = gate_ref[...]
    lo, hi = None, None
    for k in range(TOP_K):
        wk = gate[:, k:k + 1]
        lk, hk = _unpack_rows(yg_ref[k])
        lo = lk * wk if lo is None else lo + lk * wk
        hi = hk * wk if hi is None else hi + hk * wk
    routed = jnp.concatenate([lo, hi], axis=1)
    out_ref[...] = _layer_norm(ALPHA * x1 + (routed + shared), l2g_ref[...], l2b_ref[...])


def _final(x1, gates, yg, p):
    T = x1.shape[0]
    tm = TOKEN_TILE
    tok = lambda w: pl.BlockSpec((tm, w), lambda i: (i, 0))
    return pl.pallas_call(
        _final_kernel,
        grid=(T // tm,),
        in_specs=[
            tok(D_MODEL), tok(gates.shape[1]),
            pl.BlockSpec((TOP_K, tm, yg.shape[2]), lambda i: (0, i, 0)),
            _const_spec((D_MODEL, SHARED_DIM)), _const_spec((D_MODEL, SHARED_DIM)),
            _const_spec((SHARED_DIM, D_MODEL)),
            _const_spec((1, D_MODEL)), _const_spec((1, D_MODEL)),
        ],
        out_specs=tok(D_MODEL),
        out_shape=jax.ShapeDtypeStruct((T, D_MODEL), F32),
        compiler_params=pltpu.CompilerParams(
            dimension_semantics=("parallel",), vmem_limit_bytes=VMEM_LIMIT),
        name="final",
    )(x1, gates, yg, p["w_sh_gate"], p["w_sh_up"], p["w_sh_down"], p["ln2_g"], p["ln2_b"])


def _routing_layout(route, counts, n_tokens):
    mb = EXPERT_BLOCK
    n_blocks = -(-n_tokens * TOP_K // mb) + N_EXPERTS
    idx = route[0:TOP_K].astype(jnp.int32)
    rank = route[8:8 + TOP_K].astype(jnp.int32)
    counts = counts.astype(jnp.int32)
    padded = (counts + mb - 1) // mb * mb
    pad_end = jnp.cumsum(padded)
    pad_start = pad_end - padded
    experts = jnp.arange(N_EXPERTS, dtype=jnp.int32)
    dest = rank + jnp.sum(jnp.where(idx[:, :, None] == experts, pad_start, 0), axis=-1)
    destw = dest.reshape(TOP_K, n_tokens // SC_WINDOW, SC_WINDOW).transpose(1, 0, 2)
    blk_start = jnp.arange(n_blocks, dtype=jnp.int32) * mb
    blk_exp = jnp.minimum(
        jnp.sum((pad_end[None, :] <= blk_start[:, None]).astype(jnp.int32), axis=1), N_EXPERTS - 1)
    valid_end = jnp.sum(jnp.where(blk_exp[:, None] == experts, pad_start + counts, 0), axis=-1)
    n_valid = jnp.clip(valid_end - blk_start, 0, mb).astype(jnp.int32)
    return destw, blk_exp, n_valid, n_blocks * mb


def _encode(x, p):
    batch, seq, _ = x.shape
    T = batch * seq
    xt = x.reshape(T, D_MODEL)
    ag, q, gf, gb, iv, so, sgb, gmin = _inproj(xt, p)
    of, ob = _hgrn(q, gf, gb, iv, gmin, batch)
    x1, x1p, route, cnt = _mix(xt, of, ob, so, sgb, ag, p)
    destw, blk_exp, n_valid, n_rows = _routing_layout(route, cnt[:, 0], T)
    xs = _sc_dispatch(x1p, destw, n_rows)
    ys = _experts(xs, blk_exp, n_valid, p)
    out = _final(x1, route[16:24].T, _sc_combine(ys, destw), p)
    return out.reshape(batch, seq, D_MODEL)


def _prepare_params(ln_in_g, ln_in_b, w_in, a_ln_g, a_ln_b, a_ws, a_sb, hg_lb_logits, hg_norm_g,
                    w_pa, w_pb, w_o, ln1_g, ln1_b, w_router, router_bias, w_e_gate, w_e_up,
                    w_e_down, w_sh_gate, w_sh_up, w_sh_down, ln2_g, ln2_b):
    l = 0
    row = lambda v: v.reshape(1, -1).astype(F32)
    ws = a_ws[l].astype(BF16)
    wsp = jnp.concatenate([ws[0::2], ws[1::2]], axis=2)
    sbf = jnp.repeat(a_sb[l].astype(F32), A_WIDTH // A_GROUPS, axis=1)
    lb = jnp.cumsum(jax.nn.softmax(hg_lb_logits.astype(F32), axis=1), axis=1)[:, l]
    return dict(
        ln_in_g=row(ln_in_g), ln_in_b=row(ln_in_b), w_in=w_in[l].astype(BF16),
        a_ln_g=row(a_ln_g[l]), a_ln_b=row(a_ln_b[l]), wsp=wsp, sbf=sbf, lb=lb,
        w_pa=w_pa[l].astype(BF16), hg_norm_g=row(hg_norm_g[l]),
        w_pb=w_pb[l].astype(BF16), w_o=w_o[l].astype(BF16),
        ln1_g=row(ln1_g[l]), ln1_b=row(ln1_b[l]),
        w_router=w_router[l].T.astype(BF16),
        router_bias=jnp.broadcast_to(router_bias[l].astype(F32)[:, None], (N_EXPERTS, LANES)),
        w_e_gate=w_e_gate[l], w_e_up=w_e_up[l], w_e_down=w_e_down[l],
        w_sh_gate=w_sh_gate[l].astype(BF16), w_sh_up=w_sh_up[l].astype(BF16),
        w_sh_down=w_sh_down[l].astype(BF16),
        ln2_g=row(ln2_g[l]), ln2_b=row(ln2_b[l]),
    )


def kernel(x_prompt, x_sample, ln_in_g, ln_in_b, w_in, a_ln_g, a_ln_b, a_ws, a_sb, hg_lb_logits,
           hg_norm_g, w_pa, w_pb, w_o, ln1_g, ln1_b, w_router, router_bias, w_e_gate, w_e_up,
           w_e_down, w_sh_gate, w_sh_up, w_sh_down, ln2_g, ln2_b):
    p = _prepare_params(ln_in_g, ln_in_b, w_in, a_ln_g, a_ln_b, a_ws, a_sb, hg_lb_logits, hg_norm_g,
                        w_pa, w_pb, w_o, ln1_g, ln1_b, w_router, router_bias, w_e_gate, w_e_up,
                        w_e_down, w_sh_gate, w_sh_up, w_sh_down, ln2_g, ln2_b)
    return _encode(x_prompt, p), _encode(x_sample, p)
```

```python
import functools

import jax
import jax.numpy as jnp
from jax import lax
from jax.experimental import pallas as pl
from jax.experimental.pallas import tpu as pltpu
from jax.experimental.pallas import tpu_sc as plsc

F32 = jnp.float32
BF16 = jnp.bfloat16

D_MODEL = 1024
A_GROUPS = 8
A_WIDTH = 512
A_CHUNK = 128
HG_HEADS = 8
HG_DK = 128
HG_WIDTH = HG_HEADS * HG_DK
N_IN = 2 * A_WIDTH + 5 * HG_WIDTH + 2 * D_MODEL
N_EXPERTS = 64
TOP_K = 6
N_GROUPS = 8
TOPK_GROUPS = 4
GROUP_SIZE = N_EXPERTS // N_GROUPS
EXPERT_DIM = 256
SHARED_DIM = 256
ROUTED_SCALE = 2.5
DEPTH = 1
ALPHA = (2.0 * DEPTH) ** 0.25
LN_EPS = 1e-5
RMS_EPS = 1e-6

LANES = 128
TOKEN_TILE = 256
INPROJ_TILE = 256
HG_CHUNK = 128
HG_SAFE_LOGDECAY = 80.0
EXPERT_BLOCK = 512
EXPERT_BLOCK_LARGE = 1024
SC_CORES = 2
SC_SUBCORES = 16
SC_WINDOW = 32
VMEM_LIMIT = 56 * 1024 * 1024

_O_U, _O_V, _O_Q, _O_FF, _O_FB, _O_I, _O_G, _O_GA, _O_GB = (
    0, 512, 1024, 2048, 3072, 4096, 5120, 6144, 7168)


def _layer_norm(x, g, b):
    mu = jnp.mean(x, axis=-1, keepdims=True)
    xc = x - mu
    var = jnp.mean(xc * xc, axis=-1, keepdims=True)
    return xc * lax.rsqrt(var + LN_EPS) * g + b


def _bdot(a, b):
    return jnp.dot(a.astype(BF16), b.astype(BF16), preferred_element_type=F32)


def _pack_rows(x):
    w = x.shape[1] // 2
    lo = lax.bitcast_convert_type(x[:, :w].astype(BF16).astype(F32), jnp.uint32)
    hi = lax.bitcast_convert_type(x[:, w:].astype(BF16).astype(F32), jnp.uint32)
    return hi | (lo >> 16)


def _unpack_rows(p):
    lo = lax.bitcast_convert_type(p << 16, F32)
    hi = lax.bitcast_convert_type(p & jnp.uint32(0xFFFF0000), F32)
    return lo, hi


def _const_spec(shape):
    nd = len(shape)
    return pl.BlockSpec(shape, lambda *_: (0,) * nd, pipeline_mode=pl.Buffered(1))


def _inproj_kernel(x_ref, lng_ref, lnb_ref, win_ref, alng_ref, alnb_ref, wsp_ref, sbf_ref, lb_ref,
                   wpa_ref, ag_ref, q_ref, gf_ref, gb_ref, iv_ref, so_ref, sgb_ref, gmin_ref):
    tm = x_ref.shape[0]
    xb = _layer_norm(x_ref[...], lng_ref[...], lnb_ref[...]).astype(BF16)

    def sec(lo, width):
        return jnp.dot(xb, win_ref[:, lo:lo + width], preferred_element_type=F32)

    u = jax.nn.gelu(sec(_O_U, A_WIDTH))
    v = _layer_norm(jax.nn.gelu(sec(_O_V, A_WIDTH)), alng_ref[...], alnb_ref[...]).astype(BF16)
    lane = lax.broadcasted_iota(jnp.int32, (A_CHUNK, LANES), 1)
    left = lane < (A_WIDTH // A_GROUPS)
    zero = jnp.zeros((A_CHUNK, LANES), BF16)
    chunks = []
    for c in range(tm // A_CHUNK):
        vc = v[c * A_CHUNK:(c + 1) * A_CHUNK]
        cols = []
        for p in range(A_GROUPS // 2):
            vp = vc[:, p * LANES:(p + 1) * LANES]
            rhs = jnp.concatenate([jnp.where(left, vp, zero), jnp.where(left, zero, vp)], axis=0)
            cols.append(jnp.dot(wsp_ref[p], rhs, preferred_element_type=F32))
        chunks.append(jnp.concatenate(cols, axis=1) + sbf_ref[...])
    mixed = jnp.concatenate(chunks, axis=0)
    a = _bdot(u * mixed, wpa_ref[...])
    ag_ref[...] = (jax.nn.sigmoid(sec(_O_GA, D_MODEL)) * a).astype(BF16)

    q_ref[...] = jax.nn.silu(sec(_O_Q, HG_WIDTH)).astype(BF16)
    mins = []
    for d, (off, g_ref) in enumerate(((_O_FF, gf_ref), (_O_FB, gb_ref))):
        lb = lb_ref[d:d + 1, :]
        f = lb + (1.0 - lb) * jax.nn.sigmoid(sec(off, HG_WIDTH))
        g = jnp.log(f)
        g_ref[...] = g
        half = jnp.sum(g.reshape(tm // (HG_CHUNK // 2), HG_CHUNK // 2, HG_WIDTH), axis=1)
        mins.append(jnp.min(half, axis=-1, keepdims=True))
    gmin_ref[0] = jnp.broadcast_to(jnp.concatenate(mins, axis=0), gmin_ref.shape[1:])
    iv_ref[...] = sec(_O_I, HG_WIDTH).astype(BF16)
    so_ref[...] = jax.nn.silu(sec(_O_G, HG_WIDTH)).astype(BF16)
    sgb_ref[...] = jax.nn.sigmoid(sec(_O_GB, D_MODEL)).astype(BF16)


def _inproj(x, p):
    T = x.shape[0]
    tm = INPROJ_TILE
    nt = T // tm
    nh = 2 * tm // (HG_CHUNK // 2)
    tok = lambda w: pl.BlockSpec((tm, w), lambda i: (i, 0))
    outs = (
        jax.ShapeDtypeStruct((T, D_MODEL), BF16),
        jax.ShapeDtypeStruct((T, HG_WIDTH), BF16),
        jax.ShapeDtypeStruct((T, HG_WIDTH), F32),
        jax.ShapeDtypeStruct((T, HG_WIDTH), F32),
        jax.ShapeDtypeStruct((T, HG_WIDTH), BF16),
        jax.ShapeDtypeStruct((T, HG_WIDTH), BF16),
        jax.ShapeDtypeStruct((T, D_MODEL), BF16),
        jax.ShapeDtypeStruct((nt, nh, LANES), F32),
    )
    return pl.pallas_call(
        _inproj_kernel,
        grid=(nt,),
        in_specs=[
            tok(D_MODEL),
            _const_spec((1, D_MODEL)), _const_spec((1, D_MODEL)),
            _const_spec((D_MODEL, N_IN)),
            _const_spec((1, A_WIDTH)), _const_spec((1, A_WIDTH)),
            _const_spec((A_GROUPS // 2, A_CHUNK, 2 * A_CHUNK)),
            _const_spec((A_CHUNK, A_WIDTH)),
            _const_spec((2, HG_WIDTH)),
            _const_spec((A_WIDTH, D_MODEL)),
        ],
        out_specs=(tok(D_MODEL), tok(HG_WIDTH), tok(HG_WIDTH), tok(HG_WIDTH), tok(HG_WIDTH),
                   tok(HG_WIDTH), tok(D_MODEL), pl.BlockSpec((1, nh, LANES), lambda i: (i, 0, 0))),
        out_shape=outs,
        compiler_params=pltpu.CompilerParams(
            dimension_semantics=("parallel",), vmem_limit_bytes=VMEM_LIMIT),
        name="inproj",
    )(x, p["ln_in_g"], p["ln_in_b"], p["w_in"], p["a_ln_g"], p["a_ln_b"], p["wsp"], p["sbf"],
      p["lb"], p["w_pa"])


_NT = (((1,), (1,)), ((), ()))
_TN = (((0,), (0,)), ((), ()))


def _hgrn_direction(q_ref, g_ref, v_ref, st_ref, o_ref, tri, mask, fwd):
    C = q_ref.shape[0]
    g = g_ref[...]
    ghi = g.astype(BF16)
    glo = (g - ghi.astype(F32)).astype(BF16)
    b = jnp.dot(jnp.concatenate([tri, tri], axis=1), jnp.concatenate([ghi, glo], axis=0),
                preferred_element_type=F32)
    mid = C // 2 - 1 if fwd else C // 2
    end = C - 1 if fwd else 0
    r = b[mid:mid + 1, :]
    b_end = b[end:end + 1, :]
    qt = q_ref[...].astype(F32) * jnp.exp(b - r)
    kt = (1.0 - jnp.exp(g)) * jnp.exp(r - b)
    qtb = qt.astype(BF16)
    ktb = kt.astype(BF16)
    qhb = (qt * jnp.exp(r)).astype(BF16)
    khb = (kt * jnp.exp(b_end - r)).astype(BF16)
    decay = jnp.exp(b_end)
    v32 = v_ref[...].astype(F32)
    heads = [slice(h * HG_DK, (h + 1) * HG_DK) for h in range(HG_HEADS)]
    zero = jnp.zeros((C, HG_DK), BF16)
    mask2 = jnp.concatenate([mask, mask], axis=1)
    scores = []
    for p in range(HG_HEADS // 2):
        k1, k2 = ktb[:, heads[2 * p]], ktb[:, heads[2 * p + 1]]
        kk = jnp.concatenate([jnp.concatenate([k1, zero], axis=1),
                              jnp.concatenate([zero, k2], axis=1)], axis=0)
        s2 = lax.dot_general(qtb[:, 2 * p * HG_DK:(2 * p + 2) * HG_DK], kk, _NT,
                             preferred_element_type=F32)
        s2 = jnp.where(mask2, s2, 0.0).astype(BF16)
        scores += [s2[:, :C], s2[:, C:]]
    for h, sl in enumerate(heads):
        st = st_ref[h]
        vt = v32[:, sl].T.astype(BF16)
        o_ref[:, sl] = lax.dot_general(
            jnp.concatenate([scores[h], qhb[:, sl]], axis=1),
            jnp.concatenate([vt, st.astype(BF16)], axis=1), _NT,
            preferred_element_type=F32).astype(o_ref.dtype)
        st_ref[h] = st * decay[:, sl] + jnp.dot(vt, khb[:, sl], preferred_element_type=F32)


def _hgrn_direction_stepwise(q_ref, g_ref, v_ref, st_ref, o_ref, q32_ref, v32_ref, o32_ref, fwd):
    C = q_ref.shape[0]
    q32_ref[...] = q_ref[...].astype(F32)
    v32_ref[...] = v_ref[...].astype(F32)
    sub = 8
    rows = lax.broadcasted_iota(jnp.int32, (sub, HG_DK), 0)

    def group(i, carry):
        base = pl.multiple_of((i if fwd else C // sub - 1 - i) * sub, sub)
        f = jnp.exp(g_ref[pl.ds(base, sub), :])
        k = 1.0 - f
        q = q32_ref[pl.ds(base, sub), :]
        v = v32_ref[pl.ds(base, sub), :]
        for h in range(HG_HEADS):
            sl = slice(h * HG_DK, (h + 1) * HG_DK)
            st = st_ref[h]
            out = jnp.zeros((sub, HG_DK), F32)
            for r in (range(sub) if fwd else range(sub - 1, -1, -1)):
                v_t = jnp.where(rows == 0, v[r:r + 1, sl], 0.0).astype(BF16)
                k_t = jnp.broadcast_to(k[r:r + 1, sl], (sub, HG_DK)).astype(BF16)
                st = st * f[r:r + 1, sl] + lax.dot_general(v_t, k_t, _TN, preferred_element_type=F32)
                q_t = jnp.broadcast_to(q[r:r + 1, sl], (sub, HG_DK)).astype(BF16)
                o_t = lax.dot_general(q_t, st.astype(BF16), _NT, preferred_element_type=F32)
                out = jnp.where(rows == r, o_t, out)
            st_ref[h] = st
            o32_ref[pl.ds(base, sub), sl] = out
        return carry

    lax.fori_loop(0, C // sub, group, 0)
    o_ref[...] = o32_ref[...].astype(o_ref.dtype)


def _hgrn_kernel(safe_ref, qf_ref, qb_ref, gf_ref, gb_ref, vf_ref, vb_ref, tril_ref, triu_ref,
                 of_ref, ob_ref, sf_ref, sb_ref, q32_ref, v32_ref, o32_ref):
    b, j = pl.program_id(0), pl.program_id(1)
    nc = pl.num_programs(1)

    @pl.when(j == 0)
    def _():
        sf_ref[...] = jnp.zeros_like(sf_ref)
        sb_ref[...] = jnp.zeros_like(sb_ref)

    C = qf_ref.shape[0]
    row = lax.broadcasted_iota(jnp.int32, (C, C), 0)
    col = lax.broadcasted_iota(jnp.int32, (C, C), 1)
    safe_f = safe_ref[0, b * nc + j] != 0
    safe_b = safe_ref[1, b * nc + nc - 1 - j] != 0

    def forward(stepwise):
        if stepwise:
            _hgrn_direction_stepwise(qf_ref, gf_ref, vf_ref, sf_ref, of_ref, q32_ref, v32_ref, o32_ref, True)
        else:
            _hgrn_direction(qf_ref, gf_ref, vf_ref, sf_ref, of_ref, tril_ref[...], row >= col, True)

    def backward(stepwise):
        if stepwise:
            _hgrn_direction_stepwise(qb_ref, gb_ref, vb_ref, sb_ref, ob_ref, q32_ref, v32_ref, o32_ref, False)
        else:
            _hgrn_direction(qb_ref, gb_ref, vb_ref, sb_ref, ob_ref, triu_ref[...], row <= col, False)

    @pl.when(safe_f & safe_b)
    def _():
        forward(False)
        backward(False)

    @pl.when(jnp.logical_not(safe_f & safe_b))
    def _():
        @pl.when(safe_f)
        def _():
            forward(False)

        @pl.when(jnp.logical_not(safe_f))
        def _():
            forward(True)

        @pl.when(safe_b)
        def _():
            backward(False)

        @pl.when(jnp.logical_not(safe_b))
        def _():
            backward(True)


def _hgrn(q, gf, gb, iv, gmin, batch):
    T = q.shape[0]
    C = HG_CHUNK
    nc = T // batch // C
    nt, nh, _ = gmin.shape
    halves = gmin[:, :, 0].reshape(nt, 2, nh // 4, 2)
    safe = (jnp.min(halves, axis=-1) > -HG_SAFE_LOGDECAY).astype(jnp.int32)
    safe = safe.transpose(1, 0, 2).reshape(2, T // C)
    fwd = pl.BlockSpec((C, HG_WIDTH), lambda b, j, s: (b * nc + j, 0))
    bwd = pl.BlockSpec((C, HG_WIDTH), lambda b, j, s: (b * nc + nc - 1 - j, 0))
    const = lambda shape: pl.BlockSpec(shape, lambda b, j, s: (0,) * len(shape),
                                       pipeline_mode=pl.Buffered(1))
    row = lax.broadcasted_iota(jnp.int32, (C, C), 0)
    col = lax.broadcasted_iota(jnp.int32, (C, C), 1)
    tril = (row >= col).astype(BF16)
    triu = (row <= col).astype(BF16)
    grid_spec = pltpu.PrefetchScalarGridSpec(
        num_scalar_prefetch=1,
        grid=(batch, nc),
        in_specs=[fwd, bwd, fwd, bwd, fwd, bwd, const((C, C)), const((C, C))],
        out_specs=(fwd, bwd),
        scratch_shapes=[pltpu.VMEM((HG_HEADS, HG_DK, HG_DK), F32),
                        pltpu.VMEM((HG_HEADS, HG_DK, HG_DK), F32),
                        pltpu.VMEM((C, HG_WIDTH), F32), pltpu.VMEM((C, HG_WIDTH), F32),
                        pltpu.VMEM((C, HG_WIDTH), F32)],
    )
    return pl.pallas_call(
        _hgrn_kernel,
        grid_spec=grid_spec,
        out_shape=(jax.ShapeDtypeStruct((T, HG_WIDTH), BF16), jax.ShapeDtypeStruct((T, HG_WIDTH), BF16)),
        compiler_params=pltpu.CompilerParams(
            dimension_semantics=("parallel", "arbitrary"), vmem_limit_bytes=VMEM_LIMIT),
        name="hgrn",
    )(safe, q, q, gf, gb, iv, iv, tril, triu)


def _mix_kernel(x_ref, lng_ref, lnb_ref, of_ref, ob_ref, so_ref, sgb_ref, ag_ref, ng_ref, wpb_ref,
                wo_ref, l1g_ref, l1b_ref, wr_ref, rb_ref, tri_ref, x1_ref, x1p_ref, route_ref,
                cnt_ref, carry_ref):
    tm = x_ref.shape[0]

    @pl.when(pl.program_id(0) == 0)
    def _():
        carry_ref[...] = jnp.zeros_like(carry_ref)

    o = of_ref[...].astype(F32) + ob_ref[...].astype(F32)
    heads = []
    for h in range(HG_HEADS):
        oh = o[:, h * HG_DK:(h + 1) * HG_DK]
        heads.append(oh * lax.rsqrt(jnp.mean(oh * oh, axis=-1, keepdims=True) + RMS_EPS))
    rn = jnp.concatenate(heads, axis=1) * ng_ref[...] * so_ref[...].astype(F32)
    r = _bdot(rn, wpb_ref[...])
    mixed = ag_ref[...].astype(F32) + sgb_ref[...].astype(F32) * r
    y = _bdot(mixed, wo_ref[...])
    xn = _layer_norm(x_ref[...], lng_ref[...], lnb_ref[...])
    x1 = _layer_norm(ALPHA * xn + y, l1g_ref[...], l1b_ref[...])
    x1_ref[...] = x1
    x1p_ref[...] = _pack_rows(x1)

    neg = jnp.float32(-jnp.inf)
    reps = tm // LANES
    scores = jax.nn.sigmoid(lax.dot_general(wr_ref[...], x1.astype(BF16), _NT,
                                            preferred_element_type=F32))
    biased = (scores + jnp.concatenate([rb_ref[...]] * reps, axis=1)).reshape(
        N_GROUPS, GROUP_SIZE, tm)
    sub = lax.broadcasted_iota(jnp.int32, biased.shape, 1).astype(F32)
    m1 = jnp.max(biased, axis=1, keepdims=True)
    first = jnp.min(jnp.where(biased == m1, sub, float(GROUP_SIZE)), axis=1, keepdims=True)
    m2 = jnp.max(jnp.where(sub == first, neg, biased), axis=1, keepdims=True)
    gs = (m1 + m2).reshape(N_GROUPS, tm)
    grp = lax.broadcasted_iota(jnp.int32, (N_GROUPS, tm), 0)
    ahead = jnp.zeros((N_GROUPS, tm), F32)
    for d in range(1, N_GROUPS):
        other = pltpu.roll(gs, d, 0)
        tie = jnp.where(grp >= d, 1.0, 0.0)
        ahead = ahead + jnp.where(other > gs, 1.0, jnp.where(other == gs, tie, 0.0))
    keep = (ahead < TOPK_GROUPS).reshape(N_GROUPS, 1, tm)
    allowed = jnp.where(keep, biased, neg).reshape(N_EXPERTS, tm)
    row = lax.broadcasted_iota(jnp.int32, (N_EXPERTS, tm), 0).astype(F32)
    sel = jnp.zeros((N_EXPERTS, tm), F32)
    picks = []
    for _ in range(TOP_K):
        m = jnp.max(allowed, axis=0, keepdims=True)
        first = jnp.min(jnp.where(allowed == m, row, float(N_EXPERTS)), axis=0, keepdims=True)
        hit = row == first
        picks.append((first, hit, jnp.sum(jnp.where(hit, scores, 0.0), axis=0, keepdims=True)))
        sel = jnp.where(hit, 1.0, sel)
        allowed = jnp.where(hit, neg, allowed)
    wsum = picks[0][2]
    for pk in picks[1:]:
        wsum = wsum + pk[2]
    selb = sel.astype(BF16)
    carry = carry_ref[...]
    before = (jnp.dot(selb, tri_ref[...], preferred_element_type=F32)
              + jnp.concatenate([carry] * reps, axis=1))
    total = carry + jnp.dot(selb, jnp.ones((tm, LANES), BF16), preferred_element_type=F32)
    carry_ref[...] = total
    cnt_ref[...] = total
    blank = [jnp.zeros((1, tm), F32)] * (8 - TOP_K)
    route_ref[...] = jnp.concatenate(
        [pk[0] for pk in picks] + blank
        + [jnp.sum(jnp.where(pk[1], before, 0.0), axis=0, keepdims=True) for pk in picks] + blank
        + [pk[2] / wsum * ROUTED_SCALE for pk in picks] + blank, axis=0)


def _mix(x, of, ob, so, sgb, ag, p):
    T = x.shape[0]
    tm = TOKEN_TILE
    tok = lambda w: pl.BlockSpec((tm, w), lambda i: (i, 0))
    row = lax.broadcasted_iota(jnp.int32, (tm, tm), 0)
    col = lax.broadcasted_iota(jnp.int32, (tm, tm), 1)
    tri = (row < col).astype(BF16)
    return pl.pallas_call(
        _mix_kernel,
        grid=(T // tm,),
        in_specs=[
            tok(D_MODEL), _const_spec((1, D_MODEL)), _const_spec((1, D_MODEL)),
            tok(HG_WIDTH), tok(HG_WIDTH), tok(HG_WIDTH), tok(D_MODEL), tok(D_MODEL),
            _const_spec((1, HG_WIDTH)),
            _const_spec((HG_WIDTH, D_MODEL)), _const_spec((D_MODEL, D_MODEL)),
            _const_spec((1, D_MODEL)), _const_spec((1, D_MODEL)),
            _const_spec((N_EXPERTS, D_MODEL)), _const_spec((N_EXPERTS, LANES)),
            _const_spec((tm, tm)),
        ],
        out_specs=(tok(D_MODEL), tok(D_MODEL // 2), pl.BlockSpec((24, tm), lambda i: (0, i)),
                   pl.BlockSpec((N_EXPERTS, LANES), lambda i: (0, 0))),
        out_shape=(jax.ShapeDtypeStruct((T, D_MODEL), F32),
                   jax.ShapeDtypeStruct((T, D_MODEL // 2), jnp.uint32),
                   jax.ShapeDtypeStruct((24, T), F32),
                   jax.ShapeDtypeStruct((N_EXPERTS, LANES), F32)),
        scratch_shapes=[pltpu.VMEM((N_EXPERTS, LANES), F32)],
        compiler_params=pltpu.CompilerParams(
            dimension_semantics=("arbitrary",), vmem_limit_bytes=VMEM_LIMIT),
        name="mix",
    )(x, p["ln_in_g"], p["ln_in_b"], of, ob, so, sgb, ag, p["hg_norm_g"], p["w_pb"], p["w_o"],
      p["ln1_g"], p["ln1_b"], p["w_router"], p["router_bias"], tri)


def _sc_mesh():
    return plsc.VectorSubcoreMesh(core_axis_name="c", subcore_axis_name="s",
                                  num_cores=SC_CORES, num_subcores=SC_SUBCORES)


def _sc_worker():
    return lax.axis_index("s") * SC_CORES + lax.axis_index("c")


def _sc_dispatch(x1p, destw, n_rows):
    T, w = x1p.shape
    n_win, _, W = destw.shape
    per_worker = n_win // (SC_CORES * SC_SUBCORES)

    def body(x_hbm, d_hbm, o_hbm, rows_v, idx_v, sem):
        first = _sc_worker() * per_worker

        @pl.loop(0, per_worker)
        def _(j):
            win = first + j
            pltpu.sync_copy(x_hbm.at[pl.ds(win * W, W)], rows_v)
            pltpu.sync_copy(d_hbm.at[win], idx_v)
            copies = [pltpu.async_copy(rows_v, o_hbm.at[idx_v.at[k]], sem) for k in range(TOP_K)]
            for c in copies:
                c.wait()

    return pl.kernel(
        body,
        out_type=jax.ShapeDtypeStruct((n_rows, w), jnp.uint32),
        mesh=_sc_mesh(),
        scratch_types=[pltpu.VMEM((W, w), jnp.uint32), pltpu.VMEM((TOP_K, W), jnp.int32),
                       pltpu.SemaphoreType.DMA],
        name="sc_dispatch",
    )(x1p, destw)


def _sc_combine(ys, destw):
    n_win, _, W = destw.shape
    w = ys.shape[1]
    per_worker = n_win // (SC_CORES * SC_SUBCORES)

    def body(y_hbm, d_hbm, o_hbm, rows_v, idx_v, sem):
        first = _sc_worker() * per_worker

        @pl.loop(0, per_worker)
        def _(j):
            win = first + j
            pltpu.sync_copy(d_hbm.at[win], idx_v)
            gathers = [pltpu.async_copy(y_hbm.at[idx_v.at[k]], rows_v.at[k], sem)
                       for k in range(TOP_K)]
            for c in gathers:
                c.wait()
            writes = [pltpu.async_copy(rows_v.at[k], o_hbm.at[k, pl.ds(win * W, W)], sem)
                      for k in range(TOP_K)]
            for c in writes:
                c.wait()

    return pl.kernel(
        body,
        out_type=jax.ShapeDtypeStruct((TOP_K, n_win * W, w), jnp.uint32),
        mesh=_sc_mesh(),
        scratch_types=[pltpu.VMEM((TOP_K, W, w), jnp.uint32), pltpu.VMEM((TOP_K, W), jnp.int32),
                       pltpu.SemaphoreType.DMA],
        name="sc_combine",
    )(ys, destw)


def _experts_kernel(be_ref, nv_ref, xs_ref, wg_ref, wu_ref, wd_ref, ys_ref):
    del be_ref
    n_valid = nv_ref[pl.program_id(0)]
    mb, w = xs_ref.shape

    @pl.when(n_valid > 0)
    def _():
        keep = lax.broadcasted_iota(jnp.int32, (mb, w), 0) < n_valid
        lo, hi = _unpack_rows(jnp.where(keep, xs_ref[...], jnp.uint32(0)))
        lo, hi = lo.astype(BF16), hi.astype(BF16)

        def proj(w_ref):
            return (jnp.dot(lo, w_ref[0, :w, :], preferred_element_type=F32)
                    + jnp.dot(hi, w_ref[0, w:, :], preferred_element_type=F32))

        hb = jax.nn.silu(proj(wg_ref)) * proj(wu_ref)
        ys_ref[...] = _pack_rows(jnp.dot(hb.astype(BF16), wd_ref[0], preferred_element_type=F32))

    @pl.when(n_valid <= 0)
    def _():
        ys_ref[...] = jnp.zeros_like(ys_ref)


def _experts(xs, blk_exp, n_valid, p):
    n_rows, w = xs.shape
    mb = n_rows // blk_exp.shape[0]
    grid_spec = pltpu.PrefetchScalarGridSpec(
        num_scalar_prefetch=2,
        grid=(n_rows // mb,),
        in_specs=[
            pl.BlockSpec((mb, w), lambda i, be, nv: (i, 0)),
            pl.BlockSpec((1, D_MODEL, EXPERT_DIM), lambda i, be, nv: (be[i], 0, 0)),
            pl.BlockSpec((1, D_MODEL, EXPERT_DIM), lambda i, be, nv: (be[i], 0, 0)),
            pl.BlockSpec((1, EXPERT_DIM, D_MODEL), lambda i, be, nv: (be[i], 0, 0)),
        ],
        out_specs=pl.BlockSpec((mb, w), lambda i, be, nv: (i, 0)),
    )
    return pl.pallas_call(
        _experts_kernel,
        grid_spec=grid_spec,
        out_shape=jax.ShapeDtypeStruct((n_rows, w), jnp.uint32),
        compiler_params=pltpu.CompilerParams(
            dimension_semantics=("arbitrary",), vmem_limit_bytes=VMEM_LIMIT),
        name="experts",
    )(blk_exp, n_valid, xs, p["w_e_gate"], p["w_e_up"], p["w_e_down"])


def _final_kernel(x1_ref, gate_ref, yg_ref, wsg_ref, wsu_ref, wsd_ref, l2g_ref, l2b_ref, out_ref):
    x1 = x1_ref[...]
    xb = x1.astype(BF16)
    hs = (jax.nn.silu(jnp.dot(xb, wsg_ref[...], preferred_element_type=F32))
          * jnp.dot(xb, wsu_ref[...], preferred_element_type=F32))
    shared = jnp.dot(hs.astype(BF16), wsd_ref[...], preferred_element_type=F32)
    gate = gate_ref[...]
    lo, hi = None, None
    for k in range(TOP_K):
        wk = gate[:, k:k + 1]
        lk, hk = _unpack_rows(yg_ref[k])
        lo = lk * wk if lo is None else lo + lk * wk
        hi = hk * wk if hi is None else hi + hk * wk
    routed = jnp.concatenate([lo, hi], axis=1)
    out_ref[...] = _layer_norm(ALPHA * x1 + (routed + shared), l2g_ref[...], l2b_ref[...])


def _final(x1, gates, yg, p):
    T = x1.shape[0]
    tm = TOKEN_TILE
    tok = lambda w: pl.BlockSpec((tm, w), lambda i: (i, 0))
    return pl.pallas_call(
        _final_kernel,
        grid=(T // tm,),
        in_specs=[
            tok(D_MODEL), tok(gates.shape[1]),
            pl.BlockSpec((TOP_K, tm, yg.shape[2]), lambda i: (0, i, 0)),
            _const_spec((D_MODEL, SHARED_DIM)), _const_spec((D_MODEL, SHARED_DIM)),
            _const_spec((SHARED_DIM, D_MODEL)),
            _const_spec((1, D_MODEL)), _const_spec((1, D_MODEL)),
        ],
        out_specs=tok(D_MODEL),
        out_shape=jax.ShapeDtypeStruct((T, D_MODEL), F32),
        compiler_params=pltpu.CompilerParams(
            dimension_semantics=("parallel",), vmem_limit_bytes=VMEM_LIMIT),
        name="final",
    )(x1, gates, yg, p["w_sh_gate"], p["w_sh_up"], p["w_sh_down"], p["ln2_g"], p["ln2_b"])


def _expert_block(n_tokens):
    mean_rows = n_tokens * TOP_K // N_EXPERTS
    return EXPERT_BLOCK_LARGE if mean_rows >= 2 * EXPERT_BLOCK_LARGE else EXPERT_BLOCK


def _routing_layout(route, counts, n_tokens):
    mb = _expert_block(n_tokens)
    n_blocks = -(-n_tokens * TOP_K // mb) + N_EXPERTS
    idx = route[0:TOP_K].astype(jnp.int32)
    rank = route[8:8 + TOP_K].astype(jnp.int32)
    counts = counts.astype(jnp.int32)
    padded = (counts + mb - 1) // mb * mb
    pad_end = jnp.cumsum(padded)
    pad_start = pad_end - padded
    experts = jnp.arange(N_EXPERTS, dtype=jnp.int32)
    dest = rank + jnp.sum(jnp.where(idx[:, :, None] == experts, pad_start, 0), axis=-1)
    destw = dest.reshape(TOP_K, n_tokens // SC_WINDOW, SC_WINDOW).transpose(1, 0, 2)
    blk_start = jnp.arange(n_blocks, dtype=jnp.int32) * mb
    blk_exp = jnp.minimum(
        jnp.sum((pad_end[None, :] <= blk_start[:, None]).astype(jnp.int32), axis=1), N_EXPERTS - 1)
    valid_end = jnp.sum(jnp.where(blk_exp[:, None] == experts, pad_start + counts, 0), axis=-1)
    n_valid = jnp.clip(valid_end - blk_start, 0, mb).astype(jnp.int32)
    return destw, blk_exp, n_valid, n_blocks * mb


def _encode(x, p):
    batch, seq, _ = x.shape
    T = batch * seq
    xt = x.reshape(T, D_MODEL)
    ag, q, gf, gb, iv, so, sgb, gmin = _inproj(xt, p)
    of, ob = _hgrn(q, gf, gb, iv, gmin, batch)
    x1, x1p, route, cnt = _mix(xt, of, ob, so, sgb, ag, p)
    destw, blk_exp, n_valid, n_rows = _routing_layout(route, cnt[:, 0], T)
    xs = _sc_dispatch(x1p, destw, n_rows)
    ys = _experts(xs, blk_exp, n_valid, p)
    out = _final(x1, route[16:24].T, _sc_combine(ys, destw), p)
    return out.reshape(batch, seq, D_MODEL)


def _prepare_params(ln_in_g, ln_in_b, w_in, a_ln_g, a_ln_b, a_ws, a_sb, hg_lb_logits, hg_norm_g,
                    w_pa, w_pb, w_o, ln1_g, ln1_b, w_router, router_bias, w_e_gate, w_e_up,
                    w_e_down, w_sh_gate, w_sh_up, w_sh_down, ln2_g, ln2_b):
    l = 0
    row = lambda v: v.reshape(1, -1).astype(F32)
    ws = a_ws[l].astype(BF16)
    wsp = jnp.concatenate([ws[0::2], ws[1::2]], axis=2)
    sbf = jnp.repeat(a_sb[l].astype(F32), A_WIDTH // A_GROUPS, axis=1)
    lb = jnp.cumsum(jax.nn.softmax(hg_lb_logits.astype(F32), axis=1), axis=1)[:, l]
    return dict(
        ln_in_g=row(ln_in_g), ln_in_b=row(ln_in_b), w_in=w_in[l].astype(BF16),
        a_ln_g=row(a_ln_g[l]), a_ln_b=row(a_ln_b[l]), wsp=wsp, sbf=sbf, lb=lb,
        w_pa=w_pa[l].astype(BF16), hg_norm_g=row(hg_norm_g[l]),
        w_pb=w_pb[l].astype(BF16), w_o=w_o[l].astype(BF16),
        ln1_g=row(ln1_g[l]), ln1_b=row(ln1_b[l]),
        w_router=w_router[l].T.astype(BF16),
        router_bias=jnp.broadcast_to(router_bias[l].astype(F32)[:, None], (N_EXPERTS, LANES)),
        w_e_gate=w_e_gate[l].astype(BF16), w_e_up=w_e_up[l].astype(BF16),
        w_e_down=w_e_down[l].astype(BF16),
        w_sh_gate=w_sh_gate[l].astype(BF16), w_sh_up=w_sh_up[l].astype(BF16),
        w_sh_down=w_sh_down[l].astype(BF16),
        ln2_g=row(ln2_g[l]), ln2_b=row(ln2_b[l]),
    )


def kernel(x_prompt, x_sample, ln_in_g, ln_in_b, w_in, a_ln_g, a_ln_b, a_ws, a_sb, hg_lb_logits,
           hg_norm_g, w_pa, w_pb, w_o, ln1_g, ln1_b, w_router, router_bias, w_e_gate, w_e_up,
           w_e_down, w_sh_gate, w_sh_up, w_sh_down, ln2_g, ln2_b):
    p = _prepare_params(ln_in_g, ln_in_b, w_in, a_ln_g, a_ln_b, a_ws, a_sb, hg_lb_logits, hg_norm_g,
                        w_pa, w_pb, w_o, ln1_g, ln1_b, w_router, router_bias, w_e_gate, w_e_up,
                        w_e_down, w_sh_gate, w_sh_up, w_sh_down, ln2_g, ln2_b)
    return _encode(x_prompt, p), _encode(x_sample, p)
```

```python
import functools

import jax
import jax.numpy as jnp
from jax import lax
from jax.experimental import pallas as pl
from jax.experimental.pallas import tpu as pltpu
from jax.experimental.pallas import tpu_sc as plsc

F32 = jnp.float32
BF16 = jnp.bfloat16

D_MODEL = 1024
A_GROUPS = 8
A_WIDTH = 512
A_CHUNK = 128
HG_HEADS = 8
HG_DK = 128
HG_WIDTH = HG_HEADS * HG_DK
N_IN = 2 * A_WIDTH + 5 * HG_WIDTH + 2 * D_MODEL
N_EXPERTS = 64
TOP_K = 6
N_GROUPS = 8
TOPK_GROUPS = 4
GROUP_SIZE = N_EXPERTS // N_GROUPS
EXPERT_DIM = 256
SHARED_DIM = 256
ROUTED_SCALE = 2.5
DEPTH = 1
ALPHA = (2.0 * DEPTH) ** 0.25
LN_EPS = 1e-5
RMS_EPS = 1e-6

LANES = 128
TOKEN_TILE = 256
INPROJ_TILE = 256
HG_CHUNK = 128
HG_SAFE_LOGDECAY = 80.0
EXPERT_BLOCK = 512
EXPERT_BLOCK_LARGE = 1024
SC_CORES = 2
SC_SUBCORES = 16
SC_LANES = 16
SC_WINDOW = 32
SC_COMBINE_WINDOW = 16
VMEM_LIMIT = 56 * 1024 * 1024

_O_U, _O_V, _O_Q, _O_FF, _O_FB, _O_I, _O_G, _O_GA, _O_GB = (
    0, 512, 1024, 2048, 3072, 4096, 5120, 6144, 7168)


def _layer_norm(x, g, b):
    mu = jnp.mean(x, axis=-1, keepdims=True)
    xc = x - mu
    var = jnp.mean(xc * xc, axis=-1, keepdims=True)
    return xc * lax.rsqrt(var + LN_EPS) * g + b


def _bdot(a, b):
    return jnp.dot(a.astype(BF16), b.astype(BF16), preferred_element_type=F32)


def _pack_rows(x):
    w = x.shape[1] // 2
    lo = lax.bitcast_convert_type(x[:, :w].astype(BF16).astype(F32), jnp.uint32)
    hi = lax.bitcast_convert_type(x[:, w:].astype(BF16).astype(F32), jnp.uint32)
    return hi | (lo >> 16)


def _unpack_rows(p):
    lo = lax.bitcast_convert_type(p << 16, F32)
    hi = lax.bitcast_convert_type(p & jnp.uint32(0xFFFF0000), F32)
    return lo, hi


def _const_spec(shape):
    nd = len(shape)
    return pl.BlockSpec(shape, lambda *_: (0,) * nd, pipeline_mode=pl.Buffered(1))


def _inproj_kernel(x_ref, lng_ref, lnb_ref, win_ref, alng_ref, alnb_ref, wsp_ref, sbf_ref, lb_ref,
                   wpa_ref, ag_ref, q_ref, gf_ref, gb_ref, iv_ref, so_ref, sgb_ref, gmin_ref):
    tm = x_ref.shape[0]
    xb = _layer_norm(x_ref[...], lng_ref[...], lnb_ref[...]).astype(BF16)

    def sec(lo, width):
        return jnp.dot(xb, win_ref[:, lo:lo + width], preferred_element_type=F32)

    u = jax.nn.gelu(sec(_O_U, A_WIDTH))
    v = _layer_norm(jax.nn.gelu(sec(_O_V, A_WIDTH)), alng_ref[...], alnb_ref[...]).astype(BF16)
    lane = lax.broadcasted_iota(jnp.int32, (A_CHUNK, LANES), 1)
    left = lane < (A_WIDTH // A_GROUPS)
    zero = jnp.zeros((A_CHUNK, LANES), BF16)
    chunks = []
    for c in range(tm // A_CHUNK):
        vc = v[c * A_CHUNK:(c + 1) * A_CHUNK]
        cols = []
        for p in range(A_GROUPS // 2):
            vp = vc[:, p * LANES:(p + 1) * LANES]
            rhs = jnp.concatenate([jnp.where(left, vp, zero), jnp.where(left, zero, vp)], axis=0)
            cols.append(jnp.dot(wsp_ref[p], rhs, preferred_element_type=F32))
        chunks.append(jnp.concatenate(cols, axis=1) + sbf_ref[...])
    mixed = jnp.concatenate(chunks, axis=0)
    a = _bdot(u * mixed, wpa_ref[...])
    ag_ref[...] = (jax.nn.sigmoid(sec(_O_GA, D_MODEL)) * a).astype(BF16)

    q_ref[...] = jax.nn.silu(sec(_O_Q, HG_WIDTH)).astype(BF16)
    mins = []
    for d, (off, g_ref) in enumerate(((_O_FF, gf_ref), (_O_FB, gb_ref))):
        lb = lb_ref[d:d + 1, :]
        f = lb + (1.0 - lb) * jax.nn.sigmoid(sec(off, HG_WIDTH))
        g = jnp.log(f)
        g_ref[...] = g
        half = jnp.sum(g.reshape(tm // (HG_CHUNK // 2), HG_CHUNK // 2, HG_WIDTH), axis=1)
        mins.append(jnp.min(half, axis=-1, keepdims=True))
    gmin_ref[0] = jnp.broadcast_to(jnp.concatenate(mins, axis=0), gmin_ref.shape[1:])
    iv_ref[...] = sec(_O_I, HG_WIDTH).astype(BF16)
    so_ref[...] = jax.nn.silu(sec(_O_G, HG_WIDTH)).astype(BF16)
    sgb_ref[...] = jax.nn.sigmoid(sec(_O_GB, D_MODEL)).astype(BF16)


def _inproj(x, p):
    T = x.shape[0]
    tm = INPROJ_TILE
    nt = T // tm
    nh = 2 * tm // (HG_CHUNK // 2)
    tok = lambda w: pl.BlockSpec((tm, w), lambda i: (i, 0))
    outs = (
        jax.ShapeDtypeStruct((T, D_MODEL), BF16),
        jax.ShapeDtypeStruct((T, HG_WIDTH), BF16),
        jax.ShapeDtypeStruct((T, HG_WIDTH), F32),
        jax.ShapeDtypeStruct((T, HG_WIDTH), F32),
        jax.ShapeDtypeStruct((T, HG_WIDTH), BF16),
        jax.ShapeDtypeStruct((T, HG_WIDTH), BF16),
        jax.ShapeDtypeStruct((T, D_MODEL), BF16),
        jax.ShapeDtypeStruct((nt, nh, LANES), F32),
    )
    return pl.pallas_call(
        _inproj_kernel,
        grid=(nt,),
        in_specs=[
            tok(D_MODEL),
            _const_spec((1, D_MODEL)), _const_spec((1, D_MODEL)),
            _const_spec((D_MODEL, N_IN)),
            _const_spec((1, A_WIDTH)), _const_spec((1, A_WIDTH)),
            _const_spec((A_GROUPS // 2, A_CHUNK, 2 * A_CHUNK)),
            _const_spec((A_CHUNK, A_WIDTH)),
            _const_spec((2, HG_WIDTH)),
            _const_spec((A_WIDTH, D_MODEL)),
        ],
        out_specs=(tok(D_MODEL), tok(HG_WIDTH), tok(HG_WIDTH), tok(HG_WIDTH), tok(HG_WIDTH),
                   tok(HG_WIDTH), tok(D_MODEL), pl.BlockSpec((1, nh, LANES), lambda i: (i, 0, 0))),
        out_shape=outs,
        compiler_params=pltpu.CompilerParams(
            dimension_semantics=("parallel",), vmem_limit_bytes=VMEM_LIMIT),
        name="inproj",
    )(x, p["ln_in_g"], p["ln_in_b"], p["w_in"], p["a_ln_g"], p["a_ln_b"], p["wsp"], p["sbf"],
      p["lb"], p["w_pa"])


_NT = (((1,), (1,)), ((), ()))
_TN = (((0,), (0,)), ((), ()))


def _hgrn_direction(q_ref, g_ref, v_ref, st_ref, o_ref, tri, mask, fwd):
    C = q_ref.shape[0]
    g = g_ref[...]
    ghi = g.astype(BF16)
    glo = (g - ghi.astype(F32)).astype(BF16)
    b = jnp.dot(jnp.concatenate([tri, tri], axis=1), jnp.concatenate([ghi, glo], axis=0),
                preferred_element_type=F32)
    mid = C // 2 - 1 if fwd else C // 2
    end = C - 1 if fwd else 0
    r = b[mid:mid + 1, :]
    b_end = b[end:end + 1, :]
    qt = q_ref[...].astype(F32) * jnp.exp(b - r)
    kt = (1.0 - jnp.exp(g)) * jnp.exp(r - b)
    qtb = qt.astype(BF16)
    ktb = kt.astype(BF16)
    qhb = (qt * jnp.exp(r)).astype(BF16)
    khb = (kt * jnp.exp(b_end - r)).astype(BF16)
    decay = jnp.exp(b_end)
    v32 = v_ref[...].astype(F32)
    heads = [slice(h * HG_DK, (h + 1) * HG_DK) for h in range(HG_HEADS)]
    zero = jnp.zeros((C, HG_DK), BF16)
    mask2 = jnp.concatenate([mask, mask], axis=1)
    scores = []
    for p in range(HG_HEADS // 2):
        k1, k2 = ktb[:, heads[2 * p]], ktb[:, heads[2 * p + 1]]
        kk = jnp.concatenate([jnp.concatenate([k1, zero], axis=1),
                              jnp.concatenate([zero, k2], axis=1)], axis=0)
        s2 = lax.dot_general(qtb[:, 2 * p * HG_DK:(2 * p + 2) * HG_DK], kk, _NT,
                             preferred_element_type=F32)
        s2 = jnp.where(mask2, s2, 0.0).astype(BF16)
        scores += [s2[:, :C], s2[:, C:]]
    for h, sl in enumerate(heads):
        st = st_ref[h]
        vt = v32[:, sl].T.astype(BF16)
        o_ref[:, sl] = lax.dot_general(
            jnp.concatenate([scores[h], qhb[:, sl]], axis=1),
            jnp.concatenate([vt, st.astype(BF16)], axis=1), _NT,
            preferred_element_type=F32).astype(o_ref.dtype)
        st_ref[h] = st * decay[:, sl] + jnp.dot(vt, khb[:, sl], preferred_element_type=F32)


def _hgrn_direction_stepwise(q_ref, g_ref, v_ref, st_ref, o_ref, q32_ref, v32_ref, o32_ref, fwd):
    C = q_ref.shape[0]
    q32_ref[...] = q_ref[...].astype(F32)
    v32_ref[...] = v_ref[...].astype(F32)
    sub = 8
    rows = lax.broadcasted_iota(jnp.int32, (sub, HG_DK), 0)

    def group(i, carry):
        base = pl.multiple_of((i if fwd else C // sub - 1 - i) * sub, sub)
        f = jnp.exp(g_ref[pl.ds(base, sub), :])
        k = 1.0 - f
        q = q32_ref[pl.ds(base, sub), :]
        v = v32_ref[pl.ds(base, sub), :]
        for h in range(HG_HEADS):
            sl = slice(h * HG_DK, (h + 1) * HG_DK)
            st = st_ref[h]
            out = jnp.zeros((sub, HG_DK), F32)
            for r in (range(sub) if fwd else range(sub - 1, -1, -1)):
                v_t = jnp.where(rows == 0, v[r:r + 1, sl], 0.0).astype(BF16)
                k_t = jnp.broadcast_to(k[r:r + 1, sl], (sub, HG_DK)).astype(BF16)
                st = st * f[r:r + 1, sl] + lax.dot_general(v_t, k_t, _TN, preferred_element_type=F32)
                q_t = jnp.broadcast_to(q[r:r + 1, sl], (sub, HG_DK)).astype(BF16)
                o_t = lax.dot_general(q_t, st.astype(BF16), _NT, preferred_element_type=F32)
                out = jnp.where(rows == r, o_t, out)
            st_ref[h] = st
            o32_ref[pl.ds(base, sub), sl] = out
        return carry

    lax.fori_loop(0, C // sub, group, 0)
    o_ref[...] = o32_ref[...].astype(o_ref.dtype)


def _hgrn_kernel(safe_ref, qf_ref, qb_ref, gf_ref, gb_ref, vf_ref, vb_ref, tril_ref, triu_ref,
                 of_ref, ob_ref, sf_ref, sb_ref, q32_ref, v32_ref, o32_ref):
    b, j = pl.program_id(0), pl.program_id(1)
    nc = pl.num_programs(1)

    @pl.when(j == 0)
    def _():
        sf_ref[...] = jnp.zeros_like(sf_ref)
        sb_ref[...] = jnp.zeros_like(sb_ref)

    C = qf_ref.shape[0]
    row = lax.broadcasted_iota(jnp.int32, (C, C), 0)
    col = lax.broadcasted_iota(jnp.int32, (C, C), 1)
    safe_f = safe_ref[0, b * nc + j] != 0
    safe_b = safe_ref[1, b * nc + nc - 1 - j] != 0

    def forward(stepwise):
        if stepwise:
            _hgrn_direction_stepwise(qf_ref, gf_ref, vf_ref, sf_ref, of_ref, q32_ref, v32_ref, o32_ref, True)
        else:
            _hgrn_direction(qf_ref, gf_ref, vf_ref, sf_ref, of_ref, tril_ref[...], row >= col, True)

    def backward(stepwise):
        if stepwise:
            _hgrn_direction_stepwise(qb_ref, gb_ref, vb_ref, sb_ref, ob_ref, q32_ref, v32_ref, o32_ref, False)
        else:
            _hgrn_direction(qb_ref, gb_ref, vb_ref, sb_ref, ob_ref, triu_ref[...], row <= col, False)

    @pl.when(safe_f & safe_b)
    def _():
        forward(False)
        backward(False)

    @pl.when(jnp.logical_not(safe_f & safe_b))
    def _():
        @pl.when(safe_f)
        def _():
            forward(False)

        @pl.when(jnp.logical_not(safe_f))
        def _():
            forward(True)

        @pl.when(safe_b)
        def _():
            backward(False)

        @pl.when(jnp.logical_not(safe_b))
        def _():
            backward(True)


def _hgrn(q, gf, gb, iv, gmin, batch):
    T = q.shape[0]
    C = HG_CHUNK
    nc = T // batch // C
    nt, nh, _ = gmin.shape
    halves = gmin[:, :, 0].reshape(nt, 2, nh // 4, 2)
    safe = (jnp.min(halves, axis=-1) > -HG_SAFE_LOGDECAY).astype(jnp.int32)
    safe = safe.transpose(1, 0, 2).reshape(2, T // C)
    fwd = pl.BlockSpec((C, HG_WIDTH), lambda b, j, s: (b * nc + j, 0))
    bwd = pl.BlockSpec((C, HG_WIDTH), lambda b, j, s: (b * nc + nc - 1 - j, 0))
    const = lambda shape: pl.BlockSpec(shape, lambda b, j, s: (0,) * len(shape),
                                       pipeline_mode=pl.Buffered(1))
    row = lax.broadcasted_iota(jnp.int32, (C, C), 0)
    col = lax.broadcasted_iota(jnp.int32, (C, C), 1)
    tril = (row >= col).astype(BF16)
    triu = (row <= col).astype(BF16)
    grid_spec = pltpu.PrefetchScalarGridSpec(
        num_scalar_prefetch=1,
        grid=(batch, nc),
        in_specs=[fwd, bwd, fwd, bwd, fwd, bwd, const((C, C)), const((C, C))],
        out_specs=(fwd, bwd),
        scratch_shapes=[pltpu.VMEM((HG_HEADS, HG_DK, HG_DK), F32),
                        pltpu.VMEM((HG_HEADS, HG_DK, HG_DK), F32),
                        pltpu.VMEM((C, HG_WIDTH), F32), pltpu.VMEM((C, HG_WIDTH), F32),
                        pltpu.VMEM((C, HG_WIDTH), F32)],
    )
    return pl.pallas_call(
        _hgrn_kernel,
        grid_spec=grid_spec,
        out_shape=(jax.ShapeDtypeStruct((T, HG_WIDTH), BF16), jax.ShapeDtypeStruct((T, HG_WIDTH), BF16)),
        compiler_params=pltpu.CompilerParams(
            dimension_semantics=("parallel", "arbitrary"), vmem_limit_bytes=VMEM_LIMIT),
        name="hgrn",
    )(safe, q, q, gf, gb, iv, iv, tril, triu)


def _mix_kernel(x_ref, lng_ref, lnb_ref, of_ref, ob_ref, so_ref, sgb_ref, ag_ref, ng_ref, wpb_ref,
                wo_ref, l1g_ref, l1b_ref, wr_ref, rb_ref, tri_ref, x1_ref, x1p_ref, route_ref,
                cnt_ref, carry_ref):
    tm = x_ref.shape[0]

    @pl.when(pl.program_id(0) == 0)
    def _():
        carry_ref[...] = jnp.zeros_like(carry_ref)

    o = of_ref[...].astype(F32) + ob_ref[...].astype(F32)
    heads = []
    for h in range(HG_HEADS):
        oh = o[:, h * HG_DK:(h + 1) * HG_DK]
        heads.append(oh * lax.rsqrt(jnp.mean(oh * oh, axis=-1, keepdims=True) + RMS_EPS))
    rn = jnp.concatenate(heads, axis=1) * ng_ref[...] * so_ref[...].astype(F32)
    r = _bdot(rn, wpb_ref[...])
    mixed = ag_ref[...].astype(F32) + sgb_ref[...].astype(F32) * r
    y = _bdot(mixed, wo_ref[...])
    xn = _layer_norm(x_ref[...], lng_ref[...], lnb_ref[...])
    x1 = _layer_norm(ALPHA * xn + y, l1g_ref[...], l1b_ref[...])
    x1_ref[...] = x1
    x1p_ref[...] = _pack_rows(x1)

    neg = jnp.float32(-jnp.inf)
    reps = tm // LANES
    scores = jax.nn.sigmoid(lax.dot_general(wr_ref[...], x1.astype(BF16), _NT,
                                            preferred_element_type=F32))
    biased = (scores + jnp.concatenate([rb_ref[...]] * reps, axis=1)).reshape(
        N_GROUPS, GROUP_SIZE, tm)
    sub = lax.broadcasted_iota(jnp.int32, biased.shape, 1).astype(F32)
    m1 = jnp.max(biased, axis=1, keepdims=True)
    first = jnp.min(jnp.where(biased == m1, sub, float(GROUP_SIZE)), axis=1, keepdims=True)
    m2 = jnp.max(jnp.where(sub == first, neg, biased), axis=1, keepdims=True)
    gs = (m1 + m2).reshape(N_GROUPS, tm)
    grp = lax.broadcasted_iota(jnp.int32, (N_GROUPS, tm), 0)
    ahead = jnp.zeros((N_GROUPS, tm), F32)
    for d in range(1, N_GROUPS):
        other = pltpu.roll(gs, d, 0)
        tie = jnp.where(grp >= d, 1.0, 0.0)
        ahead = ahead + jnp.where(other > gs, 1.0, jnp.where(other == gs, tie, 0.0))
    keep = (ahead < TOPK_GROUPS).reshape(N_GROUPS, 1, tm)
    allowed = jnp.where(keep, biased, neg).reshape(N_EXPERTS, tm)
    row = lax.broadcasted_iota(jnp.int32, (N_EXPERTS, tm), 0).astype(F32)
    sel = jnp.zeros((N_EXPERTS, tm), F32)
    picks = []
    for _ in range(TOP_K):
        m = jnp.max(allowed, axis=0, keepdims=True)
        first = jnp.min(jnp.where(allowed == m, row, float(N_EXPERTS)), axis=0, keepdims=True)
        hit = row == first
        picks.append((first, hit, jnp.sum(jnp.where(hit, scores, 0.0), axis=0, keepdims=True)))
        sel = jnp.where(hit, 1.0, sel)
        allowed = jnp.where(hit, neg, allowed)
    wsum = picks[0][2]
    for pk in picks[1:]:
        wsum = wsum + pk[2]
    selb = sel.astype(BF16)
    carry = carry_ref[...]
    before = (jnp.dot(selb, tri_ref[...], preferred_element_type=F32)
              + jnp.concatenate([carry] * reps, axis=1))
    total = carry + jnp.dot(selb, jnp.ones((tm, LANES), BF16), preferred_element_type=F32)
    carry_ref[...] = total
    cnt_ref[...] = total
    blank = [jnp.zeros((1, tm), F32)] * (8 - TOP_K)
    route_ref[...] = jnp.concatenate(
        [pk[0] for pk in picks] + blank
        + [jnp.sum(jnp.where(pk[1], before, 0.0), axis=0, keepdims=True) for pk in picks] + blank
        + [pk[2] / wsum * ROUTED_SCALE for pk in picks] + blank, axis=0)


def _mix(x, of, ob, so, sgb, ag, p):
    T = x.shape[0]
    tm = TOKEN_TILE
    tok = lambda w: pl.BlockSpec((tm, w), lambda i: (i, 0))
    row = lax.broadcasted_iota(jnp.int32, (tm, tm), 0)
    col = lax.broadcasted_iota(jnp.int32, (tm, tm), 1)
    tri = (row < col).astype(BF16)
    return pl.pallas_call(
        _mix_kernel,
        grid=(T // tm,),
        in_specs=[
            tok(D_MODEL), _const_spec((1, D_MODEL)), _const_spec((1, D_MODEL)),
            tok(HG_WIDTH), tok(HG_WIDTH), tok(HG_WIDTH), tok(D_MODEL), tok(D_MODEL),
            _const_spec((1, HG_WIDTH)),
            _const_spec((HG_WIDTH, D_MODEL)), _const_spec((D_MODEL, D_MODEL)),
            _const_spec((1, D_MODEL)), _const_spec((1, D_MODEL)),
            _const_spec((N_EXPERTS, D_MODEL)), _const_spec((N_EXPERTS, LANES)),
            _const_spec((tm, tm)),
        ],
        out_specs=(tok(D_MODEL), tok(D_MODEL // 2), pl.BlockSpec((24, tm), lambda i: (0, i)),
                   pl.BlockSpec((N_EXPERTS, LANES), lambda i: (0, 0))),
        out_shape=(jax.ShapeDtypeStruct((T, D_MODEL), F32),
                   jax.ShapeDtypeStruct((T, D_MODEL // 2), jnp.uint32),
                   jax.ShapeDtypeStruct((24, T), F32),
                   jax.ShapeDtypeStruct((N_EXPERTS, LANES), F32)),
        scratch_shapes=[pltpu.VMEM((N_EXPERTS, LANES), F32)],
        compiler_params=pltpu.CompilerParams(
            dimension_semantics=("arbitrary",), vmem_limit_bytes=VMEM_LIMIT),
        name="mix",
    )(x, p["ln_in_g"], p["ln_in_b"], of, ob, so, sgb, ag, p["hg_norm_g"], p["w_pb"], p["w_o"],
      p["ln1_g"], p["ln1_b"], p["w_router"], p["router_bias"], tri)


def _sc_mesh():
    return plsc.VectorSubcoreMesh(core_axis_name="c", subcore_axis_name="s",
                                  num_cores=SC_CORES, num_subcores=SC_SUBCORES)


def _sc_worker():
    return lax.axis_index("s") * SC_CORES + lax.axis_index("c")


def _sc_dispatch(x1p, destw, n_rows):
    T, w = x1p.shape
    n_win, _, W = destw.shape
    per_worker = n_win // (SC_CORES * SC_SUBCORES)

    def body(x_hbm, d_hbm, o_hbm, rows_v, idx_v, sem):
        first = _sc_worker() * per_worker

        @pl.loop(0, per_worker)
        def _(j):
            win = first + j
            pltpu.sync_copy(x_hbm.at[pl.ds(win * W, W)], rows_v)
            pltpu.sync_copy(d_hbm.at[win], idx_v)
            copies = [pltpu.async_copy(rows_v, o_hbm.at[idx_v.at[k]], sem) for k in range(TOP_K)]
            for c in copies:
                c.wait()

    return pl.kernel(
        body,
        out_type=jax.ShapeDtypeStruct((n_rows, w), jnp.uint32),
        mesh=_sc_mesh(),
        scratch_types=[pltpu.VMEM((W, w), jnp.uint32), pltpu.VMEM((TOP_K, W), jnp.int32),
                       pltpu.SemaphoreType.DMA],
        name="sc_dispatch",
    )(x1p, destw)


def _sc_combine(ys, destw, gates):
    n_win, _, W = destw.shape
    w = ys.shape[1]
    per_worker = n_win // (SC_CORES * SC_SUBCORES)
    unroll = 4

    def body(y_hbm, d_hbm, g_hbm, o_hbm, rows_v, g_v, out_v, sem, *idx_v):
        first = _sc_worker() * per_worker

        @pl.loop(0, per_worker)
        def _(i):
            win = first + i
            for k in range(TOP_K):
                pltpu.sync_copy(d_hbm.at[win, k], idx_v[k])
            pltpu.sync_copy(g_hbm.at[win], g_v)
            gathers = [pltpu.async_copy(y_hbm.at[idx_v[k]], rows_v.at[k], sem)
                       for k in range(TOP_K)]
            for c in gathers:
                c.wait()

            @pl.loop(0, W)
            def _(j):
                gate = [g_v[pl.ds((j * TOP_K + k) * SC_LANES, SC_LANES)] for k in range(TOP_K)]

                @pl.loop(0, w // (SC_LANES * unroll))
                def _(c):
                    for u in range(unroll):
                        col = (c * unroll + u) * SC_LANES
                        lo = hi = None
                        for k in range(TOP_K):
                            p = rows_v[k, j, pl.ds(col, SC_LANES)]
                            lk = plsc.bitcast(p << 16, F32) * gate[k]
                            hk = plsc.bitcast(p & jnp.uint32(0xFFFF0000), F32) * gate[k]
                            lo = lk if lo is None else lo + lk
                            hi = hk if hi is None else hi + hk
                        out_v[j, pl.ds(col, SC_LANES)] = lo
                        out_v[j, pl.ds(w + col, SC_LANES)] = hi

            pltpu.sync_copy(out_v, o_hbm.at[pl.ds(win * W, W)])

    return pl.kernel(
        body,
        out_type=jax.ShapeDtypeStruct((n_win * W, 2 * w), F32),
        mesh=_sc_mesh(),
        scratch_types=[pltpu.VMEM((TOP_K, W, w), jnp.uint32),
                       pltpu.VMEM((W * TOP_K * SC_LANES,), F32), pltpu.VMEM((W, 2 * w), F32),
                       pltpu.SemaphoreType.DMA] + [pltpu.VMEM((W,), jnp.int32)] * TOP_K,
        compiler_params=pltpu.CompilerParams(needs_layout_passes=False),
        name="sc_combine",
    )(ys, destw, gates)


def _experts_kernel(be_ref, nv_ref, xs_ref, wg_ref, wu_ref, wd_ref, ys_ref):
    del be_ref
    n_valid = nv_ref[pl.program_id(0)]
    mb, w = xs_ref.shape

    @pl.when(n_valid > 0)
    def _():
        keep = lax.broadcasted_iota(jnp.int32, (mb, w), 0) < n_valid
        lo, hi = _unpack_rows(jnp.where(keep, xs_ref[...], jnp.uint32(0)))
        lo, hi = lo.astype(BF16), hi.astype(BF16)

        def proj(w_ref):
            return (jnp.dot(lo, w_ref[0, :w, :], preferred_element_type=F32)
                    + jnp.dot(hi, w_ref[0, w:, :], preferred_element_type=F32))

        hb = jax.nn.silu(proj(wg_ref)) * proj(wu_ref)
        ys_ref[...] = _pack_rows(jnp.dot(hb.astype(BF16), wd_ref[0], preferred_element_type=F32))

    @pl.when(n_valid <= 0)
    def _():
        ys_ref[...] = jnp.zeros_like(ys_ref)


def _experts(xs, blk_exp, n_valid, p):
    n_rows, w = xs.shape
    mb = n_rows // blk_exp.shape[0]
    grid_spec = pltpu.PrefetchScalarGridSpec(
        num_scalar_prefetch=2,
        grid=(n_rows // mb,),
        in_specs=[
            pl.BlockSpec((mb, w), lambda i, be, nv: (i, 0)),
            pl.BlockSpec((1, D_MODEL, EXPERT_DIM), lambda i, be, nv: (be[i], 0, 0)),
            pl.BlockSpec((1, D_MODEL, EXPERT_DIM), lambda i, be, nv: (be[i], 0, 0)),
            pl.BlockSpec((1, EXPERT_DIM, D_MODEL), lambda i, be, nv: (be[i], 0, 0)),
        ],
        out_specs=pl.BlockSpec((mb, w), lambda i, be, nv: (i, 0)),
    )
    return pl.pallas_call(
        _experts_kernel,
        grid_spec=grid_spec,
        out_shape=jax.ShapeDtypeStruct((n_rows, w), jnp.uint32),
        compiler_params=pltpu.CompilerParams(
            dimension_semantics=("arbitrary",), vmem_limit_bytes=VMEM_LIMIT),
        name="experts",
    )(blk_exp, n_valid, xs, p["w_e_gate"], p["w_e_up"], p["w_e_down"])


def _final_kernel(x1_ref, routed_ref, wsg_ref, wsu_ref, wsd_ref, l2g_ref, l2b_ref, out_ref):
    x1 = x1_ref[...]
    xb = x1.astype(BF16)
    hs = (jax.nn.silu(jnp.dot(xb, wsg_ref[...], preferred_element_type=F32))
          * jnp.dot(xb, wsu_ref[...], preferred_element_type=F32))
    shared = jnp.dot(hs.astype(BF16), wsd_ref[...], preferred_element_type=F32)
    out_ref[...] = _layer_norm(ALPHA * x1 + (routed_ref[...] + shared), l2g_ref[...], l2b_ref[...])


def _final(x1, routed, p):
    T = x1.shape[0]
    tm = TOKEN_TILE
    tok = lambda w: pl.BlockSpec((tm, w), lambda i: (i, 0))
    return pl.pallas_call(
        _final_kernel,
        grid=(T // tm,),
        in_specs=[
            tok(D_MODEL), tok(D_MODEL),
            _const_spec((D_MODEL, SHARED_DIM)), _const_spec((D_MODEL, SHARED_DIM)),
            _const_spec((SHARED_DIM, D_MODEL)),
            _const_spec((1, D_MODEL)), _const_spec((1, D_MODEL)),
        ],
        out_specs=tok(D_MODEL),
        out_shape=jax.ShapeDtypeStruct((T, D_MODEL), F32),
        compiler_params=pltpu.CompilerParams(
            dimension_semantics=("parallel",), vmem_limit_bytes=VMEM_LIMIT),
        name="final",
    )(x1, routed, p["w_sh_gate"], p["w_sh_up"], p["w_sh_down"], p["ln2_g"], p["ln2_b"])


def _expert_block(n_tokens):
    mean_rows = n_tokens * TOP_K // N_EXPERTS
    return EXPERT_BLOCK_LARGE if mean_rows >= 2 * EXPERT_BLOCK_LARGE else EXPERT_BLOCK


def _routing_layout(route, counts, n_tokens):
    mb = _expert_block(n_tokens)
    n_blocks = -(-n_tokens * TOP_K // mb) + N_EXPERTS
    idx = route[0:TOP_K].astype(jnp.int32)
    rank = route[8:8 + TOP_K].astype(jnp.int32)
    counts = counts.astype(jnp.int32)
    padded = (counts + mb - 1) // mb * mb
    pad_end = jnp.cumsum(padded)
    pad_start = pad_end - padded
    experts = jnp.arange(N_EXPERTS, dtype=jnp.int32)
    dest = rank + jnp.sum(jnp.where(idx[:, :, None] == experts, pad_start, 0), axis=-1)
    windows = lambda W: dest.reshape(TOP_K, n_tokens // W, W).transpose(1, 0, 2)
    blk_start = jnp.arange(n_blocks, dtype=jnp.int32) * mb
    blk_exp = jnp.minimum(
        jnp.sum((pad_end[None, :] <= blk_start[:, None]).astype(jnp.int32), axis=1), N_EXPERTS - 1)
    valid_end = jnp.sum(jnp.where(blk_exp[:, None] == experts, pad_start + counts, 0), axis=-1)
    n_valid = jnp.clip(valid_end - blk_start, 0, mb).astype(jnp.int32)
    gates = jnp.broadcast_to(
        route[16:16 + TOP_K].T.reshape(n_tokens // SC_COMBINE_WINDOW, SC_COMBINE_WINDOW, TOP_K, 1),
        (n_tokens // SC_COMBINE_WINDOW, SC_COMBINE_WINDOW, TOP_K, SC_LANES)).reshape(
            n_tokens // SC_COMBINE_WINDOW, -1)
    return windows(SC_WINDOW), windows(SC_COMBINE_WINDOW), gates, blk_exp, n_valid, n_blocks * mb


def _encode(x, p):
    batch, seq, _ = x.shape
    T = batch * seq
    xt = x.reshape(T, D_MODEL)
    ag, q, gf, gb, iv, so, sgb, gmin = _inproj(xt, p)
    of, ob = _hgrn(q, gf, gb, iv, gmin, batch)
    x1, x1p, route, cnt = _mix(xt, of, ob, so, sgb, ag, p)
    dest_d, dest_c, gates, blk_exp, n_valid, n_rows = _routing_layout(route, cnt[:, 0], T)
    xs = _sc_dispatch(x1p, dest_d, n_rows)
    ys = _experts(xs, blk_exp, n_valid, p)
    out = _final(x1, _sc_combine(ys, dest_c, gates), p)
    return out.reshape(batch, seq, D_MODEL)


def _prepare_params(ln_in_g, ln_in_b, w_in, a_ln_g, a_ln_b, a_ws, a_sb, hg_lb_logits, hg_norm_g,
                    w_pa, w_pb, w_o, ln1_g, ln1_b, w_router, router_bias, w_e_gate, w_e_up,
                    w_e_down, w_sh_gate, w_sh_up, w_sh_down, ln2_g, ln2_b):
    l = 0
    row = lambda v: v.reshape(1, -1).astype(F32)
    ws = a_ws[l].astype(BF16)
    wsp = jnp.concatenate([ws[0::2], ws[1::2]], axis=2)
    sbf = jnp.repeat(a_sb[l].astype(F32), A_WIDTH // A_GROUPS, axis=1)
    lb = jnp.cumsum(jax.nn.softmax(hg_lb_logits.astype(F32), axis=1), axis=1)[:, l]
    return dict(
        ln_in_g=row(ln_in_g), ln_in_b=row(ln_in_b), w_in=w_in[l].astype(BF16),
        a_ln_g=row(a_ln_g[l]), a_ln_b=row(a_ln_b[l]), wsp=wsp, sbf=sbf, lb=lb,
        w_pa=w_pa[l].astype(BF16), hg_norm_g=row(hg_norm_g[l]),
        w_pb=w_pb[l].astype(BF16), w_o=w_o[l].astype(BF16),
        ln1_g=row(ln1_g[l]), ln1_b=row(ln1_b[l]),
        w_router=w_router[l].T.astype(BF16),
        router_bias=jnp.broadcast_to(router_bias[l].astype(F32)[:, None], (N_EXPERTS, LANES)),
        w_e_gate=w_e_gate[l].astype(BF16), w_e_up=w_e_up[l].astype(BF16),
        w_e_down=w_e_down[l].astype(BF16),
        w_sh_gate=w_sh_gate[l].astype(BF16), w_sh_up=w_sh_up[l].astype(BF16),
        w_sh_down=w_sh_down[l].astype(BF16),
        ln2_g=row(ln2_g[l]), ln2_b=row(ln2_b[l]),
    )


def kernel(x_prompt, x_sample, ln_in_g, ln_in_b, w_in, a_ln_g, a_ln_b, a_ws, a_sb, hg_lb_logits,
           hg_norm_g, w_pa, w_pb, w_o, ln1_g, ln1_b, w_router, router_bias, w_e_gate, w_e_up,
           w_e_down, w_sh_gate, w_sh_up, w_sh_down, ln2_g, ln2_b):
    p = _prepare_params(ln_in_g, ln_in_b, w_in, a_ln_g, a_ln_b, a_ws, a_sb, hg_lb_logits, hg_norm_g,
                        w_pa, w_pb, w_o, ln1_g, ln1_b, w_router, router_bias, w_e_gate, w_e_up,
                        w_e_down, w_sh_gate, w_sh_up, w_sh_down, ln2_g, ln2_b)
    return _encode(x_prompt, p), _encode(x_sample, p)
```

```python
import functools

import jax
import jax.numpy as jnp
from jax import lax
from jax.experimental import pallas as pl
from jax.experimental.pallas import tpu as pltpu
from jax.experimental.pallas import tpu_sc as plsc

F32 = jnp.float32
BF16 = jnp.bfloat16

D_MODEL = 1024
A_GROUPS = 8
A_WIDTH = 512
A_CHUNK = 128
HG_HEADS = 8
HG_DK = 128
HG_WIDTH = HG_HEADS * HG_DK
N_IN = 2 * A_WIDTH + 5 * HG_WIDTH + 2 * D_MODEL
N_EXPERTS = 64
TOP_K = 6
N_GROUPS = 8
TOPK_GROUPS = 4
GROUP_SIZE = N_EXPERTS // N_GROUPS
EXPERT_DIM = 256
SHARED_DIM = 256
ROUTED_SCALE = 2.5
DEPTH = 1
ALPHA = (2.0 * DEPTH) ** 0.25
LN_EPS = 1e-5
RMS_EPS = 1e-6

LANES = 128
TOKEN_TILE = 256
INPROJ_TILE = 256
HG_CHUNK = 128
HG_SAFE_LOGDECAY = 80.0
EXPERT_BLOCK = 512
EXPERT_BLOCK_LARGE = 1024
SC_CORES = 2
SC_SUBCORES = 16
SC_LANES = 16
SC_WINDOW = 32
SC_COMBINE_WINDOW = 8
VMEM_LIMIT = 56 * 1024 * 1024

_O_U, _O_V, _O_Q, _O_FF, _O_FB, _O_I, _O_G, _O_GA, _O_GB = (
    0, 512, 1024, 2048, 3072, 4096, 5120, 6144, 7168)


def _layer_norm(x, g, b):
    mu = jnp.mean(x, axis=-1, keepdims=True)
    xc = x - mu
    var = jnp.mean(xc * xc, axis=-1, keepdims=True)
    return xc * lax.rsqrt(var + LN_EPS) * g + b


def _bdot(a, b):
    return jnp.dot(a.astype(BF16), b.astype(BF16), preferred_element_type=F32)


def _pack_rows(x):
    w = x.shape[1] // 2
    lo = lax.bitcast_convert_type(x[:, :w].astype(BF16).astype(F32), jnp.uint32)
    hi = lax.bitcast_convert_type(x[:, w:].astype(BF16).astype(F32), jnp.uint32)
    return hi | (lo >> 16)


def _unpack_rows(p):
    lo = lax.bitcast_convert_type(p << 16, F32)
    hi = lax.bitcast_convert_type(p & jnp.uint32(0xFFFF0000), F32)
    return lo, hi


def _const_spec(shape):
    nd = len(shape)
    return pl.BlockSpec(shape, lambda *_: (0,) * nd, pipeline_mode=pl.Buffered(1))


def _inproj_kernel(x_ref, lng_ref, lnb_ref, win_ref, alng_ref, alnb_ref, wsp_ref, sbf_ref, lb_ref,
                   wpa_ref, ag_ref, q_ref, gf_ref, gb_ref, iv_ref, so_ref, sgb_ref, gmin_ref):
    tm = x_ref.shape[0]
    xb = _layer_norm(x_ref[...], lng_ref[...], lnb_ref[...]).astype(BF16)

    def sec(lo, width):
        return jnp.dot(xb, win_ref[:, lo:lo + width], preferred_element_type=F32)

    u = jax.nn.gelu(sec(_O_U, A_WIDTH))
    v = _layer_norm(jax.nn.gelu(sec(_O_V, A_WIDTH)), alng_ref[...], alnb_ref[...]).astype(BF16)
    lane = lax.broadcasted_iota(jnp.int32, (A_CHUNK, LANES), 1)
    left = lane < (A_WIDTH // A_GROUPS)
    zero = jnp.zeros((A_CHUNK, LANES), BF16)
    chunks = []
    for c in range(tm // A_CHUNK):
        vc = v[c * A_CHUNK:(c + 1) * A_CHUNK]
        cols = []
        for p in range(A_GROUPS // 2):
            vp = vc[:, p * LANES:(p + 1) * LANES]
            rhs = jnp.concatenate([jnp.where(left, vp, zero), jnp.where(left, zero, vp)], axis=0)
            cols.append(jnp.dot(wsp_ref[p], rhs, preferred_element_type=F32))
        chunks.append(jnp.concatenate(cols, axis=1) + sbf_ref[...])
    mixed = jnp.concatenate(chunks, axis=0)
    a = _bdot(u * mixed, wpa_ref[...])
    ag_ref[...] = (jax.nn.sigmoid(sec(_O_GA, D_MODEL)) * a).astype(BF16)

    q_ref[...] = jax.nn.silu(sec(_O_Q, HG_WIDTH)).astype(BF16)
    mins = []
    for d, (off, g_ref) in enumerate(((_O_FF, gf_ref), (_O_FB, gb_ref))):
        lb = lb_ref[d:d + 1, :]
        f = lb + (1.0 - lb) * jax.nn.sigmoid(sec(off, HG_WIDTH))
        g = jnp.log(f)
        g_ref[...] = g
        half = jnp.sum(g.reshape(tm // (HG_CHUNK // 2), HG_CHUNK // 2, HG_WIDTH), axis=1)
        mins.append(jnp.min(half, axis=-1, keepdims=True))
    gmin_ref[0] = jnp.broadcast_to(jnp.concatenate(mins, axis=0), gmin_ref.shape[1:])
    iv_ref[...] = sec(_O_I, HG_WIDTH).astype(BF16)
    so_ref[...] = jax.nn.silu(sec(_O_G, HG_WIDTH)).astype(BF16)
    sgb_ref[...] = jax.nn.sigmoid(sec(_O_GB, D_MODEL)).astype(BF16)


def _inproj(x, p):
    T = x.shape[0]
    tm = INPROJ_TILE
    nt = T // tm
    nh = 2 * tm // (HG_CHUNK // 2)
    tok = lambda w: pl.BlockSpec((tm, w), lambda i: (i, 0))
    outs = (
        jax.ShapeDtypeStruct((T, D_MODEL), BF16),
        jax.ShapeDtypeStruct((T, HG_WIDTH), BF16),
        jax.ShapeDtypeStruct((T, HG_WIDTH), F32),
        jax.ShapeDtypeStruct((T, HG_WIDTH), F32),
        jax.ShapeDtypeStruct((T, HG_WIDTH), BF16),
        jax.ShapeDtypeStruct((T, HG_WIDTH), BF16),
        jax.ShapeDtypeStruct((T, D_MODEL), BF16),
        jax.ShapeDtypeStruct((nt, nh, LANES), F32),
    )
    return pl.pallas_call(
        _inproj_kernel,
        grid=(nt,),
        in_specs=[
            tok(D_MODEL),
            _const_spec((1, D_MODEL)), _const_spec((1, D_MODEL)),
            _const_spec((D_MODEL, N_IN)),
            _const_spec((1, A_WIDTH)), _const_spec((1, A_WIDTH)),
            _const_spec((A_GROUPS // 2, A_CHUNK, 2 * A_CHUNK)),
            _const_spec((A_CHUNK, A_WIDTH)),
            _const_spec((2, HG_WIDTH)),
            _const_spec((A_WIDTH, D_MODEL)),
        ],
        out_specs=(tok(D_MODEL), tok(HG_WIDTH), tok(HG_WIDTH), tok(HG_WIDTH), tok(HG_WIDTH),
                   tok(HG_WIDTH), tok(D_MODEL), pl.BlockSpec((1, nh, LANES), lambda i: (i, 0, 0))),
        out_shape=outs,
        compiler_params=pltpu.CompilerParams(
            dimension_semantics=("parallel",), vmem_limit_bytes=VMEM_LIMIT),
        name="inproj",
    )(x, p["ln_in_g"], p["ln_in_b"], p["w_in"], p["a_ln_g"], p["a_ln_b"], p["wsp"], p["sbf"],
      p["lb"], p["w_pa"])


_NT = (((1,), (1,)), ((), ()))
_TN = (((0,), (0,)), ((), ()))


def _hgrn_direction(q_ref, g_ref, v_ref, st_ref, o_ref, tri, mask, fwd):
    C = q_ref.shape[0]
    g = g_ref[...]
    ghi = g.astype(BF16)
    glo = (g - ghi.astype(F32)).astype(BF16)
    b = jnp.dot(jnp.concatenate([tri, tri], axis=1), jnp.concatenate([ghi, glo], axis=0),
                preferred_element_type=F32)
    mid = C // 2 - 1 if fwd else C // 2
    end = C - 1 if fwd else 0
    r = b[mid:mid + 1, :]
    b_end = b[end:end + 1, :]
    qt = q_ref[...].astype(F32) * jnp.exp(b - r)
    kt = (1.0 - jnp.exp(g)) * jnp.exp(r - b)
    qtb = qt.astype(BF16)
    ktb = kt.astype(BF16)
    qhb = (qt * jnp.exp(r)).astype(BF16)
    khb = (kt * jnp.exp(b_end - r)).astype(BF16)
    decay = jnp.exp(b_end)
    v32 = v_ref[...].astype(F32)
    heads = [slice(h * HG_DK, (h + 1) * HG_DK) for h in range(HG_HEADS)]
    zero = jnp.zeros((C, HG_DK), BF16)
    mask2 = jnp.concatenate([mask, mask], axis=1)
    scores = []
    for p in range(HG_HEADS // 2):
        k1, k2 = ktb[:, heads[2 * p]], ktb[:, heads[2 * p + 1]]
        kk = jnp.concatenate([jnp.concatenate([k1, zero], axis=1),
                              jnp.concatenate([zero, k2], axis=1)], axis=0)
        s2 = lax.dot_general(qtb[:, 2 * p * HG_DK:(2 * p + 2) * HG_DK], kk, _NT,
                             preferred_element_type=F32)
        s2 = jnp.where(mask2, s2, 0.0).astype(BF16)
        scores += [s2[:, :C], s2[:, C:]]
    for h, sl in enumerate(heads):
        st = st_ref[h]
        vt = v32[:, sl].T.astype(BF16)
        o_ref[:, sl] = lax.dot_general(
            jnp.concatenate([scores[h], qhb[:, sl]], axis=1),
            jnp.concatenate([vt, st.astype(BF16)], axis=1), _NT,
            preferred_element_type=F32).astype(o_ref.dtype)
        st_ref[h] = st * decay[:, sl] + jnp.dot(vt, khb[:, sl], preferred_element_type=F32)


def _hgrn_direction_stepwise(q_ref, g_ref, v_ref, st_ref, o_ref, q32_ref, v32_ref, o32_ref, fwd):
    C = q_ref.shape[0]
    q32_ref[...] = q_ref[...].astype(F32)
    v32_ref[...] = v_ref[...].astype(F32)
    sub = 8
    rows = lax.broadcasted_iota(jnp.int32, (sub, HG_DK), 0)

    def group(i, carry):
        base = pl.multiple_of((i if fwd else C // sub - 1 - i) * sub, sub)
        f = jnp.exp(g_ref[pl.ds(base, sub), :])
        k = 1.0 - f
        q = q32_ref[pl.ds(base, sub), :]
        v = v32_ref[pl.ds(base, sub), :]
        for h in range(HG_HEADS):
            sl = slice(h * HG_DK, (h + 1) * HG_DK)
            st = st_ref[h]
            out = jnp.zeros((sub, HG_DK), F32)
            for r in (range(sub) if fwd else range(sub - 1, -1, -1)):
                v_t = jnp.where(rows == 0, v[r:r + 1, sl], 0.0).astype(BF16)
                k_t = jnp.broadcast_to(k[r:r + 1, sl], (sub, HG_DK)).astype(BF16)
                st = st * f[r:r + 1, sl] + lax.dot_general(v_t, k_t, _TN, preferred_element_type=F32)
                q_t = jnp.broadcast_to(q[r:r + 1, sl], (sub, HG_DK)).astype(BF16)
                o_t = lax.dot_general(q_t, st.astype(BF16), _NT, preferred_element_type=F32)
                out = jnp.where(rows == r, o_t, out)
            st_ref[h] = st
            o32_ref[pl.ds(base, sub), sl] = out
        return carry

    lax.fori_loop(0, C // sub, group, 0)
    o_ref[...] = o32_ref[...].astype(o_ref.dtype)


def _hgrn_kernel(safe_ref, qf_ref, qb_ref, gf_ref, gb_ref, vf_ref, vb_ref, tril_ref, triu_ref,
                 of_ref, ob_ref, sf_ref, sb_ref, q32_ref, v32_ref, o32_ref):
    b, j = pl.program_id(0), pl.program_id(1)
    nc = pl.num_programs(1)

    @pl.when(j == 0)
    def _():
        sf_ref[...] = jnp.zeros_like(sf_ref)
        sb_ref[...] = jnp.zeros_like(sb_ref)

    C = qf_ref.shape[0]
    row = lax.broadcasted_iota(jnp.int32, (C, C), 0)
    col = lax.broadcasted_iota(jnp.int32, (C, C), 1)
    safe_f = safe_ref[0, b * nc + j] != 0
    safe_b = safe_ref[1, b * nc + nc - 1 - j] != 0

    def forward(stepwise):
        if stepwise:
            _hgrn_direction_stepwise(qf_ref, gf_ref, vf_ref, sf_ref, of_ref, q32_ref, v32_ref, o32_ref, True)
        else:
            _hgrn_direction(qf_ref, gf_ref, vf_ref, sf_ref, of_ref, tril_ref[...], row >= col, True)

    def backward(stepwise):
        if stepwise:
            _hgrn_direction_stepwise(qb_ref, gb_ref, vb_ref, sb_ref, ob_ref, q32_ref, v32_ref, o32_ref, False)
        else:
            _hgrn_direction(qb_ref, gb_ref, vb_ref, sb_ref, ob_ref, triu_ref[...], row <= col, False)

    @pl.when(safe_f & safe_b)
    def _():
        forward(False)
        backward(False)

    @pl.when(jnp.logical_not(safe_f & safe_b))
    def _():
        @pl.when(safe_f)
        def _():
            forward(False)

        @pl.when(jnp.logical_not(safe_f))
        def _():
            forward(True)

        @pl.when(safe_b)
        def _():
            backward(False)

        @pl.when(jnp.logical_not(safe_b))
        def _():
            backward(True)


def _hgrn(q, gf, gb, iv, gmin, batch):
    T = q.shape[0]
    C = HG_CHUNK
    nc = T // batch // C
    nt, nh, _ = gmin.shape
    halves = gmin[:, :, 0].reshape(nt, 2, nh // 4, 2)
    safe = (jnp.min(halves, axis=-1) > -HG_SAFE_LOGDECAY).astype(jnp.int32)
    safe = safe.transpose(1, 0, 2).reshape(2, T // C)
    fwd = pl.BlockSpec((C, HG_WIDTH), lambda b, j, s: (b * nc + j, 0))
    bwd = pl.BlockSpec((C, HG_WIDTH), lambda b, j, s: (b * nc + nc - 1 - j, 0))
    const = lambda shape: pl.BlockSpec(shape, lambda b, j, s: (0,) * len(shape),
                                       pipeline_mode=pl.Buffered(1))
    row = lax.broadcasted_iota(jnp.int32, (C, C), 0)
    col = lax.broadcasted_iota(jnp.int32, (C, C), 1)
    tril = (row >= col).astype(BF16)
    triu = (row <= col).astype(BF16)
    grid_spec = pltpu.PrefetchScalarGridSpec(
        num_scalar_prefetch=1,
        grid=(batch, nc),
        in_specs=[fwd, bwd, fwd, bwd, fwd, bwd, const((C, C)), const((C, C))],
        out_specs=(fwd, bwd),
        scratch_shapes=[pltpu.VMEM((HG_HEADS, HG_DK, HG_DK), F32),
                        pltpu.VMEM((HG_HEADS, HG_DK, HG_DK), F32),
                        pltpu.VMEM((C, HG_WIDTH), F32), pltpu.VMEM((C, HG_WIDTH), F32),
                        pltpu.VMEM((C, HG_WIDTH), F32)],
    )
    return pl.pallas_call(
        _hgrn_kernel,
        grid_spec=grid_spec,
        out_shape=(jax.ShapeDtypeStruct((T, HG_WIDTH), BF16), jax.ShapeDtypeStruct((T, HG_WIDTH), BF16)),
        compiler_params=pltpu.CompilerParams(
            dimension_semantics=("parallel", "arbitrary"), vmem_limit_bytes=VMEM_LIMIT),
        name="hgrn",
    )(safe, q, q, gf, gb, iv, iv, tril, triu)


def _mix_kernel(x_ref, lng_ref, lnb_ref, of_ref, ob_ref, so_ref, sgb_ref, ag_ref, ng_ref, wpb_ref,
                wo_ref, l1g_ref, l1b_ref, wr_ref, rb_ref, tri_ref, x1_ref, x1p_ref, route_ref,
                cnt_ref, carry_ref):
    tm = x_ref.shape[0]

    @pl.when(pl.program_id(0) == 0)
    def _():
        carry_ref[...] = jnp.zeros_like(carry_ref)

    o = of_ref[...].astype(F32) + ob_ref[...].astype(F32)
    heads = []
    for h in range(HG_HEADS):
        oh = o[:, h * HG_DK:(h + 1) * HG_DK]
        heads.append(oh * lax.rsqrt(jnp.mean(oh * oh, axis=-1, keepdims=True) + RMS_EPS))
    rn = jnp.concatenate(heads, axis=1) * ng_ref[...] * so_ref[...].astype(F32)
    r = _bdot(rn, wpb_ref[...])
    mixed = ag_ref[...].astype(F32) + sgb_ref[...].astype(F32) * r
    y = _bdot(mixed, wo_ref[...])
    xn = _layer_norm(x_ref[...], lng_ref[...], lnb_ref[...])
    x1 = _layer_norm(ALPHA * xn + y, l1g_ref[...], l1b_ref[...])
    x1_ref[...] = x1
    x1p_ref[...] = _pack_rows(x1)

    neg = jnp.float32(-jnp.inf)
    reps = tm // LANES
    scores = jax.nn.sigmoid(lax.dot_general(wr_ref[...], x1.astype(BF16), _NT,
                                            preferred_element_type=F32))
    biased = (scores + jnp.concatenate([rb_ref[...]] * reps, axis=1)).reshape(
        N_GROUPS, GROUP_SIZE, tm)
    sub = lax.broadcasted_iota(jnp.int32, biased.shape, 1).astype(F32)
    m1 = jnp.max(biased, axis=1, keepdims=True)
    first = jnp.min(jnp.where(biased == m1, sub, float(GROUP_SIZE)), axis=1, keepdims=True)
    m2 = jnp.max(jnp.where(sub == first, neg, biased), axis=1, keepdims=True)
    gs = (m1 + m2).reshape(N_GROUPS, tm)
    grp = lax.broadcasted_iota(jnp.int32, (N_GROUPS, tm), 0)
    ahead = jnp.zeros((N_GROUPS, tm), F32)
    for d in range(1, N_GROUPS):
        other = pltpu.roll(gs, d, 0)
        tie = jnp.where(grp >= d, 1.0, 0.0)
        ahead = ahead + jnp.where(other > gs, 1.0, jnp.where(other == gs, tie, 0.0))
    keep = (ahead < TOPK_GROUPS).reshape(N_GROUPS, 1, tm)
    allowed = jnp.where(keep, biased, neg).reshape(N_EXPERTS, tm)
    row = lax.broadcasted_iota(jnp.int32, (N_EXPERTS, tm), 0).astype(F32)
    sel = jnp.zeros((N_EXPERTS, tm), F32)
    picks = []
    for _ in range(TOP_K):
        m = jnp.max(allowed, axis=0, keepdims=True)
        first = jnp.min(jnp.where(allowed == m, row, float(N_EXPERTS)), axis=0, keepdims=True)
        hit = row == first
        picks.append((first, hit, jnp.sum(jnp.where(hit, scores, 0.0), axis=0, keepdims=True)))
        sel = jnp.where(hit, 1.0, sel)
        allowed = jnp.where(hit, neg, allowed)
    wsum = picks[0][2]
    for pk in picks[1:]:
        wsum = wsum + pk[2]
    selb = sel.astype(BF16)
    carry = carry_ref[...]
    before = (jnp.dot(selb, tri_ref[...], preferred_element_type=F32)
              + jnp.concatenate([carry] * reps, axis=1))
    total = carry + jnp.dot(selb, jnp.ones((tm, LANES), BF16), preferred_element_type=F32)
    carry_ref[...] = total
    cnt_ref[...] = total
    blank = [jnp.zeros((1, tm), F32)] * (8 - TOP_K)
    route_ref[...] = jnp.concatenate(
        [pk[0] for pk in picks] + blank
        + [jnp.sum(jnp.where(pk[1], before, 0.0), axis=0, keepdims=True) for pk in picks] + blank
        + [pk[2] / wsum * ROUTED_SCALE for pk in picks] + blank, axis=0)


def _mix(x, of, ob, so, sgb, ag, p):
    T = x.shape[0]
    tm = TOKEN_TILE
    tok = lambda w: pl.BlockSpec((tm, w), lambda i: (i, 0))
    row = lax.broadcasted_iota(jnp.int32, (tm, tm), 0)
    col = lax.broadcasted_iota(jnp.int32, (tm, tm), 1)
    tri = (row < col).astype(BF16)
    return pl.pallas_call(
        _mix_kernel,
        grid=(T // tm,),
        in_specs=[
            tok(D_MODEL), _const_spec((1, D_MODEL)), _const_spec((1, D_MODEL)),
            tok(HG_WIDTH), tok(HG_WIDTH), tok(HG_WIDTH), tok(D_MODEL), tok(D_MODEL),
            _const_spec((1, HG_WIDTH)),
            _const_spec((HG_WIDTH, D_MODEL)), _const_spec((D_MODEL, D_MODEL)),
            _const_spec((1, D_MODEL)), _const_spec((1, D_MODEL)),
            _const_spec((N_EXPERTS, D_MODEL)), _const_spec((N_EXPERTS, LANES)),
            _const_spec((tm, tm)),
        ],
        out_specs=(tok(D_MODEL), tok(D_MODEL // 2), pl.BlockSpec((24, tm), lambda i: (0, i)),
                   pl.BlockSpec((N_EXPERTS, LANES), lambda i: (0, 0))),
        out_shape=(jax.ShapeDtypeStruct((T, D_MODEL), F32),
                   jax.ShapeDtypeStruct((T, D_MODEL // 2), jnp.uint32),
                   jax.ShapeDtypeStruct((24, T), F32),
                   jax.ShapeDtypeStruct((N_EXPERTS, LANES), F32)),
        scratch_shapes=[pltpu.VMEM((N_EXPERTS, LANES), F32)],
        compiler_params=pltpu.CompilerParams(
            dimension_semantics=("arbitrary",), vmem_limit_bytes=VMEM_LIMIT),
        name="mix",
    )(x, p["ln_in_g"], p["ln_in_b"], of, ob, so, sgb, ag, p["hg_norm_g"], p["w_pb"], p["w_o"],
      p["ln1_g"], p["ln1_b"], p["w_router"], p["router_bias"], tri)


def _sc_mesh():
    return plsc.VectorSubcoreMesh(core_axis_name="c", subcore_axis_name="s",
                                  num_cores=SC_CORES, num_subcores=SC_SUBCORES)


def _sc_worker():
    return lax.axis_index("s") * SC_CORES + lax.axis_index("c")


def _sc_dispatch(x1p, destw, n_rows):
    T, w = x1p.shape
    n_win, _, W = destw.shape
    per_worker = n_win // (SC_CORES * SC_SUBCORES)

    def body(x_hbm, d_hbm, o_hbm, rows_v, idx_v, sem):
        first = _sc_worker() * per_worker

        @pl.loop(0, per_worker)
        def _(j):
            win = first + j
            pltpu.sync_copy(x_hbm.at[pl.ds(win * W, W)], rows_v)
            pltpu.sync_copy(d_hbm.at[win], idx_v)
            copies = [pltpu.async_copy(rows_v, o_hbm.at[idx_v.at[k]], sem) for k in range(TOP_K)]
            for c in copies:
                c.wait()

    return pl.kernel(
        body,
        out_type=jax.ShapeDtypeStruct((n_rows, w), jnp.uint32),
        mesh=_sc_mesh(),
        scratch_types=[pltpu.VMEM((W, w), jnp.uint32), pltpu.VMEM((TOP_K, W), jnp.int32),
                       pltpu.SemaphoreType.DMA],
        name="sc_dispatch",
    )(x1p, destw)


def _sc_combine(ys, destw, gates):
    n_win, _, W = destw.shape
    w = ys.shape[1]
    per_worker = n_win // (SC_CORES * SC_SUBCORES)
    assert per_worker % 2 == 0 and per_worker * SC_CORES * SC_SUBCORES == n_win

    def body(y_hbm, d_hbm, g_hbm, o_hbm, *scratch):
        slots = [dict(rows=scratch[s], gate=scratch[2 + s], out=scratch[4 + s], gsem=scratch[6 + s],
                      wsem=scratch[8 + s], idx=scratch[10 + s * TOP_K:10 + (s + 1) * TOP_K])
                 for s in range(2)]
        first = _sc_worker() * per_worker

        def gathers(b):
            return [pltpu.make_async_copy(y_hbm.at[b["idx"][k]], b["rows"].at[k], b["gsem"])
                    for k in range(TOP_K)]

        def write_back(b, win):
            return pltpu.make_async_copy(b["out"], o_hbm.at[pl.ds(win * W, W)], b["wsem"])

        def fetch(b, win):
            for k in range(TOP_K):
                pltpu.sync_copy(d_hbm.at[win, k], b["idx"][k])
            pltpu.sync_copy(g_hbm.at[win], b["gate"])
            for c in gathers(b):
                c.start()

        def reduce_rows(b):
            rows, gate_v, out = b["rows"], b["gate"], b["out"]

            @pl.loop(0, W)
            def _(j):
                gate = [gate_v[pl.ds((j * TOP_K + k) * SC_LANES, SC_LANES)] for k in range(TOP_K)]

                @plsc.parallel_loop(0, w, step=SC_LANES, unroll=4)
                def _(col):
                    lo, hi = [], []
                    for k in range(TOP_K):
                        p = rows[k, j, pl.ds(col, SC_LANES)]
                        lo.append(plsc.bitcast(p << 16, F32) * gate[k])
                        hi.append(plsc.bitcast(p & jnp.uint32(0xFFFF0000), F32) * gate[k])
                    out[j, pl.ds(col, SC_LANES)] = (lo[0] + lo[1]) + (lo[2] + lo[3]) + (lo[4] + lo[5])
                    out[j, pl.ds(w + col, SC_LANES)] = (hi[0] + hi[1]) + (hi[2] + hi[3]) + (hi[4] + hi[5])

        def process(b, win, not_first):
            for c in gathers(b):
                c.wait()

            @pl.when(not_first)
            def _():
                write_back(b, win).wait()

            reduce_rows(b)
            write_back(b, win).start()

        fetch(slots[0], first)

        @pl.loop(0, per_worker, step=2)
        def _(i):
            win = first + i
            fetch(slots[1], win + 1)
            process(slots[0], win, i > 0)

            @pl.when(i + 2 < per_worker)
            def _():
                fetch(slots[0], win + 2)

            process(slots[1], win + 1, i > 0)

        for b in slots:
            write_back(b, first).wait()

    slot_types = ([pltpu.VMEM((TOP_K, W, w), jnp.uint32)] * 2
                  + [pltpu.VMEM((W * TOP_K * SC_LANES,), F32)] * 2
                  + [pltpu.VMEM((W, 2 * w), F32)] * 2
                  + [pltpu.SemaphoreType.DMA] * 4
                  + [pltpu.VMEM((W,), jnp.int32)] * (2 * TOP_K))
    return pl.kernel(
        body,
        out_type=jax.ShapeDtypeStruct((n_win * W, 2 * w), F32),
        mesh=_sc_mesh(),
        scratch_types=slot_types,
        compiler_params=pltpu.CompilerParams(needs_layout_passes=False),
        name="sc_combine",
    )(ys, destw, gates)


def _experts_kernel(be_ref, nv_ref, xs_ref, wg_ref, wu_ref, wd_ref, ys_ref):
    del be_ref
    n_valid = nv_ref[pl.program_id(0)]
    mb, w = xs_ref.shape

    @pl.when(n_valid > 0)
    def _():
        keep = lax.broadcasted_iota(jnp.int32, (mb, w), 0) < n_valid
        lo, hi = _unpack_rows(jnp.where(keep, xs_ref[...], jnp.uint32(0)))
        lo, hi = lo.astype(BF16), hi.astype(BF16)

        def proj(w_ref):
            return (jnp.dot(lo, w_ref[0, :w, :], preferred_element_type=F32)
                    + jnp.dot(hi, w_ref[0, w:, :], preferred_element_type=F32))

        hb = jax.nn.silu(proj(wg_ref)) * proj(wu_ref)
        ys_ref[...] = _pack_rows(jnp.dot(hb.astype(BF16), wd_ref[0], preferred_element_type=F32))

    @pl.when(n_valid <= 0)
    def _():
        ys_ref[...] = jnp.zeros_like(ys_ref)


def _experts(xs, blk_exp, n_valid, p):
    n_rows, w = xs.shape
    mb = n_rows // blk_exp.shape[0]
    grid_spec = pltpu.PrefetchScalarGridSpec(
        num_scalar_prefetch=2,
        grid=(n_rows // mb,),
        in_specs=[
            pl.BlockSpec((mb, w), lambda i, be, nv: (i, 0)),
            pl.BlockSpec((1, D_MODEL, EXPERT_DIM), lambda i, be, nv: (be[i], 0, 0)),
            pl.BlockSpec((1, D_MODEL, EXPERT_DIM), lambda i, be, nv: (be[i], 0, 0)),
            pl.BlockSpec((1, EXPERT_DIM, D_MODEL), lambda i, be, nv: (be[i], 0, 0)),
        ],
        out_specs=pl.BlockSpec((mb, w), lambda i, be, nv: (i, 0)),
    )
    return pl.pallas_call(
        _experts_kernel,
        grid_spec=grid_spec,
        out_shape=jax.ShapeDtypeStruct((n_rows, w), jnp.uint32),
        compiler_params=pltpu.CompilerParams(
            dimension_semantics=("arbitrary",), vmem_limit_bytes=VMEM_LIMIT),
        name="experts",
    )(blk_exp, n_valid, xs, p["w_e_gate"], p["w_e_up"], p["w_e_down"])


def _final_kernel(x1_ref, routed_ref, wsg_ref, wsu_ref, wsd_ref, l2g_ref, l2b_ref, out_ref):
    x1 = x1_ref[...]
    xb = x1.astype(BF16)
    hs = (jax.nn.silu(jnp.dot(xb, wsg_ref[...], preferred_element_type=F32))
          * jnp.dot(xb, wsu_ref[...], preferred_element_type=F32))
    shared = jnp.dot(hs.astype(BF16), wsd_ref[...], preferred_element_type=F32)
    out_ref[...] = _layer_norm(ALPHA * x1 + (routed_ref[...] + shared), l2g_ref[...], l2b_ref[...])


def _final(x1, routed, p):
    T = x1.shape[0]
    tm = TOKEN_TILE
    tok = lambda w: pl.BlockSpec((tm, w), lambda i: (i, 0))
    return pl.pallas_call(
        _final_kernel,
        grid=(T // tm,),
        in_specs=[
            tok(D_MODEL), tok(D_MODEL),
            _const_spec((D_MODEL, SHARED_DIM)), _const_spec((D_MODEL, SHARED_DIM)),
            _const_spec((SHARED_DIM, D_MODEL)),
            _const_spec((1, D_MODEL)), _const_spec((1, D_MODEL)),
        ],
        out_specs=tok(D_MODEL),
        out_shape=jax.ShapeDtypeStruct((T, D_MODEL), F32),
        compiler_params=pltpu.CompilerParams(
            dimension_semantics=("parallel",), vmem_limit_bytes=VMEM_LIMIT),
        name="final",
    )(x1, routed, p["w_sh_gate"], p["w_sh_up"], p["w_sh_down"], p["ln2_g"], p["ln2_b"])


def _expert_block(n_tokens):
    mean_rows = n_tokens * TOP_K // N_EXPERTS
    return EXPERT_BLOCK_LARGE if mean_rows >= 2 * EXPERT_BLOCK_LARGE else EXPERT_BLOCK


def _routing_layout(route, counts, n_tokens):
    mb = _expert_block(n_tokens)
    n_blocks = -(-n_tokens * TOP_K // mb) + N_EXPERTS
    idx = route[0:TOP_K].astype(jnp.int32)
    rank = route[8:8 + TOP_K].astype(jnp.int32)
    counts = counts.astype(jnp.int32)
    padded = (counts + mb - 1) // mb * mb
    pad_end = jnp.cumsum(padded)
    pad_start = pad_end - padded
    experts = jnp.arange(N_EXPERTS, dtype=jnp.int32)
    dest = rank + jnp.sum(jnp.where(idx[:, :, None] == experts, pad_start, 0), axis=-1)
    windows = lambda W: dest.reshape(TOP_K, n_tokens // W, W).transpose(1, 0, 2)
    blk_start = jnp.arange(n_blocks, dtype=jnp.int32) * mb
    blk_exp = jnp.minimum(
        jnp.sum((pad_end[None, :] <= blk_start[:, None]).astype(jnp.int32), axis=1), N_EXPERTS - 1)
    valid_end = jnp.sum(jnp.where(blk_exp[:, None] == experts, pad_start + counts, 0), axis=-1)
    n_valid = jnp.clip(valid_end - blk_start, 0, mb).astype(jnp.int32)
    gates = jnp.broadcast_to(
        route[16:16 + TOP_K].T.reshape(n_tokens // SC_COMBINE_WINDOW, SC_COMBINE_WINDOW, TOP_K, 1),
        (n_tokens // SC_COMBINE_WINDOW, SC_COMBINE_WINDOW, TOP_K, SC_LANES)).reshape(
            n_tokens // SC_COMBINE_WINDOW, -1)
    return windows(SC_WINDOW), windows(SC_COMBINE_WINDOW), gates, blk_exp, n_valid, n_blocks * mb


def _encode(x, p):
    batch, seq, _ = x.shape
    T = batch * seq
    xt = x.reshape(T, D_MODEL)
    ag, q, gf, gb, iv, so, sgb, gmin = _inproj(xt, p)
    of, ob = _hgrn(q, gf, gb, iv, gmin, batch)
    x1, x1p, route, cnt = _mix(xt, of, ob, so, sgb, ag, p)
    dest_d, dest_c, gates, blk_exp, n_valid, n_rows = _routing_layout(route, cnt[:, 0], T)
    xs = _sc_dispatch(x1p, dest_d, n_rows)
    ys = _experts(xs, blk_exp, n_valid, p)
    out = _final(x1, _sc_combine(ys, dest_c, gates), p)
    return out.reshape(batch, seq, D_MODEL)


def _prepare_params(ln_in_g, ln_in_b, w_in, a_ln_g, a_ln_b, a_ws, a_sb, hg_lb_logits, hg_norm_g,
                    w_pa, w_pb, w_o, ln1_g, ln1_b, w_router, router_bias, w_e_gate, w_e_up,
                    w_e_down, w_sh_gate, w_sh_up, w_sh_down, ln2_g, ln2_b):
    l = 0
    row = lambda v: v.reshape(1, -1).astype(F32)
    ws = a_ws[l].astype(BF16)
    wsp = jnp.concatenate([ws[0::2], ws[1::2]], axis=2)
    sbf = jnp.repeat(a_sb[l].astype(F32), A_WIDTH // A_GROUPS, axis=1)
    lb = jnp.cumsum(jax.nn.softmax(hg_lb_logits.astype(F32), axis=1), axis=1)[:, l]
    return dict(
        ln_in_g=row(ln_in_g), ln_in_b=row(ln_in_b), w_in=w_in[l].astype(BF16),
        a_ln_g=row(a_ln_g[l]), a_ln_b=row(a_ln_b[l]), wsp=wsp, sbf=sbf, lb=lb,
        w_pa=w_pa[l].astype(BF16), hg_norm_g=row(hg_norm_g[l]),
        w_pb=w_pb[l].astype(BF16), w_o=w_o[l].astype(BF16),
        ln1_g=row(ln1_g[l]), ln1_b=row(ln1_b[l]),
        w_router=w_router[l].T.astype(BF16),
        router_bias=jnp.broadcast_to(router_bias[l].astype(F32)[:, None], (N_EXPERTS, LANES)),
        w_e_gate=w_e_gate[l].astype(BF16), w_e_up=w_e_up[l].astype(BF16),
        w_e_down=w_e_down[l].astype(BF16),
        w_sh_gate=w_sh_gate[l].astype(BF16), w_sh_up=w_sh_up[l].astype(BF16),
        w_sh_down=w_sh_down[l].astype(BF16),
        ln2_g=row(ln2_g[l]), ln2_b=row(ln2_b[l]),
    )


def kernel(x_prompt, x_sample, ln_in_g, ln_in_b, w_in, a_ln_g, a_ln_b, a_ws, a_sb, hg_lb_logits,
           hg_norm_g, w_pa, w_pb, w_o, ln1_g, ln1_b, w_router, router_bias, w_e_gate, w_e_up,
           w_e_down, w_sh_gate, w_sh_up, w_sh_down, ln2_g, ln2_b):
    p = _prepare_params(ln_in_g, ln_in_b, w_in, a_ln_g, a_ln_b, a_ws, a_sb, hg_lb_logits, hg_norm_g,
                        w_pa, w_pb, w_o, ln1_g, ln1_b, w_router, router_bias, w_e_gate, w_e_up,
                        w_e_down, w_sh_gate, w_sh_up, w_sh_down, ln2_g, ln2_b)
    return _encode(x_prompt, p), _encode(x_sample, p)
```

```python
import functools

import jax
import jax.numpy as jnp
from jax import lax
from jax.experimental import pallas as pl
from jax.experimental.pallas import tpu as pltpu
from jax.experimental.pallas import tpu_sc as plsc

F32 = jnp.float32
BF16 = jnp.bfloat16

D_MODEL = 1024
A_GROUPS = 8
A_WIDTH = 512
A_CHUNK = 128
HG_HEADS = 8
HG_DK = 128
HG_WIDTH = HG_HEADS * HG_DK
N_IN = 2 * A_WIDTH + 5 * HG_WIDTH + 2 * D_MODEL
N_EXPERTS = 64
TOP_K = 6
N_GROUPS = 8
TOPK_GROUPS = 4
GROUP_SIZE = N_EXPERTS // N_GROUPS
EXPERT_DIM = 256
SHARED_DIM = 256
ROUTED_SCALE = 2.5
DEPTH = 1
ALPHA = (2.0 * DEPTH) ** 0.25
LN_EPS = 1e-5
RMS_EPS = 1e-6

LANES = 128
TOKEN_TILE = 256
INPROJ_TILE = 256
HG_CHUNK = 128
HG_SAFE_LOGDECAY = 80.0
EXPERT_BLOCK = 512
EXPERT_BLOCK_LARGE = 1024
SC_CORES = 2
SC_SUBCORES = 16
SC_LANES = 16
SC_WINDOW = 32
SC_COMBINE_WINDOW = 8
VMEM_LIMIT = 56 * 1024 * 1024

_O_U, _O_V, _O_Q, _O_FF, _O_FB, _O_I, _O_G, _O_GA, _O_GB = (
    0, 512, 1024, 2048, 3072, 4096, 5120, 6144, 7168)


def _layer_norm(x, g, b):
    mu = jnp.mean(x, axis=-1, keepdims=True)
    xc = x - mu
    var = jnp.mean(xc * xc, axis=-1, keepdims=True)
    return xc * lax.rsqrt(var + LN_EPS) * g + b


def _bdot(a, b):
    return jnp.dot(a.astype(BF16), b.astype(BF16), preferred_element_type=F32)


def _pack_rows(x):
    w = x.shape[1] // 2
    lo = lax.bitcast_convert_type(x[:, :w].astype(BF16).astype(F32), jnp.uint32)
    hi = lax.bitcast_convert_type(x[:, w:].astype(BF16).astype(F32), jnp.uint32)
    return hi | (lo >> 16)


def _unpack_rows(p):
    lo = lax.bitcast_convert_type(p << 16, F32)
    hi = lax.bitcast_convert_type(p & jnp.uint32(0xFFFF0000), F32)
    return lo, hi


def _const_spec(shape):
    nd = len(shape)
    return pl.BlockSpec(shape, lambda *_: (0,) * nd, pipeline_mode=pl.Buffered(1))


def _inproj_kernel(x_ref, lng_ref, lnb_ref, win_ref, alng_ref, alnb_ref, wsp_ref, sbf_ref, lb_ref,
                   wpa_ref, ag_ref, q_ref, gf_ref, gb_ref, iv_ref, so_ref, sgb_ref, gmin_ref):
    tm = x_ref.shape[0]
    xb = _layer_norm(x_ref[...], lng_ref[...], lnb_ref[...]).astype(BF16)

    def sec(lo, width):
        return jnp.dot(xb, win_ref[:, lo:lo + width], preferred_element_type=F32)

    u = jax.nn.gelu(sec(_O_U, A_WIDTH))
    v = _layer_norm(jax.nn.gelu(sec(_O_V, A_WIDTH)), alng_ref[...], alnb_ref[...]).astype(BF16)
    lane = lax.broadcasted_iota(jnp.int32, (A_CHUNK, LANES), 1)
    left = lane < (A_WIDTH // A_GROUPS)
    zero = jnp.zeros((A_CHUNK, LANES), BF16)
    chunks = []
    for c in range(tm // A_CHUNK):
        vc = v[c * A_CHUNK:(c + 1) * A_CHUNK]
        cols = []
        for p in range(A_GROUPS // 2):
            vp = vc[:, p * LANES:(p + 1) * LANES]
            rhs = jnp.concatenate([jnp.where(left, vp, zero), jnp.where(left, zero, vp)], axis=0)
            cols.append(jnp.dot(wsp_ref[p], rhs, preferred_element_type=F32))
        chunks.append(jnp.concatenate(cols, axis=1) + sbf_ref[...])
    mixed = jnp.concatenate(chunks, axis=0)
    a = _bdot(u * mixed, wpa_ref[...])
    ag_ref[...] = (jax.nn.sigmoid(sec(_O_GA, D_MODEL)) * a).astype(BF16)

    q_ref[...] = jax.nn.silu(sec(_O_Q, HG_WIDTH)).astype(BF16)
    mins = []
    for d, (off, g_ref) in enumerate(((_O_FF, gf_ref), (_O_FB, gb_ref))):
        lb = lb_ref[d:d + 1, :]
        f = lb + (1.0 - lb) * jax.nn.sigmoid(sec(off, HG_WIDTH))
        g = jnp.log(f)
        g_ref[...] = g
        half = jnp.sum(g.reshape(tm // (HG_CHUNK // 2), HG_CHUNK // 2, HG_WIDTH), axis=1)
        mins.append(jnp.min(half, axis=-1, keepdims=True))
    gmin_ref[0] = jnp.broadcast_to(jnp.concatenate(mins, axis=0), gmin_ref.shape[1:])
    iv_ref[...] = sec(_O_I, HG_WIDTH).astype(BF16)
    so_ref[...] = jax.nn.silu(sec(_O_G, HG_WIDTH)).astype(BF16)
    sgb_ref[...] = jax.nn.sigmoid(sec(_O_GB, D_MODEL)).astype(BF16)


def _inproj(x, p):
    T = x.shape[0]
    tm = INPROJ_TILE
    nt = T // tm
    nh = 2 * tm // (HG_CHUNK // 2)
    tok = lambda w: pl.BlockSpec((tm, w), lambda i: (i, 0))
    outs = (
        jax.ShapeDtypeStruct((T, D_MODEL), BF16),
        jax.ShapeDtypeStruct((T, HG_WIDTH), BF16),
        jax.ShapeDtypeStruct((T, HG_WIDTH), F32),
        jax.ShapeDtypeStruct((T, HG_WIDTH), F32),
        jax.ShapeDtypeStruct((T, HG_WIDTH), BF16),
        jax.ShapeDtypeStruct((T, HG_WIDTH), BF16),
        jax.ShapeDtypeStruct((T, D_MODEL), BF16),
        jax.ShapeDtypeStruct((nt, nh, LANES), F32),
    )
    return pl.pallas_call(
        _inproj_kernel,
        grid=(nt,),
        in_specs=[
            tok(D_MODEL),
            _const_spec((1, D_MODEL)), _const_spec((1, D_MODEL)),
            _const_spec((D_MODEL, N_IN)),
            _const_spec((1, A_WIDTH)), _const_spec((1, A_WIDTH)),
            _const_spec((A_GROUPS // 2, A_CHUNK, 2 * A_CHUNK)),
            _const_spec((A_CHUNK, A_WIDTH)),
            _const_spec((2, HG_WIDTH)),
            _const_spec((A_WIDTH, D_MODEL)),
        ],
        out_specs=(tok(D_MODEL), tok(HG_WIDTH), tok(HG_WIDTH), tok(HG_WIDTH), tok(HG_WIDTH),
                   tok(HG_WIDTH), tok(D_MODEL), pl.BlockSpec((1, nh, LANES), lambda i: (i, 0, 0))),
        out_shape=outs,
        compiler_params=pltpu.CompilerParams(
            dimension_semantics=("parallel",), vmem_limit_bytes=VMEM_LIMIT),
        name="inproj",
    )(x, p["ln_in_g"], p["ln_in_b"], p["w_in"], p["a_ln_g"], p["a_ln_b"], p["wsp"], p["sbf"],
      p["lb"], p["w_pa"])


_NT = (((1,), (1,)), ((), ()))
_TN = (((0,), (0,)), ((), ()))


def _hgrn_direction(q_ref, g_ref, v_ref, st_ref, o_ref, tri, mask, fwd):
    C = q_ref.shape[0]
    g = g_ref[...]
    ghi = g.astype(BF16)
    glo = (g - ghi.astype(F32)).astype(BF16)
    b = jnp.dot(jnp.concatenate([tri, tri], axis=1), jnp.concatenate([ghi, glo], axis=0),
                preferred_element_type=F32)
    mid = C // 2 - 1 if fwd else C // 2
    end = C - 1 if fwd else 0
    r = b[mid:mid + 1, :]
    b_end = b[end:end + 1, :]
    qt = q_ref[...].astype(F32) * jnp.exp(b - r)
    kt = (1.0 - jnp.exp(g)) * jnp.exp(r - b)
    qtb = qt.astype(BF16)
    ktb = kt.astype(BF16)
    qhb = (qt * jnp.exp(r)).astype(BF16)
    khb = (kt * jnp.exp(b_end - r)).astype(BF16)
    decay = jnp.exp(b_end)
    v32 = v_ref[...].astype(F32)
    heads = [slice(h * HG_DK, (h + 1) * HG_DK) for h in range(HG_HEADS)]
    zero = jnp.zeros((C, HG_DK), BF16)
    mask2 = jnp.concatenate([mask, mask], axis=1)
    scores = []
    for p in range(HG_HEADS // 2):
        k1, k2 = ktb[:, heads[2 * p]], ktb[:, heads[2 * p + 1]]
        kk = jnp.concatenate([jnp.concatenate([k1, zero], axis=1),
                              jnp.concatenate([zero, k2], axis=1)], axis=0)
        s2 = lax.dot_general(qtb[:, 2 * p * HG_DK:(2 * p + 2) * HG_DK], kk, _NT,
                             preferred_element_type=F32)
        s2 = jnp.where(mask2, s2, 0.0).astype(BF16)
        scores += [s2[:, :C], s2[:, C:]]
    for h, sl in enumerate(heads):
        st = st_ref[h]
        vt = v32[:, sl].T.astype(BF16)
        o_ref[:, sl] = lax.dot_general(
            jnp.concatenate([scores[h], qhb[:, sl]], axis=1),
            jnp.concatenate([vt, st.astype(BF16)], axis=1), _NT,
            preferred_element_type=F32).astype(o_ref.dtype)
        st_ref[h] = st * decay[:, sl] + jnp.dot(vt, khb[:, sl], preferred_element_type=F32)


def _hgrn_direction_stepwise(q_ref, g_ref, v_ref, st_ref, o_ref, q32_ref, v32_ref, o32_ref, fwd):
    C = q_ref.shape[0]
    q32_ref[...] = q_ref[...].astype(F32)
    v32_ref[...] = v_ref[...].astype(F32)
    sub = 8
    rows = lax.broadcasted_iota(jnp.int32, (sub, HG_DK), 0)

    def group(i, carry):
        base = pl.multiple_of((i if fwd else C // sub - 1 - i) * sub, sub)
        f = jnp.exp(g_ref[pl.ds(base, sub), :])
        k = 1.0 - f
        q = q32_ref[pl.ds(base, sub), :]
        v = v32_ref[pl.ds(base, sub), :]
        for h in range(HG_HEADS):
            sl = slice(h * HG_DK, (h + 1) * HG_DK)
            st = st_ref[h]
            out = jnp.zeros((sub, HG_DK), F32)
            for r in (range(sub) if fwd else range(sub - 1, -1, -1)):
                v_t = jnp.where(rows == 0, v[r:r + 1, sl], 0.0).astype(BF16)
                k_t = jnp.broadcast_to(k[r:r + 1, sl], (sub, HG_DK)).astype(BF16)
                st = st * f[r:r + 1, sl] + lax.dot_general(v_t, k_t, _TN, preferred_element_type=F32)
                q_t = jnp.broadcast_to(q[r:r + 1, sl], (sub, HG_DK)).astype(BF16)
                o_t = lax.dot_general(q_t, st.astype(BF16), _NT, preferred_element_type=F32)
                out = jnp.where(rows == r, o_t, out)
            st_ref[h] = st
            o32_ref[pl.ds(base, sub), sl] = out
        return carry

    lax.fori_loop(0, C // sub, group, 0)
    o_ref[...] = o32_ref[...].astype(o_ref.dtype)


def _hgrn_kernel(safe_ref, qf_ref, qb_ref, gf_ref, gb_ref, vf_ref, vb_ref, tril_ref, triu_ref,
                 of_ref, ob_ref, sf_ref, sb_ref, q32_ref, v32_ref, o32_ref):
    b, j = pl.program_id(0), pl.program_id(1)
    nc = pl.num_programs(1)

    @pl.when(j == 0)
    def _():
        sf_ref[...] = jnp.zeros_like(sf_ref)
        sb_ref[...] = jnp.zeros_like(sb_ref)

    C = qf_ref.shape[0]
    row = lax.broadcasted_iota(jnp.int32, (C, C), 0)
    col = lax.broadcasted_iota(jnp.int32, (C, C), 1)
    safe_f = safe_ref[0, b * nc + j] != 0
    safe_b = safe_ref[1, b * nc + nc - 1 - j] != 0

    def forward(stepwise):
        if stepwise:
            _hgrn_direction_stepwise(qf_ref, gf_ref, vf_ref, sf_ref, of_ref, q32_ref, v32_ref, o32_ref, True)
        else:
            _hgrn_direction(qf_ref, gf_ref, vf_ref, sf_ref, of_ref, tril_ref[...], row >= col, True)

    def backward(stepwise):
        if stepwise:
            _hgrn_direction_stepwise(qb_ref, gb_ref, vb_ref, sb_ref, ob_ref, q32_ref, v32_ref, o32_ref, False)
        else:
            _hgrn_direction(qb_ref, gb_ref, vb_ref, sb_ref, ob_ref, triu_ref[...], row <= col, False)

    @pl.when(safe_f & safe_b)
    def _():
        forward(False)
        backward(False)

    @pl.when(jnp.logical_not(safe_f & safe_b))
    def _():
        @pl.when(safe_f)
        def _():
            forward(False)

        @pl.when(jnp.logical_not(safe_f))
        def _():
            forward(True)

        @pl.when(safe_b)
        def _():
            backward(False)

        @pl.when(jnp.logical_not(safe_b))
        def _():
            backward(True)


def _hgrn(q, gf, gb, iv, gmin, batch):
    T = q.shape[0]
    C = HG_CHUNK
    nc = T // batch // C
    nt, nh, _ = gmin.shape
    halves = gmin[:, :, 0].reshape(nt, 2, nh // 4, 2)
    safe = (jnp.min(halves, axis=-1) > -HG_SAFE_LOGDECAY).astype(jnp.int32)
    safe = safe.transpose(1, 0, 2).reshape(2, T // C)
    fwd = pl.BlockSpec((C, HG_WIDTH), lambda b, j, s: (b * nc + j, 0))
    bwd = pl.BlockSpec((C, HG_WIDTH), lambda b, j, s: (b * nc + nc - 1 - j, 0))
    const = lambda shape: pl.BlockSpec(shape, lambda b, j, s: (0,) * len(shape),
                                       pipeline_mode=pl.Buffered(1))
    row = lax.broadcasted_iota(jnp.int32, (C, C), 0)
    col = lax.broadcasted_iota(jnp.int32, (C, C), 1)
    tril = (row >= col).astype(BF16)
    triu = (row <= col).astype(BF16)
    grid_spec = pltpu.PrefetchScalarGridSpec(
        num_scalar_prefetch=1,
        grid=(batch, nc),
        in_specs=[fwd, bwd, fwd, bwd, fwd, bwd, const((C, C)), const((C, C))],
        out_specs=(fwd, bwd),
        scratch_shapes=[pltpu.VMEM((HG_HEADS, HG_DK, HG_DK), F32),
                        pltpu.VMEM((HG_HEADS, HG_DK, HG_DK), F32),
                        pltpu.VMEM((C, HG_WIDTH), F32), pltpu.VMEM((C, HG_WIDTH), F32),
                        pltpu.VMEM((C, HG_WIDTH), F32)],
    )
    return pl.pallas_call(
        _hgrn_kernel,
        grid_spec=grid_spec,
        out_shape=(jax.ShapeDtypeStruct((T, HG_WIDTH), BF16), jax.ShapeDtypeStruct((T, HG_WIDTH), BF16)),
        compiler_params=pltpu.CompilerParams(
            dimension_semantics=("parallel", "arbitrary"), vmem_limit_bytes=VMEM_LIMIT),
        name="hgrn",
    )(safe, q, q, gf, gb, iv, iv, tril, triu)


def _mix_kernel(x_ref, lng_ref, lnb_ref, of_ref, ob_ref, so_ref, sgb_ref, ag_ref, ng_ref, wpb_ref,
                wo_ref, l1g_ref, l1b_ref, wr_ref, rb_ref, tri_ref, x1_ref, x1p_ref, route_ref,
                cnt_ref, carry_ref):
    tm = x_ref.shape[0]

    @pl.when(pl.program_id(0) == 0)
    def _():
        carry_ref[...] = jnp.zeros_like(carry_ref)

    o = of_ref[...].astype(F32) + ob_ref[...].astype(F32)
    heads = []
    for h in range(HG_HEADS):
        oh = o[:, h * HG_DK:(h + 1) * HG_DK]
        heads.append(oh * lax.rsqrt(jnp.mean(oh * oh, axis=-1, keepdims=True) + RMS_EPS))
    rn = jnp.concatenate(heads, axis=1) * ng_ref[...] * so_ref[...].astype(F32)
    r = _bdot(rn, wpb_ref[...])
    mixed = ag_ref[...].astype(F32) + sgb_ref[...].astype(F32) * r
    y = _bdot(mixed, wo_ref[...])
    xn = _layer_norm(x_ref[...], lng_ref[...], lnb_ref[...])
    x1 = _layer_norm(ALPHA * xn + y, l1g_ref[...], l1b_ref[...])
    x1_ref[...] = x1
    x1p_ref[...] = _pack_rows(x1)

    neg = jnp.float32(-jnp.inf)
    reps = tm // LANES
    scores = jax.nn.sigmoid(lax.dot_general(wr_ref[...], x1.astype(BF16), _NT,
                                            preferred_element_type=F32))
    biased = (scores + jnp.concatenate([rb_ref[...]] * reps, axis=1)).reshape(
        N_GROUPS, GROUP_SIZE, tm)
    sub = lax.broadcasted_iota(jnp.int32, biased.shape, 1).astype(F32)
    m1 = jnp.max(biased, axis=1, keepdims=True)
    first = jnp.min(jnp.where(biased == m1, sub, float(GROUP_SIZE)), axis=1, keepdims=True)
    m2 = jnp.max(jnp.where(sub == first, neg, biased), axis=1, keepdims=True)
    gs = (m1 + m2).reshape(N_GROUPS, tm)
    grp = lax.broadcasted_iota(jnp.int32, (N_GROUPS, tm), 0)
    ahead = jnp.zeros((N_GROUPS, tm), F32)
    for d in range(1, N_GROUPS):
        other = pltpu.roll(gs, d, 0)
        tie = jnp.where(grp >= d, 1.0, 0.0)
        ahead = ahead + jnp.where(other > gs, 1.0, jnp.where(other == gs, tie, 0.0))
    keep = (ahead < TOPK_GROUPS).reshape(N_GROUPS, 1, tm)
    allowed = jnp.where(keep, biased, neg).reshape(N_EXPERTS, tm)
    row = lax.broadcasted_iota(jnp.int32, (N_EXPERTS, tm), 0).astype(F32)
    sel = jnp.zeros((N_EXPERTS, tm), F32)
    picks = []
    for _ in range(TOP_K):
        m = jnp.max(allowed, axis=0, keepdims=True)
        first = jnp.min(jnp.where(allowed == m, row, float(N_EXPERTS)), axis=0, keepdims=True)
        hit = row == first
        picks.append((first, hit, jnp.sum(jnp.where(hit, scores, 0.0), axis=0, keepdims=True)))
        sel = jnp.where(hit, 1.0, sel)
        allowed = jnp.where(hit, neg, allowed)
    wsum = picks[0][2]
    for pk in picks[1:]:
        wsum = wsum + pk[2]
    selb = sel.astype(BF16)
    carry = carry_ref[...]
    before = (jnp.dot(selb, tri_ref[...], preferred_element_type=F32)
              + jnp.concatenate([carry] * reps, axis=1))
    total = carry + jnp.dot(selb, jnp.ones((tm, LANES), BF16), preferred_element_type=F32)
    carry_ref[...] = total
    cnt_ref[...] = total
    blank = [jnp.zeros((1, tm), F32)] * (8 - TOP_K)
    route_ref[...] = jnp.concatenate(
        [pk[0] for pk in picks] + blank
        + [jnp.sum(jnp.where(pk[1], before, 0.0), axis=0, keepdims=True) for pk in picks] + blank
        + [pk[2] / wsum * ROUTED_SCALE for pk in picks] + blank, axis=0)


def _mix(x, of, ob, so, sgb, ag, p):
    T = x.shape[0]
    tm = TOKEN_TILE
    tok = lambda w: pl.BlockSpec((tm, w), lambda i: (i, 0))
    row = lax.broadcasted_iota(jnp.int32, (tm, tm), 0)
    col = lax.broadcasted_iota(jnp.int32, (tm, tm), 1)
    tri = (row < col).astype(BF16)
    return pl.pallas_call(
        _mix_kernel,
        grid=(T // tm,),
        in_specs=[
            tok(D_MODEL), _const_spec((1, D_MODEL)), _const_spec((1, D_MODEL)),
            tok(HG_WIDTH), tok(HG_WIDTH), tok(HG_WIDTH), tok(D_MODEL), tok(D_MODEL),
            _const_spec((1, HG_WIDTH)),
            _const_spec((HG_WIDTH, D_MODEL)), _const_spec((D_MODEL, D_MODEL)),
            _const_spec((1, D_MODEL)), _const_spec((1, D_MODEL)),
            _const_spec((N_EXPERTS, D_MODEL)), _const_spec((N_EXPERTS, LANES)),
            _const_spec((tm, tm)),
        ],
        out_specs=(tok(D_MODEL), tok(D_MODEL // 2), pl.BlockSpec((24, tm), lambda i: (0, i)),
                   pl.BlockSpec((N_EXPERTS, LANES), lambda i: (0, 0))),
        out_shape=(jax.ShapeDtypeStruct((T, D_MODEL), F32),
                   jax.ShapeDtypeStruct((T, D_MODEL // 2), jnp.uint32),
                   jax.ShapeDtypeStruct((24, T), F32),
                   jax.ShapeDtypeStruct((N_EXPERTS, LANES), F32)),
        scratch_shapes=[pltpu.VMEM((N_EXPERTS, LANES), F32)],
        compiler_params=pltpu.CompilerParams(
            dimension_semantics=("arbitrary",), vmem_limit_bytes=VMEM_LIMIT),
        name="mix",
    )(x, p["ln_in_g"], p["ln_in_b"], of, ob, so, sgb, ag, p["hg_norm_g"], p["w_pb"], p["w_o"],
      p["ln1_g"], p["ln1_b"], p["w_router"], p["router_bias"], tri)


def _sc_mesh():
    return plsc.VectorSubcoreMesh(core_axis_name="c", subcore_axis_name="s",
                                  num_cores=SC_CORES, num_subcores=SC_SUBCORES)


def _sc_worker():
    return lax.axis_index("s") * SC_CORES + lax.axis_index("c")


def _sc_dispatch(x1p, destw, n_rows):
    T, w = x1p.shape
    n_win, _, W = destw.shape
    per_worker = n_win // (SC_CORES * SC_SUBCORES)

    def body(x_hbm, d_hbm, o_hbm, rows_v, idx_v, sem):
        first = _sc_worker() * per_worker

        @pl.loop(0, per_worker)
        def _(j):
            win = first + j
            pltpu.sync_copy(x_hbm.at[pl.ds(win * W, W)], rows_v)
            pltpu.sync_copy(d_hbm.at[win], idx_v)
            copies = [pltpu.async_copy(rows_v, o_hbm.at[idx_v.at[k]], sem) for k in range(TOP_K)]
            for c in copies:
                c.wait()

    return pl.kernel(
        body,
        out_type=jax.ShapeDtypeStruct((n_rows, w), jnp.uint32),
        mesh=_sc_mesh(),
        scratch_types=[pltpu.VMEM((W, w), jnp.uint32), pltpu.VMEM((TOP_K, W), jnp.int32),
                       pltpu.SemaphoreType.DMA],
        name="sc_dispatch",
    )(x1p, destw)


def _sc_combine(ys, destw, gates):
    n_win, _, W = destw.shape
    w = ys.shape[1]
    per_worker = n_win // (SC_CORES * SC_SUBCORES)
    assert per_worker % 2 == 0 and per_worker * SC_CORES * SC_SUBCORES == n_win

    def body(y_hbm, d_hbm, g_hbm, o_hbm, idx_v, *scratch):
        slots = [dict(rows=scratch[s], gate=scratch[2 + s], out=scratch[4 + s], gsem=scratch[6 + s],
                      wsem=scratch[8 + s]) for s in range(2)]
        first = _sc_worker() * per_worker
        pltpu.sync_copy(d_hbm.at[pl.ds(first * TOP_K * W, per_worker * TOP_K * W)], idx_v)

        def loads(b, win):
            i = win - first
            return [pltpu.make_async_copy(g_hbm.at[win], b["gate"], b["gsem"])] + [
                pltpu.make_async_copy(y_hbm.at[idx_v.at[pl.ds((i * TOP_K + k) * W, W)]],
                                      b["rows"].at[k], b["gsem"]) for k in range(TOP_K)]

        def write_back(b, win):
            return pltpu.make_async_copy(b["out"], o_hbm.at[pl.ds(win * W, W)], b["wsem"])

        def fetch(b, win):
            for c in loads(b, win):
                c.start()

        def reduce_rows(b):
            rows, gate_v, out = b["rows"], b["gate"], b["out"]

            @pl.loop(0, W)
            def _(j):
                gate = [gate_v[pl.ds((j * TOP_K + k) * SC_LANES, SC_LANES)] for k in range(TOP_K)]

                @plsc.parallel_loop(0, w, step=SC_LANES, unroll=4)
                def _(col):
                    lo, hi = [], []
                    for k in range(TOP_K):
                        p = rows[k, j, pl.ds(col, SC_LANES)]
                        lo.append(plsc.bitcast(p << 16, F32) * gate[k])
                        hi.append(plsc.bitcast(p & jnp.uint32(0xFFFF0000), F32) * gate[k])
                    out[j, pl.ds(col, SC_LANES)] = (lo[0] + lo[1]) + (lo[2] + lo[3]) + (lo[4] + lo[5])
                    out[j, pl.ds(w + col, SC_LANES)] = (hi[0] + hi[1]) + (hi[2] + hi[3]) + (hi[4] + hi[5])

        def process(b, win, not_first):
            for c in loads(b, win):
                c.wait()

            @pl.when(not_first)
            def _():
                write_back(b, win).wait()

            reduce_rows(b)
            write_back(b, win).start()

        fetch(slots[0], first)

        @pl.loop(0, per_worker, step=2)
        def _(i):
            win = first + i
            fetch(slots[1], win + 1)
            process(slots[0], win, i > 0)

            @pl.when(i + 2 < per_worker)
            def _():
                fetch(slots[0], win + 2)

            process(slots[1], win + 1, i > 0)

        for b in slots:
            write_back(b, first).wait()

    slot_types = ([pltpu.VMEM((per_worker * TOP_K * W,), jnp.int32)]
                  + [pltpu.VMEM((TOP_K, W, w), jnp.uint32)] * 2
                  + [pltpu.VMEM((W * TOP_K * SC_LANES,), F32)] * 2
                  + [pltpu.VMEM((W, 2 * w), F32)] * 2
                  + [pltpu.SemaphoreType.DMA] * 4)
    return pl.kernel(
        body,
        out_type=jax.ShapeDtypeStruct((n_win * W, 2 * w), F32),
        mesh=_sc_mesh(),
        scratch_types=slot_types,
        compiler_params=pltpu.CompilerParams(needs_layout_passes=False),
        name="sc_combine",
    )(ys, destw.reshape(-1), gates)


def _experts_kernel(be_ref, nv_ref, xs_ref, wg_ref, wu_ref, wd_ref, ys_ref):
    del be_ref
    n_valid = nv_ref[pl.program_id(0)]
    mb, w = xs_ref.shape

    @pl.when(n_valid > 0)
    def _():
        keep = lax.broadcasted_iota(jnp.int32, (mb, w), 0) < n_valid
        lo, hi = _unpack_rows(jnp.where(keep, xs_ref[...], jnp.uint32(0)))
        lo, hi = lo.astype(BF16), hi.astype(BF16)

        def proj(w_ref):
            return (jnp.dot(lo, w_ref[0, :w, :], preferred_element_type=F32)
                    + jnp.dot(hi, w_ref[0, w:, :], preferred_element_type=F32))

        hb = jax.nn.silu(proj(wg_ref)) * proj(wu_ref)
        ys_ref[...] = _pack_rows(jnp.dot(hb.astype(BF16), wd_ref[0], preferred_element_type=F32))

    @pl.when(n_valid <= 0)
    def _():
        ys_ref[...] = jnp.zeros_like(ys_ref)


def _experts(xs, blk_exp, n_valid, p):
    n_rows, w = xs.shape
    mb = n_rows // blk_exp.shape[0]
    grid_spec = pltpu.PrefetchScalarGridSpec(
        num_scalar_prefetch=2,
        grid=(n_rows // mb,),
        in_specs=[
            pl.BlockSpec((mb, w), lambda i, be, nv: (i, 0)),
            pl.BlockSpec((1, D_MODEL, EXPERT_DIM), lambda i, be, nv: (be[i], 0, 0)),
            pl.BlockSpec((1, D_MODEL, EXPERT_DIM), lambda i, be, nv: (be[i], 0, 0)),
            pl.BlockSpec((1, EXPERT_DIM, D_MODEL), lambda i, be, nv: (be[i], 0, 0)),
        ],
        out_specs=pl.BlockSpec((mb, w), lambda i, be, nv: (i, 0)),
    )
    return pl.pallas_call(
        _experts_kernel,
        grid_spec=grid_spec,
        out_shape=jax.ShapeDtypeStruct((n_rows, w), jnp.uint32),
        compiler_params=pltpu.CompilerParams(
            dimension_semantics=("arbitrary",), vmem_limit_bytes=VMEM_LIMIT),
        name="experts",
    )(blk_exp, n_valid, xs, p["w_e_gate"], p["w_e_up"], p["w_e_down"])


def _final_kernel(x1_ref, routed_ref, wsg_ref, wsu_ref, wsd_ref, l2g_ref, l2b_ref, out_ref):
    x1 = x1_ref[...]
    xb = x1.astype(BF16)
    hs = (jax.nn.silu(jnp.dot(xb, wsg_ref[...], preferred_element_type=F32))
          * jnp.dot(xb, wsu_ref[...], preferred_element_type=F32))
    shared = jnp.dot(hs.astype(BF16), wsd_ref[...], preferred_element_type=F32)
    out_ref[...] = _layer_norm(ALPHA * x1 + (routed_ref[...] + shared), l2g_ref[...], l2b_ref[...])


def _final(x1, routed, p):
    T = x1.shape[0]
    tm = TOKEN_TILE
    tok = lambda w: pl.BlockSpec((tm, w), lambda i: (i, 0))
    return pl.pallas_call(
        _final_kernel,
        grid=(T // tm,),
        in_specs=[
            tok(D_MODEL), tok(D_MODEL),
            _const_spec((D_MODEL, SHARED_DIM)), _const_spec((D_MODEL, SHARED_DIM)),
            _const_spec((SHARED_DIM, D_MODEL)),
            _const_spec((1, D_MODEL)), _const_spec((1, D_MODEL)),
        ],
        out_specs=tok(D_MODEL),
        out_shape=jax.ShapeDtypeStruct((T, D_MODEL), F32),
        compiler_params=pltpu.CompilerParams(
            dimension_semantics=("parallel",), vmem_limit_bytes=VMEM_LIMIT),
        name="final",
    )(x1, routed, p["w_sh_gate"], p["w_sh_up"], p["w_sh_down"], p["ln2_g"], p["ln2_b"])


def _expert_block(n_tokens):
    mean_rows = n_tokens * TOP_K // N_EXPERTS
    return EXPERT_BLOCK_LARGE if mean_rows >= 2 * EXPERT_BLOCK_LARGE else EXPERT_BLOCK


def _routing_layout(route, counts, n_tokens):
    mb = _expert_block(n_tokens)
    n_blocks = -(-n_tokens * TOP_K // mb) + N_EXPERTS
    idx = route[0:TOP_K].astype(jnp.int32)
    rank = route[8:8 + TOP_K].astype(jnp.int32)
    counts = counts.astype(jnp.int32)
    padded = (counts + mb - 1) // mb * mb
    pad_end = jnp.cumsum(padded)
    pad_start = pad_end - padded
    experts = jnp.arange(N_EXPERTS, dtype=jnp.int32)
    dest = rank + jnp.sum(jnp.where(idx[:, :, None] == experts, pad_start, 0), axis=-1)
    windows = lambda W: dest.reshape(TOP_K, n_tokens // W, W).transpose(1, 0, 2)
    blk_start = jnp.arange(n_blocks, dtype=jnp.int32) * mb
    blk_exp = jnp.minimum(
        jnp.sum((pad_end[None, :] <= blk_start[:, None]).astype(jnp.int32), axis=1), N_EXPERTS - 1)
    valid_end = jnp.sum(jnp.where(blk_exp[:, None] == experts, pad_start + counts, 0), axis=-1)
    n_valid = jnp.clip(valid_end - blk_start, 0, mb).astype(jnp.int32)
    gates = jnp.broadcast_to(
        route[16:16 + TOP_K].T.reshape(n_tokens // SC_COMBINE_WINDOW, SC_COMBINE_WINDOW, TOP_K, 1),
        (n_tokens // SC_COMBINE_WINDOW, SC_COMBINE_WINDOW, TOP_K, SC_LANES)).reshape(
            n_tokens // SC_COMBINE_WINDOW, -1)
    return windows(SC_WINDOW), windows(SC_COMBINE_WINDOW), gates, blk_exp, n_valid, n_blocks * mb


def _encode(x, p):
    batch, seq, _ = x.shape
    T = batch * seq
    xt = x.reshape(T, D_MODEL)
    ag, q, gf, gb, iv, so, sgb, gmin = _inproj(xt, p)
    of, ob = _hgrn(q, gf, gb, iv, gmin, batch)
    x1, x1p, route, cnt = _mix(xt, of, ob, so, sgb, ag, p)
    dest_d, dest_c, gates, blk_exp, n_valid, n_rows = _routing_layout(route, cnt[:, 0], T)
    xs = _sc_dispatch(x1p, dest_d, n_rows)
    ys = _experts(xs, blk_exp, n_valid, p)
    out = _final(x1, _sc_combine(ys, dest_c, gates), p)
    return out.reshape(batch, seq, D_MODEL)


def _prepare_params(ln_in_g, ln_in_b, w_in, a_ln_g, a_ln_b, a_ws, a_sb, hg_lb_logits, hg_norm_g,
                    w_pa, w_pb, w_o, ln1_g, ln1_b, w_router, router_bias, w_e_gate, w_e_up,
                    w_e_down, w_sh_gate, w_sh_up, w_sh_down, ln2_g, ln2_b):
    l = 0
    row = lambda v: v.reshape(1, -1).astype(F32)
    ws = a_ws[l].astype(BF16)
    wsp = jnp.concatenate([ws[0::2], ws[1::2]], axis=2)
    sbf = jnp.repeat(a_sb[l].astype(F32), A_WIDTH // A_GROUPS, axis=1)
    lb = jnp.cumsum(jax.nn.softmax(hg_lb_logits.astype(F32), axis=1), axis=1)[:, l]
    return dict(
        ln_in_g=row(ln_in_g), ln_in_b=row(ln_in_b), w_in=w_in[l].astype(BF16),
        a_ln_g=row(a_ln_g[l]), a_ln_b=row(a_ln_b[l]), wsp=wsp, sbf=sbf, lb=lb,
        w_pa=w_pa[l].astype(BF16), hg_norm_g=row(hg_norm_g[l]),
        w_pb=w_pb[l].astype(BF16), w_o=w_o[l].astype(BF16),
        ln1_g=row(ln1_g[l]), ln1_b=row(ln1_b[l]),
        w_router=w_router[l].T.astype(BF16),
        router_bias=jnp.broadcast_to(router_bias[l].astype(F32)[:, None], (N_EXPERTS, LANES)),
        w_e_gate=w_e_gate[l].astype(BF16), w_e_up=w_e_up[l].astype(BF16),
        w_e_down=w_e_down[l].astype(BF16),
        w_sh_gate=w_sh_gate[l].astype(BF16), w_sh_up=w_sh_up[l].astype(BF16),
        w_sh_down=w_sh_down[l].astype(BF16),
        ln2_g=row(ln2_g[l]), ln2_b=row(ln2_b[l]),
    )


def kernel(x_prompt, x_sample, ln_in_g, ln_in_b, w_in, a_ln_g, a_ln_b, a_ws, a_sb, hg_lb_logits,
           hg_norm_g, w_pa, w_pb, w_o, ln1_g, ln1_b, w_router, router_bias, w_e_gate, w_e_up,
           w_e_down, w_sh_gate, w_sh_up, w_sh_down, ln2_g, ln2_b):
    p = _prepare_params(ln_in_g, ln_in_b, w_in, a_ln_g, a_ln_b, a_ws, a_sb, hg_lb_logits, hg_norm_g,
                        w_pa, w_pb, w_o, ln1_g, ln1_b, w_router, router_bias, w_e_gate, w_e_up,
                        w_e_down, w_sh_gate, w_sh_up, w_sh_down, ln2_g, ln2_b)
    return _encode(x_prompt, p), _encode(x_sample, p)
```

```python
import functools

import jax
import jax.numpy as jnp
from jax import lax
from jax.experimental import pallas as pl
from jax.experimental.pallas import tpu as pltpu
from jax.experimental.pallas import tpu_sc as plsc

F32 = jnp.float32
BF16 = jnp.bfloat16

D_MODEL = 1024
A_GROUPS = 8
A_WIDTH = 512
A_CHUNK = 128
HG_HEADS = 8
HG_DK = 128
HG_WIDTH = HG_HEADS * HG_DK
N_IN = 2 * A_WIDTH + 5 * HG_WIDTH + 2 * D_MODEL
N_EXPERTS = 64
TOP_K = 6
N_GROUPS = 8
TOPK_GROUPS = 4
GROUP_SIZE = N_EXPERTS // N_GROUPS
EXPERT_DIM = 256
SHARED_DIM = 256
ROUTED_SCALE = 2.5
DEPTH = 1
ALPHA = (2.0 * DEPTH) ** 0.25
LN_EPS = 1e-5
RMS_EPS = 1e-6

LANES = 128
TOKEN_TILE = 256
INPROJ_TILE = 256
HG_CHUNK = 128
HG_SAFE_LOGDECAY = 80.0
EXPERT_BLOCK = 1024
SC_CORES = 2
SC_SUBCORES = 16
SC_LANES = 16
SC_WINDOW = 32
SC_COMBINE_WINDOW = 8
VMEM_LIMIT = 56 * 1024 * 1024

_O_U, _O_V, _O_Q, _O_FF, _O_FB, _O_I, _O_G, _O_GA, _O_GB = (
    0, 512, 1024, 2048, 3072, 4096, 5120, 6144, 7168)


def _layer_norm(x, g, b):
    mu = jnp.mean(x, axis=-1, keepdims=True)
    xc = x - mu
    var = jnp.mean(xc * xc, axis=-1, keepdims=True)
    return xc * lax.rsqrt(var + LN_EPS) * g + b


def _bdot(a, b):
    return jnp.dot(a.astype(BF16), b.astype(BF16), preferred_element_type=F32)


def _pack_rows(x):
    w = x.shape[1] // 2
    lo = lax.bitcast_convert_type(x[:, :w].astype(BF16).astype(F32), jnp.uint32)
    hi = lax.bitcast_convert_type(x[:, w:].astype(BF16).astype(F32), jnp.uint32)
    return hi | (lo >> 16)


def _unpack_rows(p):
    lo = lax.bitcast_convert_type(p << 16, F32)
    hi = lax.bitcast_convert_type(p & jnp.uint32(0xFFFF0000), F32)
    return lo, hi


def _const_spec(shape):
    nd = len(shape)
    return pl.BlockSpec(shape, lambda *_: (0,) * nd, pipeline_mode=pl.Buffered(1))


def _inproj_kernel(x_ref, lng_ref, lnb_ref, win_ref, alng_ref, alnb_ref, wsp_ref, sbf_ref, lb_ref,
                   wpa_ref, ag_ref, q_ref, gf_ref, gb_ref, iv_ref, so_ref, sgb_ref, gmin_ref):
    tm = x_ref.shape[0]
    xb = _layer_norm(x_ref[...], lng_ref[...], lnb_ref[...]).astype(BF16)

    def sec(lo, width):
        return jnp.dot(xb, win_ref[:, lo:lo + width], preferred_element_type=F32)

    u = jax.nn.gelu(sec(_O_U, A_WIDTH))
    v = _layer_norm(jax.nn.gelu(sec(_O_V, A_WIDTH)), alng_ref[...], alnb_ref[...]).astype(BF16)
    lane = lax.broadcasted_iota(jnp.int32, (A_CHUNK, LANES), 1)
    left = lane < (A_WIDTH // A_GROUPS)
    zero = jnp.zeros((A_CHUNK, LANES), BF16)
    chunks = []
    for c in range(tm // A_CHUNK):
        vc = v[c * A_CHUNK:(c + 1) * A_CHUNK]
        cols = []
        for p in range(A_GROUPS // 2):
            vp = vc[:, p * LANES:(p + 1) * LANES]
            rhs = jnp.concatenate([jnp.where(left, vp, zero), jnp.where(left, zero, vp)], axis=0)
            cols.append(jnp.dot(wsp_ref[p], rhs, preferred_element_type=F32))
        chunks.append(jnp.concatenate(cols, axis=1) + sbf_ref[...])
    mixed = jnp.concatenate(chunks, axis=0)
    a = _bdot(u * mixed, wpa_ref[...])
    ag_ref[...] = (jax.nn.sigmoid(sec(_O_GA, D_MODEL)) * a).astype(BF16)

    q_ref[...] = jax.nn.silu(sec(_O_Q, HG_WIDTH)).astype(BF16)
    mins = []
    for d, (off, g_ref) in enumerate(((_O_FF, gf_ref), (_O_FB, gb_ref))):
        lb = lb_ref[d:d + 1, :]
        f = lb + (1.0 - lb) * jax.nn.sigmoid(sec(off, HG_WIDTH))
        g = jnp.log(f)
        g_ref[...] = g
        half = jnp.sum(g.reshape(tm // (HG_CHUNK // 2), HG_CHUNK // 2, HG_WIDTH), axis=1)
        mins.append(jnp.min(half, axis=-1, keepdims=True))
    gmin_ref[0] = jnp.broadcast_to(jnp.concatenate(mins, axis=0), gmin_ref.shape[1:])
    iv_ref[...] = sec(_O_I, HG_WIDTH).astype(BF16)
    so_ref[...] = jax.nn.silu(sec(_O_G, HG_WIDTH)).astype(BF16)
    sgb_ref[...] = jax.nn.sigmoid(sec(_O_GB, D_MODEL)).astype(BF16)


def _inproj(x, p):
    T = x.shape[0]
    tm = INPROJ_TILE
    nt = T // tm
    nh = 2 * tm // (HG_CHUNK // 2)
    tok = lambda w: pl.BlockSpec((tm, w), lambda i: (i, 0))
    outs = (
        jax.ShapeDtypeStruct((T, D_MODEL), BF16),
        jax.ShapeDtypeStruct((T, HG_WIDTH), BF16),
        jax.ShapeDtypeStruct((T, HG_WIDTH), F32),
        jax.ShapeDtypeStruct((T, HG_WIDTH), F32),
        jax.ShapeDtypeStruct((T, HG_WIDTH), BF16),
        jax.ShapeDtypeStruct((T, HG_WIDTH), BF16),
        jax.ShapeDtypeStruct((T, D_MODEL), BF16),
        jax.ShapeDtypeStruct((nt, nh, LANES), F32),
    )
    return pl.pallas_call(
        _inproj_kernel,
        grid=(nt,),
        in_specs=[
            tok(D_MODEL),
            _const_spec((1, D_MODEL)), _const_spec((1, D_MODEL)),
            _const_spec((D_MODEL, N_IN)),
            _const_spec((1, A_WIDTH)), _const_spec((1, A_WIDTH)),
            _const_spec((A_GROUPS // 2, A_CHUNK, 2 * A_CHUNK)),
            _const_spec((A_CHUNK, A_WIDTH)),
            _const_spec((2, HG_WIDTH)),
            _const_spec((A_WIDTH, D_MODEL)),
        ],
        out_specs=(tok(D_MODEL), tok(HG_WIDTH), tok(HG_WIDTH), tok(HG_WIDTH), tok(HG_WIDTH),
                   tok(HG_WIDTH), tok(D_MODEL), pl.BlockSpec((1, nh, LANES), lambda i: (i, 0, 0))),
        out_shape=outs,
        compiler_params=pltpu.CompilerParams(
            dimension_semantics=("parallel",), vmem_limit_bytes=VMEM_LIMIT),
        name="inproj",
    )(x, p["ln_in_g"], p["ln_in_b"], p["w_in"], p["a_ln_g"], p["a_ln_b"], p["wsp"], p["sbf"],
      p["lb"], p["w_pa"])


_NT = (((1,), (1,)), ((), ()))
_TN = (((0,), (0,)), ((), ()))


def _hgrn_direction(q_ref, g_ref, v_ref, st_ref, o_ref, tri, mask, fwd):
    C = q_ref.shape[0]
    g = g_ref[...]
    ghi = g.astype(BF16)
    glo = (g - ghi.astype(F32)).astype(BF16)
    b = jnp.dot(jnp.concatenate([tri, tri], axis=1), jnp.concatenate([ghi, glo], axis=0),
                preferred_element_type=F32)
    mid = C // 2 - 1 if fwd else C // 2
    end = C - 1 if fwd else 0
    r = b[mid:mid + 1, :]
    b_end = b[end:end + 1, :]
    qt = q_ref[...].astype(F32) * jnp.exp(b - r)
    kt = (1.0 - jnp.exp(g)) * jnp.exp(r - b)
    qtb = qt.astype(BF16)
    ktb = kt.astype(BF16)
    qhb = (qt * jnp.exp(r)).astype(BF16)
    khb = (kt * jnp.exp(b_end - r)).astype(BF16)
    decay = jnp.exp(b_end)
    v32 = v_ref[...].astype(F32)
    heads = [slice(h * HG_DK, (h + 1) * HG_DK) for h in range(HG_HEADS)]
    zero = jnp.zeros((C, HG_DK), BF16)
    mask2 = jnp.concatenate([mask, mask], axis=1)
    scores = []
    for p in range(HG_HEADS // 2):
        k1, k2 = ktb[:, heads[2 * p]], ktb[:, heads[2 * p + 1]]
        kk = jnp.concatenate([jnp.concatenate([k1, zero], axis=1),
                              jnp.concatenate([zero, k2], axis=1)], axis=0)
        s2 = lax.dot_general(qtb[:, 2 * p * HG_DK:(2 * p + 2) * HG_DK], kk, _NT,
                             preferred_element_type=F32)
        s2 = jnp.where(mask2, s2, 0.0).astype(BF16)
        scores += [s2[:, :C], s2[:, C:]]
    for h, sl in enumerate(heads):
        st = st_ref[h]
        vt = v32[:, sl].T.astype(BF16)
        o_ref[:, sl] = lax.dot_general(
            jnp.concatenate([scores[h], qhb[:, sl]], axis=1),
            jnp.concatenate([vt, st.astype(BF16)], axis=1), _NT,
            preferred_element_type=F32).astype(o_ref.dtype)
        st_ref[h] = st * decay[:, sl] + jnp.dot(vt, khb[:, sl], preferred_element_type=F32)


def _hgrn_direction_stepwise(q_ref, g_ref, v_ref, st_ref, o_ref, q32_ref, v32_ref, o32_ref, fwd):
    C = q_ref.shape[0]
    q32_ref[...] = q_ref[...].astype(F32)
    v32_ref[...] = v_ref[...].astype(F32)
    sub = 8
    rows = lax.broadcasted_iota(jnp.int32, (sub, HG_DK), 0)

    def group(i, carry):
        base = pl.multiple_of((i if fwd else C // sub - 1 - i) * sub, sub)
        f = jnp.exp(g_ref[pl.ds(base, sub), :])
        k = 1.0 - f
        q = q32_ref[pl.ds(base, sub), :]
        v = v32_ref[pl.ds(base, sub), :]
        for h in range(HG_HEADS):
            sl = slice(h * HG_DK, (h + 1) * HG_DK)
            st = st_ref[h]
            out = jnp.zeros((sub, HG_DK), F32)
            for r in (range(sub) if fwd else range(sub - 1, -1, -1)):
                v_t = jnp.where(rows == 0, v[r:r + 1, sl], 0.0).astype(BF16)
                k_t = jnp.broadcast_to(k[r:r + 1, sl], (sub, HG_DK)).astype(BF16)
                st = st * f[r:r + 1, sl] + lax.dot_general(v_t, k_t, _TN, preferred_element_type=F32)
                q_t = jnp.broadcast_to(q[r:r + 1, sl], (sub, HG_DK)).astype(BF16)
                o_t = lax.dot_general(q_t, st.astype(BF16), _NT, preferred_element_type=F32)
                out = jnp.where(rows == r, o_t, out)
            st_ref[h] = st
            o32_ref[pl.ds(base, sub), sl] = out
        return carry

    lax.fori_loop(0, C // sub, group, 0)
    o_ref[...] = o32_ref[...].astype(o_ref.dtype)


def _hgrn_kernel(safe_ref, qf_ref, qb_ref, gf_ref, gb_ref, vf_ref, vb_ref, tril_ref, triu_ref,
                 of_ref, ob_ref, sf_ref, sb_ref, q32_ref, v32_ref, o32_ref):
    b, j = pl.program_id(0), pl.program_id(1)
    nc = pl.num_programs(1)

    @pl.when(j == 0)
    def _():
        sf_ref[...] = jnp.zeros_like(sf_ref)
        sb_ref[...] = jnp.zeros_like(sb_ref)

    C = qf_ref.shape[0]
    row = lax.broadcasted_iota(jnp.int32, (C, C), 0)
    col = lax.broadcasted_iota(jnp.int32, (C, C), 1)
    safe_f = safe_ref[0, b * nc + j] != 0
    safe_b = safe_ref[1, b * nc + nc - 1 - j] != 0

    def forward(stepwise):
        if stepwise:
            _hgrn_direction_stepwise(qf_ref, gf_ref, vf_ref, sf_ref, of_ref, q32_ref, v32_ref, o32_ref, True)
        else:
            _hgrn_direction(qf_ref, gf_ref, vf_ref, sf_ref, of_ref, tril_ref[...], row >= col, True)

    def backward(stepwise):
        if stepwise:
            _hgrn_direction_stepwise(qb_ref, gb_ref, vb_ref, sb_ref, ob_ref, q32_ref, v32_ref, o32_ref, False)
        else:
            _hgrn_direction(qb_ref, gb_ref, vb_ref, sb_ref, ob_ref, triu_ref[...], row <= col, False)

    @pl.when(safe_f & safe_b)
    def _():
        forward(False)
        backward(False)

    @pl.when(jnp.logical_not(safe_f & safe_b))
    def _():
        @pl.when(safe_f)
        def _():
            forward(False)

        @pl.when(jnp.logical_not(safe_f))
        def _():
            forward(True)

        @pl.when(safe_b)
        def _():
            backward(False)

        @pl.when(jnp.logical_not(safe_b))
        def _():
            backward(True)


def _hgrn(q, gf, gb, iv, gmin, batch):
    T = q.shape[0]
    C = HG_CHUNK
    nc = T // batch // C
    nt, nh, _ = gmin.shape
    halves = gmin[:, :, 0].reshape(nt, 2, nh // 4, 2)
    safe = (jnp.min(halves, axis=-1) > -HG_SAFE_LOGDECAY).astype(jnp.int32)
    safe = safe.transpose(1, 0, 2).reshape(2, T // C)
    fwd = pl.BlockSpec((C, HG_WIDTH), lambda b, j, s: (b * nc + j, 0))
    bwd = pl.BlockSpec((C, HG_WIDTH), lambda b, j, s: (b * nc + nc - 1 - j, 0))
    const = lambda shape: pl.BlockSpec(shape, lambda b, j, s: (0,) * len(shape),
                                       pipeline_mode=pl.Buffered(1))
    row = lax.broadcasted_iota(jnp.int32, (C, C), 0)
    col = lax.broadcasted_iota(jnp.int32, (C, C), 1)
    tril = (row >= col).astype(BF16)
    triu = (row <= col).astype(BF16)
    grid_spec = pltpu.PrefetchScalarGridSpec(
        num_scalar_prefetch=1,
        grid=(batch, nc),
        in_specs=[fwd, bwd, fwd, bwd, fwd, bwd, const((C, C)), const((C, C))],
        out_specs=(fwd, bwd),
        scratch_shapes=[pltpu.VMEM((HG_HEADS, HG_DK, HG_DK), F32),
                        pltpu.VMEM((HG_HEADS, HG_DK, HG_DK), F32),
                        pltpu.VMEM((C, HG_WIDTH), F32), pltpu.VMEM((C, HG_WIDTH), F32),
                        pltpu.VMEM((C, HG_WIDTH), F32)],
    )
    return pl.pallas_call(
        _hgrn_kernel,
        grid_spec=grid_spec,
        out_shape=(jax.ShapeDtypeStruct((T, HG_WIDTH), BF16), jax.ShapeDtypeStruct((T, HG_WIDTH), BF16)),
        compiler_params=pltpu.CompilerParams(
            dimension_semantics=("parallel", "arbitrary"), vmem_limit_bytes=VMEM_LIMIT),
        name="hgrn",
    )(safe, q, q, gf, gb, iv, iv, tril, triu)


def _mix_kernel(x_ref, lng_ref, lnb_ref, of_ref, ob_ref, so_ref, sgb_ref, ag_ref, ng_ref, wpb_ref,
                wo_ref, l1g_ref, l1b_ref, wr_ref, rb_ref, tri_ref, x1_ref, x1p_ref, route_ref,
                cnt_ref, carry_ref):
    tm = x_ref.shape[0]

    @pl.when(pl.program_id(0) == 0)
    def _():
        carry_ref[...] = jnp.zeros_like(carry_ref)

    o = of_ref[...].astype(F32) + ob_ref[...].astype(F32)
    heads = []
    for h in range(HG_HEADS):
        oh = o[:, h * HG_DK:(h + 1) * HG_DK]
        heads.append(oh * lax.rsqrt(jnp.mean(oh * oh, axis=-1, keepdims=True) + RMS_EPS))
    rn = jnp.concatenate(heads, axis=1) * ng_ref[...] * so_ref[...].astype(F32)
    r = _bdot(rn, wpb_ref[...])
    mixed = ag_ref[...].astype(F32) + sgb_ref[...].astype(F32) * r
    y = _bdot(mixed, wo_ref[...])
    xn = _layer_norm(x_ref[...], lng_ref[...], lnb_ref[...])
    x1 = _layer_norm(ALPHA * xn + y, l1g_ref[...], l1b_ref[...])
    x1_ref[...] = x1
    x1p_ref[...] = _pack_rows(x1)

    neg = jnp.float32(-jnp.inf)
    reps = tm // LANES
    scores = jax.nn.sigmoid(lax.dot_general(wr_ref[...], x1.astype(BF16), _NT,
                                            preferred_element_type=F32))
    biased = (scores + jnp.concatenate([rb_ref[...]] * reps, axis=1)).reshape(
        N_GROUPS, GROUP_SIZE, tm)
    sub = lax.broadcasted_iota(jnp.int32, biased.shape, 1).astype(F32)
    m1 = jnp.max(biased, axis=1, keepdims=True)
    first = jnp.min(jnp.where(biased == m1, sub, float(GROUP_SIZE)), axis=1, keepdims=True)
    m2 = jnp.max(jnp.where(sub == first, neg, biased), axis=1, keepdims=True)
    gs = (m1 + m2).reshape(N_GROUPS, tm)
    grp = lax.broadcasted_iota(jnp.int32, (N_GROUPS, tm), 0)
    ahead = jnp.zeros((N_GROUPS, tm), F32)
    for d in range(1, N_GROUPS):
        other = pltpu.roll(gs, d, 0)
        tie = jnp.where(grp >= d, 1.0, 0.0)
        ahead = ahead + jnp.where(other > gs, 1.0, jnp.where(other == gs, tie, 0.0))
    keep = (ahead < TOPK_GROUPS).reshape(N_GROUPS, 1, tm)
    allowed = jnp.where(keep, biased, neg).reshape(N_EXPERTS, tm)
    row = lax.broadcasted_iota(jnp.int32, (N_EXPERTS, tm), 0).astype(F32)
    sel = jnp.zeros((N_EXPERTS, tm), F32)
    picks = []
    for _ in range(TOP_K):
        m = jnp.max(allowed, axis=0, keepdims=True)
        first = jnp.min(jnp.where(allowed == m, row, float(N_EXPERTS)), axis=0, keepdims=True)
        hit = row == first
        picks.append((first, hit, jnp.sum(jnp.where(hit, scores, 0.0), axis=0, keepdims=True)))
        sel = jnp.where(hit, 1.0, sel)
        allowed = jnp.where(hit, neg, allowed)
    wsum = picks[0][2]
    for pk in picks[1:]:
        wsum = wsum + pk[2]
    selb = sel.astype(BF16)
    carry = carry_ref[...]
    before = (jnp.dot(selb, tri_ref[...], preferred_element_type=F32)
              + jnp.concatenate([carry] * reps, axis=1))
    total = carry + jnp.dot(selb, jnp.ones((tm, LANES), BF16), preferred_element_type=F32)
    carry_ref[...] = total
    cnt_ref[...] = total
    blank = [jnp.zeros((1, tm), F32)] * (8 - TOP_K)
    route_ref[...] = jnp.concatenate(
        [pk[0] for pk in picks] + blank
        + [jnp.sum(jnp.where(pk[1], before, 0.0), axis=0, keepdims=True) for pk in picks] + blank
        + [pk[2] / wsum * ROUTED_SCALE for pk in picks] + blank, axis=0)


def _mix(x, of, ob, so, sgb, ag, p):
    T = x.shape[0]
    tm = TOKEN_TILE
    tok = lambda w: pl.BlockSpec((tm, w), lambda i: (i, 0))
    row = lax.broadcasted_iota(jnp.int32, (tm, tm), 0)
    col = lax.broadcasted_iota(jnp.int32, (tm, tm), 1)
    tri = (row < col).astype(BF16)
    return pl.pallas_call(
        _mix_kernel,
        grid=(T // tm,),
        in_specs=[
            tok(D_MODEL), _const_spec((1, D_MODEL)), _const_spec((1, D_MODEL)),
            tok(HG_WIDTH), tok(HG_WIDTH), tok(HG_WIDTH), tok(D_MODEL), tok(D_MODEL),
            _const_spec((1, HG_WIDTH)),
            _const_spec((HG_WIDTH, D_MODEL)), _const_spec((D_MODEL, D_MODEL)),
            _const_spec((1, D_MODEL)), _const_spec((1, D_MODEL)),
            _const_spec((N_EXPERTS, D_MODEL)), _const_spec((N_EXPERTS, LANES)),
            _const_spec((tm, tm)),
        ],
        out_specs=(tok(D_MODEL), tok(D_MODEL // 2), pl.BlockSpec((24, tm), lambda i: (0, i)),
                   pl.BlockSpec((N_EXPERTS, LANES), lambda i: (0, 0))),
        out_shape=(jax.ShapeDtypeStruct((T, D_MODEL), F32),
                   jax.ShapeDtypeStruct((T, D_MODEL // 2), jnp.uint32),
                   jax.ShapeDtypeStruct((24, T), F32),
                   jax.ShapeDtypeStruct((N_EXPERTS, LANES), F32)),
        scratch_shapes=[pltpu.VMEM((N_EXPERTS, LANES), F32)],
        compiler_params=pltpu.CompilerParams(
            dimension_semantics=("arbitrary",), vmem_limit_bytes=VMEM_LIMIT),
        name="mix",
    )(x, p["ln_in_g"], p["ln_in_b"], of, ob, so, sgb, ag, p["hg_norm_g"], p["w_pb"], p["w_o"],
      p["ln1_g"], p["ln1_b"], p["w_router"], p["router_bias"], tri)


def _sc_mesh():
    return plsc.VectorSubcoreMesh(core_axis_name="c", subcore_axis_name="s",
                                  num_cores=SC_CORES, num_subcores=SC_SUBCORES)


def _sc_worker():
    return lax.axis_index("s") * SC_CORES + lax.axis_index("c")


def _sc_dispatch(x1p, dest, n_rows):
    T, w = x1p.shape
    W = SC_WINDOW
    per_worker = T // W // (SC_CORES * SC_SUBCORES)

    def body(x_hbm, d_hbm, o_hbm, rows_v, idx_v, sem):
        first = _sc_worker() * per_worker
        for k in range(TOP_K):
            pltpu.sync_copy(d_hbm.at[k, pl.ds(first, per_worker)], idx_v.at[k])

        @pl.loop(0, per_worker)
        def _(j):
            pltpu.sync_copy(x_hbm.at[pl.ds((first + j) * W, W)], rows_v)
            copies = [pltpu.async_copy(rows_v, o_hbm.at[idx_v.at[k, j]], sem) for k in range(TOP_K)]
            for c in copies:
                c.wait()

    return pl.kernel(
        body,
        out_type=jax.ShapeDtypeStruct((n_rows, w), jnp.uint32),
        mesh=_sc_mesh(),
        scratch_types=[pltpu.VMEM((W, w), jnp.uint32), pltpu.VMEM((TOP_K, per_worker, W), jnp.int32),
                       pltpu.SemaphoreType.DMA],
        name="sc_dispatch",
    )(x1p, dest.reshape(TOP_K, T // W, W))


def _sc_combine(ys, dest, gates):
    T = dest.shape[1]
    W = SC_COMBINE_WINDOW
    w = ys.shape[1]
    per_worker = T // W // (SC_CORES * SC_SUBCORES)
    assert per_worker % 2 == 0 and per_worker * W * SC_CORES * SC_SUBCORES == T

    def body(y_hbm, d_hbm, g_hbm, o_hbm, idx_v, *scratch):
        slots = [dict(rows=scratch[s], gate=scratch[2 + s], out=scratch[4 + s], gsem=scratch[6 + s],
                      wsem=scratch[8 + s]) for s in range(2)]
        first = _sc_worker() * per_worker
        for k in range(TOP_K):
            pltpu.sync_copy(d_hbm.at[k, pl.ds(first * W, per_worker * W)], idx_v.at[k])

        def loads(b, win):
            i = win - first
            return [pltpu.make_async_copy(g_hbm.at[win], b["gate"], b["gsem"])] + [
                pltpu.make_async_copy(y_hbm.at[idx_v.at[k, pl.ds(i * W, W)]],
                                      b["rows"].at[k], b["gsem"]) for k in range(TOP_K)]

        def write_back(b, win):
            return pltpu.make_async_copy(b["out"], o_hbm.at[pl.ds(win * W, W)], b["wsem"])

        def fetch(b, win):
            for c in loads(b, win):
                c.start()

        def reduce_rows(b):
            rows, gate_v, out = b["rows"], b["gate"], b["out"]

            @pl.loop(0, W)
            def _(j):
                gate = [gate_v[pl.ds((j * TOP_K + k) * SC_LANES, SC_LANES)] for k in range(TOP_K)]

                @plsc.parallel_loop(0, w, step=SC_LANES, unroll=4)
                def _(col):
                    lo, hi = [], []
                    for k in range(TOP_K):
                        p = rows[k, j, pl.ds(col, SC_LANES)]
                        lo.append(plsc.bitcast(p << 16, F32) * gate[k])
                        hi.append(plsc.bitcast(p & jnp.uint32(0xFFFF0000), F32) * gate[k])
                    out[j, pl.ds(col, SC_LANES)] = (lo[0] + lo[1]) + (lo[2] + lo[3]) + (lo[4] + lo[5])
                    out[j, pl.ds(w + col, SC_LANES)] = (hi[0] + hi[1]) + (hi[2] + hi[3]) + (hi[4] + hi[5])

        def process(b, win, not_first):
            for c in loads(b, win):
                c.wait()

            @pl.when(not_first)
            def _():
                write_back(b, win).wait()

            reduce_rows(b)
            write_back(b, win).start()

        fetch(slots[0], first)

        @pl.loop(0, per_worker, step=2)
        def _(i):
            win = first + i
            fetch(slots[1], win + 1)
            process(slots[0], win, i > 0)

            @pl.when(i + 2 < per_worker)
            def _():
                fetch(slots[0], win + 2)

            process(slots[1], win + 1, i > 0)

        for b in slots:
            write_back(b, first).wait()

    slot_types = ([pltpu.VMEM((TOP_K, per_worker * W), jnp.int32)]
                  + [pltpu.VMEM((TOP_K, W, w), jnp.uint32)] * 2
                  + [pltpu.VMEM((W * TOP_K * SC_LANES,), F32)] * 2
                  + [pltpu.VMEM((W, 2 * w), F32)] * 2
                  + [pltpu.SemaphoreType.DMA] * 4)
    return pl.kernel(
        body,
        out_type=jax.ShapeDtypeStruct((T, 2 * w), F32),
        mesh=_sc_mesh(),
        scratch_types=slot_types,
        compiler_params=pltpu.CompilerParams(needs_layout_passes=False),
        name="sc_combine",
    )(ys, dest, gates)


def _experts_kernel(be_ref, nv_ref, xs_ref, wg_ref, wu_ref, wd_ref, ys_ref):
    del be_ref
    n_valid = nv_ref[pl.program_id(0)]
    mb, w = xs_ref.shape

    @pl.when(n_valid > 0)
    def _():
        keep = lax.broadcasted_iota(jnp.int32, (mb, w), 0) < n_valid
        lo, hi = _unpack_rows(jnp.where(keep, xs_ref[...], jnp.uint32(0)))
        lo, hi = lo.astype(BF16), hi.astype(BF16)

        def proj(w_ref):
            return (jnp.dot(lo, w_ref[0, :w, :], preferred_element_type=F32)
                    + jnp.dot(hi, w_ref[0, w:, :], preferred_element_type=F32))

        hb = jax.nn.silu(proj(wg_ref)) * proj(wu_ref)
        ys_ref[...] = _pack_rows(jnp.dot(hb.astype(BF16), wd_ref[0], preferred_element_type=F32))

    @pl.when(n_valid <= 0)
    def _():
        ys_ref[...] = jnp.zeros_like(ys_ref)


def _experts(xs, blk_exp, n_valid, p):
    n_rows, w = xs.shape
    mb = n_rows // blk_exp.shape[0]
    grid_spec = pltpu.PrefetchScalarGridSpec(
        num_scalar_prefetch=2,
        grid=(n_rows // mb,),
        in_specs=[
            pl.BlockSpec((mb, w), lambda i, be, nv: (i, 0)),
            pl.BlockSpec((1, D_MODEL, EXPERT_DIM), lambda i, be, nv: (be[i], 0, 0)),
            pl.BlockSpec((1, D_MODEL, EXPERT_DIM), lambda i, be, nv: (be[i], 0, 0)),
            pl.BlockSpec((1, EXPERT_DIM, D_MODEL), lambda i, be, nv: (be[i], 0, 0)),
        ],
        out_specs=pl.BlockSpec((mb, w), lambda i, be, nv: (i, 0)),
    )
    return pl.pallas_call(
        _experts_kernel,
        grid_spec=grid_spec,
        out_shape=jax.ShapeDtypeStruct((n_rows, w), jnp.uint32),
        compiler_params=pltpu.CompilerParams(
            dimension_semantics=("arbitrary",), vmem_limit_bytes=VMEM_LIMIT),
        name="experts",
    )(blk_exp, n_valid, xs, p["w_e_gate"], p["w_e_up"], p["w_e_down"])


def _final_kernel(x1_ref, routed_ref, wsg_ref, wsu_ref, wsd_ref, l2g_ref, l2b_ref, out_ref):
    x1 = x1_ref[...]
    xb = x1.astype(BF16)
    hs = (jax.nn.silu(jnp.dot(xb, wsg_ref[...], preferred_element_type=F32))
          * jnp.dot(xb, wsu_ref[...], preferred_element_type=F32))
    shared = jnp.dot(hs.astype(BF16), wsd_ref[...], preferred_element_type=F32)
    out_ref[...] = _layer_norm(ALPHA * x1 + (routed_ref[...] + shared), l2g_ref[...], l2b_ref[...])


def _final(x1, routed, p):
    T = x1.shape[0]
    tm = TOKEN_TILE
    tok = lambda w: pl.BlockSpec((tm, w), lambda i: (i, 0))
    return pl.pallas_call(
        _final_kernel,
        grid=(T // tm,),
        in_specs=[
            tok(D_MODEL), tok(D_MODEL),
            _const_spec((D_MODEL, SHARED_DIM)), _const_spec((D_MODEL, SHARED_DIM)),
            _const_spec((SHARED_DIM, D_MODEL)),
            _const_spec((1, D_MODEL)), _const_spec((1, D_MODEL)),
        ],
        out_specs=tok(D_MODEL),
        out_shape=jax.ShapeDtypeStruct((T, D_MODEL), F32),
        compiler_params=pltpu.CompilerParams(
            dimension_semantics=("parallel",), vmem_limit_bytes=VMEM_LIMIT),
        name="final",
    )(x1, routed, p["w_sh_gate"], p["w_sh_up"], p["w_sh_down"], p["ln2_g"], p["ln2_b"])


def _routing_layout(route, counts, n_tokens):
    mb = EXPERT_BLOCK
    n_blocks = -(-n_tokens * TOP_K // mb) + N_EXPERTS
    idx = route[0:TOP_K].astype(jnp.int32)
    rank = route[8:8 + TOP_K].astype(jnp.int32)
    counts = counts.astype(jnp.int32)
    padded = (counts + mb - 1) // mb * mb
    pad_end = jnp.cumsum(padded)
    pad_start = pad_end - padded
    experts = jnp.arange(N_EXPERTS, dtype=jnp.int32)
    dest = rank + jnp.sum(jnp.where(idx[:, :, None] == experts, pad_start, 0), axis=-1)
    blk_start = jnp.arange(n_blocks, dtype=jnp.int32) * mb
    blk_exp = jnp.minimum(
        jnp.sum((pad_end[None, :] <= blk_start[:, None]).astype(jnp.int32), axis=1), N_EXPERTS - 1)
    valid_end = jnp.sum(jnp.where(blk_exp[:, None] == experts, pad_start + counts, 0), axis=-1)
    n_valid = jnp.clip(valid_end - blk_start, 0, mb).astype(jnp.int32)
    gates = jnp.broadcast_to(
        route[16:16 + TOP_K].T.reshape(n_tokens // SC_COMBINE_WINDOW, SC_COMBINE_WINDOW, TOP_K, 1),
        (n_tokens // SC_COMBINE_WINDOW, SC_COMBINE_WINDOW, TOP_K, SC_LANES)).reshape(
            n_tokens // SC_COMBINE_WINDOW, -1)
    return dest, gates, blk_exp, n_valid, n_blocks * mb


def _encode(x, p):
    batch, seq, _ = x.shape
    T = batch * seq
    xt = x.reshape(T, D_MODEL)
    ag, q, gf, gb, iv, so, sgb, gmin = _inproj(xt, p)
    of, ob = _hgrn(q, gf, gb, iv, gmin, batch)
    x1, x1p, route, cnt = _mix(xt, of, ob, so, sgb, ag, p)
    dest, gates, blk_exp, n_valid, n_rows = _routing_layout(route, cnt[:, 0], T)
    xs = _sc_dispatch(x1p, dest, n_rows)
    ys = _experts(xs, blk_exp, n_valid, p)
    out = _final(x1, _sc_combine(ys, dest, gates), p)
    return out.reshape(batch, seq, D_MODEL)


def _prepare_params(ln_in_g, ln_in_b, w_in, a_ln_g, a_ln_b, a_ws, a_sb, hg_lb_logits, hg_norm_g,
                    w_pa, w_pb, w_o, ln1_g, ln1_b, w_router, router_bias, w_e_gate, w_e_up,
                    w_e_down, w_sh_gate, w_sh_up, w_sh_down, ln2_g, ln2_b):
    l = 0
    row = lambda v: v.reshape(1, -1).astype(F32)
    ws = a_ws[l].astype(BF16)
    wsp = jnp.concatenate([ws[0::2], ws[1::2]], axis=2)
    sbf = jnp.repeat(a_sb[l].astype(F32), A_WIDTH // A_GROUPS, axis=1)
    lb = jnp.cumsum(jax.nn.softmax(hg_lb_logits.astype(F32), axis=1), axis=1)[:, l]
    return dict(
        ln_in_g=row(ln_in_g), ln_in_b=row(ln_in_b), w_in=w_in[l].astype(BF16),
        a_ln_g=row(a_ln_g[l]), a_ln_b=row(a_ln_b[l]), wsp=wsp, sbf=sbf, lb=lb,
        w_pa=w_pa[l].astype(BF16), hg_norm_g=row(hg_norm_g[l]),
        w_pb=w_pb[l].astype(BF16), w_o=w_o[l].astype(BF16),
        ln1_g=row(ln1_g[l]), ln1_b=row(ln1_b[l]),
        w_router=w_router[l].T.astype(BF16),
        router_bias=jnp.broadcast_to(router_bias[l].astype(F32)[:, None], (N_EXPERTS, LANES)),
        w_e_gate=w_e_gate[l].astype(BF16), w_e_up=w_e_up[l].astype(BF16),
        w_e_down=w_e_down[l].astype(BF16),
        w_sh_gate=w_sh_gate[l].astype(BF16), w_sh_up=w_sh_up[l].astype(BF16),
        w_sh_down=w_sh_down[l].astype(BF16),
        ln2_g=row(ln2_g[l]), ln2_b=row(ln2_b[l]),
    )


def kernel(x_prompt, x_sample, ln_in_g, ln_in_b, w_in, a_ln_g, a_ln_b, a_ws, a_sb, hg_lb_logits,
           hg_norm_g, w_pa, w_pb, w_o, ln1_g, ln1_b, w_router, router_bias, w_e_gate, w_e_up,
           w_e_down, w_sh_gate, w_sh_up, w_sh_down, ln2_g, ln2_b):
    p = _prepare_params(ln_in_g, ln_in_b, w_in, a_ln_g, a_ln_b, a_ws, a_sb, hg_lb_logits, hg_norm_g,
                        w_pa, w_pb, w_o, ln1_g, ln1_b, w_router, router_bias, w_e_gate, w_e_up,
                        w_e_down, w_sh_gate, w_sh_up, w_sh_down, ln2_g, ln2_b)
    return _encode(x_prompt, p), _encode(x_sample, p)
```

```python
import functools

import jax
import jax.numpy as jnp
from jax import lax
from jax.experimental import pallas as pl
from jax.experimental.pallas import tpu as pltpu
from jax.experimental.pallas import tpu_sc as plsc

F32 = jnp.float32
BF16 = jnp.bfloat16

D_MODEL = 1024
A_GROUPS = 8
A_WIDTH = 512
A_CHUNK = 128
HG_HEADS = 8
HG_DK = 128
HG_WIDTH = HG_HEADS * HG_DK
N_IN = 2 * A_WIDTH + 5 * HG_WIDTH + 2 * D_MODEL
N_EXPERTS = 64
TOP_K = 6
N_GROUPS = 8
TOPK_GROUPS = 4
GROUP_SIZE = N_EXPERTS // N_GROUPS
EXPERT_DIM = 256
SHARED_DIM = 256
ROUTED_SCALE = 2.5
DEPTH = 1
ALPHA = (2.0 * DEPTH) ** 0.25
LN_EPS = 1e-5
RMS_EPS = 1e-6

LANES = 128
TOKEN_TILE = 512
INPROJ_TILE = 256
HG_CHUNK = 128
HG_CHUNKS_PER_STEP = 2
HG_SAFE_LOGDECAY = 80.0
EXPERT_BLOCK = 1024
SC_CORES = 2
SC_SUBCORES = 16
SC_LANES = 16
SC_WINDOW = 32
SC_COMBINE_WINDOW = 8
VMEM_LIMIT = 56 * 1024 * 1024

_O_U, _O_V, _O_Q, _O_FF, _O_FB, _O_I, _O_G, _O_GA, _O_GB = (
    0, 512, 1024, 2048, 3072, 4096, 5120, 6144, 7168)


def _layer_norm(x, g, b):
    mu = jnp.mean(x, axis=-1, keepdims=True)
    xc = x - mu
    var = jnp.mean(xc * xc, axis=-1, keepdims=True)
    return xc * lax.rsqrt(var + LN_EPS) * g + b


def _bdot(a, b):
    return jnp.dot(a.astype(BF16), b.astype(BF16), preferred_element_type=F32)


def _pack_rows(x):
    w = x.shape[1] // 2
    lo = lax.bitcast_convert_type(x[:, :w].astype(BF16).astype(F32), jnp.uint32)
    hi = lax.bitcast_convert_type(x[:, w:].astype(BF16).astype(F32), jnp.uint32)
    return hi | (lo >> 16)


def _unpack_rows(p):
    lo = lax.bitcast_convert_type(p << 16, F32)
    hi = lax.bitcast_convert_type(p & jnp.uint32(0xFFFF0000), F32)
    return lo, hi


def _const_spec(shape):
    nd = len(shape)
    return pl.BlockSpec(shape, lambda *_: (0,) * nd, pipeline_mode=pl.Buffered(1))


def _inproj_kernel(x_ref, lng_ref, lnb_ref, win_ref, alng_ref, alnb_ref, wsp_ref, sbf_ref, lb_ref,
                   wpa_ref, ag_ref, q_ref, gf_ref, gb_ref, iv_ref, so_ref, sgb_ref, gmin_ref):
    tm = x_ref.shape[0]
    xb = _layer_norm(x_ref[...], lng_ref[...], lnb_ref[...]).astype(BF16)

    def sec(lo, width):
        return jnp.dot(xb, win_ref[:, lo:lo + width], preferred_element_type=F32)

    u = jax.nn.gelu(sec(_O_U, A_WIDTH))
    v = _layer_norm(jax.nn.gelu(sec(_O_V, A_WIDTH)), alng_ref[...], alnb_ref[...]).astype(BF16)
    lane = lax.broadcasted_iota(jnp.int32, (A_CHUNK, LANES), 1)
    left = lane < (A_WIDTH // A_GROUPS)
    zero = jnp.zeros((A_CHUNK, LANES), BF16)
    chunks = []
    for c in range(tm // A_CHUNK):
        vc = v[c * A_CHUNK:(c + 1) * A_CHUNK]
        cols = []
        for p in range(A_GROUPS // 2):
            vp = vc[:, p * LANES:(p + 1) * LANES]
            rhs = jnp.concatenate([jnp.where(left, vp, zero), jnp.where(left, zero, vp)], axis=0)
            cols.append(jnp.dot(wsp_ref[p], rhs, preferred_element_type=F32))
        chunks.append(jnp.concatenate(cols, axis=1) + sbf_ref[...])
    mixed = jnp.concatenate(chunks, axis=0)
    a = _bdot(u * mixed, wpa_ref[...])
    ag_ref[...] = (jax.nn.sigmoid(sec(_O_GA, D_MODEL)) * a).astype(BF16)

    q_ref[...] = jax.nn.silu(sec(_O_Q, HG_WIDTH)).astype(BF16)
    mins = []
    for d, (off, g_ref) in enumerate(((_O_FF, gf_ref), (_O_FB, gb_ref))):
        lb = lb_ref[d:d + 1, :]
        f = lb + (1.0 - lb) * jax.nn.sigmoid(sec(off, HG_WIDTH))
        g = jnp.log(f)
        g_ref[...] = g
        half = jnp.sum(g.reshape(tm // (HG_CHUNK // 2), HG_CHUNK // 2, HG_WIDTH), axis=1)
        mins.append(jnp.min(half, axis=-1, keepdims=True))
    gmin_ref[0] = jnp.broadcast_to(jnp.concatenate(mins, axis=0), gmin_ref.shape[1:])
    iv_ref[...] = sec(_O_I, HG_WIDTH).astype(BF16)
    so_ref[...] = jax.nn.silu(sec(_O_G, HG_WIDTH)).astype(BF16)
    sgb_ref[...] = jax.nn.sigmoid(sec(_O_GB, D_MODEL)).astype(BF16)


def _inproj(x, p):
    T = x.shape[0]
    tm = INPROJ_TILE
    nt = T // tm
    nh = 2 * tm // (HG_CHUNK // 2)
    tok = lambda w: pl.BlockSpec((tm, w), lambda i: (i, 0))
    outs = (
        jax.ShapeDtypeStruct((T, D_MODEL), BF16),
        jax.ShapeDtypeStruct((T, HG_WIDTH), BF16),
        jax.ShapeDtypeStruct((T, HG_WIDTH), F32),
        jax.ShapeDtypeStruct((T, HG_WIDTH), F32),
        jax.ShapeDtypeStruct((T, HG_WIDTH), BF16),
        jax.ShapeDtypeStruct((T, HG_WIDTH), BF16),
        jax.ShapeDtypeStruct((T, D_MODEL), BF16),
        jax.ShapeDtypeStruct((nt, nh, LANES), F32),
    )
    return pl.pallas_call(
        _inproj_kernel,
        grid=(nt,),
        in_specs=[
            tok(D_MODEL),
            _const_spec((1, D_MODEL)), _const_spec((1, D_MODEL)),
            _const_spec((D_MODEL, N_IN)),
            _const_spec((1, A_WIDTH)), _const_spec((1, A_WIDTH)),
            _const_spec((A_GROUPS // 2, A_CHUNK, 2 * A_CHUNK)),
            _const_spec((A_CHUNK, A_WIDTH)),
            _const_spec((2, HG_WIDTH)),
            _const_spec((A_WIDTH, D_MODEL)),
        ],
        out_specs=(tok(D_MODEL), tok(HG_WIDTH), tok(HG_WIDTH), tok(HG_WIDTH), tok(HG_WIDTH),
                   tok(HG_WIDTH), tok(D_MODEL), pl.BlockSpec((1, nh, LANES), lambda i: (i, 0, 0))),
        out_shape=outs,
        compiler_params=pltpu.CompilerParams(
            dimension_semantics=("parallel",), vmem_limit_bytes=VMEM_LIMIT),
        name="inproj",
    )(x, p["ln_in_g"], p["ln_in_b"], p["w_in"], p["a_ln_g"], p["a_ln_b"], p["wsp"], p["sbf"],
      p["lb"], p["w_pa"])


_NT = (((1,), (1,)), ((), ()))
_TN = (((0,), (0,)), ((), ()))


def _hgrn_direction(q_ref, g_ref, v_ref, st_ref, o_ref, tri, mask, fwd):
    C = q_ref.shape[0]
    g = g_ref[...]
    ghi = g.astype(BF16)
    glo = (g - ghi.astype(F32)).astype(BF16)
    b = jnp.dot(jnp.concatenate([tri, tri], axis=1), jnp.concatenate([ghi, glo], axis=0),
                preferred_element_type=F32)
    mid = C // 2 - 1 if fwd else C // 2
    end = C - 1 if fwd else 0
    r = b[mid:mid + 1, :]
    b_end = b[end:end + 1, :]
    qt = q_ref[...].astype(F32) * jnp.exp(b - r)
    kt = (1.0 - jnp.exp(g)) * jnp.exp(r - b)
    qtb = qt.astype(BF16)
    ktb = kt.astype(BF16)
    qhb = (qt * jnp.exp(r)).astype(BF16)
    khb = (kt * jnp.exp(b_end - r)).astype(BF16)
    decay = jnp.exp(b_end)
    v32 = v_ref[...].astype(F32)
    heads = [slice(h * HG_DK, (h + 1) * HG_DK) for h in range(HG_HEADS)]
    zero = jnp.zeros((C, HG_DK), BF16)
    mask2 = jnp.concatenate([mask, mask], axis=1)
    scores = []
    for p in range(HG_HEADS // 2):
        k1, k2 = ktb[:, heads[2 * p]], ktb[:, heads[2 * p + 1]]
        kk = jnp.concatenate([jnp.concatenate([k1, zero], axis=1),
                              jnp.concatenate([zero, k2], axis=1)], axis=0)
        s2 = lax.dot_general(qtb[:, 2 * p * HG_DK:(2 * p + 2) * HG_DK], kk, _NT,
                             preferred_element_type=F32)
        s2 = jnp.where(mask2, s2, 0.0).astype(BF16)
        scores += [s2[:, :C], s2[:, C:]]
    for h, sl in enumerate(heads):
        st = st_ref[h]
        vt = v32[:, sl].T.astype(BF16)
        o_ref[:, sl] = lax.dot_general(
            jnp.concatenate([scores[h], qhb[:, sl]], axis=1),
            jnp.concatenate([vt, st.astype(BF16)], axis=1), _NT,
            preferred_element_type=F32).astype(o_ref.dtype)
        st_ref[h] = st * decay[:, sl] + jnp.dot(vt, khb[:, sl], preferred_element_type=F32)


def _hgrn_direction_stepwise(q_ref, g_ref, v_ref, st_ref, o_ref, q32_ref, v32_ref, o32_ref, fwd):
    C = q_ref.shape[0]
    q32_ref[...] = q_ref[...].astype(F32)
    v32_ref[...] = v_ref[...].astype(F32)
    sub = 8
    rows = lax.broadcasted_iota(jnp.int32, (sub, HG_DK), 0)

    def group(i, carry):
        base = pl.multiple_of((i if fwd else C // sub - 1 - i) * sub, sub)
        f = jnp.exp(g_ref[pl.ds(base, sub), :])
        k = 1.0 - f
        q = q32_ref[pl.ds(base, sub), :]
        v = v32_ref[pl.ds(base, sub), :]
        for h in range(HG_HEADS):
            sl = slice(h * HG_DK, (h + 1) * HG_DK)
            st = st_ref[h]
            out = jnp.zeros((sub, HG_DK), F32)
            for r in (range(sub) if fwd else range(sub - 1, -1, -1)):
                v_t = jnp.where(rows == 0, v[r:r + 1, sl], 0.0).astype(BF16)
                k_t = jnp.broadcast_to(k[r:r + 1, sl], (sub, HG_DK)).astype(BF16)
                st = st * f[r:r + 1, sl] + lax.dot_general(v_t, k_t, _TN, preferred_element_type=F32)
                q_t = jnp.broadcast_to(q[r:r + 1, sl], (sub, HG_DK)).astype(BF16)
                o_t = lax.dot_general(q_t, st.astype(BF16), _NT, preferred_element_type=F32)
                out = jnp.where(rows == r, o_t, out)
            st_ref[h] = st
            o32_ref[pl.ds(base, sub), sl] = out
        return carry

    lax.fori_loop(0, C // sub, group, 0)
    o_ref[...] = o32_ref[...].astype(o_ref.dtype)


def _hgrn_kernel(safe_ref, qf_ref, qb_ref, gf_ref, gb_ref, vf_ref, vb_ref, tril_ref, triu_ref,
                 of_ref, ob_ref, sf_ref, sb_ref, q32_ref, v32_ref, o32_ref):
    b, j = pl.program_id(0), pl.program_id(1)
    ns = pl.num_programs(1)
    n = HG_CHUNKS_PER_STEP
    C = qf_ref.shape[0] // n

    @pl.when(j == 0)
    def _():
        sf_ref[...] = jnp.zeros_like(sf_ref)
        sb_ref[...] = jnp.zeros_like(sb_ref)

    row = lax.broadcasted_iota(jnp.int32, (C, C), 0)
    col = lax.broadcasted_iota(jnp.int32, (C, C), 1)
    safe_f = [safe_ref[0, (b * ns + j) * n + u] != 0 for u in range(n)]
    safe_b = [safe_ref[1, (b * ns + ns - 1 - j) * n + u] != 0 for u in range(n)]
    part = lambda ref, u: ref.at[pl.ds(u * C, C), :]

    def forward(u, stepwise):
        refs = (part(qf_ref, u), part(gf_ref, u), part(vf_ref, u), sf_ref, part(of_ref, u))
        if stepwise:
            _hgrn_direction_stepwise(*refs, q32_ref, v32_ref, o32_ref, True)
        else:
            _hgrn_direction(*refs, tril_ref[...], row >= col, True)

    def backward(u, stepwise):
        refs = (part(qb_ref, u), part(gb_ref, u), part(vb_ref, u), sb_ref, part(ob_ref, u))
        if stepwise:
            _hgrn_direction_stepwise(*refs, q32_ref, v32_ref, o32_ref, False)
        else:
            _hgrn_direction(*refs, triu_ref[...], row <= col, False)

    all_safe = functools.reduce(jnp.logical_and, safe_f + safe_b)

    @pl.when(all_safe)
    def _():
        for u in range(n):
            forward(u, False)
            backward(n - 1 - u, False)

    @pl.when(jnp.logical_not(all_safe))
    def _():
        for u in range(n):
            pl.when(safe_f[u])(functools.partial(forward, u, False))
            pl.when(jnp.logical_not(safe_f[u]))(functools.partial(forward, u, True))
        for u in reversed(range(n)):
            pl.when(safe_b[u])(functools.partial(backward, u, False))
            pl.when(jnp.logical_not(safe_b[u]))(functools.partial(backward, u, True))


def _hgrn(q, gf, gb, iv, gmin, batch):
    T = q.shape[0]
    C = HG_CHUNK
    nc = T // batch // C
    nt, nh, _ = gmin.shape
    halves = gmin[:, :, 0].reshape(nt, 2, nh // 4, 2)
    safe = (jnp.min(halves, axis=-1) > -HG_SAFE_LOGDECAY).astype(jnp.int32)
    safe = safe.transpose(1, 0, 2).reshape(2, T // C)
    n = HG_CHUNKS_PER_STEP
    ns = nc // n
    fwd = pl.BlockSpec((n * C, HG_WIDTH), lambda b, j, s: (b * ns + j, 0))
    bwd = pl.BlockSpec((n * C, HG_WIDTH), lambda b, j, s: (b * ns + ns - 1 - j, 0))
    const = lambda shape: pl.BlockSpec(shape, lambda b, j, s: (0,) * len(shape),
                                       pipeline_mode=pl.Buffered(1))
    row = lax.broadcasted_iota(jnp.int32, (C, C), 0)
    col = lax.broadcasted_iota(jnp.int32, (C, C), 1)
    tril = (row >= col).astype(BF16)
    triu = (row <= col).astype(BF16)
    grid_spec = pltpu.PrefetchScalarGridSpec(
        num_scalar_prefetch=1,
        grid=(batch, ns),
        in_specs=[fwd, bwd, fwd, bwd, fwd, bwd, const((C, C)), const((C, C))],
        out_specs=(fwd, bwd),
        scratch_shapes=[pltpu.VMEM((HG_HEADS, HG_DK, HG_DK), F32),
                        pltpu.VMEM((HG_HEADS, HG_DK, HG_DK), F32),
                        pltpu.VMEM((C, HG_WIDTH), F32), pltpu.VMEM((C, HG_WIDTH), F32),
                        pltpu.VMEM((C, HG_WIDTH), F32)],
    )
    return pl.pallas_call(
        _hgrn_kernel,
        grid_spec=grid_spec,
        out_shape=(jax.ShapeDtypeStruct((T, HG_WIDTH), BF16), jax.ShapeDtypeStruct((T, HG_WIDTH), BF16)),
        compiler_params=pltpu.CompilerParams(
            dimension_semantics=("parallel", "arbitrary"), vmem_limit_bytes=VMEM_LIMIT),
        name="hgrn",
    )(safe, q, q, gf, gb, iv, iv, tril, triu)


def _mix_kernel(x_ref, lng_ref, lnb_ref, of_ref, ob_ref, so_ref, sgb_ref, ag_ref, ng_ref, wpb_ref,
                wo_ref, l1g_ref, l1b_ref, wr_ref, rb_ref, tri_ref, x1_ref, x1p_ref, route_ref,
                cnt_ref, carry_ref):
    tm = x_ref.shape[0]

    @pl.when(pl.program_id(0) == 0)
    def _():
        carry_ref[...] = jnp.zeros_like(carry_ref)

    o = of_ref[...].astype(F32) + ob_ref[...].astype(F32)
    heads = []
    for h in range(HG_HEADS):
        oh = o[:, h * HG_DK:(h + 1) * HG_DK]
        heads.append(oh * lax.rsqrt(jnp.mean(oh * oh, axis=-1, keepdims=True) + RMS_EPS))
    rn = jnp.concatenate(heads, axis=1) * ng_ref[...] * so_ref[...].astype(F32)
    r = _bdot(rn, wpb_ref[...])
    mixed = ag_ref[...].astype(F32) + sgb_ref[...].astype(F32) * r
    y = _bdot(mixed, wo_ref[...])
    xn = _layer_norm(x_ref[...], lng_ref[...], lnb_ref[...])
    x1 = _layer_norm(ALPHA * xn + y, l1g_ref[...], l1b_ref[...])
    x1_ref[...] = x1
    x1p_ref[...] = _pack_rows(x1)

    neg = jnp.float32(-jnp.inf)
    reps = tm // LANES
    scores = jax.nn.sigmoid(lax.dot_general(wr_ref[...], x1.astype(BF16), _NT,
                                            preferred_element_type=F32))
    biased = (scores + jnp.concatenate([rb_ref[...]] * reps, axis=1)).reshape(
        N_GROUPS, GROUP_SIZE, tm)
    sub = lax.broadcasted_iota(jnp.int32, biased.shape, 1).astype(F32)
    m1 = jnp.max(biased, axis=1, keepdims=True)
    first = jnp.min(jnp.where(biased == m1, sub, float(GROUP_SIZE)), axis=1, keepdims=True)
    m2 = jnp.max(jnp.where(sub == first, neg, biased), axis=1, keepdims=True)
    gs = (m1 + m2).reshape(N_GROUPS, tm)
    grp = lax.broadcasted_iota(jnp.int32, (N_GROUPS, tm), 0)
    ahead = jnp.zeros((N_GROUPS, tm), F32)
    for d in range(1, N_GROUPS):
        other = pltpu.roll(gs, d, 0)
        tie = jnp.where(grp >= d, 1.0, 0.0)
        ahead = ahead + jnp.where(other > gs, 1.0, jnp.where(other == gs, tie, 0.0))
    keep = (ahead < TOPK_GROUPS).reshape(N_GROUPS, 1, tm)
    allowed = jnp.where(keep, biased, neg).reshape(N_EXPERTS, tm)
    row = lax.broadcasted_iota(jnp.int32, (N_EXPERTS, tm), 0).astype(F32)
    sel = jnp.zeros((N_EXPERTS, tm), F32)
    picks = []
    for _ in range(TOP_K):
        m = jnp.max(allowed, axis=0, keepdims=True)
        first = jnp.min(jnp.where(allowed == m, row, float(N_EXPERTS)), axis=0, keepdims=True)
        hit = row == first
        picks.append((first, hit, jnp.sum(jnp.where(hit, scores, 0.0), axis=0, keepdims=True)))
        sel = jnp.where(hit, 1.0, sel)
        allowed = jnp.where(hit, neg, allowed)
    wsum = picks[0][2]
    for pk in picks[1:]:
        wsum = wsum + pk[2]
    selb = sel.astype(BF16)
    carry = carry_ref[...]
    before = (jnp.dot(selb, tri_ref[...], preferred_element_type=F32)
              + jnp.concatenate([carry] * reps, axis=1))
    total = carry + jnp.dot(selb, jnp.ones((tm, LANES), BF16), preferred_element_type=F32)
    carry_ref[...] = total
    cnt_ref[...] = total
    blank = [jnp.zeros((1, tm), F32)] * (8 - TOP_K)
    route_ref[...] = jnp.concatenate(
        [pk[0] for pk in picks] + blank
        + [jnp.sum(jnp.where(pk[1], before, 0.0), axis=0, keepdims=True) for pk in picks] + blank
        + [pk[2] / wsum * ROUTED_SCALE for pk in picks] + blank, axis=0)


def _mix(x, of, ob, so, sgb, ag, p):
    T = x.shape[0]
    tm = TOKEN_TILE
    tok = lambda w: pl.BlockSpec((tm, w), lambda i: (i, 0))
    row = lax.broadcasted_iota(jnp.int32, (tm, tm), 0)
    col = lax.broadcasted_iota(jnp.int32, (tm, tm), 1)
    tri = (row < col).astype(BF16)
    return pl.pallas_call(
        _mix_kernel,
        grid=(T // tm,),
        in_specs=[
            tok(D_MODEL), _const_spec((1, D_MODEL)), _const_spec((1, D_MODEL)),
            tok(HG_WIDTH), tok(HG_WIDTH), tok(HG_WIDTH), tok(D_MODEL), tok(D_MODEL),
            _const_spec((1, HG_WIDTH)),
            _const_spec((HG_WIDTH, D_MODEL)), _const_spec((D_MODEL, D_MODEL)),
            _const_spec((1, D_MODEL)), _const_spec((1, D_MODEL)),
            _const_spec((N_EXPERTS, D_MODEL)), _const_spec((N_EXPERTS, LANES)),
            _const_spec((tm, tm)),
        ],
        out_specs=(tok(D_MODEL), tok(D_MODEL // 2), pl.BlockSpec((24, tm), lambda i: (0, i)),
                   pl.BlockSpec((N_EXPERTS, LANES), lambda i: (0, 0))),
        out_shape=(jax.ShapeDtypeStruct((T, D_MODEL), F32),
                   jax.ShapeDtypeStruct((T, D_MODEL // 2), jnp.uint32),
                   jax.ShapeDtypeStruct((24, T), F32),
                   jax.ShapeDtypeStruct((N_EXPERTS, LANES), F32)),
        scratch_shapes=[pltpu.VMEM((N_EXPERTS, LANES), F32)],
        compiler_params=pltpu.CompilerParams(
            dimension_semantics=("arbitrary",), vmem_limit_bytes=VMEM_LIMIT),
        name="mix",
    )(x, p["ln_in_g"], p["ln_in_b"], of, ob, so, sgb, ag, p["hg_norm_g"], p["w_pb"], p["w_o"],
      p["ln1_g"], p["ln1_b"], p["w_router"], p["router_bias"], tri)


def _sc_mesh():
    return plsc.VectorSubcoreMesh(core_axis_name="c", subcore_axis_name="s",
                                  num_cores=SC_CORES, num_subcores=SC_SUBCORES)


def _sc_worker():
    return lax.axis_index("s") * SC_CORES + lax.axis_index("c")


def _sc_dispatch(x1p, dest, n_rows):
    T, w = x1p.shape
    W = SC_WINDOW
    per_worker = T // W // (SC_CORES * SC_SUBCORES)

    def body(x_hbm, d_hbm, o_hbm, rows_v, idx_v, sem):
        first = _sc_worker() * per_worker
        for k in range(TOP_K):
            pltpu.sync_copy(d_hbm.at[k, pl.ds(first, per_worker)], idx_v.at[k])

        @pl.loop(0, per_worker)
        def _(j):
            pltpu.sync_copy(x_hbm.at[pl.ds((first + j) * W, W)], rows_v)
            copies = [pltpu.async_copy(rows_v, o_hbm.at[idx_v.at[k, j]], sem) for k in range(TOP_K)]
            for c in copies:
                c.wait()

    return pl.kernel(
        body,
        out_type=jax.ShapeDtypeStruct((n_rows, w), jnp.uint32),
        mesh=_sc_mesh(),
        scratch_types=[pltpu.VMEM((W, w), jnp.uint32), pltpu.VMEM((TOP_K, per_worker, W), jnp.int32),
                       pltpu.SemaphoreType.DMA],
        name="sc_dispatch",
    )(x1p, dest.reshape(TOP_K, T // W, W))


def _sc_combine(ys, dest, gates):
    T = dest.shape[1]
    W = SC_COMBINE_WINDOW
    w = ys.shape[1]
    per_worker = T // W // (SC_CORES * SC_SUBCORES)
    assert per_worker % 2 == 0 and per_worker * W * SC_CORES * SC_SUBCORES == T

    def body(y_hbm, d_hbm, g_hbm, o_hbm, idx_v, *scratch):
        slots = [dict(rows=scratch[s], gate=scratch[2 + s], out=scratch[4 + s], gsem=scratch[6 + s],
                      wsem=scratch[8 + s]) for s in range(2)]
        first = _sc_worker() * per_worker
        for k in range(TOP_K):
            pltpu.sync_copy(d_hbm.at[k, pl.ds(first * W, per_worker * W)], idx_v.at[k])

        def loads(b, win):
            i = win - first
            return [pltpu.make_async_copy(g_hbm.at[win], b["gate"], b["gsem"])] + [
                pltpu.make_async_copy(y_hbm.at[idx_v.at[k, pl.ds(i * W, W)]],
                                      b["rows"].at[k], b["gsem"]) for k in range(TOP_K)]

        def write_back(b, win):
            return pltpu.make_async_copy(b["out"], o_hbm.at[pl.ds(win * W, W)], b["wsem"])

        def fetch(b, win):
            for c in loads(b, win):
                c.start()

        def reduce_rows(b):
            rows, gate_v, out = b["rows"], b["gate"], b["out"]

            @pl.loop(0, W)
            def _(j):
                gate = [gate_v[pl.ds((j * TOP_K + k) * SC_LANES, SC_LANES)] for k in range(TOP_K)]

                @plsc.parallel_loop(0, w, step=SC_LANES, unroll=4)
                def _(col):
                    lo, hi = [], []
                    for k in range(TOP_K):
                        p = rows[k, j, pl.ds(col, SC_LANES)]
                        lo.append(plsc.bitcast(p << 16, F32) * gate[k])
                        hi.append(plsc.bitcast(p & jnp.uint32(0xFFFF0000), F32) * gate[k])
                    out[j, pl.ds(col, SC_LANES)] = (lo[0] + lo[1]) + (lo[2] + lo[3]) + (lo[4] + lo[5])
                    out[j, pl.ds(w + col, SC_LANES)] = (hi[0] + hi[1]) + (hi[2] + hi[3]) + (hi[4] + hi[5])

        def process(b, win, not_first):
            for c in loads(b, win):
                c.wait()

            @pl.when(not_first)
            def _():
                write_back(b, win).wait()

            reduce_rows(b)
            write_back(b, win).start()

        fetch(slots[0], first)

        @pl.loop(0, per_worker, step=2)
        def _(i):
            win = first + i
            fetch(slots[1], win + 1)
            process(slots[0], win, i > 0)

            @pl.when(i + 2 < per_worker)
            def _():
                fetch(slots[0], win + 2)

            process(slots[1], win + 1, i > 0)

        for b in slots:
            write_back(b, first).wait()

    slot_types = ([pltpu.VMEM((TOP_K, per_worker * W), jnp.int32)]
                  + [pltpu.VMEM((TOP_K, W, w), jnp.uint32)] * 2
                  + [pltpu.VMEM((W * TOP_K * SC_LANES,), F32)] * 2
                  + [pltpu.VMEM((W, 2 * w), F32)] * 2
                  + [pltpu.SemaphoreType.DMA] * 4)
    return pl.kernel(
        body,
        out_type=jax.ShapeDtypeStruct((T, 2 * w), F32),
        mesh=_sc_mesh(),
        scratch_types=slot_types,
        compiler_params=pltpu.CompilerParams(needs_layout_passes=False),
        name="sc_combine",
    )(ys, dest, gates)


def _experts_kernel(be_ref, nv_ref, xs_ref, wg_ref, wu_ref, wd_ref, ys_ref):
    del be_ref
    n_valid = nv_ref[pl.program_id(0)]
    mb, w = xs_ref.shape

    @pl.when(n_valid > 0)
    def _():
        keep = lax.broadcasted_iota(jnp.int32, (mb, w), 0) < n_valid
        lo, hi = _unpack_rows(jnp.where(keep, xs_ref[...], jnp.uint32(0)))
        lo, hi = lo.astype(BF16), hi.astype(BF16)

        def proj(w_ref):
            return (jnp.dot(lo, w_ref[0, :w, :], preferred_element_type=F32)
                    + jnp.dot(hi, w_ref[0, w:, :], preferred_element_type=F32))

        hb = jax.nn.silu(proj(wg_ref)) * proj(wu_ref)
        ys_ref[...] = _pack_rows(jnp.dot(hb.astype(BF16), wd_ref[0], preferred_element_type=F32))

    @pl.when(n_valid <= 0)
    def _():
        ys_ref[...] = jnp.zeros_like(ys_ref)


def _experts(xs, blk_exp, n_valid, p):
    n_rows, w = xs.shape
    mb = n_rows // blk_exp.shape[0]
    grid_spec = pltpu.PrefetchScalarGridSpec(
        num_scalar_prefetch=2,
        grid=(n_rows // mb,),
        in_specs=[
            pl.BlockSpec((mb, w), lambda i, be, nv: (i, 0)),
            pl.BlockSpec((1, D_MODEL, EXPERT_DIM), lambda i, be, nv: (be[i], 0, 0)),
            pl.BlockSpec((1, D_MODEL, EXPERT_DIM), lambda i, be, nv: (be[i], 0, 0)),
            pl.BlockSpec((1, EXPERT_DIM, D_MODEL), lambda i, be, nv: (be[i], 0, 0)),
        ],
        out_specs=pl.BlockSpec((mb, w), lambda i, be, nv: (i, 0)),
    )
    return pl.pallas_call(
        _experts_kernel,
        grid_spec=grid_spec,
        out_shape=jax.ShapeDtypeStruct((n_rows, w), jnp.uint32),
        compiler_params=pltpu.CompilerParams(
            dimension_semantics=("arbitrary",), vmem_limit_bytes=VMEM_LIMIT),
        name="experts",
    )(blk_exp, n_valid, xs, p["w_e_gate"], p["w_e_up"], p["w_e_down"])


def _final_kernel(x1_ref, routed_ref, wsg_ref, wsu_ref, wsd_ref, l2g_ref, l2b_ref, out_ref):
    x1 = x1_ref[...]
    xb = x1.astype(BF16)
    hs = (jax.nn.silu(jnp.dot(xb, wsg_ref[...], preferred_element_type=F32))
          * jnp.dot(xb, wsu_ref[...], preferred_element_type=F32))
    shared = jnp.dot(hs.astype(BF16), wsd_ref[...], preferred_element_type=F32)
    out_ref[...] = _layer_norm(ALPHA * x1 + (routed_ref[...] + shared), l2g_ref[...], l2b_ref[...])


def _final(x1, routed, p):
    T = x1.shape[0]
    tm = TOKEN_TILE
    tok = lambda w: pl.BlockSpec((tm, w), lambda i: (i, 0))
    return pl.pallas_call(
        _final_kernel,
        grid=(T // tm,),
        in_specs=[
            tok(D_MODEL), tok(D_MODEL),
            _const_spec((D_MODEL, SHARED_DIM)), _const_spec((D_MODEL, SHARED_DIM)),
            _const_spec((SHARED_DIM, D_MODEL)),
            _const_spec((1, D_MODEL)), _const_spec((1, D_MODEL)),
        ],
        out_specs=tok(D_MODEL),
        out_shape=jax.ShapeDtypeStruct((T, D_MODEL), F32),
        compiler_params=pltpu.CompilerParams(
            dimension_semantics=("parallel",), vmem_limit_bytes=VMEM_LIMIT),
        name="final",
    )(x1, routed, p["w_sh_gate"], p["w_sh_up"], p["w_sh_down"], p["ln2_g"], p["ln2_b"])


def _routing_layout(route, counts, n_tokens):
    mb = EXPERT_BLOCK
    n_blocks = -(-n_tokens * TOP_K // mb) + N_EXPERTS
    idx = route[0:TOP_K].astype(jnp.int32)
    rank = route[8:8 + TOP_K].astype(jnp.int32)
    counts = counts.astype(jnp.int32)
    padded = (counts + mb - 1) // mb * mb
    pad_end = jnp.cumsum(padded)
    pad_start = pad_end - padded
    experts = jnp.arange(N_EXPERTS, dtype=jnp.int32)
    dest = rank + jnp.sum(jnp.where(idx[:, :, None] == experts, pad_start, 0), axis=-1)
    blk_start = jnp.arange(n_blocks, dtype=jnp.int32) * mb
    blk_exp = jnp.minimum(
        jnp.sum((pad_end[None, :] <= blk_start[:, None]).astype(jnp.int32), axis=1), N_EXPERTS - 1)
    valid_end = jnp.sum(jnp.where(blk_exp[:, None] == experts, pad_start + counts, 0), axis=-1)
    n_valid = jnp.clip(valid_end - blk_start, 0, mb).astype(jnp.int32)
    gates = jnp.broadcast_to(
        route[16:16 + TOP_K].T.reshape(n_tokens // SC_COMBINE_WINDOW, SC_COMBINE_WINDOW, TOP_K, 1),
        (n_tokens // SC_COMBINE_WINDOW, SC_COMBINE_WINDOW, TOP_K, SC_LANES)).reshape(
            n_tokens // SC_COMBINE_WINDOW, -1)
    return dest, gates, blk_exp, n_valid, n_blocks * mb


def _encode(x, p):
    batch, seq, _ = x.shape
    T = batch * seq
    xt = x.reshape(T, D_MODEL)
    ag, q, gf, gb, iv, so, sgb, gmin = _inproj(xt, p)
    of, ob = _hgrn(q, gf, gb, iv, gmin, batch)
    x1, x1p, route, cnt = _mix(xt, of, ob, so, sgb, ag, p)
    dest, gates, blk_exp, n_valid, n_rows = _routing_layout(route, cnt[:, 0], T)
    xs = _sc_dispatch(x1p, dest, n_rows)
    ys = _experts(xs, blk_exp, n_valid, p)
    out = _final(x1, _sc_combine(ys, dest, gates), p)
    return out.reshape(batch, seq, D_MODEL)


def _prepare_params(ln_in_g, ln_in_b, w_in, a_ln_g, a_ln_b, a_ws, a_sb, hg_lb_logits, hg_norm_g,
                    w_pa, w_pb, w_o, ln1_g, ln1_b, w_router, router_bias, w_e_gate, w_e_up,
                    w_e_down, w_sh_gate, w_sh_up, w_sh_down, ln2_g, ln2_b):
    l = 0
    row = lambda v: v.reshape(1, -1).astype(F32)
    ws = a_ws[l].astype(BF16)
    wsp = jnp.concatenate([ws[0::2], ws[1::2]], axis=2)
    sbf = jnp.repeat(a_sb[l].astype(F32), A_WIDTH // A_GROUPS, axis=1)
    lb = jnp.cumsum(jax.nn.softmax(hg_lb_logits.astype(F32), axis=1), axis=1)[:, l]
    return dict(
        ln_in_g=row(ln_in_g), ln_in_b=row(ln_in_b), w_in=w_in[l].astype(BF16),
        a_ln_g=row(a_ln_g[l]), a_ln_b=row(a_ln_b[l]), wsp=wsp, sbf=sbf, lb=lb,
        w_pa=w_pa[l].astype(BF16), hg_norm_g=row(hg_norm_g[l]),
        w_pb=w_pb[l].astype(BF16), w_o=w_o[l].astype(BF16),
        ln1_g=row(ln1_g[l]), ln1_b=row(ln1_b[l]),
        w_router=w_router[l].T.astype(BF16),
        router_bias=jnp.broadcast_to(router_bias[l].astype(F32)[:, None], (N_EXPERTS, LANES)),
        w_e_gate=w_e_gate[l].astype(BF16), w_e_up=w_e_up[l].astype(BF16),
        w_e_down=w_e_down[l].astype(BF16),
        w_sh_gate=w_sh_gate[l].astype(BF16), w_sh_up=w_sh_up[l].astype(BF16),
        w_sh_down=w_sh_down[l].astype(BF16),
        ln2_g=row(ln2_g[l]), ln2_b=row(ln2_b[l]),
    )


def kernel(x_prompt, x_sample, ln_in_g, ln_in_b, w_in, a_ln_g, a_ln_b, a_ws, a_sb, hg_lb_logits,
           hg_norm_g, w_pa, w_pb, w_o, ln1_g, ln1_b, w_router, router_bias, w_e_gate, w_e_up,
           w_e_down, w_sh_gate, w_sh_up, w_sh_down, ln2_g, ln2_b):
    p = _prepare_params(ln_in_g, ln_in_b, w_in, a_ln_g, a_ln_b, a_ws, a_sb, hg_lb_logits, hg_norm_g,
                        w_pa, w_pb, w_o, ln1_g, ln1_b, w_router, router_bias, w_e_gate, w_e_up,
                        w_e_down, w_sh_gate, w_sh_up, w_sh_down, ln2_g, ln2_b)
    return _encode(x_prompt, p), _encode(x_sample, p)
```

```python
import functools

import jax
import jax.numpy as jnp
from jax import lax
from jax.experimental import pallas as pl
from jax.experimental.pallas import tpu as pltpu
from jax.experimental.pallas import tpu_sc as plsc

F32 = jnp.float32
BF16 = jnp.bfloat16

D_MODEL = 1024
A_GROUPS = 8
A_WIDTH = 512
A_CHUNK = 128
HG_HEADS = 8
HG_DK = 128
HG_WIDTH = HG_HEADS * HG_DK
N_IN = 2 * A_WIDTH + 5 * HG_WIDTH + 2 * D_MODEL
N_EXPERTS = 64
TOP_K = 6
N_GROUPS = 8
TOPK_GROUPS = 4
GROUP_SIZE = N_EXPERTS // N_GROUPS
EXPERT_DIM = 256
SHARED_DIM = 256
ROUTED_SCALE = 2.5
DEPTH = 1
ALPHA = (2.0 * DEPTH) ** 0.25
LN_EPS = 1e-5
RMS_EPS = 1e-6

LANES = 128
TOKEN_TILE = 512
INPROJ_TILE = 256
HG_CHUNK = 128
HG_CHUNKS_PER_STEP = 2
HG_SAFE_LOGDECAY = 80.0
EXPERT_BLOCK = 1024
SC_CORES = 2
SC_SUBCORES = 16
SC_LANES = 16
SC_WINDOW = 32
SC_COMBINE_WINDOW = 8
VMEM_LIMIT = 56 * 1024 * 1024

_O_U, _O_V, _O_Q, _O_FF, _O_FB, _O_I, _O_G, _O_GA, _O_GB = (
    0, 512, 1024, 2048, 3072, 4096, 5120, 6144, 7168)


def _layer_norm(x, g, b):
    mu = jnp.mean(x, axis=-1, keepdims=True)
    xc = x - mu
    var = jnp.mean(xc * xc, axis=-1, keepdims=True)
    return xc * lax.rsqrt(var + LN_EPS) * g + b


def _bdot(a, b):
    return jnp.dot(a.astype(BF16), b.astype(BF16), preferred_element_type=F32)


def _pack_rows(x):
    w = x.shape[1] // 2
    lo = lax.bitcast_convert_type(x[:, :w].astype(BF16).astype(F32), jnp.uint32)
    hi = lax.bitcast_convert_type(x[:, w:].astype(BF16).astype(F32), jnp.uint32)
    return hi | (lo >> 16)


def _unpack_rows(p):
    lo = lax.bitcast_convert_type(p << 16, F32)
    hi = lax.bitcast_convert_type(p & jnp.uint32(0xFFFF0000), F32)
    return lo, hi


def _const_spec(shape):
    nd = len(shape)
    return pl.BlockSpec(shape, lambda *_: (0,) * nd, pipeline_mode=pl.Buffered(1))


def _inproj_kernel(x_ref, lng_ref, lnb_ref, win_ref, alng_ref, alnb_ref, wsp_ref, sbf_ref, lb_ref,
                   wpa_ref, ag_ref, q_ref, gf_ref, gb_ref, iv_ref, so_ref, sgb_ref, gmin_ref):
    tm = x_ref.shape[0]
    xb = _layer_norm(x_ref[...], lng_ref[...], lnb_ref[...]).astype(BF16)

    def sec(lo, width):
        return jnp.dot(xb, win_ref[:, lo:lo + width], preferred_element_type=F32)

    v = _layer_norm(jax.nn.gelu(sec(_O_V, A_WIDTH)), alng_ref[...], alnb_ref[...]).astype(BF16)
    u = jax.nn.gelu(sec(_O_U, A_WIDTH))

    q_ref[...] = jax.nn.silu(sec(_O_Q, HG_WIDTH)).astype(BF16)
    mins = []
    for d, (off, g_ref) in enumerate(((_O_FF, gf_ref), (_O_FB, gb_ref))):
        lb = lb_ref[d:d + 1, :]
        f = lb + (1.0 - lb) * jax.nn.sigmoid(sec(off, HG_WIDTH))
        g = jnp.log(f)
        g_ref[...] = g
        half = jnp.sum(g.reshape(tm // (HG_CHUNK // 2), HG_CHUNK // 2, HG_WIDTH), axis=1)
        mins.append(jnp.min(half, axis=-1, keepdims=True))
    gmin_ref[0] = jnp.broadcast_to(jnp.concatenate(mins, axis=0), gmin_ref.shape[1:])
    so_ref[...] = jax.nn.silu(sec(_O_G, HG_WIDTH)).astype(BF16)
    sgb_ref[...] = jax.nn.sigmoid(sec(_O_GB, D_MODEL)).astype(BF16)

    lane = lax.broadcasted_iota(jnp.int32, (A_CHUNK, LANES), 1)
    left = lane < (A_WIDTH // A_GROUPS)
    zero = jnp.zeros((A_CHUNK, LANES), BF16)
    chunks = []
    for c in range(tm // A_CHUNK):
        vc = v[c * A_CHUNK:(c + 1) * A_CHUNK]
        cols = []
        for p in range(A_GROUPS // 2):
            vp = vc[:, p * LANES:(p + 1) * LANES]
            rhs = jnp.concatenate([jnp.where(left, vp, zero), jnp.where(left, zero, vp)], axis=0)
            cols.append(jnp.dot(wsp_ref[p], rhs, preferred_element_type=F32))
        chunks.append(jnp.concatenate(cols, axis=1) + sbf_ref[...])
    mixed = jnp.concatenate(chunks, axis=0)
    a = _bdot(u * mixed, wpa_ref[...])
    ag_ref[...] = (jax.nn.sigmoid(sec(_O_GA, D_MODEL)) * a).astype(BF16)
    iv_ref[...] = sec(_O_I, HG_WIDTH).astype(BF16)


def _inproj(x, p):
    T = x.shape[0]
    tm = INPROJ_TILE
    nt = T // tm
    nh = 2 * tm // (HG_CHUNK // 2)
    tok = lambda w: pl.BlockSpec((tm, w), lambda i: (i, 0))
    outs = (
        jax.ShapeDtypeStruct((T, D_MODEL), BF16),
        jax.ShapeDtypeStruct((T, HG_WIDTH), BF16),
        jax.ShapeDtypeStruct((T, HG_WIDTH), F32),
        jax.ShapeDtypeStruct((T, HG_WIDTH), F32),
        jax.ShapeDtypeStruct((T, HG_WIDTH), BF16),
        jax.ShapeDtypeStruct((T, HG_WIDTH), BF16),
        jax.ShapeDtypeStruct((T, D_MODEL), BF16),
        jax.ShapeDtypeStruct((nt, nh, LANES), F32),
    )
    return pl.pallas_call(
        _inproj_kernel,
        grid=(nt,),
        in_specs=[
            tok(D_MODEL),
            _const_spec((1, D_MODEL)), _const_spec((1, D_MODEL)),
            _const_spec((D_MODEL, N_IN)),
            _const_spec((1, A_WIDTH)), _const_spec((1, A_WIDTH)),
            _const_spec((A_GROUPS // 2, A_CHUNK, 2 * A_CHUNK)),
            _const_spec((A_CHUNK, A_WIDTH)),
            _const_spec((2, HG_WIDTH)),
            _const_spec((A_WIDTH, D_MODEL)),
        ],
        out_specs=(tok(D_MODEL), tok(HG_WIDTH), tok(HG_WIDTH), tok(HG_WIDTH), tok(HG_WIDTH),
                   tok(HG_WIDTH), tok(D_MODEL), pl.BlockSpec((1, nh, LANES), lambda i: (i, 0, 0))),
        out_shape=outs,
        compiler_params=pltpu.CompilerParams(
            dimension_semantics=("parallel",), vmem_limit_bytes=VMEM_LIMIT),
        name="inproj",
    )(x, p["ln_in_g"], p["ln_in_b"], p["w_in"], p["a_ln_g"], p["a_ln_b"], p["wsp"], p["sbf"],
      p["lb"], p["w_pa"])


_NT = (((1,), (1,)), ((), ()))
_TN = (((0,), (0,)), ((), ()))


def _hgrn_direction(q_ref, g_ref, v_ref, st_ref, o_ref, tri, mask, fwd):
    C = q_ref.shape[0]
    g = g_ref[...]
    ghi = g.astype(BF16)
    glo = (g - ghi.astype(F32)).astype(BF16)
    b = jnp.dot(jnp.concatenate([tri, tri], axis=1), jnp.concatenate([ghi, glo], axis=0),
                preferred_element_type=F32)
    mid = C // 2 - 1 if fwd else C // 2
    end = C - 1 if fwd else 0
    r = b[mid:mid + 1, :]
    b_end = b[end:end + 1, :]
    qt = q_ref[...].astype(F32) * jnp.exp(b - r)
    kt = (1.0 - jnp.exp(g)) * jnp.exp(r - b)
    qtb = qt.astype(BF16)
    ktb = kt.astype(BF16)
    qhb = (qt * jnp.exp(r)).astype(BF16)
    khb = (kt * jnp.exp(b_end - r)).astype(BF16)
    decay = jnp.exp(b_end)
    v32 = v_ref[...].astype(F32)
    heads = [slice(h * HG_DK, (h + 1) * HG_DK) for h in range(HG_HEADS)]
    zero = jnp.zeros((C, HG_DK), BF16)
    mask2 = jnp.concatenate([mask, mask], axis=1)
    scores = []
    for p in range(HG_HEADS // 2):
        k1, k2 = ktb[:, heads[2 * p]], ktb[:, heads[2 * p + 1]]
        kk = jnp.concatenate([jnp.concatenate([k1, zero], axis=1),
                              jnp.concatenate([zero, k2], axis=1)], axis=0)
        s2 = lax.dot_general(qtb[:, 2 * p * HG_DK:(2 * p + 2) * HG_DK], kk, _NT,
                             preferred_element_type=F32)
        s2 = jnp.where(mask2, s2, 0.0).astype(BF16)
        scores += [s2[:, :C], s2[:, C:]]
    for h, sl in enumerate(heads):
        st = st_ref[h]
        vt = v32[:, sl].T.astype(BF16)
        o_ref[:, sl] = lax.dot_general(
            jnp.concatenate([scores[h], qhb[:, sl]], axis=1),
            jnp.concatenate([vt, st.astype(BF16)], axis=1), _NT,
            preferred_element_type=F32).astype(o_ref.dtype)
        st_ref[h] = st * decay[:, sl] + jnp.dot(vt, khb[:, sl], preferred_element_type=F32)


def _hgrn_direction_stepwise(q_ref, g_ref, v_ref, st_ref, o_ref, q32_ref, v32_ref, o32_ref, fwd):
    C = q_ref.shape[0]
    q32_ref[...] = q_ref[...].astype(F32)
    v32_ref[...] = v_ref[...].astype(F32)
    sub = 8
    rows = lax.broadcasted_iota(jnp.int32, (sub, HG_DK), 0)

    def group(i, carry):
        base = pl.multiple_of((i if fwd else C // sub - 1 - i) * sub, sub)
        f = jnp.exp(g_ref[pl.ds(base, sub), :])
        k = 1.0 - f
        q = q32_ref[pl.ds(base, sub), :]
        v = v32_ref[pl.ds(base, sub), :]
        for h in range(HG_HEADS):
            sl = slice(h * HG_DK, (h + 1) * HG_DK)
            st = st_ref[h]
            out = jnp.zeros((sub, HG_DK), F32)
            for r in (range(sub) if fwd else range(sub - 1, -1, -1)):
                v_t = jnp.where(rows == 0, v[r:r + 1, sl], 0.0).astype(BF16)
                k_t = jnp.broadcast_to(k[r:r + 1, sl], (sub, HG_DK)).astype(BF16)
                st = st * f[r:r + 1, sl] + lax.dot_general(v_t, k_t, _TN, preferred_element_type=F32)
                q_t = jnp.broadcast_to(q[r:r + 1, sl], (sub, HG_DK)).astype(BF16)
                o_t = lax.dot_general(q_t, st.astype(BF16), _NT, preferred_element_type=F32)
                out = jnp.where(rows == r, o_t, out)
            st_ref[h] = st
            o32_ref[pl.ds(base, sub), sl] = out
        return carry

    lax.fori_loop(0, C // sub, group, 0)
    o_ref[...] = o32_ref[...].astype(o_ref.dtype)


def _hgrn_kernel(safe_ref, qf_ref, qb_ref, gf_ref, gb_ref, vf_ref, vb_ref, tril_ref, triu_ref,
                 of_ref, ob_ref, sf_ref, sb_ref, q32_ref, v32_ref, o32_ref):
    b, j = pl.program_id(0), pl.program_id(1)
    ns = pl.num_programs(1)
    n = HG_CHUNKS_PER_STEP
    C = qf_ref.shape[0] // n

    @pl.when(j == 0)
    def _():
        sf_ref[...] = jnp.zeros_like(sf_ref)
        sb_ref[...] = jnp.zeros_like(sb_ref)

    row = lax.broadcasted_iota(jnp.int32, (C, C), 0)
    col = lax.broadcasted_iota(jnp.int32, (C, C), 1)
    safe_f = [safe_ref[0, (b * ns + j) * n + u] != 0 for u in range(n)]
    safe_b = [safe_ref[1, (b * ns + ns - 1 - j) * n + u] != 0 for u in range(n)]
    part = lambda ref, u: ref.at[pl.ds(u * C, C), :]

    def forward(u, stepwise):
        refs = (part(qf_ref, u), part(gf_ref, u), part(vf_ref, u), sf_ref, part(of_ref, u))
        if stepwise:
            _hgrn_direction_stepwise(*refs, q32_ref, v32_ref, o32_ref, True)
        else:
            _hgrn_direction(*refs, tril_ref[...], row >= col, True)

    def backward(u, stepwise):
        refs = (part(qb_ref, u), part(gb_ref, u), part(vb_ref, u), sb_ref, part(ob_ref, u))
        if stepwise:
            _hgrn_direction_stepwise(*refs, q32_ref, v32_ref, o32_ref, False)
        else:
            _hgrn_direction(*refs, triu_ref[...], row <= col, False)

    all_safe = functools.reduce(jnp.logical_and, safe_f + safe_b)

    @pl.when(all_safe)
    def _():
        for u in range(n):
            forward(u, False)
            backward(n - 1 - u, False)

    @pl.when(jnp.logical_not(all_safe))
    def _():
        for u in range(n):
            pl.when(safe_f[u])(functools.partial(forward, u, False))
            pl.when(jnp.logical_not(safe_f[u]))(functools.partial(forward, u, True))
        for u in reversed(range(n)):
            pl.when(safe_b[u])(functools.partial(backward, u, False))
            pl.when(jnp.logical_not(safe_b[u]))(functools.partial(backward, u, True))


def _hgrn(q, gf, gb, iv, gmin, batch):
    T = q.shape[0]
    C = HG_CHUNK
    nc = T // batch // C
    nt, nh, _ = gmin.shape
    halves = gmin[:, :, 0].reshape(nt, 2, nh // 4, 2)
    safe = (jnp.min(halves, axis=-1) > -HG_SAFE_LOGDECAY).astype(jnp.int32)
    safe = safe.transpose(1, 0, 2).reshape(2, T // C)
    n = HG_CHUNKS_PER_STEP
    ns = nc // n
    fwd = pl.BlockSpec((n * C, HG_WIDTH), lambda b, j, s: (b * ns + j, 0))
    bwd = pl.BlockSpec((n * C, HG_WIDTH), lambda b, j, s: (b * ns + ns - 1 - j, 0))
    const = lambda shape: pl.BlockSpec(shape, lambda b, j, s: (0,) * len(shape),
                                       pipeline_mode=pl.Buffered(1))
    row = lax.broadcasted_iota(jnp.int32, (C, C), 0)
    col = lax.broadcasted_iota(jnp.int32, (C, C), 1)
    tril = (row >= col).astype(BF16)
    triu = (row <= col).astype(BF16)
    grid_spec = pltpu.PrefetchScalarGridSpec(
        num_scalar_prefetch=1,
        grid=(batch, ns),
        in_specs=[fwd, bwd, fwd, bwd, fwd, bwd, const((C, C)), const((C, C))],
        out_specs=(fwd, bwd),
        scratch_shapes=[pltpu.VMEM((HG_HEADS, HG_DK, HG_DK), F32),
                        pltpu.VMEM((HG_HEADS, HG_DK, HG_DK), F32),
                        pltpu.VMEM((C, HG_WIDTH), F32), pltpu.VMEM((C, HG_WIDTH), F32),
                        pltpu.VMEM((C, HG_WIDTH), F32)],
    )
    return pl.pallas_call(
        _hgrn_kernel,
        grid_spec=grid_spec,
        out_shape=(jax.ShapeDtypeStruct((T, HG_WIDTH), BF16), jax.ShapeDtypeStruct((T, HG_WIDTH), BF16)),
        compiler_params=pltpu.CompilerParams(
            dimension_semantics=("parallel", "arbitrary"), vmem_limit_bytes=VMEM_LIMIT),
        name="hgrn",
    )(safe, q, q, gf, gb, iv, iv, tril, triu)


def _mix_kernel(x_ref, lng_ref, lnb_ref, of_ref, ob_ref, so_ref, sgb_ref, ag_ref, ng_ref, wpb_ref,
                wo_ref, l1g_ref, l1b_ref, wr_ref, rb_ref, tri_ref, x1_ref, x1p_ref, route_ref,
                cnt_ref, carry_ref):
    tm = x_ref.shape[0]

    @pl.when(pl.program_id(0) == 0)
    def _():
        carry_ref[...] = jnp.zeros_like(carry_ref)

    o = of_ref[...].astype(F32) + ob_ref[...].astype(F32)
    heads = []
    for h in range(HG_HEADS):
        oh = o[:, h * HG_DK:(h + 1) * HG_DK]
        heads.append(oh * lax.rsqrt(jnp.mean(oh * oh, axis=-1, keepdims=True) + RMS_EPS))
    rn = jnp.concatenate(heads, axis=1) * ng_ref[...] * so_ref[...].astype(F32)
    r = _bdot(rn, wpb_ref[...])
    mixed = ag_ref[...].astype(F32) + sgb_ref[...].astype(F32) * r
    y = _bdot(mixed, wo_ref[...])
    xn = _layer_norm(x_ref[...], lng_ref[...], lnb_ref[...])
    x1 = _layer_norm(ALPHA * xn + y, l1g_ref[...], l1b_ref[...])
    x1_ref[...] = x1
    x1p_ref[...] = _pack_rows(x1)

    neg = jnp.float32(-jnp.inf)
    reps = tm // LANES
    scores = jax.nn.sigmoid(lax.dot_general(wr_ref[...], x1.astype(BF16), _NT,
                                            preferred_element_type=F32))
    biased = (scores + jnp.concatenate([rb_ref[...]] * reps, axis=1)).reshape(
        N_GROUPS, GROUP_SIZE, tm)
    sub = lax.broadcasted_iota(jnp.int32, biased.shape, 1).astype(F32)
    m1 = jnp.max(biased, axis=1, keepdims=True)
    first = jnp.min(jnp.where(biased == m1, sub, float(GROUP_SIZE)), axis=1, keepdims=True)
    m2 = jnp.max(jnp.where(sub == first, neg, biased), axis=1, keepdims=True)
    gs = (m1 + m2).reshape(N_GROUPS, tm)
    grp = lax.broadcasted_iota(jnp.int32, (N_GROUPS, tm), 0)
    ahead = jnp.zeros((N_GROUPS, tm), F32)
    for d in range(1, N_GROUPS):
        other = pltpu.roll(gs, d, 0)
        tie = jnp.where(grp >= d, 1.0, 0.0)
        ahead = ahead + jnp.where(other > gs, 1.0, jnp.where(other == gs, tie, 0.0))
    keep = (ahead < TOPK_GROUPS).reshape(N_GROUPS, 1, tm)
    allowed = jnp.where(keep, biased, neg).reshape(N_EXPERTS, tm)
    row = lax.broadcasted_iota(jnp.int32, (N_EXPERTS, tm), 0).astype(F32)
    sel = jnp.zeros((N_EXPERTS, tm), F32)
    picks = []
    for _ in range(TOP_K):
        m = jnp.max(allowed, axis=0, keepdims=True)
        first = jnp.min(jnp.where(allowed == m, row, float(N_EXPERTS)), axis=0, keepdims=True)
        hit = row == first
        picks.append((first, hit, jnp.sum(jnp.where(hit, scores, 0.0), axis=0, keepdims=True)))
        sel = jnp.where(hit, 1.0, sel)
        allowed = jnp.where(hit, neg, allowed)
    wsum = picks[0][2]
    for pk in picks[1:]:
        wsum = wsum + pk[2]
    selb = sel.astype(BF16)
    carry = carry_ref[...]
    before = (jnp.dot(selb, tri_ref[...], preferred_element_type=F32)
              + jnp.concatenate([carry] * reps, axis=1))
    total = carry + jnp.dot(selb, jnp.ones((tm, LANES), BF16), preferred_element_type=F32)
    carry_ref[...] = total
    cnt_ref[...] = total
    blank = [jnp.zeros((1, tm), F32)] * (8 - TOP_K)
    route_ref[...] = jnp.concatenate(
        [pk[0] for pk in picks] + blank
        + [jnp.sum(jnp.where(pk[1], before, 0.0), axis=0, keepdims=True) for pk in picks] + blank
        + [pk[2] / wsum * ROUTED_SCALE for pk in picks] + blank, axis=0)


def _mix(x, of, ob, so, sgb, ag, p):
    T = x.shape[0]
    tm = TOKEN_TILE
    tok = lambda w: pl.BlockSpec((tm, w), lambda i: (i, 0))
    row = lax.broadcasted_iota(jnp.int32, (tm, tm), 0)
    col = lax.broadcasted_iota(jnp.int32, (tm, tm), 1)
    tri = (row < col).astype(BF16)
    return pl.pallas_call(
        _mix_kernel,
        grid=(T // tm,),
        in_specs=[
            tok(D_MODEL), _const_spec((1, D_MODEL)), _const_spec((1, D_MODEL)),
            tok(HG_WIDTH), tok(HG_WIDTH), tok(HG_WIDTH), tok(D_MODEL), tok(D_MODEL),
            _const_spec((1, HG_WIDTH)),
            _const_spec((HG_WIDTH, D_MODEL)), _const_spec((D_MODEL, D_MODEL)),
            _const_spec((1, D_MODEL)), _const_spec((1, D_MODEL)),
            _const_spec((N_EXPERTS, D_MODEL)), _const_spec((N_EXPERTS, LANES)),
            _const_spec((tm, tm)),
        ],
        out_specs=(tok(D_MODEL), tok(D_MODEL // 2), pl.BlockSpec((24, tm), lambda i: (0, i)),
                   pl.BlockSpec((N_EXPERTS, LANES), lambda i: (0, 0))),
        out_shape=(jax.ShapeDtypeStruct((T, D_MODEL), F32),
                   jax.ShapeDtypeStruct((T, D_MODEL // 2), jnp.uint32),
                   jax.ShapeDtypeStruct((24, T), F32),
                   jax.ShapeDtypeStruct((N_EXPERTS, LANES), F32)),
        scratch_shapes=[pltpu.VMEM((N_EXPERTS, LANES), F32)],
        compiler_params=pltpu.CompilerParams(
            dimension_semantics=("arbitrary",), vmem_limit_bytes=VMEM_LIMIT),
        name="mix",
    )(x, p["ln_in_g"], p["ln_in_b"], of, ob, so, sgb, ag, p["hg_norm_g"], p["w_pb"], p["w_o"],
      p["ln1_g"], p["ln1_b"], p["w_router"], p["router_bias"], tri)


def _sc_mesh():
    return plsc.VectorSubcoreMesh(core_axis_name="c", subcore_axis_name="s",
                                  num_cores=SC_CORES, num_subcores=SC_SUBCORES)


def _sc_worker():
    return lax.axis_index("s") * SC_CORES + lax.axis_index("c")


def _sc_dispatch(x1p, dest, n_rows):
    T, w = x1p.shape
    W = SC_WINDOW
    per_worker = T // W // (SC_CORES * SC_SUBCORES)

    def body(x_hbm, d_hbm, o_hbm, rows_v, idx_v, sem):
        first = _sc_worker() * per_worker
        for k in range(TOP_K):
            pltpu.sync_copy(d_hbm.at[k, pl.ds(first, per_worker)], idx_v.at[k])

        @pl.loop(0, per_worker)
        def _(j):
            pltpu.sync_copy(x_hbm.at[pl.ds((first + j) * W, W)], rows_v)
            copies = [pltpu.async_copy(rows_v, o_hbm.at[idx_v.at[k, j]], sem) for k in range(TOP_K)]
            for c in copies:
                c.wait()

    return pl.kernel(
        body,
        out_type=jax.ShapeDtypeStruct((n_rows, w), jnp.uint32),
        mesh=_sc_mesh(),
        scratch_types=[pltpu.VMEM((W, w), jnp.uint32), pltpu.VMEM((TOP_K, per_worker, W), jnp.int32),
                       pltpu.SemaphoreType.DMA],
        name="sc_dispatch",
    )(x1p, dest.reshape(TOP_K, T // W, W))


def _sc_combine(ys, dest, gates):
    T = dest.shape[1]
    W = SC_COMBINE_WINDOW
    w = ys.shape[1]
    per_worker = T // W // (SC_CORES * SC_SUBCORES)
    assert per_worker % 2 == 0 and per_worker * W * SC_CORES * SC_SUBCORES == T

    def body(y_hbm, d_hbm, g_hbm, o_hbm, idx_v, *scratch):
        slots = [dict(rows=scratch[s], gate=scratch[2 + s], out=scratch[4 + s], gsem=scratch[6 + s],
                      wsem=scratch[8 + s]) for s in range(2)]
        first = _sc_worker() * per_worker
        for k in range(TOP_K):
            pltpu.sync_copy(d_hbm.at[k, pl.ds(first * W, per_worker * W)], idx_v.at[k])

        def loads(b, win):
            i = win - first
            return [pltpu.make_async_copy(g_hbm.at[win], b["gate"], b["gsem"])] + [
                pltpu.make_async_copy(y_hbm.at[idx_v.at[k, pl.ds(i * W, W)]],
                                      b["rows"].at[k], b["gsem"]) for k in range(TOP_K)]

        def write_back(b, win):
            return pltpu.make_async_copy(b["out"], o_hbm.at[pl.ds(win * W, W)], b["wsem"])

        def fetch(b, win):
            for c in loads(b, win):
                c.start()

        def reduce_rows(b):
            rows, gate_v, out = b["rows"], b["gate"], b["out"]

            @pl.loop(0, W)
            def _(j):
                gate = [gate_v[pl.ds((j * TOP_K + k) * SC_LANES, SC_LANES)] for k in range(TOP_K)]

                @plsc.parallel_loop(0, w, step=SC_LANES, unroll=4)
                def _(col):
                    lo, hi = [], []
                    for k in range(TOP_K):
                        p = rows[k, j, pl.ds(col, SC_LANES)]
                        lo.append(plsc.bitcast(p << 16, F32) * gate[k])
                        hi.append(plsc.bitcast(p & jnp.uint32(0xFFFF0000), F32) * gate[k])
                    out[j, pl.ds(col, SC_LANES)] = (lo[0] + lo[1]) + (lo[2] + lo[3]) + (lo[4] + lo[5])
                    out[j, pl.ds(w + col, SC_LANES)] = (hi[0] + hi[1]) + (hi[2] + hi[3]) + (hi[4] + hi[5])

        def process(b, win, not_first):
            for c in loads(b, win):
                c.wait()

            @pl.when(not_first)
            def _():
                write_back(b, win).wait()

            reduce_rows(b)
            write_back(b, win).start()

        fetch(slots[0], first)

        @pl.loop(0, per_worker, step=2)
        def _(i):
            win = first + i
            fetch(slots[1], win + 1)
            process(slots[0], win, i > 0)

            @pl.when(i + 2 < per_worker)
            def _():
                fetch(slots[0], win + 2)

            process(slots[1], win + 1, i > 0)

        for b in slots:
            write_back(b, first).wait()

    slot_types = ([pltpu.VMEM((TOP_K, per_worker * W), jnp.int32)]
                  + [pltpu.VMEM((TOP_K, W, w), jnp.uint32)] * 2
                  + [pltpu.VMEM((W * TOP_K * SC_LANES,), F32)] * 2
                  + [pltpu.VMEM((W, 2 * w), F32)] * 2
                  + [pltpu.SemaphoreType.DMA] * 4)
    return pl.kernel(
        body,
        out_type=jax.ShapeDtypeStruct((T, 2 * w), F32),
        mesh=_sc_mesh(),
        scratch_types=slot_types,
        compiler_params=pltpu.CompilerParams(needs_layout_passes=False),
        name="sc_combine",
    )(ys, dest, gates)


def _experts_kernel(be_ref, nv_ref, io_ref, xs_ref, wg_ref, wu_ref, wd_ref, ys_ref):
    del be_ref, io_ref
    n_valid = nv_ref[pl.program_id(0)]
    mb, w = xs_ref.shape

    @pl.when(n_valid > 0)
    def _():
        keep = lax.broadcasted_iota(jnp.int32, (mb, w), 0) < n_valid
        lo, hi = _unpack_rows(jnp.where(keep, xs_ref[...], jnp.uint32(0)))
        lo, hi = lo.astype(BF16), hi.astype(BF16)

        def proj(w_ref):
            return (jnp.dot(lo, w_ref[0, :w, :], preferred_element_type=F32)
                    + jnp.dot(hi, w_ref[0, w:, :], preferred_element_type=F32))

        hb = jax.nn.silu(proj(wg_ref)) * proj(wu_ref)
        ys_ref[...] = _pack_rows(jnp.dot(hb.astype(BF16), wd_ref[0], preferred_element_type=F32))


def _experts(xs, blk_exp, n_valid, blk_io, p):
    n_rows, w = xs.shape
    mb = n_rows // blk_exp.shape[0]
    grid_spec = pltpu.PrefetchScalarGridSpec(
        num_scalar_prefetch=3,
        grid=(n_rows // mb,),
        in_specs=[
            pl.BlockSpec((mb, w), lambda i, be, nv, io: (io[i], 0)),
            pl.BlockSpec((1, D_MODEL, EXPERT_DIM), lambda i, be, nv, io: (be[i], 0, 0)),
            pl.BlockSpec((1, D_MODEL, EXPERT_DIM), lambda i, be, nv, io: (be[i], 0, 0)),
            pl.BlockSpec((1, EXPERT_DIM, D_MODEL), lambda i, be, nv, io: (be[i], 0, 0)),
        ],
        out_specs=pl.BlockSpec((mb, w), lambda i, be, nv, io: (io[i], 0)),
    )
    return pl.pallas_call(
        _experts_kernel,
        grid_spec=grid_spec,
        out_shape=jax.ShapeDtypeStruct((n_rows, w), jnp.uint32),
        compiler_params=pltpu.CompilerParams(
            dimension_semantics=("arbitrary",), vmem_limit_bytes=VMEM_LIMIT),
        name="experts",
    )(blk_exp, n_valid, blk_io, xs, p["w_e_gate"], p["w_e_up"], p["w_e_down"])


def _final_kernel(x1_ref, routed_ref, wsg_ref, wsu_ref, wsd_ref, l2g_ref, l2b_ref, out_ref):
    x1 = x1_ref[...]
    xb = x1.astype(BF16)
    hs = (jax.nn.silu(jnp.dot(xb, wsg_ref[...], preferred_element_type=F32))
          * jnp.dot(xb, wsu_ref[...], preferred_element_type=F32))
    shared = jnp.dot(hs.astype(BF16), wsd_ref[...], preferred_element_type=F32)
    out_ref[...] = _layer_norm(ALPHA * x1 + (routed_ref[...] + shared), l2g_ref[...], l2b_ref[...])


def _final(x1, routed, p):
    T = x1.shape[0]
    tm = TOKEN_TILE
    tok = lambda w: pl.BlockSpec((tm, w), lambda i: (i, 0))
    return pl.pallas_call(
        _final_kernel,
        grid=(T // tm,),
        in_specs=[
            tok(D_MODEL), tok(D_MODEL),
            _const_spec((D_MODEL, SHARED_DIM)), _const_spec((D_MODEL, SHARED_DIM)),
            _const_spec((SHARED_DIM, D_MODEL)),
            _const_spec((1, D_MODEL)), _const_spec((1, D_MODEL)),
        ],
        out_specs=tok(D_MODEL),
        out_shape=jax.ShapeDtypeStruct((T, D_MODEL), F32),
        compiler_params=pltpu.CompilerParams(
            dimension_semantics=("parallel",), vmem_limit_bytes=VMEM_LIMIT),
        name="final",
    )(x1, routed, p["w_sh_gate"], p["w_sh_up"], p["w_sh_down"], p["ln2_g"], p["ln2_b"])


def _routing_layout(route, counts, n_tokens):
    mb = EXPERT_BLOCK
    n_blocks = -(-n_tokens * TOP_K // mb) + N_EXPERTS
    idx = route[0:TOP_K].astype(jnp.int32)
    rank = route[8:8 + TOP_K].astype(jnp.int32)
    counts = counts.astype(jnp.int32)
    padded = (counts + mb - 1) // mb * mb
    pad_end = jnp.cumsum(padded)
    pad_start = pad_end - padded
    experts = jnp.arange(N_EXPERTS, dtype=jnp.int32)
    dest = rank + jnp.sum(jnp.where(idx[:, :, None] == experts, pad_start, 0), axis=-1)
    blk_start = jnp.arange(n_blocks, dtype=jnp.int32) * mb
    blk_exp = jnp.minimum(
        jnp.sum((pad_end[None, :] <= blk_start[:, None]).astype(jnp.int32), axis=1), N_EXPERTS - 1)
    valid_end = jnp.sum(jnp.where(blk_exp[:, None] == experts, pad_start + counts, 0), axis=-1)
    n_valid = jnp.clip(valid_end - blk_start, 0, mb).astype(jnp.int32)
    blk_io = jnp.minimum(jnp.arange(n_blocks, dtype=jnp.int32), pad_end[-1] // mb - 1)
    gates = jnp.broadcast_to(
        route[16:16 + TOP_K].T.reshape(n_tokens // SC_COMBINE_WINDOW, SC_COMBINE_WINDOW, TOP_K, 1),
        (n_tokens // SC_COMBINE_WINDOW, SC_COMBINE_WINDOW, TOP_K, SC_LANES)).reshape(
            n_tokens // SC_COMBINE_WINDOW, -1)
    return dest, gates, blk_exp, n_valid, blk_io, n_blocks * mb


def _encode(x, p):
    batch, seq, _ = x.shape
    T = batch * seq
    xt = x.reshape(T, D_MODEL)
    ag, q, gf, gb, iv, so, sgb, gmin = _inproj(xt, p)
    of, ob = _hgrn(q, gf, gb, iv, gmin, batch)
    x1, x1p, route, cnt = _mix(xt, of, ob, so, sgb, ag, p)
    dest, gates, blk_exp, n_valid, blk_io, n_rows = _routing_layout(route, cnt[:, 0], T)
    xs = _sc_dispatch(x1p, dest, n_rows)
    ys = _experts(xs, blk_exp, n_valid, blk_io, p)
    out = _final(x1, _sc_combine(ys, dest, gates), p)
    return out.reshape(batch, seq, D_MODEL)


def _prepare_params(ln_in_g, ln_in_b, w_in, a_ln_g, a_ln_b, a_ws, a_sb, hg_lb_logits, hg_norm_g,
                    w_pa, w_pb, w_o, ln1_g, ln1_b, w_router, router_bias, w_e_gate, w_e_up,
                    w_e_down, w_sh_gate, w_sh_up, w_sh_down, ln2_g, ln2_b):
    l = 0
    row = lambda v: v.reshape(1, -1).astype(F32)
    ws = a_ws[l].astype(BF16)
    wsp = jnp.concatenate([ws[0::2], ws[1::2]], axis=2)
    sbf = jnp.repeat(a_sb[l].astype(F32), A_WIDTH // A_GROUPS, axis=1)
    lb = jnp.cumsum(jax.nn.softmax(hg_lb_logits.astype(F32), axis=1), axis=1)[:, l]
    return dict(
        ln_in_g=row(ln_in_g), ln_in_b=row(ln_in_b), w_in=w_in[l].astype(BF16),
        a_ln_g=row(a_ln_g[l]), a_ln_b=row(a_ln_b[l]), wsp=wsp, sbf=sbf, lb=lb,
        w_pa=w_pa[l].astype(BF16), hg_norm_g=row(hg_norm_g[l]),
        w_pb=w_pb[l].astype(BF16), w_o=w_o[l].astype(BF16),
        ln1_g=row(ln1_g[l]), ln1_b=row(ln1_b[l]),
        w_router=w_router[l].T.astype(BF16),
        router_bias=jnp.broadcast_to(router_bias[l].astype(F32)[:, None], (N_EXPERTS, LANES)),
        w_e_gate=w_e_gate[l].astype(BF16), w_e_up=w_e_up[l].astype(BF16),
        w_e_down=w_e_down[l].astype(BF16),
        w_sh_gate=w_sh_gate[l].astype(BF16), w_sh_up=w_sh_up[l].astype(BF16),
        w_sh_down=w_sh_down[l].astype(BF16),
        ln2_g=row(ln2_g[l]), ln2_b=row(ln2_b[l]),
    )


def kernel(x_prompt, x_sample, ln_in_g, ln_in_b, w_in, a_ln_g, a_ln_b, a_ws, a_sb, hg_lb_logits,
           hg_norm_g, w_pa, w_pb, w_o, ln1_g, ln1_b, w_router, router_bias, w_e_gate, w_e_up,
           w_e_down, w_sh_gate, w_sh_up, w_sh_down, ln2_g, ln2_b):
    p = _prepare_params(ln_in_g, ln_in_b, w_in, a_ln_g, a_ln_b, a_ws, a_sb, hg_lb_logits, hg_norm_g,
                        w_pa, w_pb, w_o, ln1_g, ln1_b, w_router, router_bias, w_e_gate, w_e_up,
                        w_e_down, w_sh_gate, w_sh_up, w_sh_down, ln2_g, ln2_b)
    return _encode(x_prompt, p), _encode(x_sample, p)
```

```python
import functools

import jax
import jax.numpy as jnp
from jax import lax
from jax.experimental import pallas as pl
from jax.experimental.pallas import tpu as pltpu
from jax.experimental.pallas import tpu_sc as plsc

F32 = jnp.float32
BF16 = jnp.bfloat16

D_MODEL = 1024
A_GROUPS = 8
A_WIDTH = 512
A_CHUNK = 128
HG_HEADS = 8
HG_DK = 128
HG_WIDTH = HG_HEADS * HG_DK
N_IN = 2 * A_WIDTH + 5 * HG_WIDTH + 2 * D_MODEL
N_EXPERTS = 64
TOP_K = 6
N_GROUPS = 8
TOPK_GROUPS = 4
GROUP_SIZE = N_EXPERTS // N_GROUPS
EXPERT_DIM = 256
SHARED_DIM = 256
ROUTED_SCALE = 2.5
DEPTH = 1
ALPHA = (2.0 * DEPTH) ** 0.25
LN_EPS = 1e-5
RMS_EPS = 1e-6

LANES = 128
TOKEN_TILE = 512
INPROJ_TILE = 256
HG_CHUNK = 128
HG_CHUNKS_PER_STEP = 4
HG_SAFE_LOGDECAY = 80.0
EXPERT_BLOCK = 1024
SC_CORES = 2
SC_SUBCORES = 16
SC_LANES = 16
SC_WINDOW = 32
SC_COMBINE_WINDOW = 8
VMEM_LIMIT = 56 * 1024 * 1024

_O_U, _O_V, _O_Q, _O_FF, _O_FB, _O_I, _O_G, _O_GA, _O_GB = (
    0, 512, 1024, 2048, 3072, 4096, 5120, 6144, 7168)


def _layer_norm(x, g, b):
    mu = jnp.mean(x, axis=-1, keepdims=True)
    xc = x - mu
    var = jnp.mean(xc * xc, axis=-1, keepdims=True)
    return xc * lax.rsqrt(var + LN_EPS) * g + b


def _bdot(a, b):
    return jnp.dot(a.astype(BF16), b.astype(BF16), preferred_element_type=F32)


def _pack_rows(x):
    w = x.shape[1] // 2
    lo = lax.bitcast_convert_type(x[:, :w].astype(BF16).astype(F32), jnp.uint32)
    hi = lax.bitcast_convert_type(x[:, w:].astype(BF16).astype(F32), jnp.uint32)
    return hi | (lo >> 16)


def _unpack_rows(p):
    lo = lax.bitcast_convert_type(p << 16, F32)
    hi = lax.bitcast_convert_type(p & jnp.uint32(0xFFFF0000), F32)
    return lo, hi


def _const_spec(shape):
    nd = len(shape)
    return pl.BlockSpec(shape, lambda *_: (0,) * nd, pipeline_mode=pl.Buffered(1))


def _inproj_tile(xb_ref, xb_next_ref, x_next_ref, lng_ref, lnb_ref, win_ref, alng_ref, alnb_ref,
                 wsp_ref, sbf_ref, lb_ref, wpa_ref, ag_ref, q_ref, gf_ref, gb_ref, iv_ref, so_ref,
                 sgb_ref, gmin_ref):
    tm = xb_ref.shape[0]

    def sec(lo, width):
        return jnp.dot(xb_ref[...], win_ref[:, lo:lo + width], preferred_element_type=F32)

    v = _layer_norm(jax.nn.gelu(sec(_O_V, A_WIDTH)), alng_ref[...], alnb_ref[...]).astype(BF16)
    u = jax.nn.gelu(sec(_O_U, A_WIDTH))
    xb_next_ref[...] = _layer_norm(x_next_ref[...], lng_ref[...], lnb_ref[...]).astype(BF16)

    q_ref[...] = jax.nn.silu(sec(_O_Q, HG_WIDTH)).astype(BF16)
    mins = []
    for d, (off, g_ref) in enumerate(((_O_FF, gf_ref), (_O_FB, gb_ref))):
        lb = lb_ref[d:d + 1, :]
        f = lb + (1.0 - lb) * jax.nn.sigmoid(sec(off, HG_WIDTH))
        g = jnp.log(f)
        g_ref[...] = g
        half = jnp.sum(g.reshape(tm // (HG_CHUNK // 2), HG_CHUNK // 2, HG_WIDTH), axis=1)
        mins.append(jnp.min(half, axis=-1, keepdims=True))
    gmin_ref[0] = jnp.broadcast_to(jnp.concatenate(mins, axis=0), gmin_ref.shape[1:])
    so_ref[...] = jax.nn.silu(sec(_O_G, HG_WIDTH)).astype(BF16)
    sgb_ref[...] = jax.nn.sigmoid(sec(_O_GB, D_MODEL)).astype(BF16)

    lane = lax.broadcasted_iota(jnp.int32, (A_CHUNK, LANES), 1)
    left = lane < (A_WIDTH // A_GROUPS)
    zero = jnp.zeros((A_CHUNK, LANES), BF16)
    chunks = []
    for c in range(tm // A_CHUNK):
        vc = v[c * A_CHUNK:(c + 1) * A_CHUNK]
        cols = []
        for p in range(A_GROUPS // 2):
            vp = vc[:, p * LANES:(p + 1) * LANES]
            rhs = jnp.concatenate([jnp.where(left, vp, zero), jnp.where(left, zero, vp)], axis=0)
            cols.append(jnp.dot(wsp_ref[p], rhs, preferred_element_type=F32))
        chunks.append(jnp.concatenate(cols, axis=1) + sbf_ref[...])
    mixed = jnp.concatenate(chunks, axis=0)
    a = _bdot(u * mixed, wpa_ref[...])
    ag_ref[...] = (jax.nn.sigmoid(sec(_O_GA, D_MODEL)) * a).astype(BF16)
    iv_ref[...] = sec(_O_I, HG_WIDTH).astype(BF16)


def _inproj_kernel(x_first_ref, x_next_ref, lng_ref, lnb_ref, *rest):
    *refs, xb_even_ref, xb_odd_ref = rest
    i = pl.program_id(0)

    @pl.when(i == 0)
    def _():
        xb_even_ref[...] = _layer_norm(x_first_ref[...], lng_ref[...], lnb_ref[...]).astype(BF16)

    @pl.when(i % 2 == 0)
    def _():
        _inproj_tile(xb_even_ref, xb_odd_ref, x_next_ref, lng_ref, lnb_ref, *refs)

    @pl.when(i % 2 == 1)
    def _():
        _inproj_tile(xb_odd_ref, xb_even_ref, x_next_ref, lng_ref, lnb_ref, *refs)


def _inproj(x, p):
    T = x.shape[0]
    tm = INPROJ_TILE
    nt = T // tm
    nh = 2 * tm // (HG_CHUNK // 2)
    tok = lambda w: pl.BlockSpec((tm, w), lambda i: (i, 0))
    outs = (
        jax.ShapeDtypeStruct((T, D_MODEL), BF16),
        jax.ShapeDtypeStruct((T, HG_WIDTH), BF16),
        jax.ShapeDtypeStruct((T, HG_WIDTH), F32),
        jax.ShapeDtypeStruct((T, HG_WIDTH), F32),
        jax.ShapeDtypeStruct((T, HG_WIDTH), BF16),
        jax.ShapeDtypeStruct((T, HG_WIDTH), BF16),
        jax.ShapeDtypeStruct((T, D_MODEL), BF16),
        jax.ShapeDtypeStruct((nt, nh, LANES), F32),
    )
    return pl.pallas_call(
        _inproj_kernel,
        grid=(nt,),
        in_specs=[
            _const_spec((tm, D_MODEL)),
            pl.BlockSpec((tm, D_MODEL), lambda i: (jnp.minimum(i + 1, nt - 1), 0)),
            _const_spec((1, D_MODEL)), _const_spec((1, D_MODEL)),
            _const_spec((D_MODEL, N_IN)),
            _const_spec((1, A_WIDTH)), _const_spec((1, A_WIDTH)),
            _const_spec((A_GROUPS // 2, A_CHUNK, 2 * A_CHUNK)),
            _const_spec((A_CHUNK, A_WIDTH)),
            _const_spec((2, HG_WIDTH)),
            _const_spec((A_WIDTH, D_MODEL)),
        ],
        out_specs=(tok(D_MODEL), tok(HG_WIDTH), tok(HG_WIDTH), tok(HG_WIDTH), tok(HG_WIDTH),
                   tok(HG_WIDTH), tok(D_MODEL), pl.BlockSpec((1, nh, LANES), lambda i: (i, 0, 0))),
        out_shape=outs,
        scratch_shapes=[pltpu.VMEM((tm, D_MODEL), BF16), pltpu.VMEM((tm, D_MODEL), BF16)],
        compiler_params=pltpu.CompilerParams(
            dimension_semantics=("arbitrary",), vmem_limit_bytes=VMEM_LIMIT),
        name="inproj",
    )(x, x, p["ln_in_g"], p["ln_in_b"], p["w_in"], p["a_ln_g"], p["a_ln_b"], p["wsp"], p["sbf"],
      p["lb"], p["w_pa"])


_NT = (((1,), (1,)), ((), ()))
_TN = (((0,), (0,)), ((), ()))


def _hgrn_direction(q_ref, g_ref, v_ref, st_ref, o_ref, tri, mask, fwd):
    C = q_ref.shape[0]
    g = g_ref[...]
    ghi = g.astype(BF16)
    glo = (g - ghi.astype(F32)).astype(BF16)
    b = jnp.dot(jnp.concatenate([tri, tri], axis=1), jnp.concatenate([ghi, glo], axis=0),
                preferred_element_type=F32)
    mid = C // 2 - 1 if fwd else C // 2
    end = C - 1 if fwd else 0
    r = b[mid:mid + 1, :]
    b_end = b[end:end + 1, :]
    qt = q_ref[...].astype(F32) * jnp.exp(b - r)
    kt = (1.0 - jnp.exp(g)) * jnp.exp(r - b)
    qtb = qt.astype(BF16)
    ktb = kt.astype(BF16)
    qhb = (qt * jnp.exp(r)).astype(BF16)
    khb = (kt * jnp.exp(b_end - r)).astype(BF16)
    decay = jnp.exp(b_end)
    v32 = v_ref[...].astype(F32)
    heads = [slice(h * HG_DK, (h + 1) * HG_DK) for h in range(HG_HEADS)]
    zero = jnp.zeros((C, HG_DK), BF16)
    mask2 = jnp.concatenate([mask, mask], axis=1)
    scores = []
    for p in range(HG_HEADS // 2):
        k1, k2 = ktb[:, heads[2 * p]], ktb[:, heads[2 * p + 1]]
        kk = jnp.concatenate([jnp.concatenate([k1, zero], axis=1),
                              jnp.concatenate([zero, k2], axis=1)], axis=0)
        s2 = lax.dot_general(qtb[:, 2 * p * HG_DK:(2 * p + 2) * HG_DK], kk, _NT,
                             preferred_element_type=F32)
        s2 = jnp.where(mask2, s2, 0.0).astype(BF16)
        scores += [s2[:, :C], s2[:, C:]]
    for h, sl in enumerate(heads):
        st = st_ref[h]
        vt = v32[:, sl].T.astype(BF16)
        o_ref[:, sl] = lax.dot_general(
            jnp.concatenate([scores[h], qhb[:, sl]], axis=1),
            jnp.concatenate([vt, st.astype(BF16)], axis=1), _NT,
            preferred_element_type=F32).astype(o_ref.dtype)
        st_ref[h] = st * decay[:, sl] + jnp.dot(vt, khb[:, sl], preferred_element_type=F32)


def _hgrn_direction_stepwise(q_ref, g_ref, v_ref, st_ref, o_ref, q32_ref, v32_ref, o32_ref, fwd):
    C = q_ref.shape[0]
    q32_ref[...] = q_ref[...].astype(F32)
    v32_ref[...] = v_ref[...].astype(F32)
    sub = 8
    rows = lax.broadcasted_iota(jnp.int32, (sub, HG_DK), 0)

    def group(i, carry):
        base = pl.multiple_of((i if fwd else C // sub - 1 - i) * sub, sub)
        f = jnp.exp(g_ref[pl.ds(base, sub), :])
        k = 1.0 - f
        q = q32_ref[pl.ds(base, sub), :]
        v = v32_ref[pl.ds(base, sub), :]
        for h in range(HG_HEADS):
            sl = slice(h * HG_DK, (h + 1) * HG_DK)
            st = st_ref[h]
            out = jnp.zeros((sub, HG_DK), F32)
            for r in (range(sub) if fwd else range(sub - 1, -1, -1)):
                v_t = jnp.where(rows == 0, v[r:r + 1, sl], 0.0).astype(BF16)
                k_t = jnp.broadcast_to(k[r:r + 1, sl], (sub, HG_DK)).astype(BF16)
                st = st * f[r:r + 1, sl] + lax.dot_general(v_t, k_t, _TN, preferred_element_type=F32)
                q_t = jnp.broadcast_to(q[r:r + 1, sl], (sub, HG_DK)).astype(BF16)
                o_t = lax.dot_general(q_t, st.astype(BF16), _NT, preferred_element_type=F32)
                out = jnp.where(rows == r, o_t, out)
            st_ref[h] = st
            o32_ref[pl.ds(base, sub), sl] = out
        return carry

    lax.fori_loop(0, C // sub, group, 0)
    o_ref[...] = o32_ref[...].astype(o_ref.dtype)


def _hgrn_kernel(safe_ref, qf_ref, qb_ref, gf_ref, gb_ref, vf_ref, vb_ref, tril_ref, triu_ref,
                 of_ref, ob_ref, sf_ref, sb_ref, q32_ref, v32_ref, o32_ref):
    b, j = pl.program_id(0), pl.program_id(1)
    ns = pl.num_programs(1)
    n = HG_CHUNKS_PER_STEP
    C = qf_ref.shape[0] // n

    @pl.when(j == 0)
    def _():
        sf_ref[...] = jnp.zeros_like(sf_ref)
        sb_ref[...] = jnp.zeros_like(sb_ref)

    row = lax.broadcasted_iota(jnp.int32, (C, C), 0)
    col = lax.broadcasted_iota(jnp.int32, (C, C), 1)
    safe_f = [safe_ref[0, (b * ns + j) * n + u] != 0 for u in range(n)]
    safe_b = [safe_ref[1, (b * ns + ns - 1 - j) * n + u] != 0 for u in range(n)]
    part = lambda ref, u: ref.at[pl.ds(u * C, C), :]

    def forward(u, stepwise):
        refs = (part(qf_ref, u), part(gf_ref, u), part(vf_ref, u), sf_ref, part(of_ref, u))
        if stepwise:
            _hgrn_direction_stepwise(*refs, q32_ref, v32_ref, o32_ref, True)
        else:
            _hgrn_direction(*refs, tril_ref[...], row >= col, True)

    def backward(u, stepwise):
        refs = (part(qb_ref, u), part(gb_ref, u), part(vb_ref, u), sb_ref, part(ob_ref, u))
        if stepwise:
            _hgrn_direction_stepwise(*refs, q32_ref, v32_ref, o32_ref, False)
        else:
            _hgrn_direction(*refs, triu_ref[...], row <= col, False)

    all_safe = functools.reduce(jnp.logical_and, safe_f + safe_b)

    @pl.when(all_safe)
    def _():
        for u in range(n):
            forward(u, False)
            backward(n - 1 - u, False)

    @pl.when(jnp.logical_not(all_safe))
    def _():
        for u in range(n):
            pl.when(safe_f[u])(functools.partial(forward, u, False))
            pl.when(jnp.logical_not(safe_f[u]))(functools.partial(forward, u, True))
        for u in reversed(range(n)):
            pl.when(safe_b[u])(functools.partial(backward, u, False))
            pl.when(jnp.logical_not(safe_b[u]))(functools.partial(backward, u, True))


def _hgrn(q, gf, gb, iv, gmin, batch):
    T = q.shape[0]
    C = HG_CHUNK
    nc = T // batch // C
    nt, nh, _ = gmin.shape
    halves = gmin[:, :, 0].reshape(nt, 2, nh // 4, 2)
    safe = (jnp.min(halves, axis=-1) > -HG_SAFE_LOGDECAY).astype(jnp.int32)
    safe = safe.transpose(1, 0, 2).reshape(2, T // C)
    n = HG_CHUNKS_PER_STEP
    ns = nc // n
    fwd = pl.BlockSpec((n * C, HG_WIDTH), lambda b, j, s: (b * ns + j, 0))
    bwd = pl.BlockSpec((n * C, HG_WIDTH), lambda b, j, s: (b * ns + ns - 1 - j, 0))
    const = lambda shape: pl.BlockSpec(shape, lambda b, j, s: (0,) * len(shape),
                                       pipeline_mode=pl.Buffered(1))
    row = lax.broadcasted_iota(jnp.int32, (C, C), 0)
    col = lax.broadcasted_iota(jnp.int32, (C, C), 1)
    tril = (row >= col).astype(BF16)
    triu = (row <= col).astype(BF16)
    grid_spec = pltpu.PrefetchScalarGridSpec(
        num_scalar_prefetch=1,
        grid=(batch, ns),
        in_specs=[fwd, bwd, fwd, bwd, fwd, bwd, const((C, C)), const((C, C))],
        out_specs=(fwd, bwd),
        scratch_shapes=[pltpu.VMEM((HG_HEADS, HG_DK, HG_DK), F32),
                        pltpu.VMEM((HG_HEADS, HG_DK, HG_DK), F32),
                        pltpu.VMEM((C, HG_WIDTH), F32), pltpu.VMEM((C, HG_WIDTH), F32),
                        pltpu.VMEM((C, HG_WIDTH), F32)],
    )
    return pl.pallas_call(
        _hgrn_kernel,
        grid_spec=grid_spec,
        out_shape=(jax.ShapeDtypeStruct((T, HG_WIDTH), BF16), jax.ShapeDtypeStruct((T, HG_WIDTH), BF16)),
        compiler_params=pltpu.CompilerParams(
            dimension_semantics=("parallel", "arbitrary"), vmem_limit_bytes=VMEM_LIMIT),
        name="hgrn",
    )(safe, q, q, gf, gb, iv, iv, tril, triu)


def _mix_kernel(x_ref, lng_ref, lnb_ref, of_ref, ob_ref, so_ref, sgb_ref, ag_ref, ng_ref, wpb_ref,
                wo_ref, l1g_ref, l1b_ref, wr_ref, rb_ref, tri_ref, x1_ref, x1p_ref, route_ref,
                cnt_ref, carry_ref):
    tm = x_ref.shape[0]

    @pl.when(pl.program_id(0) == 0)
    def _():
        carry_ref[...] = jnp.zeros_like(carry_ref)

    o = of_ref[...].astype(F32) + ob_ref[...].astype(F32)
    heads = []
    for h in range(HG_HEADS):
        oh = o[:, h * HG_DK:(h + 1) * HG_DK]
        heads.append(oh * lax.rsqrt(jnp.mean(oh * oh, axis=-1, keepdims=True) + RMS_EPS))
    rn = jnp.concatenate(heads, axis=1) * ng_ref[...] * so_ref[...].astype(F32)
    r = _bdot(rn, wpb_ref[...])
    mixed = ag_ref[...].astype(F32) + sgb_ref[...].astype(F32) * r
    y = _bdot(mixed, wo_ref[...])
    xn = _layer_norm(x_ref[...], lng_ref[...], lnb_ref[...])
    x1 = _layer_norm(ALPHA * xn + y, l1g_ref[...], l1b_ref[...])
    x1_ref[...] = x1
    x1p_ref[...] = _pack_rows(x1)

    neg = jnp.float32(-jnp.inf)
    reps = tm // LANES
    scores = jax.nn.sigmoid(lax.dot_general(wr_ref[...], x1.astype(BF16), _NT,
                                            preferred_element_type=F32))
    biased = (scores + jnp.concatenate([rb_ref[...]] * reps, axis=1)).reshape(
        N_GROUPS, GROUP_SIZE, tm)
    sub = lax.broadcasted_iota(jnp.int32, biased.shape, 1).astype(F32)
    m1 = jnp.max(biased, axis=1, keepdims=True)
    first = jnp.min(jnp.where(biased == m1, sub, float(GROUP_SIZE)), axis=1, keepdims=True)
    m2 = jnp.max(jnp.where(sub == first, neg, biased), axis=1, keepdims=True)
    gs = (m1 + m2).reshape(N_GROUPS, tm)
    grp = lax.broadcasted_iota(jnp.int32, (N_GROUPS, tm), 0)
    ahead = jnp.zeros((N_GROUPS, tm), F32)
    for d in range(1, N_GROUPS):
        other = pltpu.roll(gs, d, 0)
        tie = jnp.where(grp >= d, 1.0, 0.0)
        ahead = ahead + jnp.where(other > gs, 1.0, jnp.where(other == gs, tie, 0.0))
    keep = (ahead < TOPK_GROUPS).reshape(N_GROUPS, 1, tm)
    allowed = jnp.where(keep, biased, neg).reshape(N_EXPERTS, tm)
    row = lax.broadcasted_iota(jnp.int32, (N_EXPERTS, tm), 0).astype(F32)
    sel = jnp.zeros((N_EXPERTS, tm), F32)
    picks = []
    for _ in range(TOP_K):
        m = jnp.max(allowed, axis=0, keepdims=True)
        first = jnp.min(jnp.where(allowed == m, row, float(N_EXPERTS)), axis=0, keepdims=True)
        hit = row == first
        picks.append((first, hit, jnp.sum(jnp.where(hit, scores, 0.0), axis=0, keepdims=True)))
        sel = jnp.where(hit, 1.0, sel)
        allowed = jnp.where(hit, neg, allowed)
    wsum = picks[0][2]
    for pk in picks[1:]:
        wsum = wsum + pk[2]
    selb = sel.astype(BF16)
    carry = carry_ref[...]
    before = (jnp.dot(selb, tri_ref[...], preferred_element_type=F32)
              + jnp.concatenate([carry] * reps, axis=1))
    total = carry + jnp.dot(selb, jnp.ones((tm, LANES), BF16), preferred_element_type=F32)
    carry_ref[...] = total
    cnt_ref[...] = total
    blank = [jnp.zeros((1, tm), F32)] * (8 - TOP_K)
    route_ref[...] = jnp.concatenate(
        [pk[0] for pk in picks] + blank
        + [jnp.sum(jnp.where(pk[1], before, 0.0), axis=0, keepdims=True) for pk in picks] + blank
        + [pk[2] / wsum * ROUTED_SCALE for pk in picks] + blank, axis=0)


def _mix(x, of, ob, so, sgb, ag, p):
    T = x.shape[0]
    tm = TOKEN_TILE
    tok = lambda w: pl.BlockSpec((tm, w), lambda i: (i, 0))
    row = lax.broadcasted_iota(jnp.int32, (tm, tm), 0)
    col = lax.broadcasted_iota(jnp.int32, (tm, tm), 1)
    tri = (row < col).astype(BF16)
    return pl.pallas_call(
        _mix_kernel,
        grid=(T // tm,),
        in_specs=[
            tok(D_MODEL), _const_spec((1, D_MODEL)), _const_spec((1, D_MODEL)),
            tok(HG_WIDTH), tok(HG_WIDTH), tok(HG_WIDTH), tok(D_MODEL), tok(D_MODEL),
            _const_spec((1, HG_WIDTH)),
            _const_spec((HG_WIDTH, D_MODEL)), _const_spec((D_MODEL, D_MODEL)),
            _const_spec((1, D_MODEL)), _const_spec((1, D_MODEL)),
            _const_spec((N_EXPERTS, D_MODEL)), _const_spec((N_EXPERTS, LANES)),
            _const_spec((tm, tm)),
        ],
        out_specs=(tok(D_MODEL), tok(D_MODEL // 2), pl.BlockSpec((24, tm), lambda i: (0, i)),
                   pl.BlockSpec((N_EXPERTS, LANES), lambda i: (0, 0))),
        out_shape=(jax.ShapeDtypeStruct((T, D_MODEL), F32),
                   jax.ShapeDtypeStruct((T, D_MODEL // 2), jnp.uint32),
                   jax.ShapeDtypeStruct((24, T), F32),
                   jax.ShapeDtypeStruct((N_EXPERTS, LANES), F32)),
        scratch_shapes=[pltpu.VMEM((N_EXPERTS, LANES), F32)],
        compiler_params=pltpu.CompilerParams(
            dimension_semantics=("arbitrary",), vmem_limit_bytes=VMEM_LIMIT),
        name="mix",
    )(x, p["ln_in_g"], p["ln_in_b"], of, ob, so, sgb, ag, p["hg_norm_g"], p["w_pb"], p["w_o"],
      p["ln1_g"], p["ln1_b"], p["w_router"], p["router_bias"], tri)


def _sc_mesh():
    return plsc.VectorSubcoreMesh(core_axis_name="c", subcore_axis_name="s",
                                  num_cores=SC_CORES, num_subcores=SC_SUBCORES)


def _sc_worker():
    return lax.axis_index("s") * SC_CORES + lax.axis_index("c")


def _sc_dispatch(x1p, dest, n_rows):
    T, w = x1p.shape
    W = SC_WINDOW
    per_worker = T // W // (SC_CORES * SC_SUBCORES)

    def body(x_hbm, d_hbm, o_hbm, rows_v, idx_v, sem):
        first = _sc_worker() * per_worker
        for k in range(TOP_K):
            pltpu.sync_copy(d_hbm.at[k, pl.ds(first, per_worker)], idx_v.at[k])

        @pl.loop(0, per_worker)
        def _(j):
            pltpu.sync_copy(x_hbm.at[pl.ds((first + j) * W, W)], rows_v)
            copies = [pltpu.async_copy(rows_v, o_hbm.at[idx_v.at[k, j]], sem) for k in range(TOP_K)]
            for c in copies:
                c.wait()

    return pl.kernel(
        body,
        out_type=jax.ShapeDtypeStruct((n_rows, w), jnp.uint32),
        mesh=_sc_mesh(),
        scratch_types=[pltpu.VMEM((W, w), jnp.uint32), pltpu.VMEM((TOP_K, per_worker, W), jnp.int32),
                       pltpu.SemaphoreType.DMA],
        name="sc_dispatch",
    )(x1p, dest.reshape(TOP_K, T // W, W))


def _sc_combine(ys, dest, gates):
    T = dest.shape[1]
    W = SC_COMBINE_WINDOW
    w = ys.shape[1]
    per_worker = T // W // (SC_CORES * SC_SUBCORES)
    assert per_worker % 2 == 0 and per_worker * W * SC_CORES * SC_SUBCORES == T

    def body(y_hbm, d_hbm, g_hbm, o_hbm, idx_v, *scratch):
        slots = [dict(rows=scratch[s], gate=scratch[2 + s], out=scratch[4 + s], gsem=scratch[6 + s],
                      wsem=scratch[8 + s]) for s in range(2)]
        first = _sc_worker() * per_worker
        for k in range(TOP_K):
            pltpu.sync_copy(d_hbm.at[k, pl.ds(first * W, per_worker * W)], idx_v.at[k])

        def loads(b, win):
            i = win - first
            return [pltpu.make_async_copy(g_hbm.at[win], b["gate"], b["gsem"])] + [
                pltpu.make_async_copy(y_hbm.at[idx_v.at[k, pl.ds(i * W, W)]],
                                      b["rows"].at[k], b["gsem"]) for k in range(TOP_K)]

        def write_back(b, win):
            return pltpu.make_async_copy(b["out"], o_hbm.at[pl.ds(win * W, W)], b["wsem"])

        def fetch(b, win):
            for c in loads(b, win):
                c.start()

        def reduce_rows(b):
            rows, gate_v, out = b["rows"], b["gate"], b["out"]

            @pl.loop(0, W)
            def _(j):
                gate = [gate_v[pl.ds((j * TOP_K + k) * SC_LANES, SC_LANES)] for k in range(TOP_K)]

                @plsc.parallel_loop(0, w, step=SC_LANES, unroll=4)
                def _(col):
                    lo, hi = [], []
                    for k in range(TOP_K):
                        p = rows[k, j, pl.ds(col, SC_LANES)]
                        lo.append(plsc.bitcast(p << 16, F32) * gate[k])
                        hi.append(plsc.bitcast(p & jnp.uint32(0xFFFF0000), F32) * gate[k])
                    out[j, pl.ds(col, SC_LANES)] = (lo[0] + lo[1]) + (lo[2] + lo[3]) + (lo[4] + lo[5])
                    out[j, pl.ds(w + col, SC_LANES)] = (hi[0] + hi[1]) + (hi[2] + hi[3]) + (hi[4] + hi[5])

        def process(b, win, not_first):
            for c in loads(b, win):
                c.wait()

            @pl.when(not_first)
            def _():
                write_back(b, win).wait()

            reduce_rows(b)
            write_back(b, win).start()

        fetch(slots[0], first)

        @pl.loop(0, per_worker, step=2)
        def _(i):
            win = first + i
            fetch(slots[1], win + 1)
            process(slots[0], win, i > 0)

            @pl.when(i + 2 < per_worker)
            def _():
                fetch(slots[0], win + 2)

            process(slots[1], win + 1, i > 0)

        for b in slots:
            write_back(b, first).wait()

    slot_types = ([pltpu.VMEM((TOP_K, per_worker * W), jnp.int32)]
                  + [pltpu.VMEM((TOP_K, W, w), jnp.uint32)] * 2
                  + [pltpu.VMEM((W * TOP_K * SC_LANES,), F32)] * 2
                  + [pltpu.VMEM((W, 2 * w), F32)] * 2
                  + [pltpu.SemaphoreType.DMA] * 4)
    return pl.kernel(
        body,
        out_type=jax.ShapeDtypeStruct((T, 2 * w), F32),
        mesh=_sc_mesh(),
        scratch_types=slot_types,
        compiler_params=pltpu.CompilerParams(needs_layout_passes=False),
        name="sc_combine",
    )(ys, dest, gates)


def _experts_kernel(be_ref, nv_ref, io_ref, xs_ref, wg_ref, wu_ref, wd_ref, ys_ref):
    del be_ref, io_ref
    n_valid = nv_ref[pl.program_id(0)]
    mb, w = xs_ref.shape

    @pl.when(n_valid > 0)
    def _():
        keep = lax.broadcasted_iota(jnp.int32, (mb, w), 0) < n_valid
        lo, hi = _unpack_rows(jnp.where(keep, xs_ref[...], jnp.uint32(0)))
        lo, hi = lo.astype(BF16), hi.astype(BF16)

        def proj(w_ref):
            return (jnp.dot(lo, w_ref[0, :w, :], preferred_element_type=F32)
                    + jnp.dot(hi, w_ref[0, w:, :], preferred_element_type=F32))

        hb = jax.nn.silu(proj(wg_ref)) * proj(wu_ref)
        ys_ref[...] = _pack_rows(jnp.dot(hb.astype(BF16), wd_ref[0], preferred_element_type=F32))


def _experts(xs, blk_exp, n_valid, blk_io, p):
    n_rows, w = xs.shape
    mb = n_rows // blk_exp.shape[0]
    grid_spec = pltpu.PrefetchScalarGridSpec(
        num_scalar_prefetch=3,
        grid=(n_rows // mb,),
        in_specs=[
            pl.BlockSpec((mb, w), lambda i, be, nv, io: (io[i], 0)),
            pl.BlockSpec((1, D_MODEL, EXPERT_DIM), lambda i, be, nv, io: (be[i], 0, 0)),
            pl.BlockSpec((1, D_MODEL, EXPERT_DIM), lambda i, be, nv, io: (be[i], 0, 0)),
            pl.BlockSpec((1, EXPERT_DIM, D_MODEL), lambda i, be, nv, io: (be[i], 0, 0)),
        ],
        out_specs=pl.BlockSpec((mb, w), lambda i, be, nv, io: (io[i], 0)),
    )
    return pl.pallas_call(
        _experts_kernel,
        grid_spec=grid_spec,
        out_shape=jax.ShapeDtypeStruct((n_rows, w), jnp.uint32),
        compiler_params=pltpu.CompilerParams(
            dimension_semantics=("arbitrary",), vmem_limit_bytes=VMEM_LIMIT),
        name="experts",
    )(blk_exp, n_valid, blk_io, xs, p["w_e_gate"], p["w_e_up"], p["w_e_down"])


def _final_kernel(x1_ref, routed_ref, wsg_ref, wsu_ref, wsd_ref, l2g_ref, l2b_ref, out_ref):
    x1 = x1_ref[...]
    xb = x1.astype(BF16)
    hs = (jax.nn.silu(jnp.dot(xb, wsg_ref[...], preferred_element_type=F32))
          * jnp.dot(xb, wsu_ref[...], preferred_element_type=F32))
    shared = jnp.dot(hs.astype(BF16), wsd_ref[...], preferred_element_type=F32)
    out_ref[...] = _layer_norm(ALPHA * x1 + (routed_ref[...] + shared), l2g_ref[...], l2b_ref[...])


def _final(x1, routed, p):
    T = x1.shape[0]
    tm = TOKEN_TILE
    tok = lambda w: pl.BlockSpec((tm, w), lambda i: (i, 0))
    return pl.pallas_call(
        _final_kernel,
        grid=(T // tm,),
        in_specs=[
            tok(D_MODEL), tok(D_MODEL),
            _const_spec((D_MODEL, SHARED_DIM)), _const_spec((D_MODEL, SHARED_DIM)),
            _const_spec((SHARED_DIM, D_MODEL)),
            _const_spec((1, D_MODEL)), _const_spec((1, D_MODEL)),
        ],
        out_specs=tok(D_MODEL),
        out_shape=jax.ShapeDtypeStruct((T, D_MODEL), F32),
        compiler_params=pltpu.CompilerParams(
            dimension_semantics=("parallel",), vmem_limit_bytes=VMEM_LIMIT),
        name="final",
    )(x1, routed, p["w_sh_gate"], p["w_sh_up"], p["w_sh_down"], p["ln2_g"], p["ln2_b"])


def _routing_layout(route, counts, n_tokens):
    mb = EXPERT_BLOCK
    n_blocks = -(-n_tokens * TOP_K // mb) + N_EXPERTS
    idx = route[0:TOP_K].astype(jnp.int32)
    rank = route[8:8 + TOP_K].astype(jnp.int32)
    counts = counts.astype(jnp.int32)
    padded = (counts + mb - 1) // mb * mb
    pad_end = jnp.cumsum(padded)
    pad_start = pad_end - padded
    experts = jnp.arange(N_EXPERTS, dtype=jnp.int32)
    dest = rank + jnp.sum(jnp.where(idx[:, :, None] == experts, pad_start, 0), axis=-1)
    blk_start = jnp.arange(n_blocks, dtype=jnp.int32) * mb
    blk_exp = jnp.minimum(
        jnp.sum((pad_end[None, :] <= blk_start[:, None]).astype(jnp.int32), axis=1), N_EXPERTS - 1)
    valid_end = jnp.sum(jnp.where(blk_exp[:, None] == experts, pad_start + counts, 0), axis=-1)
    n_valid = jnp.clip(valid_end - blk_start, 0, mb).astype(jnp.int32)
    blk_io = jnp.minimum(jnp.arange(n_blocks, dtype=jnp.int32), pad_end[-1] // mb - 1)
    gates = jnp.broadcast_to(
        route[16:16 + TOP_K].T.reshape(n_tokens // SC_COMBINE_WINDOW, SC_COMBINE_WINDOW, TOP_K, 1),
        (n_tokens // SC_COMBINE_WINDOW, SC_COMBINE_WINDOW, TOP_K, SC_LANES)).reshape(
            n_tokens // SC_COMBINE_WINDOW, -1)
    return dest, gates, blk_exp, n_valid, blk_io, n_blocks * mb


def _encode(x, p):
    batch, seq, _ = x.shape
    T = batch * seq
    xt = x.reshape(T, D_MODEL)
    ag, q, gf, gb, iv, so, sgb, gmin = _inproj(xt, p)
    of, ob = _hgrn(q, gf, gb, iv, gmin, batch)
    x1, x1p, route, cnt = _mix(xt, of, ob, so, sgb, ag, p)
    dest, gates, blk_exp, n_valid, blk_io, n_rows = _routing_layout(route, cnt[:, 0], T)
    xs = _sc_dispatch(x1p, dest, n_rows)
    ys = _experts(xs, blk_exp, n_valid, blk_io, p)
    out = _final(x1, _sc_combine(ys, dest, gates), p)
    return out.reshape(batch, seq, D_MODEL)


def _prepare_params(ln_in_g, ln_in_b, w_in, a_ln_g, a_ln_b, a_ws, a_sb, hg_lb_logits, hg_norm_g,
                    w_pa, w_pb, w_o, ln1_g, ln1_b, w_router, router_bias, w_e_gate, w_e_up,
                    w_e_down, w_sh_gate, w_sh_up, w_sh_down, ln2_g, ln2_b):
    l = 0
    row = lambda v: v.reshape(1, -1).astype(F32)
    ws = a_ws[l].astype(BF16)
    wsp = jnp.concatenate([ws[0::2], ws[1::2]], axis=2)
    sbf = jnp.repeat(a_sb[l].astype(F32), A_WIDTH // A_GROUPS, axis=1)
    lb = jnp.cumsum(jax.nn.softmax(hg_lb_logits.astype(F32), axis=1), axis=1)[:, l]
    return dict(
        ln_in_g=row(ln_in_g), ln_in_b=row(ln_in_b), w_in=w_in[l].astype(BF16),
        a_ln_g=row(a_ln_g[l]), a_ln_b=row(a_ln_b[l]), wsp=wsp, sbf=sbf, lb=lb,
        w_pa=w_pa[l].astype(BF16), hg_norm_g=row(hg_norm_g[l]),
        w_pb=w_pb[l].astype(BF16), w_o=w_o[l].astype(BF16),
        ln1_g=row(ln1_g[l]), ln1_b=row(ln1_b[l]),
        w_router=w_router[l].T.astype(BF16),
        router_bias=jnp.broadcast_to(router_bias[l].astype(F32)[:, None], (N_EXPERTS, LANES)),
        w_e_gate=w_e_gate[l].astype(BF16), w_e_up=w_e_up[l].astype(BF16),
        w_e_down=w_e_down[l].astype(BF16),
        w_sh_gate=w_sh_gate[l].astype(BF16), w_sh_up=w_sh_up[l].astype(BF16),
        w_sh_down=w_sh_down[l].astype(BF16),
        ln2_g=row(ln2_g[l]), ln2_b=row(ln2_b[l]),
    )


def kernel(x_prompt, x_sample, ln_in_g, ln_in_b, w_in, a_ln_g, a_ln_b, a_ws, a_sb, hg_lb_logits,
           hg_norm_g, w_pa, w_pb, w_o, ln1_g, ln1_b, w_router, router_bias, w_e_gate, w_e_up,
           w_e_down, w_sh_gate, w_sh_up, w_sh_down, ln2_g, ln2_b):
    p = _prepare_params(ln_in_g, ln_in_b, w_in, a_ln_g, a_ln_b, a_ws, a_sb, hg_lb_logits, hg_norm_g,
                        w_pa, w_pb, w_o, ln1_g, ln1_b, w_router, router_bias, w_e_gate, w_e_up,
                        w_e_down, w_sh_gate, w_sh_up, w_sh_down, ln2_g, ln2_b)
    return _encode(x_prompt, p), _encode(x_sample, p)
```

```python
import functools

import jax
import jax.numpy as jnp
from jax import lax
from jax.experimental import pallas as pl
from jax.experimental.pallas import tpu as pltpu
from jax.experimental.pallas import tpu_sc as plsc

F32 = jnp.float32
BF16 = jnp.bfloat16

D_MODEL = 1024
A_GROUPS = 8
A_WIDTH = 512
A_CHUNK = 128
HG_HEADS = 8
HG_DK = 128
HG_WIDTH = HG_HEADS * HG_DK
N_IN = 2 * A_WIDTH + 5 * HG_WIDTH + 2 * D_MODEL
N_EXPERTS = 64
TOP_K = 6
N_GROUPS = 8
TOPK_GROUPS = 4
GROUP_SIZE = N_EXPERTS // N_GROUPS
EXPERT_DIM = 256
SHARED_DIM = 256
ROUTED_SCALE = 2.5
DEPTH = 1
ALPHA = (2.0 * DEPTH) ** 0.25
LN_EPS = 1e-5
RMS_EPS = 1e-6

LANES = 128
TOKEN_TILE = 512
INPROJ_TILE = 256
HG_CHUNK = 128
HG_CHUNKS_PER_STEP = 2
HG_SAFE_LOGDECAY = 80.0
EXPERT_BLOCK = 1024
SC_CORES = 2
SC_SUBCORES = 16
SC_LANES = 16
SC_WINDOW = 32
SC_COMBINE_WINDOW = 8
VMEM_LIMIT = 56 * 1024 * 1024

_O_U, _O_V, _O_Q, _O_FF, _O_FB, _O_I, _O_G, _O_GA, _O_GB = (
    0, 512, 1024, 2048, 3072, 4096, 5120, 6144, 7168)


def _layer_norm(x, g, b):
    mu = jnp.mean(x, axis=-1, keepdims=True)
    xc = x - mu
    var = jnp.mean(xc * xc, axis=-1, keepdims=True)
    return xc * lax.rsqrt(var + LN_EPS) * g + b


def _bdot(a, b):
    return jnp.dot(a.astype(BF16), b.astype(BF16), preferred_element_type=F32)


def _pack_rows(x):
    w = x.shape[1] // 2
    lo = lax.bitcast_convert_type(x[:, :w].astype(BF16).astype(F32), jnp.uint32)
    hi = lax.bitcast_convert_type(x[:, w:].astype(BF16).astype(F32), jnp.uint32)
    return hi | (lo >> 16)


def _unpack_rows(p):
    lo = lax.bitcast_convert_type(p << 16, F32)
    hi = lax.bitcast_convert_type(p & jnp.uint32(0xFFFF0000), F32)
    return lo, hi


def _const_spec(shape):
    nd = len(shape)
    return pl.BlockSpec(shape, lambda *_: (0,) * nd, pipeline_mode=pl.Buffered(1))


def _inproj_kernel(x_ref, lng_ref, lnb_ref, win_ref, alng_ref, alnb_ref, wsp_ref, sbf_ref, lb_ref,
                   wpa_ref, ag_ref, q_ref, gf_ref, gb_ref, iv_ref, so_ref, sgb_ref, gmin_ref):
    tm = x_ref.shape[0]
    xb = _layer_norm(x_ref[...], lng_ref[...], lnb_ref[...]).astype(BF16)

    def sec(lo, width):
        return jnp.dot(xb, win_ref[:, lo:lo + width], preferred_element_type=F32)

    v = _layer_norm(jax.nn.gelu(sec(_O_V, A_WIDTH)), alng_ref[...], alnb_ref[...]).astype(BF16)
    u = jax.nn.gelu(sec(_O_U, A_WIDTH))

    q_ref[...] = jax.nn.silu(sec(_O_Q, HG_WIDTH)).astype(BF16)
    mins = []
    for d, (off, g_ref) in enumerate(((_O_FF, gf_ref), (_O_FB, gb_ref))):
        lb = lb_ref[d:d + 1, :]
        f = lb + (1.0 - lb) * jax.nn.sigmoid(sec(off, HG_WIDTH))
        g = jnp.log(f)
        g_ref[...] = g
        half = jnp.sum(g.reshape(tm // (HG_CHUNK // 2), HG_CHUNK // 2, HG_WIDTH), axis=1)
        mins.append(jnp.min(half, axis=-1, keepdims=True))
    gmin_ref[0] = jnp.broadcast_to(jnp.concatenate(mins, axis=0), gmin_ref.shape[1:])
    so_ref[...] = jax.nn.silu(sec(_O_G, HG_WIDTH)).astype(BF16)
    sgb_ref[...] = jax.nn.sigmoid(sec(_O_GB, D_MODEL)).astype(BF16)

    lane = lax.broadcasted_iota(jnp.int32, (A_CHUNK, LANES), 1)
    left = lane < (A_WIDTH // A_GROUPS)
    zero = jnp.zeros((A_CHUNK, LANES), BF16)
    chunks = []
    for c in range(tm // A_CHUNK):
        vc = v[c * A_CHUNK:(c + 1) * A_CHUNK]
        cols = []
        for p in range(A_GROUPS // 2):
            vp = vc[:, p * LANES:(p + 1) * LANES]
            rhs = jnp.concatenate([jnp.where(left, vp, zero), jnp.where(left, zero, vp)], axis=0)
            cols.append(jnp.dot(wsp_ref[p], rhs, preferred_element_type=F32))
        chunks.append(jnp.concatenate(cols, axis=1) + sbf_ref[...])
    mixed = jnp.concatenate(chunks, axis=0)
    a = _bdot(u * mixed, wpa_ref[...])
    ag_ref[...] = (jax.nn.sigmoid(sec(_O_GA, D_MODEL)) * a).astype(BF16)
    iv_ref[...] = sec(_O_I, HG_WIDTH).astype(BF16)


def _inproj(x, p):
    T = x.shape[0]
    tm = INPROJ_TILE
    nt = T // tm
    nh = 2 * tm // (HG_CHUNK // 2)
    tok = lambda w: pl.BlockSpec((tm, w), lambda i: (i, 0))
    outs = (
        jax.ShapeDtypeStruct((T, D_MODEL), BF16),
        jax.ShapeDtypeStruct((T, HG_WIDTH), BF16),
        jax.ShapeDtypeStruct((T, HG_WIDTH), F32),
        jax.ShapeDtypeStruct((T, HG_WIDTH), F32),
        jax.ShapeDtypeStruct((T, HG_WIDTH), BF16),
        jax.ShapeDtypeStruct((T, HG_WIDTH), BF16),
        jax.ShapeDtypeStruct((T, D_MODEL), BF16),
        jax.ShapeDtypeStruct((nt, nh, LANES), F32),
    )
    return pl.pallas_call(
        _inproj_kernel,
        grid=(nt,),
        in_specs=[
            tok(D_MODEL),
            _const_spec((1, D_MODEL)), _const_spec((1, D_MODEL)),
            _const_spec((D_MODEL, N_IN)),
            _const_spec((1, A_WIDTH)), _const_spec((1, A_WIDTH)),
            _const_spec((A_GROUPS // 2, A_CHUNK, 2 * A_CHUNK)),
            _const_spec((A_CHUNK, A_WIDTH)),
            _const_spec((2, HG_WIDTH)),
            _const_spec((A_WIDTH, D_MODEL)),
        ],
        out_specs=(tok(D_MODEL), tok(HG_WIDTH), tok(HG_WIDTH), tok(HG_WIDTH), tok(HG_WIDTH),
                   tok(HG_WIDTH), tok(D_MODEL), pl.BlockSpec((1, nh, LANES), lambda i: (i, 0, 0))),
        out_shape=outs,
        compiler_params=pltpu.CompilerParams(
            dimension_semantics=("parallel",), vmem_limit_bytes=VMEM_LIMIT),
        name="inproj",
    )(x, p["ln_in_g"], p["ln_in_b"], p["w_in"], p["a_ln_g"], p["a_ln_b"], p["wsp"], p["sbf"],
      p["lb"], p["w_pa"])


_NT = (((1,), (1,)), ((), ()))
_TN = (((0,), (0,)), ((), ()))


def _hgrn_direction(q_ref, g_ref, v_ref, st_ref, o_ref, tri, mask, fwd):
    C = q_ref.shape[0]
    g = g_ref[...]
    ghi = g.astype(BF16)
    glo = (g - ghi.astype(F32)).astype(BF16)
    b = jnp.dot(jnp.concatenate([tri, tri], axis=1), jnp.concatenate([ghi, glo], axis=0),
                preferred_element_type=F32)
    mid = C // 2 - 1 if fwd else C // 2
    end = C - 1 if fwd else 0
    r = b[mid:mid + 1, :]
    b_end = b[end:end + 1, :]
    qt = q_ref[...].astype(F32) * jnp.exp(b - r)
    kt = (1.0 - jnp.exp(g)) * jnp.exp(r - b)
    qtb = qt.astype(BF16)
    ktb = kt.astype(BF16)
    qhb = (qt * jnp.exp(r)).astype(BF16)
    khb = (kt * jnp.exp(b_end - r)).astype(BF16)
    decay = jnp.exp(b_end)
    v32 = v_ref[...].astype(F32)
    heads = [slice(h * HG_DK, (h + 1) * HG_DK) for h in range(HG_HEADS)]
    zero = jnp.zeros((C, HG_DK), BF16)
    mask2 = jnp.concatenate([mask, mask], axis=1)
    scores = []
    for p in range(HG_HEADS // 2):
        k1, k2 = ktb[:, heads[2 * p]], ktb[:, heads[2 * p + 1]]
        kk = jnp.concatenate([jnp.concatenate([k1, zero], axis=1),
                              jnp.concatenate([zero, k2], axis=1)], axis=0)
        s2 = lax.dot_general(qtb[:, 2 * p * HG_DK:(2 * p + 2) * HG_DK], kk, _NT,
                             preferred_element_type=F32)
        s2 = jnp.where(mask2, s2, 0.0).astype(BF16)
        scores += [s2[:, :C], s2[:, C:]]
    for h, sl in enumerate(heads):
        st = st_ref[h]
        vt = v32[:, sl].T.astype(BF16)
        o_ref[:, sl] = lax.dot_general(
            jnp.concatenate([scores[h], qhb[:, sl]], axis=1),
            jnp.concatenate([vt, st.astype(BF16)], axis=1), _NT,
            preferred_element_type=F32).astype(o_ref.dtype)
        st_ref[h] = st * decay[:, sl] + jnp.dot(vt, khb[:, sl], preferred_element_type=F32)


def _hgrn_direction_stepwise(q_ref, g_ref, v_ref, st_ref, o_ref, q32_ref, v32_ref, o32_ref, fwd):
    C = q_ref.shape[0]
    q32_ref[...] = q_ref[...].astype(F32)
    v32_ref[...] = v_ref[...].astype(F32)
    sub = 8
    rows = lax.broadcasted_iota(jnp.int32, (sub, HG_DK), 0)

    def group(i, carry):
        base = pl.multiple_of((i if fwd else C // sub - 1 - i) * sub, sub)
        f = jnp.exp(g_ref[pl.ds(base, sub), :])
        k = 1.0 - f
        q = q32_ref[pl.ds(base, sub), :]
        v = v32_ref[pl.ds(base, sub), :]
        for h in range(HG_HEADS):
            sl = slice(h * HG_DK, (h + 1) * HG_DK)
            st = st_ref[h]
            out = jnp.zeros((sub, HG_DK), F32)
            for r in (range(sub) if fwd else range(sub - 1, -1, -1)):
                v_t = jnp.where(rows == 0, v[r:r + 1, sl], 0.0).astype(BF16)
                k_t = jnp.broadcast_to(k[r:r + 1, sl], (sub, HG_DK)).astype(BF16)
                st = st * f[r:r + 1, sl] + lax.dot_general(v_t, k_t, _TN, preferred_element_type=F32)
                q_t = jnp.broadcast_to(q[r:r + 1, sl], (sub, HG_DK)).astype(BF16)
                o_t = lax.dot_general(q_t, st.astype(BF16), _NT, preferred_element_type=F32)
                out = jnp.where(rows == r, o_t, out)
            st_ref[h] = st
            o32_ref[pl.ds(base, sub), sl] = out
        return carry

    lax.fori_loop(0, C // sub, group, 0)
    o_ref[...] = o32_ref[...].astype(o_ref.dtype)


def _hgrn_kernel(safe_ref, qf_ref, qb_ref, gf_ref, gb_ref, vf_ref, vb_ref, tril_ref, triu_ref,
                 of_ref, ob_ref, sf_ref, sb_ref, q32_ref, v32_ref, o32_ref):
    b, j = pl.program_id(0), pl.program_id(1)
    ns = pl.num_programs(1)
    n = HG_CHUNKS_PER_STEP
    C = qf_ref.shape[0] // n

    @pl.when(j == 0)
    def _():
        sf_ref[...] = jnp.zeros_like(sf_ref)
        sb_ref[...] = jnp.zeros_like(sb_ref)

    row = lax.broadcasted_iota(jnp.int32, (C, C), 0)
    col = lax.broadcasted_iota(jnp.int32, (C, C), 1)
    safe_f = [safe_ref[0, (b * ns + j) * n + u] != 0 for u in range(n)]
    safe_b = [safe_ref[1, (b * ns + ns - 1 - j) * n + u] != 0 for u in range(n)]
    part = lambda ref, u: ref.at[pl.ds(u * C, C), :]

    def forward(u, stepwise):
        refs = (part(qf_ref, u), part(gf_ref, u), part(vf_ref, u), sf_ref, part(of_ref, u))
        if stepwise:
            _hgrn_direction_stepwise(*refs, q32_ref, v32_ref, o32_ref, True)
        else:
            _hgrn_direction(*refs, tril_ref[...], row >= col, True)

    def backward(u, stepwise):
        refs = (part(qb_ref, u), part(gb_ref, u), part(vb_ref, u), sb_ref, part(ob_ref, u))
        if stepwise:
            _hgrn_direction_stepwise(*refs, q32_ref, v32_ref, o32_ref, False)
        else:
            _hgrn_direction(*refs, triu_ref[...], row <= col, False)

    all_safe = functools.reduce(jnp.logical_and, safe_f + safe_b)

    @pl.when(all_safe)
    def _():
        for u in range(n):
            forward(u, False)
            backward(n - 1 - u, False)

    @pl.when(jnp.logical_not(all_safe))
    def _():
        for u in range(n):
            pl.when(safe_f[u])(functools.partial(forward, u, False))
            pl.when(jnp.logical_not(safe_f[u]))(functools.partial(forward, u, True))
        for u in reversed(range(n)):
            pl.when(safe_b[u])(functools.partial(backward, u, False))
            pl.when(jnp.logical_not(safe_b[u]))(functools.partial(backward, u, True))


def _hgrn(q, gf, gb, iv, gmin, batch):
    T = q.shape[0]
    C = HG_CHUNK
    nc = T // batch // C
    nt, nh, _ = gmin.shape
    halves = gmin[:, :, 0].reshape(nt, 2, nh // 4, 2)
    safe = (jnp.min(halves, axis=-1) > -HG_SAFE_LOGDECAY).astype(jnp.int32)
    safe = safe.transpose(1, 0, 2).reshape(2, T // C)
    n = HG_CHUNKS_PER_STEP
    ns = nc // n
    fwd = pl.BlockSpec((n * C, HG_WIDTH), lambda b, j, s: (b * ns + j, 0))
    bwd = pl.BlockSpec((n * C, HG_WIDTH), lambda b, j, s: (b * ns + ns - 1 - j, 0))
    const = lambda shape: pl.BlockSpec(shape, lambda b, j, s: (0,) * len(shape),
                                       pipeline_mode=pl.Buffered(1))
    row = lax.broadcasted_iota(jnp.int32, (C, C), 0)
    col = lax.broadcasted_iota(jnp.int32, (C, C), 1)
    tril = (row >= col).astype(BF16)
    triu = (row <= col).astype(BF16)
    grid_spec = pltpu.PrefetchScalarGridSpec(
        num_scalar_prefetch=1,
        grid=(batch, ns),
        in_specs=[fwd, bwd, fwd, bwd, fwd, bwd, const((C, C)), const((C, C))],
        out_specs=(fwd, bwd),
        scratch_shapes=[pltpu.VMEM((HG_HEADS, HG_DK, HG_DK), F32),
                        pltpu.VMEM((HG_HEADS, HG_DK, HG_DK), F32),
                        pltpu.VMEM((C, HG_WIDTH), F32), pltpu.VMEM((C, HG_WIDTH), F32),
                        pltpu.VMEM((C, HG_WIDTH), F32)],
    )
    return pl.pallas_call(
        _hgrn_kernel,
        grid_spec=grid_spec,
        out_shape=(jax.ShapeDtypeStruct((T, HG_WIDTH), BF16), jax.ShapeDtypeStruct((T, HG_WIDTH), BF16)),
        compiler_params=pltpu.CompilerParams(
            dimension_semantics=("parallel", "arbitrary"), vmem_limit_bytes=VMEM_LIMIT),
        name="hgrn",
    )(safe, q, q, gf, gb, iv, iv, tril, triu)


def _mix_kernel(x_ref, lng_ref, lnb_ref, of_ref, ob_ref, so_ref, sgb_ref, ag_ref, ng_ref, wpb_ref,
                wo_ref, l1g_ref, l1b_ref, wr_ref, rb_ref, tri_ref, x1_ref, x1p_ref, route_ref,
                cnt_ref, carry_ref):
    tm = x_ref.shape[0]

    @pl.when(pl.program_id(0) == 0)
    def _():
        carry_ref[...] = jnp.zeros_like(carry_ref)

    o = of_ref[...].astype(F32) + ob_ref[...].astype(F32)
    heads = []
    for h in range(HG_HEADS):
        oh = o[:, h * HG_DK:(h + 1) * HG_DK]
        heads.append(oh * lax.rsqrt(jnp.mean(oh * oh, axis=-1, keepdims=True) + RMS_EPS))
    rn = jnp.concatenate(heads, axis=1) * ng_ref[...] * so_ref[...].astype(F32)
    r = _bdot(rn, wpb_ref[...])
    mixed = ag_ref[...].astype(F32) + sgb_ref[...].astype(F32) * r
    y = _bdot(mixed, wo_ref[...])
    xn = _layer_norm(x_ref[...], lng_ref[...], lnb_ref[...])
    x1 = _layer_norm(ALPHA * xn + y, l1g_ref[...], l1b_ref[...])
    x1_ref[...] = x1
    x1p_ref[...] = _pack_rows(x1)

    neg = jnp.float32(-jnp.inf)
    reps = tm // LANES
    scores = jax.nn.sigmoid(lax.dot_general(wr_ref[...], x1.astype(BF16), _NT,
                                            preferred_element_type=F32))
    biased = (scores + jnp.concatenate([rb_ref[...]] * reps, axis=1)).reshape(
        N_GROUPS, GROUP_SIZE, tm)
    sub = lax.broadcasted_iota(jnp.int32, biased.shape, 1).astype(F32)
    m1 = jnp.max(biased, axis=1, keepdims=True)
    first = jnp.min(jnp.where(biased == m1, sub, float(GROUP_SIZE)), axis=1, keepdims=True)
    m2 = jnp.max(jnp.where(sub == first, neg, biased), axis=1, keepdims=True)
    gs = (m1 + m2).reshape(N_GROUPS, tm)
    grp = lax.broadcasted_iota(jnp.int32, (N_GROUPS, tm), 0)
    ahead = jnp.zeros((N_GROUPS, tm), F32)
    for d in range(1, N_GROUPS):
        other = pltpu.roll(gs, d, 0)
        tie = jnp.where(grp >= d, 1.0, 0.0)
        ahead = ahead + jnp.where(other > gs, 1.0, jnp.where(other == gs, tie, 0.0))
    keep = (ahead < TOPK_GROUPS).reshape(N_GROUPS, 1, tm)
    allowed = jnp.where(keep, biased, neg).reshape(N_EXPERTS, tm)
    row = lax.broadcasted_iota(jnp.int32, (N_EXPERTS, tm), 0).astype(F32)
    sel = jnp.zeros((N_EXPERTS, tm), F32)
    picks = []
    for _ in range(TOP_K):
        m = jnp.max(allowed, axis=0, keepdims=True)
        first = jnp.min(jnp.where(allowed == m, row, float(N_EXPERTS)), axis=0, keepdims=True)
        hit = row == first
        picks.append((first, hit, jnp.sum(jnp.where(hit, scores, 0.0), axis=0, keepdims=True)))
        sel = jnp.where(hit, 1.0, sel)
        allowed = jnp.where(hit, neg, allowed)
    wsum = picks[0][2]
    for pk in picks[1:]:
        wsum = wsum + pk[2]
    selb = sel.astype(BF16)
    carry = carry_ref[...]
    before = (jnp.dot(selb, tri_ref[...], preferred_element_type=F32)
              + jnp.concatenate([carry] * reps, axis=1))
    total = carry + jnp.dot(selb, jnp.ones((tm, LANES), BF16), preferred_element_type=F32)
    carry_ref[...] = total
    cnt_ref[...] = total
    blank = [jnp.zeros((1, tm), F32)] * (8 - TOP_K)
    route_ref[...] = jnp.concatenate(
        [pk[0] for pk in picks] + blank
        + [jnp.sum(jnp.where(pk[1], before, 0.0), axis=0, keepdims=True) for pk in picks] + blank
        + [pk[2] / wsum * ROUTED_SCALE for pk in picks] + blank, axis=0)


def _mix(x, of, ob, so, sgb, ag, p):
    T = x.shape[0]
    tm = TOKEN_TILE
    tok = lambda w: pl.BlockSpec((tm, w), lambda i: (i, 0))
    row = lax.broadcasted_iota(jnp.int32, (tm, tm), 0)
    col = lax.broadcasted_iota(jnp.int32, (tm, tm), 1)
    tri = (row < col).astype(BF16)
    return pl.pallas_call(
        _mix_kernel,
        grid=(T // tm,),
        in_specs=[
            tok(D_MODEL), _const_spec((1, D_MODEL)), _const_spec((1, D_MODEL)),
            tok(HG_WIDTH), tok(HG_WIDTH), tok(HG_WIDTH), tok(D_MODEL), tok(D_MODEL),
            _const_spec((1, HG_WIDTH)),
            _const_spec((HG_WIDTH, D_MODEL)), _const_spec((D_MODEL, D_MODEL)),
            _const_spec((1, D_MODEL)), _const_spec((1, D_MODEL)),
            _const_spec((N_EXPERTS, D_MODEL)), _const_spec((N_EXPERTS, LANES)),
            _const_spec((tm, tm)),
        ],
        out_specs=(tok(D_MODEL), tok(D_MODEL // 2), pl.BlockSpec((24, tm), lambda i: (0, i)),
                   pl.BlockSpec((N_EXPERTS, LANES), lambda i: (0, 0))),
        out_shape=(jax.ShapeDtypeStruct((T, D_MODEL), F32),
                   jax.ShapeDtypeStruct((T, D_MODEL // 2), jnp.uint32),
                   jax.ShapeDtypeStruct((24, T), F32),
                   jax.ShapeDtypeStruct((N_EXPERTS, LANES), F32)),
        scratch_shapes=[pltpu.VMEM((N_EXPERTS, LANES), F32)],
        compiler_params=pltpu.CompilerParams(
            dimension_semantics=("arbitrary",), vmem_limit_bytes=VMEM_LIMIT),
        name="mix",
    )(x, p["ln_in_g"], p["ln_in_b"], of, ob, so, sgb, ag, p["hg_norm_g"], p["w_pb"], p["w_o"],
      p["ln1_g"], p["ln1_b"], p["w_router"], p["router_bias"], tri)


def _sc_mesh():
    return plsc.VectorSubcoreMesh(core_axis_name="c", subcore_axis_name="s",
                                  num_cores=SC_CORES, num_subcores=SC_SUBCORES)


def _sc_worker():
    return lax.axis_index("s") * SC_CORES + lax.axis_index("c")


def _sc_dispatch(x1p, dest, n_rows):
    T, w = x1p.shape
    W = SC_WINDOW
    per_worker = T // W // (SC_CORES * SC_SUBCORES)

    def body(x_hbm, d_hbm, o_hbm, rows_v, idx_v, sem):
        first = _sc_worker() * per_worker
        for k in range(TOP_K):
            pltpu.sync_copy(d_hbm.at[k, pl.ds(first, per_worker)], idx_v.at[k])

        @pl.loop(0, per_worker)
        def _(j):
            pltpu.sync_copy(x_hbm.at[pl.ds((first + j) * W, W)], rows_v)
            copies = [pltpu.async_copy(rows_v, o_hbm.at[idx_v.at[k, j]], sem) for k in range(TOP_K)]
            for c in copies:
                c.wait()

    return pl.kernel(
        body,
        out_type=jax.ShapeDtypeStruct((n_rows, w), jnp.uint32),
        mesh=_sc_mesh(),
        scratch_types=[pltpu.VMEM((W, w), jnp.uint32), pltpu.VMEM((TOP_K, per_worker, W), jnp.int32),
                       pltpu.SemaphoreType.DMA],
        name="sc_dispatch",
    )(x1p, dest.reshape(TOP_K, T // W, W))


def _sc_combine(ys, dest, gates):
    T = dest.shape[1]
    W = SC_COMBINE_WINDOW
    w = ys.shape[1]
    per_worker = T // W // (SC_CORES * SC_SUBCORES)
    assert per_worker % 2 == 0 and per_worker * W * SC_CORES * SC_SUBCORES == T

    def body(y_hbm, d_hbm, g_hbm, o_hbm, idx_v, *scratch):
        slots = [dict(rows=scratch[s], gate=scratch[2 + s], out=scratch[4 + s], gsem=scratch[6 + s],
                      wsem=scratch[8 + s]) for s in range(2)]
        first = _sc_worker() * per_worker
        for k in range(TOP_K):
            pltpu.sync_copy(d_hbm.at[k, pl.ds(first * W, per_worker * W)], idx_v.at[k])

        def loads(b, win):
            i = win - first
            return [pltpu.make_async_copy(g_hbm.at[win], b["gate"], b["gsem"])] + [
                pltpu.make_async_copy(y_hbm.at[idx_v.at[k, pl.ds(i * W, W)]],
                                      b["rows"].at[k], b["gsem"]) for k in range(TOP_K)]

        def write_back(b, win):
            return pltpu.make_async_copy(b["out"], o_hbm.at[pl.ds(win * W, W)], b["wsem"])

        def fetch(b, win):
            for c in loads(b, win):
                c.start()

        def reduce_rows(b):
            rows, gate_v, out = b["rows"], b["gate"], b["out"]

            @pl.loop(0, W)
            def _(j):
                gate = [gate_v[pl.ds((j * TOP_K + k) * SC_LANES, SC_LANES)] for k in range(TOP_K)]

                @plsc.parallel_loop(0, w, step=SC_LANES, unroll=4)
                def _(col):
                    lo, hi = [], []
                    for k in range(TOP_K):
                        p = rows[k, j, pl.ds(col, SC_LANES)]
                        lo.append(plsc.bitcast(p << 16, F32) * gate[k])
                        hi.append(plsc.bitcast(p & jnp.uint32(0xFFFF0000), F32) * gate[k])
                    out[j, pl.ds(col, SC_LANES)] = (lo[0] + lo[1]) + (lo[2] + lo[3]) + (lo[4] + lo[5])
                    out[j, pl.ds(w + col, SC_LANES)] = (hi[0] + hi[1]) + (hi[2] + hi[3]) + (hi[4] + hi[5])

        def process(b, win, not_first):
            for c in loads(b, win):
                c.wait()

            @pl.when(not_first)
            def _():
                write_back(b, win).wait()

            reduce_rows(b)
            write_back(b, win).start()

        fetch(slots[0], first)

        @pl.loop(0, per_worker, step=2)
        def _(i):
            win = first + i
            fetch(slots[1], win + 1)
            process(slots[0], win, i > 0)

            @pl.when(i + 2 < per_worker)
            def _():
                fetch(slots[0], win + 2)

            process(slots[1], win + 1, i > 0)

        for b in slots:
            write_back(b, first).wait()

    slot_types = ([pltpu.VMEM((TOP_K, per_worker * W), jnp.int32)]
                  + [pltpu.VMEM((TOP_K, W, w), jnp.uint32)] * 2
                  + [pltpu.VMEM((W * TOP_K * SC_LANES,), F32)] * 2
                  + [pltpu.VMEM((W, 2 * w), F32)] * 2
                  + [pltpu.SemaphoreType.DMA] * 4)
    return pl.kernel(
        body,
        out_type=jax.ShapeDtypeStruct((T, 2 * w), F32),
        mesh=_sc_mesh(),
        scratch_types=slot_types,
        compiler_params=pltpu.CompilerParams(needs_layout_passes=False),
        name="sc_combine",
    )(ys, dest, gates)


def _experts_kernel(be_ref, nv_ref, io_ref, xs_ref, wg_ref, wu_ref, wd_ref, ys_ref, *cast_refs):
    del io_ref
    i = pl.program_id(0)
    n_valid = nv_ref[i]
    mb, w = xs_ref.shape
    if cast_refs:
        @pl.when((i == 0) | (be_ref[i] != be_ref[jnp.maximum(i, 1) - 1]))
        def _():
            for src, dst in zip((wg_ref, wu_ref, wd_ref), cast_refs):
                dst[...] = src[...].astype(BF16)
        wg_ref, wu_ref, wd_ref = cast_refs

    @pl.when(n_valid > 0)
    def _():
        keep = lax.broadcasted_iota(jnp.int32, (mb, w), 0) < n_valid
        lo, hi = _unpack_rows(jnp.where(keep, xs_ref[...], jnp.uint32(0)))
        lo, hi = lo.astype(BF16), hi.astype(BF16)

        def proj(w_ref):
            return (jnp.dot(lo, w_ref[0, :w, :], preferred_element_type=F32)
                    + jnp.dot(hi, w_ref[0, w:, :], preferred_element_type=F32))

        hb = jax.nn.silu(proj(wg_ref)) * proj(wu_ref)
        ys_ref[...] = _pack_rows(jnp.dot(hb.astype(BF16), wd_ref[0], preferred_element_type=F32))


def _experts(xs, blk_exp, n_valid, blk_io, weights):
    n_rows, w = xs.shape
    mb = n_rows // blk_exp.shape[0]
    cast = weights[0].dtype != BF16
    by_expert = lambda shape: pl.BlockSpec((1,) + shape, lambda i, be, nv, io: (be[i], 0, 0))
    w_specs = [by_expert((D_MODEL, EXPERT_DIM)), by_expert((D_MODEL, EXPERT_DIM)),
               by_expert((EXPERT_DIM, D_MODEL))]
    ys_spec = pl.BlockSpec((mb, w), lambda i, be, nv, io: (io[i], 0))
    ys_shape = jax.ShapeDtypeStruct((n_rows, w), jnp.uint32)
    grid_spec = pltpu.PrefetchScalarGridSpec(
        num_scalar_prefetch=3,
        grid=(n_rows // mb,),
        in_specs=[pl.BlockSpec((mb, w), lambda i, be, nv, io: (io[i], 0))] + w_specs,
        out_specs=(ys_spec, *w_specs) if cast else ys_spec,
    )
    out = pl.pallas_call(
        _experts_kernel,
        grid_spec=grid_spec,
        out_shape=(ys_shape, *[jax.ShapeDtypeStruct(x.shape, BF16) for x in weights]) if cast else ys_shape,
        compiler_params=pltpu.CompilerParams(
            dimension_semantics=("arbitrary",), vmem_limit_bytes=VMEM_LIMIT),
        name="experts",
    )(blk_exp, n_valid, blk_io, xs, *weights)
    return (out[0], tuple(out[1:])) if cast else (out, weights)


def _final_kernel(x1_ref, routed_ref, wsg_ref, wsu_ref, wsd_ref, l2g_ref, l2b_ref, out_ref):
    x1 = x1_ref[...]
    xb = x1.astype(BF16)
    hs = (jax.nn.silu(jnp.dot(xb, wsg_ref[...], preferred_element_type=F32))
          * jnp.dot(xb, wsu_ref[...], preferred_element_type=F32))
    shared = jnp.dot(hs.astype(BF16), wsd_ref[...], preferred_element_type=F32)
    out_ref[...] = _layer_norm(ALPHA * x1 + (routed_ref[...] + shared), l2g_ref[...], l2b_ref[...])


def _final(x1, routed, p):
    T = x1.shape[0]
    tm = TOKEN_TILE
    tok = lambda w: pl.BlockSpec((tm, w), lambda i: (i, 0))
    return pl.pallas_call(
        _final_kernel,
        grid=(T // tm,),
        in_specs=[
            tok(D_MODEL), tok(D_MODEL),
            _const_spec((D_MODEL, SHARED_DIM)), _const_spec((D_MODEL, SHARED_DIM)),
            _const_spec((SHARED_DIM, D_MODEL)),
            _const_spec((1, D_MODEL)), _const_spec((1, D_MODEL)),
        ],
        out_specs=tok(D_MODEL),
        out_shape=jax.ShapeDtypeStruct((T, D_MODEL), F32),
        compiler_params=pltpu.CompilerParams(
            dimension_semantics=("parallel",), vmem_limit_bytes=VMEM_LIMIT),
        name="final",
    )(x1, routed, p["w_sh_gate"], p["w_sh_up"], p["w_sh_down"], p["ln2_g"], p["ln2_b"])


def _routing_layout(route, counts, n_tokens):
    mb = EXPERT_BLOCK
    n_blocks = -(-n_tokens * TOP_K // mb) + N_EXPERTS
    idx = route[0:TOP_K].astype(jnp.int32)
    rank = route[8:8 + TOP_K].astype(jnp.int32)
    counts = counts.astype(jnp.int32)
    padded = jnp.maximum((counts + mb - 1) // mb, 1) * mb
    pad_end = jnp.cumsum(padded)
    pad_start = pad_end - padded
    experts = jnp.arange(N_EXPERTS, dtype=jnp.int32)
    dest = rank + jnp.sum(jnp.where(idx[:, :, None] == experts, pad_start, 0), axis=-1)
    blk_start = jnp.arange(n_blocks, dtype=jnp.int32) * mb
    blk_exp = jnp.minimum(
        jnp.sum((pad_end[None, :] <= blk_start[:, None]).astype(jnp.int32), axis=1), N_EXPERTS - 1)
    valid_end = jnp.sum(jnp.where(blk_exp[:, None] == experts, pad_start + counts, 0), axis=-1)
    n_valid = jnp.clip(valid_end - blk_start, 0, mb).astype(jnp.int32)
    blk_io = jnp.minimum(jnp.arange(n_blocks, dtype=jnp.int32), pad_end[-1] // mb - 1)
    gates = jnp.repeat(route[16:16 + TOP_K], SC_LANES, axis=0).T.reshape(
        n_tokens // SC_COMBINE_WINDOW, -1)
    return dest, gates, blk_exp, n_valid, blk_io, n_blocks * mb


def _encode(x, p, expert_weights):
    batch, seq, _ = x.shape
    T = batch * seq
    xt = x.reshape(T, D_MODEL)
    ag, q, gf, gb, iv, so, sgb, gmin = _inproj(xt, p)
    of, ob = _hgrn(q, gf, gb, iv, gmin, batch)
    x1, x1p, route, cnt = _mix(xt, of, ob, so, sgb, ag, p)
    dest, gates, blk_exp, n_valid, blk_io, n_rows = _routing_layout(route, cnt[:, 0], T)
    xs = _sc_dispatch(x1p, dest, n_rows)
    ys, expert_weights = _experts(xs, blk_exp, n_valid, blk_io, expert_weights)
    out = _final(x1, _sc_combine(ys, dest, gates), p)
    return out.reshape(batch, seq, D_MODEL), expert_weights


def _prepare_params(ln_in_g, ln_in_b, w_in, a_ln_g, a_ln_b, a_ws, a_sb, hg_lb_logits, hg_norm_g,
                    w_pa, w_pb, w_o, ln1_g, ln1_b, w_router, router_bias, w_sh_gate, w_sh_up,
                    w_sh_down, ln2_g, ln2_b):
    l = 0
    row = lambda v: v.reshape(1, -1).astype(F32)
    ws = a_ws[l].astype(BF16)
    wsp = jnp.concatenate([ws[0::2], ws[1::2]], axis=2)
    sbf = jnp.repeat(a_sb[l].astype(F32), A_WIDTH // A_GROUPS, axis=1)
    lb = jnp.cumsum(jax.nn.softmax(hg_lb_logits.astype(F32), axis=1), axis=1)[:, l]
    return dict(
        ln_in_g=row(ln_in_g), ln_in_b=row(ln_in_b), w_in=w_in[l].astype(BF16),
        a_ln_g=row(a_ln_g[l]), a_ln_b=row(a_ln_b[l]), wsp=wsp, sbf=sbf, lb=lb,
        w_pa=w_pa[l].astype(BF16), hg_norm_g=row(hg_norm_g[l]),
        w_pb=w_pb[l].astype(BF16), w_o=w_o[l].astype(BF16),
        ln1_g=row(ln1_g[l]), ln1_b=row(ln1_b[l]),
        w_router=w_router[l].T.astype(BF16),
        router_bias=jnp.broadcast_to(router_bias[l].astype(F32)[:, None], (N_EXPERTS, LANES)),
        w_sh_gate=w_sh_gate[l].astype(BF16), w_sh_up=w_sh_up[l].astype(BF16),
        w_sh_down=w_sh_down[l].astype(BF16),
        ln2_g=row(ln2_g[l]), ln2_b=row(ln2_b[l]),
    )


def kernel(x_prompt, x_sample, ln_in_g, ln_in_b, w_in, a_ln_g, a_ln_b, a_ws, a_sb, hg_lb_logits,
           hg_norm_g, w_pa, w_pb, w_o, ln1_g, ln1_b, w_router, router_bias, w_e_gate, w_e_up,
           w_e_down, w_sh_gate, w_sh_up, w_sh_down, ln2_g, ln2_b):
    p = _prepare_params(ln_in_g, ln_in_b, w_in, a_ln_g, a_ln_b, a_ws, a_sb, hg_lb_logits, hg_norm_g,
                        w_pa, w_pb, w_o, ln1_g, ln1_b, w_router, router_bias, w_sh_gate, w_sh_up,
                        w_sh_down, ln2_g, ln2_b)
    y_prompt, expert_weights = _encode(x_prompt, p, (w_e_gate[0], w_e_up[0], w_e_down[0]))
    y_sample, _ = _encode(x_sample, p, expert_weights)
    return y_prompt, y_sample
```

```python
import functools

import jax
import jax.numpy as jnp
from jax import lax
from jax.experimental import pallas as pl
from jax.experimental.pallas import tpu as pltpu
from jax.experimental.pallas import tpu_sc as plsc

F32 = jnp.float32
BF16 = jnp.bfloat16

D_MODEL = 1024
A_GROUPS = 8
A_WIDTH = 512
A_CHUNK = 128
HG_HEADS = 8
HG_DK = 128
HG_WIDTH = HG_HEADS * HG_DK
N_IN = 2 * A_WIDTH + 5 * HG_WIDTH + 2 * D_MODEL
N_EXPERTS = 64
TOP_K = 6
N_GROUPS = 8
TOPK_GROUPS = 4
GROUP_SIZE = N_EXPERTS // N_GROUPS
EXPERT_DIM = 256
SHARED_DIM = 256
ROUTED_SCALE = 2.5
DEPTH = 1
ALPHA = (2.0 * DEPTH) ** 0.25
LN_EPS = 1e-5
RMS_EPS = 1e-6

LANES = 128
SUBLANES = 8
ROUTE_ROWS = 3 * SUBLANES
TOKEN_TILE = 512
INPROJ_TILE = 256
HG_CHUNK = 128
HG_CHUNKS_PER_STEP = 2
HG_SAFE_LOGDECAY = 80.0
EXPERT_BLOCK = 1024
SC_CORES = 2
SC_SUBCORES = 16
SC_LANES = 16
SC_WINDOW = 32
SC_COMBINE_WINDOW = 8
VMEM_LIMIT = 56 * 1024 * 1024

_O_U, _O_V, _O_Q, _O_FF, _O_FB, _O_I, _O_G, _O_GA, _O_GB = (
    0, 512, 1024, 2048, 3072, 4096, 5120, 6144, 7168)


def _layer_norm(x, g, b):
    mu = jnp.mean(x, axis=-1, keepdims=True)
    xc = x - mu
    var = jnp.mean(xc * xc, axis=-1, keepdims=True)
    return xc * lax.rsqrt(var + LN_EPS) * g + b


def _bdot(a, b):
    return jnp.dot(a.astype(BF16), b.astype(BF16), preferred_element_type=F32)


def _pack_rows(x):
    w = x.shape[1] // 2
    lo = lax.bitcast_convert_type(x[:, :w].astype(BF16).astype(F32), jnp.uint32)
    hi = lax.bitcast_convert_type(x[:, w:].astype(BF16).astype(F32), jnp.uint32)
    return hi | (lo >> 16)


def _unpack_rows(p):
    lo = lax.bitcast_convert_type(p << 16, F32)
    hi = lax.bitcast_convert_type(p & jnp.uint32(0xFFFF0000), F32)
    return lo, hi


def _const_spec(shape):
    nd = len(shape)
    return pl.BlockSpec(shape, lambda *_: (0,) * nd, pipeline_mode=pl.Buffered(1))


def _inproj_kernel(x_ref, lng_ref, lnb_ref, win_ref, alng_ref, alnb_ref, wsp_ref, sbf_ref, lb_ref,
                   wpa_ref, ag_ref, q_ref, gf_ref, gb_ref, iv_ref, so_ref, sgb_ref, gmin_ref):
    tm = x_ref.shape[0]
    xb = _layer_norm(x_ref[...], lng_ref[...], lnb_ref[...]).astype(BF16)

    def sec(lo, width):
        return jnp.dot(xb, win_ref[:, lo:lo + width], preferred_element_type=F32)

    v = _layer_norm(jax.nn.gelu(sec(_O_V, A_WIDTH)), alng_ref[...], alnb_ref[...]).astype(BF16)
    u = jax.nn.gelu(sec(_O_U, A_WIDTH))

    q_ref[...] = jax.nn.silu(sec(_O_Q, HG_WIDTH)).astype(BF16)
    mins = []
    for d, (off, g_ref) in enumerate(((_O_FF, gf_ref), (_O_FB, gb_ref))):
        lb = lb_ref[d:d + 1, :]
        f = lb + (1.0 - lb) * jax.nn.sigmoid(sec(off, HG_WIDTH))
        g = jnp.log(f)
        g_ref[...] = g
        half = jnp.sum(g.reshape(tm // (HG_CHUNK // 2), HG_CHUNK // 2, HG_WIDTH), axis=1)
        mins.append(jnp.min(half, axis=-1, keepdims=True))
    gmin_ref[0] = jnp.broadcast_to(jnp.concatenate(mins, axis=0), gmin_ref.shape[1:])
    so_ref[...] = jax.nn.silu(sec(_O_G, HG_WIDTH)).astype(BF16)
    sgb_ref[...] = jax.nn.sigmoid(sec(_O_GB, D_MODEL)).astype(BF16)

    lane = lax.broadcasted_iota(jnp.int32, (A_CHUNK, LANES), 1)
    left = lane < (A_WIDTH // A_GROUPS)
    zero = jnp.zeros((A_CHUNK, LANES), BF16)
    chunks = []
    for c in range(tm // A_CHUNK):
        vc = v[c * A_CHUNK:(c + 1) * A_CHUNK]
        cols = []
        for p in range(A_GROUPS // 2):
            vp = vc[:, p * LANES:(p + 1) * LANES]
            rhs = jnp.concatenate([jnp.where(left, vp, zero), jnp.where(left, zero, vp)], axis=0)
            cols.append(jnp.dot(wsp_ref[p], rhs, preferred_element_type=F32))
        chunks.append(jnp.concatenate(cols, axis=1) + sbf_ref[...])
    mixed = jnp.concatenate(chunks, axis=0)
    a = _bdot(u * mixed, wpa_ref[...])
    ag_ref[...] = (jax.nn.sigmoid(sec(_O_GA, D_MODEL)) * a).astype(BF16)
    iv_ref[...] = sec(_O_I, HG_WIDTH).astype(BF16)


def _inproj(x, p):
    T = x.shape[0]
    tm = INPROJ_TILE
    nt = T // tm
    nh = 2 * tm // (HG_CHUNK // 2)
    tok = lambda w: pl.BlockSpec((tm, w), lambda i: (i, 0))
    outs = (
        jax.ShapeDtypeStruct((T, D_MODEL), BF16),
        jax.ShapeDtypeStruct((T, HG_WIDTH), BF16),
        jax.ShapeDtypeStruct((T, HG_WIDTH), F32),
        jax.ShapeDtypeStruct((T, HG_WIDTH), F32),
        jax.ShapeDtypeStruct((T, HG_WIDTH), BF16),
        jax.ShapeDtypeStruct((T, HG_WIDTH), BF16),
        jax.ShapeDtypeStruct((T, D_MODEL), BF16),
        jax.ShapeDtypeStruct((nt, nh, LANES), F32),
    )
    return pl.pallas_call(
        _inproj_kernel,
        grid=(nt,),
        in_specs=[
            tok(D_MODEL),
            _const_spec((1, D_MODEL)), _const_spec((1, D_MODEL)),
            _const_spec((D_MODEL, N_IN)),
            _const_spec((1, A_WIDTH)), _const_spec((1, A_WIDTH)),
            _const_spec((A_GROUPS // 2, A_CHUNK, 2 * A_CHUNK)),
            _const_spec((A_CHUNK, A_WIDTH)),
            _const_spec((2, HG_WIDTH)),
            _const_spec((A_WIDTH, D_MODEL)),
        ],
        out_specs=(tok(D_MODEL), tok(HG_WIDTH), tok(HG_WIDTH), tok(HG_WIDTH), tok(HG_WIDTH),
                   tok(HG_WIDTH), tok(D_MODEL), pl.BlockSpec((1, nh, LANES), lambda i: (i, 0, 0))),
        out_shape=outs,
        compiler_params=pltpu.CompilerParams(
            dimension_semantics=("parallel",), vmem_limit_bytes=VMEM_LIMIT),
        name="inproj",
    )(x, p["ln_in_g"], p["ln_in_b"], p["w_in"], p["a_ln_g"], p["a_ln_b"], p["wsp"], p["sbf"],
      p["lb"], p["w_pa"])


_NT = (((1,), (1,)), ((), ()))
_TN = (((0,), (0,)), ((), ()))


def _hgrn_direction(q_ref, g_ref, v_ref, st_ref, o_ref, tri, mask, fwd):
    C = q_ref.shape[0]
    g = g_ref[...]
    ghi = g.astype(BF16)
    glo = (g - ghi.astype(F32)).astype(BF16)
    b = jnp.dot(jnp.concatenate([tri, tri], axis=1), jnp.concatenate([ghi, glo], axis=0),
                preferred_element_type=F32)
    mid = C // 2 - 1 if fwd else C // 2
    end = C - 1 if fwd else 0
    r = b[mid:mid + 1, :]
    b_end = b[end:end + 1, :]
    qt = q_ref[...].astype(F32) * jnp.exp(b - r)
    kt = (1.0 - jnp.exp(g)) * jnp.exp(r - b)
    qtb = qt.astype(BF16)
    ktb = kt.astype(BF16)
    qhb = (qt * jnp.exp(r)).astype(BF16)
    khb = (kt * jnp.exp(b_end - r)).astype(BF16)
    decay = jnp.exp(b_end)
    v = v_ref[...]
    heads =[slice(h * HG_DK, (h + 1) * HG_DK) for h in range(HG_HEADS)]
    zero = jnp.zeros((C, HG_DK), BF16)
    mask2 = jnp.concatenate([mask, mask], axis=1)
    scores = []
    for p in range(HG_HEADS // 2):
        k1, k2 = ktb[:, heads[2 * p]], ktb[:, heads[2 * p + 1]]
        kk = jnp.concatenate([jnp.concatenate([k1, zero], axis=1),
                              jnp.concatenate([zero, k2], axis=1)], axis=0)
        s2 = lax.dot_general(qtb[:, 2 * p * HG_DK:(2 * p + 2) * HG_DK], kk, _NT,
                             preferred_element_type=F32)
        s2 = jnp.where(mask2, s2, 0.0).astype(BF16)
        scores += [s2[:, :C], s2[:, C:]]
    for h, sl in enumerate(heads):
        st = st_ref[h]
        vt = v[:, sl].T
        o_ref[:, sl] = lax.dot_general(
            jnp.concatenate([scores[h], qhb[:, sl]], axis=1),
            jnp.concatenate([vt, st.astype(BF16)], axis=1), _NT,
            preferred_element_type=F32).astype(o_ref.dtype)
        st_ref[h] = st * decay[:, sl] + jnp.dot(vt, khb[:, sl], preferred_element_type=F32)


def _hgrn_direction_stepwise(q_ref, g_ref, v_ref, st_ref, o_ref, q32_ref, v32_ref, o32_ref, fwd):
    C = q_ref.shape[0]
    q32_ref[...] = q_ref[...].astype(F32)
    v32_ref[...] = v_ref[...].astype(F32)
    sub = SUBLANES
    rows = lax.broadcasted_iota(jnp.int32, (sub, HG_DK), 0)

    def group(i, carry):
        base = pl.multiple_of((i if fwd else C // sub - 1 - i) * sub, sub)
        f = jnp.exp(g_ref[pl.ds(base, sub), :])
        k = 1.0 - f
        q = q32_ref[pl.ds(base, sub), :]
        v = v32_ref[pl.ds(base, sub), :]
        for h in range(HG_HEADS):
            sl = slice(h * HG_DK, (h + 1) * HG_DK)
            st = st_ref[h]
            out = jnp.zeros((sub, HG_DK), F32)
            for r in (range(sub) if fwd else range(sub - 1, -1, -1)):
                v_t = jnp.where(rows == 0, v[r:r + 1, sl], 0.0).astype(BF16)
                k_t = jnp.broadcast_to(k[r:r + 1, sl], (sub, HG_DK)).astype(BF16)
                st = st * f[r:r + 1, sl] + lax.dot_general(v_t, k_t, _TN, preferred_element_type=F32)
                q_t = jnp.broadcast_to(q[r:r + 1, sl], (sub, HG_DK)).astype(BF16)
                o_t = lax.dot_general(q_t, st.astype(BF16), _NT, preferred_element_type=F32)
                out = jnp.where(rows == r, o_t, out)
            st_ref[h] = st
            o32_ref[pl.ds(base, sub), sl] = out
        return carry

    lax.fori_loop(0, C // sub, group, 0)
    o_ref[...] = o32_ref[...].astype(o_ref.dtype)


def _hgrn_kernel(safe_ref, qf_ref, qb_ref, gf_ref, gb_ref, vf_ref, vb_ref, tril_ref, triu_ref,
                 of_ref, ob_ref, sf_ref, sb_ref, q32_ref, v32_ref, o32_ref):
    b, j = pl.program_id(0), pl.program_id(1)
    ns = pl.num_programs(1)
    n = HG_CHUNKS_PER_STEP
    C = qf_ref.shape[0] // n

    @pl.when(j == 0)
    def _():
        sf_ref[...] = jnp.zeros_like(sf_ref)
        sb_ref[...] = jnp.zeros_like(sb_ref)

    row = lax.broadcasted_iota(jnp.int32, (C, C), 0)
    col = lax.broadcasted_iota(jnp.int32, (C, C), 1)
    safe_f = [safe_ref[0, (b * ns + j) * n + u] != 0 for u in range(n)]
    safe_b = [safe_ref[1, (b * ns + ns - 1 - j) * n + u] != 0 for u in range(n)]
    part = lambda ref, u: ref.at[pl.ds(u * C, C), :]

    def forward(u, stepwise):
        refs = (part(qf_ref, u), part(gf_ref, u), part(vf_ref, u), sf_ref, part(of_ref, u))
        if stepwise:
            _hgrn_direction_stepwise(*refs, q32_ref, v32_ref, o32_ref, True)
        else:
            _hgrn_direction(*refs, tril_ref[...], row >= col, True)

    def backward(u, stepwise):
        refs = (part(qb_ref, u), part(gb_ref, u), part(vb_ref, u), sb_ref, part(ob_ref, u))
        if stepwise:
            _hgrn_direction_stepwise(*refs, q32_ref, v32_ref, o32_ref, False)
        else:
            _hgrn_direction(*refs, triu_ref[...], row <= col, False)

    all_safe = functools.reduce(jnp.logical_and, safe_f + safe_b)

    @pl.when(all_safe)
    def _():
        for u in range(n):
            forward(u, False)
            backward(n - 1 - u, False)

    @pl.when(jnp.logical_not(all_safe))
    def _():
        for u in range(n):
            pl.when(safe_f[u])(functools.partial(forward, u, False))
            pl.when(jnp.logical_not(safe_f[u]))(functools.partial(forward, u, True))
        for u in reversed(range(n)):
            pl.when(safe_b[u])(functools.partial(backward, u, False))
            pl.when(jnp.logical_not(safe_b[u]))(functools.partial(backward, u, True))


def _hgrn(q, gf, gb, iv, gmin, batch):
    T = q.shape[0]
    C = HG_CHUNK
    nc = T // batch // C
    nt, nh, _ = gmin.shape
    halves = gmin[:, :, 0].reshape(nt, 2, nh // 4, 2)
    safe = (jnp.min(halves, axis=-1) > -HG_SAFE_LOGDECAY).astype(jnp.int32)
    safe = safe.transpose(1, 0, 2).reshape(2, T // C)
    n = HG_CHUNKS_PER_STEP
    ns = nc // n
    fwd = pl.BlockSpec((n * C, HG_WIDTH), lambda b, j, s: (b * ns + j, 0))
    bwd = pl.BlockSpec((n * C, HG_WIDTH), lambda b, j, s: (b * ns + ns - 1 - j, 0))
    const = lambda shape: pl.BlockSpec(shape, lambda b, j, s: (0,) * len(shape),
                                       pipeline_mode=pl.Buffered(1))
    row = lax.broadcasted_iota(jnp.int32, (C, C), 0)
    col = lax.broadcasted_iota(jnp.int32, (C, C), 1)
    tril = (row >= col).astype(BF16)
    triu = (row <= col).astype(BF16)
    grid_spec = pltpu.PrefetchScalarGridSpec(
        num_scalar_prefetch=1,
        grid=(batch, ns),
        in_specs=[fwd, bwd, fwd, bwd, fwd, bwd, const((C, C)), const((C, C))],
        out_specs=(fwd, bwd),
        scratch_shapes=[pltpu.VMEM((HG_HEADS, HG_DK, HG_DK), F32),
                        pltpu.VMEM((HG_HEADS, HG_DK, HG_DK), F32),
                        pltpu.VMEM((C, HG_WIDTH), F32), pltpu.VMEM((C, HG_WIDTH), F32),
                        pltpu.VMEM((C, HG_WIDTH), F32)],
    )
    return pl.pallas_call(
        _hgrn_kernel,
        grid_spec=grid_spec,
        out_shape=(jax.ShapeDtypeStruct((T, HG_WIDTH), BF16), jax.ShapeDtypeStruct((T, HG_WIDTH), BF16)),
        compiler_params=pltpu.CompilerParams(
            dimension_semantics=("parallel", "arbitrary"), vmem_limit_bytes=VMEM_LIMIT),
        name="hgrn",
    )(safe, q, q, gf, gb, iv, iv, tril, triu)


def _mix_kernel(x_ref, lng_ref, lnb_ref, of_ref, ob_ref, so_ref, sgb_ref, ag_ref, ng_ref, wpb_ref,
                wo_ref, l1g_ref, l1b_ref, wr_ref, rb_ref, tri_ref, x1_ref, x1p_ref, route_ref,
                cnt_ref, carry_ref):
    tm = x_ref.shape[0]

    @pl.when(pl.program_id(0) == 0)
    def _():
        carry_ref[...] = jnp.zeros_like(carry_ref)

    o = of_ref[...].astype(F32) + ob_ref[...].astype(F32)
    heads = []
    for h in range(HG_HEADS):
        oh = o[:, h * HG_DK:(h + 1) * HG_DK]
        heads.append(oh * lax.rsqrt(jnp.mean(oh * oh, axis=-1, keepdims=True) + RMS_EPS))
    rn = jnp.concatenate(heads, axis=1) * ng_ref[...] * so_ref[...].astype(F32)
    r = _bdot(rn, wpb_ref[...])
    mixed = ag_ref[...].astype(F32) + sgb_ref[...].astype(F32) * r
    y = _bdot(mixed, wo_ref[...])
    xn = _layer_norm(x_ref[...], lng_ref[...], lnb_ref[...])
    x1 = _layer_norm(ALPHA * xn + y, l1g_ref[...], l1b_ref[...])
    x1_ref[...] = x1
    x1p_ref[...] = _pack_rows(x1)

    neg = jnp.float32(-jnp.inf)
    reps = tm // LANES
    scores = jax.nn.sigmoid(lax.dot_general(wr_ref[...], x1.astype(BF16), _NT,
                                            preferred_element_type=F32))
    biased = (scores + jnp.concatenate([rb_ref[...]] * reps, axis=1)).reshape(
        N_GROUPS, GROUP_SIZE, tm)
    sub = lax.broadcasted_iota(jnp.int32, biased.shape, 1).astype(F32)
    m1 = jnp.max(biased, axis=1, keepdims=True)
    first = jnp.min(jnp.where(biased == m1, sub, float(GROUP_SIZE)), axis=1, keepdims=True)
    m2 = jnp.max(jnp.where(sub == first, neg, biased), axis=1, keepdims=True)
    gs = (m1 + m2).reshape(N_GROUPS, tm)
    grp = lax.broadcasted_iota(jnp.int32, (N_GROUPS, tm), 0)
    ahead = jnp.zeros((N_GROUPS, tm), F32)
    for d in range(1, N_GROUPS):
        other = pltpu.roll(gs, d, 0)
        tie = jnp.where(grp >= d, 1.0, 0.0)
        ahead = ahead + jnp.where(other > gs, 1.0, jnp.where(other == gs, tie, 0.0))
    keep = (ahead < TOPK_GROUPS).reshape(N_GROUPS, 1, tm)
    allowed = jnp.where(keep, biased, neg).reshape(N_EXPERTS, tm)
    row = lax.broadcasted_iota(jnp.int32, (N_EXPERTS, tm), 0).astype(F32)
    sel = jnp.zeros((N_EXPERTS, tm), F32)
    picks = []
    for _ in range(TOP_K):
        m = jnp.max(allowed, axis=0, keepdims=True)
        first = jnp.min(jnp.where(allowed == m, row, float(N_EXPERTS)), axis=0, keepdims=True)
        hit = row == first
        picks.append((first, hit, jnp.sum(jnp.where(hit, scores, 0.0), axis=0, keepdims=True)))
        sel = jnp.where(hit, 1.0, sel)
        allowed = jnp.where(hit, neg, allowed)
    wsum = picks[0][2]
    for pk in picks[1:]:
        wsum = wsum + pk[2]
    selb = sel.astype(BF16)
    carry = carry_ref[...]
    before = (jnp.dot(selb, tri_ref[...], preferred_element_type=F32)
              + jnp.concatenate([carry] * reps, axis=1))
    total = carry + jnp.dot(selb, jnp.ones((tm, LANES), BF16), preferred_element_type=F32)
    carry_ref[...] = total
    cnt_ref[...] = total
    blank = [jnp.zeros((1, tm), F32)] * (SUBLANES - TOP_K)
    route_ref[...] = jnp.concatenate(
        [pk[0] for pk in picks] + blank
        + [jnp.sum(jnp.where(pk[1], before, 0.0), axis=0, keepdims=True) for pk in picks] + blank
        + [pk[2] / wsum * ROUTED_SCALE for pk in picks] + blank, axis=0)


def _mix(x, of, ob, so, sgb, ag, p):
    T = x.shape[0]
    tm = TOKEN_TILE
    tok = lambda w: pl.BlockSpec((tm, w), lambda i: (i, 0))
    row = lax.broadcasted_iota(jnp.int32, (tm, tm), 0)
    col = lax.broadcasted_iota(jnp.int32, (tm, tm), 1)
    tri = (row < col).astype(BF16)
    return pl.pallas_call(
        _mix_kernel,
        grid=(T // tm,),
        in_specs=[
            tok(D_MODEL), _const_spec((1, D_MODEL)), _const_spec((1, D_MODEL)),
            tok(HG_WIDTH), tok(HG_WIDTH), tok(HG_WIDTH), tok(D_MODEL), tok(D_MODEL),
            _const_spec((1, HG_WIDTH)),
            _const_spec((HG_WIDTH, D_MODEL)), _const_spec((D_MODEL, D_MODEL)),
            _const_spec((1, D_MODEL)), _const_spec((1, D_MODEL)),
            _const_spec((N_EXPERTS, D_MODEL)), _const_spec((N_EXPERTS, LANES)),
            _const_spec((tm, tm)),
        ],
        out_specs=(tok(D_MODEL), tok(D_MODEL // 2), pl.BlockSpec((ROUTE_ROWS, tm), lambda i: (0, i)),
                   pl.BlockSpec((N_EXPERTS, LANES), lambda i: (0, 0))),
        out_shape=(jax.ShapeDtypeStruct((T, D_MODEL), F32),
                   jax.ShapeDtypeStruct((T, D_MODEL // 2), jnp.uint32),
                   jax.ShapeDtypeStruct((ROUTE_ROWS, T), F32),
                   jax.ShapeDtypeStruct((N_EXPERTS, LANES), F32)),
        scratch_shapes=[pltpu.VMEM((N_EXPERTS, LANES), F32)],
        compiler_params=pltpu.CompilerParams(
            dimension_semantics=("arbitrary",), vmem_limit_bytes=VMEM_LIMIT),
        name="mix",
    )(x, p["ln_in_g"], p["ln_in_b"], of, ob, so, sgb, ag, p["hg_norm_g"], p["w_pb"], p["w_o"],
      p["ln1_g"], p["ln1_b"], p["w_router"], p["router_bias"], tri)


def _sc_mesh():
    return plsc.VectorSubcoreMesh(core_axis_name="c", subcore_axis_name="s",
                                  num_cores=SC_CORES, num_subcores=SC_SUBCORES)


def _sc_worker():
    return lax.axis_index("s") * SC_CORES + lax.axis_index("c")


def _sc_dispatch(x1p, dest, n_rows):
    T, w = x1p.shape
    W = SC_WINDOW
    per_worker = T // W // (SC_CORES * SC_SUBCORES)

    def body(x_hbm, d_hbm, o_hbm, rows_v, idx_v, sem):
        first = _sc_worker() * per_worker
        for k in range(TOP_K):
            pltpu.sync_copy(d_hbm.at[k, pl.ds(first, per_worker)], idx_v.at[k])

        @pl.loop(0, per_worker)
        def _(j):
            pltpu.sync_copy(x_hbm.at[pl.ds((first + j) * W, W)], rows_v)
            copies = [pltpu.async_copy(rows_v, o_hbm.at[idx_v.at[k, j]], sem) for k in range(TOP_K)]
            for c in copies:
                c.wait()

    return pl.kernel(
        body,
        out_type=jax.ShapeDtypeStruct((n_rows, w), jnp.uint32),
        mesh=_sc_mesh(),
        scratch_types=[pltpu.VMEM((W, w), jnp.uint32), pltpu.VMEM((TOP_K, per_worker, W), jnp.int32),
                       pltpu.SemaphoreType.DMA],
        name="sc_dispatch",
    )(x1p, dest.reshape(TOP_K, T // W, W))


def _sc_combine(ys, dest, gates):
    T = dest.shape[1]
    W = SC_COMBINE_WINDOW
    w = ys.shape[1]
    per_worker = T // W // (SC_CORES * SC_SUBCORES)
    assert per_worker % 2 == 0 and per_worker * W * SC_CORES * SC_SUBCORES == T

    def body(y_hbm, d_hbm, g_hbm, o_hbm, idx_v, *scratch):
        slots = [dict(rows=scratch[s], gate=scratch[2 + s], out=scratch[4 + s], gsem=scratch[6 + s],
                      wsem=scratch[8 + s]) for s in range(2)]
        first = _sc_worker() * per_worker
        for k in range(TOP_K):
            pltpu.sync_copy(d_hbm.at[k, pl.ds(first * W, per_worker * W)], idx_v.at[k])

        def loads(b, win):
            i = win - first
            return [pltpu.make_async_copy(g_hbm.at[win], b["gate"], b["gsem"])] + [
                pltpu.make_async_copy(y_hbm.at[idx_v.at[k, pl.ds(i * W, W)]],
                                      b["rows"].at[k], b["gsem"]) for k in range(TOP_K)]

        def write_back(b, win):
            return pltpu.make_async_copy(b["out"], o_hbm.at[pl.ds(win * W, W)], b["wsem"])

        def fetch(b, win):
            for c in loads(b, win):
                c.start()

        def reduce_rows(b):
            rows, gate_v, out = b["rows"], b["gate"], b["out"]

            @pl.loop(0, W)
            def _(j):
                gate = [gate_v[pl.ds((j * TOP_K + k) * SC_LANES, SC_LANES)] for k in range(TOP_K)]

                @plsc.parallel_loop(0, w, step=SC_LANES, unroll=4)
                def _(col):
                    lo, hi = [], []
                    for k in range(TOP_K):
                        p = rows[k, j, pl.ds(col, SC_LANES)]
                        lo.append(plsc.bitcast(p << 16, F32) * gate[k])
                        hi.append(plsc.bitcast(p & jnp.uint32(0xFFFF0000), F32) * gate[k])
                    out[j, pl.ds(col, SC_LANES)] = (lo[0] + lo[1]) + (lo[2] + lo[3]) + (lo[4] + lo[5])
                    out[j, pl.ds(w + col, SC_LANES)] = (hi[0] + hi[1]) + (hi[2] + hi[3]) + (hi[4] + hi[5])

        def process(b, win, not_first):
            for c in loads(b, win):
                c.wait()

            @pl.when(not_first)
            def _():
                write_back(b, win).wait()

            reduce_rows(b)
            write_back(b, win).start()

        fetch(slots[0], first)

        @pl.loop(0, per_worker, step=2)
        def _(i):
            win = first + i
            fetch(slots[1], win + 1)
            process(slots[0], win, i > 0)

            @pl.when(i + 2 < per_worker)
            def _():
                fetch(slots[0], win + 2)

            process(slots[1], win + 1, i > 0)

        for b in slots:
            write_back(b, first).wait()

    slot_types = ([pltpu.VMEM((TOP_K, per_worker * W), jnp.int32)]
                  + [pltpu.VMEM((TOP_K, W, w), jnp.uint32)] * 2
                  + [pltpu.VMEM((W * TOP_K * SC_LANES,), F32)] * 2
                  + [pltpu.VMEM((W, 2 * w), F32)] * 2
                  + [pltpu.SemaphoreType.DMA] * 4)
    return pl.kernel(
        body,
        out_type=jax.ShapeDtypeStruct((T, 2 * w), F32),
        mesh=_sc_mesh(),
        scratch_types=slot_types,
        compiler_params=pltpu.CompilerParams(needs_layout_passes=False),
        name="sc_combine",
    )(ys, dest, gates)


def _experts_kernel(be_ref, nv_ref, io_ref, xs_ref, wg_ref, wu_ref, wd_ref, ys_ref, *cast_refs):
    del io_ref
    i = pl.program_id(0)
    n_valid = nv_ref[i]
    mb, w = xs_ref.shape
    if cast_refs:
        @pl.when((i == 0) | (be_ref[i] != be_ref[jnp.maximum(i, 1) - 1]))
        def _():
            for src, dst in zip((wg_ref, wu_ref, wd_ref), cast_refs):
                dst[...] = src[...].astype(BF16)
        wg_ref, wu_ref, wd_ref = cast_refs

    @pl.when(n_valid > 0)
    def _():
        keep = lax.broadcasted_iota(jnp.int32, (mb, w), 0) < n_valid
        lo, hi = _unpack_rows(jnp.where(keep, xs_ref[...], jnp.uint32(0)))
        lo, hi = lo.astype(BF16), hi.astype(BF16)

        def proj(w_ref):
            return (jnp.dot(lo, w_ref[0, :w, :], preferred_element_type=F32)
                    + jnp.dot(hi, w_ref[0, w:, :], preferred_element_type=F32))

        hb = jax.nn.silu(proj(wg_ref)) * proj(wu_ref)
        ys_ref[...] = _pack_rows(jnp.dot(hb.astype(BF16), wd_ref[0], preferred_element_type=F32))


def _experts(xs, blk_exp, n_valid, blk_io, weights):
    n_rows, w = xs.shape
    mb = n_rows // blk_exp.shape[0]
    cast = weights[0].dtype != BF16
    by_expert = lambda shape: pl.BlockSpec((1,) + shape, lambda i, be, nv, io: (be[i], 0, 0))
    w_specs = [by_expert((D_MODEL, EXPERT_DIM)), by_expert((D_MODEL, EXPERT_DIM)),
               by_expert((EXPERT_DIM, D_MODEL))]
    ys_spec = pl.BlockSpec((mb, w), lambda i, be, nv, io: (io[i], 0))
    ys_shape = jax.ShapeDtypeStruct((n_rows, w), jnp.uint32)
    grid_spec = pltpu.PrefetchScalarGridSpec(
        num_scalar_prefetch=3,
        grid=(n_rows // mb,),
        in_specs=[pl.BlockSpec((mb, w), lambda i, be, nv, io: (io[i], 0))] + w_specs,
        out_specs=(ys_spec, *w_specs) if cast else ys_spec,
    )
    out = pl.pallas_call(
        _experts_kernel,
        grid_spec=grid_spec,
        out_shape=(ys_shape, *[jax.ShapeDtypeStruct(x.shape, BF16) for x in weights]) if cast else ys_shape,
        compiler_params=pltpu.CompilerParams(
            dimension_semantics=("arbitrary",), vmem_limit_bytes=VMEM_LIMIT),
        name="experts",
    )(blk_exp, n_valid, blk_io, xs, *weights)
    return (out[0], tuple(out[1:])) if cast else (out, weights)


def _final_kernel(x1_ref, routed_ref, wsg_ref, wsu_ref, wsd_ref, l2g_ref, l2b_ref, out_ref):
    x1 = x1_ref[...]
    xb = x1.astype(BF16)
    hs = (jax.nn.silu(jnp.dot(xb, wsg_ref[...], preferred_element_type=F32))
          * jnp.dot(xb, wsu_ref[...], preferred_element_type=F32))
    shared = jnp.dot(hs.astype(BF16), wsd_ref[...], preferred_element_type=F32)
    out_ref[...] = _layer_norm(ALPHA * x1 + (routed_ref[...] + shared), l2g_ref[...], l2b_ref[...])


def _final(x1, routed, p):
    T = x1.shape[0]
    tm = TOKEN_TILE
    tok = lambda w: pl.BlockSpec((tm, w), lambda i: (i, 0))
    return pl.pallas_call(
        _final_kernel,
        grid=(T // tm,),
        in_specs=[
            tok(D_MODEL), tok(D_MODEL),
            _const_spec((D_MODEL, SHARED_DIM)), _const_spec((D_MODEL, SHARED_DIM)),
            _const_spec((SHARED_DIM, D_MODEL)),
            _const_spec((1, D_MODEL)), _const_spec((1, D_MODEL)),
        ],
        out_specs=tok(D_MODEL),
        out_shape=jax.ShapeDtypeStruct((T, D_MODEL), F32),
        compiler_params=pltpu.CompilerParams(
            dimension_semantics=("parallel",), vmem_limit_bytes=VMEM_LIMIT),
        name="final",
    )(x1, routed, p["w_sh_gate"], p["w_sh_up"], p["w_sh_down"], p["ln2_g"], p["ln2_b"])


def _routing_layout(route, counts, n_tokens):
    mb = EXPERT_BLOCK
    n_blocks = -(-n_tokens * TOP_K // mb) + N_EXPERTS
    idx = route[0:TOP_K].astype(jnp.int32)
    rank = route[SUBLANES:SUBLANES + TOP_K].astype(jnp.int32)
    counts = counts.astype(jnp.int32)
    padded = jnp.maximum((counts + mb - 1) // mb, 1) * mb
    pad_end = jnp.cumsum(padded)
    pad_start = pad_end - padded
    experts = jnp.arange(N_EXPERTS, dtype=jnp.int32)
    dest = rank + jnp.sum(jnp.where(idx[:, :, None] == experts, pad_start, 0), axis=-1)
    blk_start = jnp.arange(n_blocks, dtype=jnp.int32) * mb
    blk_exp = jnp.minimum(
        jnp.sum((pad_end[None, :] <= blk_start[:, None]).astype(jnp.int32), axis=1), N_EXPERTS - 1)
    valid_end = jnp.sum(jnp.where(blk_exp[:, None] == experts, pad_start + counts, 0), axis=-1)
    n_valid = jnp.clip(valid_end - blk_start, 0, mb).astype(jnp.int32)
    blk_io = jnp.minimum(jnp.arange(n_blocks, dtype=jnp.int32), pad_end[-1] // mb - 1)
    gates = jnp.repeat(route[2 * SUBLANES:2 * SUBLANES + TOP_K], SC_LANES, axis=0).T.reshape(
        n_tokens // SC_COMBINE_WINDOW, -1)
    return dest, gates, blk_exp, n_valid, blk_io, n_blocks * mb


def _encode(x, p, expert_weights):
    batch, seq, _ = x.shape
    T = batch * seq
    xt = x.reshape(T, D_MODEL)
    ag, q, gf, gb, iv, so, sgb, gmin = _inproj(xt, p)
    of, ob = _hgrn(q, gf, gb, iv, gmin, batch)
    x1, x1p, route, cnt = _mix(xt, of, ob, so, sgb, ag, p)
    dest, gates, blk_exp, n_valid, blk_io, n_rows = _routing_layout(route, cnt[:, 0], T)
    xs = _sc_dispatch(x1p, dest, n_rows)
    ys, expert_weights = _experts(xs, blk_exp, n_valid, blk_io, expert_weights)
    out = _final(x1, _sc_combine(ys, dest, gates), p)
    return out.reshape(batch, seq, D_MODEL), expert_weights


def _prepare_params(ln_in_g, ln_in_b, w_in, a_ln_g, a_ln_b, a_ws, a_sb, hg_lb_logits, hg_norm_g,
                    w_pa, w_pb, w_o, ln1_g, ln1_b, w_router, router_bias, w_sh_gate, w_sh_up,
                    w_sh_down, ln2_g, ln2_b):
    l = 0
    row = lambda v: v.reshape(1, -1).astype(F32)
    ws = a_ws[l].astype(BF16)
    wsp = jnp.concatenate([ws[0::2], ws[1::2]], axis=2)
    sbf = jnp.repeat(a_sb[l].astype(F32), A_WIDTH // A_GROUPS, axis=1)
    lb = jnp.cumsum(jax.nn.softmax(hg_lb_logits.astype(F32), axis=1), axis=1)[:, l]
    return dict(
        ln_in_g=row(ln_in_g), ln_in_b=row(ln_in_b), w_in=w_in[l].astype(BF16),
        a_ln_g=row(a_ln_g[l]), a_ln_b=row(a_ln_b[l]), wsp=wsp, sbf=sbf, lb=lb,
        w_pa=w_pa[l].astype(BF16), hg_norm_g=row(hg_norm_g[l]),
        w_pb=w_pb[l].astype(BF16), w_o=w_o[l].astype(BF16),
        ln1_g=row(ln1_g[l]), ln1_b=row(ln1_b[l]),
        w_router=w_router[l].T.astype(BF16),
        router_bias=jnp.broadcast_to(router_bias[l].astype(F32)[:, None], (N_EXPERTS, LANES)),
        w_sh_gate=w_sh_gate[l].astype(BF16), w_sh_up=w_sh_up[l].astype(BF16),
        w_sh_down=w_sh_down[l].astype(BF16),
        ln2_g=row(ln2_g[l]), ln2_b=row(ln2_b[l]),
    )


def kernel(x_prompt, x_sample, ln_in_g, ln_in_b, w_in, a_ln_g, a_ln_b, a_ws, a_sb, hg_lb_logits,
           hg_norm_g, w_pa, w_pb, w_o, ln1_g, ln1_b, w_router, router_bias, w_e_gate, w_e_up,
           w_e_down, w_sh_gate, w_sh_up, w_sh_down, ln2_g, ln2_b):
    p = _prepare_params(ln_in_g, ln_in_b, w_in, a_ln_g, a_ln_b, a_ws, a_sb, hg_lb_logits, hg_norm_g,
                        w_pa, w_pb, w_o, ln1_g, ln1_b, w_router, router_bias, w_sh_gate, w_sh_up,
                        w_sh_down, ln2_g, ln2_b)
    y_prompt, expert_weights = _encode(x_prompt, p, (w_e_gate[0], w_e_up[0], w_e_down[0]))
    y_sample, _ = _encode(x_sample, p, expert_weights)
    return y_prompt, y_sample
```

```python
import functools

import jax
import jax.numpy as jnp
from jax import lax
from jax.experimental import pallas as pl
from jax.experimental.pallas import tpu as pltpu
from jax.experimental.pallas import tpu_sc as plsc

F32 = jnp.float32
BF16 = jnp.bfloat16

D_MODEL = 1024
A_GROUPS = 8
A_WIDTH = 512
A_CHUNK = 128
HG_HEADS = 8
HG_DK = 128
HG_WIDTH = HG_HEADS * HG_DK
N_IN = 2 * A_WIDTH + 5 * HG_WIDTH + 2 * D_MODEL
N_EXPERTS = 64
TOP_K = 6
N_GROUPS = 8
TOPK_GROUPS = 4
GROUP_SIZE = N_EXPERTS // N_GROUPS
EXPERT_DIM = 256
SHARED_DIM = 256
ROUTED_SCALE = 2.5
DEPTH = 1
ALPHA = (2.0 * DEPTH) ** 0.25
LN_EPS = 1e-5
RMS_EPS = 1e-6

LANES = 128
SUBLANES = 8
ROUTE_ROWS = 3 * SUBLANES
TOKEN_TILE = 512
INPROJ_TILE = 256
HG_CHUNK = 128
HG_CHUNKS_PER_STEP = 2
HG_SAFE_LOGDECAY = 80.0
EXPERT_BLOCK = 1024
SC_CORES = 2
SC_SUBCORES = 16
SC_LANES = 16
SC_WINDOW = 32
SC_COMBINE_WINDOW = 8
VMEM_LIMIT = 56 * 1024 * 1024

_O_U, _O_V, _O_Q, _O_FF, _O_FB, _O_I, _O_G, _O_GA, _O_GB = (
    0, 512, 1024, 2048, 3072, 4096, 5120, 6144, 7168)


def _layer_norm(x, g, b):
    mu = jnp.mean(x, axis=-1, keepdims=True)
    xc = x - mu
    var = jnp.mean(xc * xc, axis=-1, keepdims=True)
    return xc * lax.rsqrt(var + LN_EPS) * g + b


def _bdot(a, b):
    return jnp.dot(a.astype(BF16), b.astype(BF16), preferred_element_type=F32)


def _pack_rows(x):
    w = x.shape[1] // 2
    lo = lax.bitcast_convert_type(x[:, :w].astype(BF16).astype(F32), jnp.uint32)
    hi = lax.bitcast_convert_type(x[:, w:].astype(BF16).astype(F32), jnp.uint32)
    return hi | (lo >> 16)


def _unpack_rows(p):
    lo = lax.bitcast_convert_type(p << 16, F32)
    hi = lax.bitcast_convert_type(p & jnp.uint32(0xFFFF0000), F32)
    return lo, hi


def _const_spec(shape):
    nd = len(shape)
    return pl.BlockSpec(shape, lambda *_: (0,) * nd, pipeline_mode=pl.Buffered(1))


def _inproj_kernel(x_ref, lng_ref, lnb_ref, win_ref, alng_ref, alnb_ref, wsp_ref, sbf_ref, lb_ref,
                   wpa_ref, ag_ref, q_ref, gf_ref, gb_ref, iv_ref, so_ref, sgb_ref, gmin_ref):
    tm = x_ref.shape[0]
    xb = _layer_norm(x_ref[...], lng_ref[...], lnb_ref[...]).astype(BF16)

    def sec(lo, width):
        return jnp.dot(xb, win_ref[:, lo:lo + width], preferred_element_type=F32)

    v = _layer_norm(jax.nn.gelu(sec(_O_V, A_WIDTH)), alng_ref[...], alnb_ref[...]).astype(BF16)
    u = jax.nn.gelu(sec(_O_U, A_WIDTH))

    q_ref[...] = jax.nn.silu(sec(_O_Q, HG_WIDTH)).astype(BF16)
    mins = []
    for d, (off, g_ref) in enumerate(((_O_FF, gf_ref), (_O_FB, gb_ref))):
        lb = lb_ref[d:d + 1, :]
        f = lb + (1.0 - lb) * jax.nn.sigmoid(sec(off, HG_WIDTH))
        g = jnp.log(f)
        g_ref[...] = g
        half = jnp.sum(g.reshape(tm // (HG_CHUNK // 2), HG_CHUNK // 2, HG_WIDTH), axis=1)
        mins.append(jnp.min(half, axis=-1, keepdims=True))
    gmin_ref[0] = jnp.broadcast_to(jnp.concatenate(mins, axis=0), gmin_ref.shape[1:])
    so_ref[...] = jax.nn.silu(sec(_O_G, HG_WIDTH)).astype(BF16)
    sgb_ref[...] = jax.nn.sigmoid(sec(_O_GB, D_MODEL)).astype(BF16)

    lane = lax.broadcasted_iota(jnp.int32, (A_CHUNK, LANES), 1)
    left = lane < (A_WIDTH // A_GROUPS)
    zero = jnp.zeros((A_CHUNK, LANES), BF16)
    chunks = []
    for c in range(tm // A_CHUNK):
        vc = v[c * A_CHUNK:(c + 1) * A_CHUNK]
        cols = []
        for p in range(A_GROUPS // 2):
            vp = vc[:, p * LANES:(p + 1) * LANES]
            rhs = jnp.concatenate([jnp.where(left, vp, zero), jnp.where(left, zero, vp)], axis=0)
            cols.append(jnp.dot(wsp_ref[p], rhs, preferred_element_type=F32))
        chunks.append(jnp.concatenate(cols, axis=1) + sbf_ref[...])
    mixed = jnp.concatenate(chunks, axis=0)
    a = _bdot(u * mixed, wpa_ref[...])
    ag_ref[...] = (jax.nn.sigmoid(sec(_O_GA, D_MODEL)) * a).astype(BF16)
    iv_ref[...] = sec(_O_I, HG_WIDTH).astype(BF16)


def _inproj(x, p):
    T = x.shape[0]
    tm = INPROJ_TILE
    nt = T // tm
    nh = 2 * tm // (HG_CHUNK // 2)
    tok = lambda w: pl.BlockSpec((tm, w), lambda i: (i, 0))
    outs = (
        jax.ShapeDtypeStruct((T, D_MODEL), BF16),
        jax.ShapeDtypeStruct((T, HG_WIDTH), BF16),
        jax.ShapeDtypeStruct((T, HG_WIDTH), F32),
        jax.ShapeDtypeStruct((T, HG_WIDTH), F32),
        jax.ShapeDtypeStruct((T, HG_WIDTH), BF16),
        jax.ShapeDtypeStruct((T, HG_WIDTH), BF16),
        jax.ShapeDtypeStruct((T, D_MODEL), BF16),
        jax.ShapeDtypeStruct((nt, nh, LANES), F32),
    )
    return pl.pallas_call(
        _inproj_kernel,
        grid=(nt,),
        in_specs=[
            tok(D_MODEL),
            _const_spec((1, D_MODEL)), _const_spec((1, D_MODEL)),
            _const_spec((D_MODEL, N_IN)),
            _const_spec((1, A_WIDTH)), _const_spec((1, A_WIDTH)),
            _const_spec((A_GROUPS // 2, A_CHUNK, 2 * A_CHUNK)),
            _const_spec((A_CHUNK, A_WIDTH)),
            _const_spec((2, HG_WIDTH)),
            _const_spec((A_WIDTH, D_MODEL)),
        ],
        out_specs=(tok(D_MODEL), tok(HG_WIDTH), tok(HG_WIDTH), tok(HG_WIDTH), tok(HG_WIDTH),
                   tok(HG_WIDTH), tok(D_MODEL), pl.BlockSpec((1, nh, LANES), lambda i: (i, 0, 0))),
        out_shape=outs,
        compiler_params=pltpu.CompilerParams(
            dimension_semantics=("parallel",), vmem_limit_bytes=VMEM_LIMIT),
        name="inproj",
    )(x, p["ln_in_g"], p["ln_in_b"], p["w_in"], p["a_ln_g"], p["a_ln_b"], p["wsp"], p["sbf"],
      p["lb"], p["w_pa"])


_NT = (((1,), (1,)), ((), ()))
_TN = (((0,), (0,)), ((), ()))


def _hgrn_direction(q_ref, g_ref, v_ref, st_ref, o_ref, tri, mask, fwd):
    C = q_ref.shape[0]
    g = g_ref[...]
    ghi = g.astype(BF16)
    glo = (g - ghi.astype(F32)).astype(BF16)
    b = jnp.dot(jnp.concatenate([tri, tri], axis=1), jnp.concatenate([ghi, glo], axis=0),
                preferred_element_type=F32)
    mid = C // 2 - 1 if fwd else C // 2
    end = C - 1 if fwd else 0
    r = b[mid:mid + 1, :]
    b_end = b[end:end + 1, :]
    qt = q_ref[...].astype(F32) * jnp.exp(b - r)
    kt = (1.0 - jnp.exp(g)) * jnp.exp(r - b)
    qtb = qt.astype(BF16)
    ktb = kt.astype(BF16)
    qhb = (qt * jnp.exp(r)).astype(BF16)
    khb = (kt * jnp.exp(b_end - r)).astype(BF16)
    decay = jnp.exp(b_end)
    v = v_ref[...]
    heads =[slice(h * HG_DK, (h + 1) * HG_DK) for h in range(HG_HEADS)]
    zero = jnp.zeros((C, HG_DK), BF16)
    mask2 = jnp.concatenate([mask, mask], axis=1)
    scores = []
    for p in range(HG_HEADS // 2):
        k1, k2 = ktb[:, heads[2 * p]], ktb[:, heads[2 * p + 1]]
        kk = jnp.concatenate([jnp.concatenate([k1, zero], axis=1),
                              jnp.concatenate([zero, k2], axis=1)], axis=0)
        s2 = lax.dot_general(qtb[:, 2 * p * HG_DK:(2 * p + 2) * HG_DK], kk, _NT,
                             preferred_element_type=F32)
        s2 = jnp.where(mask2, s2, 0.0).astype(BF16)
        scores += [s2[:, :C], s2[:, C:]]
    for h, sl in enumerate(heads):
        st = st_ref[h]
        vt = v[:, sl].T
        o_ref[:, sl] = lax.dot_general(
            jnp.concatenate([scores[h], qhb[:, sl]], axis=1),
            jnp.concatenate([vt, st.astype(BF16)], axis=1), _NT,
            preferred_element_type=F32).astype(o_ref.dtype)
        st_ref[h] = st * decay[:, sl] + jnp.dot(vt, khb[:, sl], preferred_element_type=F32)


def _hgrn_direction_stepwise(q_ref, g_ref, v_ref, st_ref, o_ref, q32_ref, v32_ref, o32_ref, fwd):
    C = q_ref.shape[0]
    q32_ref[...] = q_ref[...].astype(F32)
    v32_ref[...] = v_ref[...].astype(F32)
    sub = SUBLANES
    rows = lax.broadcasted_iota(jnp.int32, (sub, HG_DK), 0)

    def group(i, carry):
        base = pl.multiple_of((i if fwd else C // sub - 1 - i) * sub, sub)
        f = jnp.exp(g_ref[pl.ds(base, sub), :])
        k = 1.0 - f
        q = q32_ref[pl.ds(base, sub), :]
        v = v32_ref[pl.ds(base, sub), :]
        for h in range(HG_HEADS):
            sl = slice(h * HG_DK, (h + 1) * HG_DK)
            st = st_ref[h]
            out = jnp.zeros((sub, HG_DK), F32)
            for r in (range(sub) if fwd else range(sub - 1, -1, -1)):
                v_t = jnp.where(rows == 0, v[r:r + 1, sl], 0.0).astype(BF16)
                k_t = jnp.broadcast_to(k[r:r + 1, sl], (sub, HG_DK)).astype(BF16)
                st = st * f[r:r + 1, sl] + lax.dot_general(v_t, k_t, _TN, preferred_element_type=F32)
                q_t = jnp.broadcast_to(q[r:r + 1, sl], (sub, HG_DK)).astype(BF16)
                o_t = lax.dot_general(q_t, st.astype(BF16), _NT, preferred_element_type=F32)
                out = jnp.where(rows == r, o_t, out)
            st_ref[h] = st
            o32_ref[pl.ds(base, sub), sl] = out
        return carry

    lax.fori_loop(0, C // sub, group, 0)
    o_ref[...] = o32_ref[...].astype(o_ref.dtype)


def _hgrn_kernel(safe_ref, qf_ref, qb_ref, gf_ref, gb_ref, vf_ref, vb_ref, tril_ref, triu_ref,
                 of_ref, ob_ref, sf_ref, sb_ref, q32_ref, v32_ref, o32_ref):
    b, j = pl.program_id(0), pl.program_id(1)
    ns = pl.num_programs(1)
    n = HG_CHUNKS_PER_STEP
    C = qf_ref.shape[0] // n

    @pl.when(j == 0)
    def _():
        sf_ref[...] = jnp.zeros_like(sf_ref)
        sb_ref[...] = jnp.zeros_like(sb_ref)

    row = lax.broadcasted_iota(jnp.int32, (C, C), 0)
    col = lax.broadcasted_iota(jnp.int32, (C, C), 1)
    safe_f = [safe_ref[0, (b * ns + j) * n + u] != 0 for u in range(n)]
    safe_b = [safe_ref[1, (b * ns + ns - 1 - j) * n + u] != 0 for u in range(n)]
    part = lambda ref, u: ref.at[pl.ds(u * C, C), :]

    def forward(u, stepwise):
        refs = (part(qf_ref, u), part(gf_ref, u), part(vf_ref, u), sf_ref, part(of_ref, u))
        if stepwise:
            _hgrn_direction_stepwise(*refs, q32_ref, v32_ref, o32_ref, True)
        else:
            _hgrn_direction(*refs, tril_ref[...], row >= col, True)

    def backward(u, stepwise):
        refs = (part(qb_ref, u), part(gb_ref, u), part(vb_ref, u), sb_ref, part(ob_ref, u))
        if stepwise:
            _hgrn_direction_stepwise(*refs, q32_ref, v32_ref, o32_ref, False)
        else:
            _hgrn_direction(*refs, triu_ref[...], row <= col, False)

    all_safe = functools.reduce(jnp.logical_and, safe_f + safe_b)

    @pl.when(all_safe)
    def _():
        for u in range(n):
            forward(u, False)
            backward(n - 1 - u, False)

    @pl.when(jnp.logical_not(all_safe))
    def _():
        for u in range(n):
            pl.when(safe_f[u])(functools.partial(forward, u, False))
            pl.when(jnp.logical_not(safe_f[u]))(functools.partial(forward, u, True))
        for u in reversed(range(n)):
            pl.when(safe_b[u])(functools.partial(backward, u, False))
            pl.when(jnp.logical_not(safe_b[u]))(functools.partial(backward, u, True))


def _hgrn(q, gf, gb, iv, gmin, batch):
    T = q.shape[0]
    C = HG_CHUNK
    nc = T // batch // C
    nt, nh, _ = gmin.shape
    halves = gmin[:, :, 0].reshape(nt, 2, nh // 4, 2)
    safe = (jnp.min(halves, axis=-1) > -HG_SAFE_LOGDECAY).astype(jnp.int32)
    safe = safe.transpose(1, 0, 2).reshape(2, T // C)
    n = HG_CHUNKS_PER_STEP
    ns = nc // n
    fwd = pl.BlockSpec((n * C, HG_WIDTH), lambda b, j, s: (b * ns + j, 0))
    bwd = pl.BlockSpec((n * C, HG_WIDTH), lambda b, j, s: (b * ns + ns - 1 - j, 0))
    const = lambda shape: pl.BlockSpec(shape, lambda b, j, s: (0,) * len(shape),
                                       pipeline_mode=pl.Buffered(1))
    row = lax.broadcasted_iota(jnp.int32, (C, C), 0)
    col = lax.broadcasted_iota(jnp.int32, (C, C), 1)
    tril = (row >= col).astype(BF16)
    triu = (row <= col).astype(BF16)
    grid_spec = pltpu.PrefetchScalarGridSpec(
        num_scalar_prefetch=1,
        grid=(batch, ns),
        in_specs=[fwd, bwd, fwd, bwd, fwd, bwd, const((C, C)), const((C, C))],
        out_specs=(fwd, bwd),
        scratch_shapes=[pltpu.VMEM((HG_HEADS, HG_DK, HG_DK), F32),
                        pltpu.VMEM((HG_HEADS, HG_DK, HG_DK), F32),
                        pltpu.VMEM((C, HG_WIDTH), F32), pltpu.VMEM((C, HG_WIDTH), F32),
                        pltpu.VMEM((C, HG_WIDTH), F32)],
    )
    return pl.pallas_call(
        _hgrn_kernel,
        grid_spec=grid_spec,
        out_shape=(jax.ShapeDtypeStruct((T, HG_WIDTH), BF16), jax.ShapeDtypeStruct((T, HG_WIDTH), BF16)),
        compiler_params=pltpu.CompilerParams(
            dimension_semantics=("parallel", "arbitrary"), vmem_limit_bytes=VMEM_LIMIT),
        name="hgrn",
    )(safe, q, q, gf, gb, iv, iv, tril, triu)


def _mix_kernel(x_ref, lng_ref, lnb_ref, of_ref, ob_ref, so_ref, sgb_ref, ag_ref, ng_ref, wpb_ref,
                wo_ref, l1g_ref, l1b_ref, wr_ref, rb_ref, tri_ref, x1_ref, x1p_ref, route_ref,
                cnt_ref, carry_ref):
    tm = x_ref.shape[0]

    @pl.when(pl.program_id(0) == 0)
    def _():
        carry_ref[...] = jnp.zeros_like(carry_ref)

    o = of_ref[...].astype(F32) + ob_ref[...].astype(F32)
    heads = []
    for h in range(HG_HEADS):
        oh = o[:, h * HG_DK:(h + 1) * HG_DK]
        heads.append(oh * lax.rsqrt(jnp.mean(oh * oh, axis=-1, keepdims=True) + RMS_EPS))
    rn = jnp.concatenate(heads, axis=1) * ng_ref[...] * so_ref[...].astype(F32)
    r = _bdot(rn, wpb_ref[...])
    mixed = ag_ref[...].astype(F32) + sgb_ref[...].astype(F32) * r
    y = _bdot(mixed, wo_ref[...])
    xn = _layer_norm(x_ref[...], lng_ref[...], lnb_ref[...])
    x1 = _layer_norm(ALPHA * xn + y, l1g_ref[...], l1b_ref[...])
    x1_ref[...] = x1
    x1p_ref[...] = _pack_rows(x1)

    neg = jnp.float32(-jnp.inf)
    reps = tm // LANES
    scores = jax.nn.sigmoid(lax.dot_general(wr_ref[...], x1.astype(BF16), _NT,
                                            preferred_element_type=F32))
    biased = (scores + jnp.concatenate([rb_ref[...]] * reps, axis=1)).reshape(
        N_GROUPS, GROUP_SIZE, tm)
    sub = lax.broadcasted_iota(jnp.int32, biased.shape, 1).astype(F32)
    m1 = jnp.max(biased, axis=1, keepdims=True)
    first = jnp.min(jnp.where(biased == m1, sub, float(GROUP_SIZE)), axis=1, keepdims=True)
    m2 = jnp.max(jnp.where(sub == first, neg, biased), axis=1, keepdims=True)
    gs = (m1 + m2).reshape(N_GROUPS, tm)
    grp = lax.broadcasted_iota(jnp.int32, (N_GROUPS, tm), 0)
    ahead = jnp.zeros((N_GROUPS, tm), F32)
    for d in range(1, N_GROUPS):
        other = pltpu.roll(gs, d, 0)
        tie = jnp.where(grp >= d, 1.0, 0.0)
        ahead = ahead + jnp.where(other > gs, 1.0, jnp.where(other == gs, tie, 0.0))
    keep = (ahead < TOPK_GROUPS).reshape(N_GROUPS, 1, tm)
    allowed = jnp.where(keep, biased, neg).reshape(N_EXPERTS, tm)
    row = lax.broadcasted_iota(jnp.int32, (N_EXPERTS, tm), 0).astype(F32)
    sel = jnp.zeros((N_EXPERTS, tm), F32)
    picks = []
    for _ in range(TOP_K):
        m = jnp.max(allowed, axis=0, keepdims=True)
        first = jnp.min(jnp.where(allowed == m, row, float(N_EXPERTS)), axis=0, keepdims=True)
        hit = row == first
        picks.append((first, hit, jnp.sum(jnp.where(hit, scores, 0.0), axis=0, keepdims=True)))
        sel = jnp.where(hit, 1.0, sel)
        allowed = jnp.where(hit, neg, allowed)
    wsum = picks[0][2]
    for pk in picks[1:]:
        wsum = wsum + pk[2]
    selb = sel.astype(BF16)
    carry = carry_ref[...]
    before = (jnp.dot(selb, tri_ref[...], preferred_element_type=F32)
              + jnp.concatenate([carry] * reps, axis=1))
    total = carry + jnp.dot(selb, jnp.ones((tm, LANES), BF16), preferred_element_type=F32)
    carry_ref[...] = total
    cnt_ref[...] = total
    blank = [jnp.zeros((1, tm), F32)] * (SUBLANES - TOP_K)
    route_ref[...] = jnp.concatenate(
        [pk[0] for pk in picks] + blank
        + [jnp.sum(jnp.where(pk[1], before, 0.0), axis=0, keepdims=True) for pk in picks] + blank
        + [pk[2] / wsum * ROUTED_SCALE for pk in picks] + blank, axis=0)


def _mix(x, of, ob, so, sgb, ag, p):
    T = x.shape[0]
    tm = TOKEN_TILE
    tok = lambda w: pl.BlockSpec((tm, w), lambda i: (i, 0))
    row = lax.broadcasted_iota(jnp.int32, (tm, tm), 0)
    col = lax.broadcasted_iota(jnp.int32, (tm, tm), 1)
    tri = (row < col).astype(BF16)
    return pl.pallas_call(
        _mix_kernel,
        grid=(T // tm,),
        in_specs=[
            tok(D_MODEL), _const_spec((1, D_MODEL)), _const_spec((1, D_MODEL)),
            tok(HG_WIDTH), tok(HG_WIDTH), tok(HG_WIDTH), tok(D_MODEL), tok(D_MODEL),
            _const_spec((1, HG_WIDTH)),
            _const_spec((HG_WIDTH, D_MODEL)), _const_spec((D_MODEL, D_MODEL)),
            _const_spec((1, D_MODEL)), _const_spec((1, D_MODEL)),
            _const_spec((N_EXPERTS, D_MODEL)), _const_spec((N_EXPERTS, LANES)),
            _const_spec((tm, tm)),
        ],
        out_specs=(tok(D_MODEL), tok(D_MODEL // 2), pl.BlockSpec((ROUTE_ROWS, tm), lambda i: (0, i)),
                   pl.BlockSpec((N_EXPERTS, LANES), lambda i: (0, 0))),
        out_shape=(jax.ShapeDtypeStruct((T, D_MODEL), F32),
                   jax.ShapeDtypeStruct((T, D_MODEL // 2), jnp.uint32),
                   jax.ShapeDtypeStruct((ROUTE_ROWS, T), F32),
                   jax.ShapeDtypeStruct((N_EXPERTS, LANES), F32)),
        scratch_shapes=[pltpu.VMEM((N_EXPERTS, LANES), F32)],
        compiler_params=pltpu.CompilerParams(
            dimension_semantics=("arbitrary",), vmem_limit_bytes=VMEM_LIMIT),
        name="mix",
    )(x, p["ln_in_g"], p["ln_in_b"], of, ob, so, sgb, ag, p["hg_norm_g"], p["w_pb"], p["w_o"],
      p["ln1_g"], p["ln1_b"], p["w_router"], p["router_bias"], tri)


def _sc_mesh():
    return plsc.VectorSubcoreMesh(core_axis_name="c", subcore_axis_name="s",
                                  num_cores=SC_CORES, num_subcores=SC_SUBCORES)


def _sc_worker():
    return lax.axis_index("s") * SC_CORES + lax.axis_index("c")


def _sc_dispatch(x1p, dest, pad_rows, n_rows):
    T, w = x1p.shape
    W = SC_WINDOW
    workers = SC_CORES * SC_SUBCORES
    per_worker = T // W // workers
    pad_windows = pad_rows.size // W // workers
    burst = 8

    def body(x_hbm, d_hbm, p_hbm, z_hbm, o_hbm, rows_v, idx_v, pad_v, sem):
        first = _sc_worker() * per_worker
        for k in range(TOP_K):
            pltpu.sync_copy(d_hbm.at[k, pl.ds(first, per_worker)], idx_v.at[k])
        pltpu.sync_copy(p_hbm.at[pl.ds(_sc_worker() * pad_windows, pad_windows)], pad_v)

        @pl.loop(0, per_worker)
        def _(j):
            pltpu.sync_copy(x_hbm.at[pl.ds((first + j) * W, W)], rows_v)
            copies = [pltpu.async_copy(rows_v, o_hbm.at[idx_v.at[k, j]], sem) for k in range(TOP_K)]
            for c in copies:
                c.wait()

        pltpu.sync_copy(z_hbm, rows_v)

        @pl.loop(0, pad_windows, step=burst)
        def _(j):
            copies = [pltpu.async_copy(rows_v, o_hbm.at[pad_v.at[j + u]], sem) for u in range(burst)]
            for c in copies:
                c.wait()

    return pl.kernel(
        body,
        out_type=jax.ShapeDtypeStruct((n_rows + W, w), jnp.uint32),
        mesh=_sc_mesh(),
        scratch_types=[pltpu.VMEM((W, w), jnp.uint32), pltpu.VMEM((TOP_K, per_worker, W), jnp.int32),
                       pltpu.VMEM((pad_windows, W), jnp.int32), pltpu.SemaphoreType.DMA],
        name="sc_dispatch",
    )(x1p, dest.reshape(TOP_K, T // W, W), pad_rows.reshape(-1, W), jnp.zeros((W, w), jnp.uint32))


def _sc_combine(ys, dest, gates):
    T = dest.shape[1]
    W = SC_COMBINE_WINDOW
    w = ys.shape[1]
    per_worker = T // W // (SC_CORES * SC_SUBCORES)
    assert per_worker % 2 == 0 and per_worker * W * SC_CORES * SC_SUBCORES == T

    def body(y_hbm, d_hbm, g_hbm, o_hbm, idx_v, *scratch):
        slots = [dict(rows=scratch[s], gate=scratch[2 + s], out=scratch[4 + s], gsem=scratch[6 + s],
                      wsem=scratch[8 + s]) for s in range(2)]
        first = _sc_worker() * per_worker
        for k in range(TOP_K):
            pltpu.sync_copy(d_hbm.at[k, pl.ds(first * W, per_worker * W)], idx_v.at[k])

        def loads(b, win):
            i = win - first
            return [pltpu.make_async_copy(g_hbm.at[win], b["gate"], b["gsem"])] + [
                pltpu.make_async_copy(y_hbm.at[idx_v.at[k, pl.ds(i * W, W)]],
                                      b["rows"].at[k], b["gsem"]) for k in range(TOP_K)]

        def write_back(b, win):
            return pltpu.make_async_copy(b["out"], o_hbm.at[pl.ds(win * W, W)], b["wsem"])

        def fetch(b, win):
            for c in loads(b, win):
                c.start()

        def reduce_rows(b):
            rows, gate_v, out = b["rows"], b["gate"], b["out"]

            @pl.loop(0, W)
            def _(j):
                gate = [gate_v[pl.ds((j * TOP_K + k) * SC_LANES, SC_LANES)] for k in range(TOP_K)]

                @plsc.parallel_loop(0, w, step=SC_LANES, unroll=4)
                def _(col):
                    lo, hi = [], []
                    for k in range(TOP_K):
                        p = rows[k, j, pl.ds(col, SC_LANES)]
                        lo.append(plsc.bitcast(p << 16, F32) * gate[k])
                        hi.append(plsc.bitcast(p & jnp.uint32(0xFFFF0000), F32) * gate[k])
                    out[j, pl.ds(col, SC_LANES)] = (lo[0] + lo[1]) + (lo[2] + lo[3]) + (lo[4] + lo[5])
                    out[j, pl.ds(w + col, SC_LANES)] = (hi[0] + hi[1]) + (hi[2] + hi[3]) + (hi[4] + hi[5])

        def process(b, win, not_first):
            for c in loads(b, win):
                c.wait()

            @pl.when(not_first)
            def _():
                write_back(b, win).wait()

            reduce_rows(b)
            write_back(b, win).start()

        fetch(slots[0], first)

        @pl.loop(0, per_worker, step=2)
        def _(i):
            win = first + i
            fetch(slots[1], win + 1)
            process(slots[0], win, i > 0)

            @pl.when(i + 2 < per_worker)
            def _():
                fetch(slots[0], win + 2)

            process(slots[1], win + 1, i > 0)

        for b in slots:
            write_back(b, first).wait()

    slot_types = ([pltpu.VMEM((TOP_K, per_worker * W), jnp.int32)]
                  + [pltpu.VMEM((TOP_K, W, w), jnp.uint32)] * 2
                  + [pltpu.VMEM((W * TOP_K * SC_LANES,), F32)] * 2
                  + [pltpu.VMEM((W, 2 * w), F32)] * 2
                  + [pltpu.SemaphoreType.DMA] * 4)
    return pl.kernel(
        body,
        out_type=jax.ShapeDtypeStruct((T, 2 * w), F32),
        mesh=_sc_mesh(),
        scratch_types=slot_types,
        compiler_params=pltpu.CompilerParams(needs_layout_passes=False),
        name="sc_combine",
    )(ys, dest, gates)


def _experts_kernel(be_ref, nv_ref, io_ref, xs_ref, wg_ref, wu_ref, wd_ref, ys_ref, *cast_refs):
    del io_ref
    i = pl.program_id(0)
    n_valid = nv_ref[i]
    mb, w = xs_ref.shape
    if cast_refs:
        @pl.when((i == 0) | (be_ref[i] != be_ref[jnp.maximum(i, 1) - 1]))
        def _():
            for src, dst in zip((wg_ref, wu_ref, wd_ref), cast_refs):
                dst[...] = src[...].astype(BF16)
        wg_ref, wu_ref, wd_ref = cast_refs

    @pl.when(n_valid <= 0)
    def _():
        ys_ref[...] = jnp.zeros_like(ys_ref)

    @pl.when(n_valid > 0)
    def _():
        lo, hi = _unpack_rows(xs_ref[...])
        lo, hi = lo.astype(BF16), hi.astype(BF16)

        def proj(w_ref):
            return (jnp.dot(lo, w_ref[0, :w, :], preferred_element_type=F32)
                    + jnp.dot(hi, w_ref[0, w:, :], preferred_element_type=F32))

        hb = jax.nn.silu(proj(wg_ref)) * proj(wu_ref)
        ys_ref[...] = _pack_rows(jnp.dot(hb.astype(BF16), wd_ref[0], preferred_element_type=F32))


def _experts(xs, blk_exp, n_valid, blk_io, weights):
    n_blocks, mb, w = blk_exp.shape[0], EXPERT_BLOCK, xs.shape[1]
    n_rows = n_blocks * mb
    cast = weights[0].dtype != BF16
    by_expert = lambda shape: pl.BlockSpec((1,) + shape, lambda i, be, nv, io: (be[i], 0, 0))
    w_specs = [by_expert((D_MODEL, EXPERT_DIM)), by_expert((D_MODEL, EXPERT_DIM)),
               by_expert((EXPERT_DIM, D_MODEL))]
    ys_spec = pl.BlockSpec((mb, w), lambda i, be, nv, io: (i, 0))
    ys_shape = jax.ShapeDtypeStruct((n_rows, w), jnp.uint32)
    grid_spec = pltpu.PrefetchScalarGridSpec(
        num_scalar_prefetch=3,
        grid=(n_blocks,),
        in_specs=[pl.BlockSpec((mb, w), lambda i, be, nv, io: (io[i], 0))] + w_specs,
        out_specs=(ys_spec, *w_specs) if cast else ys_spec,
    )
    out = pl.pallas_call(
        _experts_kernel,
        grid_spec=grid_spec,
        out_shape=(ys_shape, *[jax.ShapeDtypeStruct(x.shape, BF16) for x in weights]) if cast else ys_shape,
        compiler_params=pltpu.CompilerParams(
            dimension_semantics=("arbitrary",), vmem_limit_bytes=VMEM_LIMIT),
        name="experts",
    )(blk_exp, n_valid, blk_io, xs, *weights)
    return (out[0], tuple(out[1:])) if cast else (out, weights)


def _final_kernel(x1_ref, routed_ref, wsg_ref, wsu_ref, wsd_ref, l2g_ref, l2b_ref, out_ref):
    x1 = x1_ref[...]
    xb = x1.astype(BF16)
    hs = (jax.nn.silu(jnp.dot(xb, wsg_ref[...], preferred_element_type=F32))
          * jnp.dot(xb, wsu_ref[...], preferred_element_type=F32))
    shared = jnp.dot(hs.astype(BF16), wsd_ref[...], preferred_element_type=F32)
    out_ref[...] = _layer_norm(ALPHA * x1 + (routed_ref[...] + shared), l2g_ref[...], l2b_ref[...])


def _final(x1, routed, p):
    T = x1.shape[0]
    tm = TOKEN_TILE
    tok = lambda w: pl.BlockSpec((tm, w), lambda i: (i, 0))
    return pl.pallas_call(
        _final_kernel,
        grid=(T // tm,),
        in_specs=[
            tok(D_MODEL), tok(D_MODEL),
            _const_spec((D_MODEL, SHARED_DIM)), _const_spec((D_MODEL, SHARED_DIM)),
            _const_spec((SHARED_DIM, D_MODEL)),
            _const_spec((1, D_MODEL)), _const_spec((1, D_MODEL)),
        ],
        out_specs=tok(D_MODEL),
        out_shape=jax.ShapeDtypeStruct((T, D_MODEL), F32),
        compiler_params=pltpu.CompilerParams(
            dimension_semantics=("parallel",), vmem_limit_bytes=VMEM_LIMIT),
        name="final",
    )(x1, routed, p["w_sh_gate"], p["w_sh_up"], p["w_sh_down"], p["ln2_g"], p["ln2_b"])


def _routing_layout(route, counts, n_tokens):
    mb = EXPERT_BLOCK
    n_blocks = -(-n_tokens * TOP_K // mb) + N_EXPERTS
    idx = route[0:TOP_K].astype(jnp.int32)
    rank = route[SUBLANES:SUBLANES + TOP_K].astype(jnp.int32)
    counts = counts.astype(jnp.int32)
    padded = jnp.maximum((counts + mb - 1) // mb, 1) * mb
    pad_end = jnp.cumsum(padded)
    pad_start = pad_end - padded
    experts = jnp.arange(N_EXPERTS, dtype=jnp.int32)
    dest = rank + jnp.sum(jnp.where(idx[:, :, None] == experts, pad_start, 0), axis=-1)
    blk_start = jnp.arange(n_blocks, dtype=jnp.int32) * mb
    blk_exp = jnp.minimum(
        jnp.sum((pad_end[None, :] <= blk_start[:, None]).astype(jnp.int32), axis=1), N_EXPERTS - 1)
    valid_end = jnp.sum(jnp.where(blk_exp[:, None] == experts, pad_start + counts, 0), axis=-1)
    n_valid = jnp.clip(valid_end - blk_start, 0, mb).astype(jnp.int32)
    blk_io = jnp.minimum(jnp.arange(n_blocks, dtype=jnp.int32), pad_end[-1] // mb - 1)
    pad_rows = (pad_start + counts)[:, None] + jnp.arange(mb, dtype=jnp.int32)
    pad_rows = jnp.where(pad_rows < pad_end[:, None], pad_rows, n_blocks * mb)
    gates = jnp.repeat(route[2 * SUBLANES:2 * SUBLANES + TOP_K], SC_LANES, axis=0).T.reshape(
        n_tokens // SC_COMBINE_WINDOW, -1)
    return dest, pad_rows, gates, blk_exp, n_valid, blk_io, n_blocks * mb


def _encode(x, p, expert_weights):
    batch, seq, _ = x.shape
    T = batch * seq
    xt = x.reshape(T, D_MODEL)
    ag, q, gf, gb, iv, so, sgb, gmin = _inproj(xt, p)
    of, ob = _hgrn(q, gf, gb, iv, gmin, batch)
    x1, x1p, route, cnt = _mix(xt, of, ob, so, sgb, ag, p)
    dest, pad_rows, gates, blk_exp, n_valid, blk_io, n_rows = _routing_layout(route, cnt[:, 0], T)
    xs = _sc_dispatch(x1p, dest, pad_rows, n_rows)
    ys, expert_weights = _experts(xs, blk_exp, n_valid, blk_io, expert_weights)
    out = _final(x1, _sc_combine(ys, dest, gates), p)
    return out.reshape(batch, seq, D_MODEL), expert_weights


def _prepare_params(ln_in_g, ln_in_b, w_in, a_ln_g, a_ln_b, a_ws, a_sb, hg_lb_logits, hg_norm_g,
                    w_pa, w_pb, w_o, ln1_g, ln1_b, w_router, router_bias, w_sh_gate, w_sh_up,
                    w_sh_down, ln2_g, ln2_b):
    l = 0
    row = lambda v: v.reshape(1, -1).astype(F32)
    ws = a_ws[l].astype(BF16)
    wsp = jnp.concatenate([ws[0::2], ws[1::2]], axis=2)
    sbf = jnp.repeat(a_sb[l].astype(F32), A_WIDTH // A_GROUPS, axis=1)
    lb = jnp.cumsum(jax.nn.softmax(hg_lb_logits.astype(F32), axis=1), axis=1)[:, l]
    return dict(
        ln_in_g=row(ln_in_g), ln_in_b=row(ln_in_b), w_in=w_in[l].astype(BF16),
        a_ln_g=row(a_ln_g[l]), a_ln_b=row(a_ln_b[l]), wsp=wsp, sbf=sbf, lb=lb,
        w_pa=w_pa[l].astype(BF16), hg_norm_g=row(hg_norm_g[l]),
        w_pb=w_pb[l].astype(BF16), w_o=w_o[l].astype(BF16),
        ln1_g=row(ln1_g[l]), ln1_b=row(ln1_b[l]),
        w_router=w_router[l].T.astype(BF16),
        router_bias=jnp.broadcast_to(router_bias[l].astype(F32)[:, None], (N_EXPERTS, LANES)),
        w_sh_gate=w_sh_gate[l].astype(BF16), w_sh_up=w_sh_up[l].astype(BF16),
        w_sh_down=w_sh_down[l].astype(BF16),
        ln2_g=row(ln2_g[l]), ln2_b=row(ln2_b[l]),
    )


def kernel(x_prompt, x_sample, ln_in_g, ln_in_b, w_in, a_ln_g, a_ln_b, a_ws, a_sb, hg_lb_logits,
           hg_norm_g, w_pa, w_pb, w_o, ln1_g, ln1_b, w_router, router_bias, w_e_gate, w_e_up,
           w_e_down, w_sh_gate, w_sh_up, w_sh_down, ln2_g, ln2_b):
    p = _prepare_params(ln_in_g, ln_in_b, w_in, a_ln_g, a_ln_b, a_ws, a_sb, hg_lb_logits, hg_norm_g,
                        w_pa, w_pb, w_o, ln1_g, ln1_b, w_router, router_bias, w_sh_gate, w_sh_up,
                        w_sh_down, ln2_g, ln2_b)
    y_prompt, expert_weights = _encode(x_prompt, p, (w_e_gate[0], w_e_up[0], w_e_down[0]))
    y_sample, _ = _encode(x_sample, p, expert_weights)
    return y_prompt, y_sample
```

```python
import functools

import jax
import jax.numpy as jnp
from jax import lax
from jax.experimental import pallas as pl
from jax.experimental.pallas import tpu as pltpu
from jax.experimental.pallas import tpu_sc as plsc

F32 = jnp.float32
BF16 = jnp.bfloat16

D_MODEL = 1024
A_GROUPS = 8
A_WIDTH = 512
A_CHUNK = 128
HG_HEADS = 8
HG_DK = 128
HG_WIDTH = HG_HEADS * HG_DK
N_IN = 2 * A_WIDTH + 5 * HG_WIDTH + 2 * D_MODEL
N_EXPERTS = 64
TOP_K = 6
N_GROUPS = 8
TOPK_GROUPS = 4
GROUP_SIZE = N_EXPERTS // N_GROUPS
EXPERT_DIM = 256
SHARED_DIM = 256
ROUTED_SCALE = 2.5
DEPTH = 1
ALPHA = (2.0 * DEPTH) ** 0.25
LN_EPS = 1e-5
RMS_EPS = 1e-6

LANES = 128
SUBLANES = 8
ROUTE_ROWS = 3 * SUBLANES
TOKEN_TILE = 512
INPROJ_TILE = 256
HG_CHUNK = 128
HG_CHUNKS_PER_STEP = 2
HG_SAFE_LOGDECAY = 80.0
EXPERT_BLOCK = 1024
SC_CORES = 2
SC_SUBCORES = 16
SC_LANES = 16
SC_WINDOW = 32
SC_COMBINE_WINDOW = 8
VMEM_LIMIT = 56 * 1024 * 1024

_O_U, _O_V, _O_Q, _O_FF, _O_FB, _O_I, _O_G, _O_GA, _O_GB = (
    0, 512, 1024, 2048, 3072, 4096, 5120, 6144, 7168)


def _layer_norm(x, g, b):
    mu = jnp.mean(x, axis=-1, keepdims=True)
    xc = x - mu
    var = jnp.mean(xc * xc, axis=-1, keepdims=True)
    return xc * lax.rsqrt(var + LN_EPS) * g + b


def _bdot(a, b):
    return jnp.dot(a.astype(BF16), b.astype(BF16), preferred_element_type=F32)


def _pack_rows(x):
    w = x.shape[1] // 2
    lo = lax.bitcast_convert_type(x[:, :w].astype(BF16).astype(F32), jnp.uint32)
    hi = lax.bitcast_convert_type(x[:, w:].astype(BF16).astype(F32), jnp.uint32)
    return hi | (lo >> 16)


def _unpack_rows(p):
    lo = lax.bitcast_convert_type(p << 16, F32)
    hi = lax.bitcast_convert_type(p & jnp.uint32(0xFFFF0000), F32)
    return lo, hi


def _const_spec(shape):
    nd = len(shape)
    return pl.BlockSpec(shape, lambda *_: (0,) * nd, pipeline_mode=pl.Buffered(1))


def _inproj_kernel(x_ref, lng_ref, lnb_ref, win_ref, alng_ref, alnb_ref, wsp_ref, sbf_ref, lb_ref,
                   wpa_ref, ag_ref, q_ref, gf_ref, gb_ref, iv_ref, so_ref, sgb_ref, gmin_ref):
    tm = x_ref.shape[0]
    xb = _layer_norm(x_ref[...], lng_ref[...], lnb_ref[...]).astype(BF16)

    def sec(lo, width):
        return jnp.dot(xb, win_ref[:, lo:lo + width], preferred_element_type=F32)

    v = _layer_norm(jax.nn.gelu(sec(_O_V, A_WIDTH)), alng_ref[...], alnb_ref[...]).astype(BF16)
    u = jax.nn.gelu(sec(_O_U, A_WIDTH))

    q_ref[...] = jax.nn.silu(sec(_O_Q, HG_WIDTH)).astype(BF16)
    mins = []
    for d, (off, g_ref) in enumerate(((_O_FF, gf_ref), (_O_FB, gb_ref))):
        lb = lb_ref[d:d + 1, :]
        f = lb + (1.0 - lb) * jax.nn.sigmoid(sec(off, HG_WIDTH))
        g = jnp.log(f)
        g_ref[...] = g
        half = jnp.sum(g.reshape(tm // (HG_CHUNK // 2), HG_CHUNK // 2, HG_WIDTH), axis=1)
        mins.append(jnp.min(half, axis=-1, keepdims=True))
    gmin_ref[0] = jnp.broadcast_to(jnp.concatenate(mins, axis=0), gmin_ref.shape[1:])
    so_ref[...] = jax.nn.silu(sec(_O_G, HG_WIDTH)).astype(BF16)
    sgb_ref[...] = jax.nn.sigmoid(sec(_O_GB, D_MODEL)).astype(BF16)

    lane = lax.broadcasted_iota(jnp.int32, (A_CHUNK, LANES), 1)
    left = lane < (A_WIDTH // A_GROUPS)
    zero = jnp.zeros((A_CHUNK, LANES), BF16)
    chunks = []
    for c in range(tm // A_CHUNK):
        vc = v[c * A_CHUNK:(c + 1) * A_CHUNK]
        cols = []
        for p in range(A_GROUPS // 2):
            vp = vc[:, p * LANES:(p + 1) * LANES]
            rhs = jnp.concatenate([jnp.where(left, vp, zero), jnp.where(left, zero, vp)], axis=0)
            cols.append(jnp.dot(wsp_ref[p], rhs, preferred_element_type=F32))
        chunks.append(jnp.concatenate(cols, axis=1) + sbf_ref[...])
    mixed = jnp.concatenate(chunks, axis=0)
    a = _bdot(u * mixed, wpa_ref[...])
    ag_ref[...] = (jax.nn.sigmoid(sec(_O_GA, D_MODEL)) * a).astype(BF16)
    iv_ref[...] = sec(_O_I, HG_WIDTH).astype(BF16)


def _inproj(x, p):
    T = x.shape[0]
    tm = INPROJ_TILE
    nt = T // tm
    nh = 2 * tm // (HG_CHUNK // 2)
    tok = lambda w: pl.BlockSpec((tm, w), lambda i: (i, 0))
    outs = (
        jax.ShapeDtypeStruct((T, D_MODEL), BF16),
        jax.ShapeDtypeStruct((T, HG_WIDTH), BF16),
        jax.ShapeDtypeStruct((T, HG_WIDTH), F32),
        jax.ShapeDtypeStruct((T, HG_WIDTH), F32),
        jax.ShapeDtypeStruct((T, HG_WIDTH), BF16),
        jax.ShapeDtypeStruct((T, HG_WIDTH), BF16),
        jax.ShapeDtypeStruct((T, D_MODEL), BF16),
        jax.ShapeDtypeStruct((nt, nh, LANES), F32),
    )
    return pl.pallas_call(
        _inproj_kernel,
        grid=(nt,),
        in_specs=[
            tok(D_MODEL),
            _const_spec((1, D_MODEL)), _const_spec((1, D_MODEL)),
            _const_spec((D_MODEL, N_IN)),
            _const_spec((1, A_WIDTH)), _const_spec((1, A_WIDTH)),
            _const_spec((A_GROUPS // 2, A_CHUNK, 2 * A_CHUNK)),
            _const_spec((A_CHUNK, A_WIDTH)),
            _const_spec((2, HG_WIDTH)),
            _const_spec((A_WIDTH, D_MODEL)),
        ],
        out_specs=(tok(D_MODEL), tok(HG_WIDTH), tok(HG_WIDTH), tok(HG_WIDTH), tok(HG_WIDTH),
                   tok(HG_WIDTH), tok(D_MODEL), pl.BlockSpec((1, nh, LANES), lambda i: (i, 0, 0))),
        out_shape=outs,
        compiler_params=pltpu.CompilerParams(
            dimension_semantics=("parallel",), vmem_limit_bytes=VMEM_LIMIT),
        name="inproj",
    )(x, p["ln_in_g"], p["ln_in_b"], p["w_in"], p["a_ln_g"], p["a_ln_b"], p["wsp"], p["sbf"],
      p["lb"], p["w_pa"])


_NT = (((1,), (1,)), ((), ()))
_TN = (((0,), (0,)), ((), ()))


def _hgrn_direction(q_ref, g_ref, v_ref, st_ref, o_ref, tri, mask, fwd):
    C = q_ref.shape[0]
    g = g_ref[...]
    ghi = g.astype(BF16)
    glo = (g - ghi.astype(F32)).astype(BF16)
    b = jnp.dot(jnp.concatenate([tri, tri], axis=1), jnp.concatenate([ghi, glo], axis=0),
                preferred_element_type=F32)
    mid = C // 2 - 1 if fwd else C // 2
    end = C - 1 if fwd else 0
    r = b[mid:mid + 1, :]
    b_end = b[end:end + 1, :]
    qt = q_ref[...].astype(F32) * jnp.exp(b - r)
    kt = (1.0 - jnp.exp(g)) * jnp.exp(r - b)
    qtb = qt.astype(BF16)
    ktb = kt.astype(BF16)
    qhb = (qt * jnp.exp(r)).astype(BF16)
    khb = (kt * jnp.exp(b_end - r)).astype(BF16)
    decay = jnp.exp(b_end)
    v = v_ref[...]
    heads =[slice(h * HG_DK, (h + 1) * HG_DK) for h in range(HG_HEADS)]
    zero = jnp.zeros((C, HG_DK), BF16)
    mask2 = jnp.concatenate([mask, mask], axis=1)
    scores = []
    for p in range(HG_HEADS // 2):
        k1, k2 = ktb[:, heads[2 * p]], ktb[:, heads[2 * p + 1]]
        kk = jnp.concatenate([jnp.concatenate([k1, zero], axis=1),
                              jnp.concatenate([zero, k2], axis=1)], axis=0)
        s2 = lax.dot_general(qtb[:, 2 * p * HG_DK:(2 * p + 2) * HG_DK], kk, _NT,
                             preferred_element_type=F32)
        s2 = jnp.where(mask2, s2, 0.0).astype(BF16)
        scores += [s2[:, :C], s2[:, C:]]
    for h, sl in enumerate(heads):
        st = st_ref[h]
        vt = v[:, sl].T
        o_ref[:, sl] = lax.dot_general(
            jnp.concatenate([scores[h], qhb[:, sl]], axis=1),
            jnp.concatenate([vt, st.astype(BF16)], axis=1), _NT,
            preferred_element_type=F32).astype(o_ref.dtype)
        st_ref[h] = st * decay[:, sl] + jnp.dot(vt, khb[:, sl], preferred_element_type=F32)


def _hgrn_direction_stepwise(q_ref, g_ref, v_ref, st_ref, o_ref, q32_ref, v32_ref, o32_ref, fwd):
    C = q_ref.shape[0]
    q32_ref[...] = q_ref[...].astype(F32)
    v32_ref[...] = v_ref[...].astype(F32)
    sub = SUBLANES
    rows = lax.broadcasted_iota(jnp.int32, (sub, HG_DK), 0)

    def group(i, carry):
        base = pl.multiple_of((i if fwd else C // sub - 1 - i) * sub, sub)
        f = jnp.exp(g_ref[pl.ds(base, sub), :])
        k = 1.0 - f
        q = q32_ref[pl.ds(base, sub), :]
        v = v32_ref[pl.ds(base, sub), :]
        for h in range(HG_HEADS):
            sl = slice(h * HG_DK, (h + 1) * HG_DK)
            st = st_ref[h]
            out = jnp.zeros((sub, HG_DK), F32)
            for r in (range(sub) if fwd else range(sub - 1, -1, -1)):
                v_t = jnp.where(rows == 0, v[r:r + 1, sl], 0.0).astype(BF16)
                k_t = jnp.broadcast_to(k[r:r + 1, sl], (sub, HG_DK)).astype(BF16)
                st = st * f[r:r + 1, sl] + lax.dot_general(v_t, k_t, _TN, preferred_element_type=F32)
                q_t = jnp.broadcast_to(q[r:r + 1, sl], (sub, HG_DK)).astype(BF16)
                o_t = lax.dot_general(q_t, st.astype(BF16), _NT, preferred_element_type=F32)
                out = jnp.where(rows == r, o_t, out)
            st_ref[h] = st
            o32_ref[pl.ds(base, sub), sl] = out
        return carry

    lax.fori_loop(0, C // sub, group, 0)
    o_ref[...] = o32_ref[...].astype(o_ref.dtype)


def _hgrn_kernel(safe_ref, qf_ref, qb_ref, gf_ref, gb_ref, vf_ref, vb_ref, tril_ref, triu_ref,
                 of_ref, ob_ref, sf_ref, sb_ref, q32_ref, v32_ref, o32_ref):
    b, j = pl.program_id(0), pl.program_id(1)
    ns = pl.num_programs(1)
    n = HG_CHUNKS_PER_STEP
    C = qf_ref.shape[0] // n

    @pl.when(j == 0)
    def _():
        sf_ref[...] = jnp.zeros_like(sf_ref)
        sb_ref[...] = jnp.zeros_like(sb_ref)

    row = lax.broadcasted_iota(jnp.int32, (C, C), 0)
    col = lax.broadcasted_iota(jnp.int32, (C, C), 1)
    safe_f = [safe_ref[0, (b * ns + j) * n + u] != 0 for u in range(n)]
    safe_b = [safe_ref[1, (b * ns + ns - 1 - j) * n + u] != 0 for u in range(n)]
    part = lambda ref, u: ref.at[pl.ds(u * C, C), :]

    def forward(u, stepwise):
        refs = (part(qf_ref, u), part(gf_ref, u), part(vf_ref, u), sf_ref, part(of_ref, u))
        if stepwise:
            _hgrn_direction_stepwise(*refs, q32_ref, v32_ref, o32_ref, True)
        else:
            _hgrn_direction(*refs, tril_ref[...], row >= col, True)

    def backward(u, stepwise):
        refs = (part(qb_ref, u), part(gb_ref, u), part(vb_ref, u), sb_ref, part(ob_ref, u))
        if stepwise:
            _hgrn_direction_stepwise(*refs, q32_ref, v32_ref, o32_ref, False)
        else:
            _hgrn_direction(*refs, triu_ref[...], row <= col, False)

    all_safe = functools.reduce(jnp.logical_and, safe_f + safe_b)

    @pl.when(all_safe)
    def _():
        for u in range(n):
            forward(u, False)
            backward(n - 1 - u, False)

    @pl.when(jnp.logical_not(all_safe))
    def _():
        for u in range(n):
            pl.when(safe_f[u])(functools.partial(forward, u, False))
            pl.when(jnp.logical_not(safe_f[u]))(functools.partial(forward, u, True))
        for u in reversed(range(n)):
            pl.when(safe_b[u])(functools.partial(backward, u, False))
            pl.when(jnp.logical_not(safe_b[u]))(functools.partial(backward, u, True))


def _hgrn(q, gf, gb, iv, gmin, batch):
    T = q.shape[0]
    C = HG_CHUNK
    nc = T // batch // C
    nt, nh, _ = gmin.shape
    halves = gmin[:, :, 0].reshape(nt, 2, nh // 4, 2)
    safe = (jnp.min(halves, axis=-1) > -HG_SAFE_LOGDECAY).astype(jnp.int32)
    safe = safe.transpose(1, 0, 2).reshape(2, T // C)
    n = HG_CHUNKS_PER_STEP
    ns = nc // n
    fwd = pl.BlockSpec((n * C, HG_WIDTH), lambda b, j, s: (b * ns + j, 0))
    bwd = pl.BlockSpec((n * C, HG_WIDTH), lambda b, j, s: (b * ns + ns - 1 - j, 0))
    const = lambda shape: pl.BlockSpec(shape, lambda b, j, s: (0,) * len(shape),
                                       pipeline_mode=pl.Buffered(1))
    row = lax.broadcasted_iota(jnp.int32, (C, C), 0)
    col = lax.broadcasted_iota(jnp.int32, (C, C), 1)
    tril = (row >= col).astype(BF16)
    triu = (row <= col).astype(BF16)
    grid_spec = pltpu.PrefetchScalarGridSpec(
        num_scalar_prefetch=1,
        grid=(batch, ns),
        in_specs=[fwd, bwd, fwd, bwd, fwd, bwd, const((C, C)), const((C, C))],
        out_specs=(fwd, bwd),
        scratch_shapes=[pltpu.VMEM((HG_HEADS, HG_DK, HG_DK), F32),
                        pltpu.VMEM((HG_HEADS, HG_DK, HG_DK), F32),
                        pltpu.VMEM((C, HG_WIDTH), F32), pltpu.VMEM((C, HG_WIDTH), F32),
                        pltpu.VMEM((C, HG_WIDTH), F32)],
    )
    return pl.pallas_call(
        _hgrn_kernel,
        grid_spec=grid_spec,
        out_shape=(jax.ShapeDtypeStruct((T, HG_WIDTH), BF16), jax.ShapeDtypeStruct((T, HG_WIDTH), BF16)),
        compiler_params=pltpu.CompilerParams(
            dimension_semantics=("parallel", "arbitrary"), vmem_limit_bytes=VMEM_LIMIT),
        name="hgrn",
    )(safe, q, q, gf, gb, iv, iv, tril, triu)


def _mix_kernel(x_ref, lng_ref, lnb_ref, of_ref, ob_ref, so_ref, sgb_ref, ag_ref, ng_ref, wpb_ref,
                wo_ref, l1g_ref, l1b_ref, wr_ref, rb_ref, tri_ref, x1_ref, x1p_ref, route_ref,
                cnt_ref, carry_ref):
    tm = x_ref.shape[0]

    @pl.when(pl.program_id(0) == 0)
    def _():
        carry_ref[...] = jnp.zeros_like(carry_ref)

    o = of_ref[...].astype(F32) + ob_ref[...].astype(F32)
    heads = []
    for h in range(HG_HEADS):
        oh = o[:, h * HG_DK:(h + 1) * HG_DK]
        heads.append(oh * lax.rsqrt(jnp.mean(oh * oh, axis=-1, keepdims=True) + RMS_EPS))
    rn = jnp.concatenate(heads, axis=1) * ng_ref[...] * so_ref[...].astype(F32)
    r = _bdot(rn, wpb_ref[...])
    mixed = ag_ref[...].astype(F32) + sgb_ref[...].astype(F32) * r
    y = _bdot(mixed, wo_ref[...])
    xn = _layer_norm(x_ref[...], lng_ref[...], lnb_ref[...])
    x1 = _layer_norm(ALPHA * xn + y, l1g_ref[...], l1b_ref[...])
    x1_ref[...] = x1
    x1p_ref[...] = _pack_rows(x1)

    neg = jnp.float32(-jnp.inf)
    reps = tm // LANES
    scores = jax.nn.sigmoid(lax.dot_general(wr_ref[...], x1.astype(BF16), _NT,
                                            preferred_element_type=F32))
    biased = (scores + jnp.concatenate([rb_ref[...]] * reps, axis=1)).reshape(
        N_GROUPS, GROUP_SIZE, tm)
    sub = lax.broadcasted_iota(jnp.int32, biased.shape, 1).astype(F32)
    m1 = jnp.max(biased, axis=1, keepdims=True)
    first = jnp.min(jnp.where(biased == m1, sub, float(GROUP_SIZE)), axis=1, keepdims=True)
    m2 = jnp.max(jnp.where(sub == first, neg, biased), axis=1, keepdims=True)
    gs = (m1 + m2).reshape(N_GROUPS, tm)
    grp = lax.broadcasted_iota(jnp.int32, (N_GROUPS, tm), 0)
    ahead = jnp.zeros((N_GROUPS, tm), F32)
    for d in range(1, N_GROUPS):
        other = pltpu.roll(gs, d, 0)
        tie = jnp.where(grp >= d, 1.0, 0.0)
        ahead = ahead + jnp.where(other > gs, 1.0, jnp.where(other == gs, tie, 0.0))
    keep = (ahead < TOPK_GROUPS).reshape(N_GROUPS, 1, tm)
    allowed = jnp.where(keep, biased, neg).reshape(N_EXPERTS, tm)
    row = lax.broadcasted_iota(jnp.int32, (N_EXPERTS, tm), 0).astype(F32)
    sel = jnp.zeros((N_EXPERTS, tm), F32)
    picks = []
    for _ in range(TOP_K):
        m = jnp.max(allowed, axis=0, keepdims=True)
        first = jnp.min(jnp.where(allowed == m, row, float(N_EXPERTS)), axis=0, keepdims=True)
        hit = row == first
        picks.append((first, hit, jnp.sum(jnp.where(hit, scores, 0.0), axis=0, keepdims=True)))
        sel = jnp.where(hit, 1.0, sel)
        allowed = jnp.where(hit, neg, allowed)
    wsum = picks[0][2]
    for pk in picks[1:]:
        wsum = wsum + pk[2]
    selb = sel.astype(BF16)
    carry = carry_ref[...]
    before = (jnp.dot(selb, tri_ref[...], preferred_element_type=F32)
              + jnp.concatenate([carry] * reps, axis=1))
    total = carry + jnp.dot(selb, jnp.ones((tm, LANES), BF16), preferred_element_type=F32)
    carry_ref[...] = total
    cnt_ref[...] = total
    blank = [jnp.zeros((1, tm), F32)] * (SUBLANES - TOP_K)
    route_ref[...] = jnp.concatenate(
        [pk[0] for pk in picks] + blank
        + [jnp.sum(jnp.where(pk[1], before, 0.0), axis=0, keepdims=True) for pk in picks] + blank
        + [pk[2] / wsum * ROUTED_SCALE for pk in picks] + blank, axis=0)


def _mix(x, of, ob, so, sgb, ag, p):
    T = x.shape[0]
    tm = TOKEN_TILE
    tok = lambda w: pl.BlockSpec((tm, w), lambda i: (i, 0))
    row = lax.broadcasted_iota(jnp.int32, (tm, tm), 0)
    col = lax.broadcasted_iota(jnp.int32, (tm, tm), 1)
    tri = (row < col).astype(BF16)
    return pl.pallas_call(
        _mix_kernel,
        grid=(T // tm,),
        in_specs=[
            tok(D_MODEL), _const_spec((1, D_MODEL)), _const_spec((1, D_MODEL)),
            tok(HG_WIDTH), tok(HG_WIDTH), tok(HG_WIDTH), tok(D_MODEL), tok(D_MODEL),
            _const_spec((1, HG_WIDTH)),
            _const_spec((HG_WIDTH, D_MODEL)), _const_spec((D_MODEL, D_MODEL)),
            _const_spec((1, D_MODEL)), _const_spec((1, D_MODEL)),
            _const_spec((N_EXPERTS, D_MODEL)), _const_spec((N_EXPERTS, LANES)),
            _const_spec((tm, tm)),
        ],
        out_specs=(tok(D_MODEL), tok(D_MODEL // 2), pl.BlockSpec((ROUTE_ROWS, tm), lambda i: (0, i)),
                   pl.BlockSpec((N_EXPERTS, LANES), lambda i: (0, 0))),
        out_shape=(jax.ShapeDtypeStruct((T, D_MODEL), F32),
                   jax.ShapeDtypeStruct((T, D_MODEL // 2), jnp.uint32),
                   jax.ShapeDtypeStruct((ROUTE_ROWS, T), F32),
                   jax.ShapeDtypeStruct((N_EXPERTS, LANES), F32)),
        scratch_shapes=[pltpu.VMEM((N_EXPERTS, LANES), F32)],
        compiler_params=pltpu.CompilerParams(
            dimension_semantics=("arbitrary",), vmem_limit_bytes=VMEM_LIMIT),
        name="mix",
    )(x, p["ln_in_g"], p["ln_in_b"], of, ob, so, sgb, ag, p["hg_norm_g"], p["w_pb"], p["w_o"],
      p["ln1_g"], p["ln1_b"], p["w_router"], p["router_bias"], tri)


def _sc_mesh():
    return plsc.VectorSubcoreMesh(core_axis_name="c", subcore_axis_name="s",
                                  num_cores=SC_CORES, num_subcores=SC_SUBCORES)


def _sc_worker():
    return lax.axis_index("s") * SC_CORES + lax.axis_index("c")


def _sc_dispatch(x1p, dest, pad_rows, n_rows):
    T, w = x1p.shape
    W = SC_WINDOW
    workers = SC_CORES * SC_SUBCORES
    per_worker = T // W // workers
    pad_windows = pad_rows.size // W // workers
    burst = 8

    def body(x_hbm, d_hbm, p_hbm, z_hbm, o_hbm, rows_v, idx_v, pad_v, sem):
        first = _sc_worker() * per_worker
        for k in range(TOP_K):
            pltpu.sync_copy(d_hbm.at[k, pl.ds(first, per_worker)], idx_v.at[k])
        pltpu.sync_copy(p_hbm.at[pl.ds(_sc_worker() * pad_windows, pad_windows)], pad_v)

        @pl.loop(0, per_worker)
        def _(j):
            pltpu.sync_copy(x_hbm.at[pl.ds((first + j) * W, W)], rows_v)
            copies = [pltpu.async_copy(rows_v, o_hbm.at[idx_v.at[k, j]], sem) for k in range(TOP_K)]
            for c in copies:
                c.wait()

        pltpu.sync_copy(z_hbm, rows_v)

        @pl.loop(0, pad_windows, step=burst)
        def _(j):
            copies = [pltpu.async_copy(rows_v, o_hbm.at[pad_v.at[j + u]], sem) for u in range(burst)]
            for c in copies:
                c.wait()

    return pl.kernel(
        body,
        out_type=jax.ShapeDtypeStruct((n_rows + EXPERT_BLOCK, w), jnp.uint32),
        mesh=_sc_mesh(),
        scratch_types=[pltpu.VMEM((W, w), jnp.uint32), pltpu.VMEM((TOP_K, per_worker, W), jnp.int32),
                       pltpu.VMEM((pad_windows, W), jnp.int32), pltpu.SemaphoreType.DMA],
        name="sc_dispatch",
    )(x1p, dest.reshape(TOP_K, T // W, W), pad_rows.reshape(-1, W), jnp.zeros((W, w), jnp.uint32))


def _sc_combine(ys, dest, gates):
    T = dest.shape[1]
    W = SC_COMBINE_WINDOW
    w = ys.shape[1]
    per_worker = T // W // (SC_CORES * SC_SUBCORES)
    assert per_worker % 2 == 0 and per_worker * W * SC_CORES * SC_SUBCORES == T

    def body(y_hbm, d_hbm, g_hbm, o_hbm, idx_v, *scratch):
        slots = [dict(rows=scratch[s], gate=scratch[2 + s], out=scratch[4 + s], gsem=scratch[6 + s],
                      wsem=scratch[8 + s]) for s in range(2)]
        first = _sc_worker() * per_worker
        for k in range(TOP_K):
            pltpu.sync_copy(d_hbm.at[k, pl.ds(first * W, per_worker * W)], idx_v.at[k])

        def loads(b, win):
            i = win - first
            return [pltpu.make_async_copy(g_hbm.at[win], b["gate"], b["gsem"])] + [
                pltpu.make_async_copy(y_hbm.at[idx_v.at[k, pl.ds(i * W, W)]],
                                      b["rows"].at[k], b["gsem"]) for k in range(TOP_K)]

        def write_back(b, win):
            return pltpu.make_async_copy(b["out"], o_hbm.at[pl.ds(win * W, W)], b["wsem"])

        def fetch(b, win):
            for c in loads(b, win):
                c.start()

        def reduce_rows(b):
            rows, gate_v, out = b["rows"], b["gate"], b["out"]

            @pl.loop(0, W)
            def _(j):
                gate = [gate_v[pl.ds((j * TOP_K + k) * SC_LANES, SC_LANES)] for k in range(TOP_K)]

                @plsc.parallel_loop(0, w, step=SC_LANES, unroll=4)
                def _(col):
                    lo, hi = [], []
                    for k in range(TOP_K):
                        p = rows[k, j, pl.ds(col, SC_LANES)]
                        lo.append(plsc.bitcast(p << 16, F32) * gate[k])
                        hi.append(plsc.bitcast(p & jnp.uint32(0xFFFF0000), F32) * gate[k])
                    out[j, pl.ds(col, SC_LANES)] = (lo[0] + lo[1]) + (lo[2] + lo[3]) + (lo[4] + lo[5])
                    out[j, pl.ds(w + col, SC_LANES)] = (hi[0] + hi[1]) + (hi[2] + hi[3]) + (hi[4] + hi[5])

        def process(b, win, not_first):
            for c in loads(b, win):
                c.wait()

            @pl.when(not_first)
            def _():
                write_back(b, win).wait()

            reduce_rows(b)
            write_back(b, win).start()

        fetch(slots[0], first)

        @pl.loop(0, per_worker, step=2)
        def _(i):
            win = first + i
            fetch(slots[1], win + 1)
            process(slots[0], win, i > 0)

            @pl.when(i + 2 < per_worker)
            def _():
                fetch(slots[0], win + 2)

            process(slots[1], win + 1, i > 0)

        for b in slots:
            write_back(b, first).wait()

    slot_types = ([pltpu.VMEM((TOP_K, per_worker * W), jnp.int32)]
                  + [pltpu.VMEM((TOP_K, W, w), jnp.uint32)] * 2
                  + [pltpu.VMEM((W * TOP_K * SC_LANES,), F32)] * 2
                  + [pltpu.VMEM((W, 2 * w), F32)] * 2
                  + [pltpu.SemaphoreType.DMA] * 4)
    return pl.kernel(
        body,
        out_type=jax.ShapeDtypeStruct((T, 2 * w), F32),
        mesh=_sc_mesh(),
        scratch_types=slot_types,
        compiler_params=pltpu.CompilerParams(needs_layout_passes=False),
        name="sc_combine",
    )(ys, dest, gates)


def _experts_kernel(be_ref, nv_ref, io_ref, xs_ref, wg_ref, wu_ref, wd_ref, ys_ref, *cast_refs):
    del io_ref
    i = pl.program_id(0)
    n_valid = nv_ref[i]
    mb, w = xs_ref.shape
    if cast_refs:
        @pl.when((i == 0) | (be_ref[i] != be_ref[jnp.maximum(i, 1) - 1]))
        def _():
            for src, dst in zip((wg_ref, wu_ref, wd_ref), cast_refs):
                dst[...] = src[...].astype(BF16)
        wg_ref, wu_ref, wd_ref = cast_refs

    @pl.when(n_valid <= 0)
    def _():
        ys_ref[...] = jnp.zeros_like(ys_ref)

    @pl.when(n_valid > 0)
    def _():
        lo, hi = _unpack_rows(xs_ref[...])
        lo, hi = lo.astype(BF16), hi.astype(BF16)

        def proj(w_ref):
            return (jnp.dot(lo, w_ref[0, :w, :], preferred_element_type=F32)
                    + jnp.dot(hi, w_ref[0, w:, :], preferred_element_type=F32))

        hb = jax.nn.silu(proj(wg_ref)) * proj(wu_ref)
        ys_ref[...] = _pack_rows(jnp.dot(hb.astype(BF16), wd_ref[0], preferred_element_type=F32))


def _experts(xs, blk_exp, n_valid, blk_io, weights):
    n_blocks, mb, w = blk_exp.shape[0], EXPERT_BLOCK, xs.shape[1]
    n_rows = n_blocks * mb
    cast = weights[0].dtype != BF16
    by_expert = lambda shape: pl.BlockSpec((1,) + shape, lambda i, be, nv, io: (be[i], 0, 0))
    w_specs = [by_expert((D_MODEL, EXPERT_DIM)), by_expert((D_MODEL, EXPERT_DIM)),
               by_expert((EXPERT_DIM, D_MODEL))]
    ys_spec = pl.BlockSpec((mb, w), lambda i, be, nv, io: (i, 0))
    ys_shape = jax.ShapeDtypeStruct((n_rows, w), jnp.uint32)
    grid_spec = pltpu.PrefetchScalarGridSpec(
        num_scalar_prefetch=3,
        grid=(n_blocks,),
        in_specs=[pl.BlockSpec((mb, w), lambda i, be, nv, io: (io[i], 0))] + w_specs,
        out_specs=(ys_spec, *w_specs) if cast else ys_spec,
    )
    out = pl.pallas_call(
        _experts_kernel,
        grid_spec=grid_spec,
        out_shape=(ys_shape, *[jax.ShapeDtypeStruct(x.shape, BF16) for x in weights]) if cast else ys_shape,
        compiler_params=pltpu.CompilerParams(
            dimension_semantics=("arbitrary",), vmem_limit_bytes=VMEM_LIMIT),
        name="experts",
    )(blk_exp, n_valid, blk_io, xs, *weights)
    return (out[0], tuple(out[1:])) if cast else (out, weights)


def _final_kernel(x1_ref, routed_ref, wsg_ref, wsu_ref, wsd_ref, l2g_ref, l2b_ref, out_ref):
    x1 = x1_ref[...]
    xb = x1.astype(BF16)
    hs = (jax.nn.silu(jnp.dot(xb, wsg_ref[...], preferred_element_type=F32))
          * jnp.dot(xb, wsu_ref[...], preferred_element_type=F32))
    shared = jnp.dot(hs.astype(BF16), wsd_ref[...], preferred_element_type=F32)
    out_ref[...] = _layer_norm(ALPHA * x1 + (routed_ref[...] + shared), l2g_ref[...], l2b_ref[...])


def _final(x1, routed, p):
    T = x1.shape[0]
    tm = TOKEN_TILE
    tok = lambda w: pl.BlockSpec((tm, w), lambda i: (i, 0))
    return pl.pallas_call(
        _final_kernel,
        grid=(T // tm,),
        in_specs=[
            tok(D_MODEL), tok(D_MODEL),
            _const_spec((D_MODEL, SHARED_DIM)), _const_spec((D_MODEL, SHARED_DIM)),
            _const_spec((SHARED_DIM, D_MODEL)),
            _const_spec((1, D_MODEL)), _const_spec((1, D_MODEL)),
        ],
        out_specs=tok(D_MODEL),
        out_shape=jax.ShapeDtypeStruct((T, D_MODEL), F32),
        compiler_params=pltpu.CompilerParams(
            dimension_semantics=("parallel",), vmem_limit_bytes=VMEM_LIMIT),
        name="final",
    )(x1, routed, p["w_sh_gate"], p["w_sh_up"], p["w_sh_down"], p["ln2_g"], p["ln2_b"])


def _routing_layout(route, counts, n_tokens):
    mb = EXPERT_BLOCK
    n_blocks = -(-n_tokens * TOP_K // mb) + N_EXPERTS
    idx = route[0:TOP_K].astype(jnp.int32)
    rank = route[SUBLANES:SUBLANES + TOP_K].astype(jnp.int32)
    counts = counts.astype(jnp.int32)
    padded = jnp.maximum((counts + mb - 1) // mb, 1) * mb
    pad_end = jnp.cumsum(padded)
    pad_start = pad_end - padded
    experts = jnp.arange(N_EXPERTS, dtype=jnp.int32)
    dest = rank + jnp.sum(jnp.where(idx[:, :, None] == experts, pad_start, 0), axis=-1)
    blk_start = jnp.arange(n_blocks, dtype=jnp.int32) * mb
    blk_exp = jnp.minimum(
        jnp.sum((pad_end[None, :] <= blk_start[:, None]).astype(jnp.int32), axis=1), N_EXPERTS - 1)
    valid_end = jnp.sum(jnp.where(blk_exp[:, None] == experts, pad_start + counts, 0), axis=-1)
    n_valid = jnp.clip(valid_end - blk_start, 0, mb).astype(jnp.int32)
    blk_io = jnp.minimum(jnp.arange(n_blocks, dtype=jnp.int32), pad_end[-1] // mb - 1)
    spare = n_blocks * mb + jnp.arange(mb, dtype=jnp.int32)
    pad_rows = (pad_start + counts)[:, None] + jnp.arange(mb, dtype=jnp.int32)
    pad_rows = jnp.where(pad_rows < pad_end[:, None], pad_rows, spare)
    gates = jnp.repeat(route[2 * SUBLANES:2 * SUBLANES + TOP_K], SC_LANES, axis=0).T.reshape(
        n_tokens // SC_COMBINE_WINDOW, -1)
    return dest, pad_rows, gates, blk_exp, n_valid, blk_io, n_blocks * mb


def _encode(x, p, expert_weights):
    batch, seq, _ = x.shape
    T = batch * seq
    xt = x.reshape(T, D_MODEL)
    ag, q, gf, gb, iv, so, sgb, gmin = _inproj(xt, p)
    of, ob = _hgrn(q, gf, gb, iv, gmin, batch)
    x1, x1p, route, cnt = _mix(xt, of, ob, so, sgb, ag, p)
    dest, pad_rows, gates, blk_exp, n_valid, blk_io, n_rows = _routing_layout(route, cnt[:, 0], T)
    xs = _sc_dispatch(x1p, dest, pad_rows, n_rows)
    ys, expert_weights = _experts(xs, blk_exp, n_valid, blk_io, expert_weights)
    out = _final(x1, _sc_combine(ys, dest, gates), p)
    return out.reshape(batch, seq, D_MODEL), expert_weights


def _prepare_params(ln_in_g, ln_in_b, w_in, a_ln_g, a_ln_b, a_ws, a_sb, hg_lb_logits, hg_norm_g,
                    w_pa, w_pb, w_o, ln1_g, ln1_b, w_router, router_bias, w_sh_gate, w_sh_up,
                    w_sh_down, ln2_g, ln2_b):
    l = 0
    row = lambda v: v.reshape(1, -1).astype(F32)
    ws = a_ws[l].astype(BF16)
    wsp = jnp.concatenate([ws[0::2], ws[1::2]], axis=2)
    sbf = jnp.repeat(a_sb[l].astype(F32), A_WIDTH // A_GROUPS, axis=1)
    lb = jnp.cumsum(jax.nn.softmax(hg_lb_logits.astype(F32), axis=1), axis=1)[:, l]
    return dict(
        ln_in_g=row(ln_in_g), ln_in_b=row(ln_in_b), w_in=w_in[l].astype(BF16),
        a_ln_g=row(a_ln_g[l]), a_ln_b=row(a_ln_b[l]), wsp=wsp, sbf=sbf, lb=lb,
        w_pa=w_pa[l].astype(BF16), hg_norm_g=row(hg_norm_g[l]),
        w_pb=w_pb[l].astype(BF16), w_o=w_o[l].astype(BF16),
        ln1_g=row(ln1_g[l]), ln1_b=row(ln1_b[l]),
        w_router=w_router[l].T.astype(BF16),
        router_bias=jnp.broadcast_to(router_bias[l].astype(F32)[:, None], (N_EXPERTS, LANES)),
        w_sh_gate=w_sh_gate[l].astype(BF16), w_sh_up=w_sh_up[l].astype(BF16),
        w_sh_down=w_sh_down[l].astype(BF16),
        ln2_g=row(ln2_g[l]), ln2_b=row(ln2_b[l]),
    )


def kernel(x_prompt, x_sample, ln_in_g, ln_in_b, w_in, a_ln_g, a_ln_b, a_ws, a_sb, hg_lb_logits,
           hg_norm_g, w_pa, w_pb, w_o, ln1_g, ln1_b, w_router, router_bias, w_e_gate, w_e_up,
           w_e_down, w_sh_gate, w_sh_up, w_sh_down, ln2_g, ln2_b):
    p = _prepare_params(ln_in_g, ln_in_b, w_in, a_ln_g, a_ln_b, a_ws, a_sb, hg_lb_logits, hg_norm_g,
                        w_pa, w_pb, w_o, ln1_g, ln1_b, w_router, router_bias, w_sh_gate, w_sh_up,
                        w_sh_down, ln2_g, ln2_b)
    y_prompt, expert_weights = _encode(x_prompt, p, (w_e_gate[0], w_e_up[0], w_e_down[0]))
    y_sample, _ = _encode(x_sample, p, expert_weights)
    return y_prompt, y_sample
```

```python
import functools

import jax
import jax.numpy as jnp
from jax import lax
from jax.experimental import pallas as pl
from jax.experimental.pallas import tpu as pltpu
from jax.experimental.pallas import tpu_sc as plsc

F32 = jnp.float32
BF16 = jnp.bfloat16

D_MODEL = 1024
A_GROUPS = 8
A_WIDTH = 512
A_CHUNK = 128
HG_HEADS = 8
HG_DK = 128
HG_WIDTH = HG_HEADS * HG_DK
N_IN = 2 * A_WIDTH + 5 * HG_WIDTH + 2 * D_MODEL
N_EXPERTS = 64
TOP_K = 6
N_GROUPS = 8
TOPK_GROUPS = 4
GROUP_SIZE = N_EXPERTS // N_GROUPS
EXPERT_DIM = 256
SHARED_DIM = 256
ROUTED_SCALE = 2.5
DEPTH = 1
ALPHA = (2.0 * DEPTH) ** 0.25
LN_EPS = 1e-5
RMS_EPS = 1e-6

LANES = 128
SUBLANES = 8
ROUTE_ROWS = 3 * SUBLANES
TOKEN_TILE = 512
FINAL_TILE = 1024
INPROJ_TILE = 256
HG_CHUNK = 128
HG_CHUNKS_PER_STEP = 2
HG_SAFE_LOGDECAY = 80.0
EXPERT_BLOCK = 1024
SC_CORES = 2
SC_SUBCORES = 16
SC_LANES = 16
SC_WINDOW = 32
SC_COMBINE_WINDOW = 8
VMEM_LIMIT = 56 * 1024 * 1024

_O_U, _O_V, _O_Q, _O_FF, _O_FB, _O_I, _O_G, _O_GA, _O_GB = (
    0, 512, 1024, 2048, 3072, 4096, 5120, 6144, 7168)


def _layer_norm(x, g, b):
    mu = jnp.mean(x, axis=-1, keepdims=True)
    xc = x - mu
    var = jnp.mean(xc * xc, axis=-1, keepdims=True)
    return xc * lax.rsqrt(var + LN_EPS) * g + b


def _bdot(a, b):
    return jnp.dot(a.astype(BF16), b.astype(BF16), preferred_element_type=F32)


def _pack_rows(x):
    w = x.shape[1] // 2
    lo = lax.bitcast_convert_type(x[:, :w].astype(BF16).astype(F32), jnp.uint32)
    hi = lax.bitcast_convert_type(x[:, w:].astype(BF16).astype(F32), jnp.uint32)
    return hi | (lo >> 16)


def _unpack_rows(p):
    lo = lax.bitcast_convert_type(p << 16, F32)
    hi = lax.bitcast_convert_type(p & jnp.uint32(0xFFFF0000), F32)
    return lo, hi


def _const_spec(shape):
    nd = len(shape)
    return pl.BlockSpec(shape, lambda *_: (0,) * nd, pipeline_mode=pl.Buffered(1))


def _inproj_kernel(x_ref, lng_ref, lnb_ref, win_ref, alng_ref, alnb_ref, wsp_ref, sbf_ref, lb_ref,
                   wpa_ref, ag_ref, q_ref, gf_ref, gb_ref, iv_ref, so_ref, sgb_ref, gmin_ref):
    tm = x_ref.shape[0]
    xb = _layer_norm(x_ref[...], lng_ref[...], lnb_ref[...]).astype(BF16)

    def sec(lo, width):
        return jnp.dot(xb, win_ref[:, lo:lo + width], preferred_element_type=F32)

    v = _layer_norm(jax.nn.gelu(sec(_O_V, A_WIDTH)), alng_ref[...], alnb_ref[...]).astype(BF16)
    u = jax.nn.gelu(sec(_O_U, A_WIDTH))

    q_ref[...] = jax.nn.silu(sec(_O_Q, HG_WIDTH)).astype(BF16)
    mins = []
    for d, (off, g_ref) in enumerate(((_O_FF, gf_ref), (_O_FB, gb_ref))):
        lb = lb_ref[d:d + 1, :]
        f = lb + (1.0 - lb) * jax.nn.sigmoid(sec(off, HG_WIDTH))
        g = jnp.log(f)
        g_ref[...] = g
        half = jnp.sum(g.reshape(tm // (HG_CHUNK // 2), HG_CHUNK // 2, HG_WIDTH), axis=1)
        mins.append(jnp.min(half, axis=-1, keepdims=True))
    gmin_ref[0] = jnp.broadcast_to(jnp.concatenate(mins, axis=0), gmin_ref.shape[1:])
    so_ref[...] = jax.nn.silu(sec(_O_G, HG_WIDTH)).astype(BF16)
    sgb_ref[...] = jax.nn.sigmoid(sec(_O_GB, D_MODEL)).astype(BF16)

    lane = lax.broadcasted_iota(jnp.int32, (A_CHUNK, LANES), 1)
    left = lane < (A_WIDTH // A_GROUPS)
    zero = jnp.zeros((A_CHUNK, LANES), BF16)
    chunks = []
    for c in range(tm // A_CHUNK):
        vc = v[c * A_CHUNK:(c + 1) * A_CHUNK]
        cols = []
        for p in range(A_GROUPS // 2):
            vp = vc[:, p * LANES:(p + 1) * LANES]
            rhs = jnp.concatenate([jnp.where(left, vp, zero), jnp.where(left, zero, vp)], axis=0)
            cols.append(jnp.dot(wsp_ref[p], rhs, preferred_element_type=F32))
        chunks.append(jnp.concatenate(cols, axis=1) + sbf_ref[...])
    mixed = jnp.concatenate(chunks, axis=0)
    a = _bdot(u * mixed, wpa_ref[...])
    ag_ref[...] = (jax.nn.sigmoid(sec(_O_GA, D_MODEL)) * a).astype(BF16)
    iv_ref[...] = sec(_O_I, HG_WIDTH).astype(BF16)


def _inproj(x, p):
    T = x.shape[0]
    tm = INPROJ_TILE
    nt = T // tm
    nh = 2 * tm // (HG_CHUNK // 2)
    tok = lambda w: pl.BlockSpec((tm, w), lambda i: (i, 0))
    outs = (
        jax.ShapeDtypeStruct((T, D_MODEL), BF16),
        jax.ShapeDtypeStruct((T, HG_WIDTH), BF16),
        jax.ShapeDtypeStruct((T, HG_WIDTH), F32),
        jax.ShapeDtypeStruct((T, HG_WIDTH), F32),
        jax.ShapeDtypeStruct((T, HG_WIDTH), BF16),
        jax.ShapeDtypeStruct((T, HG_WIDTH), BF16),
        jax.ShapeDtypeStruct((T, D_MODEL), BF16),
        jax.ShapeDtypeStruct((nt, nh, LANES), F32),
    )
    return pl.pallas_call(
        _inproj_kernel,
        grid=(nt,),
        in_specs=[
            tok(D_MODEL),
            _const_spec((1, D_MODEL)), _const_spec((1, D_MODEL)),
            _const_spec((D_MODEL, N_IN)),
            _const_spec((1, A_WIDTH)), _const_spec((1, A_WIDTH)),
            _const_spec((A_GROUPS // 2, A_CHUNK, 2 * A_CHUNK)),
            _const_spec((A_CHUNK, A_WIDTH)),
            _const_spec((2, HG_WIDTH)),
            _const_spec((A_WIDTH, D_MODEL)),
        ],
        out_specs=(tok(D_MODEL), tok(HG_WIDTH), tok(HG_WIDTH), tok(HG_WIDTH), tok(HG_WIDTH),
                   tok(HG_WIDTH), tok(D_MODEL), pl.BlockSpec((1, nh, LANES), lambda i: (i, 0, 0))),
        out_shape=outs,
        compiler_params=pltpu.CompilerParams(
            dimension_semantics=("parallel",), vmem_limit_bytes=VMEM_LIMIT),
        name="inproj",
    )(x, p["ln_in_g"], p["ln_in_b"], p["w_in"], p["a_ln_g"], p["a_ln_b"], p["wsp"], p["sbf"],
      p["lb"], p["w_pa"])


_NT = (((1,), (1,)), ((), ()))
_TN = (((0,), (0,)), ((), ()))


def _hgrn_direction(q_ref, g_ref, v_ref, st_ref, o_ref, tri, mask, fwd):
    C = q_ref.shape[0]
    g = g_ref[...]
    ghi = g.astype(BF16)
    glo = (g - ghi.astype(F32)).astype(BF16)
    b = jnp.dot(jnp.concatenate([tri, tri], axis=1), jnp.concatenate([ghi, glo], axis=0),
                preferred_element_type=F32)
    mid = C // 2 - 1 if fwd else C // 2
    end = C - 1 if fwd else 0
    r = b[mid:mid + 1, :]
    b_end = b[end:end + 1, :]
    qt = q_ref[...].astype(F32) * jnp.exp(b - r)
    kt = (1.0 - jnp.exp(g)) * jnp.exp(r - b)
    qtb = qt.astype(BF16)
    ktb = kt.astype(BF16)
    qhb = (qt * jnp.exp(r)).astype(BF16)
    khb = (kt * jnp.exp(b_end - r)).astype(BF16)
    decay = jnp.exp(b_end)
    v = v_ref[...]
    heads =[slice(h * HG_DK, (h + 1) * HG_DK) for h in range(HG_HEADS)]
    zero = jnp.zeros((C, HG_DK), BF16)
    mask2 = jnp.concatenate([mask, mask], axis=1)
    scores = []
    for p in range(HG_HEADS // 2):
        k1, k2 = ktb[:, heads[2 * p]], ktb[:, heads[2 * p + 1]]
        kk = jnp.concatenate([jnp.concatenate([k1, zero], axis=1),
                              jnp.concatenate([zero, k2], axis=1)], axis=0)
        s2 = lax.dot_general(qtb[:, 2 * p * HG_DK:(2 * p + 2) * HG_DK], kk, _NT,
                             preferred_element_type=F32)
        s2 = jnp.where(mask2, s2, 0.0).astype(BF16)
        scores += [s2[:, :C], s2[:, C:]]
    for h, sl in enumerate(heads):
        st = st_ref[h]
        vt = v[:, sl].T
        o_ref[:, sl] = lax.dot_general(
            jnp.concatenate([scores[h], qhb[:, sl]], axis=1),
            jnp.concatenate([vt, st.astype(BF16)], axis=1), _NT,
            preferred_element_type=F32).astype(o_ref.dtype)
        st_ref[h] = st * decay[:, sl] + jnp.dot(vt, khb[:, sl], preferred_element_type=F32)


def _hgrn_direction_stepwise(q_ref, g_ref, v_ref, st_ref, o_ref, q32_ref, v32_ref, o32_ref, fwd):
    C = q_ref.shape[0]
    q32_ref[...] = q_ref[...].astype(F32)
    v32_ref[...] = v_ref[...].astype(F32)
    sub = SUBLANES
    rows = lax.broadcasted_iota(jnp.int32, (sub, HG_DK), 0)

    def group(i, carry):
        base = pl.multiple_of((i if fwd else C // sub - 1 - i) * sub, sub)
        f = jnp.exp(g_ref[pl.ds(base, sub), :])
        k = 1.0 - f
        q = q32_ref[pl.ds(base, sub), :]
        v = v32_ref[pl.ds(base, sub), :]
        for h in range(HG_HEADS):
            sl = slice(h * HG_DK, (h + 1) * HG_DK)
            st = st_ref[h]
            out = jnp.zeros((sub, HG_DK), F32)
            for r in (range(sub) if fwd else range(sub - 1, -1, -1)):
                v_t = jnp.where(rows == 0, v[r:r + 1, sl], 0.0).astype(BF16)
                k_t = jnp.broadcast_to(k[r:r + 1, sl], (sub, HG_DK)).astype(BF16)
                st = st * f[r:r + 1, sl] + lax.dot_general(v_t, k_t, _TN, preferred_element_type=F32)
                q_t = jnp.broadcast_to(q[r:r + 1, sl], (sub, HG_DK)).astype(BF16)
                o_t = lax.dot_general(q_t, st.astype(BF16), _NT, preferred_element_type=F32)
                out = jnp.where(rows == r, o_t, out)
            st_ref[h] = st
            o32_ref[pl.ds(base, sub), sl] = out
        return carry

    lax.fori_loop(0, C // sub, group, 0)
    o_ref[...] = o32_ref[...].astype(o_ref.dtype)


def _hgrn_kernel(safe_ref, qf_ref, qb_ref, gf_ref, gb_ref, vf_ref, vb_ref, tril_ref, triu_ref,
                 of_ref, ob_ref, sf_ref, sb_ref, q32_ref, v32_ref, o32_ref):
    b, j = pl.program_id(0), pl.program_id(1)
    ns = pl.num_programs(1)
    n = HG_CHUNKS_PER_STEP
    C = qf_ref.shape[0] // n

    @pl.when(j == 0)
    def _():
        sf_ref[...] = jnp.zeros_like(sf_ref)
        sb_ref[...] = jnp.zeros_like(sb_ref)

    row = lax.broadcasted_iota(jnp.int32, (C, C), 0)
    col = lax.broadcasted_iota(jnp.int32, (C, C), 1)
    safe_f = [safe_ref[0, (b * ns + j) * n + u] != 0 for u in range(n)]
    safe_b = [safe_ref[1, (b * ns + ns - 1 - j) * n + u] != 0 for u in range(n)]
    part = lambda ref, u: ref.at[pl.ds(u * C, C), :]

    def forward(u, stepwise):
        refs = (part(qf_ref, u), part(gf_ref, u), part(vf_ref, u), sf_ref, part(of_ref, u))
        if stepwise:
            _hgrn_direction_stepwise(*refs, q32_ref, v32_ref, o32_ref, True)
        else:
            _hgrn_direction(*refs, tril_ref[...], row >= col, True)

    def backward(u, stepwise):
        refs = (part(qb_ref, u), part(gb_ref, u), part(vb_ref, u), sb_ref, part(ob_ref, u))
        if stepwise:
            _hgrn_direction_stepwise(*refs, q32_ref, v32_ref, o32_ref, False)
        else:
            _hgrn_direction(*refs, triu_ref[...], row <= col, False)

    all_safe = functools.reduce(jnp.logical_and, safe_f + safe_b)

    @pl.when(all_safe)
    def _():
        for u in range(n):
            forward(u, False)
            backward(n - 1 - u, False)

    @pl.when(jnp.logical_not(all_safe))
    def _():
        for u in range(n):
            pl.when(safe_f[u])(functools.partial(forward, u, False))
            pl.when(jnp.logical_not(safe_f[u]))(functools.partial(forward, u, True))
        for u in reversed(range(n)):
            pl.when(safe_b[u])(functools.partial(backward, u, False))
            pl.when(jnp.logical_not(safe_b[u]))(functools.partial(backward, u, True))


def _hgrn(q, gf, gb, iv, gmin, batch):
    T = q.shape[0]
    C = HG_CHUNK
    nc = T // batch // C
    nt, nh, _ = gmin.shape
    halves = gmin[:, :, 0].reshape(nt, 2, nh // 4, 2)
    safe = (jnp.min(halves, axis=-1) > -HG_SAFE_LOGDECAY).astype(jnp.int32)
    safe = safe.transpose(1, 0, 2).reshape(2, T // C)
    n = HG_CHUNKS_PER_STEP
    ns = nc // n
    fwd = pl.BlockSpec((n * C, HG_WIDTH), lambda b, j, s: (b * ns + j, 0))
    bwd = pl.BlockSpec((n * C, HG_WIDTH), lambda b, j, s: (b * ns + ns - 1 - j, 0))
    const = lambda shape: pl.BlockSpec(shape, lambda b, j, s: (0,) * len(shape),
                                       pipeline_mode=pl.Buffered(1))
    row = lax.broadcasted_iota(jnp.int32, (C, C), 0)
    col = lax.broadcasted_iota(jnp.int32, (C, C), 1)
    tril = (row >= col).astype(BF16)
    triu = (row <= col).astype(BF16)
    grid_spec = pltpu.PrefetchScalarGridSpec(
        num_scalar_prefetch=1,
        grid=(batch, ns),
        in_specs=[fwd, bwd, fwd, bwd, fwd, bwd, const((C, C)), const((C, C))],
        out_specs=(fwd, bwd),
        scratch_shapes=[pltpu.VMEM((HG_HEADS, HG_DK, HG_DK), F32),
                        pltpu.VMEM((HG_HEADS, HG_DK, HG_DK), F32),
                        pltpu.VMEM((C, HG_WIDTH), F32), pltpu.VMEM((C, HG_WIDTH), F32),
                        pltpu.VMEM((C, HG_WIDTH), F32)],
    )
    return pl.pallas_call(
        _hgrn_kernel,
        grid_spec=grid_spec,
        out_shape=(jax.ShapeDtypeStruct((T, HG_WIDTH), BF16), jax.ShapeDtypeStruct((T, HG_WIDTH), BF16)),
        compiler_params=pltpu.CompilerParams(
            dimension_semantics=("parallel", "arbitrary"), vmem_limit_bytes=VMEM_LIMIT),
        name="hgrn",
    )(safe, q, q, gf, gb, iv, iv, tril, triu)


def _mix_kernel(x_ref, lng_ref, lnb_ref, of_ref, ob_ref, so_ref, sgb_ref, ag_ref, ng_ref, wpb_ref,
                wo_ref, l1g_ref, l1b_ref, wr_ref, rb_ref, tri_ref, x1_ref, x1p_ref, route_ref,
                cnt_ref, carry_ref):
    tm = x_ref.shape[0]

    @pl.when(pl.program_id(0) == 0)
    def _():
        carry_ref[...] = jnp.zeros_like(carry_ref)

    o = of_ref[...].astype(F32) + ob_ref[...].astype(F32)
    heads = []
    for h in range(HG_HEADS):
        oh = o[:, h * HG_DK:(h + 1) * HG_DK]
        heads.append(oh * lax.rsqrt(jnp.mean(oh * oh, axis=-1, keepdims=True) + RMS_EPS))
    rn = jnp.concatenate(heads, axis=1) * ng_ref[...] * so_ref[...].astype(F32)
    r = _bdot(rn, wpb_ref[...])
    mixed = ag_ref[...].astype(F32) + sgb_ref[...].astype(F32) * r
    y = _bdot(mixed, wo_ref[...])
    xn = _layer_norm(x_ref[...], lng_ref[...], lnb_ref[...])
    x1 = _layer_norm(ALPHA * xn + y, l1g_ref[...], l1b_ref[...])
    x1_ref[...] = x1
    x1p_ref[...] = _pack_rows(x1)

    neg = jnp.float32(-jnp.inf)
    reps = tm // LANES
    scores = jax.nn.sigmoid(lax.dot_general(wr_ref[...], x1.astype(BF16), _NT,
                                            preferred_element_type=F32))
    biased = (scores + jnp.concatenate([rb_ref[...]] * reps, axis=1)).reshape(
        N_GROUPS, GROUP_SIZE, tm)
    sub = lax.broadcasted_iota(jnp.int32, biased.shape, 1).astype(F32)
    m1 = jnp.max(biased, axis=1, keepdims=True)
    first = jnp.min(jnp.where(biased == m1, sub, float(GROUP_SIZE)), axis=1, keepdims=True)
    m2 = jnp.max(jnp.where(sub == first, neg, biased), axis=1, keepdims=True)
    gs = (m1 + m2).reshape(N_GROUPS, tm)
    grp = lax.broadcasted_iota(jnp.int32, (N_GROUPS, tm), 0)
    ahead = jnp.zeros((N_GROUPS, tm), F32)
    for d in range(1, N_GROUPS):
        other = pltpu.roll(gs, d, 0)
        tie = jnp.where(grp >= d, 1.0, 0.0)
        ahead = ahead + jnp.where(other > gs, 1.0, jnp.where(other == gs, tie, 0.0))
    keep = (ahead < TOPK_GROUPS).reshape(N_GROUPS, 1, tm)
    allowed = jnp.where(keep, biased, neg).reshape(N_EXPERTS, tm)
    row = lax.broadcasted_iota(jnp.int32, (N_EXPERTS, tm), 0).astype(F32)
    sel = jnp.zeros((N_EXPERTS, tm), F32)
    picks = []
    for _ in range(TOP_K):
        m = jnp.max(allowed, axis=0, keepdims=True)
        first = jnp.min(jnp.where(allowed == m, row, float(N_EXPERTS)), axis=0, keepdims=True)
        hit = row == first
        picks.append((first, hit, jnp.sum(jnp.where(hit, scores, 0.0), axis=0, keepdims=True)))
        sel = jnp.where(hit, 1.0, sel)
        allowed = jnp.where(hit, neg, allowed)
    wsum = picks[0][2]
    for pk in picks[1:]:
        wsum = wsum + pk[2]
    selb = sel.astype(BF16)
    carry = carry_ref[...]
    before = (jnp.dot(selb, tri_ref[...], preferred_element_type=F32)
              + jnp.concatenate([carry] * reps, axis=1))
    total = carry + jnp.dot(selb, jnp.ones((tm, LANES), BF16), preferred_element_type=F32)
    carry_ref[...] = total
    cnt_ref[...] = total
    blank = [jnp.zeros((1, tm), F32)] * (SUBLANES - TOP_K)
    route_ref[...] = jnp.concatenate(
        [pk[0] for pk in picks] + blank
        + [jnp.sum(jnp.where(pk[1], before, 0.0), axis=0, keepdims=True) for pk in picks] + blank
        + [pk[2] / wsum * ROUTED_SCALE for pk in picks] + blank, axis=0)


def _mix(x, of, ob, so, sgb, ag, p):
    T = x.shape[0]
    tm = TOKEN_TILE
    tok = lambda w: pl.BlockSpec((tm, w), lambda i: (i, 0))
    row = lax.broadcasted_iota(jnp.int32, (tm, tm), 0)
    col = lax.broadcasted_iota(jnp.int32, (tm, tm), 1)
    tri = (row < col).astype(BF16)
    return pl.pallas_call(
        _mix_kernel,
        grid=(T // tm,),
        in_specs=[
            tok(D_MODEL), _const_spec((1, D_MODEL)), _const_spec((1, D_MODEL)),
            tok(HG_WIDTH), tok(HG_WIDTH), tok(HG_WIDTH), tok(D_MODEL), tok(D_MODEL),
            _const_spec((1, HG_WIDTH)),
            _const_spec((HG_WIDTH, D_MODEL)), _const_spec((D_MODEL, D_MODEL)),
            _const_spec((1, D_MODEL)), _const_spec((1, D_MODEL)),
            _const_spec((N_EXPERTS, D_MODEL)), _const_spec((N_EXPERTS, LANES)),
            _const_spec((tm, tm)),
        ],
        out_specs=(tok(D_MODEL), tok(D_MODEL // 2), pl.BlockSpec((ROUTE_ROWS, tm), lambda i: (0, i)),
                   pl.BlockSpec((N_EXPERTS, LANES), lambda i: (0, 0))),
        out_shape=(jax.ShapeDtypeStruct((T, D_MODEL), F32),
                   jax.ShapeDtypeStruct((T, D_MODEL // 2), jnp.uint32),
                   jax.ShapeDtypeStruct((ROUTE_ROWS, T), F32),
                   jax.ShapeDtypeStruct((N_EXPERTS, LANES), F32)),
        scratch_shapes=[pltpu.VMEM((N_EXPERTS, LANES), F32)],
        compiler_params=pltpu.CompilerParams(
            dimension_semantics=("arbitrary",), vmem_limit_bytes=VMEM_LIMIT),
        name="mix",
    )(x, p["ln_in_g"], p["ln_in_b"], of, ob, so, sgb, ag, p["hg_norm_g"], p["w_pb"], p["w_o"],
      p["ln1_g"], p["ln1_b"], p["w_router"], p["router_bias"], tri)


def _sc_mesh():
    return plsc.VectorSubcoreMesh(core_axis_name="c", subcore_axis_name="s",
                                  num_cores=SC_CORES, num_subcores=SC_SUBCORES)


def _sc_worker():
    return lax.axis_index("s") * SC_CORES + lax.axis_index("c")


def _sc_dispatch(x1p, dest, n_rows):
    T, w = x1p.shape
    W = SC_WINDOW
    per_worker = T // W // (SC_CORES * SC_SUBCORES)

    def body(x_hbm, d_hbm, o_hbm, rows_v, idx_v, sem):
        first = _sc_worker() * per_worker
        for k in range(TOP_K):
            pltpu.sync_copy(d_hbm.at[k, pl.ds(first, per_worker)], idx_v.at[k])

        @pl.loop(0, per_worker)
        def _(j):
            pltpu.sync_copy(x_hbm.at[pl.ds((first + j) * W, W)], rows_v)
            copies = [pltpu.async_copy(rows_v, o_hbm.at[idx_v.at[k, j]], sem) for k in range(TOP_K)]
            for c in copies:
                c.wait()

    return pl.kernel(
        body,
        out_type=jax.ShapeDtypeStruct((n_rows, w), jnp.uint32),
        mesh=_sc_mesh(),
        scratch_types=[pltpu.VMEM((W, w), jnp.uint32), pltpu.VMEM((TOP_K, per_worker, W), jnp.int32),
                       pltpu.SemaphoreType.DMA],
        name="sc_dispatch",
    )(x1p, dest.reshape(TOP_K, T // W, W))


def _sc_combine(ys, dest, gates):
    T = dest.shape[1]
    W = SC_COMBINE_WINDOW
    w = ys.shape[1]
    per_worker = T // W // (SC_CORES * SC_SUBCORES)
    assert per_worker % 2 == 0 and per_worker * W * SC_CORES * SC_SUBCORES == T
    assert 2 * W == SC_LANES

    def body(y_hbm, d_hbm, g_hbm, o_hbm, idx_v, gate_v, *scratch):
        slots = [dict(rows=scratch[s], out=scratch[2 + s], gsem=scratch[4 + s], wsem=scratch[6 + s],
                      lane0=s * W) for s in range(2)]
        first = _sc_worker() * per_worker
        for k in range(TOP_K):
            pltpu.sync_copy(d_hbm.at[k, pl.ds(first * W, per_worker * W)], idx_v.at[k])
            pltpu.sync_copy(g_hbm.at[k, pl.ds(first * W, per_worker * W)], gate_v.at[k])

        def loads(b, win):
            i = win - first
            return [pltpu.make_async_copy(y_hbm.at[idx_v.at[k, pl.ds(i * W, W)]],
                                          b["rows"].at[k], b["gsem"]) for k in range(TOP_K)]

        def write_back(b, win):
            return pltpu.make_async_copy(b["out"], o_hbm.at[pl.ds(win * W, W)], b["wsem"])

        def fetch(b, win):
            for c in loads(b, win):
                c.start()

        def reduce_rows(b, pair_gates):
            rows, out = b["rows"], b["out"]
            lane = lax.iota(jnp.int32, SC_LANES)

            @pl.loop(0, W)
            def _(j):
                mine = lane == b["lane0"] + j
                gate = [jnp.broadcast_to(jnp.sum(jnp.where(mine, g, 0.0)), (SC_LANES,))
                        for g in pair_gates]

                @plsc.parallel_loop(0, w, step=SC_LANES, unroll=4)
                def _(col):
                    lo, hi = [], []
                    for k in range(TOP_K):
                        p = rows[k, j, pl.ds(col, SC_LANES)]
                        lo.append(plsc.bitcast(p << 16, F32) * gate[k])
                        hi.append(plsc.bitcast(p & jnp.uint32(0xFFFF0000), F32) * gate[k])
                    out[j, pl.ds(col, SC_LANES)] = (lo[0] + lo[1]) + (lo[2] + lo[3]) + (lo[4] + lo[5])
                    out[j, pl.ds(w + col, SC_LANES)] = (hi[0] + hi[1]) + (hi[2] + hi[3]) + (hi[4] + hi[5])

        def process(b, win, pair_gates, not_first):
            for c in loads(b, win):
                c.wait()

            @pl.when(not_first)
            def _():
                write_back(b, win).wait()

            reduce_rows(b, pair_gates)
            write_back(b, win).start()

        fetch(slots[0], first)

        @pl.loop(0, per_worker, step=2)
        def _(i):
            win = first + i
            pair_gates = [gate_v[k, pl.ds(i * W, SC_LANES)] for k in range(TOP_K)]
            fetch(slots[1], win + 1)
            process(slots[0], win, pair_gates, i > 0)

            @pl.when(i + 2 < per_worker)
            def _():
                fetch(slots[0], win + 2)

            process(slots[1], win + 1, pair_gates, i > 0)

        for b in slots:
            write_back(b, first).wait()

    slot_types = ([pltpu.VMEM((TOP_K, per_worker * W), jnp.int32),
                   pltpu.VMEM((TOP_K, per_worker * W), F32)]
                  + [pltpu.VMEM((TOP_K, W, w), jnp.uint32)] * 2
                  + [pltpu.VMEM((W, 2 * w), F32)] * 2
                  + [pltpu.SemaphoreType.DMA] * 4)
    return pl.kernel(
        body,
        out_type=jax.ShapeDtypeStruct((T, 2 * w), F32),
        mesh=_sc_mesh(),
        scratch_types=slot_types,
        compiler_params=pltpu.CompilerParams(needs_layout_passes=False),
        name="sc_combine",
    )(ys, dest, gates)


def _experts_kernel(be_ref, nv_ref, io_ref, xs_ref, wg_ref, wu_ref, wd_ref, ys_ref, *cast_refs):
    del io_ref
    i = pl.program_id(0)
    n_valid = nv_ref[i]
    mb, w = xs_ref.shape
    if cast_refs:
        @pl.when((i == 0) | (be_ref[i] != be_ref[jnp.maximum(i, 1) - 1]))
        def _():
            for src, dst in zip((wg_ref, wu_ref, wd_ref), cast_refs):
                dst[...] = src[...].astype(BF16)
        wg_ref, wu_ref, wd_ref = cast_refs

    @pl.when(n_valid > 0)
    def _():
        keep = lax.broadcasted_iota(jnp.int32, (mb, w), 0) < n_valid
        lo, hi = _unpack_rows(jnp.where(keep, xs_ref[...], jnp.uint32(0)))
        lo, hi = lo.astype(BF16), hi.astype(BF16)

        def proj(w_ref):
            return (jnp.dot(lo, w_ref[0, :w, :], preferred_element_type=F32)
                    + jnp.dot(hi, w_ref[0, w:, :], preferred_element_type=F32))

        hb = jax.nn.silu(proj(wg_ref)) * proj(wu_ref)
        ys_ref[...] = _pack_rows(jnp.dot(hb.astype(BF16), wd_ref[0], preferred_element_type=F32))


def _experts(xs, blk_exp, n_valid, blk_io, weights):
    n_rows, w = xs.shape
    mb = n_rows // blk_exp.shape[0]
    cast = weights[0].dtype != BF16
    by_expert = lambda shape: pl.BlockSpec((1,) + shape, lambda i, be, nv, io: (be[i], 0, 0))
    w_specs = [by_expert((D_MODEL, EXPERT_DIM)), by_expert((D_MODEL, EXPERT_DIM)),
               by_expert((EXPERT_DIM, D_MODEL))]
    ys_spec = pl.BlockSpec((mb, w), lambda i, be, nv, io: (io[i], 0))
    ys_shape = jax.ShapeDtypeStruct((n_rows, w), jnp.uint32)
    grid_spec = pltpu.PrefetchScalarGridSpec(
        num_scalar_prefetch=3,
        grid=(n_rows // mb,),
        in_specs=[pl.BlockSpec((mb, w), lambda i, be, nv, io: (io[i], 0))] + w_specs,
        out_specs=(ys_spec, *w_specs) if cast else ys_spec,
    )
    out = pl.pallas_call(
        _experts_kernel,
        grid_spec=grid_spec,
        out_shape=(ys_shape, *[jax.ShapeDtypeStruct(x.shape, BF16) for x in weights]) if cast else ys_shape,
        compiler_params=pltpu.CompilerParams(
            dimension_semantics=("arbitrary",), vmem_limit_bytes=VMEM_LIMIT),
        name="experts",
    )(blk_exp, n_valid, blk_io, xs, *weights)
    return (out[0], tuple(out[1:])) if cast else (out, weights)


def _final_kernel(x1_ref, routed_ref, wsg_ref, wsu_ref, wsd_ref, l2g_ref, l2b_ref, out_ref):
    x1 = x1_ref[...]
    xb = x1.astype(BF16)
    hs = (jax.nn.silu(jnp.dot(xb, wsg_ref[...], preferred_element_type=F32))
          * jnp.dot(xb, wsu_ref[...], preferred_element_type=F32))
    shared = jnp.dot(hs.astype(BF16), wsd_ref[...], preferred_element_type=F32)
    out_ref[...] = _layer_norm(ALPHA * x1 + (routed_ref[...] + shared), l2g_ref[...], l2b_ref[...])


def _final(x1, routed, p):
    T = x1.shape[0]
    tm = FINAL_TILE
    tok = lambda w: pl.BlockSpec((tm, w), lambda i: (i, 0))
    return pl.pallas_call(
        _final_kernel,
        grid=(T // tm,),
        in_specs=[
            tok(D_MODEL), tok(D_MODEL),
            _const_spec((D_MODEL, SHARED_DIM)), _const_spec((D_MODEL, SHARED_DIM)),
            _const_spec((SHARED_DIM, D_MODEL)),
            _const_spec((1, D_MODEL)), _const_spec((1, D_MODEL)),
        ],
        out_specs=tok(D_MODEL),
        out_shape=jax.ShapeDtypeStruct((T, D_MODEL), F32),
        compiler_params=pltpu.CompilerParams(
            dimension_semantics=("parallel",), vmem_limit_bytes=VMEM_LIMIT),
        name="final",
    )(x1, routed, p["w_sh_gate"], p["w_sh_up"], p["w_sh_down"], p["ln2_g"], p["ln2_b"])


def _routing_layout(route, counts, n_tokens):
    mb = EXPERT_BLOCK
    n_blocks = -(-n_tokens * TOP_K // mb) + N_EXPERTS
    idx = route[0:TOP_K].astype(jnp.int32)
    rank = route[SUBLANES:SUBLANES + TOP_K].astype(jnp.int32)
    counts = counts.astype(jnp.int32)
    padded = jnp.maximum((counts + mb - 1) // mb, 1) * mb
    pad_end = jnp.cumsum(padded)
    pad_start = pad_end - padded
    experts = jnp.arange(N_EXPERTS, dtype=jnp.int32)
    dest = rank + jnp.sum(jnp.where(idx[:, :, None] == experts, pad_start, 0), axis=-1)
    blk_start = jnp.arange(n_blocks, dtype=jnp.int32) * mb
    blk_exp = jnp.minimum(
        jnp.sum((pad_end[None, :] <= blk_start[:, None]).astype(jnp.int32), axis=1), N_EXPERTS - 1)
    valid_end = jnp.sum(jnp.where(blk_exp[:, None] == experts, pad_start + counts, 0), axis=-1)
    n_valid = jnp.clip(valid_end - blk_start, 0, mb).astype(jnp.int32)
    blk_io = jnp.minimum(jnp.arange(n_blocks, dtype=jnp.int32), pad_end[-1] // mb - 1)
    gates = route[2 * SUBLANES:2 * SUBLANES + TOP_K]
    return dest, gates, blk_exp, n_valid, blk_io, n_blocks * mb


def _encode(x, p, expert_weights):
    batch, seq, _ = x.shape
    T = batch * seq
    xt = x.reshape(T, D_MODEL)
    ag, q, gf, gb, iv, so, sgb, gmin = _inproj(xt, p)
    of, ob = _hgrn(q, gf, gb, iv, gmin, batch)
    x1, x1p, route, cnt = _mix(xt, of, ob, so, sgb, ag, p)
    dest, gates, blk_exp, n_valid, blk_io, n_rows = _routing_layout(route, cnt[:, 0], T)
    xs = _sc_dispatch(x1p, dest, n_rows)
    ys, expert_weights = _experts(xs, blk_exp, n_valid, blk_io, expert_weights)
    out = _final(x1, _sc_combine(ys, dest, gates), p)
    return out.reshape(batch, seq, D_MODEL), expert_weights


def _prepare_params(ln_in_g, ln_in_b, w_in, a_ln_g, a_ln_b, a_ws, a_sb, hg_lb_logits, hg_norm_g,
                    w_pa, w_pb, w_o, ln1_g, ln1_b, w_router, router_bias, w_sh_gate, w_sh_up,
                    w_sh_down, ln2_g, ln2_b):
    l = 0
    row = lambda v: v.reshape(1, -1).astype(F32)
    ws = a_ws[l].astype(BF16)
    wsp = jnp.concatenate([ws[0::2], ws[1::2]], axis=2)
    sbf = jnp.repeat(a_sb[l].astype(F32), A_WIDTH // A_GROUPS, axis=1)
    lb = jnp.cumsum(jax.nn.softmax(hg_lb_logits.astype(F32), axis=1), axis=1)[:, l]
    return dict(
        ln_in_g=row(ln_in_g), ln_in_b=row(ln_in_b), w_in=w_in[l].astype(BF16),
        a_ln_g=row(a_ln_g[l]), a_ln_b=row(a_ln_b[l]), wsp=wsp, sbf=sbf, lb=lb,
        w_pa=w_pa[l].astype(BF16), hg_norm_g=row(hg_norm_g[l]),
        w_pb=w_pb[l].astype(BF16), w_o=w_o[l].astype(BF16),
        ln1_g=row(ln1_g[l]), ln1_b=row(ln1_b[l]),
        w_router=w_router[l].T.astype(BF16),
        router_bias=jnp.broadcast_to(router_bias[l].astype(F32)[:, None], (N_EXPERTS, LANES)),
        w_sh_gate=w_sh_gate[l].astype(BF16), w_sh_up=w_sh_up[l].astype(BF16),
        w_sh_down=w_sh_down[l].astype(BF16),
        ln2_g=row(ln2_g[l]), ln2_b=row(ln2_b[l]),
    )


def kernel(x_prompt, x_sample, ln_in_g, ln_in_b, w_in, a_ln_g, a_ln_b, a_ws, a_sb, hg_lb_logits,
           hg_norm_g, w_pa, w_pb, w_o, ln1_g, ln1_b, w_router, router_bias, w_e_gate, w_e_up,
           w_e_down, w_sh_gate, w_sh_up, w_sh_down, ln2_g, ln2_b):
    p = _prepare_params(ln_in_g, ln_in_b, w_in, a_ln_g, a_ln_b, a_ws, a_sb, hg_lb_logits, hg_norm_g,
                        w_pa, w_pb, w_o, ln1_g, ln1_b, w_router, router_bias, w_sh_gate, w_sh_up,
                        w_sh_down, ln2_g, ln2_b)
    y_prompt, expert_weights = _encode(x_prompt, p, (w_e_gate[0], w_e_up[0], w_e_down[0]))
    y_sample, _ = _encode(x_sample, p, expert_weights)
    return y_prompt, y_sample
```

```python
import functools

import jax
import jax.numpy as jnp
from jax import lax
from jax.experimental import pallas as pl
from jax.experimental.pallas import tpu as pltpu
from jax.experimental.pallas import tpu_sc as plsc

F32 = jnp.float32
BF16 = jnp.bfloat16

D_MODEL = 1024
A_GROUPS = 8
A_WIDTH = 512
A_CHUNK = 128
HG_HEADS = 8
HG_DK = 128
HG_WIDTH = HG_HEADS * HG_DK
N_IN = 2 * A_WIDTH + 5 * HG_WIDTH + 2 * D_MODEL
N_EXPERTS = 64
TOP_K = 6
N_GROUPS = 8
TOPK_GROUPS = 4
GROUP_SIZE = N_EXPERTS // N_GROUPS
EXPERT_DIM = 256
SHARED_DIM = 256
ROUTED_SCALE = 2.5
DEPTH = 1
ALPHA = (2.0 * DEPTH) ** 0.25
LN_EPS = 1e-5
RMS_EPS = 1e-6

LANES = 128
SUBLANES = 8
ROUTE_ROWS = 3 * SUBLANES
TOKEN_TILE = 512
FINAL_TILE = 1024
INPROJ_TILE = 512
HG_CHUNK = 128
HG_CHUNKS_PER_STEP = 2
HG_SAFE_LOGDECAY = 80.0
EXPERT_BLOCK = 1024
SC_CORES = 2
SC_SUBCORES = 16
SC_LANES = 16
SC_WINDOW = 32
SC_COMBINE_WINDOW = 8
VMEM_LIMIT = 56 * 1024 * 1024

_O_U, _O_V, _O_Q, _O_FF, _O_FB, _O_I, _O_G, _O_GA, _O_GB = (
    0, 512, 1024, 2048, 3072, 4096, 5120, 6144, 7168)


def _layer_norm(x, g, b):
    mu = jnp.mean(x, axis=-1, keepdims=True)
    xc = x - mu
    var = jnp.mean(xc * xc, axis=-1, keepdims=True)
    return xc * lax.rsqrt(var + LN_EPS) * g + b


def _bdot(a, b):
    return jnp.dot(a.astype(BF16), b.astype(BF16), preferred_element_type=F32)


def _pack_rows(x):
    w = x.shape[1] // 2
    lo = lax.bitcast_convert_type(x[:, :w].astype(BF16).astype(F32), jnp.uint32)
    hi = lax.bitcast_convert_type(x[:, w:].astype(BF16).astype(F32), jnp.uint32)
    return hi | (lo >> 16)


def _unpack_rows(p):
    lo = lax.bitcast_convert_type(p << 16, F32)
    hi = lax.bitcast_convert_type(p & jnp.uint32(0xFFFF0000), F32)
    return lo, hi


def _const_spec(shape):
    nd = len(shape)
    return pl.BlockSpec(shape, lambda *_: (0,) * nd, pipeline_mode=pl.Buffered(1))


def _inproj_kernel(x_ref, lng_ref, lnb_ref, win_ref, alng_ref, alnb_ref, wsp_ref, sbf_ref, lb_ref,
                   wpa_ref, ag_ref, q_ref, gf_ref, gb_ref, iv_ref, so_ref, sgb_ref, gmin_ref):
    tm = x_ref.shape[0]
    xb = _layer_norm(x_ref[...], lng_ref[...], lnb_ref[...]).astype(BF16)

    def sec(lo, width):
        return jnp.dot(xb, win_ref[:, lo:lo + width], preferred_element_type=F32)

    v = _layer_norm(jax.nn.gelu(sec(_O_V, A_WIDTH)), alng_ref[...], alnb_ref[...]).astype(BF16)
    u = jax.nn.gelu(sec(_O_U, A_WIDTH))

    q_ref[...] = jax.nn.silu(sec(_O_Q, HG_WIDTH)).astype(BF16)
    mins = []
    for d, (off, g_ref) in enumerate(((_O_FF, gf_ref), (_O_FB, gb_ref))):
        lb = lb_ref[d:d + 1, :]
        f = lb + (1.0 - lb) * jax.nn.sigmoid(sec(off, HG_WIDTH))
        g = jnp.log(f)
        g_ref[...] = g
        half = jnp.sum(g.reshape(tm // (HG_CHUNK // 2), HG_CHUNK // 2, HG_WIDTH), axis=1)
        mins.append(jnp.min(half, axis=-1, keepdims=True))
    gmin_ref[0] = jnp.broadcast_to(jnp.concatenate(mins, axis=0), gmin_ref.shape[1:])
    so_ref[...] = jax.nn.silu(sec(_O_G, HG_WIDTH)).astype(BF16)
    sgb_ref[...] = jax.nn.sigmoid(sec(_O_GB, D_MODEL)).astype(BF16)

    lane = lax.broadcasted_iota(jnp.int32, (A_CHUNK, LANES), 1)
    left = lane < (A_WIDTH // A_GROUPS)
    zero = jnp.zeros((A_CHUNK, LANES), BF16)
    chunks = []
    for c in range(tm // A_CHUNK):
        vc = v[c * A_CHUNK:(c + 1) * A_CHUNK]
        cols = []
        for p in range(A_GROUPS // 2):
            vp = vc[:, p * LANES:(p + 1) * LANES]
            rhs = jnp.concatenate([jnp.where(left, vp, zero), jnp.where(left, zero, vp)], axis=0)
            cols.append(jnp.dot(wsp_ref[p], rhs, preferred_element_type=F32))
        chunks.append(jnp.concatenate(cols, axis=1) + sbf_ref[...])
    mixed = jnp.concatenate(chunks, axis=0)
    a = _bdot(u * mixed, wpa_ref[...])
    ag_ref[...] = (jax.nn.sigmoid(sec(_O_GA, D_MODEL)) * a).astype(BF16)
    iv_ref[...] = sec(_O_I, HG_WIDTH).astype(BF16)


def _inproj(x, p):
    T = x.shape[0]
    tm = INPROJ_TILE
    nt = T // tm
    nh = 2 * tm // (HG_CHUNK // 2)
    tok = lambda w: pl.BlockSpec((tm, w), lambda i: (i, 0))
    outs = (
        jax.ShapeDtypeStruct((T, D_MODEL), BF16),
        jax.ShapeDtypeStruct((T, HG_WIDTH), BF16),
        jax.ShapeDtypeStruct((T, HG_WIDTH), F32),
        jax.ShapeDtypeStruct((T, HG_WIDTH), F32),
        jax.ShapeDtypeStruct((T, HG_WIDTH), BF16),
        jax.ShapeDtypeStruct((T, HG_WIDTH), BF16),
        jax.ShapeDtypeStruct((T, D_MODEL), BF16),
        jax.ShapeDtypeStruct((nt, nh, LANES), F32),
    )
    return pl.pallas_call(
        _inproj_kernel,
        grid=(nt,),
        in_specs=[
            tok(D_MODEL),
            _const_spec((1, D_MODEL)), _const_spec((1, D_MODEL)),
            _const_spec((D_MODEL, N_IN)),
            _const_spec((1, A_WIDTH)), _const_spec((1, A_WIDTH)),
            _const_spec((A_GROUPS // 2, A_CHUNK, 2 * A_CHUNK)),
            _const_spec((A_CHUNK, A_WIDTH)),
            _const_spec((2, HG_WIDTH)),
            _const_spec((A_WIDTH, D_MODEL)),
        ],
        out_specs=(tok(D_MODEL), tok(HG_WIDTH), tok(HG_WIDTH), tok(HG_WIDTH), tok(HG_WIDTH),
                   tok(HG_WIDTH), tok(D_MODEL), pl.BlockSpec((1, nh, LANES), lambda i: (i, 0, 0))),
        out_shape=outs,
        compiler_params=pltpu.CompilerParams(
            dimension_semantics=("parallel",), vmem_limit_bytes=VMEM_LIMIT),
        name="inproj",
    )(x, p["ln_in_g"], p["ln_in_b"], p["w_in"], p["a_ln_g"], p["a_ln_b"], p["wsp"], p["sbf"],
      p["lb"], p["w_pa"])


_NT = (((1,), (1,)), ((), ()))
_TN = (((0,), (0,)), ((), ()))


def _hgrn_direction(q_ref, g_ref, v_ref, st_ref, o_ref, tri, mask, fwd):
    C = q_ref.shape[0]
    g = g_ref[...]
    ghi = g.astype(BF16)
    glo = (g - ghi.astype(F32)).astype(BF16)
    b = jnp.dot(jnp.concatenate([tri, tri], axis=1), jnp.concatenate([ghi, glo], axis=0),
                preferred_element_type=F32)
    mid = C // 2 - 1 if fwd else C // 2
    end = C - 1 if fwd else 0
    r = b[mid:mid + 1, :]
    b_end = b[end:end + 1, :]
    qt = q_ref[...].astype(F32) * jnp.exp(b - r)
    kt = (1.0 - jnp.exp(g)) * jnp.exp(r - b)
    qtb = qt.astype(BF16)
    ktb = kt.astype(BF16)
    qhb = (qt * jnp.exp(r)).astype(BF16)
    khb = (kt * jnp.exp(b_end - r)).astype(BF16)
    decay = jnp.exp(b_end)
    v = v_ref[...]
    heads =[slice(h * HG_DK, (h + 1) * HG_DK) for h in range(HG_HEADS)]
    zero = jnp.zeros((C, HG_DK), BF16)
    mask2 = jnp.concatenate([mask, mask], axis=1)
    scores = []
    for p in range(HG_HEADS // 2):
        k1, k2 = ktb[:, heads[2 * p]], ktb[:, heads[2 * p + 1]]
        kk = jnp.concatenate([jnp.concatenate([k1, zero], axis=1),
                              jnp.concatenate([zero, k2], axis=1)], axis=0)
        s2 = lax.dot_general(qtb[:, 2 * p * HG_DK:(2 * p + 2) * HG_DK], kk, _NT,
                             preferred_element_type=F32)
        s2 = jnp.where(mask2, s2, 0.0).astype(BF16)
        scores += [s2[:, :C], s2[:, C:]]
    for h, sl in enumerate(heads):
        st = st_ref[h]
        vt = v[:, sl].T
        o_ref[:, sl] = lax.dot_general(
            jnp.concatenate([scores[h], qhb[:, sl]], axis=1),
            jnp.concatenate([vt, st.astype(BF16)], axis=1), _NT,
            preferred_element_type=F32).astype(o_ref.dtype)
        st_ref[h] = st * decay[:, sl] + jnp.dot(vt, khb[:, sl], preferred_element_type=F32)


def _hgrn_direction_stepwise(q_ref, g_ref, v_ref, st_ref, o_ref, q32_ref, v32_ref, o32_ref, fwd):
    C = q_ref.shape[0]
    q32_ref[...] = q_ref[...].astype(F32)
    v32_ref[...] = v_ref[...].astype(F32)
    sub = SUBLANES
    rows = lax.broadcasted_iota(jnp.int32, (sub, HG_DK), 0)

    def group(i, carry):
        base = pl.multiple_of((i if fwd else C // sub - 1 - i) * sub, sub)
        f = jnp.exp(g_ref[pl.ds(base, sub), :])
        k = 1.0 - f
        q = q32_ref[pl.ds(base, sub), :]
        v = v32_ref[pl.ds(base, sub), :]
        for h in range(HG_HEADS):
            sl = slice(h * HG_DK, (h + 1) * HG_DK)
            st = st_ref[h]
            out = jnp.zeros((sub, HG_DK), F32)
            for r in (range(sub) if fwd else range(sub - 1, -1, -1)):
                v_t = jnp.where(rows == 0, v[r:r + 1, sl], 0.0).astype(BF16)
                k_t = jnp.broadcast_to(k[r:r + 1, sl], (sub, HG_DK)).astype(BF16)
                st = st * f[r:r + 1, sl] + lax.dot_general(v_t, k_t, _TN, preferred_element_type=F32)
                q_t = jnp.broadcast_to(q[r:r + 1, sl], (sub, HG_DK)).astype(BF16)
                o_t = lax.dot_general(q_t, st.astype(BF16), _NT, preferred_element_type=F32)
                out = jnp.where(rows == r, o_t, out)
            st_ref[h] = st
            o32_ref[pl.ds(base, sub), sl] = out
        return carry

    lax.fori_loop(0, C // sub, group, 0)
    o_ref[...] = o32_ref[...].astype(o_ref.dtype)


def _hgrn_kernel(safe_ref, qf_ref, qb_ref, gf_ref, gb_ref, vf_ref, vb_ref, tril_ref, triu_ref,
                 of_ref, ob_ref, sf_ref, sb_ref, q32_ref, v32_ref, o32_ref):
    b, j = pl.program_id(0), pl.program_id(1)
    ns = pl.num_programs(1)
    n = HG_CHUNKS_PER_STEP
    C = qf_ref.shape[0] // n

    @pl.when(j == 0)
    def _():
        sf_ref[...] = jnp.zeros_like(sf_ref)
        sb_ref[...] = jnp.zeros_like(sb_ref)

    row = lax.broadcasted_iota(jnp.int32, (C, C), 0)
    col = lax.broadcasted_iota(jnp.int32, (C, C), 1)
    safe_f = [safe_ref[0, (b * ns + j) * n + u] != 0 for u in range(n)]
    safe_b = [safe_ref[1, (b * ns + ns - 1 - j) * n + u] != 0 for u in range(n)]
    part = lambda ref, u: ref.at[pl.ds(u * C, C), :]

    def forward(u, stepwise):
        refs = (part(qf_ref, u), part(gf_ref, u), part(vf_ref, u), sf_ref, part(of_ref, u))
        if stepwise:
            _hgrn_direction_stepwise(*refs, q32_ref, v32_ref, o32_ref, True)
        else:
            _hgrn_direction(*refs, tril_ref[...], row >= col, True)

    def backward(u, stepwise):
        refs = (part(qb_ref, u), part(gb_ref, u), part(vb_ref, u), sb_ref, part(ob_ref, u))
        if stepwise:
            _hgrn_direction_stepwise(*refs, q32_ref, v32_ref, o32_ref, False)
        else:
            _hgrn_direction(*refs, triu_ref[...], row <= col, False)

    all_safe = functools.reduce(jnp.logical_and, safe_f + safe_b)

    @pl.when(all_safe)
    def _():
        for u in range(n):
            forward(u, False)
            backward(n - 1 - u, False)

    @pl.when(jnp.logical_not(all_safe))
    def _():
        for u in range(n):
            pl.when(safe_f[u])(functools.partial(forward, u, False))
            pl.when(jnp.logical_not(safe_f[u]))(functools.partial(forward, u, True))
        for u in reversed(range(n)):
            pl.when(safe_b[u])(functools.partial(backward, u, False))
            pl.when(jnp.logical_not(safe_b[u]))(functools.partial(backward, u, True))


def _hgrn(q, gf, gb, iv, gmin, batch):
    T = q.shape[0]
    C = HG_CHUNK
    nc = T // batch // C
    nt, nh, _ = gmin.shape
    halves = gmin[:, :, 0].reshape(nt, 2, nh // 4, 2)
    safe = (jnp.min(halves, axis=-1) > -HG_SAFE_LOGDECAY).astype(jnp.int32)
    safe = safe.transpose(1, 0, 2).reshape(2, T // C)
    n = HG_CHUNKS_PER_STEP
    ns = nc // n
    fwd = pl.BlockSpec((n * C, HG_WIDTH), lambda b, j, s: (b * ns + j, 0))
    bwd = pl.BlockSpec((n * C, HG_WIDTH), lambda b, j, s: (b * ns + ns - 1 - j, 0))
    const = lambda shape: pl.BlockSpec(shape, lambda b, j, s: (0,) * len(shape),
                                       pipeline_mode=pl.Buffered(1))
    row = lax.broadcasted_iota(jnp.int32, (C, C), 0)
    col = lax.broadcasted_iota(jnp.int32, (C, C), 1)
    tril = (row >= col).astype(BF16)
    triu = (row <= col).astype(BF16)
    grid_spec = pltpu.PrefetchScalarGridSpec(
        num_scalar_prefetch=1,
        grid=(batch, ns),
        in_specs=[fwd, bwd, fwd, bwd, fwd, bwd, const((C, C)), const((C, C))],
        out_specs=(fwd, bwd),
        scratch_shapes=[pltpu.VMEM((HG_HEADS, HG_DK, HG_DK), F32),
                        pltpu.VMEM((HG_HEADS, HG_DK, HG_DK), F32),
                        pltpu.VMEM((C, HG_WIDTH), F32), pltpu.VMEM((C, HG_WIDTH), F32),
                        pltpu.VMEM((C, HG_WIDTH), F32)],
    )
    return pl.pallas_call(
        _hgrn_kernel,
        grid_spec=grid_spec,
        out_shape=(jax.ShapeDtypeStruct((T, HG_WIDTH), BF16), jax.ShapeDtypeStruct((T, HG_WIDTH), BF16)),
        compiler_params=pltpu.CompilerParams(
            dimension_semantics=("parallel", "arbitrary"), vmem_limit_bytes=VMEM_LIMIT),
        name="hgrn",
    )(safe, q, q, gf, gb, iv, iv, tril, triu)


def _mix_kernel(x_ref, lng_ref, lnb_ref, of_ref, ob_ref, so_ref, sgb_ref, ag_ref, ng_ref, wpb_ref,
                wo_ref, l1g_ref, l1b_ref, wr_ref, rb_ref, tri_ref, x1_ref, x1p_ref, route_ref,
                cnt_ref, carry_ref):
    tm = x_ref.shape[0]

    @pl.when(pl.program_id(0) == 0)
    def _():
        carry_ref[...] = jnp.zeros_like(carry_ref)

    o = of_ref[...].astype(F32) + ob_ref[...].astype(F32)
    heads = []
    for h in range(HG_HEADS):
        oh = o[:, h * HG_DK:(h + 1) * HG_DK]
        heads.append(oh * lax.rsqrt(jnp.mean(oh * oh, axis=-1, keepdims=True) + RMS_EPS))
    rn = jnp.concatenate(heads, axis=1) * ng_ref[...] * so_ref[...].astype(F32)
    r = _bdot(rn, wpb_ref[...])
    mixed = ag_ref[...].astype(F32) + sgb_ref[...].astype(F32) * r
    y = _bdot(mixed, wo_ref[...])
    xn = _layer_norm(x_ref[...], lng_ref[...], lnb_ref[...])
    x1 = _layer_norm(ALPHA * xn + y, l1g_ref[...], l1b_ref[...])
    x1_ref[...] = x1
    x1p_ref[...] = _pack_rows(x1)

    neg = jnp.float32(-jnp.inf)
    reps = tm // LANES
    scores = jax.nn.sigmoid(lax.dot_general(wr_ref[...], x1.astype(BF16), _NT,
                                            preferred_element_type=F32))
    biased = (scores + jnp.concatenate([rb_ref[...]] * reps, axis=1)).reshape(
        N_GROUPS, GROUP_SIZE, tm)
    sub = lax.broadcasted_iota(jnp.int32, biased.shape, 1).astype(F32)
    m1 = jnp.max(biased, axis=1, keepdims=True)
    first = jnp.min(jnp.where(biased == m1, sub, float(GROUP_SIZE)), axis=1, keepdims=True)
    m2 = jnp.max(jnp.where(sub == first, neg, biased), axis=1, keepdims=True)
    gs = (m1 + m2).reshape(N_GROUPS, tm)
    grp = lax.broadcasted_iota(jnp.int32, (N_GROUPS, tm), 0)
    ahead = jnp.zeros((N_GROUPS, tm), F32)
    for d in range(1, N_GROUPS):
        other = pltpu.roll(gs, d, 0)
        tie = jnp.where(grp >= d, 1.0, 0.0)
        ahead = ahead + jnp.where(other > gs, 1.0, jnp.where(other == gs, tie, 0.0))
    keep = (ahead < TOPK_GROUPS).reshape(N_GROUPS, 1, tm)
    allowed = jnp.where(keep, biased, neg).reshape(N_EXPERTS, tm)
    row = lax.broadcasted_iota(jnp.int32, (N_EXPERTS, tm), 0).astype(F32)
    sel = jnp.zeros((N_EXPERTS, tm), F32)
    picks = []
    for _ in range(TOP_K):
        m = jnp.max(allowed, axis=0, keepdims=True)
        first = jnp.min(jnp.where(allowed == m, row, float(N_EXPERTS)), axis=0, keepdims=True)
        hit = row == first
        picks.append((first, hit, jnp.sum(jnp.where(hit, scores, 0.0), axis=0, keepdims=True)))
        sel = jnp.where(hit, 1.0, sel)
        allowed = jnp.where(hit, neg, allowed)
    wsum = picks[0][2]
    for pk in picks[1:]:
        wsum = wsum + pk[2]
    selb = sel.astype(BF16)
    carry = carry_ref[...]
    before = (jnp.dot(selb, tri_ref[...], preferred_element_type=F32)
              + jnp.concatenate([carry] * reps, axis=1))
    total = carry + jnp.dot(selb, jnp.ones((tm, LANES), BF16), preferred_element_type=F32)
    carry_ref[...] = total
    cnt_ref[...] = total
    blank = [jnp.zeros((1, tm), F32)] * (SUBLANES - TOP_K)
    route_ref[...] = jnp.concatenate(
        [pk[0] for pk in picks] + blank
        + [jnp.sum(jnp.where(pk[1], before, 0.0), axis=0, keepdims=True) for pk in picks] + blank
        + [pk[2] / wsum * ROUTED_SCALE for pk in picks] + blank, axis=0)


def _mix(x, of, ob, so, sgb, ag, p):
    T = x.shape[0]
    tm = TOKEN_TILE
    tok = lambda w: pl.BlockSpec((tm, w), lambda i: (i, 0))
    row = lax.broadcasted_iota(jnp.int32, (tm, tm), 0)
    col = lax.broadcasted_iota(jnp.int32, (tm, tm), 1)
    tri = (row < col).astype(BF16)
    return pl.pallas_call(
        _mix_kernel,
        grid=(T // tm,),
        in_specs=[
            tok(D_MODEL), _const_spec((1, D_MODEL)), _const_spec((1, D_MODEL)),
            tok(HG_WIDTH), tok(HG_WIDTH), tok(HG_WIDTH), tok(D_MODEL), tok(D_MODEL),
            _const_spec((1, HG_WIDTH)),
            _const_spec((HG_WIDTH, D_MODEL)), _const_spec((D_MODEL, D_MODEL)),
            _const_spec((1, D_MODEL)), _const_spec((1, D_MODEL)),
            _const_spec((N_EXPERTS, D_MODEL)), _const_spec((N_EXPERTS, LANES)),
            _const_spec((tm, tm)),
        ],
        out_specs=(tok(D_MODEL), tok(D_MODEL // 2), pl.BlockSpec((ROUTE_ROWS, tm), lambda i: (0, i)),
                   pl.BlockSpec((N_EXPERTS, LANES), lambda i: (0, 0))),
        out_shape=(jax.ShapeDtypeStruct((T, D_MODEL), F32),
                   jax.ShapeDtypeStruct((T, D_MODEL // 2), jnp.uint32),
                   jax.ShapeDtypeStruct((ROUTE_ROWS, T), F32),
                   jax.ShapeDtypeStruct((N_EXPERTS, LANES), F32)),
        scratch_shapes=[pltpu.VMEM((N_EXPERTS, LANES), F32)],
        compiler_params=pltpu.CompilerParams(
            dimension_semantics=("arbitrary",), vmem_limit_bytes=VMEM_LIMIT),
        name="mix",
    )(x, p["ln_in_g"], p["ln_in_b"], of, ob, so, sgb, ag, p["hg_norm_g"], p["w_pb"], p["w_o"],
      p["ln1_g"], p["ln1_b"], p["w_router"], p["router_bias"], tri)


def _sc_mesh():
    return plsc.VectorSubcoreMesh(core_axis_name="c", subcore_axis_name="s",
                                  num_cores=SC_CORES, num_subcores=SC_SUBCORES)


def _sc_worker():
    return lax.axis_index("s") * SC_CORES + lax.axis_index("c")


def _sc_dispatch(x1p, dest, n_rows):
    T, w = x1p.shape
    W = SC_WINDOW
    per_worker = T // W // (SC_CORES * SC_SUBCORES)

    def body(x_hbm, d_hbm, o_hbm, rows_v, idx_v, sem):
        first = _sc_worker() * per_worker
        for k in range(TOP_K):
            pltpu.sync_copy(d_hbm.at[k, pl.ds(first, per_worker)], idx_v.at[k])

        @pl.loop(0, per_worker)
        def _(j):
            pltpu.sync_copy(x_hbm.at[pl.ds((first + j) * W, W)], rows_v)
            copies = [pltpu.async_copy(rows_v, o_hbm.at[idx_v.at[k, j]], sem) for k in range(TOP_K)]
            for c in copies:
                c.wait()

    return pl.kernel(
        body,
        out_type=jax.ShapeDtypeStruct((n_rows, w), jnp.uint32),
        mesh=_sc_mesh(),
        scratch_types=[pltpu.VMEM((W, w), jnp.uint32), pltpu.VMEM((TOP_K, per_worker, W), jnp.int32),
                       pltpu.SemaphoreType.DMA],
        name="sc_dispatch",
    )(x1p, dest.reshape(TOP_K, T // W, W))


def _sc_combine(ys, dest, gates):
    T = dest.shape[1]
    W = SC_COMBINE_WINDOW
    w = ys.shape[1]
    per_worker = T // W // (SC_CORES * SC_SUBCORES)
    assert per_worker % 2 == 0 and per_worker * W * SC_CORES * SC_SUBCORES == T
    assert 2 * W == SC_LANES

    def body(y_hbm, d_hbm, g_hbm, o_hbm, idx_v, gate_v, *scratch):
        slots = [dict(rows=scratch[s], out=scratch[2 + s], gsem=scratch[4 + s], wsem=scratch[6 + s],
                      lane0=s * W) for s in range(2)]
        first = _sc_worker() * per_worker
        for k in range(TOP_K):
            pltpu.sync_copy(d_hbm.at[k, pl.ds(first * W, per_worker * W)], idx_v.at[k])
            pltpu.sync_copy(g_hbm.at[k, pl.ds(first * W, per_worker * W)], gate_v.at[k])

        def loads(b, win):
            i = win - first
            return [pltpu.make_async_copy(y_hbm.at[idx_v.at[k, pl.ds(i * W, W)]],
                                          b["rows"].at[k], b["gsem"]) for k in range(TOP_K)]

        def write_back(b, win):
            return pltpu.make_async_copy(b["out"], o_hbm.at[pl.ds(win * W, W)], b["wsem"])

        def fetch(b, win):
            for c in loads(b, win):
                c.start()

        def reduce_rows(b, pair_gates):
            rows, out = b["rows"], b["out"]
            lane = lax.iota(jnp.int32, SC_LANES)

            @pl.loop(0, W)
            def _(j):
                mine = lane == b["lane0"] + j
                gate = [jnp.broadcast_to(jnp.sum(jnp.where(mine, g, 0.0)), (SC_LANES,))
                        for g in pair_gates]

                @plsc.parallel_loop(0, w, step=SC_LANES, unroll=4)
                def _(col):
                    lo, hi = [], []
                    for k in range(TOP_K):
                        p = rows[k, j, pl.ds(col, SC_LANES)]
                        lo.append(plsc.bitcast(p << 16, F32) * gate[k])
                        hi.append(plsc.bitcast(p & jnp.uint32(0xFFFF0000), F32) * gate[k])
                    out[j, pl.ds(col, SC_LANES)] = (lo[0] + lo[1]) + (lo[2] + lo[3]) + (lo[4] + lo[5])
                    out[j, pl.ds(w + col, SC_LANES)] = (hi[0] + hi[1]) + (hi[2] + hi[3]) + (hi[4] + hi[5])

        def process(b, win, pair_gates, not_first):
            for c in loads(b, win):
                c.wait()

            @pl.when(not_first)
            def _():
                write_back(b, win).wait()

            reduce_rows(b, pair_gates)
            write_back(b, win).start()

        fetch(slots[0], first)

        @pl.loop(0, per_worker, step=2)
        def _(i):
            win = first + i
            pair_gates = [gate_v[k, pl.ds(i * W, SC_LANES)] for k in range(TOP_K)]
            fetch(slots[1], win + 1)
            process(slots[0], win, pair_gates, i > 0)

            @pl.when(i + 2 < per_worker)
            def _():
                fetch(slots[0], win + 2)

            process(slots[1], win + 1, pair_gates, i > 0)

        for b in slots:
            write_back(b, first).wait()

    slot_types = ([pltpu.VMEM((TOP_K, per_worker * W), jnp.int32),
                   pltpu.VMEM((TOP_K, per_worker * W), F32)]
                  + [pltpu.VMEM((TOP_K, W, w), jnp.uint32)] * 2
                  + [pltpu.VMEM((W, 2 * w), F32)] * 2
                  + [pltpu.SemaphoreType.DMA] * 4)
    return pl.kernel(
        body,
        out_type=jax.ShapeDtypeStruct((T, 2 * w), F32),
        mesh=_sc_mesh(),
        scratch_types=slot_types,
        compiler_params=pltpu.CompilerParams(needs_layout_passes=False),
        name="sc_combine",
    )(ys, dest, gates)


def _experts_kernel(be_ref, nv_ref, io_ref, xs_ref, wg_ref, wu_ref, wd_ref, ys_ref, *cast_refs):
    del io_ref
    i = pl.program_id(0)
    n_valid = nv_ref[i]
    mb, w = xs_ref.shape
    if cast_refs:
        @pl.when((i == 0) | (be_ref[i] != be_ref[jnp.maximum(i, 1) - 1]))
        def _():
            for src, dst in zip((wg_ref, wu_ref, wd_ref), cast_refs):
                dst[...] = src[...].astype(BF16)
        wg_ref, wu_ref, wd_ref = cast_refs

    @pl.when(n_valid > 0)
    def _():
        keep = lax.broadcasted_iota(jnp.int32, (mb, w), 0) < n_valid
        lo, hi = _unpack_rows(jnp.where(keep, xs_ref[...], jnp.uint32(0)))
        lo, hi = lo.astype(BF16), hi.astype(BF16)

        def proj(w_ref):
            return (jnp.dot(lo, w_ref[0, :w, :], preferred_element_type=F32)
                    + jnp.dot(hi, w_ref[0, w:, :], preferred_element_type=F32))

        hb = jax.nn.silu(proj(wg_ref)) * proj(wu_ref)
        ys_ref[...] = _pack_rows(jnp.dot(hb.astype(BF16), wd_ref[0], preferred_element_type=F32))


def _experts(xs, blk_exp, n_valid, blk_io, weights):
    n_rows, w = xs.shape
    mb = n_rows // blk_exp.shape[0]
    cast = weights[0].dtype != BF16
    by_expert = lambda shape: pl.BlockSpec((1,) + shape, lambda i, be, nv, io: (be[i], 0, 0))
    w_specs = [by_expert((D_MODEL, EXPERT_DIM)), by_expert((D_MODEL, EXPERT_DIM)),
               by_expert((EXPERT_DIM, D_MODEL))]
    ys_spec = pl.BlockSpec((mb, w), lambda i, be, nv, io: (io[i], 0))
    ys_shape = jax.ShapeDtypeStruct((n_rows, w), jnp.uint32)
    grid_spec = pltpu.PrefetchScalarGridSpec(
        num_scalar_prefetch=3,
        grid=(n_rows // mb,),
        in_specs=[pl.BlockSpec((mb, w), lambda i, be, nv, io: (io[i], 0))] + w_specs,
        out_specs=(ys_spec, *w_specs) if cast else ys_spec,
    )
    out = pl.pallas_call(
        _experts_kernel,
        grid_spec=grid_spec,
        out_shape=(ys_shape, *[jax.ShapeDtypeStruct(x.shape, BF16) for x in weights]) if cast else ys_shape,
        compiler_params=pltpu.CompilerParams(
            dimension_semantics=("arbitrary",), vmem_limit_bytes=VMEM_LIMIT),
        name="experts",
    )(blk_exp, n_valid, blk_io, xs, *weights)
    return (out[0], tuple(out[1:])) if cast else (out, weights)


def _final_kernel(x1_ref, routed_ref, wsg_ref, wsu_ref, wsd_ref, l2g_ref, l2b_ref, out_ref):
    x1 = x1_ref[...]
    xb = x1.astype(BF16)
    hs = (jax.nn.silu(jnp.dot(xb, wsg_ref[...], preferred_element_type=F32))
          * jnp.dot(xb, wsu_ref[...], preferred_element_type=F32))
    shared = jnp.dot(hs.astype(BF16), wsd_ref[...], preferred_element_type=F32)
    out_ref[...] = _layer_norm(ALPHA * x1 + (routed_ref[...] + shared), l2g_ref[...], l2b_ref[...])


def _final(x1, routed, p):
    T = x1.shape[0]
    tm = FINAL_TILE
    tok = lambda w: pl.BlockSpec((tm, w), lambda i: (i, 0))
    return pl.pallas_call(
        _final_kernel,
        grid=(T // tm,),
        in_specs=[
            tok(D_MODEL), tok(D_MODEL),
            _const_spec((D_MODEL, SHARED_DIM)), _const_spec((D_MODEL, SHARED_DIM)),
            _const_spec((SHARED_DIM, D_MODEL)),
            _const_spec((1, D_MODEL)), _const_spec((1, D_MODEL)),
        ],
        out_specs=tok(D_MODEL),
        out_shape=jax.ShapeDtypeStruct((T, D_MODEL), F32),
        compiler_params=pltpu.CompilerParams(
            dimension_semantics=("parallel",), vmem_limit_bytes=VMEM_LIMIT),
        name="final",
    )(x1, routed, p["w_sh_gate"], p["w_sh_up"], p["w_sh_down"], p["ln2_g"], p["ln2_b"])


def _routing_layout(route, counts, n_tokens):
    mb = EXPERT_BLOCK
    n_blocks = -(-n_tokens * TOP_K // mb) + N_EXPERTS
    idx = route[0:TOP_K].astype(jnp.int32)
    rank = route[SUBLANES:SUBLANES + TOP_K].astype(jnp.int32)
    counts = counts.astype(jnp.int32)
    padded = jnp.maximum((counts + mb - 1) // mb, 1) * mb
    pad_end = jnp.cumsum(padded)
    pad_start = pad_end - padded
    experts = jnp.arange(N_EXPERTS, dtype=jnp.int32)
    dest = rank + jnp.sum(jnp.where(idx[:, :, None] == experts, pad_start, 0), axis=-1)
    blk_start = jnp.arange(n_blocks, dtype=jnp.int32) * mb
    blk_exp = jnp.minimum(
        jnp.sum((pad_end[None, :] <= blk_start[:, None]).astype(jnp.int32), axis=1), N_EXPERTS - 1)
    valid_end = jnp.sum(jnp.where(blk_exp[:, None] == experts, pad_start + counts, 0), axis=-1)
    n_valid = jnp.clip(valid_end - blk_start, 0, mb).astype(jnp.int32)
    blk_io = jnp.minimum(jnp.arange(n_blocks, dtype=jnp.int32), pad_end[-1] // mb - 1)
    gates = route[2 * SUBLANES:2 * SUBLANES + TOP_K]
    return dest, gates, blk_exp, n_valid, blk_io, n_blocks * mb


def _encode(x, p, expert_weights):
    batch, seq, _ = x.shape
    T = batch * seq
    xt = x.reshape(T, D_MODEL)
    ag, q, gf, gb, iv, so, sgb, gmin = _inproj(xt, p)
    of, ob = _hgrn(q, gf, gb, iv, gmin, batch)
    x1, x1p, route, cnt = _mix(xt, of, ob, so, sgb, ag, p)
    dest, gates, blk_exp, n_valid, blk_io, n_rows = _routing_layout(route, cnt[:, 0], T)
    xs = _sc_dispatch(x1p, dest, n_rows)
    ys, expert_weights = _experts(xs, blk_exp, n_valid, blk_io, expert_weights)
    out = _final(x1, _sc_combine(ys, dest, gates), p)
    return out.reshape(batch, seq, D_MODEL), expert_weights


def _prepare_params(ln_in_g, ln_in_b, w_in, a_ln_g, a_ln_b, a_ws, a_sb, hg_lb_logits, hg_norm_g,
                    w_pa, w_pb, w_o, ln1_g, ln1_b, w_router, router_bias, w_sh_gate, w_sh_up,
                    w_sh_down, ln2_g, ln2_b):
    l = 0
    row = lambda v: v.reshape(1, -1).astype(F32)
    ws = a_ws[l].astype(BF16)
    wsp = jnp.concatenate([ws[0::2], ws[1::2]], axis=2)
    sbf = jnp.repeat(a_sb[l].astype(F32), A_WIDTH // A_GROUPS, axis=1)
    lb = jnp.cumsum(jax.nn.softmax(hg_lb_logits.astype(F32), axis=1), axis=1)[:, l]
    return dict(
        ln_in_g=row(ln_in_g), ln_in_b=row(ln_in_b), w_in=w_in[l].astype(BF16),
        a_ln_g=row(a_ln_g[l]), a_ln_b=row(a_ln_b[l]), wsp=wsp, sbf=sbf, lb=lb,
        w_pa=w_pa[l].astype(BF16), hg_norm_g=row(hg_norm_g[l]),
        w_pb=w_pb[l].astype(BF16), w_o=w_o[l].astype(BF16),
        ln1_g=row(ln1_g[l]), ln1_b=row(ln1_b[l]),
        w_router=w_router[l].T.astype(BF16),
        router_bias=jnp.broadcast_to(router_bias[l].astype(F32)[:, None], (N_EXPERTS, LANES)),
        w_sh_gate=w_sh_gate[l].astype(BF16), w_sh_up=w_sh_up[l].astype(BF16),
        w_sh_down=w_sh_down[l].astype(BF16),
        ln2_g=row(ln2_g[l]), ln2_b=row(ln2_b[l]),
    )


def kernel(x_prompt, x_sample, ln_in_g, ln_in_b, w_in, a_ln_g, a_ln_b, a_ws, a_sb, hg_lb_logits,
           hg_norm_g, w_pa, w_pb, w_o, ln1_g, ln1_b, w_router, router_bias, w_e_gate, w_e_up,
           w_e_down, w_sh_gate, w_sh_up, w_sh_down, ln2_g, ln2_b):
    p = _prepare_params(ln_in_g, ln_in_b, w_in, a_ln_g, a_ln_b, a_ws, a_sb, hg_lb_logits, hg_norm_g,
                        w_pa, w_pb, w_o, ln1_g, ln1_b, w_router, router_bias, w_sh_gate, w_sh_up,
                        w_sh_down, ln2_g, ln2_b)
    y_prompt, expert_weights = _encode(x_prompt, p, (w_e_gate[0], w_e_up[0], w_e_down[0]))
    y_sample, _ = _encode(x_sample, p, expert_weights)
    return y_prompt, y_sample
```

```python
import functools

import jax
import jax.numpy as jnp
from jax import lax
from jax.experimental import pallas as pl
from jax.experimental.pallas import tpu as pltpu
from jax.experimental.pallas import tpu_sc as plsc

F32 = jnp.float32
BF16 = jnp.bfloat16

D_MODEL = 1024
A_GROUPS = 8
A_WIDTH = 512
A_CHUNK = 128
HG_HEADS = 8
HG_DK = 128
HG_WIDTH = HG_HEADS * HG_DK
N_IN = 2 * A_WIDTH + 5 * HG_WIDTH + 2 * D_MODEL
N_EXPERTS = 64
TOP_K = 6
N_GROUPS = 8
TOPK_GROUPS = 4
GROUP_SIZE = N_EXPERTS // N_GROUPS
EXPERT_DIM = 256
SHARED_DIM = 256
ROUTED_SCALE = 2.5
DEPTH = 1
ALPHA = (2.0 * DEPTH) ** 0.25
LN_EPS = 1e-5
RMS_EPS = 1e-6

LANES = 128
SUBLANES = 8
ROUTE_ROWS = 3 * SUBLANES
TOKEN_TILE = 512
FINAL_TILE = 1024
INPROJ_TILE = 256
HG_CHUNK = 128
HG_CHUNKS_PER_STEP = 2
HG_SAFE_LOGDECAY = 80.0
EXPERT_BLOCK = 1024
SC_CORES = 2
SC_SUBCORES = 16
SC_LANES = 16
SC_WINDOW = 32
SC_COMBINE_WINDOW = 8
VMEM_LIMIT = 56 * 1024 * 1024

_O_U, _O_V, _O_Q, _O_FF, _O_FB, _O_I, _O_G, _O_GA, _O_GB = (
    0, 512, 1024, 2048, 3072, 4096, 5120, 6144, 7168)


def _layer_norm(x, g, b):
    mu = jnp.mean(x, axis=-1, keepdims=True)
    xc = x - mu
    var = jnp.mean(xc * xc, axis=-1, keepdims=True)
    return xc * lax.rsqrt(var + LN_EPS) * g + b


def _bdot(a, b):
    return jnp.dot(a.astype(BF16), b.astype(BF16), preferred_element_type=F32)


def _pack_rows(x):
    w = x.shape[1] // 2
    lo = lax.bitcast_convert_type(x[:, :w].astype(BF16).astype(F32), jnp.uint32)
    hi = lax.bitcast_convert_type(x[:, w:].astype(BF16).astype(F32), jnp.uint32)
    return hi | (lo >> 16)


def _unpack_rows(p):
    lo = lax.bitcast_convert_type(p << 16, F32)
    hi = lax.bitcast_convert_type(p & jnp.uint32(0xFFFF0000), F32)
    return lo, hi


def _const_spec(shape):
    nd = len(shape)
    return pl.BlockSpec(shape, lambda *_: (0,) * nd, pipeline_mode=pl.Buffered(1))


def _inproj_kernel(x_ref, lng_ref, lnb_ref, win_ref, alng_ref, alnb_ref, wsp_ref, sbf_ref, lb_ref,
                   wpa_ref, ag_ref, q_ref, gf_ref, gb_ref, iv_ref, so_ref, sgb_ref, gmin_ref):
    tm = x_ref.shape[0]
    xb = _layer_norm(x_ref[...], lng_ref[...], lnb_ref[...]).astype(BF16)

    def sec(lo, width):
        return jnp.dot(xb, win_ref[:, lo:lo + width], preferred_element_type=F32)

    v = _layer_norm(jax.nn.gelu(sec(_O_V, A_WIDTH)), alng_ref[...], alnb_ref[...]).astype(BF16)
    u = jax.nn.gelu(sec(_O_U, A_WIDTH))

    q_ref[...] = jax.nn.silu(sec(_O_Q, HG_WIDTH)).astype(BF16)
    mins = []
    for d, (off, g_ref) in enumerate(((_O_FF, gf_ref), (_O_FB, gb_ref))):
        lb = lb_ref[d:d + 1, :]
        f = lb + (1.0 - lb) * jax.nn.sigmoid(sec(off, HG_WIDTH))
        g = jnp.log(f)
        g_ref[...] = g
        half = jnp.sum(g.reshape(tm // (HG_CHUNK // 2), HG_CHUNK // 2, HG_WIDTH), axis=1)
        mins.append(jnp.min(half, axis=-1, keepdims=True))
    gmin_ref[0] = jnp.broadcast_to(jnp.concatenate(mins, axis=0), gmin_ref.shape[1:])
    so_ref[...] = jax.nn.silu(sec(_O_G, HG_WIDTH)).astype(BF16)
    sgb_ref[...] = jax.nn.sigmoid(sec(_O_GB, D_MODEL)).astype(BF16)

    lane = lax.broadcasted_iota(jnp.int32, (A_CHUNK, LANES), 1)
    left = lane < (A_WIDTH // A_GROUPS)
    zero = jnp.zeros((A_CHUNK, LANES), BF16)
    chunks = []
    for c in range(tm // A_CHUNK):
        vc = v[c * A_CHUNK:(c + 1) * A_CHUNK]
        cols = []
        for p in range(A_GROUPS // 2):
            vp = vc[:, p * LANES:(p + 1) * LANES]
            rhs = jnp.concatenate([jnp.where(left, vp, zero), jnp.where(left, zero, vp)], axis=0)
            cols.append(jnp.dot(wsp_ref[p], rhs, preferred_element_type=F32))
        chunks.append(jnp.concatenate(cols, axis=1) + sbf_ref[...])
    mixed = jnp.concatenate(chunks, axis=0)
    a = _bdot(u * mixed, wpa_ref[...])
    ag_ref[...] = (jax.nn.sigmoid(sec(_O_GA, D_MODEL)) * a).astype(BF16)
    iv_ref[...] = sec(_O_I, HG_WIDTH).astype(BF16)


def _inproj(x, p):
    T = x.shape[0]
    tm = INPROJ_TILE
    nt = T // tm
    nh = 2 * tm // (HG_CHUNK // 2)
    tok = lambda w: pl.BlockSpec((tm, w), lambda i: (i, 0))
    outs = (
        jax.ShapeDtypeStruct((T, D_MODEL), BF16),
        jax.ShapeDtypeStruct((T, HG_WIDTH), BF16),
        jax.ShapeDtypeStruct((T, HG_WIDTH), F32),
        jax.ShapeDtypeStruct((T, HG_WIDTH), F32),
        jax.ShapeDtypeStruct((T, HG_WIDTH), BF16),
        jax.ShapeDtypeStruct((T, HG_WIDTH), BF16),
        jax.ShapeDtypeStruct((T, D_MODEL), BF16),
        jax.ShapeDtypeStruct((nt, nh, LANES), F32),
    )
    return pl.pallas_call(
        _inproj_kernel,
        grid=(nt,),
        in_specs=[
            tok(D_MODEL),
            _const_spec((1, D_MODEL)), _const_spec((1, D_MODEL)),
            _const_spec((D_MODEL, N_IN)),
            _const_spec((1, A_WIDTH)), _const_spec((1, A_WIDTH)),
            _const_spec((A_GROUPS // 2, A_CHUNK, 2 * A_CHUNK)),
            _const_spec((A_CHUNK, A_WIDTH)),
            _const_spec((2, HG_WIDTH)),
            _const_spec((A_WIDTH, D_MODEL)),
        ],
        out_specs=(tok(D_MODEL), tok(HG_WIDTH), tok(HG_WIDTH), tok(HG_WIDTH), tok(HG_WIDTH),
                   tok(HG_WIDTH), tok(D_MODEL), pl.BlockSpec((1, nh, LANES), lambda i: (i, 0, 0))),
        out_shape=outs,
        compiler_params=pltpu.CompilerParams(
            dimension_semantics=("parallel",), vmem_limit_bytes=VMEM_LIMIT),
        name="inproj",
    )(x, p["ln_in_g"], p["ln_in_b"], p["w_in"], p["a_ln_g"], p["a_ln_b"], p["wsp"], p["sbf"],
      p["lb"], p["w_pa"])


_NT = (((1,), (1,)), ((), ()))
_TN = (((0,), (0,)), ((), ()))


def _lockstep(*phased):
    phased = list(phased)
    while phased:
        phased = [g for g in phased if next(g, None) is not None]


def _hgrn_direction(q_ref, g_ref, v_ref, st_ref, o_ref, tri, mask, fwd):
    C = q_ref.shape[0]
    g = g_ref[...]
    ghi = g.astype(BF16)
    glo = (g - ghi.astype(F32)).astype(BF16)
    b = jnp.dot(jnp.concatenate([tri, tri], axis=1), jnp.concatenate([ghi, glo], axis=0),
                preferred_element_type=F32)
    mid = C // 2 - 1 if fwd else C // 2
    end = C - 1 if fwd else 0
    r = b[mid:mid + 1, :]
    b_end = b[end:end + 1, :]
    qt = q_ref[...].astype(F32) * jnp.exp(b - r)
    kt = (1.0 - jnp.exp(g)) * jnp.exp(r - b)
    qtb = qt.astype(BF16)
    ktb = kt.astype(BF16)
    qhb = (qt * jnp.exp(r)).astype(BF16)
    khb = (kt * jnp.exp(b_end - r)).astype(BF16)
    decay = jnp.exp(b_end)
    v = v_ref[...]
    heads =[slice(h * HG_DK, (h + 1) * HG_DK) for h in range(HG_HEADS)]
    zero = jnp.zeros((C, HG_DK), BF16)
    mask2 = jnp.concatenate([mask, mask], axis=1)
    yield True
    scores = []
    for p in range(HG_HEADS // 2):
        k1, k2 = ktb[:, heads[2 * p]], ktb[:, heads[2 * p + 1]]
        kk = jnp.concatenate([jnp.concatenate([k1, zero], axis=1),
                              jnp.concatenate([zero, k2], axis=1)], axis=0)
        s2 = lax.dot_general(qtb[:, 2 * p * HG_DK:(2 * p + 2) * HG_DK], kk, _NT,
                             preferred_element_type=F32)
        s2 = jnp.where(mask2, s2, 0.0).astype(BF16)
        scores += [s2[:, :C], s2[:, C:]]
    yield True
    for h, sl in enumerate(heads):
        st = st_ref[h]
        vt = v[:, sl].T
        o_ref[:, sl] = lax.dot_general(
            jnp.concatenate([scores[h], qhb[:, sl]], axis=1),
            jnp.concatenate([vt, st.astype(BF16)], axis=1), _NT,
            preferred_element_type=F32).astype(o_ref.dtype)
        st_ref[h] = st * decay[:, sl] + jnp.dot(vt, khb[:, sl], preferred_element_type=F32)


def _hgrn_direction_stepwise(q_ref, g_ref, v_ref, st_ref, o_ref, q32_ref, v32_ref, o32_ref, fwd):
    C = q_ref.shape[0]
    q32_ref[...] = q_ref[...].astype(F32)
    v32_ref[...] = v_ref[...].astype(F32)
    sub = SUBLANES
    rows = lax.broadcasted_iota(jnp.int32, (sub, HG_DK), 0)

    def group(i, carry):
        base = pl.multiple_of((i if fwd else C // sub - 1 - i) * sub, sub)
        f = jnp.exp(g_ref[pl.ds(base, sub), :])
        k = 1.0 - f
        q = q32_ref[pl.ds(base, sub), :]
        v = v32_ref[pl.ds(base, sub), :]
        for h in range(HG_HEADS):
            sl = slice(h * HG_DK, (h + 1) * HG_DK)
            st = st_ref[h]
            out = jnp.zeros((sub, HG_DK), F32)
            for r in (range(sub) if fwd else range(sub - 1, -1, -1)):
                v_t = jnp.where(rows == 0, v[r:r + 1, sl], 0.0).astype(BF16)
                k_t = jnp.broadcast_to(k[r:r + 1, sl], (sub, HG_DK)).astype(BF16)
                st = st * f[r:r + 1, sl] + lax.dot_general(v_t, k_t, _TN, preferred_element_type=F32)
                q_t = jnp.broadcast_to(q[r:r + 1, sl], (sub, HG_DK)).astype(BF16)
                o_t = lax.dot_general(q_t, st.astype(BF16), _NT, preferred_element_type=F32)
                out = jnp.where(rows == r, o_t, out)
            st_ref[h] = st
            o32_ref[pl.ds(base, sub), sl] = out
        return carry

    lax.fori_loop(0, C // sub, group, 0)
    o_ref[...] = o32_ref[...].astype(o_ref.dtype)


def _hgrn_kernel(safe_ref, qf_ref, qb_ref, gf_ref, gb_ref, vf_ref, vb_ref, tril_ref, triu_ref,
                 of_ref, ob_ref, sf_ref, sb_ref, q32_ref, v32_ref, o32_ref):
    b, j = pl.program_id(0), pl.program_id(1)
    ns = pl.num_programs(1)
    n = HG_CHUNKS_PER_STEP
    C = qf_ref.shape[0] // n

    @pl.when(j == 0)
    def _():
        sf_ref[...] = jnp.zeros_like(sf_ref)
        sb_ref[...] = jnp.zeros_like(sb_ref)

    row = lax.broadcasted_iota(jnp.int32, (C, C), 0)
    col = lax.broadcasted_iota(jnp.int32, (C, C), 1)
    safe_f = [safe_ref[0, (b * ns + j) * n + u] != 0 for u in range(n)]
    safe_b = [safe_ref[1, (b * ns + ns - 1 - j) * n + u] != 0 for u in range(n)]
    part = lambda ref, u: ref.at[pl.ds(u * C, C), :]

    def forward(u, stepwise):
        refs = (part(qf_ref, u), part(gf_ref, u), part(vf_ref, u), sf_ref, part(of_ref, u))
        if stepwise:
            _hgrn_direction_stepwise(*refs, q32_ref, v32_ref, o32_ref, True)
            return iter(())
        return _hgrn_direction(*refs, tril_ref[...], row >= col, True)

    def backward(u, stepwise):
        refs = (part(qb_ref, u), part(gb_ref, u), part(vb_ref, u), sb_ref, part(ob_ref, u))
        if stepwise:
            _hgrn_direction_stepwise(*refs, q32_ref, v32_ref, o32_ref, False)
            return iter(())
        return _hgrn_direction(*refs, triu_ref[...], row <= col, False)

    all_safe = functools.reduce(jnp.logical_and, safe_f + safe_b)

    @pl.when(all_safe)
    def _():
        _lockstep(*[d for u in range(n) for d in (forward(u, False), backward(n - 1 - u, False))])

    @pl.when(jnp.logical_not(all_safe))
    def _():
        run = lambda direction, u, stepwise: lambda: _lockstep(direction(u, stepwise))
        for u in range(n):
            pl.when(safe_f[u])(run(forward, u, False))
            pl.when(jnp.logical_not(safe_f[u]))(run(forward, u, True))
        for u in reversed(range(n)):
            pl.when(safe_b[u])(run(backward, u, False))
            pl.when(jnp.logical_not(safe_b[u]))(run(backward, u, True))


def _hgrn(q, gf, gb, iv, gmin, batch):
    T = q.shape[0]
    C = HG_CHUNK
    nc = T // batch // C
    nt, nh, _ = gmin.shape
    halves = gmin[:, :, 0].reshape(nt, 2, nh // 4, 2)
    safe = (jnp.min(halves, axis=-1) > -HG_SAFE_LOGDECAY).astype(jnp.int32)
    safe = safe.transpose(1, 0, 2).reshape(2, T // C)
    n = HG_CHUNKS_PER_STEP
    ns = nc // n
    fwd = pl.BlockSpec((n * C, HG_WIDTH), lambda b, j, s: (b * ns + j, 0))
    bwd = pl.BlockSpec((n * C, HG_WIDTH), lambda b, j, s: (b * ns + ns - 1 - j, 0))
    const = lambda shape: pl.BlockSpec(shape, lambda b, j, s: (0,) * len(shape),
                                       pipeline_mode=pl.Buffered(1))
    row = lax.broadcasted_iota(jnp.int32, (C, C), 0)
    col = lax.broadcasted_iota(jnp.int32, (C, C), 1)
    tril = (row >= col).astype(BF16)
    triu = (row <= col).astype(BF16)
    grid_spec = pltpu.PrefetchScalarGridSpec(
        num_scalar_prefetch=1,
        grid=(batch, ns),
        in_specs=[fwd, bwd, fwd, bwd, fwd, bwd, const((C, C)), const((C, C))],
        out_specs=(fwd, bwd),
        scratch_shapes=[pltpu.VMEM((HG_HEADS, HG_DK, HG_DK), F32),
                        pltpu.VMEM((HG_HEADS, HG_DK, HG_DK), F32),
                        pltpu.VMEM((C, HG_WIDTH), F32), pltpu.VMEM((C, HG_WIDTH), F32),
                        pltpu.VMEM((C, HG_WIDTH), F32)],
    )
    return pl.pallas_call(
        _hgrn_kernel,
        grid_spec=grid_spec,
        out_shape=(jax.ShapeDtypeStruct((T, HG_WIDTH), BF16), jax.ShapeDtypeStruct((T, HG_WIDTH), BF16)),
        compiler_params=pltpu.CompilerParams(
            dimension_semantics=("parallel", "arbitrary"), vmem_limit_bytes=VMEM_LIMIT),
        name="hgrn",
    )(safe, q, q, gf, gb, iv, iv, tril, triu)


def _mix_kernel(x_ref, lng_ref, lnb_ref, of_ref, ob_ref, so_ref, sgb_ref, ag_ref, ng_ref, wpb_ref,
                wo_ref, l1g_ref, l1b_ref, wr_ref, rb_ref, tri_ref, x1_ref, x1p_ref, route_ref,
                cnt_ref, carry_ref):
    tm = x_ref.shape[0]

    @pl.when(pl.program_id(0) == 0)
    def _():
        carry_ref[...] = jnp.zeros_like(carry_ref)

    o = of_ref[...].astype(F32) + ob_ref[...].astype(F32)
    heads = []
    for h in range(HG_HEADS):
        oh = o[:, h * HG_DK:(h + 1) * HG_DK]
        heads.append(oh * lax.rsqrt(jnp.mean(oh * oh, axis=-1, keepdims=True) + RMS_EPS))
    rn = jnp.concatenate(heads, axis=1) * ng_ref[...] * so_ref[...].astype(F32)
    r = _bdot(rn, wpb_ref[...])
    mixed = ag_ref[...].astype(F32) + sgb_ref[...].astype(F32) * r
    y = _bdot(mixed, wo_ref[...])
    xn = _layer_norm(x_ref[...], lng_ref[...], lnb_ref[...])
    x1 = _layer_norm(ALPHA * xn + y, l1g_ref[...], l1b_ref[...])
    x1_ref[...] = x1
    x1p_ref[...] = _pack_rows(x1)

    neg = jnp.float32(-jnp.inf)
    reps = tm // LANES
    scores = jax.nn.sigmoid(lax.dot_general(wr_ref[...], x1.astype(BF16), _NT,
                                            preferred_element_type=F32))
    biased = (scores + jnp.concatenate([rb_ref[...]] * reps, axis=1)).reshape(
        N_GROUPS, GROUP_SIZE, tm)
    sub = lax.broadcasted_iota(jnp.int32, biased.shape, 1).astype(F32)
    m1 = jnp.max(biased, axis=1, keepdims=True)
    first = jnp.min(jnp.where(biased == m1, sub, float(GROUP_SIZE)), axis=1, keepdims=True)
    m2 = jnp.max(jnp.where(sub == first, neg, biased), axis=1, keepdims=True)
    gs = (m1 + m2).reshape(N_GROUPS, tm)
    grp = lax.broadcasted_iota(jnp.int32, (N_GROUPS, tm), 0)
    ahead = jnp.zeros((N_GROUPS, tm), F32)
    for d in range(1, N_GROUPS):
        other = pltpu.roll(gs, d, 0)
        tie = jnp.where(grp >= d, 1.0, 0.0)
        ahead = ahead + jnp.where(other > gs, 1.0, jnp.where(other == gs, tie, 0.0))
    keep = (ahead < TOPK_GROUPS).reshape(N_GROUPS, 1, tm)
    allowed = jnp.where(keep, biased, neg).reshape(N_EXPERTS, tm)
    row = lax.broadcasted_iota(jnp.int32, (N_EXPERTS, tm), 0).astype(F32)
    sel = jnp.zeros((N_EXPERTS, tm), F32)
    picks = []
    for _ in range(TOP_K):
        m = jnp.max(allowed, axis=0, keepdims=True)
        first = jnp.min(jnp.where(allowed == m, row, float(N_EXPERTS)), axis=0, keepdims=True)
        hit = row == first
        picks.append((first, hit, jnp.sum(jnp.where(hit, scores, 0.0), axis=0, keepdims=True)))
        sel = jnp.where(hit, 1.0, sel)
        allowed = jnp.where(hit, neg, allowed)
    wsum = picks[0][2]
    for pk in picks[1:]:
        wsum = wsum + pk[2]
    selb = sel.astype(BF16)
    carry = carry_ref[...]
    before = (jnp.dot(selb, tri_ref[...], preferred_element_type=F32)
              + jnp.concatenate([carry] * reps, axis=1))
    total = carry + jnp.dot(selb, jnp.ones((tm, LANES), BF16), preferred_element_type=F32)
    carry_ref[...] = total
    cnt_ref[...] = total
    blank = [jnp.zeros((1, tm), F32)] * (SUBLANES - TOP_K)
    route_ref[...] = jnp.concatenate(
        [pk[0] for pk in picks] + blank
        + [jnp.sum(jnp.where(pk[1], before, 0.0), axis=0, keepdims=True) for pk in picks] + blank
        + [pk[2] / wsum * ROUTED_SCALE for pk in picks] + blank, axis=0)


def _mix(x, of, ob, so, sgb, ag, p):
    T = x.shape[0]
    tm = TOKEN_TILE
    tok = lambda w: pl.BlockSpec((tm, w), lambda i: (i, 0))
    row = lax.broadcasted_iota(jnp.int32, (tm, tm), 0)
    col = lax.broadcasted_iota(jnp.int32, (tm, tm), 1)
    tri = (row < col).astype(BF16)
    return pl.pallas_call(
        _mix_kernel,
        grid=(T // tm,),
        in_specs=[
            tok(D_MODEL), _const_spec((1, D_MODEL)), _const_spec((1, D_MODEL)),
            tok(HG_WIDTH), tok(HG_WIDTH), tok(HG_WIDTH), tok(D_MODEL), tok(D_MODEL),
            _const_spec((1, HG_WIDTH)),
            _const_spec((HG_WIDTH, D_MODEL)), _const_spec((D_MODEL, D_MODEL)),
            _const_spec((1, D_MODEL)), _const_spec((1, D_MODEL)),
            _const_spec((N_EXPERTS, D_MODEL)), _const_spec((N_EXPERTS, LANES)),
            _const_spec((tm, tm)),
        ],
        out_specs=(tok(D_MODEL), tok(D_MODEL // 2), pl.BlockSpec((ROUTE_ROWS, tm), lambda i: (0, i)),
                   pl.BlockSpec((N_EXPERTS, LANES), lambda i: (0, 0))),
        out_shape=(jax.ShapeDtypeStruct((T, D_MODEL), F32),
                   jax.ShapeDtypeStruct((T, D_MODEL // 2), jnp.uint32),
                   jax.ShapeDtypeStruct((ROUTE_ROWS, T), F32),
                   jax.ShapeDtypeStruct((N_EXPERTS, LANES), F32)),
        scratch_shapes=[pltpu.VMEM((N_EXPERTS, LANES), F32)],
        compiler_params=pltpu.CompilerParams(
            dimension_semantics=("arbitrary",), vmem_limit_bytes=VMEM_LIMIT),
        name="mix",
    )(x, p["ln_in_g"], p["ln_in_b"], of, ob, so, sgb, ag, p["hg_norm_g"], p["w_pb"], p["w_o"],
      p["ln1_g"], p["ln1_b"], p["w_router"], p["router_bias"], tri)


def _sc_mesh():
    return plsc.VectorSubcoreMesh(core_axis_name="c", subcore_axis_name="s",
                                  num_cores=SC_CORES, num_subcores=SC_SUBCORES)


def _sc_worker():
    return lax.axis_index("s") * SC_CORES + lax.axis_index("c")


def _sc_dispatch(x1p, dest, n_rows):
    T, w = x1p.shape
    W = SC_WINDOW
    per_worker = T // W // (SC_CORES * SC_SUBCORES)

    def body(x_hbm, d_hbm, o_hbm, rows_v, idx_v, sem):
        first = _sc_worker() * per_worker
        for k in range(TOP_K):
            pltpu.sync_copy(d_hbm.at[k, pl.ds(first, per_worker)], idx_v.at[k])

        @pl.loop(0, per_worker)
        def _(j):
            pltpu.sync_copy(x_hbm.at[pl.ds((first + j) * W, W)], rows_v)
            copies = [pltpu.async_copy(rows_v, o_hbm.at[idx_v.at[k, j]], sem) for k in range(TOP_K)]
            for c in copies:
                c.wait()

    return pl.kernel(
        body,
        out_type=jax.ShapeDtypeStruct((n_rows, w), jnp.uint32),
        mesh=_sc_mesh(),
        scratch_types=[pltpu.VMEM((W, w), jnp.uint32), pltpu.VMEM((TOP_K, per_worker, W), jnp.int32),
                       pltpu.SemaphoreType.DMA],
        name="sc_dispatch",
    )(x1p, dest.reshape(TOP_K, T // W, W))


def _sc_combine(ys, dest, gates):
    T = dest.shape[1]
    W = SC_COMBINE_WINDOW
    w = ys.shape[1]
    per_worker = T // W // (SC_CORES * SC_SUBCORES)
    assert per_worker % 2 == 0 and per_worker * W * SC_CORES * SC_SUBCORES == T
    assert 2 * W == SC_LANES

    def body(y_hbm, d_hbm, g_hbm, o_hbm, idx_v, gate_v, *scratch):
        slots = [dict(rows=scratch[s], out=scratch[2 + s], gsem=scratch[4 + s], wsem=scratch[6 + s],
                      lane0=s * W) for s in range(2)]
        first = _sc_worker() * per_worker
        for k in range(TOP_K):
            pltpu.sync_copy(d_hbm.at[k, pl.ds(first * W, per_worker * W)], idx_v.at[k])
            pltpu.sync_copy(g_hbm.at[k, pl.ds(first * W, per_worker * W)], gate_v.at[k])

        def loads(b, win):
            i = win - first
            return [pltpu.make_async_copy(y_hbm.at[idx_v.at[k, pl.ds(i * W, W)]],
                                          b["rows"].at[k], b["gsem"]) for k in range(TOP_K)]

        def write_back(b, win):
            return pltpu.make_async_copy(b["out"], o_hbm.at[pl.ds(win * W, W)], b["wsem"])

        def fetch(b, win):
            for c in loads(b, win):
                c.start()

        def reduce_rows(b, pair_gates):
            rows, out = b["rows"], b["out"]
            lane = lax.iota(jnp.int32, SC_LANES)

            @pl.loop(0, W)
            def _(j):
                mine = lane == b["lane0"] + j
                gate = [jnp.broadcast_to(jnp.sum(jnp.where(mine, g, 0.0)), (SC_LANES,))
                        for g in pair_gates]

                @plsc.parallel_loop(0, w, step=SC_LANES, unroll=4)
                def _(col):
                    lo, hi = [], []
                    for k in range(TOP_K):
                        p = rows[k, j, pl.ds(col, SC_LANES)]
                        lo.append(plsc.bitcast(p << 16, F32) * gate[k])
                        hi.append(plsc.bitcast(p & jnp.uint32(0xFFFF0000), F32) * gate[k])
                    out[j, pl.ds(col, SC_LANES)] = (lo[0] + lo[1]) + (lo[2] + lo[3]) + (lo[4] + lo[5])
                    out[j, pl.ds(w + col, SC_LANES)] = (hi[0] + hi[1]) + (hi[2] + hi[3]) + (hi[4] + hi[5])

        def process(b, win, pair_gates, not_first):
            for c in loads(b, win):
                c.wait()

            @pl.when(not_first)
            def _():
                write_back(b, win).wait()

            reduce_rows(b, pair_gates)
            write_back(b, win).start()

        fetch(slots[0], first)

        @pl.loop(0, per_worker, step=2)
        def _(i):
            win = first + i
            pair_gates = [gate_v[k, pl.ds(i * W, SC_LANES)] for k in range(TOP_K)]
            fetch(slots[1], win + 1)
            process(slots[0], win, pair_gates, i > 0)

            @pl.when(i + 2 < per_worker)
            def _():
                fetch(slots[0], win + 2)

            process(slots[1], win + 1, pair_gates, i > 0)

        for b in slots:
            write_back(b, first).wait()

    slot_types = ([pltpu.VMEM((TOP_K, per_worker * W), jnp.int32),
                   pltpu.VMEM((TOP_K, per_worker * W), F32)]
                  + [pltpu.VMEM((TOP_K, W, w), jnp.uint32)] * 2
                  + [pltpu.VMEM((W, 2 * w), F32)] * 2
                  + [pltpu.SemaphoreType.DMA] * 4)
    return pl.kernel(
        body,
        out_type=jax.ShapeDtypeStruct((T, 2 * w), F32),
        mesh=_sc_mesh(),
        scratch_types=slot_types,
        compiler_params=pltpu.CompilerParams(needs_layout_passes=False),
        name="sc_combine",
    )(ys, dest, gates)


def _experts_kernel(be_ref, nv_ref, io_ref, xs_ref, wg_ref, wu_ref, wd_ref, ys_ref, *cast_refs):
    del io_ref
    i = pl.program_id(0)
    n_valid = nv_ref[i]
    mb, w = xs_ref.shape
    if cast_refs:
        @pl.when((i == 0) | (be_ref[i] != be_ref[jnp.maximum(i, 1) - 1]))
        def _():
            for src, dst in zip((wg_ref, wu_ref, wd_ref), cast_refs):
                dst[...] = src[...].astype(BF16)
        wg_ref, wu_ref, wd_ref = cast_refs

    @pl.when(n_valid > 0)
    def _():
        keep = lax.broadcasted_iota(jnp.int32, (mb, w), 0) < n_valid
        lo, hi = _unpack_rows(jnp.where(keep, xs_ref[...], jnp.uint32(0)))
        lo, hi = lo.astype(BF16), hi.astype(BF16)

        def proj(w_ref):
            return (jnp.dot(lo, w_ref[0, :w, :], preferred_element_type=F32)
                    + jnp.dot(hi, w_ref[0, w:, :], preferred_element_type=F32))

        hb = jax.nn.silu(proj(wg_ref)) * proj(wu_ref)
        ys_ref[...] = _pack_rows(jnp.dot(hb.astype(BF16), wd_ref[0], preferred_element_type=F32))


def _experts(xs, blk_exp, n_valid, blk_io, weights):
    n_rows, w = xs.shape
    mb = n_rows // blk_exp.shape[0]
    cast = weights[0].dtype != BF16
    by_expert = lambda shape: pl.BlockSpec((1,) + shape, lambda i, be, nv, io: (be[i], 0, 0))
    w_specs = [by_expert((D_MODEL, EXPERT_DIM)), by_expert((D_MODEL, EXPERT_DIM)),
               by_expert((EXPERT_DIM, D_MODEL))]
    ys_spec = pl.BlockSpec((mb, w), lambda i, be, nv, io: (io[i], 0))
    ys_shape = jax.ShapeDtypeStruct((n_rows, w), jnp.uint32)
    grid_spec = pltpu.PrefetchScalarGridSpec(
        num_scalar_prefetch=3,
        grid=(n_rows // mb,),
        in_specs=[pl.BlockSpec((mb, w), lambda i, be, nv, io: (io[i], 0))] + w_specs,
        out_specs=(ys_spec, *w_specs) if cast else ys_spec,
    )
    out = pl.pallas_call(
        _experts_kernel,
        grid_spec=grid_spec,
        out_shape=(ys_shape, *[jax.ShapeDtypeStruct(x.shape, BF16) for x in weights]) if cast else ys_shape,
        compiler_params=pltpu.CompilerParams(
            dimension_semantics=("arbitrary",), vmem_limit_bytes=VMEM_LIMIT),
        name="experts",
    )(blk_exp, n_valid, blk_io, xs, *weights)
    return (out[0], tuple(out[1:])) if cast else (out, weights)


def _final_kernel(x1_ref, routed_ref, wsg_ref, wsu_ref, wsd_ref, l2g_ref, l2b_ref, out_ref):
    x1 = x1_ref[...]
    xb = x1.astype(BF16)
    hs = (jax.nn.silu(jnp.dot(xb, wsg_ref[...], preferred_element_type=F32))
          * jnp.dot(xb, wsu_ref[...], preferred_element_type=F32))
    shared = jnp.dot(hs.astype(BF16), wsd_ref[...], preferred_element_type=F32)
    out_ref[...] = _layer_norm(ALPHA * x1 + (routed_ref[...] + shared), l2g_ref[...], l2b_ref[...])


def _final(x1, routed, p):
    T = x1.shape[0]
    tm = FINAL_TILE
    tok = lambda w: pl.BlockSpec((tm, w), lambda i: (i, 0))
    return pl.pallas_call(
        _final_kernel,
        grid=(T // tm,),
        in_specs=[
            tok(D_MODEL), tok(D_MODEL),
            _const_spec((D_MODEL, SHARED_DIM)), _const_spec((D_MODEL, SHARED_DIM)),
            _const_spec((SHARED_DIM, D_MODEL)),
            _const_spec((1, D_MODEL)), _const_spec((1, D_MODEL)),
        ],
        out_specs=tok(D_MODEL),
        out_shape=jax.ShapeDtypeStruct((T, D_MODEL), F32),
        compiler_params=pltpu.CompilerParams(
            dimension_semantics=("parallel",), vmem_limit_bytes=VMEM_LIMIT),
        name="final",
    )(x1, routed, p["w_sh_gate"], p["w_sh_up"], p["w_sh_down"], p["ln2_g"], p["ln2_b"])


def _routing_layout(route, counts, n_tokens):
    mb = EXPERT_BLOCK
    n_blocks = -(-n_tokens * TOP_K // mb) + N_EXPERTS
    idx = route[0:TOP_K].astype(jnp.int32)
    rank = route[SUBLANES:SUBLANES + TOP_K].astype(jnp.int32)
    counts = counts.astype(jnp.int32)
    padded = jnp.maximum((counts + mb - 1) // mb, 1) * mb
    pad_end = jnp.cumsum(padded)
    pad_start = pad_end - padded
    experts = jnp.arange(N_EXPERTS, dtype=jnp.int32)
    dest = rank + jnp.sum(jnp.where(idx[:, :, None] == experts, pad_start, 0), axis=-1)
    blk_start = jnp.arange(n_blocks, dtype=jnp.int32) * mb
    blk_exp = jnp.minimum(
        jnp.sum((pad_end[None, :] <= blk_start[:, None]).astype(jnp.int32), axis=1), N_EXPERTS - 1)
    valid_end = jnp.sum(jnp.where(blk_exp[:, None] == experts, pad_start + counts, 0), axis=-1)
    n_valid = jnp.clip(valid_end - blk_start, 0, mb).astype(jnp.int32)
    blk_io = jnp.minimum(jnp.arange(n_blocks, dtype=jnp.int32), pad_end[-1] // mb - 1)
    gates = route[2 * SUBLANES:2 * SUBLANES + TOP_K]
    return dest, gates, blk_exp, n_valid, blk_io, n_blocks * mb


def _encode(x, p, expert_weights):
    batch, seq, _ = x.shape
    T = batch * seq
    xt = x.reshape(T, D_MODEL)
    ag, q, gf, gb, iv, so, sgb, gmin = _inproj(xt, p)
    of, ob = _hgrn(q, gf, gb, iv, gmin, batch)
    x1, x1p, route, cnt = _mix(xt, of, ob, so, sgb, ag, p)
    dest, gates, blk_exp, n_valid, blk_io, n_rows = _routing_layout(route, cnt[:, 0], T)
    xs = _sc_dispatch(x1p, dest, n_rows)
    ys, expert_weights = _experts(xs, blk_exp, n_valid, blk_io, expert_weights)
    out = _final(x1, _sc_combine(ys, dest, gates), p)
    return out.reshape(batch, seq, D_MODEL), expert_weights


def _prepare_params(ln_in_g, ln_in_b, w_in, a_ln_g, a_ln_b, a_ws, a_sb, hg_lb_logits, hg_norm_g,
                    w_pa, w_pb, w_o, ln1_g, ln1_b, w_router, router_bias, w_sh_gate, w_sh_up,
                    w_sh_down, ln2_g, ln2_b):
    l = 0
    row = lambda v: v.reshape(1, -1).astype(F32)
    ws = a_ws[l].astype(BF16)
    wsp = jnp.concatenate([ws[0::2], ws[1::2]], axis=2)
    sbf = jnp.repeat(a_sb[l].astype(F32), A_WIDTH // A_GROUPS, axis=1)
    lb = jnp.cumsum(jax.nn.softmax(hg_lb_logits.astype(F32), axis=1), axis=1)[:, l]
    return dict(
        ln_in_g=row(ln_in_g), ln_in_b=row(ln_in_b), w_in=w_in[l].astype(BF16),
        a_ln_g=row(a_ln_g[l]), a_ln_b=row(a_ln_b[l]), wsp=wsp, sbf=sbf, lb=lb,
        w_pa=w_pa[l].astype(BF16), hg_norm_g=row(hg_norm_g[l]),
        w_pb=w_pb[l].astype(BF16), w_o=w_o[l].astype(BF16),
        ln1_g=row(ln1_g[l]), ln1_b=row(ln1_b[l]),
        w_router=w_router[l].T.astype(BF16),
        router_bias=jnp.broadcast_to(router_bias[l].astype(F32)[:, None], (N_EXPERTS, LANES)),
        w_sh_gate=w_sh_gate[l].astype(BF16), w_sh_up=w_sh_up[l].astype(BF16),
        w_sh_down=w_sh_down[l].astype(BF16),
        ln2_g=row(ln2_g[l]), ln2_b=row(ln2_b[l]),
    )


def kernel(x_prompt, x_sample, ln_in_g, ln_in_b, w_in, a_ln_g, a_ln_b, a_ws, a_sb, hg_lb_logits,
           hg_norm_g, w_pa, w_pb, w_o, ln1_g, ln1_b, w_router, router_bias, w_e_gate, w_e_up,
           w_e_down, w_sh_gate, w_sh_up, w_sh_down, ln2_g, ln2_b):
    p = _prepare_params(ln_in_g, ln_in_b, w_in, a_ln_g, a_ln_b, a_ws, a_sb, hg_lb_logits, hg_norm_g,
                        w_pa, w_pb, w_o, ln1_g, ln1_b, w_router, router_bias, w_sh_gate, w_sh_up,
                        w_sh_down, ln2_g, ln2_b)
    y_prompt, expert_weights = _encode(x_prompt, p, (w_e_gate[0], w_e_up[0], w_e_down[0]))
    y_sample, _ = _encode(x_sample, p, expert_weights)
    return y_prompt, y_sample
```

```python
import functools

import jax
import jax.numpy as jnp
from jax import lax
from jax.experimental import pallas as pl
from jax.experimental.pallas import tpu as pltpu
from jax.experimental.pallas import tpu_sc as plsc

F32 = jnp.float32
BF16 = jnp.bfloat16

D_MODEL = 1024
A_GROUPS = 8
A_WIDTH = 512
A_CHUNK = 128
HG_HEADS = 8
HG_DK = 128
HG_WIDTH = HG_HEADS * HG_DK
N_IN = 2 * A_WIDTH + 5 * HG_WIDTH + 2 * D_MODEL
N_EXPERTS = 64
TOP_K = 6
N_GROUPS = 8
TOPK_GROUPS = 4
GROUP_SIZE = N_EXPERTS // N_GROUPS
EXPERT_DIM = 256
SHARED_DIM = 256
ROUTED_SCALE = 2.5
DEPTH = 1
ALPHA = (2.0 * DEPTH) ** 0.25
LN_EPS = 1e-5
RMS_EPS = 1e-6

LANES = 128
SUBLANES = 8
ROUTE_ROWS = 3 * SUBLANES
TOKEN_TILE = 512
FINAL_TILE = 1024
INPROJ_TILE = 256
HG_CHUNK = 128
HG_CHUNKS_PER_STEP = 4
HG_SAFE_LOGDECAY = 80.0
EXPERT_BLOCK = 1024
SC_CORES = 2
SC_SUBCORES = 16
SC_LANES = 16
SC_WINDOW = 32
SC_COMBINE_WINDOW = 8
VMEM_LIMIT = 56 * 1024 * 1024

_O_U, _O_V, _O_Q, _O_FF, _O_FB, _O_I, _O_G, _O_GA, _O_GB = (
    0, 512, 1024, 2048, 3072, 4096, 5120, 6144, 7168)


def _layer_norm(x, g, b):
    mu = jnp.mean(x, axis=-1, keepdims=True)
    xc = x - mu
    var = jnp.mean(xc * xc, axis=-1, keepdims=True)
    return xc * lax.rsqrt(var + LN_EPS) * g + b


def _bdot(a, b):
    return jnp.dot(a.astype(BF16), b.astype(BF16), preferred_element_type=F32)


def _pack_rows(x):
    w = x.shape[1] // 2
    lo = lax.bitcast_convert_type(x[:, :w].astype(BF16).astype(F32), jnp.uint32)
    hi = lax.bitcast_convert_type(x[:, w:].astype(BF16).astype(F32), jnp.uint32)
    return hi | (lo >> 16)


def _unpack_rows(p):
    lo = lax.bitcast_convert_type(p << 16, F32)
    hi = lax.bitcast_convert_type(p & jnp.uint32(0xFFFF0000), F32)
    return lo, hi


def _const_spec(shape):
    nd = len(shape)
    return pl.BlockSpec(shape, lambda *_: (0,) * nd, pipeline_mode=pl.Buffered(1))


def _inproj_kernel(x_ref, lng_ref, lnb_ref, win_ref, alng_ref, alnb_ref, wsp_ref, sbf_ref, lb_ref,
                   wpa_ref, ag_ref, q_ref, gf_ref, gb_ref, iv_ref, so_ref, sgb_ref, gmin_ref):
    tm = x_ref.shape[0]
    xb = _layer_norm(x_ref[...], lng_ref[...], lnb_ref[...]).astype(BF16)

    def sec(lo, width):
        return jnp.dot(xb, win_ref[:, lo:lo + width], preferred_element_type=F32)

    v = _layer_norm(jax.nn.gelu(sec(_O_V, A_WIDTH)), alng_ref[...], alnb_ref[...]).astype(BF16)
    u = jax.nn.gelu(sec(_O_U, A_WIDTH))

    q_ref[...] = jax.nn.silu(sec(_O_Q, HG_WIDTH)).astype(BF16)
    mins = []
    for d, (off, g_ref) in enumerate(((_O_FF, gf_ref), (_O_FB, gb_ref))):
        lb = lb_ref[d:d + 1, :]
        f = lb + (1.0 - lb) * jax.nn.sigmoid(sec(off, HG_WIDTH))
        g = jnp.log(f)
        g_ref[...] = g
        half = jnp.sum(g.reshape(tm // (HG_CHUNK // 2), HG_CHUNK // 2, HG_WIDTH), axis=1)
        mins.append(jnp.min(half, axis=-1, keepdims=True))
    gmin_ref[0] = jnp.broadcast_to(jnp.concatenate(mins, axis=0), gmin_ref.shape[1:])
    so_ref[...] = jax.nn.silu(sec(_O_G, HG_WIDTH)).astype(BF16)
    sgb_ref[...] = jax.nn.sigmoid(sec(_O_GB, D_MODEL)).astype(BF16)

    lane = lax.broadcasted_iota(jnp.int32, (A_CHUNK, LANES), 1)
    left = lane < (A_WIDTH // A_GROUPS)
    zero = jnp.zeros((A_CHUNK, LANES), BF16)
    chunks = []
    for c in range(tm // A_CHUNK):
        vc = v[c * A_CHUNK:(c + 1) * A_CHUNK]
        cols = []
        for p in range(A_GROUPS // 2):
            vp = vc[:, p * LANES:(p + 1) * LANES]
            rhs = jnp.concatenate([jnp.where(left, vp, zero), jnp.where(left, zero, vp)], axis=0)
            cols.append(jnp.dot(wsp_ref[p], rhs, preferred_element_type=F32))
        chunks.append(jnp.concatenate(cols, axis=1) + sbf_ref[...])
    mixed = jnp.concatenate(chunks, axis=0)
    a = _bdot(u * mixed, wpa_ref[...])
    ag_ref[...] = (jax.nn.sigmoid(sec(_O_GA, D_MODEL)) * a).astype(BF16)
    iv_ref[...] = sec(_O_I, HG_WIDTH).astype(BF16)


def _inproj(x, p):
    T = x.shape[0]
    tm = INPROJ_TILE
    nt = T // tm
    nh = 2 * tm // (HG_CHUNK // 2)
    tok = lambda w: pl.BlockSpec((tm, w), lambda i: (i, 0))
    outs = (
        jax.ShapeDtypeStruct((T, D_MODEL), BF16),
        jax.ShapeDtypeStruct((T, HG_WIDTH), BF16),
        jax.ShapeDtypeStruct((T, HG_WIDTH), F32),
        jax.ShapeDtypeStruct((T, HG_WIDTH), F32),
        jax.ShapeDtypeStruct((T, HG_WIDTH), BF16),
        jax.ShapeDtypeStruct((T, HG_WIDTH), BF16),
        jax.ShapeDtypeStruct((T, D_MODEL), BF16),
        jax.ShapeDtypeStruct((nt, nh, LANES), F32),
    )
    return pl.pallas_call(
        _inproj_kernel,
        grid=(nt,),
        in_specs=[
            tok(D_MODEL),
            _const_spec((1, D_MODEL)), _const_spec((1, D_MODEL)),
            _const_spec((D_MODEL, N_IN)),
            _const_spec((1, A_WIDTH)), _const_spec((1, A_WIDTH)),
            _const_spec((A_GROUPS // 2, A_CHUNK, 2 * A_CHUNK)),
            _const_spec((A_CHUNK, A_WIDTH)),
            _const_spec((2, HG_WIDTH)),
            _const_spec((A_WIDTH, D_MODEL)),
        ],
        out_specs=(tok(D_MODEL), tok(HG_WIDTH), tok(HG_WIDTH), tok(HG_WIDTH), tok(HG_WIDTH),
                   tok(HG_WIDTH), tok(D_MODEL), pl.BlockSpec((1, nh, LANES), lambda i: (i, 0, 0))),
        out_shape=outs,
        compiler_params=pltpu.CompilerParams(
            dimension_semantics=("parallel",), vmem_limit_bytes=VMEM_LIMIT),
        name="inproj",
    )(x, p["ln_in_g"], p["ln_in_b"], p["w_in"], p["a_ln_g"], p["a_ln_b"], p["wsp"], p["sbf"],
      p["lb"], p["w_pa"])


_NT = (((1,), (1,)), ((), ()))
_TN = (((0,), (0,)), ((), ()))


def _lockstep(*phased):
    phased = list(phased)
    while phased:
        phased = [g for g in phased if next(g, None) is not None]


def _hgrn_direction(q_ref, g_ref, v_ref, st_ref, o_ref, tri, mask, fwd):
    C = q_ref.shape[0]
    g = g_ref[...]
    ghi = g.astype(BF16)
    glo = (g - ghi.astype(F32)).astype(BF16)
    b = jnp.dot(jnp.concatenate([tri, tri], axis=1), jnp.concatenate([ghi, glo], axis=0),
                preferred_element_type=F32)
    mid = C // 2 - 1 if fwd else C // 2
    end = C - 1 if fwd else 0
    r = b[mid:mid + 1, :]
    b_end = b[end:end + 1, :]
    qt = q_ref[...].astype(F32) * jnp.exp(b - r)
    kt = (1.0 - jnp.exp(g)) * jnp.exp(r - b)
    qtb = qt.astype(BF16)
    ktb = kt.astype(BF16)
    qhb = (qt * jnp.exp(r)).astype(BF16)
    khb = (kt * jnp.exp(b_end - r)).astype(BF16)
    decay = jnp.exp(b_end)
    v = v_ref[...]
    heads =[slice(h * HG_DK, (h + 1) * HG_DK) for h in range(HG_HEADS)]
    zero = jnp.zeros((C, HG_DK), BF16)
    mask2 = jnp.concatenate([mask, mask], axis=1)
    yield True
    scores = []
    for p in range(HG_HEADS // 2):
        k1, k2 = ktb[:, heads[2 * p]], ktb[:, heads[2 * p + 1]]
        kk = jnp.concatenate([jnp.concatenate([k1, zero], axis=1),
                              jnp.concatenate([zero, k2], axis=1)], axis=0)
        s2 = lax.dot_general(qtb[:, 2 * p * HG_DK:(2 * p + 2) * HG_DK], kk, _NT,
                             preferred_element_type=F32)
        s2 = jnp.where(mask2, s2, 0.0).astype(BF16)
        scores += [s2[:, :C], s2[:, C:]]
    yield True
    for h, sl in enumerate(heads):
        st = st_ref[h]
        vt = v[:, sl].T
        o_ref[:, sl] = lax.dot_general(
            jnp.concatenate([scores[h], qhb[:, sl]], axis=1),
            jnp.concatenate([vt, st.astype(BF16)], axis=1), _NT,
            preferred_element_type=F32).astype(o_ref.dtype)
        st_ref[h] = st * decay[:, sl] + jnp.dot(vt, khb[:, sl], preferred_element_type=F32)


def _hgrn_direction_stepwise(q_ref, g_ref, v_ref, st_ref, o_ref, q32_ref, v32_ref, o32_ref, fwd):
    C = q_ref.shape[0]
    q32_ref[...] = q_ref[...].astype(F32)
    v32_ref[...] = v_ref[...].astype(F32)
    sub = SUBLANES
    rows = lax.broadcasted_iota(jnp.int32, (sub, HG_DK), 0)

    def group(i, carry):
        base = pl.multiple_of((i if fwd else C // sub - 1 - i) * sub, sub)
        f = jnp.exp(g_ref[pl.ds(base, sub), :])
        k = 1.0 - f
        q = q32_ref[pl.ds(base, sub), :]
        v = v32_ref[pl.ds(base, sub), :]
        for h in range(HG_HEADS):
            sl = slice(h * HG_DK, (h + 1) * HG_DK)
            st = st_ref[h]
            out = jnp.zeros((sub, HG_DK), F32)
            for r in (range(sub) if fwd else range(sub - 1, -1, -1)):
                v_t = jnp.where(rows == 0, v[r:r + 1, sl], 0.0).astype(BF16)
                k_t = jnp.broadcast_to(k[r:r + 1, sl], (sub, HG_DK)).astype(BF16)
                st = st * f[r:r + 1, sl] + lax.dot_general(v_t, k_t, _TN, preferred_element_type=F32)
                q_t = jnp.broadcast_to(q[r:r + 1, sl], (sub, HG_DK)).astype(BF16)
                o_t = lax.dot_general(q_t, st.astype(BF16), _NT, preferred_element_type=F32)
                out = jnp.where(rows == r, o_t, out)
            st_ref[h] = st
            o32_ref[pl.ds(base, sub), sl] = out
        return carry

    lax.fori_loop(0, C // sub, group, 0)
    o_ref[...] = o32_ref[...].astype(o_ref.dtype)


def _hgrn_kernel(safe_ref, qf_ref, qb_ref, gf_ref, gb_ref, vf_ref, vb_ref, tril_ref, triu_ref,
                 of_ref, ob_ref, sf_ref, sb_ref, q32_ref, v32_ref, o32_ref):
    b, j = pl.program_id(0), pl.program_id(1)
    ns = pl.num_programs(1)
    n = HG_CHUNKS_PER_STEP
    C = qf_ref.shape[0] // n

    @pl.when(j == 0)
    def _():
        sf_ref[...] = jnp.zeros_like(sf_ref)
        sb_ref[...] = jnp.zeros_like(sb_ref)

    row = lax.broadcasted_iota(jnp.int32, (C, C), 0)
    col = lax.broadcasted_iota(jnp.int32, (C, C), 1)
    safe_f = [safe_ref[0, (b * ns + j) * n + u] != 0 for u in range(n)]
    safe_b = [safe_ref[1, (b * ns + ns - 1 - j) * n + u] != 0 for u in range(n)]
    part = lambda ref, u: ref.at[pl.ds(u * C, C), :]

    def forward(u, stepwise):
        refs = (part(qf_ref, u), part(gf_ref, u), part(vf_ref, u), sf_ref, part(of_ref, u))
        if stepwise:
            _hgrn_direction_stepwise(*refs, q32_ref, v32_ref, o32_ref, True)
            return iter(())
        return _hgrn_direction(*refs, tril_ref[...], row >= col, True)

    def backward(u, stepwise):
        refs = (part(qb_ref, u), part(gb_ref, u), part(vb_ref, u), sb_ref, part(ob_ref, u))
        if stepwise:
            _hgrn_direction_stepwise(*refs, q32_ref, v32_ref, o32_ref, False)
            return iter(())
        return _hgrn_direction(*refs, triu_ref[...], row <= col, False)

    all_safe = functools.reduce(jnp.logical_and, safe_f + safe_b)

    @pl.when(all_safe)
    def _():
        _lockstep(*[d for u in range(n) for d in (forward(u, False), backward(n - 1 - u, False))])

    @pl.when(jnp.logical_not(all_safe))
    def _():
        run = lambda direction, u, stepwise: lambda: _lockstep(direction(u, stepwise))
        for u in range(n):
            pl.when(safe_f[u])(run(forward, u, False))
            pl.when(jnp.logical_not(safe_f[u]))(run(forward, u, True))
        for u in reversed(range(n)):
            pl.when(safe_b[u])(run(backward, u, False))
            pl.when(jnp.logical_not(safe_b[u]))(run(backward, u, True))


def _hgrn(q, gf, gb, iv, gmin, batch):
    T = q.shape[0]
    C = HG_CHUNK
    nc = T // batch // C
    nt, nh, _ = gmin.shape
    halves = gmin[:, :, 0].reshape(nt, 2, nh // 4, 2)
    safe = (jnp.min(halves, axis=-1) > -HG_SAFE_LOGDECAY).astype(jnp.int32)
    safe = safe.transpose(1, 0, 2).reshape(2, T // C)
    n = HG_CHUNKS_PER_STEP
    ns = nc // n
    fwd = pl.BlockSpec((n * C, HG_WIDTH), lambda b, j, s: (b * ns + j, 0))
    bwd = pl.BlockSpec((n * C, HG_WIDTH), lambda b, j, s: (b * ns + ns - 1 - j, 0))
    const = lambda shape: pl.BlockSpec(shape, lambda b, j, s: (0,) * len(shape),
                                       pipeline_mode=pl.Buffered(1))
    row = lax.broadcasted_iota(jnp.int32, (C, C), 0)
    col = lax.broadcasted_iota(jnp.int32, (C, C), 1)
    tril = (row >= col).astype(BF16)
    triu = (row <= col).astype(BF16)
    grid_spec = pltpu.PrefetchScalarGridSpec(
        num_scalar_prefetch=1,
        grid=(batch, ns),
        in_specs=[fwd, bwd, fwd, bwd, fwd, bwd, const((C, C)), const((C, C))],
        out_specs=(fwd, bwd),
        scratch_shapes=[pltpu.VMEM((HG_HEADS, HG_DK, HG_DK), F32),
                        pltpu.VMEM((HG_HEADS, HG_DK, HG_DK), F32),
                        pltpu.VMEM((C, HG_WIDTH), F32), pltpu.VMEM((C, HG_WIDTH), F32),
                        pltpu.VMEM((C, HG_WIDTH), F32)],
    )
    return pl.pallas_call(
        _hgrn_kernel,
        grid_spec=grid_spec,
        out_shape=(jax.ShapeDtypeStruct((T, HG_WIDTH), BF16), jax.ShapeDtypeStruct((T, HG_WIDTH), BF16)),
        compiler_params=pltpu.CompilerParams(
            dimension_semantics=("parallel", "arbitrary"), vmem_limit_bytes=VMEM_LIMIT),
        name="hgrn",
    )(safe, q, q, gf, gb, iv, iv, tril, triu)


def _mix_kernel(x_ref, lng_ref, lnb_ref, of_ref, ob_ref, so_ref, sgb_ref, ag_ref, ng_ref, wpb_ref,
                wo_ref, l1g_ref, l1b_ref, wr_ref, rb_ref, tri_ref, x1_ref, x1p_ref, route_ref,
                cnt_ref, carry_ref):
    tm = x_ref.shape[0]

    @pl.when(pl.program_id(0) == 0)
    def _():
        carry_ref[...] = jnp.zeros_like(carry_ref)

    o = of_ref[...].astype(F32) + ob_ref[...].astype(F32)
    heads = []
    for h in range(HG_HEADS):
        oh = o[:, h * HG_DK:(h + 1) * HG_DK]
        heads.append(oh * lax.rsqrt(jnp.mean(oh * oh, axis=-1, keepdims=True) + RMS_EPS))
    rn = jnp.concatenate(heads, axis=1) * ng_ref[...] * so_ref[...].astype(F32)
    r = _bdot(rn, wpb_ref[...])
    mixed = ag_ref[...].astype(F32) + sgb_ref[...].astype(F32) * r
    y = _bdot(mixed, wo_ref[...])
    xn = _layer_norm(x_ref[...], lng_ref[...], lnb_ref[...])
    x1 = _layer_norm(ALPHA * xn + y, l1g_ref[...], l1b_ref[...])
    x1_ref[...] = x1
    x1p_ref[...] = _pack_rows(x1)

    neg = jnp.float32(-jnp.inf)
    reps = tm // LANES
    scores = jax.nn.sigmoid(lax.dot_general(wr_ref[...], x1.astype(BF16), _NT,
                                            preferred_element_type=F32))
    biased = (scores + jnp.concatenate([rb_ref[...]] * reps, axis=1)).reshape(
        N_GROUPS, GROUP_SIZE, tm)
    sub = lax.broadcasted_iota(jnp.int32, biased.shape, 1).astype(F32)
    m1 = jnp.max(biased, axis=1, keepdims=True)
    first = jnp.min(jnp.where(biased == m1, sub, float(GROUP_SIZE)), axis=1, keepdims=True)
    m2 = jnp.max(jnp.where(sub == first, neg, biased), axis=1, keepdims=True)
    gs = (m1 + m2).reshape(N_GROUPS, tm)
    grp = lax.broadcasted_iota(jnp.int32, (N_GROUPS, tm), 0)
    ahead = jnp.zeros((N_GROUPS, tm), F32)
    for d in range(1, N_GROUPS):
        other = pltpu.roll(gs, d, 0)
        tie = jnp.where(grp >= d, 1.0, 0.0)
        ahead = ahead + jnp.where(other > gs, 1.0, jnp.where(other == gs, tie, 0.0))
    keep = (ahead < TOPK_GROUPS).reshape(N_GROUPS, 1, tm)
    allowed = jnp.where(keep, biased, neg).reshape(N_EXPERTS, tm)
    row = lax.broadcasted_iota(jnp.int32, (N_EXPERTS, tm), 0).astype(F32)
    sel = jnp.zeros((N_EXPERTS, tm), F32)
    picks = []
    for _ in range(TOP_K):
        m = jnp.max(allowed, axis=0, keepdims=True)
        first = jnp.min(jnp.where(allowed == m, row, float(N_EXPERTS)), axis=0, keepdims=True)
        hit = row == first
        picks.append((first, hit, jnp.sum(jnp.where(hit, scores, 0.0), axis=0, keepdims=True)))
        sel = jnp.where(hit, 1.0, sel)
        allowed = jnp.where(hit, neg, allowed)
    wsum = picks[0][2]
    for pk in picks[1:]:
        wsum = wsum + pk[2]
    selb = sel.astype(BF16)
    carry = carry_ref[...]
    before = (jnp.dot(selb, tri_ref[...], preferred_element_type=F32)
              + jnp.concatenate([carry] * reps, axis=1))
    total = carry + jnp.dot(selb, jnp.ones((tm, LANES), BF16), preferred_element_type=F32)
    carry_ref[...] = total
    cnt_ref[...] = total
    blank = [jnp.zeros((1, tm), F32)] * (SUBLANES - TOP_K)
    route_ref[...] = jnp.concatenate(
        [pk[0] for pk in picks] + blank
        + [jnp.sum(jnp.where(pk[1], before, 0.0), axis=0, keepdims=True) for pk in picks] + blank
        + [pk[2] / wsum * ROUTED_SCALE for pk in picks] + blank, axis=0)


def _mix(x, of, ob, so, sgb, ag, p):
    T = x.shape[0]
    tm = TOKEN_TILE
    tok = lambda w: pl.BlockSpec((tm, w), lambda i: (i, 0))
    row = lax.broadcasted_iota(jnp.int32, (tm, tm), 0)
    col = lax.broadcasted_iota(jnp.int32, (tm, tm), 1)
    tri = (row < col).astype(BF16)
    return pl.pallas_call(
        _mix_kernel,
        grid=(T // tm,),
        in_specs=[
            tok(D_MODEL), _const_spec((1, D_MODEL)), _const_spec((1, D_MODEL)),
            tok(HG_WIDTH), tok(HG_WIDTH), tok(HG_WIDTH), tok(D_MODEL), tok(D_MODEL),
            _const_spec((1, HG_WIDTH)),
            _const_spec((HG_WIDTH, D_MODEL)), _const_spec((D_MODEL, D_MODEL)),
            _const_spec((1, D_MODEL)), _const_spec((1, D_MODEL)),
            _const_spec((N_EXPERTS, D_MODEL)), _const_spec((N_EXPERTS, LANES)),
            _const_spec((tm, tm)),
        ],
        out_specs=(tok(D_MODEL), tok(D_MODEL // 2), pl.BlockSpec((ROUTE_ROWS, tm), lambda i: (0, i)),
                   pl.BlockSpec((N_EXPERTS, LANES), lambda i: (0, 0))),
        out_shape=(jax.ShapeDtypeStruct((T, D_MODEL), F32),
                   jax.ShapeDtypeStruct((T, D_MODEL // 2), jnp.uint32),
                   jax.ShapeDtypeStruct((ROUTE_ROWS, T), F32),
                   jax.ShapeDtypeStruct((N_EXPERTS, LANES), F32)),
        scratch_shapes=[pltpu.VMEM((N_EXPERTS, LANES), F32)],
        compiler_params=pltpu.CompilerParams(
            dimension_semantics=("arbitrary",), vmem_limit_bytes=VMEM_LIMIT),
        name="mix",
    )(x, p["ln_in_g"], p["ln_in_b"], of, ob, so, sgb, ag, p["hg_norm_g"], p["w_pb"], p["w_o"],
      p["ln1_g"], p["ln1_b"], p["w_router"], p["router_bias"], tri)


def _sc_mesh():
    return plsc.VectorSubcoreMesh(core_axis_name="c", subcore_axis_name="s",
                                  num_cores=SC_CORES, num_subcores=SC_SUBCORES)


def _sc_worker():
    return lax.axis_index("s") * SC_CORES + lax.axis_index("c")


def _sc_dispatch(x1p, dest, n_rows):
    T, w = x1p.shape
    W = SC_WINDOW
    per_worker = T // W // (SC_CORES * SC_SUBCORES)

    def body(x_hbm, d_hbm, o_hbm, rows_v, idx_v, sem):
        first = _sc_worker() * per_worker
        for k in range(TOP_K):
            pltpu.sync_copy(d_hbm.at[k, pl.ds(first, per_worker)], idx_v.at[k])

        @pl.loop(0, per_worker)
        def _(j):
            pltpu.sync_copy(x_hbm.at[pl.ds((first + j) * W, W)], rows_v)
            copies = [pltpu.async_copy(rows_v, o_hbm.at[idx_v.at[k, j]], sem) for k in range(TOP_K)]
            for c in copies:
                c.wait()

    return pl.kernel(
        body,
        out_type=jax.ShapeDtypeStruct((n_rows, w), jnp.uint32),
        mesh=_sc_mesh(),
        scratch_types=[pltpu.VMEM((W, w), jnp.uint32), pltpu.VMEM((TOP_K, per_worker, W), jnp.int32),
                       pltpu.SemaphoreType.DMA],
        name="sc_dispatch",
    )(x1p, dest.reshape(TOP_K, T // W, W))


def _sc_combine(ys, dest, gates):
    T = dest.shape[1]
    W = SC_COMBINE_WINDOW
    w = ys.shape[1]
    per_worker = T // W // (SC_CORES * SC_SUBCORES)
    assert per_worker % 2 == 0 and per_worker * W * SC_CORES * SC_SUBCORES == T
    assert 2 * W == SC_LANES

    def body(y_hbm, d_hbm, g_hbm, o_hbm, idx_v, gate_v, *scratch):
        slots = [dict(rows=scratch[s], out=scratch[2 + s], gsem=scratch[4 + s], wsem=scratch[6 + s],
                      lane0=s * W) for s in range(2)]
        first = _sc_worker() * per_worker
        for k in range(TOP_K):
            pltpu.sync_copy(d_hbm.at[k, pl.ds(first * W, per_worker * W)], idx_v.at[k])
            pltpu.sync_copy(g_hbm.at[k, pl.ds(first * W, per_worker * W)], gate_v.at[k])

        def loads(b, win):
            i = win - first
            return [pltpu.make_async_copy(y_hbm.at[idx_v.at[k, pl.ds(i * W, W)]],
                                          b["rows"].at[k], b["gsem"]) for k in range(TOP_K)]

        def write_back(b, win):
            return pltpu.make_async_copy(b["out"], o_hbm.at[pl.ds(win * W, W)], b["wsem"])

        def fetch(b, win):
            for c in loads(b, win):
                c.start()

        def reduce_rows(b, pair_gates):
            rows, out = b["rows"], b["out"]
            lane = lax.iota(jnp.int32, SC_LANES)

            @pl.loop(0, W)
            def _(j):
                mine = lane == b["lane0"] + j
                gate = [jnp.broadcast_to(jnp.sum(jnp.where(mine, g, 0.0)), (SC_LANES,))
                        for g in pair_gates]

                @plsc.parallel_loop(0, w, step=SC_LANES, unroll=4)
                def _(col):
                    lo, hi = [], []
                    for k in range(TOP_K):
                        p = rows[k, j, pl.ds(col, SC_LANES)]
                        lo.append(plsc.bitcast(p << 16, F32) * gate[k])
                        hi.append(plsc.bitcast(p & jnp.uint32(0xFFFF0000), F32) * gate[k])
                    out[j, pl.ds(col, SC_LANES)] = (lo[0] + lo[1]) + (lo[2] + lo[3]) + (lo[4] + lo[5])
                    out[j, pl.ds(w + col, SC_LANES)] = (hi[0] + hi[1]) + (hi[2] + hi[3]) + (hi[4] + hi[5])

        def process(b, win, pair_gates, not_first):
            for c in loads(b, win):
                c.wait()

            @pl.when(not_first)
            def _():
                write_back(b, win).wait()

            reduce_rows(b, pair_gates)
            write_back(b, win).start()

        fetch(slots[0], first)

        @pl.loop(0, per_worker, step=2)
        def _(i):
            win = first + i
            pair_gates = [gate_v[k, pl.ds(i * W, SC_LANES)] for k in range(TOP_K)]
            fetch(slots[1], win + 1)
            process(slots[0], win, pair_gates, i > 0)

            @pl.when(i + 2 < per_worker)
            def _():
                fetch(slots[0], win + 2)

            process(slots[1], win + 1, pair_gates, i > 0)

        for b in slots:
            write_back(b, first).wait()

    slot_types = ([pltpu.VMEM((TOP_K, per_worker * W), jnp.int32),
                   pltpu.VMEM((TOP_K, per_worker * W), F32)]
                  + [pltpu.VMEM((TOP_K, W, w), jnp.uint32)] * 2
                  + [pltpu.VMEM((W, 2 * w), F32)] * 2
                  + [pltpu.SemaphoreType.DMA] * 4)
    return pl.kernel(
        body,
        out_type=jax.ShapeDtypeStruct((T, 2 * w), F32),
        mesh=_sc_mesh(),
        scratch_types=slot_types,
        compiler_params=pltpu.CompilerParams(needs_layout_passes=False),
        name="sc_combine",
    )(ys, dest, gates)


def _experts_kernel(be_ref, nv_ref, io_ref, xs_ref, wg_ref, wu_ref, wd_ref, ys_ref, *cast_refs):
    del io_ref
    i = pl.program_id(0)
    n_valid = nv_ref[i]
    mb, w = xs_ref.shape
    if cast_refs:
        @pl.when((i == 0) | (be_ref[i] != be_ref[jnp.maximum(i, 1) - 1]))
        def _():
            for src, dst in zip((wg_ref, wu_ref, wd_ref), cast_refs):
                dst[...] = src[...].astype(BF16)
        wg_ref, wu_ref, wd_ref = cast_refs

    @pl.when(n_valid > 0)
    def _():
        keep = lax.broadcasted_iota(jnp.int32, (mb, w), 0) < n_valid
        lo, hi = _unpack_rows(jnp.where(keep, xs_ref[...], jnp.uint32(0)))
        lo, hi = lo.astype(BF16), hi.astype(BF16)

        def proj(w_ref):
            return (jnp.dot(lo, w_ref[0, :w, :], preferred_element_type=F32)
                    + jnp.dot(hi, w_ref[0, w:, :], preferred_element_type=F32))

        hb = jax.nn.silu(proj(wg_ref)) * proj(wu_ref)
        ys_ref[...] = _pack_rows(jnp.dot(hb.astype(BF16), wd_ref[0], preferred_element_type=F32))


def _experts(xs, blk_exp, n_valid, blk_io, weights):
    n_rows, w = xs.shape
    mb = n_rows // blk_exp.shape[0]
    cast = weights[0].dtype != BF16
    by_expert = lambda shape: pl.BlockSpec((1,) + shape, lambda i, be, nv, io: (be[i], 0, 0))
    w_specs = [by_expert((D_MODEL, EXPERT_DIM)), by_expert((D_MODEL, EXPERT_DIM)),
               by_expert((EXPERT_DIM, D_MODEL))]
    ys_spec = pl.BlockSpec((mb, w), lambda i, be, nv, io: (io[i], 0))
    ys_shape = jax.ShapeDtypeStruct((n_rows, w), jnp.uint32)
    grid_spec = pltpu.PrefetchScalarGridSpec(
        num_scalar_prefetch=3,
        grid=(n_rows // mb,),
        in_specs=[pl.BlockSpec((mb, w), lambda i, be, nv, io: (io[i], 0))] + w_specs,
        out_specs=(ys_spec, *w_specs) if cast else ys_spec,
    )
    out = pl.pallas_call(
        _experts_kernel,
        grid_spec=grid_spec,
        out_shape=(ys_shape, *[jax.ShapeDtypeStruct(x.shape, BF16) for x in weights]) if cast else ys_shape,
        compiler_params=pltpu.CompilerParams(
            dimension_semantics=("arbitrary",), vmem_limit_bytes=VMEM_LIMIT),
        name="experts",
    )(blk_exp, n_valid, blk_io, xs, *weights)
    return (out[0], tuple(out[1:])) if cast else (out, weights)


def _final_kernel(x1_ref, routed_ref, wsg_ref, wsu_ref, wsd_ref, l2g_ref, l2b_ref, out_ref):
    x1 = x1_ref[...]
    xb = x1.astype(BF16)
    hs = (jax.nn.silu(jnp.dot(xb, wsg_ref[...], preferred_element_type=F32))
          * jnp.dot(xb, wsu_ref[...], preferred_element_type=F32))
    shared = jnp.dot(hs.astype(BF16), wsd_ref[...], preferred_element_type=F32)
    out_ref[...] = _layer_norm(ALPHA * x1 + (routed_ref[...] + shared), l2g_ref[...], l2b_ref[...])


def _final(x1, routed, p):
    T = x1.shape[0]
    tm = FINAL_TILE
    tok = lambda w: pl.BlockSpec((tm, w), lambda i: (i, 0))
    return pl.pallas_call(
        _final_kernel,
        grid=(T // tm,),
        in_specs=[
            tok(D_MODEL), tok(D_MODEL),
            _const_spec((D_MODEL, SHARED_DIM)), _const_spec((D_MODEL, SHARED_DIM)),
            _const_spec((SHARED_DIM, D_MODEL)),
            _const_spec((1, D_MODEL)), _const_spec((1, D_MODEL)),
        ],
        out_specs=tok(D_MODEL),
        out_shape=jax.ShapeDtypeStruct((T, D_MODEL), F32),
        compiler_params=pltpu.CompilerParams(
            dimension_semantics=("parallel",), vmem_limit_bytes=VMEM_LIMIT),
        name="final",
    )(x1, routed, p["w_sh_gate"], p["w_sh_up"], p["w_sh_down"], p["ln2_g"], p["ln2_b"])


def _routing_layout(route, counts, n_tokens):
    mb = EXPERT_BLOCK
    n_blocks = -(-n_tokens * TOP_K // mb) + N_EXPERTS
    idx = route[0:TOP_K].astype(jnp.int32)
    rank = route[SUBLANES:SUBLANES + TOP_K].astype(jnp.int32)
    counts = counts.astype(jnp.int32)
    padded = jnp.maximum((counts + mb - 1) // mb, 1) * mb
    pad_end = jnp.cumsum(padded)
    pad_start = pad_end - padded
    experts = jnp.arange(N_EXPERTS, dtype=jnp.int32)
    dest = rank + jnp.sum(jnp.where(idx[:, :, None] == experts, pad_start, 0), axis=-1)
    blk_start = jnp.arange(n_blocks, dtype=jnp.int32) * mb
    blk_exp = jnp.minimum(
        jnp.sum((pad_end[None, :] <= blk_start[:, None]).astype(jnp.int32), axis=1), N_EXPERTS - 1)
    valid_end = jnp.sum(jnp.where(blk_exp[:, None] == experts, pad_start + counts, 0), axis=-1)
    n_valid = jnp.clip(valid_end - blk_start, 0, mb).astype(jnp.int32)
    blk_io = jnp.minimum(jnp.arange(n_blocks, dtype=jnp.int32), pad_end[-1] // mb - 1)
    gates = route[2 * SUBLANES:2 * SUBLANES + TOP_K]
    return dest, gates, blk_exp, n_valid, blk_io, n_blocks * mb


def _encode(x, p, expert_weights):
    batch, seq, _ = x.shape
    T = batch * seq
    xt = x.reshape(T, D_MODEL)
    ag, q, gf, gb, iv, so, sgb, gmin = _inproj(xt, p)
    of, ob = _hgrn(q, gf, gb, iv, gmin, batch)
    x1, x1p, route, cnt = _mix(xt, of, ob, so, sgb, ag, p)
    dest, gates, blk_exp, n_valid, blk_io, n_rows = _routing_layout(route, cnt[:, 0], T)
    xs = _sc_dispatch(x1p, dest, n_rows)
    ys, expert_weights = _experts(xs, blk_exp, n_valid, blk_io, expert_weights)
    out = _final(x1, _sc_combine(ys, dest, gates), p)
    return out.reshape(batch, seq, D_MODEL), expert_weights


def _prepare_params(ln_in_g, ln_in_b, w_in, a_ln_g, a_ln_b, a_ws, a_sb, hg_lb_logits, hg_norm_g,
                    w_pa, w_pb, w_o, ln1_g, ln1_b, w_router, router_bias, w_sh_gate, w_sh_up,
                    w_sh_down, ln2_g, ln2_b):
    l = 0
    row = lambda v: v.reshape(1, -1).astype(F32)
    ws = a_ws[l].astype(BF16)
    wsp = jnp.concatenate([ws[0::2], ws[1::2]], axis=2)
    sbf = jnp.repeat(a_sb[l].astype(F32), A_WIDTH // A_GROUPS, axis=1)
    lb = jnp.cumsum(jax.nn.softmax(hg_lb_logits.astype(F32), axis=1), axis=1)[:, l]
    return dict(
        ln_in_g=row(ln_in_g), ln_in_b=row(ln_in_b), w_in=w_in[l].astype(BF16),
        a_ln_g=row(a_ln_g[l]), a_ln_b=row(a_ln_b[l]), wsp=wsp, sbf=sbf, lb=lb,
        w_pa=w_pa[l].astype(BF16), hg_norm_g=row(hg_norm_g[l]),
        w_pb=w_pb[l].astype(BF16), w_o=w_o[l].astype(BF16),
        ln1_g=row(ln1_g[l]), ln1_b=row(ln1_b[l]),
        w_router=w_router[l].T.astype(BF16),
        router_bias=jnp.broadcast_to(router_bias[l].astype(F32)[:, None], (N_EXPERTS, LANES)),
        w_sh_gate=w_sh_gate[l].astype(BF16), w_sh_up=w_sh_up[l].astype(BF16),
        w_sh_down=w_sh_down[l].astype(BF16),
        ln2_g=row(ln2_g[l]), ln2_b=row(ln2_b[l]),
    )


def kernel(x_prompt, x_sample, ln_in_g, ln_in_b, w_in, a_ln_g, a_ln_b, a_ws, a_sb, hg_lb_logits,
           hg_norm_g, w_pa, w_pb, w_o, ln1_g, ln1_b, w_router, router_bias, w_e_gate, w_e_up,
           w_e_down, w_sh_gate, w_sh_up, w_sh_down, ln2_g, ln2_b):
    p = _prepare_params(ln_in_g, ln_in_b, w_in, a_ln_g, a_ln_b, a_ws, a_sb, hg_lb_logits, hg_norm_g,
                        w_pa, w_pb, w_o, ln1_g, ln1_b, w_router, router_bias, w_sh_gate, w_sh_up,
                        w_sh_down, ln2_g, ln2_b)
    y_prompt, expert_weights = _encode(x_prompt, p, (w_e_gate[0], w_e_up[0], w_e_down[0]))
    y_sample, _ = _encode(x_sample, p, expert_weights)
    return y_prompt, y_sample
```

```python
import functools

import jax
import jax.numpy as jnp
from jax import lax
from jax.experimental import pallas as pl
from jax.experimental.pallas import tpu as pltpu
from jax.experimental.pallas import tpu_sc as plsc

F32 = jnp.float32
BF16 = jnp.bfloat16

D_MODEL = 1024
A_GROUPS = 8
A_WIDTH = 512
A_CHUNK = 128
HG_HEADS = 8
HG_DK = 128
HG_WIDTH = HG_HEADS * HG_DK
N_IN = 2 * A_WIDTH + 5 * HG_WIDTH + 2 * D_MODEL
N_EXPERTS = 64
TOP_K = 6
N_GROUPS = 8
TOPK_GROUPS = 4
GROUP_SIZE = N_EXPERTS // N_GROUPS
EXPERT_DIM = 256
SHARED_DIM = 256
ROUTED_SCALE = 2.5
DEPTH = 1
ALPHA = (2.0 * DEPTH) ** 0.25
LN_EPS = 1e-5
RMS_EPS = 1e-6

LANES = 128
SUBLANES = 8
ROUTE_ROWS = 3 * SUBLANES
TOKEN_TILE = 512
FINAL_TILE = 1024
INPROJ_TILE = 256
HG_CHUNK = 128
HG_CHUNKS_PER_STEP = 2
HG_SAFE_LOGDECAY = 80.0
EXPERT_BLOCK = 1024
SC_CORES = 2
SC_SUBCORES = 16
SC_LANES = 16
SC_WINDOW = 32
SC_COMBINE_WINDOW = 8
VMEM_MIB = dict(inproj=40, hgrn=32, mix=40, experts=32, final=40)

_O_U, _O_V, _O_Q, _O_FF, _O_FB, _O_I, _O_G, _O_GA, _O_GB = (
    0, 512, 1024, 2048, 3072, 4096, 5120, 6144, 7168)


def _layer_norm(x, g, b):
    mu = jnp.mean(x, axis=-1, keepdims=True)
    xc = x - mu
    var = jnp.mean(xc * xc, axis=-1, keepdims=True)
    return xc * lax.rsqrt(var + LN_EPS) * g + b


def _bdot(a, b):
    return jnp.dot(a.astype(BF16), b.astype(BF16), preferred_element_type=F32)


def _pack_rows(x):
    w = x.shape[1] // 2
    lo = lax.bitcast_convert_type(x[:, :w].astype(BF16).astype(F32), jnp.uint32)
    hi = lax.bitcast_convert_type(x[:, w:].astype(BF16).astype(F32), jnp.uint32)
    return hi | (lo >> 16)


def _unpack_rows(p):
    lo = lax.bitcast_convert_type(p << 16, F32)
    hi = lax.bitcast_convert_type(p & jnp.uint32(0xFFFF0000), F32)
    return lo, hi


def _const_spec(shape):
    nd = len(shape)
    return pl.BlockSpec(shape, lambda *_: (0,) * nd, pipeline_mode=pl.Buffered(1))


def _inproj_kernel(x_ref, lng_ref, lnb_ref, win_ref, alng_ref, alnb_ref, wsp_ref, sbf_ref, lb_ref,
                   wpa_ref, ag_ref, q_ref, gf_ref, gb_ref, iv_ref, so_ref, sgb_ref, gmin_ref):
    tm = x_ref.shape[0]
    xb = _layer_norm(x_ref[...], lng_ref[...], lnb_ref[...]).astype(BF16)

    def sec(lo, width):
        return jnp.dot(xb, win_ref[:, lo:lo + width], preferred_element_type=F32)

    v = _layer_norm(jax.nn.gelu(sec(_O_V, A_WIDTH)), alng_ref[...], alnb_ref[...]).astype(BF16)
    u = jax.nn.gelu(sec(_O_U, A_WIDTH))

    q_ref[...] = jax.nn.silu(sec(_O_Q, HG_WIDTH)).astype(BF16)
    mins = []
    for d, (off, g_ref) in enumerate(((_O_FF, gf_ref), (_O_FB, gb_ref))):
        lb = lb_ref[d:d + 1, :]
        f = lb + (1.0 - lb) * jax.nn.sigmoid(sec(off, HG_WIDTH))
        g = jnp.log(f)
        g_ref[...] = g
        half = jnp.sum(g.reshape(tm // (HG_CHUNK // 2), HG_CHUNK // 2, HG_WIDTH), axis=1)
        mins.append(jnp.min(half, axis=-1, keepdims=True))
    gmin_ref[0] = jnp.broadcast_to(jnp.concatenate(mins, axis=0), gmin_ref.shape[1:])
    so_ref[...] = jax.nn.silu(sec(_O_G, HG_WIDTH)).astype(BF16)
    sgb_ref[...] = jax.nn.sigmoid(sec(_O_GB, D_MODEL)).astype(BF16)

    lane = lax.broadcasted_iota(jnp.int32, (A_CHUNK, LANES), 1)
    left = lane < (A_WIDTH // A_GROUPS)
    zero = jnp.zeros((A_CHUNK, LANES), BF16)
    chunks = []
    for c in range(tm // A_CHUNK):
        vc = v[c * A_CHUNK:(c + 1) * A_CHUNK]
        cols = []
        for p in range(A_GROUPS // 2):
            vp = vc[:, p * LANES:(p + 1) * LANES]
            rhs = jnp.concatenate([jnp.where(left, vp, zero), jnp.where(left, zero, vp)], axis=0)
            cols.append(jnp.dot(wsp_ref[p], rhs, preferred_element_type=F32))
        chunks.append(jnp.concatenate(cols, axis=1) + sbf_ref[...])
    mixed = jnp.concatenate(chunks, axis=0)
    a = _bdot(u * mixed, wpa_ref[...])
    ag_ref[...] = (jax.nn.sigmoid(sec(_O_GA, D_MODEL)) * a).astype(BF16)
    iv_ref[...] = sec(_O_I, HG_WIDTH).astype(BF16)


def _inproj(x, p):
    T = x.shape[0]
    tm = INPROJ_TILE
    nt = T // tm
    nh = 2 * tm // (HG_CHUNK // 2)
    tok = lambda w: pl.BlockSpec((tm, w), lambda i: (i, 0))
    outs = (
        jax.ShapeDtypeStruct((T, D_MODEL), BF16),
        jax.ShapeDtypeStruct((T, HG_WIDTH), BF16),
        jax.ShapeDtypeStruct((T, HG_WIDTH), F32),
        jax.ShapeDtypeStruct((T, HG_WIDTH), F32),
        jax.ShapeDtypeStruct((T, HG_WIDTH), BF16),
        jax.ShapeDtypeStruct((T, HG_WIDTH), BF16),
        jax.ShapeDtypeStruct((T, D_MODEL), BF16),
        jax.ShapeDtypeStruct((nt, nh, LANES), F32),
    )
    return pl.pallas_call(
        _inproj_kernel,
        grid=(nt,),
        in_specs=[
            tok(D_MODEL),
            _const_spec((1, D_MODEL)), _const_spec((1, D_MODEL)),
            _const_spec((D_MODEL, N_IN)),
            _const_spec((1, A_WIDTH)), _const_spec((1, A_WIDTH)),
            _const_spec((A_GROUPS // 2, A_CHUNK, 2 * A_CHUNK)),
            _const_spec((A_CHUNK, A_WIDTH)),
            _const_spec((2, HG_WIDTH)),
            _const_spec((A_WIDTH, D_MODEL)),
        ],
        out_specs=(tok(D_MODEL), tok(HG_WIDTH), tok(HG_WIDTH), tok(HG_WIDTH), tok(HG_WIDTH),
                   tok(HG_WIDTH), tok(D_MODEL), pl.BlockSpec((1, nh, LANES), lambda i: (i, 0, 0))),
        out_shape=outs,
        compiler_params=pltpu.CompilerParams(
            dimension_semantics=("parallel",), vmem_limit_bytes=VMEM_MIB["inproj"] << 20),
        name="inproj",
    )(x, p["ln_in_g"], p["ln_in_b"], p["w_in"], p["a_ln_g"], p["a_ln_b"], p["wsp"], p["sbf"],
      p["lb"], p["w_pa"])


_NT = (((1,), (1,)), ((), ()))
_TN = (((0,), (0,)), ((), ()))


def _lockstep(*phased):
    phased = list(phased)
    while phased:
        phased = [g for g in phased if next(g, None) is not None]


def _hgrn_direction(q_ref, g_ref, v_ref, st_ref, o_ref, tri, mask, fwd):
    C = q_ref.shape[0]
    g = g_ref[...]
    ghi = g.astype(BF16)
    glo = (g - ghi.astype(F32)).astype(BF16)
    b = jnp.dot(jnp.concatenate([tri, tri], axis=1), jnp.concatenate([ghi, glo], axis=0),
                preferred_element_type=F32)
    mid = C // 2 - 1 if fwd else C // 2
    end = C - 1 if fwd else 0
    r = b[mid:mid + 1, :]
    b_end = b[end:end + 1, :]
    qt = q_ref[...].astype(F32) * jnp.exp(b - r)
    kt = (1.0 - jnp.exp(g)) * jnp.exp(r - b)
    qtb = qt.astype(BF16)
    ktb = kt.astype(BF16)
    qhb = (qt * jnp.exp(r)).astype(BF16)
    khb = (kt * jnp.exp(b_end - r)).astype(BF16)
    decay = jnp.exp(b_end)
    v = v_ref[...]
    heads =[slice(h * HG_DK, (h + 1) * HG_DK) for h in range(HG_HEADS)]
    zero = jnp.zeros((C, HG_DK), BF16)
    mask2 = jnp.concatenate([mask, mask], axis=1)
    yield True
    scores = []
    for p in range(HG_HEADS // 2):
        k1, k2 = ktb[:, heads[2 * p]], ktb[:, heads[2 * p + 1]]
        kk = jnp.concatenate([jnp.concatenate([k1, zero], axis=1),
                              jnp.concatenate([zero, k2], axis=1)], axis=0)
        s2 = lax.dot_general(qtb[:, 2 * p * HG_DK:(2 * p + 2) * HG_DK], kk, _NT,
                             preferred_element_type=F32)
        s2 = jnp.where(mask2, s2, 0.0).astype(BF16)
        scores += [s2[:, :C], s2[:, C:]]
    yield True
    for h, sl in enumerate(heads):
        st = st_ref[h]
        vt = v[:, sl].T
        o_ref[:, sl] = lax.dot_general(
            jnp.concatenate([scores[h], qhb[:, sl]], axis=1),
            jnp.concatenate([vt, st.astype(BF16)], axis=1), _NT,
            preferred_element_type=F32).astype(o_ref.dtype)
        st_ref[h] = st * decay[:, sl] + jnp.dot(vt, khb[:, sl], preferred_element_type=F32)


def _hgrn_direction_stepwise(q_ref, g_ref, v_ref, st_ref, o_ref, q32_ref, v32_ref, o32_ref, fwd):
    C = q_ref.shape[0]
    q32_ref[...] = q_ref[...].astype(F32)
    v32_ref[...] = v_ref[...].astype(F32)
    sub = SUBLANES
    rows = lax.broadcasted_iota(jnp.int32, (sub, HG_DK), 0)

    def group(i, carry):
        base = pl.multiple_of((i if fwd else C // sub - 1 - i) * sub, sub)
        f = jnp.exp(g_ref[pl.ds(base, sub), :])
        k = 1.0 - f
        q = q32_ref[pl.ds(base, sub), :]
        v = v32_ref[pl.ds(base, sub), :]
        for h in range(HG_HEADS):
            sl = slice(h * HG_DK, (h + 1) * HG_DK)
            st = st_ref[h]
            out = jnp.zeros((sub, HG_DK), F32)
            for r in (range(sub) if fwd else range(sub - 1, -1, -1)):
                v_t = jnp.where(rows == 0, v[r:r + 1, sl], 0.0).astype(BF16)
                k_t = jnp.broadcast_to(k[r:r + 1, sl], (sub, HG_DK)).astype(BF16)
                st = st * f[r:r + 1, sl] + lax.dot_general(v_t, k_t, _TN, preferred_element_type=F32)
                q_t = jnp.broadcast_to(q[r:r + 1, sl], (sub, HG_DK)).astype(BF16)
                o_t = lax.dot_general(q_t, st.astype(BF16), _NT, preferred_element_type=F32)
                out = jnp.where(rows == r, o_t, out)
            st_ref[h] = st
            o32_ref[pl.ds(base, sub), sl] = out
        return carry

    lax.fori_loop(0, C // sub, group, 0)
    o_ref[...] = o32_ref[...].astype(o_ref.dtype)


def _hgrn_kernel(safe_ref, qf_ref, qb_ref, gf_ref, gb_ref, vf_ref, vb_ref, tril_ref, triu_ref,
                 of_ref, ob_ref, sf_ref, sb_ref, q32_ref, v32_ref, o32_ref):
    b, j = pl.program_id(0), pl.program_id(1)
    ns = pl.num_programs(1)
    n = HG_CHUNKS_PER_STEP
    C = qf_ref.shape[0] // n

    @pl.when(j == 0)
    def _():
        sf_ref[...] = jnp.zeros_like(sf_ref)
        sb_ref[...] = jnp.zeros_like(sb_ref)

    row = lax.broadcasted_iota(jnp.int32, (C, C), 0)
    col = lax.broadcasted_iota(jnp.int32, (C, C), 1)
    safe_f = [safe_ref[0, (b * ns + j) * n + u] != 0 for u in range(n)]
    safe_b = [safe_ref[1, (b * ns + ns - 1 - j) * n + u] != 0 for u in range(n)]
    part = lambda ref, u: ref.at[pl.ds(u * C, C), :]

    def forward(u, stepwise):
        refs = (part(qf_ref, u), part(gf_ref, u), part(vf_ref, u), sf_ref, part(of_ref, u))
        if stepwise:
            _hgrn_direction_stepwise(*refs, q32_ref, v32_ref, o32_ref, True)
            return iter(())
        return _hgrn_direction(*refs, tril_ref[...], row >= col, True)

    def backward(u, stepwise):
        refs = (part(qb_ref, u), part(gb_ref, u), part(vb_ref, u), sb_ref, part(ob_ref, u))
        if stepwise:
            _hgrn_direction_stepwise(*refs, q32_ref, v32_ref, o32_ref, False)
            return iter(())
        return _hgrn_direction(*refs, triu_ref[...], row <= col, False)

    all_safe = functools.reduce(jnp.logical_and, safe_f + safe_b)

    @pl.when(all_safe)
    def _():
        _lockstep(*[d for u in range(n) for d in (forward(u, False), backward(n - 1 - u, False))])

    @pl.when(jnp.logical_not(all_safe))
    def _():
        run = lambda direction, u, stepwise: lambda: _lockstep(direction(u, stepwise))
        for u in range(n):
            pl.when(safe_f[u])(run(forward, u, False))
            pl.when(jnp.logical_not(safe_f[u]))(run(forward, u, True))
        for u in reversed(range(n)):
            pl.when(safe_b[u])(run(backward, u, False))
            pl.when(jnp.logical_not(safe_b[u]))(run(backward, u, True))


def _hgrn(q, gf, gb, iv, gmin, batch):
    T = q.shape[0]
    C = HG_CHUNK
    nc = T // batch // C
    nt, nh, _ = gmin.shape
    halves = gmin[:, :, 0].reshape(nt, 2, nh // 4, 2)
    safe = (jnp.min(halves, axis=-1) > -HG_SAFE_LOGDECAY).astype(jnp.int32)
    safe = safe.transpose(1, 0, 2).reshape(2, T // C)
    n = HG_CHUNKS_PER_STEP
    ns = nc // n
    fwd = pl.BlockSpec((n * C, HG_WIDTH), lambda b, j, s: (b * ns + j, 0))
    bwd = pl.BlockSpec((n * C, HG_WIDTH), lambda b, j, s: (b * ns + ns - 1 - j, 0))
    const = lambda shape: pl.BlockSpec(shape, lambda b, j, s: (0,) * len(shape),
                                       pipeline_mode=pl.Buffered(1))
    row = lax.broadcasted_iota(jnp.int32, (C, C), 0)
    col = lax.broadcasted_iota(jnp.int32, (C, C), 1)
    tril = (row >= col).astype(BF16)
    triu = (row <= col).astype(BF16)
    grid_spec = pltpu.PrefetchScalarGridSpec(
        num_scalar_prefetch=1,
        grid=(batch, ns),
        in_specs=[fwd, bwd, fwd, bwd, fwd, bwd, const((C, C)), const((C, C))],
        out_specs=(fwd, bwd),
        scratch_shapes=[pltpu.VMEM((HG_HEADS, HG_DK, HG_DK), F32),
                        pltpu.VMEM((HG_HEADS, HG_DK, HG_DK), F32),
                        pltpu.VMEM((C, HG_WIDTH), F32), pltpu.VMEM((C, HG_WIDTH), F32),
                        pltpu.VMEM((C, HG_WIDTH), F32)],
    )
    return pl.pallas_call(
        _hgrn_kernel,
        grid_spec=grid_spec,
        out_shape=(jax.ShapeDtypeStruct((T, HG_WIDTH), BF16), jax.ShapeDtypeStruct((T, HG_WIDTH), BF16)),
        compiler_params=pltpu.CompilerParams(
            dimension_semantics=("parallel", "arbitrary"), vmem_limit_bytes=VMEM_MIB["hgrn"] << 20),
        name="hgrn",
    )(safe, q, q, gf, gb, iv, iv, tril, triu)


def _mix_kernel(x_ref, lng_ref, lnb_ref, of_ref, ob_ref, so_ref, sgb_ref, ag_ref, ng_ref, wpb_ref,
                wo_ref, l1g_ref, l1b_ref, wr_ref, rb_ref, tri_ref, x1_ref, x1p_ref, route_ref,
                cnt_ref, carry_ref):
    tm = x_ref.shape[0]

    @pl.when(pl.program_id(0) == 0)
    def _():
        carry_ref[...] = jnp.zeros_like(carry_ref)

    o = of_ref[...].astype(F32) + ob_ref[...].astype(F32)
    heads = []
    for h in range(HG_HEADS):
        oh = o[:, h * HG_DK:(h + 1) * HG_DK]
        heads.append(oh * lax.rsqrt(jnp.mean(oh * oh, axis=-1, keepdims=True) + RMS_EPS))
    rn = jnp.concatenate(heads, axis=1) * ng_ref[...] * so_ref[...].astype(F32)
    r = _bdot(rn, wpb_ref[...])
    mixed = ag_ref[...].astype(F32) + sgb_ref[...].astype(F32) * r
    y = _bdot(mixed, wo_ref[...])
    xn = _layer_norm(x_ref[...], lng_ref[...], lnb_ref[...])
    x1 = _layer_norm(ALPHA * xn + y, l1g_ref[...], l1b_ref[...])
    x1_ref[...] = x1
    x1p_ref[...] = _pack_rows(x1)

    neg = jnp.float32(-jnp.inf)
    reps = tm // LANES
    scores = jax.nn.sigmoid(lax.dot_general(wr_ref[...], x1.astype(BF16), _NT,
                                            preferred_element_type=F32))
    biased = (scores + jnp.concatenate([rb_ref[...]] * reps, axis=1)).reshape(
        N_GROUPS, GROUP_SIZE, tm)
    sub = lax.broadcasted_iota(jnp.int32, biased.shape, 1).astype(F32)
    m1 = jnp.max(biased, axis=1, keepdims=True)
    first = jnp.min(jnp.where(biased == m1, sub, float(GROUP_SIZE)), axis=1, keepdims=True)
    m2 = jnp.max(jnp.where(sub == first, neg, biased), axis=1, keepdims=True)
    gs = (m1 + m2).reshape(N_GROUPS, tm)
    grp = lax.broadcasted_iota(jnp.int32, (N_GROUPS, tm), 0)
    ahead = jnp.zeros((N_GROUPS, tm), F32)
    for d in range(1, N_GROUPS):
        other = pltpu.roll(gs, d, 0)
        tie = jnp.where(grp >= d, 1.0, 0.0)
        ahead = ahead + jnp.where(other > gs, 1.0, jnp.where(other == gs, tie, 0.0))
    keep = (ahead < TOPK_GROUPS).reshape(N_GROUPS, 1, tm)
    allowed = jnp.where(keep, biased, neg).reshape(N_EXPERTS, tm)
    row = lax.broadcasted_iota(jnp.int32, (N_EXPERTS, tm), 0).astype(F32)
    sel = jnp.zeros((N_EXPERTS, tm), F32)
    picks = []
    for _ in range(TOP_K):
        m = jnp.max(allowed, axis=0, keepdims=True)
        first = jnp.min(jnp.where(allowed == m, row, float(N_EXPERTS)), axis=0, keepdims=True)
        hit = row == first
        picks.append((first, hit, jnp.sum(jnp.where(hit, scores, 0.0), axis=0, keepdims=True)))
        sel = jnp.where(hit, 1.0, sel)
        allowed = jnp.where(hit, neg, allowed)
    wsum = picks[0][2]
    for pk in picks[1:]:
        wsum = wsum + pk[2]
    selb = sel.astype(BF16)
    carry = carry_ref[...]
    before = (jnp.dot(selb, tri_ref[...], preferred_element_type=F32)
              + jnp.concatenate([carry] * reps, axis=1))
    total = carry + jnp.dot(selb, jnp.ones((tm, LANES), BF16), preferred_element_type=F32)
    carry_ref[...] = total
    cnt_ref[...] = total
    blank = [jnp.zeros((1, tm), F32)] * (SUBLANES - TOP_K)
    route_ref[...] = jnp.concatenate(
        [pk[0] for pk in picks] + blank
        + [jnp.sum(jnp.where(pk[1], before, 0.0), axis=0, keepdims=True) for pk in picks] + blank
        + [pk[2] / wsum * ROUTED_SCALE for pk in picks] + blank, axis=0)


def _mix(x, of, ob, so, sgb, ag, p):
    T = x.shape[0]
    tm = TOKEN_TILE
    tok = lambda w: pl.BlockSpec((tm, w), lambda i: (i, 0))
    row = lax.broadcasted_iota(jnp.int32, (tm, tm), 0)
    col = lax.broadcasted_iota(jnp.int32, (tm, tm), 1)
    tri = (row < col).astype(BF16)
    return pl.pallas_call(
        _mix_kernel,
        grid=(T // tm,),
        in_specs=[
            tok(D_MODEL), _const_spec((1, D_MODEL)), _const_spec((1, D_MODEL)),
            tok(HG_WIDTH), tok(HG_WIDTH), tok(HG_WIDTH), tok(D_MODEL), tok(D_MODEL),
            _const_spec((1, HG_WIDTH)),
            _const_spec((HG_WIDTH, D_MODEL)), _const_spec((D_MODEL, D_MODEL)),
            _const_spec((1, D_MODEL)), _const_spec((1, D_MODEL)),
            _const_spec((N_EXPERTS, D_MODEL)), _const_spec((N_EXPERTS, LANES)),
            _const_spec((tm, tm)),
        ],
        out_specs=(tok(D_MODEL), tok(D_MODEL // 2), pl.BlockSpec((ROUTE_ROWS, tm), lambda i: (0, i)),
                   pl.BlockSpec((N_EXPERTS, LANES), lambda i: (0, 0))),
        out_shape=(jax.ShapeDtypeStruct((T, D_MODEL), F32),
                   jax.ShapeDtypeStruct((T, D_MODEL // 2), jnp.uint32),
                   jax.ShapeDtypeStruct((ROUTE_ROWS, T), F32),
                   jax.ShapeDtypeStruct((N_EXPERTS, LANES), F32)),
        scratch_shapes=[pltpu.VMEM((N_EXPERTS, LANES), F32)],
        compiler_params=pltpu.CompilerParams(
            dimension_semantics=("arbitrary",), vmem_limit_bytes=VMEM_MIB["mix"] << 20),
        name="mix",
    )(x, p["ln_in_g"], p["ln_in_b"], of, ob, so, sgb, ag, p["hg_norm_g"], p["w_pb"], p["w_o"],
      p["ln1_g"], p["ln1_b"], p["w_router"], p["router_bias"], tri)


def _sc_mesh():
    return plsc.VectorSubcoreMesh(core_axis_name="c", subcore_axis_name="s",
                                  num_cores=SC_CORES, num_subcores=SC_SUBCORES)


def _sc_worker():
    return lax.axis_index("s") * SC_CORES + lax.axis_index("c")


def _sc_dispatch(x1p, dest, n_rows):
    T, w = x1p.shape
    W = SC_WINDOW
    per_worker = T // W // (SC_CORES * SC_SUBCORES)

    def body(x_hbm, d_hbm, o_hbm, rows_v, idx_v, sem):
        first = _sc_worker() * per_worker
        for k in range(TOP_K):
            pltpu.sync_copy(d_hbm.at[k, pl.ds(first, per_worker)], idx_v.at[k])

        @pl.loop(0, per_worker)
        def _(j):
            pltpu.sync_copy(x_hbm.at[pl.ds((first + j) * W, W)], rows_v)
            copies = [pltpu.async_copy(rows_v, o_hbm.at[idx_v.at[k, j]], sem) for k in range(TOP_K)]
            for c in copies:
                c.wait()

    return pl.kernel(
        body,
        out_type=jax.ShapeDtypeStruct((n_rows, w), jnp.uint32),
        mesh=_sc_mesh(),
        scratch_types=[pltpu.VMEM((W, w), jnp.uint32), pltpu.VMEM((TOP_K, per_worker, W), jnp.int32),
                       pltpu.SemaphoreType.DMA],
        name="sc_dispatch",
    )(x1p, dest.reshape(TOP_K, T // W, W))


def _sc_combine(ys, dest, gates):
    T = dest.shape[1]
    W = SC_COMBINE_WINDOW
    w = ys.shape[1]
    per_worker = T // W // (SC_CORES * SC_SUBCORES)
    assert per_worker % 2 == 0 and per_worker * W * SC_CORES * SC_SUBCORES == T
    assert 2 * W == SC_LANES

    def body(y_hbm, d_hbm, g_hbm, o_hbm, idx_v, gate_v, *scratch):
        slots = [dict(rows=scratch[s], out=scratch[2 + s], gsem=scratch[4 + s], wsem=scratch[6 + s],
                      lane0=s * W) for s in range(2)]
        first = _sc_worker() * per_worker
        for k in range(TOP_K):
            pltpu.sync_copy(d_hbm.at[k, pl.ds(first * W, per_worker * W)], idx_v.at[k])
            pltpu.sync_copy(g_hbm.at[k, pl.ds(first * W, per_worker * W)], gate_v.at[k])

        def loads(b, win):
            i = win - first
            return [pltpu.make_async_copy(y_hbm.at[idx_v.at[k, pl.ds(i * W, W)]],
                                          b["rows"].at[k], b["gsem"]) for k in range(TOP_K)]

        def write_back(b, win):
            return pltpu.make_async_copy(b["out"], o_hbm.at[pl.ds(win * W, W)], b["wsem"])

        def fetch(b, win):
            for c in loads(b, win):
                c.start()

        def reduce_rows(b, pair_gates):
            rows, out = b["rows"], b["out"]
            lane = lax.iota(jnp.int32, SC_LANES)

            @pl.loop(0, W)
            def _(j):
                mine = lane == b["lane0"] + j
                gate = [jnp.broadcast_to(jnp.sum(jnp.where(mine, g, 0.0)), (SC_LANES,))
                        for g in pair_gates]

                @plsc.parallel_loop(0, w, step=SC_LANES, unroll=4)
                def _(col):
                    lo, hi = [], []
                    for k in range(TOP_K):
                        p = rows[k, j, pl.ds(col, SC_LANES)]
                        lo.append(plsc.bitcast(p << 16, F32) * gate[k])
                        hi.append(plsc.bitcast(p & jnp.uint32(0xFFFF0000), F32) * gate[k])
                    out[j, pl.ds(col, SC_LANES)] = (lo[0] + lo[1]) + (lo[2] + lo[3]) + (lo[4] + lo[5])
                    out[j, pl.ds(w + col, SC_LANES)] = (hi[0] + hi[1]) + (hi[2] + hi[3]) + (hi[4] + hi[5])

        def process(b, win, pair_gates, not_first):
            for c in loads(b, win):
                c.wait()

            @pl.when(not_first)
            def _():
                write_back(b, win).wait()

            reduce_rows(b, pair_gates)
            write_back(b, win).start()

        fetch(slots[0], first)

        @pl.loop(0, per_worker, step=2)
        def _(i):
            win = first + i
            pair_gates = [gate_v[k, pl.ds(i * W, SC_LANES)] for k in range(TOP_K)]
            fetch(slots[1], win + 1)
            process(slots[0], win, pair_gates, i > 0)

            @pl.when(i + 2 < per_worker)
            def _():
                fetch(slots[0], win + 2)

            process(slots[1], win + 1, pair_gates, i > 0)

        for b in slots:
            write_back(b, first).wait()

    slot_types = ([pltpu.VMEM((TOP_K, per_worker * W), jnp.int32),
                   pltpu.VMEM((TOP_K, per_worker * W), F32)]
                  + [pltpu.VMEM((TOP_K, W, w), jnp.uint32)] * 2
                  + [pltpu.VMEM((W, 2 * w), F32)] * 2
                  + [pltpu.SemaphoreType.DMA] * 4)
    return pl.kernel(
        body,
        out_type=jax.ShapeDtypeStruct((T, 2 * w), F32),
        mesh=_sc_mesh(),
        scratch_types=slot_types,
        compiler_params=pltpu.CompilerParams(needs_layout_passes=False),
        name="sc_combine",
    )(ys, dest, gates)


def _experts_kernel(be_ref, nv_ref, io_ref, xs_ref, wg_ref, wu_ref, wd_ref, ys_ref, *cast_refs):
    del io_ref
    i = pl.program_id(0)
    n_valid = nv_ref[i]
    mb, w = xs_ref.shape
    if cast_refs:
        @pl.when((i == 0) | (be_ref[i] != be_ref[jnp.maximum(i, 1) - 1]))
        def _():
            for src, dst in zip((wg_ref, wu_ref, wd_ref), cast_refs):
                dst[...] = src[...].astype(BF16)
        wg_ref, wu_ref, wd_ref = cast_refs

    @pl.when(n_valid > 0)
    def _():
        keep = lax.broadcasted_iota(jnp.int32, (mb, w), 0) < n_valid
        lo, hi = _unpack_rows(jnp.where(keep, xs_ref[...], jnp.uint32(0)))
        lo, hi = lo.astype(BF16), hi.astype(BF16)

        def proj(w_ref):
            return (jnp.dot(lo, w_ref[0, :w, :], preferred_element_type=F32)
                    + jnp.dot(hi, w_ref[0, w:, :], preferred_element_type=F32))

        hb = jax.nn.silu(proj(wg_ref)) * proj(wu_ref)
        ys_ref[...] = _pack_rows(jnp.dot(hb.astype(BF16), wd_ref[0], preferred_element_type=F32))


def _experts(xs, blk_exp, n_valid, blk_io, weights):
    n_rows, w = xs.shape
    mb = n_rows // blk_exp.shape[0]
    cast = weights[0].dtype != BF16
    by_expert = lambda shape: pl.BlockSpec((1,) + shape, lambda i, be, nv, io: (be[i], 0, 0))
    w_specs = [by_expert((D_MODEL, EXPERT_DIM)), by_expert((D_MODEL, EXPERT_DIM)),
               by_expert((EXPERT_DIM, D_MODEL))]
    ys_spec = pl.BlockSpec((mb, w), lambda i, be, nv, io: (io[i], 0))
    ys_shape = jax.ShapeDtypeStruct((n_rows, w), jnp.uint32)
    grid_spec = pltpu.PrefetchScalarGridSpec(
        num_scalar_prefetch=3,
        grid=(n_rows // mb,),
        in_specs=[pl.BlockSpec((mb, w), lambda i, be, nv, io: (io[i], 0))] + w_specs,
        out_specs=(ys_spec, *w_specs) if cast else ys_spec,
    )
    out = pl.pallas_call(
        _experts_kernel,
        grid_spec=grid_spec,
        out_shape=(ys_shape, *[jax.ShapeDtypeStruct(x.shape, BF16) for x in weights]) if cast else ys_shape,
        compiler_params=pltpu.CompilerParams(
            dimension_semantics=("arbitrary",), vmem_limit_bytes=VMEM_MIB["experts"] << 20),
        name="experts",
    )(blk_exp, n_valid, blk_io, xs, *weights)
    return (out[0], tuple(out[1:])) if cast else (out, weights)


def _final_kernel(x1_ref, routed_ref, wsg_ref, wsu_ref, wsd_ref, l2g_ref, l2b_ref, out_ref):
    x1 = x1_ref[...]
    xb = x1.astype(BF16)
    hs = (jax.nn.silu(jnp.dot(xb, wsg_ref[...], preferred_element_type=F32))
          * jnp.dot(xb, wsu_ref[...], preferred_element_type=F32))
    shared = jnp.dot(hs.astype(BF16), wsd_ref[...], preferred_element_type=F32)
    out_ref[...] = _layer_norm(ALPHA * x1 + (routed_ref[...] + shared), l2g_ref[...], l2b_ref[...])


def _final(x1, routed, p):
    T = x1.shape[0]
    tm = FINAL_TILE
    tok = lambda w: pl.BlockSpec((tm, w), lambda i: (i, 0))
    return pl.pallas_call(
        _final_kernel,
        grid=(T // tm,),
        in_specs=[
            tok(D_MODEL), tok(D_MODEL),
            _const_spec((D_MODEL, SHARED_DIM)), _const_spec((D_MODEL, SHARED_DIM)),
            _const_spec((SHARED_DIM, D_MODEL)),
            _const_spec((1, D_MODEL)), _const_spec((1, D_MODEL)),
        ],
        out_specs=tok(D_MODEL),
        out_shape=jax.ShapeDtypeStruct((T, D_MODEL), F32),
        compiler_params=pltpu.CompilerParams(
            dimension_semantics=("parallel",), vmem_limit_bytes=VMEM_MIB["final"] << 20),
        name="final",
    )(x1, routed, p["w_sh_gate"], p["w_sh_up"], p["w_sh_down"], p["ln2_g"], p["ln2_b"])


def _routing_layout(route, counts, n_tokens):
    mb = EXPERT_BLOCK
    n_blocks = -(-n_tokens * TOP_K // mb) + N_EXPERTS
    idx = route[0:TOP_K].astype(jnp.int32)
    rank = route[SUBLANES:SUBLANES + TOP_K].astype(jnp.int32)
    counts = counts.astype(jnp.int32)
    padded = jnp.maximum((counts + mb - 1) // mb, 1) * mb
    pad_end = jnp.cumsum(padded)
    pad_start = pad_end - padded
    experts = jnp.arange(N_EXPERTS, dtype=jnp.int32)
    dest = rank + jnp.sum(jnp.where(idx[:, :, None] == experts, pad_start, 0), axis=-1)
    blk_start = jnp.arange(n_blocks, dtype=jnp.int32) * mb
    blk_exp = jnp.minimum(
        jnp.sum((pad_end[None, :] <= blk_start[:, None]).astype(jnp.int32), axis=1), N_EXPERTS - 1)
    valid_end = jnp.sum(jnp.where(blk_exp[:, None] == experts, pad_start + counts, 0), axis=-1)
    n_valid = jnp.clip(valid_end - blk_start, 0, mb).astype(jnp.int32)
    blk_io = jnp.minimum(jnp.arange(n_blocks, dtype=jnp.int32), pad_end[-1] // mb - 1)
    gates = route[2 * SUBLANES:2 * SUBLANES + TOP_K]
    return dest, gates, blk_exp, n_valid, blk_io, n_blocks * mb


def _encode(x, p, expert_weights):
    batch, seq, _ = x.shape
    T = batch * seq
    xt = x.reshape(T, D_MODEL)
    ag, q, gf, gb, iv, so, sgb, gmin = _inproj(xt, p)
    of, ob = _hgrn(q, gf, gb, iv, gmin, batch)
    x1, x1p, route, cnt = _mix(xt, of, ob, so, sgb, ag, p)
    dest, gates, blk_exp, n_valid, blk_io, n_rows = _routing_layout(route, cnt[:, 0], T)
    xs = _sc_dispatch(x1p, dest, n_rows)
    ys, expert_weights = _experts(xs, blk_exp, n_valid, blk_io, expert_weights)
    out = _final(x1, _sc_combine(ys, dest, gates), p)
    return out.reshape(batch, seq, D_MODEL), expert_weights


def _prepare_params(ln_in_g, ln_in_b, w_in, a_ln_g, a_ln_b, a_ws, a_sb, hg_lb_logits, hg_norm_g,
                    w_pa, w_pb, w_o, ln1_g, ln1_b, w_router, router_bias, w_sh_gate, w_sh_up,
                    w_sh_down, ln2_g, ln2_b):
    l = 0
    row = lambda v: v.reshape(1, -1).astype(F32)
    ws = a_ws[l].astype(BF16)
    wsp = jnp.concatenate([ws[0::2], ws[1::2]], axis=2)
    sbf = jnp.repeat(a_sb[l].astype(F32), A_WIDTH // A_GROUPS, axis=1)
    lb = jnp.cumsum(jax.nn.softmax(hg_lb_logits.astype(F32), axis=1), axis=1)[:, l]
    return dict(
        ln_in_g=row(ln_in_g), ln_in_b=row(ln_in_b), w_in=w_in[l].astype(BF16),
        a_ln_g=row(a_ln_g[l]), a_ln_b=row(a_ln_b[l]), wsp=wsp, sbf=sbf, lb=lb,
        w_pa=w_pa[l].astype(BF16), hg_norm_g=row(hg_norm_g[l]),
        w_pb=w_pb[l].astype(BF16), w_o=w_o[l].astype(BF16),
        ln1_g=row(ln1_g[l]), ln1_b=row(ln1_b[l]),
        w_router=w_router[l].T.astype(BF16),
        router_bias=jnp.broadcast_to(router_bias[l].astype(F32)[:, None], (N_EXPERTS, LANES)),
        w_sh_gate=w_sh_gate[l].astype(BF16), w_sh_up=w_sh_up[l].astype(BF16),
        w_sh_down=w_sh_down[l].astype(BF16),
        ln2_g=row(ln2_g[l]), ln2_b=row(ln2_b[l]),
    )


def kernel(x_prompt, x_sample, ln_in_g, ln_in_b, w_in, a_ln_g, a_ln_b, a_ws, a_sb, hg_lb_logits,
           hg_norm_g, w_pa, w_pb, w_o, ln1_g, ln1_b, w_router, router_bias, w_e_gate, w_e_up,
           w_e_down, w_sh_gate, w_sh_up, w_sh_down, ln2_g, ln2_b):
    p = _prepare_params(ln_in_g, ln_in_b, w_in, a_ln_g, a_ln_b, a_ws, a_sb, hg_lb_logits, hg_norm_g,
                        w_pa, w_pb, w_o, ln1_g, ln1_b, w_router, router_bias, w_sh_gate, w_sh_up,
                        w_sh_down, ln2_g, ln2_b)
    y_prompt, expert_weights = _encode(x_prompt, p, (w_e_gate[0], w_e_up[0], w_e_down[0]))
    y_sample, _ = _encode(x_sample, p, expert_weights)
    return y_prompt, y_sample
```

```python
import functools

import jax
import jax.numpy as jnp
from jax import lax
from jax.experimental import pallas as pl
from jax.experimental.pallas import tpu as pltpu
from jax.experimental.pallas import tpu_sc as plsc

F32 = jnp.float32
BF16 = jnp.bfloat16

D_MODEL = 1024
A_GROUPS = 8
A_WIDTH = 512
A_CHUNK = 128
HG_HEADS = 8
HG_DK = 128
HG_WIDTH = HG_HEADS * HG_DK
N_IN = 2 * A_WIDTH + 5 * HG_WIDTH + 2 * D_MODEL
N_EXPERTS = 64
TOP_K = 6
N_GROUPS = 8
TOPK_GROUPS = 4
GROUP_SIZE = N_EXPERTS // N_GROUPS
EXPERT_DIM = 256
SHARED_DIM = 256
ROUTED_SCALE = 2.5
DEPTH = 1
ALPHA = (2.0 * DEPTH) ** 0.25
LN_EPS = 1e-5
RMS_EPS = 1e-6

LANES = 128
SUBLANES = 8
ROUTE_ROWS = 3 * SUBLANES
TOKEN_TILE = 512
FINAL_TILE = 1024
INPROJ_TILE = 256
HG_CHUNK = 128
HG_CHUNKS_PER_STEP = 2
HG_SAFE_LOGDECAY = 80.0
EXPERT_BLOCK = 1024
SC_CORES = 2
SC_SUBCORES = 16
SC_LANES = 16
SC_WINDOW = 32
SC_COMBINE_WINDOW = 8
VMEM_MIB = dict(inproj=40, hgrn=32, mix=40, experts=32, final=40)

_O_U, _O_V, _O_Q, _O_FF, _O_FB, _O_I, _O_G, _O_GA, _O_GB = (
    0, 512, 1024, 2048, 3072, 4096, 5120, 6144, 7168)


def _layer_norm(x, g, b):
    mu = jnp.mean(x, axis=-1, keepdims=True)
    xc = x - mu
    var = jnp.mean(xc * xc, axis=-1, keepdims=True)
    return xc * lax.rsqrt(var + LN_EPS) * g + b


def _bdot(a, b):
    return jnp.dot(a.astype(BF16), b.astype(BF16), preferred_element_type=F32)


def _pack_rows(x):
    w = x.shape[1] // 2
    lo = lax.bitcast_convert_type(x[:, :w].astype(BF16).astype(F32), jnp.uint32)
    hi = lax.bitcast_convert_type(x[:, w:].astype(BF16).astype(F32), jnp.uint32)
    return hi | (lo >> 16)


def _unpack_rows(p):
    lo = lax.bitcast_convert_type(p << 16, F32)
    hi = lax.bitcast_convert_type(p & jnp.uint32(0xFFFF0000), F32)
    return lo, hi


def _const_spec(shape):
    nd = len(shape)
    return pl.BlockSpec(shape, lambda *_: (0,) * nd, pipeline_mode=pl.Buffered(1))


def _inproj_kernel(x_ref, lng_ref, lnb_ref, win_ref, alng_ref, alnb_ref, wsp_ref, sbf_ref, lb_ref,
                   wpa_ref, ag_ref, q_ref, gf_ref, gb_ref, iv_ref, so_ref, sgb_ref, gmin_ref):
    tm = x_ref.shape[0]
    xb = _layer_norm(x_ref[...], lng_ref[...], lnb_ref[...]).astype(BF16)

    def sec(lo, width):
        return jnp.dot(xb, win_ref[:, lo:lo + width], preferred_element_type=F32)

    v = _layer_norm(jax.nn.gelu(sec(_O_V, A_WIDTH)), alng_ref[...], alnb_ref[...]).astype(BF16)
    u = jax.nn.gelu(sec(_O_U, A_WIDTH))

    q_ref[...] = jax.nn.silu(sec(_O_Q, HG_WIDTH)).astype(BF16)
    mins = []
    for d, (off, g_ref) in enumerate(((_O_FF, gf_ref), (_O_FB, gb_ref))):
        lb = lb_ref[d:d + 1, :]
        f = lb + (1.0 - lb) * jax.nn.sigmoid(sec(off, HG_WIDTH))
        g = jnp.log(f)
        g_ref[...] = g
        half = jnp.sum(g.reshape(tm // (HG_CHUNK // 2), HG_CHUNK // 2, HG_WIDTH), axis=1)
        mins.append(jnp.min(half, axis=-1, keepdims=True))
    gmin_ref[0] = jnp.broadcast_to(jnp.concatenate(mins, axis=0), gmin_ref.shape[1:])
    so_ref[...] = jax.nn.silu(sec(_O_G, HG_WIDTH)).astype(BF16)
    sgb_ref[...] = jax.nn.sigmoid(sec(_O_GB, D_MODEL)).astype(BF16)

    lane = lax.broadcasted_iota(jnp.int32, (A_CHUNK, LANES), 1)
    left = lane < (A_WIDTH // A_GROUPS)
    zero = jnp.zeros((A_CHUNK, LANES), BF16)
    chunks = []
    for c in range(tm // A_CHUNK):
        vc = v[c * A_CHUNK:(c + 1) * A_CHUNK]
        cols = []
        for p in range(A_GROUPS // 2):
            vp = vc[:, p * LANES:(p + 1) * LANES]
            rhs = jnp.concatenate([jnp.where(left, vp, zero), jnp.where(left, zero, vp)], axis=0)
            cols.append(jnp.dot(wsp_ref[p], rhs, preferred_element_type=F32))
        chunks.append(jnp.concatenate(cols, axis=1) + sbf_ref[...])
    mixed = jnp.concatenate(chunks, axis=0)
    a = _bdot(u * mixed, wpa_ref[...])
    ag_ref[...] = (jax.nn.sigmoid(sec(_O_GA, D_MODEL)) * a).astype(BF16)
    iv_ref[...] = sec(_O_I, HG_WIDTH).astype(BF16)


def _inproj(x, p):
    T = x.shape[0]
    tm = INPROJ_TILE
    nt = T // tm
    nh = 2 * tm // (HG_CHUNK // 2)
    tok = lambda w: pl.BlockSpec((tm, w), lambda i: (i, 0))
    outs = (
        jax.ShapeDtypeStruct((T, D_MODEL), BF16),
        jax.ShapeDtypeStruct((T, HG_WIDTH), BF16),
        jax.ShapeDtypeStruct((T, HG_WIDTH), F32),
        jax.ShapeDtypeStruct((T, HG_WIDTH), F32),
        jax.ShapeDtypeStruct((T, HG_WIDTH), BF16),
        jax.ShapeDtypeStruct((T, HG_WIDTH), BF16),
        jax.ShapeDtypeStruct((T, D_MODEL), BF16),
        jax.ShapeDtypeStruct((nt, nh, LANES), F32),
    )
    return pl.pallas_call(
        _inproj_kernel,
        grid=(nt,),
        in_specs=[
            tok(D_MODEL),
            _const_spec((1, D_MODEL)), _const_spec((1, D_MODEL)),
            _const_spec((D_MODEL, N_IN)),
            _const_spec((1, A_WIDTH)), _const_spec((1, A_WIDTH)),
            _const_spec((A_GROUPS // 2, A_CHUNK, 2 * A_CHUNK)),
            _const_spec((A_CHUNK, A_WIDTH)),
            _const_spec((2, HG_WIDTH)),
            _const_spec((A_WIDTH, D_MODEL)),
        ],
        out_specs=(tok(D_MODEL), tok(HG_WIDTH), tok(HG_WIDTH), tok(HG_WIDTH), tok(HG_WIDTH),
                   tok(HG_WIDTH), tok(D_MODEL), pl.BlockSpec((1, nh, LANES), lambda i: (i, 0, 0))),
        out_shape=outs,
        compiler_params=pltpu.CompilerParams(
            dimension_semantics=("parallel",), vmem_limit_bytes=VMEM_MIB["inproj"] << 20),
        name="inproj",
    )(x, p["ln_in_g"], p["ln_in_b"], p["w_in"], p["a_ln_g"], p["a_ln_b"], p["wsp"], p["sbf"],
      p["lb"], p["w_pa"])


_NT = (((1,), (1,)), ((), ()))
_TN = (((0,), (0,)), ((), ()))


def _lockstep(*phased):
    phased = list(phased)
    while phased:
        phased = [g for g in phased if next(g, None) is not None]


def _hgrn_direction(q_ref, g_ref, v_ref, st_ref, o_ref, tri, mask, fwd):
    C = q_ref.shape[0]
    g = g_ref[...]
    ghi = g.astype(BF16)
    glo = (g - ghi.astype(F32)).astype(BF16)
    b = jnp.dot(jnp.concatenate([tri, tri], axis=1), jnp.concatenate([ghi, glo], axis=0),
                preferred_element_type=F32)
    mid = C // 2 - 1 if fwd else C // 2
    end = C - 1 if fwd else 0
    r = b[mid:mid + 1, :]
    b_end = b[end:end + 1, :]
    qt = q_ref[...].astype(F32) * jnp.exp(b - r)
    kt = (1.0 - jnp.exp(g)) * jnp.exp(r - b)
    qtb = qt.astype(BF16)
    ktb = kt.astype(BF16)
    qhb = (qt * jnp.exp(r)).astype(BF16)
    khb = (kt * jnp.exp(b_end - r)).astype(BF16)
    decay = jnp.exp(b_end)
    v = v_ref[...]
    heads =[slice(h * HG_DK, (h + 1) * HG_DK) for h in range(HG_HEADS)]
    zero = jnp.zeros((C, HG_DK), BF16)
    mask2 = jnp.concatenate([mask, mask], axis=1)
    yield True
    scores = []
    for p in range(HG_HEADS // 2):
        k1, k2 = ktb[:, heads[2 * p]], ktb[:, heads[2 * p + 1]]
        kk = jnp.concatenate([jnp.concatenate([k1, zero], axis=1),
                              jnp.concatenate([zero, k2], axis=1)], axis=0)
        s2 = lax.dot_general(qtb[:, 2 * p * HG_DK:(2 * p + 2) * HG_DK], kk, _NT,
                             preferred_element_type=F32)
        s2 = jnp.where(mask2, s2, 0.0).astype(BF16)
        scores += [s2[:, :C], s2[:, C:]]
    yield True
    for h, sl in enumerate(heads):
        st = st_ref[h]
        vt = v[:, sl].T
        o_ref[:, sl] = lax.dot_general(
            jnp.concatenate([scores[h], qhb[:, sl]], axis=1),
            jnp.concatenate([vt, st.astype(BF16)], axis=1), _NT,
            preferred_element_type=F32).astype(o_ref.dtype)
        st_ref[h] = st * decay[:, sl] + jnp.dot(vt, khb[:, sl], preferred_element_type=F32)


def _hgrn_direction_stepwise(q_ref, g_ref, v_ref, st_ref, o_ref, q32_ref, v32_ref, o32_ref, fwd):
    C = q_ref.shape[0]
    q32_ref[...] = q_ref[...].astype(F32)
    v32_ref[...] = v_ref[...].astype(F32)
    sub = SUBLANES
    rows = lax.broadcasted_iota(jnp.int32, (sub, HG_DK), 0)

    def group(i, carry):
        base = pl.multiple_of((i if fwd else C // sub - 1 - i) * sub, sub)
        f = jnp.exp(g_ref[pl.ds(base, sub), :])
        k = 1.0 - f
        q = q32_ref[pl.ds(base, sub), :]
        v = v32_ref[pl.ds(base, sub), :]
        for h in range(HG_HEADS):
            sl = slice(h * HG_DK, (h + 1) * HG_DK)
            st = st_ref[h]
            out = jnp.zeros((sub, HG_DK), F32)
            for r in (range(sub) if fwd else range(sub - 1, -1, -1)):
                v_t = jnp.where(rows == 0, v[r:r + 1, sl], 0.0).astype(BF16)
                k_t = jnp.broadcast_to(k[r:r + 1, sl], (sub, HG_DK)).astype(BF16)
                st = st * f[r:r + 1, sl] + lax.dot_general(v_t, k_t, _TN, preferred_element_type=F32)
                q_t = jnp.broadcast_to(q[r:r + 1, sl], (sub, HG_DK)).astype(BF16)
                o_t = lax.dot_general(q_t, st.astype(BF16), _NT, preferred_element_type=F32)
                out = jnp.where(rows == r, o_t, out)
            st_ref[h] = st
            o32_ref[pl.ds(base, sub), sl] = out
        return carry

    lax.fori_loop(0, C // sub, group, 0)
    o_ref[...] = o32_ref[...].astype(o_ref.dtype)


def _hgrn_kernel(safe_ref, qf_ref, qb_ref, gf_ref, gb_ref, vf_ref, vb_ref, tril_ref, triu_ref,
                 of_ref, ob_ref, sf_ref, sb_ref, q32_ref, v32_ref, o32_ref):
    b, j = pl.program_id(0), pl.program_id(1)
    ns = pl.num_programs(1)
    n = HG_CHUNKS_PER_STEP
    C = qf_ref.shape[0] // n

    @pl.when(j == 0)
    def _():
        sf_ref[...] = jnp.zeros_like(sf_ref)
        sb_ref[...] = jnp.zeros_like(sb_ref)

    row = lax.broadcasted_iota(jnp.int32, (C, C), 0)
    col = lax.broadcasted_iota(jnp.int32, (C, C), 1)
    safe_f = [safe_ref[0, (b * ns + j) * n + u] != 0 for u in range(n)]
    safe_b = [safe_ref[1, (b * ns + ns - 1 - j) * n + u] != 0 for u in range(n)]
    part = lambda ref, u: ref.at[pl.ds(u * C, C), :]

    def forward(u, stepwise):
        refs = (part(qf_ref, u), part(gf_ref, u), part(vf_ref, u), sf_ref, part(of_ref, u))
        if stepwise:
            _hgrn_direction_stepwise(*refs, q32_ref, v32_ref, o32_ref, True)
            return iter(())
        return _hgrn_direction(*refs, tril_ref[...], row >= col, True)

    def backward(u, stepwise):
        refs = (part(qb_ref, u), part(gb_ref, u), part(vb_ref, u), sb_ref, part(ob_ref, u))
        if stepwise:
            _hgrn_direction_stepwise(*refs, q32_ref, v32_ref, o32_ref, False)
            return iter(())
        return _hgrn_direction(*refs, triu_ref[...], row <= col, False)

    all_safe = functools.reduce(jnp.logical_and, safe_f + safe_b)

    @pl.when(all_safe)
    def _():
        _lockstep(*[d for u in range(n) for d in (forward(u, False), backward(n - 1 - u, False))])

    @pl.when(jnp.logical_not(all_safe))
    def _():
        run = lambda direction, u, stepwise: lambda: _lockstep(direction(u, stepwise))
        for u in range(n):
            pl.when(safe_f[u])(run(forward, u, False))
            pl.when(jnp.logical_not(safe_f[u]))(run(forward, u, True))
        for u in reversed(range(n)):
            pl.when(safe_b[u])(run(backward, u, False))
            pl.when(jnp.logical_not(safe_b[u]))(run(backward, u, True))


def _hgrn(q, gf, gb, iv, gmin, batch):
    T = q.shape[0]
    C = HG_CHUNK
    nc = T // batch // C
    nt, nh, _ = gmin.shape
    halves = gmin[:, :, 0].reshape(nt, 2, nh // 4, 2)
    safe = (jnp.min(halves, axis=-1) > -HG_SAFE_LOGDECAY).astype(jnp.int32)
    safe = safe.transpose(1, 0, 2).reshape(2, T // C)
    n = HG_CHUNKS_PER_STEP
    ns = nc // n
    fwd = pl.BlockSpec((n * C, HG_WIDTH), lambda b, j, s: (b * ns + j, 0))
    bwd = pl.BlockSpec((n * C, HG_WIDTH), lambda b, j, s: (b * ns + ns - 1 - j, 0))
    const = lambda shape: pl.BlockSpec(shape, lambda b, j, s: (0,) * len(shape),
                                       pipeline_mode=pl.Buffered(1))
    row = lax.broadcasted_iota(jnp.int32, (C, C), 0)
    col = lax.broadcasted_iota(jnp.int32, (C, C), 1)
    tril = (row >= col).astype(BF16)
    triu = (row <= col).astype(BF16)
    grid_spec = pltpu.PrefetchScalarGridSpec(
        num_scalar_prefetch=1,
        grid=(batch, ns),
        in_specs=[fwd, bwd, fwd, bwd, fwd, bwd, const((C, C)), const((C, C))],
        out_specs=(fwd, bwd),
        scratch_shapes=[pltpu.VMEM((HG_HEADS, HG_DK, HG_DK), F32),
                        pltpu.VMEM((HG_HEADS, HG_DK, HG_DK), F32),
                        pltpu.VMEM((C, HG_WIDTH), F32), pltpu.VMEM((C, HG_WIDTH), F32),
                        pltpu.VMEM((C, HG_WIDTH), F32)],
    )
    return pl.pallas_call(
        _hgrn_kernel,
        grid_spec=grid_spec,
        out_shape=(jax.ShapeDtypeStruct((T, HG_WIDTH), BF16), jax.ShapeDtypeStruct((T, HG_WIDTH), BF16)),
        compiler_params=pltpu.CompilerParams(
            dimension_semantics=("parallel", "arbitrary"), vmem_limit_bytes=VMEM_MIB["hgrn"] << 20),
        name="hgrn",
    )(safe, q, q, gf, gb, iv, iv, tril, triu)


def _mix_kernel(x_ref, lng_ref, lnb_ref, of_ref, ob_ref, so_ref, sgb_ref, ag_ref, ng_ref, wpb_ref,
                wo_ref, l1g_ref, l1b_ref, wr_ref, rb_ref, tri_ref, x1_ref, x1p_ref, route_ref,
                cnt_ref, carry_ref):
    tm = x_ref.shape[0]

    @pl.when(pl.program_id(0) == 0)
    def _():
        carry_ref[...] = jnp.zeros_like(carry_ref)

    o = of_ref[...].astype(F32) + ob_ref[...].astype(F32)
    heads = []
    for h in range(HG_HEADS):
        oh = o[:, h * HG_DK:(h + 1) * HG_DK]
        heads.append(oh * lax.rsqrt(jnp.mean(oh * oh, axis=-1, keepdims=True) + RMS_EPS))
    rn = jnp.concatenate(heads, axis=1) * ng_ref[...] * so_ref[...].astype(F32)
    r = _bdot(rn, wpb_ref[...])
    mixed = ag_ref[...].astype(F32) + sgb_ref[...].astype(F32) * r
    y = _bdot(mixed, wo_ref[...])
    xn = _layer_norm(x_ref[...], lng_ref[...], lnb_ref[...])
    x1 = _layer_norm(ALPHA * xn + y, l1g_ref[...], l1b_ref[...])
    x1_ref[...] = x1
    x1p_ref[...] = _pack_rows(x1)

    neg = jnp.float32(-jnp.inf)
    reps = tm // LANES
    scores = jax.nn.sigmoid(lax.dot_general(wr_ref[...], x1.astype(BF16), _NT,
                                            preferred_element_type=F32))
    biased = (scores + jnp.concatenate([rb_ref[...]] * reps, axis=1)).reshape(
        N_GROUPS, GROUP_SIZE, tm)
    sub = lax.broadcasted_iota(jnp.int32, biased.shape, 1).astype(F32)
    m1 = jnp.max(biased, axis=1, keepdims=True)
    first = jnp.min(jnp.where(biased == m1, sub, float(GROUP_SIZE)), axis=1, keepdims=True)
    m2 = jnp.max(jnp.where(sub == first, neg, biased), axis=1, keepdims=True)
    gs = (m1 + m2).reshape(N_GROUPS, tm)
    grp = lax.broadcasted_iota(jnp.int32, (N_GROUPS, tm), 0)
    ahead = jnp.zeros((N_GROUPS, tm), F32)
    for d in range(1, N_GROUPS):
        other = pltpu.roll(gs, d, 0)
        tie = jnp.where(grp >= d, 1.0, 0.0)
        ahead = ahead + jnp.where(other > gs, 1.0, jnp.where(other == gs, tie, 0.0))
    keep = (ahead < TOPK_GROUPS).reshape(N_GROUPS, 1, tm)
    allowed = jnp.where(keep, biased, neg).reshape(N_EXPERTS, tm)
    row = lax.broadcasted_iota(jnp.int32, (N_EXPERTS, tm), 0).astype(F32)
    sel = jnp.zeros((N_EXPERTS, tm), F32)
    picks = []
    for _ in range(TOP_K):
        m = jnp.max(allowed, axis=0, keepdims=True)
        first = jnp.min(jnp.where(allowed == m, row, float(N_EXPERTS)), axis=0, keepdims=True)
        hit = row == first
        picks.append((first, hit, jnp.sum(jnp.where(hit, scores, 0.0), axis=0, keepdims=True)))
        sel = jnp.where(hit, 1.0, sel)
        allowed = jnp.where(hit, neg, allowed)
    wsum = picks[0][2]
    for pk in picks[1:]:
        wsum = wsum + pk[2]
    selb = sel.astype(BF16)
    carry = carry_ref[...]
    before = (jnp.dot(selb, tri_ref[...], preferred_element_type=F32)
              + jnp.concatenate([carry] * reps, axis=1))
    total = carry + jnp.dot(selb, jnp.ones((tm, LANES), BF16), preferred_element_type=F32)
    carry_ref[...] = total
    cnt_ref[...] = total
    blank = [jnp.zeros((1, tm), F32)] * (SUBLANES - TOP_K)
    route_ref[...] = jnp.concatenate(
        [pk[0] for pk in picks] + blank
        + [jnp.sum(jnp.where(pk[1], before, 0.0), axis=0, keepdims=True) for pk in picks] + blank
        + [pk[2] / wsum * ROUTED_SCALE for pk in picks] + blank, axis=0)


def _mix(x, of, ob, so, sgb, ag, p):
    T = x.shape[0]
    tm = TOKEN_TILE
    tok = lambda w: pl.BlockSpec((tm, w), lambda i: (i, 0))
    row = lax.broadcasted_iota(jnp.int32, (tm, tm), 0)
    col = lax.broadcasted_iota(jnp.int32, (tm, tm), 1)
    tri = (row < col).astype(BF16)
    return pl.pallas_call(
        _mix_kernel,
        grid=(T // tm,),
        in_specs=[
            tok(D_MODEL), _const_spec((1, D_MODEL)), _const_spec((1, D_MODEL)),
            tok(HG_WIDTH), tok(HG_WIDTH), tok(HG_WIDTH), tok(D_MODEL), tok(D_MODEL),
            _const_spec((1, HG_WIDTH)),
            _const_spec((HG_WIDTH, D_MODEL)), _const_spec((D_MODEL, D_MODEL)),
            _const_spec((1, D_MODEL)), _const_spec((1, D_MODEL)),
            _const_spec((N_EXPERTS, D_MODEL)), _const_spec((N_EXPERTS, LANES)),
            _const_spec((tm, tm)),
        ],
        out_specs=(tok(D_MODEL), tok(D_MODEL // 2), pl.BlockSpec((ROUTE_ROWS, tm), lambda i: (0, i)),
                   pl.BlockSpec((N_EXPERTS, LANES), lambda i: (0, 0))),
        out_shape=(jax.ShapeDtypeStruct((T, D_MODEL), F32),
                   jax.ShapeDtypeStruct((T, D_MODEL // 2), jnp.uint32),
                   jax.ShapeDtypeStruct((ROUTE_ROWS, T), F32),
                   jax.ShapeDtypeStruct((N_EXPERTS, LANES), F32)),
        scratch_shapes=[pltpu.VMEM((N_EXPERTS, LANES), F32)],
        compiler_params=pltpu.CompilerParams(
            dimension_semantics=("arbitrary",), vmem_limit_bytes=VMEM_MIB["mix"] << 20),
        name="mix",
    )(x, p["ln_in_g"], p["ln_in_b"], of, ob, so, sgb, ag, p["hg_norm_g"], p["w_pb"], p["w_o"],
      p["ln1_g"], p["ln1_b"], p["w_router"], p["router_bias"], tri)


def _sc_mesh():
    return plsc.VectorSubcoreMesh(core_axis_name="c", subcore_axis_name="s",
                                  num_cores=SC_CORES, num_subcores=SC_SUBCORES)


def _sc_worker():
    return lax.axis_index("s") * SC_CORES + lax.axis_index("c")


def _sc_dispatch(x1p, dest, n_rows):
    T, w = x1p.shape
    W = SC_WINDOW
    per_worker = T // W // (SC_CORES * SC_SUBCORES)

    def body(x_hbm, d_hbm, o_hbm, rows_v, idx_v, sem):
        first = _sc_worker() * per_worker
        for k in range(TOP_K):
            pltpu.sync_copy(d_hbm.at[k, pl.ds(first, per_worker)], idx_v.at[k])

        @pl.loop(0, per_worker)
        def _(j):
            pltpu.sync_copy(x_hbm.at[pl.ds((first + j) * W, W)], rows_v)
            copies = [pltpu.async_copy(rows_v, o_hbm.at[idx_v.at[k, j]], sem) for k in range(TOP_K)]
            for c in copies:
                c.wait()

    return pl.kernel(
        body,
        out_type=jax.ShapeDtypeStruct((n_rows, w), jnp.uint32),
        mesh=_sc_mesh(),
        scratch_types=[pltpu.VMEM((W, w), jnp.uint32), pltpu.VMEM((TOP_K, per_worker, W), jnp.int32),
                       pltpu.SemaphoreType.DMA],
        name="sc_dispatch",
    )(x1p, dest.reshape(TOP_K, T // W, W))


def _sc_combine(ys, dest, gates):
    T = dest.shape[1]
    W = SC_COMBINE_WINDOW
    w = ys.shape[1]
    per_worker = T // W // (SC_CORES * SC_SUBCORES)
    assert per_worker % 2 == 0 and per_worker * W * SC_CORES * SC_SUBCORES == T
    assert 2 * W == SC_LANES

    def body(y_hbm, d_hbm, g_hbm, o_hbm, idx_v, gate_v, *scratch):
        slots = [dict(rows=scratch[s], out=scratch[2 + s], gsem=scratch[4 + s], wsem=scratch[6 + s],
                      lane0=s * W) for s in range(2)]
        first = _sc_worker() * per_worker
        for k in range(TOP_K):
            pltpu.sync_copy(d_hbm.at[k, pl.ds(first * W, per_worker * W)], idx_v.at[k])
            pltpu.sync_copy(g_hbm.at[k, pl.ds(first * W, per_worker * W)], gate_v.at[k])

        def loads(b, win):
            i = win - first
            return [pltpu.make_async_copy(y_hbm.at[idx_v.at[k, pl.ds(i * W, W)]],
                                          b["rows"].at[k], b["gsem"]) for k in range(TOP_K)]

        def write_back(b, win):
            return pltpu.make_async_copy(b["out"], o_hbm.at[pl.ds(win * W, W)], b["wsem"])

        def fetch(b, win):
            for c in loads(b, win):
                c.start()

        def reduce_rows(b, pair_gates):
            rows, out = b["rows"], b["out"]
            lane = lax.iota(jnp.int32, SC_LANES)

            @pl.loop(0, W)
            def _(j):
                mine = lane == b["lane0"] + j
                gate = [jnp.broadcast_to(jnp.sum(jnp.where(mine, g, 0.0)), (SC_LANES,))
                        for g in pair_gates]

                @plsc.parallel_loop(0, w, step=SC_LANES, unroll=4)
                def _(col):
                    lo, hi = [], []
                    for k in range(TOP_K):
                        p = rows[k, j, pl.ds(col, SC_LANES)]
                        lo.append(plsc.bitcast(p << 16, F32) * gate[k])
                        hi.append(plsc.bitcast(p & jnp.uint32(0xFFFF0000), F32) * gate[k])
                    out[j, pl.ds(col, SC_LANES)] = (lo[0] + lo[1]) + (lo[2] + lo[3]) + (lo[4] + lo[5])
                    out[j, pl.ds(w + col, SC_LANES)] = (hi[0] + hi[1]) + (hi[2] + hi[3]) + (hi[4] + hi[5])

        def process(b, win, pair_gates, not_first):
            for c in loads(b, win):
                c.wait()

            @pl.when(not_first)
            def _():
                write_back(b, win).wait()

            reduce_rows(b, pair_gates)
            write_back(b, win).start()

        fetch(slots[0], first)

        @pl.loop(0, per_worker, step=2)
        def _(i):
            win = first + i
            pair_gates = [gate_v[k, pl.ds(i * W, SC_LANES)] for k in range(TOP_K)]
            fetch(slots[1], win + 1)
            process(slots[0], win, pair_gates, i > 0)

            @pl.when(i + 2 < per_worker)
            def _():
                fetch(slots[0], win + 2)

            process(slots[1], win + 1, pair_gates, i > 0)

        for b in slots:
            write_back(b, first).wait()

    slot_types = ([pltpu.VMEM((TOP_K, per_worker * W), jnp.int32),
                   pltpu.VMEM((TOP_K, per_worker * W), F32)]
                  + [pltpu.VMEM((TOP_K, W, w), jnp.uint32)] * 2
                  + [pltpu.VMEM((W, 2 * w), F32)] * 2
                  + [pltpu.SemaphoreType.DMA] * 4)
    return pl.kernel(
        body,
        out_type=jax.ShapeDtypeStruct((T, 2 * w), F32),
        mesh=_sc_mesh(),
        scratch_types=slot_types,
        compiler_params=pltpu.CompilerParams(needs_layout_passes=False),
        name="sc_combine",
    )(ys, dest, gates)


def _experts_kernel(be_ref, nv_ref, io_ref, xs_ref, wg_ref, wu_ref, wd_ref, ys_ref, *cast_refs):
    del io_ref
    i = pl.program_id(0)
    n_valid = nv_ref[i]
    mb, w = xs_ref.shape
    if cast_refs:
        @pl.when((i == 0) | (be_ref[i] != be_ref[jnp.maximum(i, 1) - 1]))
        def _():
            for src, dst in zip((wg_ref, wu_ref, wd_ref), cast_refs):
                dst[...] = src[...].astype(BF16)
        wg_ref, wu_ref, wd_ref = cast_refs

    @pl.when(n_valid > 0)
    def _():
        keep = lax.broadcasted_iota(jnp.int32, (mb, w), 0) < n_valid
        lo, hi = _unpack_rows(jnp.where(keep, xs_ref[...], jnp.uint32(0)))
        lo, hi = lo.astype(BF16), hi.astype(BF16)

        def proj(w_ref):
            return (jnp.dot(lo, w_ref[0, :w, :], preferred_element_type=F32)
                    + jnp.dot(hi, w_ref[0, w:, :], preferred_element_type=F32))

        hb = jax.nn.silu(proj(wg_ref)) * proj(wu_ref)
        ys_ref[...] = _pack_rows(jnp.dot(hb.astype(BF16), wd_ref[0], preferred_element_type=F32))


def _experts(xs, blk_exp, n_valid, blk_io, weights):
    n_rows, w = xs.shape
    mb = n_rows // blk_exp.shape[0]
    cast = weights[0].dtype != BF16
    by_expert = lambda shape: pl.BlockSpec((1,) + shape, lambda i, be, nv, io: (be[i], 0, 0))
    w_specs = [by_expert((D_MODEL, EXPERT_DIM)), by_expert((D_MODEL, EXPERT_DIM)),
               by_expert((EXPERT_DIM, D_MODEL))]
    ys_spec = pl.BlockSpec((mb, w), lambda i, be, nv, io: (io[i], 0))
    ys_shape = jax.ShapeDtypeStruct((n_rows, w), jnp.uint32)
    grid_spec = pltpu.PrefetchScalarGridSpec(
        num_scalar_prefetch=3,
        grid=(n_rows // mb,),
        in_specs=[pl.BlockSpec((mb, w), lambda i, be, nv, io: (io[i], 0))] + w_specs,
        out_specs=(ys_spec, *w_specs) if cast else ys_spec,
    )
    out = pl.pallas_call(
        _experts_kernel,
        grid_spec=grid_spec,
        out_shape=(ys_shape, *[jax.ShapeDtypeStruct(x.shape, BF16) for x in weights]) if cast else ys_shape,
        compiler_params=pltpu.CompilerParams(
            dimension_semantics=("arbitrary",), vmem_limit_bytes=VMEM_MIB["experts"] << 20),
        name="experts",
    )(blk_exp, n_valid, blk_io, xs, *weights)
    return (out[0], tuple(out[1:])) if cast else (out, weights)


def _final_kernel(x1_ref, routed_ref, wsg_ref, wsu_ref, wsd_ref, l2g_ref, l2b_ref, out_ref):
    x1 = x1_ref[...]
    xb = x1.astype(BF16)
    hs = (jax.nn.silu(jnp.dot(xb, wsg_ref[...], preferred_element_type=F32))
          * jnp.dot(xb, wsu_ref[...], preferred_element_type=F32))
    shared = jnp.dot(hs.astype(BF16), wsd_ref[...], preferred_element_type=F32)
    out_ref[...] = _layer_norm(ALPHA * x1 + (routed_ref[...] + shared), l2g_ref[...], l2b_ref[...])


def _final(x1, routed, p):
    T = x1.shape[0]
    tm = FINAL_TILE
    tok = lambda w: pl.BlockSpec((tm, w), lambda i: (i, 0))
    return pl.pallas_call(
        _final_kernel,
        grid=(T // tm,),
        in_specs=[
            tok(D_MODEL), tok(D_MODEL),
            _const_spec((D_MODEL, SHARED_DIM)), _const_spec((D_MODEL, SHARED_DIM)),
            _const_spec((SHARED_DIM, D_MODEL)),
            _const_spec((1, D_MODEL)), _const_spec((1, D_MODEL)),
        ],
        out_specs=tok(D_MODEL),
        out_shape=jax.ShapeDtypeStruct((T, D_MODEL), F32),
        compiler_params=pltpu.CompilerParams(
            dimension_semantics=("parallel",), vmem_limit_bytes=VMEM_MIB["final"] << 20),
        name="final",
    )(x1, routed, p["w_sh_gate"], p["w_sh_up"], p["w_sh_down"], p["ln2_g"], p["ln2_b"])


def _routing_layout(route, counts, n_tokens):
    mb = EXPERT_BLOCK if n_tokens * TOP_K >= 2 * EXPERT_BLOCK * N_EXPERTS else EXPERT_BLOCK // 2
    n_blocks = -(-n_tokens * TOP_K // mb) + N_EXPERTS
    idx = route[0:TOP_K].astype(jnp.int32)
    rank = route[SUBLANES:SUBLANES + TOP_K].astype(jnp.int32)
    counts = counts.astype(jnp.int32)
    padded = jnp.maximum((counts + mb - 1) // mb, 1) * mb
    pad_end = jnp.cumsum(padded)
    pad_start = pad_end - padded
    experts = jnp.arange(N_EXPERTS, dtype=jnp.int32)
    dest = rank + jnp.sum(jnp.where(idx[:, :, None] == experts, pad_start, 0), axis=-1)
    blk_start = jnp.arange(n_blocks, dtype=jnp.int32) * mb
    blk_exp = jnp.minimum(
        jnp.sum((pad_end[None, :] <= blk_start[:, None]).astype(jnp.int32), axis=1), N_EXPERTS - 1)
    valid_end = jnp.sum(jnp.where(blk_exp[:, None] == experts, pad_start + counts, 0), axis=-1)
    n_valid = jnp.clip(valid_end - blk_start, 0, mb).astype(jnp.int32)
    blk_io = jnp.minimum(jnp.arange(n_blocks, dtype=jnp.int32), pad_end[-1] // mb - 1)
    gates = route[2 * SUBLANES:2 * SUBLANES + TOP_K]
    return dest, gates, blk_exp, n_valid, blk_io, n_blocks * mb


def _encode(x, p, expert_weights):
    batch, seq, _ = x.shape
    T = batch * seq
    xt = x.reshape(T, D_MODEL)
    ag, q, gf, gb, iv, so, sgb, gmin = _inproj(xt, p)
    of, ob = _hgrn(q, gf, gb, iv, gmin, batch)
    x1, x1p, route, cnt = _mix(xt, of, ob, so, sgb, ag, p)
    dest, gates, blk_exp, n_valid, blk_io, n_rows = _routing_layout(route, cnt[:, 0], T)
    xs = _sc_dispatch(x1p, dest, n_rows)
    ys, expert_weights = _experts(xs, blk_exp, n_valid, blk_io, expert_weights)
    out = _final(x1, _sc_combine(ys, dest, gates), p)
    return out.reshape(batch, seq, D_MODEL), expert_weights


def _prepare_params(ln_in_g, ln_in_b, w_in, a_ln_g, a_ln_b, a_ws, a_sb, hg_lb_logits, hg_norm_g,
                    w_pa, w_pb, w_o, ln1_g, ln1_b, w_router, router_bias, w_sh_gate, w_sh_up,
                    w_sh_down, ln2_g, ln2_b):
    l = 0
    row = lambda v: v.reshape(1, -1).astype(F32)
    ws = a_ws[l].astype(BF16)
    wsp = jnp.concatenate([ws[0::2], ws[1::2]], axis=2)
    sbf = jnp.repeat(a_sb[l].astype(F32), A_WIDTH // A_GROUPS, axis=1)
    lb = jnp.cumsum(jax.nn.softmax(hg_lb_logits.astype(F32), axis=1), axis=1)[:, l]
    return dict(
        ln_in_g=row(ln_in_g), ln_in_b=row(ln_in_b), w_in=w_in[l].astype(BF16),
        a_ln_g=row(a_ln_g[l]), a_ln_b=row(a_ln_b[l]), wsp=wsp, sbf=sbf, lb=lb,
        w_pa=w_pa[l].astype(BF16), hg_norm_g=row(hg_norm_g[l]),
        w_pb=w_pb[l].astype(BF16), w_o=w_o[l].astype(BF16),
        ln1_g=row(ln1_g[l]), ln1_b=row(ln1_b[l]),
        w_router=w_router[l].T.astype(BF16),
        router_bias=jnp.broadcast_to(router_bias[l].astype(F32)[:, None], (N_EXPERTS, LANES)),
        w_sh_gate=w_sh_gate[l].astype(BF16), w_sh_up=w_sh_up[l].astype(BF16),
        w_sh_down=w_sh_down[l].astype(BF16),
        ln2_g=row(ln2_g[l]), ln2_b=row(ln2_b[l]),
    )


def kernel(x_prompt, x_sample, ln_in_g, ln_in_b, w_in, a_ln_g, a_ln_b, a_ws, a_sb, hg_lb_logits,
           hg_norm_g, w_pa, w_pb, w_o, ln1_g, ln1_b, w_router, router_bias, w_e_gate, w_e_up,
           w_e_down, w_sh_gate, w_sh_up, w_sh_down, ln2_g, ln2_b):
    p = _prepare_params(ln_in_g, ln_in_b, w_in, a_ln_g, a_ln_b, a_ws, a_sb, hg_lb_logits, hg_norm_g,
                        w_pa, w_pb, w_o, ln1_g, ln1_b, w_router, router_bias, w_sh_gate, w_sh_up,
                        w_sh_down, ln2_g, ln2_b)
    y_prompt, expert_weights = _encode(x_prompt, p, (w_e_gate[0], w_e_up[0], w_e_down[0]))
    y_sample, _ = _encode(x_sample, p, expert_weights)
    return y_prompt, y_sample
```

```python
import functools

import jax
import jax.numpy as jnp
from jax import lax
from jax.experimental import pallas as pl
from jax.experimental.pallas import tpu as pltpu
from jax.experimental.pallas import tpu_sc as plsc

F32 = jnp.float32
BF16 = jnp.bfloat16

D_MODEL = 1024
A_GROUPS = 8
A_WIDTH = 512
A_CHUNK = 128
HG_HEADS = 8
HG_DK = 128
HG_WIDTH = HG_HEADS * HG_DK
N_IN = 2 * A_WIDTH + 5 * HG_WIDTH + 2 * D_MODEL
N_EXPERTS = 64
TOP_K = 6
N_GROUPS = 8
TOPK_GROUPS = 4
GROUP_SIZE = N_EXPERTS // N_GROUPS
EXPERT_DIM = 256
SHARED_DIM = 256
ROUTED_SCALE = 2.5
DEPTH = 1
ALPHA = (2.0 * DEPTH) ** 0.25
LN_EPS = 1e-5
RMS_EPS = 1e-6

LANES = 128
SUBLANES = 8
ROUTE_ROWS = 3 * SUBLANES
TOKEN_TILE = 512
FINAL_TILE = 1024
INPROJ_TILE = 256
HG_CHUNK = 128
HG_CHUNKS_PER_STEP = 2
HG_SAFE_LOGDECAY = 80.0
EXPERT_BLOCK = 1024
SC_CORES = 2
SC_SUBCORES = 16
SC_LANES = 16
SC_WINDOW = 32
SC_COMBINE_WINDOW = 8
VMEM_MIB = dict(inproj=48, hgrn=32, mix=40, experts=32, final=40)

_O_U, _O_V, _O_Q, _O_FF, _O_FB, _O_I, _O_G, _O_GA, _O_GB = (
    0, 512, 1024, 2048, 3072, 4096, 5120, 6144, 7168)


def _layer_norm(x, g, b):
    mu = jnp.mean(x, axis=-1, keepdims=True)
    xc = x - mu
    var = jnp.mean(xc * xc, axis=-1, keepdims=True)
    return xc * lax.rsqrt(var + LN_EPS) * g + b


def _bdot(a, b):
    return jnp.dot(a.astype(BF16), b.astype(BF16), preferred_element_type=F32)


def _pack_rows(x):
    w = x.shape[1] // 2
    lo = lax.bitcast_convert_type(x[:, :w].astype(BF16).astype(F32), jnp.uint32)
    hi = lax.bitcast_convert_type(x[:, w:].astype(BF16).astype(F32), jnp.uint32)
    return hi | (lo >> 16)


def _unpack_rows(p):
    lo = lax.bitcast_convert_type(p << 16, F32)
    hi = lax.bitcast_convert_type(p & jnp.uint32(0xFFFF0000), F32)
    return lo, hi


def _const_spec(shape):
    nd = len(shape)
    return pl.BlockSpec(shape, lambda *_: (0,) * nd, pipeline_mode=pl.Buffered(1))


def _inproj_kernel(x_ref, lng_ref, lnb_ref, win_ref, alng_ref, alnb_ref, wsp_ref, sbf_ref, lb_ref,
                   wpa_ref, *rest):
    if len(rest) > 8:
        steps_per_expert = pl.num_programs(0) // N_EXPERTS

        @pl.when(pl.program_id(0) % steps_per_expert == 0)
        def _():
            for src, dst in zip(rest[:3], rest[11:]):
                dst[...] = src[...].astype(BF16)
        rest = rest[3:11]
    ag_ref, q_ref, gf_ref, gb_ref, iv_ref, so_ref, sgb_ref, gmin_ref = rest
    tm = x_ref.shape[0]
    xb = _layer_norm(x_ref[...], lng_ref[...], lnb_ref[...]).astype(BF16)

    def sec(lo, width):
        return jnp.dot(xb, win_ref[:, lo:lo + width], preferred_element_type=F32)

    v = _layer_norm(jax.nn.gelu(sec(_O_V, A_WIDTH)), alng_ref[...], alnb_ref[...]).astype(BF16)
    u = jax.nn.gelu(sec(_O_U, A_WIDTH))

    q_ref[...] = jax.nn.silu(sec(_O_Q, HG_WIDTH)).astype(BF16)
    mins = []
    for d, (off, g_ref) in enumerate(((_O_FF, gf_ref), (_O_FB, gb_ref))):
        lb = lb_ref[d:d + 1, :]
        f = lb + (1.0 - lb) * jax.nn.sigmoid(sec(off, HG_WIDTH))
        g = jnp.log(f)
        g_ref[...] = g
        half = jnp.sum(g.reshape(tm // (HG_CHUNK // 2), HG_CHUNK // 2, HG_WIDTH), axis=1)
        mins.append(jnp.min(half, axis=-1, keepdims=True))
    gmin_ref[0] = jnp.broadcast_to(jnp.concatenate(mins, axis=0), gmin_ref.shape[1:])
    so_ref[...] = jax.nn.silu(sec(_O_G, HG_WIDTH)).astype(BF16)
    sgb_ref[...] = jax.nn.sigmoid(sec(_O_GB, D_MODEL)).astype(BF16)

    lane = lax.broadcasted_iota(jnp.int32, (A_CHUNK, LANES), 1)
    left = lane < (A_WIDTH // A_GROUPS)
    zero = jnp.zeros((A_CHUNK, LANES), BF16)
    chunks = []
    for c in range(tm // A_CHUNK):
        vc = v[c * A_CHUNK:(c + 1) * A_CHUNK]
        cols = []
        for p in range(A_GROUPS // 2):
            vp = vc[:, p * LANES:(p + 1) * LANES]
            rhs = jnp.concatenate([jnp.where(left, vp, zero), jnp.where(left, zero, vp)], axis=0)
            cols.append(jnp.dot(wsp_ref[p], rhs, preferred_element_type=F32))
        chunks.append(jnp.concatenate(cols, axis=1) + sbf_ref[...])
    mixed = jnp.concatenate(chunks, axis=0)
    a = _bdot(u * mixed, wpa_ref[...])
    ag_ref[...] = (jax.nn.sigmoid(sec(_O_GA, D_MODEL)) * a).astype(BF16)
    iv_ref[...] = sec(_O_I, HG_WIDTH).astype(BF16)


def _inproj(x, p, expert_weights=None):
    T = x.shape[0]
    tm = INPROJ_TILE
    nt = T // tm
    nh = 2 * tm // (HG_CHUNK // 2)
    tok = lambda w: pl.BlockSpec((tm, w), lambda i: (i, 0))
    outs = (
        jax.ShapeDtypeStruct((T, D_MODEL), BF16),
        jax.ShapeDtypeStruct((T, HG_WIDTH), BF16),
        jax.ShapeDtypeStruct((T, HG_WIDTH), F32),
        jax.ShapeDtypeStruct((T, HG_WIDTH), F32),
        jax.ShapeDtypeStruct((T, HG_WIDTH), BF16),
        jax.ShapeDtypeStruct((T, HG_WIDTH), BF16),
        jax.ShapeDtypeStruct((T, D_MODEL), BF16),
        jax.ShapeDtypeStruct((nt, nh, LANES), F32),
    )
    in_specs = [
        tok(D_MODEL),
        _const_spec((1, D_MODEL)), _const_spec((1, D_MODEL)),
        _const_spec((D_MODEL, N_IN)),
        _const_spec((1, A_WIDTH)), _const_spec((1, A_WIDTH)),
        _const_spec((A_GROUPS // 2, A_CHUNK, 2 * A_CHUNK)),
        _const_spec((A_CHUNK, A_WIDTH)),
        _const_spec((2, HG_WIDTH)),
        _const_spec((A_WIDTH, D_MODEL)),
    ]
    out_specs = [tok(D_MODEL), tok(HG_WIDTH), tok(HG_WIDTH), tok(HG_WIDTH), tok(HG_WIDTH),
                 tok(HG_WIDTH), tok(D_MODEL), pl.BlockSpec((1, nh, LANES), lambda i: (i, 0, 0))]
    args = [x, p["ln_in_g"], p["ln_in_b"], p["w_in"], p["a_ln_g"], p["a_ln_b"], p["wsp"], p["sbf"],
            p["lb"], p["w_pa"]]
    if expert_weights is not None:
        assert nt % N_EXPERTS == 0
        spe = nt // N_EXPERTS
        w_specs = [pl.BlockSpec((1,) + w.shape[1:], lambda i: (i // spe, 0, 0)) for w in expert_weights]
        in_specs += w_specs
        out_specs += w_specs
        outs += tuple(jax.ShapeDtypeStruct(w.shape, BF16) for w in expert_weights)
        args += list(expert_weights)
    res = pl.pallas_call(
        _inproj_kernel,
        grid=(nt,),
        in_specs=in_specs,
        out_specs=tuple(out_specs),
        out_shape=outs,
        compiler_params=pltpu.CompilerParams(
            dimension_semantics=("arbitrary",), vmem_limit_bytes=VMEM_MIB["inproj"] << 20),
        name="inproj",
    )(*args)
    return res[:8], tuple(res[8:])


_NT = (((1,), (1,)), ((), ()))
_TN = (((0,), (0,)), ((), ()))


def _lockstep(*phased):
    phased = list(phased)
    while phased:
        phased = [g for g in phased if next(g, None) is not None]


def _hgrn_direction(q_ref, g_ref, v_ref, st_ref, o_ref, tri, mask, fwd):
    C = q_ref.shape[0]
    g = g_ref[...]
    ghi = g.astype(BF16)
    glo = (g - ghi.astype(F32)).astype(BF16)
    b = jnp.dot(jnp.concatenate([tri, tri], axis=1), jnp.concatenate([ghi, glo], axis=0),
                preferred_element_type=F32)
    mid = C // 2 - 1 if fwd else C // 2
    end = C - 1 if fwd else 0
    r = b[mid:mid + 1, :]
    b_end = b[end:end + 1, :]
    qt = q_ref[...].astype(F32) * jnp.exp(b - r)
    kt = (1.0 - jnp.exp(g)) * jnp.exp(r - b)
    qtb = qt.astype(BF16)
    ktb = kt.astype(BF16)
    qhb = (qt * jnp.exp(r)).astype(BF16)
    khb = (kt * jnp.exp(b_end - r)).astype(BF16)
    decay = jnp.exp(b_end)
    v = v_ref[...]
    heads =[slice(h * HG_DK, (h + 1) * HG_DK) for h in range(HG_HEADS)]
    zero = jnp.zeros((C, HG_DK), BF16)
    mask2 = jnp.concatenate([mask, mask], axis=1)
    yield True
    scores = []
    for p in range(HG_HEADS // 2):
        k1, k2 = ktb[:, heads[2 * p]], ktb[:, heads[2 * p + 1]]
        kk = jnp.concatenate([jnp.concatenate([k1, zero], axis=1),
                              jnp.concatenate([zero, k2], axis=1)], axis=0)
        s2 = lax.dot_general(qtb[:, 2 * p * HG_DK:(2 * p + 2) * HG_DK], kk, _NT,
                             preferred_element_type=F32)
        s2 = jnp.where(mask2, s2, 0.0).astype(BF16)
        scores += [s2[:, :C], s2[:, C:]]
    yield True
    for h, sl in enumerate(heads):
        st = st_ref[h]
        vt = v[:, sl].T
        o_ref[:, sl] = lax.dot_general(
            jnp.concatenate([scores[h], qhb[:, sl]], axis=1),
            jnp.concatenate([vt, st.astype(BF16)], axis=1), _NT,
            preferred_element_type=F32).astype(o_ref.dtype)
        st_ref[h] = st * decay[:, sl] + jnp.dot(vt, khb[:, sl], preferred_element_type=F32)


def _hgrn_direction_stepwise(q_ref, g_ref, v_ref, st_ref, o_ref, q32_ref, v32_ref, o32_ref, fwd):
    C = q_ref.shape[0]
    q32_ref[...] = q_ref[...].astype(F32)
    v32_ref[...] = v_ref[...].astype(F32)
    sub = SUBLANES
    rows = lax.broadcasted_iota(jnp.int32, (sub, HG_DK), 0)

    def group(i, carry):
        base = pl.multiple_of((i if fwd else C // sub - 1 - i) * sub, sub)
        f = jnp.exp(g_ref[pl.ds(base, sub), :])
        k = 1.0 - f
        q = q32_ref[pl.ds(base, sub), :]
        v = v32_ref[pl.ds(base, sub), :]
        for h in range(HG_HEADS):
            sl = slice(h * HG_DK, (h + 1) * HG_DK)
            st = st_ref[h]
            out = jnp.zeros((sub, HG_DK), F32)
            for r in (range(sub) if fwd else range(sub - 1, -1, -1)):
                v_t = jnp.where(rows == 0, v[r:r + 1, sl], 0.0).astype(BF16)
                k_t = jnp.broadcast_to(k[r:r + 1, sl], (sub, HG_DK)).astype(BF16)
                st = st * f[r:r + 1, sl] + lax.dot_general(v_t, k_t, _TN, preferred_element_type=F32)
                q_t = jnp.broadcast_to(q[r:r + 1, sl], (sub, HG_DK)).astype(BF16)
                o_t = lax.dot_general(q_t, st.astype(BF16), _NT, preferred_element_type=F32)
                out = jnp.where(rows == r, o_t, out)
            st_ref[h] = st
            o32_ref[pl.ds(base, sub), sl] = out
        return carry

    lax.fori_loop(0, C // sub, group, 0)
    o_ref[...] = o32_ref[...].astype(o_ref.dtype)


def _hgrn_kernel(safe_ref, qf_ref, qb_ref, gf_ref, gb_ref, vf_ref, vb_ref, tril_ref, triu_ref,
                 of_ref, ob_ref, sf_ref, sb_ref, q32_ref, v32_ref, o32_ref):
    b, j = pl.program_id(0), pl.program_id(1)
    ns = pl.num_programs(1)
    n = HG_CHUNKS_PER_STEP
    C = qf_ref.shape[0] // n

    @pl.when(j == 0)
    def _():
        sf_ref[...] = jnp.zeros_like(sf_ref)
        sb_ref[...] = jnp.zeros_like(sb_ref)

    row = lax.broadcasted_iota(jnp.int32, (C, C), 0)
    col = lax.broadcasted_iota(jnp.int32, (C, C), 1)
    safe_f = [safe_ref[0, (b * ns + j) * n + u] != 0 for u in range(n)]
    safe_b = [safe_ref[1, (b * ns + ns - 1 - j) * n + u] != 0 for u in range(n)]
    part = lambda ref, u: ref.at[pl.ds(u * C, C), :]

    def forward(u, stepwise):
        refs = (part(qf_ref, u), part(gf_ref, u), part(vf_ref, u), sf_ref, part(of_ref, u))
        if stepwise:
            _hgrn_direction_stepwise(*refs, q32_ref, v32_ref, o32_ref, True)
            return iter(())
        return _hgrn_direction(*refs, tril_ref[...], row >= col, True)

    def backward(u, stepwise):
        refs = (part(qb_ref, u), part(gb_ref, u), part(vb_ref, u), sb_ref, part(ob_ref, u))
        if stepwise:
            _hgrn_direction_stepwise(*refs, q32_ref, v32_ref, o32_ref, False)
            return iter(())
        return _hgrn_direction(*refs, triu_ref[...], row <= col, False)

    all_safe = functools.reduce(jnp.logical_and, safe_f + safe_b)

    @pl.when(all_safe)
    def _():
        _lockstep(*[d for u in range(n) for d in (forward(u, False), backward(n - 1 - u, False))])

    @pl.when(jnp.logical_not(all_safe))
    def _():
        run = lambda direction, u, stepwise: lambda: _lockstep(direction(u, stepwise))
        for u in range(n):
            pl.when(safe_f[u])(run(forward, u, False))
            pl.when(jnp.logical_not(safe_f[u]))(run(forward, u, True))
        for u in reversed(range(n)):
            pl.when(safe_b[u])(run(backward, u, False))
            pl.when(jnp.logical_not(safe_b[u]))(run(backward, u, True))


def _hgrn(q, gf, gb, iv, gmin, batch):
    T = q.shape[0]
    C = HG_CHUNK
    nc = T // batch // C
    nt, nh, _ = gmin.shape
    halves = gmin[:, :, 0].reshape(nt, 2, nh // 4, 2)
    safe = (jnp.min(halves, axis=-1) > -HG_SAFE_LOGDECAY).astype(jnp.int32)
    safe = safe.transpose(1, 0, 2).reshape(2, T // C)
    n = HG_CHUNKS_PER_STEP
    ns = nc // n
    fwd = pl.BlockSpec((n * C, HG_WIDTH), lambda b, j, s: (b * ns + j, 0))
    bwd = pl.BlockSpec((n * C, HG_WIDTH), lambda b, j, s: (b * ns + ns - 1 - j, 0))
    const = lambda shape: pl.BlockSpec(shape, lambda b, j, s: (0,) * len(shape),
                                       pipeline_mode=pl.Buffered(1))
    row = lax.broadcasted_iota(jnp.int32, (C, C), 0)
    col = lax.broadcasted_iota(jnp.int32, (C, C), 1)
    tril = (row >= col).astype(BF16)
    triu = (row <= col).astype(BF16)
    grid_spec = pltpu.PrefetchScalarGridSpec(
        num_scalar_prefetch=1,
        grid=(batch, ns),
        in_specs=[fwd, bwd, fwd, bwd, fwd, bwd, const((C, C)), const((C, C))],
        out_specs=(fwd, bwd),
        scratch_shapes=[pltpu.VMEM((HG_HEADS, HG_DK, HG_DK), F32),
                        pltpu.VMEM((HG_HEADS, HG_DK, HG_DK), F32),
                        pltpu.VMEM((C, HG_WIDTH), F32), pltpu.VMEM((C, HG_WIDTH), F32),
                        pltpu.VMEM((C, HG_WIDTH), F32)],
    )
    return pl.pallas_call(
        _hgrn_kernel,
        grid_spec=grid_spec,
        out_shape=(jax.ShapeDtypeStruct((T, HG_WIDTH), BF16), jax.ShapeDtypeStruct((T, HG_WIDTH), BF16)),
        compiler_params=pltpu.CompilerParams(
            dimension_semantics=("parallel", "arbitrary"), vmem_limit_bytes=VMEM_MIB["hgrn"] << 20),
        name="hgrn",
    )(safe, q, q, gf, gb, iv, iv, tril, triu)


def _mix_kernel(x_ref, lng_ref, lnb_ref, of_ref, ob_ref, so_ref, sgb_ref, ag_ref, ng_ref, wpb_ref,
                wo_ref, l1g_ref, l1b_ref, wr_ref, rb_ref, tri_ref, x1_ref, x1p_ref, route_ref,
                cnt_ref, carry_ref):
    tm = x_ref.shape[0]

    @pl.when(pl.program_id(0) == 0)
    def _():
        carry_ref[...] = jnp.zeros_like(carry_ref)

    o = of_ref[...].astype(F32) + ob_ref[...].astype(F32)
    heads = []
    for h in range(HG_HEADS):
        oh = o[:, h * HG_DK:(h + 1) * HG_DK]
        heads.append(oh * lax.rsqrt(jnp.mean(oh * oh, axis=-1, keepdims=True) + RMS_EPS))
    rn = jnp.concatenate(heads, axis=1) * ng_ref[...] * so_ref[...].astype(F32)
    r = _bdot(rn, wpb_ref[...])
    mixed = ag_ref[...].astype(F32) + sgb_ref[...].astype(F32) * r
    y = _bdot(mixed, wo_ref[...])
    xn = _layer_norm(x_ref[...], lng_ref[...], lnb_ref[...])
    x1 = _layer_norm(ALPHA * xn + y, l1g_ref[...], l1b_ref[...])
    x1_ref[...] = x1
    x1p_ref[...] = _pack_rows(x1)

    neg = jnp.float32(-jnp.inf)
    reps = tm // LANES
    scores = jax.nn.sigmoid(lax.dot_general(wr_ref[...], x1.astype(BF16), _NT,
                                            preferred_element_type=F32))
    biased = (scores + jnp.concatenate([rb_ref[...]] * reps, axis=1)).reshape(
        N_GROUPS, GROUP_SIZE, tm)
    sub = lax.broadcasted_iota(jnp.int32, biased.shape, 1).astype(F32)
    m1 = jnp.max(biased, axis=1, keepdims=True)
    first = jnp.min(jnp.where(biased == m1, sub, float(GROUP_SIZE)), axis=1, keepdims=True)
    m2 = jnp.max(jnp.where(sub == first, neg, biased), axis=1, keepdims=True)
    gs = (m1 + m2).reshape(N_GROUPS, tm)
    grp = lax.broadcasted_iota(jnp.int32, (N_GROUPS, tm), 0)
    ahead = jnp.zeros((N_GROUPS, tm), F32)
    for d in range(1, N_GROUPS):
        other = pltpu.roll(gs, d, 0)
        tie = jnp.where(grp >= d, 1.0, 0.0)
        ahead = ahead + jnp.where(other > gs, 1.0, jnp.where(other == gs, tie, 0.0))
    keep = (ahead < TOPK_GROUPS).reshape(N_GROUPS, 1, tm)
    allowed = jnp.where(keep, biased, neg).reshape(N_EXPERTS, tm)
    row = lax.broadcasted_iota(jnp.int32, (N_EXPERTS, tm), 0).astype(F32)
    sel = jnp.zeros((N_EXPERTS, tm), F32)
    picks = []
    for _ in range(TOP_K):
        m = jnp.max(allowed, axis=0, keepdims=True)
        first = jnp.min(jnp.where(allowed == m, row, float(N_EXPERTS)), axis=0, keepdims=True)
        hit = row == first
        picks.append((first, hit, jnp.sum(jnp.where(hit, scores, 0.0), axis=0, keepdims=True)))
        sel = jnp.where(hit, 1.0, sel)
        allowed = jnp.where(hit, neg, allowed)
    wsum = picks[0][2]
    for pk in picks[1:]:
        wsum = wsum + pk[2]
    selb = sel.astype(BF16)
    carry = carry_ref[...]
    before = (jnp.dot(selb, tri_ref[...], preferred_element_type=F32)
              + jnp.concatenate([carry] * reps, axis=1))
    total = carry + jnp.dot(selb, jnp.ones((tm, LANES), BF16), preferred_element_type=F32)
    carry_ref[...] = total
    cnt_ref[...] = total
    blank = [jnp.zeros((1, tm), F32)] * (SUBLANES - TOP_K)
    route_ref[...] = jnp.concatenate(
        [pk[0] for pk in picks] + blank
        + [jnp.sum(jnp.where(pk[1], before, 0.0), axis=0, keepdims=True) for pk in picks] + blank
        + [pk[2] / wsum * ROUTED_SCALE for pk in picks] + blank, axis=0)


def _mix(x, of, ob, so, sgb, ag, p):
    T = x.shape[0]
    tm = TOKEN_TILE
    tok = lambda w: pl.BlockSpec((tm, w), lambda i: (i, 0))
    row = lax.broadcasted_iota(jnp.int32, (tm, tm), 0)
    col = lax.broadcasted_iota(jnp.int32, (tm, tm), 1)
    tri = (row < col).astype(BF16)
    return pl.pallas_call(
        _mix_kernel,
        grid=(T // tm,),
        in_specs=[
            tok(D_MODEL), _const_spec((1, D_MODEL)), _const_spec((1, D_MODEL)),
            tok(HG_WIDTH), tok(HG_WIDTH), tok(HG_WIDTH), tok(D_MODEL), tok(D_MODEL),
            _const_spec((1, HG_WIDTH)),
            _const_spec((HG_WIDTH, D_MODEL)), _const_spec((D_MODEL, D_MODEL)),
            _const_spec((1, D_MODEL)), _const_spec((1, D_MODEL)),
            _const_spec((N_EXPERTS, D_MODEL)), _const_spec((N_EXPERTS, LANES)),
            _const_spec((tm, tm)),
        ],
        out_specs=(tok(D_MODEL), tok(D_MODEL // 2), pl.BlockSpec((ROUTE_ROWS, tm), lambda i: (0, i)),
                   pl.BlockSpec((N_EXPERTS, LANES), lambda i: (0, 0))),
        out_shape=(jax.ShapeDtypeStruct((T, D_MODEL), F32),
                   jax.ShapeDtypeStruct((T, D_MODEL // 2), jnp.uint32),
                   jax.ShapeDtypeStruct((ROUTE_ROWS, T), F32),
                   jax.ShapeDtypeStruct((N_EXPERTS, LANES), F32)),
        scratch_shapes=[pltpu.VMEM((N_EXPERTS, LANES), F32)],
        compiler_params=pltpu.CompilerParams(
            dimension_semantics=("arbitrary",), vmem_limit_bytes=VMEM_MIB["mix"] << 20),
        name="mix",
    )(x, p["ln_in_g"], p["ln_in_b"], of, ob, so, sgb, ag, p["hg_norm_g"], p["w_pb"], p["w_o"],
      p["ln1_g"], p["ln1_b"], p["w_router"], p["router_bias"], tri)


def _sc_mesh():
    return plsc.VectorSubcoreMesh(core_axis_name="c", subcore_axis_name="s",
                                  num_cores=SC_CORES, num_subcores=SC_SUBCORES)


def _sc_worker():
    return lax.axis_index("s") * SC_CORES + lax.axis_index("c")


def _sc_dispatch(x1p, dest, n_rows):
    T, w = x1p.shape
    W = SC_WINDOW
    per_worker = T // W // (SC_CORES * SC_SUBCORES)

    def body(x_hbm, d_hbm, o_hbm, rows_v, idx_v, sem):
        first = _sc_worker() * per_worker
        for k in range(TOP_K):
            pltpu.sync_copy(d_hbm.at[k, pl.ds(first, per_worker)], idx_v.at[k])

        @pl.loop(0, per_worker)
        def _(j):
            pltpu.sync_copy(x_hbm.at[pl.ds((first + j) * W, W)], rows_v)
            copies = [pltpu.async_copy(rows_v, o_hbm.at[idx_v.at[k, j]], sem) for k in range(TOP_K)]
            for c in copies:
                c.wait()

    return pl.kernel(
        body,
        out_type=jax.ShapeDtypeStruct((n_rows, w), jnp.uint32),
        mesh=_sc_mesh(),
        scratch_types=[pltpu.VMEM((W, w), jnp.uint32), pltpu.VMEM((TOP_K, per_worker, W), jnp.int32),
                       pltpu.SemaphoreType.DMA],
        name="sc_dispatch",
    )(x1p, dest.reshape(TOP_K, T // W, W))


def _sc_combine(ys, dest, gates):
    T = dest.shape[1]
    W = SC_COMBINE_WINDOW
    w = ys.shape[1]
    per_worker = T // W // (SC_CORES * SC_SUBCORES)
    assert per_worker % 2 == 0 and per_worker * W * SC_CORES * SC_SUBCORES == T
    assert 2 * W == SC_LANES

    def body(y_hbm, d_hbm, g_hbm, o_hbm, idx_v, gate_v, *scratch):
        slots = [dict(rows=scratch[s], out=scratch[2 + s], gsem=scratch[4 + s], wsem=scratch[6 + s],
                      lane0=s * W) for s in range(2)]
        first = _sc_worker() * per_worker
        for k in range(TOP_K):
            pltpu.sync_copy(d_hbm.at[k, pl.ds(first * W, per_worker * W)], idx_v.at[k])
            pltpu.sync_copy(g_hbm.at[k, pl.ds(first * W, per_worker * W)], gate_v.at[k])

        def loads(b, win):
            i = win - first
            return [pltpu.make_async_copy(y_hbm.at[idx_v.at[k, pl.ds(i * W, W)]],
                                          b["rows"].at[k], b["gsem"]) for k in range(TOP_K)]

        def write_back(b, win):
            return pltpu.make_async_copy(b["out"], o_hbm.at[pl.ds(win * W, W)], b["wsem"])

        def fetch(b, win):
            for c in loads(b, win):
                c.start()

        def reduce_rows(b, pair_gates):
            rows, out = b["rows"], b["out"]
            lane = lax.iota(jnp.int32, SC_LANES)

            @pl.loop(0, W)
            def _(j):
                mine = lane == b["lane0"] + j
                gate = [jnp.broadcast_to(jnp.sum(jnp.where(mine, g, 0.0)), (SC_LANES,))
                        for g in pair_gates]

                @plsc.parallel_loop(0, w, step=SC_LANES, unroll=4)
                def _(col):
                    lo, hi = [], []
                    for k in range(TOP_K):
                        p = rows[k, j, pl.ds(col, SC_LANES)]
                        lo.append(plsc.bitcast(p << 16, F32) * gate[k])
                        hi.append(plsc.bitcast(p & jnp.uint32(0xFFFF0000), F32) * gate[k])
                    out[j, pl.ds(col, SC_LANES)] = (lo[0] + lo[1]) + (lo[2] + lo[3]) + (lo[4] + lo[5])
                    out[j, pl.ds(w + col, SC_LANES)] = (hi[0] + hi[1]) + (hi[2] + hi[3]) + (hi[4] + hi[5])

        def process(b, win, pair_gates, not_first):
            for c in loads(b, win):
                c.wait()

            @pl.when(not_first)
            def _():
                write_back(b, win).wait()

            reduce_rows(b, pair_gates)
            write_back(b, win).start()

        fetch(slots[0], first)

        @pl.loop(0, per_worker, step=2)
        def _(i):
            win = first + i
            pair_gates = [gate_v[k, pl.ds(i * W, SC_LANES)] for k in range(TOP_K)]
            fetch(slots[1], win + 1)
            process(slots[0], win, pair_gates, i > 0)

            @pl.when(i + 2 < per_worker)
            def _():
                fetch(slots[0], win + 2)

            process(slots[1], win + 1, pair_gates, i > 0)

        for b in slots:
            write_back(b, first).wait()

    slot_types = ([pltpu.VMEM((TOP_K, per_worker * W), jnp.int32),
                   pltpu.VMEM((TOP_K, per_worker * W), F32)]
                  + [pltpu.VMEM((TOP_K, W, w), jnp.uint32)] * 2
                  + [pltpu.VMEM((W, 2 * w), F32)] * 2
                  + [pltpu.SemaphoreType.DMA] * 4)
    return pl.kernel(
        body,
        out_type=jax.ShapeDtypeStruct((T, 2 * w), F32),
        mesh=_sc_mesh(),
        scratch_types=slot_types,
        compiler_params=pltpu.CompilerParams(needs_layout_passes=False),
        name="sc_combine",
    )(ys, dest, gates)


def _experts_kernel(be_ref, nv_ref, io_ref, xs_ref, wg_ref, wu_ref, wd_ref, ys_ref):
    del be_ref, io_ref
    n_valid = nv_ref[pl.program_id(0)]
    mb, w = xs_ref.shape

    @pl.when(n_valid > 0)
    def _():
        keep = lax.broadcasted_iota(jnp.int32, (mb, w), 0) < n_valid
        lo, hi = _unpack_rows(jnp.where(keep, xs_ref[...], jnp.uint32(0)))
        lo, hi = lo.astype(BF16), hi.astype(BF16)

        def proj(w_ref):
            return (jnp.dot(lo, w_ref[0, :w, :], preferred_element_type=F32)
                    + jnp.dot(hi, w_ref[0, w:, :], preferred_element_type=F32))

        hb = jax.nn.silu(proj(wg_ref)) * proj(wu_ref)
        ys_ref[...] = _pack_rows(jnp.dot(hb.astype(BF16), wd_ref[0], preferred_element_type=F32))


def _experts(xs, blk_exp, n_valid, blk_io, weights):
    n_rows, w = xs.shape
    mb = n_rows // blk_exp.shape[0]
    by_expert = lambda shape: pl.BlockSpec((1,) + shape, lambda i, be, nv, io: (be[i], 0, 0))
    w_specs = [by_expert((D_MODEL, EXPERT_DIM)), by_expert((D_MODEL, EXPERT_DIM)),
               by_expert((EXPERT_DIM, D_MODEL))]
    ys_spec = pl.BlockSpec((mb, w), lambda i, be, nv, io: (io[i], 0))
    ys_shape = jax.ShapeDtypeStruct((n_rows, w), jnp.uint32)
    grid_spec = pltpu.PrefetchScalarGridSpec(
        num_scalar_prefetch=3,
        grid=(n_rows // mb,),
        in_specs=[pl.BlockSpec((mb, w), lambda i, be, nv, io: (io[i], 0))] + w_specs,
        out_specs=ys_spec,
    )
    return pl.pallas_call(
        _experts_kernel,
        grid_spec=grid_spec,
        out_shape=ys_shape,
        compiler_params=pltpu.CompilerParams(
            dimension_semantics=("arbitrary",), vmem_limit_bytes=VMEM_MIB["experts"] << 20),
        name="experts",
    )(blk_exp, n_valid, blk_io, xs, *weights)


def _final_kernel(x1_ref, routed_ref, wsg_ref, wsu_ref, wsd_ref, l2g_ref, l2b_ref, out_ref):
    x1 = x1_ref[...]
    xb = x1.astype(BF16)
    hs = (jax.nn.silu(jnp.dot(xb, wsg_ref[...], preferred_element_type=F32))
          * jnp.dot(xb, wsu_ref[...], preferred_element_type=F32))
    shared = jnp.dot(hs.astype(BF16), wsd_ref[...], preferred_element_type=F32)
    out_ref[...] = _layer_norm(ALPHA * x1 + (routed_ref[...] + shared), l2g_ref[...], l2b_ref[...])


def _final(x1, routed, p):
    T = x1.shape[0]
    tm = FINAL_TILE
    tok = lambda w: pl.BlockSpec((tm, w), lambda i: (i, 0))
    return pl.pallas_call(
        _final_kernel,
        grid=(T // tm,),
        in_specs=[
            tok(D_MODEL), tok(D_MODEL),
            _const_spec((D_MODEL, SHARED_DIM)), _const_spec((D_MODEL, SHARED_DIM)),
            _const_spec((SHARED_DIM, D_MODEL)),
            _const_spec((1, D_MODEL)), _const_spec((1, D_MODEL)),
        ],
        out_specs=tok(D_MODEL),
        out_shape=jax.ShapeDtypeStruct((T, D_MODEL), F32),
        compiler_params=pltpu.CompilerParams(
            dimension_semantics=("parallel",), vmem_limit_bytes=VMEM_MIB["final"] << 20),
        name="final",
    )(x1, routed, p["w_sh_gate"], p["w_sh_up"], p["w_sh_down"], p["ln2_g"], p["ln2_b"])


def _routing_layout(route, counts, n_tokens):
    mb = EXPERT_BLOCK
    n_blocks = -(-n_tokens * TOP_K // mb) + N_EXPERTS
    idx = route[0:TOP_K].astype(jnp.int32)
    rank = route[SUBLANES:SUBLANES + TOP_K].astype(jnp.int32)
    counts = counts.astype(jnp.int32)
    padded = (counts + mb - 1) // mb * mb
    pad_end = jnp.cumsum(padded)
    pad_start = pad_end - padded
    experts = jnp.arange(N_EXPERTS, dtype=jnp.int32)
    dest = rank + jnp.sum(jnp.where(idx[:, :, None] == experts, pad_start, 0), axis=-1)
    blk_start = jnp.arange(n_blocks, dtype=jnp.int32) * mb
    blk_exp = jnp.minimum(
        jnp.sum((pad_end[None, :] <= blk_start[:, None]).astype(jnp.int32), axis=1), N_EXPERTS - 1)
    valid_end = jnp.sum(jnp.where(blk_exp[:, None] == experts, pad_start + counts, 0), axis=-1)
    n_valid = jnp.clip(valid_end - blk_start, 0, mb).astype(jnp.int32)
    blk_io = jnp.minimum(jnp.arange(n_blocks, dtype=jnp.int32), pad_end[-1] // mb - 1)
    gates = route[2 * SUBLANES:2 * SUBLANES + TOP_K]
    return dest, gates, blk_exp, n_valid, blk_io, n_blocks * mb


def _encode(x, p, expert_weights):
    batch, seq, _ = x.shape
    T = batch * seq
    xt = x.reshape(T, D_MODEL)
    if expert_weights[0].dtype == BF16:
        (ag, q, gf, gb, iv, so, sgb, gmin), _ = _inproj(xt, p)
    else:
        (ag, q, gf, gb, iv, so, sgb, gmin), expert_weights = _inproj(xt, p, expert_weights)
    of, ob = _hgrn(q, gf, gb, iv, gmin, batch)
    x1, x1p, route, cnt = _mix(xt, of, ob, so, sgb, ag, p)
    dest, gates, blk_exp, n_valid, blk_io, n_rows = _routing_layout(route, cnt[:, 0], T)
    xs = _sc_dispatch(x1p, dest, n_rows)
    ys = _experts(xs, blk_exp, n_valid, blk_io, expert_weights)
    out = _final(x1, _sc_combine(ys, dest, gates), p)
    return out.reshape(batch, seq, D_MODEL), expert_weights


def _prepare_params(ln_in_g, ln_in_b, w_in, a_ln_g, a_ln_b, a_ws, a_sb, hg_lb_logits, hg_norm_g,
                    w_pa, w_pb, w_o, ln1_g, ln1_b, w_router, router_bias, w_sh_gate, w_sh_up,
                    w_sh_down, ln2_g, ln2_b):
    l = 0
    row = lambda v: v.reshape(1, -1).astype(F32)
    ws = a_ws[l].astype(BF16)
    wsp = jnp.concatenate([ws[0::2], ws[1::2]], axis=2)
    sbf = jnp.repeat(a_sb[l].astype(F32), A_WIDTH // A_GROUPS, axis=1)
    lb = jnp.cumsum(jax.nn.softmax(hg_lb_logits.astype(F32), axis=1), axis=1)[:, l]
    return dict(
        ln_in_g=row(ln_in_g), ln_in_b=row(ln_in_b), w_in=w_in[l].astype(BF16),
        a_ln_g=row(a_ln_g[l]), a_ln_b=row(a_ln_b[l]), wsp=wsp, sbf=sbf, lb=lb,
        w_pa=w_pa[l].astype(BF16), hg_norm_g=row(hg_norm_g[l]),
        w_pb=w_pb[l].astype(BF16), w_o=w_o[l].astype(BF16),
        ln1_g=row(ln1_g[l]), ln1_b=row(ln1_b[l]),
        w_router=w_router[l].T.astype(BF16),
        router_bias=jnp.broadcast_to(router_bias[l].astype(F32)[:, None], (N_EXPERTS, LANES)),
        w_sh_gate=w_sh_gate[l].astype(BF16), w_sh_up=w_sh_up[l].astype(BF16),
        w_sh_down=w_sh_down[l].astype(BF16),
        ln2_g=row(ln2_g[l]), ln2_b=row(ln2_b[l]),
    )


def kernel(x_prompt, x_sample, ln_in_g, ln_in_b, w_in, a_ln_g, a_ln_b, a_ws, a_sb, hg_lb_logits,
           hg_norm_g, w_pa, w_pb, w_o, ln1_g, ln1_b, w_router, router_bias, w_e_gate, w_e_up,
           w_e_down, w_sh_gate, w_sh_up, w_sh_down, ln2_g, ln2_b):
    p = _prepare_params(ln_in_g, ln_in_b, w_in, a_ln_g, a_ln_b, a_ws, a_sb, hg_lb_logits, hg_norm_g,
                        w_pa, w_pb, w_o, ln1_g, ln1_b, w_router, router_bias, w_sh_gate, w_sh_up,
                        w_sh_down, ln2_g, ln2_b)
    y_prompt, expert_weights = _encode(x_prompt, p, (w_e_gate[0], w_e_up[0], w_e_down[0]))
    y_sample, _ = _encode(x_sample, p, expert_weights)
    return y_prompt, y_sample
```

```python
import functools

import jax
import jax.numpy as jnp
from jax import lax
from jax.experimental import pallas as pl
from jax.experimental.pallas import tpu as pltpu
from jax.experimental.pallas import tpu_sc as plsc

F32 = jnp.float32
BF16 = jnp.bfloat16

D_MODEL = 1024
A_GROUPS = 8
A_WIDTH = 512
A_CHUNK = 128
HG_HEADS = 8
HG_DK = 128
HG_WIDTH = HG_HEADS * HG_DK
N_IN = 2 * A_WIDTH + 5 * HG_WIDTH + 2 * D_MODEL
N_EXPERTS = 64
TOP_K = 6
N_GROUPS = 8
TOPK_GROUPS = 4
GROUP_SIZE = N_EXPERTS // N_GROUPS
EXPERT_DIM = 256
SHARED_DIM = 256
ROUTED_SCALE = 2.5
DEPTH = 1
ALPHA = (2.0 * DEPTH) ** 0.25
LN_EPS = 1e-5
RMS_EPS = 1e-6

LANES = 128
SUBLANES = 8
ROUTE_ROWS = 3 * SUBLANES
TOKEN_TILE = 512
FINAL_TILE = 1024
INPROJ_TILE = 256
HG_CHUNK = 128
HG_CHUNKS_PER_STEP = 2
HG_SAFE_LOGDECAY = 80.0
EXPERT_BLOCK = 1024
SC_CORES = 2
SC_SUBCORES = 16
SC_LANES = 16
SC_WINDOW = 32
SC_COMBINE_WINDOW = 8
VMEM_MIB = dict(inproj=48, hgrn=32, mix=40, experts=32, final=40)

_O_U, _O_V, _O_Q, _O_FF, _O_FB, _O_I, _O_G, _O_GA, _O_GB = (
    0, 512, 1024, 2048, 3072, 4096, 5120, 6144, 7168)


def _layer_norm(x, g, b):
    mu = jnp.mean(x, axis=-1, keepdims=True)
    xc = x - mu
    var = jnp.mean(xc * xc, axis=-1, keepdims=True)
    return xc * lax.rsqrt(var + LN_EPS) * g + b


def _bdot(a, b):
    return jnp.dot(a.astype(BF16), b.astype(BF16), preferred_element_type=F32)


def _pack_rows(x):
    w = x.shape[1] // 2
    lo = lax.bitcast_convert_type(x[:, :w].astype(BF16).astype(F32), jnp.uint32)
    hi = lax.bitcast_convert_type(x[:, w:].astype(BF16).astype(F32), jnp.uint32)
    return hi | (lo >> 16)


def _unpack_rows(p):
    lo = lax.bitcast_convert_type(p << 16, F32)
    hi = lax.bitcast_convert_type(p & jnp.uint32(0xFFFF0000), F32)
    return lo, hi


def _const_spec(shape):
    nd = len(shape)
    return pl.BlockSpec(shape, lambda *_: (0,) * nd, pipeline_mode=pl.Buffered(1))


def _inproj_kernel(x_ref, lng_ref, lnb_ref, win_ref, alng_ref, alnb_ref, wsp_ref, sbf_ref, lb_ref,
                   wpa_ref, *rest):
    if len(rest) > 8:
        for src, dst in zip(rest[:3], rest[11:]):
            dst[...] = src[...].astype(BF16)
        rest = rest[3:11]
    ag_ref, q_ref, gf_ref, gb_ref, iv_ref, so_ref, sgb_ref, gmin_ref = rest
    tm = x_ref.shape[0]
    xb = _layer_norm(x_ref[...], lng_ref[...], lnb_ref[...]).astype(BF16)

    def sec(lo, width):
        return jnp.dot(xb, win_ref[:, lo:lo + width], preferred_element_type=F32)

    v = _layer_norm(jax.nn.gelu(sec(_O_V, A_WIDTH)), alng_ref[...], alnb_ref[...]).astype(BF16)
    u = jax.nn.gelu(sec(_O_U, A_WIDTH))

    q_ref[...] = jax.nn.silu(sec(_O_Q, HG_WIDTH)).astype(BF16)
    mins = []
    for d, (off, g_ref) in enumerate(((_O_FF, gf_ref), (_O_FB, gb_ref))):
        lb = lb_ref[d:d + 1, :]
        f = lb + (1.0 - lb) * jax.nn.sigmoid(sec(off, HG_WIDTH))
        g = jnp.log(f)
        g_ref[...] = g
        half = jnp.sum(g.reshape(tm // (HG_CHUNK // 2), HG_CHUNK // 2, HG_WIDTH), axis=1)
        mins.append(jnp.min(half, axis=-1, keepdims=True))
    gmin_ref[0] = jnp.broadcast_to(jnp.concatenate(mins, axis=0), gmin_ref.shape[1:])
    so_ref[...] = jax.nn.silu(sec(_O_G, HG_WIDTH)).astype(BF16)
    sgb_ref[...] = jax.nn.sigmoid(sec(_O_GB, D_MODEL)).astype(BF16)

    lane = lax.broadcasted_iota(jnp.int32, (A_CHUNK, LANES), 1)
    left = lane < (A_WIDTH // A_GROUPS)
    zero = jnp.zeros((A_CHUNK, LANES), BF16)
    chunks = []
    for c in range(tm // A_CHUNK):
        vc = v[c * A_CHUNK:(c + 1) * A_CHUNK]
        cols = []
        for p in range(A_GROUPS // 2):
            vp = vc[:, p * LANES:(p + 1) * LANES]
            rhs = jnp.concatenate([jnp.where(left, vp, zero), jnp.where(left, zero, vp)], axis=0)
            cols.append(jnp.dot(wsp_ref[p], rhs, preferred_element_type=F32))
        chunks.append(jnp.concatenate(cols, axis=1) + sbf_ref[...])
    mixed = jnp.concatenate(chunks, axis=0)
    a = _bdot(u * mixed, wpa_ref[...])
    ag_ref[...] = (jax.nn.sigmoid(sec(_O_GA, D_MODEL)) * a).astype(BF16)
    iv_ref[...] = sec(_O_I, HG_WIDTH).astype(BF16)


def _inproj(x, p, expert_weights=None):
    T = x.shape[0]
    tm = INPROJ_TILE
    nt = T // tm
    nh = 2 * tm // (HG_CHUNK // 2)
    tok = lambda w: pl.BlockSpec((tm, w), lambda i: (i, 0))
    outs = (
        jax.ShapeDtypeStruct((T, D_MODEL), BF16),
        jax.ShapeDtypeStruct((T, HG_WIDTH), BF16),
        jax.ShapeDtypeStruct((T, HG_WIDTH), F32),
        jax.ShapeDtypeStruct((T, HG_WIDTH), F32),
        jax.ShapeDtypeStruct((T, HG_WIDTH), BF16),
        jax.ShapeDtypeStruct((T, HG_WIDTH), BF16),
        jax.ShapeDtypeStruct((T, D_MODEL), BF16),
        jax.ShapeDtypeStruct((nt, nh, LANES), F32),
    )
    in_specs = [
        tok(D_MODEL),
        _const_spec((1, D_MODEL)), _const_spec((1, D_MODEL)),
        _const_spec((D_MODEL, N_IN)),
        _const_spec((1, A_WIDTH)), _const_spec((1, A_WIDTH)),
        _const_spec((A_GROUPS // 2, A_CHUNK, 2 * A_CHUNK)),
        _const_spec((A_CHUNK, A_WIDTH)),
        _const_spec((2, HG_WIDTH)),
        _const_spec((A_WIDTH, D_MODEL)),
    ]
    out_specs = [tok(D_MODEL), tok(HG_WIDTH), tok(HG_WIDTH), tok(HG_WIDTH), tok(HG_WIDTH),
                 tok(HG_WIDTH), tok(D_MODEL), pl.BlockSpec((1, nh, LANES), lambda i: (i, 0, 0))]
    args = [x, p["ln_in_g"], p["ln_in_b"], p["w_in"], p["a_ln_g"], p["a_ln_b"], p["wsp"], p["sbf"],
            p["lb"], p["w_pa"]]
    if expert_weights is not None:
        slabs = [w.reshape(nt, w.shape[0] * w.shape[1] // nt, w.shape[2]) for w in expert_weights]
        w_specs = [pl.BlockSpec((1,) + w.shape[1:], lambda i: (i, 0, 0)) for w in slabs]
        in_specs += w_specs
        out_specs += w_specs
        outs += tuple(jax.ShapeDtypeStruct(w.shape, BF16) for w in slabs)
        args += slabs
    res = pl.pallas_call(
        _inproj_kernel,
        grid=(nt,),
        in_specs=in_specs,
        out_specs=tuple(out_specs),
        out_shape=outs,
        compiler_params=pltpu.CompilerParams(
            dimension_semantics=("parallel",), vmem_limit_bytes=VMEM_MIB["inproj"] << 20),
        name="inproj",
    )(*args)
    return res[:8], tuple(r.reshape(w.shape) for r, w in zip(res[8:], expert_weights or ()))


_NT = (((1,), (1,)), ((), ()))
_TN = (((0,), (0,)), ((), ()))


def _lockstep(*phased):
    phased = list(phased)
    while phased:
        phased = [g for g in phased if next(g, None) is not None]


def _hgrn_direction(q_ref, g_ref, v_ref, st_ref, o_ref, tri, mask, fwd):
    C = q_ref.shape[0]
    g = g_ref[...]
    ghi = g.astype(BF16)
    glo = (g - ghi.astype(F32)).astype(BF16)
    b = jnp.dot(jnp.concatenate([tri, tri], axis=1), jnp.concatenate([ghi, glo], axis=0),
                preferred_element_type=F32)
    mid = C // 2 - 1 if fwd else C // 2
    end = C - 1 if fwd else 0
    r = b[mid:mid + 1, :]
    b_end = b[end:end + 1, :]
    qt = q_ref[...].astype(F32) * jnp.exp(b - r)
    kt = (1.0 - jnp.exp(g)) * jnp.exp(r - b)
    qtb = qt.astype(BF16)
    ktb = kt.astype(BF16)
    qhb = (qt * jnp.exp(r)).astype(BF16)
    khb = (kt * jnp.exp(b_end - r)).astype(BF16)
    decay = jnp.exp(b_end)
    v = v_ref[...]
    heads =[slice(h * HG_DK, (h + 1) * HG_DK) for h in range(HG_HEADS)]
    zero = jnp.zeros((C, HG_DK), BF16)
    mask2 = jnp.concatenate([mask, mask], axis=1)
    yield True
    scores = []
    for p in range(HG_HEADS // 2):
        k1, k2 = ktb[:, heads[2 * p]], ktb[:, heads[2 * p + 1]]
        kk = jnp.concatenate([jnp.concatenate([k1, zero], axis=1),
                              jnp.concatenate([zero, k2], axis=1)], axis=0)
        s2 = lax.dot_general(qtb[:, 2 * p * HG_DK:(2 * p + 2) * HG_DK], kk, _NT,
                             preferred_element_type=F32)
        s2 = jnp.where(mask2, s2, 0.0).astype(BF16)
        scores += [s2[:, :C], s2[:, C:]]
    yield True
    for h, sl in enumerate(heads):
        st = st_ref[h]
        vt = v[:, sl].T
        o_ref[:, sl] = lax.dot_general(
            jnp.concatenate([scores[h], qhb[:, sl]], axis=1),
            jnp.concatenate([vt, st.astype(BF16)], axis=1), _NT,
            preferred_element_type=F32).astype(o_ref.dtype)
        st_ref[h] = st * decay[:, sl] + jnp.dot(vt, khb[:, sl], preferred_element_type=F32)


def _hgrn_direction_stepwise(q_ref, g_ref, v_ref, st_ref, o_ref, q32_ref, v32_ref, o32_ref, fwd):
    C = q_ref.shape[0]
    q32_ref[...] = q_ref[...].astype(F32)
    v32_ref[...] = v_ref[...].astype(F32)
    sub = SUBLANES
    rows = lax.broadcasted_iota(jnp.int32, (sub, HG_DK), 0)

    def group(i, carry):
        base = pl.multiple_of((i if fwd else C // sub - 1 - i) * sub, sub)
        f = jnp.exp(g_ref[pl.ds(base, sub), :])
        k = 1.0 - f
        q = q32_ref[pl.ds(base, sub), :]
        v = v32_ref[pl.ds(base, sub), :]
        for h in range(HG_HEADS):
            sl = slice(h * HG_DK, (h + 1) * HG_DK)
            st = st_ref[h]
            out = jnp.zeros((sub, HG_DK), F32)
            for r in (range(sub) if fwd else range(sub - 1, -1, -1)):
                v_t = jnp.where(rows == 0, v[r:r + 1, sl], 0.0).astype(BF16)
                k_t = jnp.broadcast_to(k[r:r + 1, sl], (sub, HG_DK)).astype(BF16)
                st = st * f[r:r + 1, sl] + lax.dot_general(v_t, k_t, _TN, preferred_element_type=F32)
                q_t = jnp.broadcast_to(q[r:r + 1, sl], (sub, HG_DK)).astype(BF16)
                o_t = lax.dot_general(q_t, st.astype(BF16), _NT, preferred_element_type=F32)
                out = jnp.where(rows == r, o_t, out)
            st_ref[h] = st
            o32_ref[pl.ds(base, sub), sl] = out
        return carry

    lax.fori_loop(0, C // sub, group, 0)
    o_ref[...] = o32_ref[...].astype(o_ref.dtype)


def _hgrn_kernel(safe_ref, qf_ref, qb_ref, gf_ref, gb_ref, vf_ref, vb_ref, tril_ref, triu_ref,
                 of_ref, ob_ref, sf_ref, sb_ref, q32_ref, v32_ref, o32_ref):
    b, j = pl.program_id(0), pl.program_id(1)
    ns = pl.num_programs(1)
    n = HG_CHUNKS_PER_STEP
    C = qf_ref.shape[0] // n

    @pl.when(j == 0)
    def _():
        sf_ref[...] = jnp.zeros_like(sf_ref)
        sb_ref[...] = jnp.zeros_like(sb_ref)

    row = lax.broadcasted_iota(jnp.int32, (C, C), 0)
    col = lax.broadcasted_iota(jnp.int32, (C, C), 1)
    safe_f = [safe_ref[0, (b * ns + j) * n + u] != 0 for u in range(n)]
    safe_b = [safe_ref[1, (b * ns + ns - 1 - j) * n + u] != 0 for u in range(n)]
    part = lambda ref, u: ref.at[pl.ds(u * C, C), :]

    def forward(u, stepwise):
        refs = (part(qf_ref, u), part(gf_ref, u), part(vf_ref, u), sf_ref, part(of_ref, u))
        if stepwise:
            _hgrn_direction_stepwise(*refs, q32_ref, v32_ref, o32_ref, True)
            return iter(())
        return _hgrn_direction(*refs, tril_ref[...], row >= col, True)

    def backward(u, stepwise):
        refs = (part(qb_ref, u), part(gb_ref, u), part(vb_ref, u), sb_ref, part(ob_ref, u))
        if stepwise:
            _hgrn_direction_stepwise(*refs, q32_ref, v32_ref, o32_ref, False)
            return iter(())
        return _hgrn_direction(*refs, triu_ref[...], row <= col, False)

    all_safe = functools.reduce(jnp.logical_and, safe_f + safe_b)

    @pl.when(all_safe)
    def _():
        _lockstep(*[d for u in range(n) for d in (forward(u, False), backward(n - 1 - u, False))])

    @pl.when(jnp.logical_not(all_safe))
    def _():
        run = lambda direction, u, stepwise: lambda: _lockstep(direction(u, stepwise))
        for u in range(n):
            pl.when(safe_f[u])(run(forward, u, False))
            pl.when(jnp.logical_not(safe_f[u]))(run(forward, u, True))
        for u in reversed(range(n)):
            pl.when(safe_b[u])(run(backward, u, False))
            pl.when(jnp.logical_not(safe_b[u]))(run(backward, u, True))


def _hgrn(q, gf, gb, iv, gmin, batch):
    T = q.shape[0]
    C = HG_CHUNK
    nc = T // batch // C
    nt, nh, _ = gmin.shape
    halves = gmin[:, :, 0].reshape(nt, 2, nh // 4, 2)
    safe = (jnp.min(halves, axis=-1) > -HG_SAFE_LOGDECAY).astype(jnp.int32)
    safe = safe.transpose(1, 0, 2).reshape(2, T // C)
    n = HG_CHUNKS_PER_STEP
    ns = nc // n
    fwd = pl.BlockSpec((n * C, HG_WIDTH), lambda b, j, s: (b * ns + j, 0))
    bwd = pl.BlockSpec((n * C, HG_WIDTH), lambda b, j, s: (b * ns + ns - 1 - j, 0))
    const = lambda shape: pl.BlockSpec(shape, lambda b, j, s: (0,) * len(shape),
                                       pipeline_mode=pl.Buffered(1))
    row = lax.broadcasted_iota(jnp.int32, (C, C), 0)
    col = lax.broadcasted_iota(jnp.int32, (C, C), 1)
    tril = (row >= col).astype(BF16)
    triu = (row <= col).astype(BF16)
    grid_spec = pltpu.PrefetchScalarGridSpec(
        num_scalar_prefetch=1,
        grid=(batch, ns),
        in_specs=[fwd, bwd, fwd, bwd, fwd, bwd, const((C, C)), const((C, C))],
        out_specs=(fwd, bwd),
        scratch_shapes=[pltpu.VMEM((HG_HEADS, HG_DK, HG_DK), F32),
                        pltpu.VMEM((HG_HEADS, HG_DK, HG_DK), F32),
                        pltpu.VMEM((C, HG_WIDTH), F32), pltpu.VMEM((C, HG_WIDTH), F32),
                        pltpu.VMEM((C, HG_WIDTH), F32)],
    )
    return pl.pallas_call(
        _hgrn_kernel,
        grid_spec=grid_spec,
        out_shape=(jax.ShapeDtypeStruct((T, HG_WIDTH), BF16), jax.ShapeDtypeStruct((T, HG_WIDTH), BF16)),
        compiler_params=pltpu.CompilerParams(
            dimension_semantics=("parallel", "arbitrary"), vmem_limit_bytes=VMEM_MIB["hgrn"] << 20),
        name="hgrn",
    )(safe, q, q, gf, gb, iv, iv, tril, triu)


def _mix_kernel(x_ref, lng_ref, lnb_ref, of_ref, ob_ref, so_ref, sgb_ref, ag_ref, ng_ref, wpb_ref,
                wo_ref, l1g_ref, l1b_ref, wr_ref, rb_ref, tri_ref, x1_ref, x1p_ref, route_ref,
                cnt_ref, carry_ref):
    tm = x_ref.shape[0]

    @pl.when(pl.program_id(0) == 0)
    def _():
        carry_ref[...] = jnp.zeros_like(carry_ref)

    o = of_ref[...].astype(F32) + ob_ref[...].astype(F32)
    heads = []
    for h in range(HG_HEADS):
        oh = o[:, h * HG_DK:(h + 1) * HG_DK]
        heads.append(oh * lax.rsqrt(jnp.mean(oh * oh, axis=-1, keepdims=True) + RMS_EPS))
    rn = jnp.concatenate(heads, axis=1) * ng_ref[...] * so_ref[...].astype(F32)
    r = _bdot(rn, wpb_ref[...])
    mixed = ag_ref[...].astype(F32) + sgb_ref[...].astype(F32) * r
    y = _bdot(mixed, wo_ref[...])
    xn = _layer_norm(x_ref[...], lng_ref[...], lnb_ref[...])
    x1 = _layer_norm(ALPHA * xn + y, l1g_ref[...], l1b_ref[...])
    x1_ref[...] = x1
    x1p_ref[...] = _pack_rows(x1)

    neg = jnp.float32(-jnp.inf)
    reps = tm // LANES
    scores = jax.nn.sigmoid(lax.dot_general(wr_ref[...], x1.astype(BF16), _NT,
                                            preferred_element_type=F32))
    biased = (scores + jnp.concatenate([rb_ref[...]] * reps, axis=1)).reshape(
        N_GROUPS, GROUP_SIZE, tm)
    sub = lax.broadcasted_iota(jnp.int32, biased.shape, 1).astype(F32)
    m1 = jnp.max(biased, axis=1, keepdims=True)
    first = jnp.min(jnp.where(biased == m1, sub, float(GROUP_SIZE)), axis=1, keepdims=True)
    m2 = jnp.max(jnp.where(sub == first, neg, biased), axis=1, keepdims=True)
    gs = (m1 + m2).reshape(N_GROUPS, tm)
    grp = lax.broadcasted_iota(jnp.int32, (N_GROUPS, tm), 0)
    ahead = jnp.zeros((N_GROUPS, tm), F32)
    for d in range(1, N_GROUPS):
        other = pltpu.roll(gs, d, 0)
        tie = jnp.where(grp >= d, 1.0, 0.0)
        ahead = ahead + jnp.where(other > gs, 1.0, jnp.where(other == gs, tie, 0.0))
    keep = (ahead < TOPK_GROUPS).reshape(N_GROUPS, 1, tm)
    allowed = jnp.where(keep, biased, neg).reshape(N_EXPERTS, tm)
    row = lax.broadcasted_iota(jnp.int32, (N_EXPERTS, tm), 0).astype(F32)
    sel = jnp.zeros((N_EXPERTS, tm), F32)
    picks = []
    for _ in range(TOP_K):
        m = jnp.max(allowed, axis=0, keepdims=True)
        first = jnp.min(jnp.where(allowed == m, row, float(N_EXPERTS)), axis=0, keepdims=True)
        hit = row == first
        picks.append((first, hit, jnp.sum(jnp.where(hit, scores, 0.0), axis=0, keepdims=True)))
        sel = jnp.where(hit, 1.0, sel)
        allowed = jnp.where(hit, neg, allowed)
    wsum = picks[0][2]
    for pk in picks[1:]:
        wsum = wsum + pk[2]
    selb = sel.astype(BF16)
    carry = carry_ref[...]
    before = (jnp.dot(selb, tri_ref[...], preferred_element_type=F32)
              + jnp.concatenate([carry] * reps, axis=1))
    total = carry + jnp.dot(selb, jnp.ones((tm, LANES), BF16), preferred_element_type=F32)
    carry_ref[...] = total
    cnt_ref[...] = total
    blank = [jnp.zeros((1, tm), F32)] * (SUBLANES - TOP_K)
    route_ref[...] = jnp.concatenate(
        [pk[0] for pk in picks] + blank
        + [jnp.sum(jnp.where(pk[1], before, 0.0), axis=0, keepdims=True) for pk in picks] + blank
        + [pk[2] / wsum * ROUTED_SCALE for pk in picks] + blank, axis=0)


def _mix(x, of, ob, so, sgb, ag, p):
    T = x.shape[0]
    tm = TOKEN_TILE
    tok = lambda w: pl.BlockSpec((tm, w), lambda i: (i, 0))
    row = lax.broadcasted_iota(jnp.int32, (tm, tm), 0)
    col = lax.broadcasted_iota(jnp.int32, (tm, tm), 1)
    tri = (row < col).astype(BF16)
    return pl.pallas_call(
        _mix_kernel,
        grid=(T // tm,),
        in_specs=[
            tok(D_MODEL), _const_spec((1, D_MODEL)), _const_spec((1, D_MODEL)),
            tok(HG_WIDTH), tok(HG_WIDTH), tok(HG_WIDTH), tok(D_MODEL), tok(D_MODEL),
            _const_spec((1, HG_WIDTH)),
            _const_spec((HG_WIDTH, D_MODEL)), _const_spec((D_MODEL, D_MODEL)),
            _const_spec((1, D_MODEL)), _const_spec((1, D_MODEL)),
            _const_spec((N_EXPERTS, D_MODEL)), _const_spec((N_EXPERTS, LANES)),
            _const_spec((tm, tm)),
        ],
        out_specs=(tok(D_MODEL), tok(D_MODEL // 2), pl.BlockSpec((ROUTE_ROWS, tm), lambda i: (0, i)),
                   pl.BlockSpec((N_EXPERTS, LANES), lambda i: (0, 0))),
        out_shape=(jax.ShapeDtypeStruct((T, D_MODEL), F32),
                   jax.ShapeDtypeStruct((T, D_MODEL // 2), jnp.uint32),
                   jax.ShapeDtypeStruct((ROUTE_ROWS, T), F32),
                   jax.ShapeDtypeStruct((N_EXPERTS, LANES), F32)),
        scratch_shapes=[pltpu.VMEM((N_EXPERTS, LANES), F32)],
        compiler_params=pltpu.CompilerParams(
            dimension_semantics=("arbitrary",), vmem_limit_bytes=VMEM_MIB["mix"] << 20),
        name="mix",
    )(x, p["ln_in_g"], p["ln_in_b"], of, ob, so, sgb, ag, p["hg_norm_g"], p["w_pb"], p["w_o"],
      p["ln1_g"], p["ln1_b"], p["w_router"], p["router_bias"], tri)


def _sc_mesh():
    return plsc.VectorSubcoreMesh(core_axis_name="c", subcore_axis_name="s",
                                  num_cores=SC_CORES, num_subcores=SC_SUBCORES)


def _sc_worker():
    return lax.axis_index("s") * SC_CORES + lax.axis_index("c")


def _sc_dispatch(x1p, dest, n_rows):
    T, w = x1p.shape
    W = SC_WINDOW
    per_worker = T // W // (SC_CORES * SC_SUBCORES)

    def body(x_hbm, d_hbm, o_hbm, rows_v, idx_v, sem):
        first = _sc_worker() * per_worker
        for k in range(TOP_K):
            pltpu.sync_copy(d_hbm.at[k, pl.ds(first, per_worker)], idx_v.at[k])

        @pl.loop(0, per_worker)
        def _(j):
            pltpu.sync_copy(x_hbm.at[pl.ds((first + j) * W, W)], rows_v)
            copies = [pltpu.async_copy(rows_v, o_hbm.at[idx_v.at[k, j]], sem) for k in range(TOP_K)]
            for c in copies:
                c.wait()

    return pl.kernel(
        body,
        out_type=jax.ShapeDtypeStruct((n_rows, w), jnp.uint32),
        mesh=_sc_mesh(),
        scratch_types=[pltpu.VMEM((W, w), jnp.uint32), pltpu.VMEM((TOP_K, per_worker, W), jnp.int32),
                       pltpu.SemaphoreType.DMA],
        name="sc_dispatch",
    )(x1p, dest.reshape(TOP_K, T // W, W))


def _sc_combine(ys, dest, gates):
    T = dest.shape[1]
    W = SC_COMBINE_WINDOW
    w = ys.shape[1]
    per_worker = T // W // (SC_CORES * SC_SUBCORES)
    assert per_worker % 2 == 0 and per_worker * W * SC_CORES * SC_SUBCORES == T
    assert 2 * W == SC_LANES

    def body(y_hbm, d_hbm, g_hbm, o_hbm, idx_v, gate_v, *scratch):
        slots = [dict(rows=scratch[s], out=scratch[2 + s], gsem=scratch[4 + s], wsem=scratch[6 + s],
                      lane0=s * W) for s in range(2)]
        first = _sc_worker() * per_worker
        for k in range(TOP_K):
            pltpu.sync_copy(d_hbm.at[k, pl.ds(first * W, per_worker * W)], idx_v.at[k])
            pltpu.sync_copy(g_hbm.at[k, pl.ds(first * W, per_worker * W)], gate_v.at[k])

        def loads(b, win):
            i = win - first
            return [pltpu.make_async_copy(y_hbm.at[idx_v.at[k, pl.ds(i * W, W)]],
                                          b["rows"].at[k], b["gsem"]) for k in range(TOP_K)]

        def write_back(b, win):
            return pltpu.make_async_copy(b["out"], o_hbm.at[pl.ds(win * W, W)], b["wsem"])

        def fetch(b, win):
            for c in loads(b, win):
                c.start()

        def reduce_rows(b, pair_gates):
            rows, out = b["rows"], b["out"]
            lane = lax.iota(jnp.int32, SC_LANES)

            @pl.loop(0, W)
            def _(j):
                mine = lane == b["lane0"] + j
                gate = [jnp.broadcast_to(jnp.sum(jnp.where(mine, g, 0.0)), (SC_LANES,))
                        for g in pair_gates]

                @plsc.parallel_loop(0, w, step=SC_LANES, unroll=4)
                def _(col):
                    lo, hi = [], []
                    for k in range(TOP_K):
                        p = rows[k, j, pl.ds(col, SC_LANES)]
                        lo.append(plsc.bitcast(p << 16, F32) * gate[k])
                        hi.append(plsc.bitcast(p & jnp.uint32(0xFFFF0000), F32) * gate[k])
                    out[j, pl.ds(col, SC_LANES)] = (lo[0] + lo[1]) + (lo[2] + lo[3]) + (lo[4] + lo[5])
                    out[j, pl.ds(w + col, SC_LANES)] = (hi[0] + hi[1]) + (hi[2] + hi[3]) + (hi[4] + hi[5])

        def process(b, win, pair_gates, not_first):
            for c in loads(b, win):
                c.wait()

            @pl.when(not_first)
            def _():
                write_back(b, win).wait()

            reduce_rows(b, pair_gates)
            write_back(b, win).start()

        fetch(slots[0], first)

        @pl.loop(0, per_worker, step=2)
        def _(i):
            win = first + i
            pair_gates = [gate_v[k, pl.ds(i * W, SC_LANES)] for k in range(TOP_K)]
            fetch(slots[1], win + 1)
            process(slots[0], win, pair_gates, i > 0)

            @pl.when(i + 2 < per_worker)
            def _():
                fetch(slots[0], win + 2)

            process(slots[1], win + 1, pair_gates, i > 0)

        for b in slots:
            write_back(b, first).wait()

    slot_types = ([pltpu.VMEM((TOP_K, per_worker * W), jnp.int32),
                   pltpu.VMEM((TOP_K, per_worker * W), F32)]
                  + [pltpu.VMEM((TOP_K, W, w), jnp.uint32)] * 2
                  + [pltpu.VMEM((W, 2 * w), F32)] * 2
                  + [pltpu.SemaphoreType.DMA] * 4)
    return pl.kernel(
        body,
        out_type=jax.ShapeDtypeStruct((T, 2 * w), F32),
        mesh=_sc_mesh(),
        scratch_types=slot_types,
        compiler_params=pltpu.CompilerParams(needs_layout_passes=False),
        name="sc_combine",
    )(ys, dest, gates)


def _experts_kernel(be_ref, nv_ref, io_ref, xs_ref, wg_ref, wu_ref, wd_ref, ys_ref):
    del be_ref, io_ref
    n_valid = nv_ref[pl.program_id(0)]
    mb, w = xs_ref.shape

    @pl.when(n_valid > 0)
    def _():
        keep = lax.broadcasted_iota(jnp.int32, (mb, w), 0) < n_valid
        lo, hi = _unpack_rows(jnp.where(keep, xs_ref[...], jnp.uint32(0)))
        lo, hi = lo.astype(BF16), hi.astype(BF16)

        def proj(w_ref):
            return (jnp.dot(lo, w_ref[0, :w, :], preferred_element_type=F32)
                    + jnp.dot(hi, w_ref[0, w:, :], preferred_element_type=F32))

        hb = jax.nn.silu(proj(wg_ref)) * proj(wu_ref)
        ys_ref[...] = _pack_rows(jnp.dot(hb.astype(BF16), wd_ref[0], preferred_element_type=F32))


def _experts(xs, blk_exp, n_valid, blk_io, weights):
    n_rows, w = xs.shape
    mb = n_rows // blk_exp.shape[0]
    by_expert = lambda shape: pl.BlockSpec((1,) + shape, lambda i, be, nv, io: (be[i], 0, 0))
    w_specs = [by_expert((D_MODEL, EXPERT_DIM)), by_expert((D_MODEL, EXPERT_DIM)),
               by_expert((EXPERT_DIM, D_MODEL))]
    ys_spec = pl.BlockSpec((mb, w), lambda i, be, nv, io: (io[i], 0))
    ys_shape = jax.ShapeDtypeStruct((n_rows, w), jnp.uint32)
    grid_spec = pltpu.PrefetchScalarGridSpec(
        num_scalar_prefetch=3,
        grid=(n_rows // mb,),
        in_specs=[pl.BlockSpec((mb, w), lambda i, be, nv, io: (io[i], 0))] + w_specs,
        out_specs=ys_spec,
    )
    return pl.pallas_call(
        _experts_kernel,
        grid_spec=grid_spec,
        out_shape=ys_shape,
        compiler_params=pltpu.CompilerParams(
            dimension_semantics=("arbitrary",), vmem_limit_bytes=VMEM_MIB["experts"] << 20),
        name="experts",
    )(blk_exp, n_valid, blk_io, xs, *weights)


def _final_kernel(x1_ref, routed_ref, wsg_ref, wsu_ref, wsd_ref, l2g_ref, l2b_ref, out_ref):
    x1 = x1_ref[...]
    xb = x1.astype(BF16)
    hs = (jax.nn.silu(jnp.dot(xb, wsg_ref[...], preferred_element_type=F32))
          * jnp.dot(xb, wsu_ref[...], preferred_element_type=F32))
    shared = jnp.dot(hs.astype(BF16), wsd_ref[...], preferred_element_type=F32)
    out_ref[...] = _layer_norm(ALPHA * x1 + (routed_ref[...] + shared), l2g_ref[...], l2b_ref[...])


def _final(x1, routed, p):
    T = x1.shape[0]
    tm = FINAL_TILE
    tok = lambda w: pl.BlockSpec((tm, w), lambda i: (i, 0))
    return pl.pallas_call(
        _final_kernel,
        grid=(T // tm,),
        in_specs=[
            tok(D_MODEL), tok(D_MODEL),
            _const_spec((D_MODEL, SHARED_DIM)), _const_spec((D_MODEL, SHARED_DIM)),
            _const_spec((SHARED_DIM, D_MODEL)),
            _const_spec((1, D_MODEL)), _const_spec((1, D_MODEL)),
        ],
        out_specs=tok(D_MODEL),
        out_shape=jax.ShapeDtypeStruct((T, D_MODEL), F32),
        compiler_params=pltpu.CompilerParams(
            dimension_semantics=("parallel",), vmem_limit_bytes=VMEM_MIB["final"] << 20),
        name="final",
    )(x1, routed, p["w_sh_gate"], p["w_sh_up"], p["w_sh_down"], p["ln2_g"], p["ln2_b"])


def _routing_layout(route, counts, n_tokens):
    mb = EXPERT_BLOCK
    n_blocks = -(-n_tokens * TOP_K // mb) + N_EXPERTS
    idx = route[0:TOP_K].astype(jnp.int32)
    rank = route[SUBLANES:SUBLANES + TOP_K].astype(jnp.int32)
    counts = counts.astype(jnp.int32)
    padded = (counts + mb - 1) // mb * mb
    pad_end = jnp.cumsum(padded)
    pad_start = pad_end - padded
    experts = jnp.arange(N_EXPERTS, dtype=jnp.int32)
    dest = rank + jnp.sum(jnp.where(idx[:, :, None] == experts, pad_start, 0), axis=-1)
    blk_start = jnp.arange(n_blocks, dtype=jnp.int32) * mb
    blk_exp = jnp.minimum(
        jnp.sum((pad_end[None, :] <= blk_start[:, None]).astype(jnp.int32), axis=1), N_EXPERTS - 1)
    valid_end = jnp.sum(jnp.where(blk_exp[:, None] == experts, pad_start + counts, 0), axis=-1)
    n_valid = jnp.clip(valid_end - blk_start, 0, mb).astype(jnp.int32)
    blk_io = jnp.minimum(jnp.arange(n_blocks, dtype=jnp.int32), pad_end[-1] // mb - 1)
    gates = route[2 * SUBLANES:2 * SUBLANES + TOP_K]
    return dest, gates, blk_exp, n_valid, blk_io, n_blocks * mb


def _encode(x, p, expert_weights):
    batch, seq, _ = x.shape
    T = batch * seq
    xt = x.reshape(T, D_MODEL)
    if expert_weights[0].dtype == BF16:
        (ag, q, gf, gb, iv, so, sgb, gmin), _ = _inproj(xt, p)
    else:
        (ag, q, gf, gb, iv, so, sgb, gmin), expert_weights = _inproj(xt, p, expert_weights)
    of, ob = _hgrn(q, gf, gb, iv, gmin, batch)
    x1, x1p, route, cnt = _mix(xt, of, ob, so, sgb, ag, p)
    dest, gates, blk_exp, n_valid, blk_io, n_rows = _routing_layout(route, cnt[:, 0], T)
    xs = _sc_dispatch(x1p, dest, n_rows)
    ys = _experts(xs, blk_exp, n_valid, blk_io, expert_weights)
    out = _final(x1, _sc_combine(ys, dest, gates), p)
    return out.reshape(batch, seq, D_MODEL), expert_weights


def _prepare_params(ln_in_g, ln_in_b, w_in, a_ln_g, a_ln_b, a_ws, a_sb, hg_lb_logits, hg_norm_g,
                    w_pa, w_pb, w_o, ln1_g, ln1_b, w_router, router_bias, w_sh_gate, w_sh_up,
                    w_sh_down, ln2_g, ln2_b):
    l = 0
    row = lambda v: v.reshape(1, -1).astype(F32)
    ws = a_ws[l].astype(BF16)
    wsp = jnp.concatenate([ws[0::2], ws[1::2]], axis=2)
    sbf = jnp.repeat(a_sb[l].astype(F32), A_WIDTH // A_GROUPS, axis=1)
    lb = jnp.cumsum(jax.nn.softmax(hg_lb_logits.astype(F32), axis=1), axis=1)[:, l]
    return dict(
        ln_in_g=row(ln_in_g), ln_in_b=row(ln_in_b), w_in=w_in[l].astype(BF16),
        a_ln_g=row(a_ln_g[l]), a_ln_b=row(a_ln_b[l]), wsp=wsp, sbf=sbf, lb=lb,
        w_pa=w_pa[l].astype(BF16), hg_norm_g=row(hg_norm_g[l]),
        w_pb=w_pb[l].astype(BF16), w_o=w_o[l].astype(BF16),
        ln1_g=row(ln1_g[l]), ln1_b=row(ln1_b[l]),
        w_router=w_router[l].T.astype(BF16),
        router_bias=jnp.broadcast_to(router_bias[l].astype(F32)[:, None], (N_EXPERTS, LANES)),
        w_sh_gate=w_sh_gate[l].astype(BF16), w_sh_up=w_sh_up[l].astype(BF16),
        w_sh_down=w_sh_down[l].astype(BF16),
        ln2_g=row(ln2_g[l]), ln2_b=row(ln2_b[l]),
    )


def kernel(x_prompt, x_sample, ln_in_g, ln_in_b, w_in, a_ln_g, a_ln_b, a_ws, a_sb, hg_lb_logits,
           hg_norm_g, w_pa, w_pb, w_o, ln1_g, ln1_b, w_router, router_bias, w_e_gate, w_e_up,
           w_e_down, w_sh_gate, w_sh_up, w_sh_down, ln2_g, ln2_b):
    p = _prepare_params(ln_in_g, ln_in_b, w_in, a_ln_g, a_ln_b, a_ws, a_sb, hg_lb_logits, hg_norm_g,
                        w_pa, w_pb, w_o, ln1_g, ln1_b, w_router, router_bias, w_sh_gate, w_sh_up,
                        w_sh_down, ln2_g, ln2_b)
    y_prompt, expert_weights = _encode(x_prompt, p, (w_e_gate[0], w_e_up[0], w_e_down[0]))
    y_sample, _ = _encode(x_sample, p, expert_weights)
    return y_prompt, y_sample
```

```python
import functools

import jax
import jax.numpy as jnp
from jax import lax
from jax.experimental import pallas as pl
from jax.experimental.pallas import tpu as pltpu
from jax.experimental.pallas import tpu_sc as plsc

F32 = jnp.float32
BF16 = jnp.bfloat16

D_MODEL = 1024
A_GROUPS = 8
A_WIDTH = 512
A_CHUNK = 128
HG_HEADS = 8
HG_DK = 128
HG_WIDTH = HG_HEADS * HG_DK
N_IN = 2 * A_WIDTH + 5 * HG_WIDTH + 2 * D_MODEL
N_EXPERTS = 64
TOP_K = 6
N_GROUPS = 8
TOPK_GROUPS = 4
GROUP_SIZE = N_EXPERTS // N_GROUPS
EXPERT_DIM = 256
SHARED_DIM = 256
ROUTED_SCALE = 2.5
DEPTH = 1
ALPHA = (2.0 * DEPTH) ** 0.25
LN_EPS = 1e-5
RMS_EPS = 1e-6

LANES = 128
SUBLANES = 8
ROUTE_ROWS = 3 * SUBLANES
TOKEN_TILE = 512
FINAL_TILE = 1024
INPROJ_TILE = 256
HG_CHUNK = 128
HG_CHUNKS_PER_STEP = 2
HG_SAFE_LOGDECAY = 80.0
EXPERT_BLOCK = 1024
SC_CORES = 2
SC_SUBCORES = 16
SC_LANES = 16
SC_WINDOW = 32
SC_COMBINE_WINDOW = 8
VMEM_MIB = dict(inproj=48, hgrn=32, mix=40, experts=32, final=40)

_O_U, _O_V, _O_Q, _O_FF, _O_FB, _O_I, _O_G, _O_GA, _O_GB = (
    0, 512, 1024, 2048, 3072, 4096, 5120, 6144, 7168)


def _layer_norm(x, g, b):
    mu = jnp.mean(x, axis=-1, keepdims=True)
    xc = x - mu
    var = jnp.mean(xc * xc, axis=-1, keepdims=True)
    return xc * lax.rsqrt(var + LN_EPS) * g + b


def _bdot(a, b):
    return jnp.dot(a.astype(BF16), b.astype(BF16), preferred_element_type=F32)


def _pack_rows(x):
    w = x.shape[1] // 2
    lo = lax.bitcast_convert_type(x[:, :w].astype(BF16).astype(F32), jnp.uint32)
    hi = lax.bitcast_convert_type(x[:, w:].astype(BF16).astype(F32), jnp.uint32)
    return hi | (lo >> 16)


def _unpack_rows(p):
    lo = lax.bitcast_convert_type(p << 16, F32)
    hi = lax.bitcast_convert_type(p & jnp.uint32(0xFFFF0000), F32)
    return lo, hi


def _const_spec(shape):
    nd = len(shape)
    return pl.BlockSpec(shape, lambda *_: (0,) * nd, pipeline_mode=pl.Buffered(1))


def _inproj_kernel(x_ref, lng_ref, lnb_ref, win_ref, alng_ref, alnb_ref, wsp_ref, sbf_ref, lb_ref,
                   wpa_ref, *rest):
    if len(rest) > 8:
        for src, dst in zip(rest[:3], rest[11:]):
            dst[...] = src[...].astype(BF16)
        rest = rest[3:11]
    ag_ref, q_ref, gf_ref, gb_ref, iv_ref, so_ref, sgb_ref, gmin_ref = rest
    tm = x_ref.shape[0]
    xb = _layer_norm(x_ref[...], lng_ref[...], lnb_ref[...]).astype(BF16)

    def sec(lo, width):
        return jnp.dot(xb, win_ref[:, lo:lo + width], preferred_element_type=F32)

    v = _layer_norm(jax.nn.gelu(sec(_O_V, A_WIDTH)), alng_ref[...], alnb_ref[...]).astype(BF16)
    u = jax.nn.gelu(sec(_O_U, A_WIDTH))

    q_ref[...] = jax.nn.silu(sec(_O_Q, HG_WIDTH)).astype(BF16)
    mins = []
    for d, (off, g_ref) in enumerate(((_O_FF, gf_ref), (_O_FB, gb_ref))):
        lb = lb_ref[d:d + 1, :]
        f = lb + (1.0 - lb) * jax.nn.sigmoid(sec(off, HG_WIDTH))
        g = jnp.log(f)
        g_ref[...] = g
        half = jnp.sum(g.reshape(tm // (HG_CHUNK // 2), HG_CHUNK // 2, HG_WIDTH), axis=1)
        mins.append(jnp.min(half, axis=-1, keepdims=True))
    gmin_ref[0] = jnp.broadcast_to(jnp.concatenate(mins, axis=0), gmin_ref.shape[1:])
    so_ref[...] = jax.nn.silu(sec(_O_G, HG_WIDTH)).astype(BF16)
    sgb_ref[...] = jax.nn.sigmoid(sec(_O_GB, D_MODEL)).astype(BF16)

    lane = lax.broadcasted_iota(jnp.int32, (A_CHUNK, LANES), 1)
    left = lane < (A_WIDTH // A_GROUPS)
    zero = jnp.zeros((A_CHUNK, LANES), BF16)
    chunks = []
    for c in range(tm // A_CHUNK):
        vc = v[c * A_CHUNK:(c + 1) * A_CHUNK]
        cols = []
        for p in range(A_GROUPS // 2):
            vp = vc[:, p * LANES:(p + 1) * LANES]
            rhs = jnp.concatenate([jnp.where(left, vp, zero), jnp.where(left, zero, vp)], axis=0)
            cols.append(jnp.dot(wsp_ref[p], rhs, preferred_element_type=F32))
        chunks.append(jnp.concatenate(cols, axis=1) + sbf_ref[...])
    mixed = jnp.concatenate(chunks, axis=0)
    a = _bdot(u * mixed, wpa_ref[...])
    ag_ref[...] = (jax.nn.sigmoid(sec(_O_GA, D_MODEL)) * a).astype(BF16)
    iv_ref[...] = sec(_O_I, HG_WIDTH).astype(BF16)


def _inproj(x, p, expert_weights=None):
    T = x.shape[0]
    tm = INPROJ_TILE
    nt = T // tm
    nh = 2 * tm // (HG_CHUNK // 2)
    tok = lambda w: pl.BlockSpec((tm, w), lambda i: (i, 0))
    outs = (
        jax.ShapeDtypeStruct((T, D_MODEL), BF16),
        jax.ShapeDtypeStruct((T, HG_WIDTH), BF16),
        jax.ShapeDtypeStruct((T, HG_WIDTH), F32),
        jax.ShapeDtypeStruct((T, HG_WIDTH), F32),
        jax.ShapeDtypeStruct((T, HG_WIDTH), BF16),
        jax.ShapeDtypeStruct((T, HG_WIDTH), BF16),
        jax.ShapeDtypeStruct((T, D_MODEL), BF16),
        jax.ShapeDtypeStruct((nt, nh, LANES), F32),
    )
    in_specs = [
        tok(D_MODEL),
        _const_spec((1, D_MODEL)), _const_spec((1, D_MODEL)),
        _const_spec((D_MODEL, N_IN)),
        _const_spec((1, A_WIDTH)), _const_spec((1, A_WIDTH)),
        _const_spec((A_GROUPS // 2, A_CHUNK, 2 * A_CHUNK)),
        _const_spec((A_CHUNK, A_WIDTH)),
        _const_spec((2, HG_WIDTH)),
        _const_spec((A_WIDTH, D_MODEL)),
    ]
    out_specs = [tok(D_MODEL), tok(HG_WIDTH), tok(HG_WIDTH), tok(HG_WIDTH), tok(HG_WIDTH),
                 tok(HG_WIDTH), tok(D_MODEL), pl.BlockSpec((1, nh, LANES), lambda i: (i, 0, 0))]
    args = [x, p["ln_in_g"], p["ln_in_b"], p["w_in"], p["a_ln_g"], p["a_ln_b"], p["wsp"], p["sbf"],
            p["lb"], p["w_pa"]]
    if expert_weights is not None:
        slabs = [w.reshape(nt, w.shape[0] * w.shape[1] // nt, w.shape[2]) for w in expert_weights]
        w_specs = [pl.BlockSpec((1,) + w.shape[1:], lambda i: (i, 0, 0)) for w in slabs]
        in_specs += w_specs
        out_specs += w_specs
        outs += tuple(jax.ShapeDtypeStruct(w.shape, BF16) for w in slabs)
        args += slabs
    res = pl.pallas_call(
        _inproj_kernel,
        grid=(nt,),
        in_specs=in_specs,
        out_specs=tuple(out_specs),
        out_shape=outs,
        compiler_params=pltpu.CompilerParams(
            dimension_semantics=("parallel",), vmem_limit_bytes=VMEM_MIB["inproj"] << 20),
        name="inproj",
    )(*args)
    return res[:8], tuple(r.reshape(w.shape) for r, w in zip(res[8:], expert_weights or ()))


_NT = (((1,), (1,)), ((), ()))
_TN = (((0,), (0,)), ((), ()))


def _lockstep(*phased):
    phased = list(phased)
    while phased:
        phased = [g for g in phased if next(g, None) is not None]


def _hgrn_direction(q_ref, g_ref, v_ref, st_ref, o_ref, tri, mask, fwd):
    C = q_ref.shape[0]
    g = g_ref[...]
    ghi = g.astype(BF16)
    glo = (g - ghi.astype(F32)).astype(BF16)
    b = jnp.dot(jnp.concatenate([tri, tri], axis=1), jnp.concatenate([ghi, glo], axis=0),
                preferred_element_type=F32)
    mid = C // 2 - 1 if fwd else C // 2
    end = C - 1 if fwd else 0
    r = b[mid:mid + 1, :]
    b_end = b[end:end + 1, :]
    qt = q_ref[...].astype(F32) * jnp.exp(b - r)
    kt = (1.0 - jnp.exp(g)) * jnp.exp(r - b)
    qtb = qt.astype(BF16)
    ktb = kt.astype(BF16)
    qhb = (qt * jnp.exp(r)).astype(BF16)
    khb = (kt * jnp.exp(b_end - r)).astype(BF16)
    decay = jnp.exp(b_end)
    v = v_ref[...]
    heads =[slice(h * HG_DK, (h + 1) * HG_DK) for h in range(HG_HEADS)]
    zero = jnp.zeros((C, HG_DK), BF16)
    mask2 = jnp.concatenate([mask, mask], axis=1)
    yield True
    scores = []
    for p in range(HG_HEADS // 2):
        k1, k2 = ktb[:, heads[2 * p]], ktb[:, heads[2 * p + 1]]
        kk = jnp.concatenate([jnp.concatenate([k1, zero], axis=1),
                              jnp.concatenate([zero, k2], axis=1)], axis=0)
        s2 = lax.dot_general(qtb[:, 2 * p * HG_DK:(2 * p + 2) * HG_DK], kk, _NT,
                             preferred_element_type=F32)
        s2 = jnp.where(mask2, s2, 0.0).astype(BF16)
        scores += [s2[:, :C], s2[:, C:]]
    yield True
    for h, sl in enumerate(heads):
        st = st_ref[h]
        vt = v[:, sl].T
        o_ref[:, sl] = lax.dot_general(
            jnp.concatenate([scores[h], qhb[:, sl]], axis=1),
            jnp.concatenate([vt, st.astype(BF16)], axis=1), _NT,
            preferred_element_type=F32).astype(o_ref.dtype)
        st_ref[h] = st * decay[:, sl] + jnp.dot(vt, khb[:, sl], preferred_element_type=F32)


def _hgrn_direction_stepwise(q_ref, g_ref, v_ref, st_ref, o_ref, q32_ref, v32_ref, o32_ref, fwd):
    C = q_ref.shape[0]
    q32_ref[...] = q_ref[...].astype(F32)
    v32_ref[...] = v_ref[...].astype(F32)
    sub = SUBLANES
    rows = lax.broadcasted_iota(jnp.int32, (sub, HG_DK), 0)

    def group(i, carry):
        base = pl.multiple_of((i if fwd else C // sub - 1 - i) * sub, sub)
        f = jnp.exp(g_ref[pl.ds(base, sub), :])
        k = 1.0 - f
        q = q32_ref[pl.ds(base, sub), :]
        v = v32_ref[pl.ds(base, sub), :]
        for h in range(HG_HEADS):
            sl = slice(h * HG_DK, (h + 1) * HG_DK)
            st = st_ref[h]
            out = jnp.zeros((sub, HG_DK), F32)
            for r in (range(sub) if fwd else range(sub - 1, -1, -1)):
                v_t = jnp.where(rows == 0, v[r:r + 1, sl], 0.0).astype(BF16)
                k_t = jnp.broadcast_to(k[r:r + 1, sl], (sub, HG_DK)).astype(BF16)
                st = st * f[r:r + 1, sl] + lax.dot_general(v_t, k_t, _TN, preferred_element_type=F32)
                q_t = jnp.broadcast_to(q[r:r + 1, sl], (sub, HG_DK)).astype(BF16)
                o_t = lax.dot_general(q_t, st.astype(BF16), _NT, preferred_element_type=F32)
                out = jnp.where(rows == r, o_t, out)
            st_ref[h] = st
            o32_ref[pl.ds(base, sub), sl] = out
        return carry

    lax.fori_loop(0, C // sub, group, 0)
    o_ref[...] = o32_ref[...].astype(o_ref.dtype)


def _hgrn_kernel(safe_ref, qf_ref, qb_ref, gf_ref, gb_ref, vf_ref, vb_ref, tril_ref, triu_ref,
                 of_ref, ob_ref, sf_ref, sb_ref, q32_ref, v32_ref, o32_ref):
    b, j = pl.program_id(0), pl.program_id(1)
    ns = pl.num_programs(1)
    n = HG_CHUNKS_PER_STEP
    C = qf_ref.shape[0] // n

    @pl.when(j == 0)
    def _():
        sf_ref[...] = jnp.zeros_like(sf_ref)
        sb_ref[...] = jnp.zeros_like(sb_ref)

    row = lax.broadcasted_iota(jnp.int32, (C, C), 0)
    col = lax.broadcasted_iota(jnp.int32, (C, C), 1)
    safe_f = [safe_ref[0, (b * ns + j) * n + u] != 0 for u in range(n)]
    safe_b = [safe_ref[1, (b * ns + ns - 1 - j) * n + u] != 0 for u in range(n)]
    part = lambda ref, u: ref.at[pl.ds(u * C, C), :]

    def forward(u, stepwise):
        refs = (part(qf_ref, u), part(gf_ref, u), part(vf_ref, u), sf_ref, part(of_ref, u))
        if stepwise:
            _hgrn_direction_stepwise(*refs, q32_ref, v32_ref, o32_ref, True)
            return iter(())
        return _hgrn_direction(*refs, tril_ref[...], row >= col, True)

    def backward(u, stepwise):
        refs = (part(qb_ref, u), part(gb_ref, u), part(vb_ref, u), sb_ref, part(ob_ref, u))
        if stepwise:
            _hgrn_direction_stepwise(*refs, q32_ref, v32_ref, o32_ref, False)
            return iter(())
        return _hgrn_direction(*refs, triu_ref[...], row <= col, False)

    all_safe = functools.reduce(jnp.logical_and, safe_f + safe_b)

    @pl.when(all_safe)
    def _():
        _lockstep(*[d for u in range(n) for d in (forward(u, False), backward(n - 1 - u, False))])

    @pl.when(jnp.logical_not(all_safe))
    def _():
        run = lambda direction, u, stepwise: lambda: _lockstep(direction(u, stepwise))
        for u in range(n):
            pl.when(safe_f[u])(run(forward, u, False))
            pl.when(jnp.logical_not(safe_f[u]))(run(forward, u, True))
        for u in reversed(range(n)):
            pl.when(safe_b[u])(run(backward, u, False))
            pl.when(jnp.logical_not(safe_b[u]))(run(backward, u, True))


def _hgrn(q, gf, gb, iv, gmin, batch):
    T = q.shape[0]
    C = HG_CHUNK
    nc = T // batch // C
    nt, nh, _ = gmin.shape
    halves = gmin[:, :, 0].reshape(nt, 2, nh // 4, 2)
    safe = (jnp.min(halves, axis=-1) > -HG_SAFE_LOGDECAY).astype(jnp.int32)
    safe = safe.transpose(1, 0, 2).reshape(2, T // C)
    n = HG_CHUNKS_PER_STEP
    ns = nc // n
    fwd = pl.BlockSpec((n * C, HG_WIDTH), lambda b, j, s: (b * ns + j, 0))
    bwd = pl.BlockSpec((n * C, HG_WIDTH), lambda b, j, s: (b * ns + ns - 1 - j, 0))
    const = lambda shape: pl.BlockSpec(shape, lambda b, j, s: (0,) * len(shape),
                                       pipeline_mode=pl.Buffered(1))
    row = lax.broadcasted_iota(jnp.int32, (C, C), 0)
    col = lax.broadcasted_iota(jnp.int32, (C, C), 1)
    tril = (row >= col).astype(BF16)
    triu = (row <= col).astype(BF16)
    grid_spec = pltpu.PrefetchScalarGridSpec(
        num_scalar_prefetch=1,
        grid=(batch, ns),
        in_specs=[fwd, bwd, fwd, bwd, fwd, bwd, const((C, C)), const((C, C))],
        out_specs=(fwd, bwd),
        scratch_shapes=[pltpu.VMEM((HG_HEADS, HG_DK, HG_DK), F32),
                        pltpu.VMEM((HG_HEADS, HG_DK, HG_DK), F32),
                        pltpu.VMEM((C, HG_WIDTH), F32), pltpu.VMEM((C, HG_WIDTH), F32),
                        pltpu.VMEM((C, HG_WIDTH), F32)],
    )
    return pl.pallas_call(
        _hgrn_kernel,
        grid_spec=grid_spec,
        out_shape=(jax.ShapeDtypeStruct((T, HG_WIDTH), BF16), jax.ShapeDtypeStruct((T, HG_WIDTH), BF16)),
        compiler_params=pltpu.CompilerParams(
            dimension_semantics=("parallel", "arbitrary"), vmem_limit_bytes=VMEM_MIB["hgrn"] << 20),
        name="hgrn",
    )(safe, q, q, gf, gb, iv, iv, tril, triu)


def _mix_kernel(x_ref, lng_ref, lnb_ref, of_ref, ob_ref, so_ref, sgb_ref, ag_ref, ng_ref, wpb_ref,
                wo_ref, l1g_ref, l1b_ref, wr_ref, rb_ref, tri_ref, x1_ref, x1p_ref, route_ref,
                cnt_ref, carry_ref):
    tm = x_ref.shape[0]

    @pl.when(pl.program_id(0) == 0)
    def _():
        carry_ref[...] = jnp.zeros_like(carry_ref)

    o = of_ref[...].astype(F32) + ob_ref[...].astype(F32)
    heads = []
    for h in range(HG_HEADS):
        oh = o[:, h * HG_DK:(h + 1) * HG_DK]
        heads.append(oh * lax.rsqrt(jnp.mean(oh * oh, axis=-1, keepdims=True) + RMS_EPS))
    rn = jnp.concatenate(heads, axis=1) * ng_ref[...] * so_ref[...].astype(F32)
    r = _bdot(rn, wpb_ref[...])
    mixed = ag_ref[...].astype(F32) + sgb_ref[...].astype(F32) * r
    y = _bdot(mixed, wo_ref[...])
    xn = _layer_norm(x_ref[...], lng_ref[...], lnb_ref[...])
    x1 = _layer_norm(ALPHA * xn + y, l1g_ref[...], l1b_ref[...])
    x1_ref[...] = x1
    x1p_ref[...] = _pack_rows(x1)

    neg = jnp.float32(-jnp.inf)
    reps = tm // LANES
    scores = jax.nn.sigmoid(lax.dot_general(wr_ref[...], x1.astype(BF16), _NT,
                                            preferred_element_type=F32))
    biased = (scores + jnp.concatenate([rb_ref[...]] * reps, axis=1)).reshape(
        N_GROUPS, GROUP_SIZE, tm)
    sub = lax.broadcasted_iota(jnp.int32, biased.shape, 1).astype(F32)
    m1 = jnp.max(biased, axis=1, keepdims=True)
    first = jnp.min(jnp.where(biased == m1, sub, float(GROUP_SIZE)), axis=1, keepdims=True)
    m2 = jnp.max(jnp.where(sub == first, neg, biased), axis=1, keepdims=True)
    gs = (m1 + m2).reshape(N_GROUPS, tm)
    grp = lax.broadcasted_iota(jnp.int32, (N_GROUPS, tm), 0)
    ahead = jnp.zeros((N_GROUPS, tm), F32)
    for d in range(1, N_GROUPS):
        other = pltpu.roll(gs, d, 0)
        tie = jnp.where(grp >= d, 1.0, 0.0)
        ahead = ahead + jnp.where(other > gs, 1.0, jnp.where(other == gs, tie, 0.0))
    keep = (ahead < TOPK_GROUPS).reshape(N_GROUPS, 1, tm)
    allowed = jnp.where(keep, biased, neg).reshape(N_EXPERTS, tm)
    row = lax.broadcasted_iota(jnp.int32, (N_EXPERTS, tm), 0).astype(F32)
    sel = jnp.zeros((N_EXPERTS, tm), F32)
    picks = []
    for _ in range(TOP_K):
        m = jnp.max(allowed, axis=0, keepdims=True)
        first = jnp.min(jnp.where(allowed == m, row, float(N_EXPERTS)), axis=0, keepdims=True)
        hit = row == first
        picks.append((first, hit, jnp.sum(jnp.where(hit, scores, 0.0), axis=0, keepdims=True)))
        sel = jnp.where(hit, 1.0, sel)
        allowed = jnp.where(hit, neg, allowed)
    wsum = picks[0][2]
    for pk in picks[1:]:
        wsum = wsum + pk[2]
    selb = sel.astype(BF16)
    carry = carry_ref[...]
    before = (jnp.dot(selb, tri_ref[...], preferred_element_type=F32)
              + jnp.concatenate([carry] * reps, axis=1))
    total = carry + jnp.dot(selb, jnp.ones((tm, LANES), BF16), preferred_element_type=F32)
    carry_ref[...] = total
    cnt_ref[...] = total
    blank = [jnp.zeros((1, tm), F32)] * (SUBLANES - TOP_K)
    route_ref[...] = jnp.concatenate(
        [pk[0] for pk in picks] + blank
        + [jnp.sum(jnp.where(pk[1], before, 0.0), axis=0, keepdims=True) for pk in picks] + blank
        + [pk[2] / wsum * ROUTED_SCALE for pk in picks] + blank, axis=0)


def _mix(x, of, ob, so, sgb, ag, p):
    T = x.shape[0]
    tm = TOKEN_TILE
    tok = lambda w: pl.BlockSpec((tm, w), lambda i: (i, 0))
    row = lax.broadcasted_iota(jnp.int32, (tm, tm), 0)
    col = lax.broadcasted_iota(jnp.int32, (tm, tm), 1)
    tri = (row < col).astype(BF16)
    return pl.pallas_call(
        _mix_kernel,
        grid=(T // tm,),
        in_specs=[
            tok(D_MODEL), _const_spec((1, D_MODEL)), _const_spec((1, D_MODEL)),
            tok(HG_WIDTH), tok(HG_WIDTH), tok(HG_WIDTH), tok(D_MODEL), tok(D_MODEL),
            _const_spec((1, HG_WIDTH)),
            _const_spec((HG_WIDTH, D_MODEL)), _const_spec((D_MODEL, D_MODEL)),
            _const_spec((1, D_MODEL)), _const_spec((1, D_MODEL)),
            _const_spec((N_EXPERTS, D_MODEL)), _const_spec((N_EXPERTS, LANES)),
            _const_spec((tm, tm)),
        ],
        out_specs=(tok(D_MODEL), tok(D_MODEL // 2), pl.BlockSpec((ROUTE_ROWS, tm), lambda i: (0, i)),
                   pl.BlockSpec((N_EXPERTS, LANES), lambda i: (0, 0))),
        out_shape=(jax.ShapeDtypeStruct((T, D_MODEL), F32),
                   jax.ShapeDtypeStruct((T, D_MODEL // 2), jnp.uint32),
                   jax.ShapeDtypeStruct((ROUTE_ROWS, T), F32),
                   jax.ShapeDtypeStruct((N_EXPERTS, LANES), F32)),
        scratch_shapes=[pltpu.VMEM((N_EXPERTS, LANES), F32)],
        compiler_params=pltpu.CompilerParams(
            dimension_semantics=("arbitrary",), vmem_limit_bytes=VMEM_MIB["mix"] << 20),
        name="mix",
    )(x, p["ln_in_g"], p["ln_in_b"], of, ob, so, sgb, ag, p["hg_norm_g"], p["w_pb"], p["w_o"],
      p["ln1_g"], p["ln1_b"], p["w_router"], p["router_bias"], tri)


def _sc_mesh():
    return plsc.VectorSubcoreMesh(core_axis_name="c", subcore_axis_name="s",
                                  num_cores=SC_CORES, num_subcores=SC_SUBCORES)


def _sc_worker():
    return lax.axis_index("s") * SC_CORES + lax.axis_index("c")


def _sc_dispatch(x1p, dest, n_rows):
    T, w = x1p.shape
    W = SC_WINDOW
    per_worker = T // W // (SC_CORES * SC_SUBCORES)

    def body(x_hbm, d_hbm, o_hbm, rows_v, idx_v, sem):
        first = _sc_worker() * per_worker
        for k in range(TOP_K):
            pltpu.sync_copy(d_hbm.at[k, pl.ds(first, per_worker)], idx_v.at[k])

        @pl.loop(0, per_worker)
        def _(j):
            pltpu.sync_copy(x_hbm.at[pl.ds((first + j) * W, W)], rows_v)
            copies = [pltpu.async_copy(rows_v, o_hbm.at[idx_v.at[k, j]], sem) for k in range(TOP_K)]
            for c in copies:
                c.wait()

    return pl.kernel(
        body,
        out_type=jax.ShapeDtypeStruct((n_rows, w), jnp.uint32),
        mesh=_sc_mesh(),
        scratch_types=[pltpu.VMEM((W, w), jnp.uint32), pltpu.VMEM((TOP_K, per_worker, W), jnp.int32),
                       pltpu.SemaphoreType.DMA],
        name="sc_dispatch",
    )(x1p, dest.reshape(TOP_K, T // W, W))


def _sc_combine(ys, dest, gates):
    T = dest.shape[1]
    W = SC_COMBINE_WINDOW
    w = ys.shape[1]
    per_worker = T // W // (SC_CORES * SC_SUBCORES)
    assert per_worker % 2 == 0 and per_worker * W * SC_CORES * SC_SUBCORES == T
    assert 2 * W == SC_LANES

    def body(y_hbm, d_hbm, g_hbm, o_hbm, idx_v, gate_v, *scratch):
        slots = [dict(rows=scratch[s], out=scratch[2 + s], gsem=scratch[4 + s], wsem=scratch[6 + s],
                      lane0=s * W) for s in range(2)]
        first = _sc_worker() * per_worker
        for k in range(TOP_K):
            pltpu.sync_copy(d_hbm.at[k, pl.ds(first * W, per_worker * W)], idx_v.at[k])
            pltpu.sync_copy(g_hbm.at[k, pl.ds(first * W, per_worker * W)], gate_v.at[k])

        def loads(b, win):
            i = win - first
            return [pltpu.make_async_copy(y_hbm.at[idx_v.at[k, pl.ds(i * W, W)]],
                                          b["rows"].at[k], b["gsem"]) for k in range(TOP_K)]

        def write_back(b, win):
            return pltpu.make_async_copy(b["out"], o_hbm.at[pl.ds(win * W, W)], b["wsem"])

        def fetch(b, win):
            for c in loads(b, win):
                c.start()

        def rounded(x):
            u = plsc.bitcast(x, jnp.uint32)
            return u + (jnp.uint32(0x7FFF) + ((u >> 16) & jnp.uint32(1)))

        def reduce_rows(b, pair_gates):
            rows, out = b["rows"], b["out"]
            lane = lax.iota(jnp.int32, SC_LANES)

            @pl.loop(0, W)
            def _(j):
                mine = lane == b["lane0"] + j
                gate = [jnp.broadcast_to(jnp.sum(jnp.where(mine, g, 0.0)), (SC_LANES,))
                        for g in pair_gates]

                @plsc.parallel_loop(0, w, step=SC_LANES, unroll=4)
                def _(col):
                    lo, hi = [], []
                    for k in range(TOP_K):
                        p = rows[k, j, pl.ds(col, SC_LANES)]
                        lo.append(plsc.bitcast(p << 16, F32) * gate[k])
                        hi.append(plsc.bitcast(p & jnp.uint32(0xFFFF0000), F32) * gate[k])
                    lo = rounded((lo[0] + lo[1]) + (lo[2] + lo[3]) + (lo[4] + lo[5]))
                    hi = rounded((hi[0] + hi[1]) + (hi[2] + hi[3]) + (hi[4] + hi[5]))
                    out[j, pl.ds(col, SC_LANES)] = (hi & jnp.uint32(0xFFFF0000)) | (lo >> 16)

        def process(b, win, pair_gates, not_first):
            for c in loads(b, win):
                c.wait()

            @pl.when(not_first)
            def _():
                write_back(b, win).wait()

            reduce_rows(b, pair_gates)
            write_back(b, win).start()

        fetch(slots[0], first)

        @pl.loop(0, per_worker, step=2)
        def _(i):
            win = first + i
            pair_gates = [gate_v[k, pl.ds(i * W, SC_LANES)] for k in range(TOP_K)]
            fetch(slots[1], win + 1)
            process(slots[0], win, pair_gates, i > 0)

            @pl.when(i + 2 < per_worker)
            def _():
                fetch(slots[0], win + 2)

            process(slots[1], win + 1, pair_gates, i > 0)

        for b in slots:
            write_back(b, first).wait()

    slot_types = ([pltpu.VMEM((TOP_K, per_worker * W), jnp.int32),
                   pltpu.VMEM((TOP_K, per_worker * W), F32)]
                  + [pltpu.VMEM((TOP_K, W, w), jnp.uint32)] * 2
                  + [pltpu.VMEM((W, w), jnp.uint32)] * 2
                  + [pltpu.SemaphoreType.DMA] * 4)
    return pl.kernel(
        body,
        out_type=jax.ShapeDtypeStruct((T, w), jnp.uint32),
        mesh=_sc_mesh(),
        scratch_types=slot_types,
        compiler_params=pltpu.CompilerParams(needs_layout_passes=False),
        name="sc_combine",
    )(ys, dest, gates)


def _experts_kernel(be_ref, nv_ref, io_ref, xs_ref, wg_ref, wu_ref, wd_ref, ys_ref):
    del be_ref, io_ref
    n_valid = nv_ref[pl.program_id(0)]
    mb, w = xs_ref.shape

    @pl.when(n_valid > 0)
    def _():
        keep = lax.broadcasted_iota(jnp.int32, (mb, w), 0) < n_valid
        lo, hi = _unpack_rows(jnp.where(keep, xs_ref[...], jnp.uint32(0)))
        lo, hi = lo.astype(BF16), hi.astype(BF16)

        def proj(w_ref):
            return (jnp.dot(lo, w_ref[0, :w, :], preferred_element_type=F32)
                    + jnp.dot(hi, w_ref[0, w:, :], preferred_element_type=F32))

        hb = jax.nn.silu(proj(wg_ref)) * proj(wu_ref)
        ys_ref[...] = _pack_rows(jnp.dot(hb.astype(BF16), wd_ref[0], preferred_element_type=F32))


def _experts(xs, blk_exp, n_valid, blk_io, weights):
    n_rows, w = xs.shape
    mb = n_rows // blk_exp.shape[0]
    by_expert = lambda shape: pl.BlockSpec((1,) + shape, lambda i, be, nv, io: (be[i], 0, 0))
    w_specs = [by_expert((D_MODEL, EXPERT_DIM)), by_expert((D_MODEL, EXPERT_DIM)),
               by_expert((EXPERT_DIM, D_MODEL))]
    ys_spec = pl.BlockSpec((mb, w), lambda i, be, nv, io: (io[i], 0))
    ys_shape = jax.ShapeDtypeStruct((n_rows, w), jnp.uint32)
    grid_spec = pltpu.PrefetchScalarGridSpec(
        num_scalar_prefetch=3,
        grid=(n_rows // mb,),
        in_specs=[pl.BlockSpec((mb, w), lambda i, be, nv, io: (io[i], 0))] + w_specs,
        out_specs=ys_spec,
    )
    return pl.pallas_call(
        _experts_kernel,
        grid_spec=grid_spec,
        out_shape=ys_shape,
        compiler_params=pltpu.CompilerParams(
            dimension_semantics=("arbitrary",), vmem_limit_bytes=VMEM_MIB["experts"] << 20),
        name="experts",
    )(blk_exp, n_valid, blk_io, xs, *weights)


def _final_kernel(x1_ref, routed_ref, wsg_ref, wsu_ref, wsd_ref, l2g_ref, l2b_ref, out_ref):
    x1 = x1_ref[...]
    xb = x1.astype(BF16)
    hs = (jax.nn.silu(jnp.dot(xb, wsg_ref[...], preferred_element_type=F32))
          * jnp.dot(xb, wsu_ref[...], preferred_element_type=F32))
    shared = jnp.dot(hs.astype(BF16), wsd_ref[...], preferred_element_type=F32)
    routed = jnp.concatenate(_unpack_rows(routed_ref[...]), axis=1)
    out_ref[...] = _layer_norm(ALPHA * x1 + (routed + shared), l2g_ref[...], l2b_ref[...])


def _final(x1, routed, p):
    T = x1.shape[0]
    tm = FINAL_TILE
    tok = lambda w: pl.BlockSpec((tm, w), lambda i: (i, 0))
    return pl.pallas_call(
        _final_kernel,
        grid=(T // tm,),
        in_specs=[
            tok(D_MODEL), tok(D_MODEL // 2),
            _const_spec((D_MODEL, SHARED_DIM)), _const_spec((D_MODEL, SHARED_DIM)),
            _const_spec((SHARED_DIM, D_MODEL)),
            _const_spec((1, D_MODEL)), _const_spec((1, D_MODEL)),
        ],
        out_specs=tok(D_MODEL),
        out_shape=jax.ShapeDtypeStruct((T, D_MODEL), F32),
        compiler_params=pltpu.CompilerParams(
            dimension_semantics=("parallel",), vmem_limit_bytes=VMEM_MIB["final"] << 20),
        name="final",
    )(x1, routed, p["w_sh_gate"], p["w_sh_up"], p["w_sh_down"], p["ln2_g"], p["ln2_b"])


def _routing_layout(route, counts, n_tokens):
    mb = EXPERT_BLOCK
    n_blocks = -(-n_tokens * TOP_K // mb) + N_EXPERTS
    idx = route[0:TOP_K].astype(jnp.int32)
    rank = route[SUBLANES:SUBLANES + TOP_K].astype(jnp.int32)
    counts = counts.astype(jnp.int32)
    padded = (counts + mb - 1) // mb * mb
    pad_end = jnp.cumsum(padded)
    pad_start = pad_end - padded
    experts = jnp.arange(N_EXPERTS, dtype=jnp.int32)
    dest = rank + jnp.sum(jnp.where(idx[:, :, None] == experts, pad_start, 0), axis=-1)
    blk_start = jnp.arange(n_blocks, dtype=jnp.int32) * mb
    blk_exp = jnp.minimum(
        jnp.sum((pad_end[None, :] <= blk_start[:, None]).astype(jnp.int32), axis=1), N_EXPERTS - 1)
    valid_end = jnp.sum(jnp.where(blk_exp[:, None] == experts, pad_start + counts, 0), axis=-1)
    n_valid = jnp.clip(valid_end - blk_start, 0, mb).astype(jnp.int32)
    blk_io = jnp.minimum(jnp.arange(n_blocks, dtype=jnp.int32), pad_end[-1] // mb - 1)
    gates = route[2 * SUBLANES:2 * SUBLANES + TOP_K]
    return dest, gates, blk_exp, n_valid, blk_io, n_blocks * mb


def _encode(x, p, expert_weights):
    batch, seq, _ = x.shape
    T = batch * seq
    xt = x.reshape(T, D_MODEL)
    if expert_weights[0].dtype == BF16:
        (ag, q, gf, gb, iv, so, sgb, gmin), _ = _inproj(xt, p)
    else:
        (ag, q, gf, gb, iv, so, sgb, gmin), expert_weights = _inproj(xt, p, expert_weights)
    of, ob = _hgrn(q, gf, gb, iv, gmin, batch)
    x1, x1p, route, cnt = _mix(xt, of, ob, so, sgb, ag, p)
    dest, gates, blk_exp, n_valid, blk_io, n_rows = _routing_layout(route, cnt[:, 0], T)
    xs = _sc_dispatch(x1p, dest, n_rows)
    ys = _experts(xs, blk_exp, n_valid, blk_io, expert_weights)
    out = _final(x1, _sc_combine(ys, dest, gates), p)
    return out.reshape(batch, seq, D_MODEL), expert_weights


def _prepare_params(ln_in_g, ln_in_b, w_in, a_ln_g, a_ln_b, a_ws, a_sb, hg_lb_logits, hg_norm_g,
                    w_pa, w_pb, w_o, ln1_g, ln1_b, w_router, router_bias, w_sh_gate, w_sh_up,
                    w_sh_down, ln2_g, ln2_b):
    l = 0
    row = lambda v: v.reshape(1, -1).astype(F32)
    ws = a_ws[l].astype(BF16)
    wsp = jnp.concatenate([ws[0::2], ws[1::2]], axis=2)
    sbf = jnp.repeat(a_sb[l].astype(F32), A_WIDTH // A_GROUPS, axis=1)
    lb = jnp.cumsum(jax.nn.softmax(hg_lb_logits.astype(F32), axis=1), axis=1)[:, l]
    return dict(
        ln_in_g=row(ln_in_g), ln_in_b=row(ln_in_b), w_in=w_in[l].astype(BF16),
        a_ln_g=row(a_ln_g[l]), a_ln_b=row(a_ln_b[l]), wsp=wsp, sbf=sbf, lb=lb,
        w_pa=w_pa[l].astype(BF16), hg_norm_g=row(hg_norm_g[l]),
        w_pb=w_pb[l].astype(BF16), w_o=w_o[l].astype(BF16),
        ln1_g=row(ln1_g[l]), ln1_b=row(ln1_b[l]),
        w_router=w_router[l].T.astype(BF16),
        router_bias=jnp.broadcast_to(router_bias[l].astype(F32)[:, None], (N_EXPERTS, LANES)),
        w_sh_gate=w_sh_gate[l].astype(BF16), w_sh_up=w_sh_up[l].astype(BF16),
        w_sh_down=w_sh_down[l].astype(BF16),
        ln2_g=row(ln2_g[l]), ln2_b=row(ln2_b[l]),
    )


def kernel(x_prompt, x_sample, ln_in_g, ln_in_b, w_in, a_ln_g, a_ln_b, a_ws, a_sb, hg_lb_logits,
           hg_norm_g, w_pa, w_pb, w_o, ln1_g, ln1_b, w_router, router_bias, w_e_gate, w_e_up,
           w_e_down, w_sh_gate, w_sh_up, w_sh_down, ln2_g, ln2_b):
    p = _prepare_params(ln_in_g, ln_in_b, w_in, a_ln_g, a_ln_b, a_ws, a_sb, hg_lb_logits, hg_norm_g,
                        w_pa, w_pb, w_o, ln1_g, ln1_b, w_router, router_bias, w_sh_gate, w_sh_up,
                        w_sh_down, ln2_g, ln2_b)
    y_prompt, expert_weights = _encode(x_prompt, p, (w_e_gate[0], w_e_up[0], w_e_down[0]))
    y_sample, _ = _encode(x_sample, p, expert_weights)
    return y_prompt, y_sample
```

```python
import functools

import jax
import jax.numpy as jnp
from jax import lax
from jax.experimental import pallas as pl
from jax.experimental.pallas import tpu as pltpu
from jax.experimental.pallas import tpu_sc as plsc

F32 = jnp.float32
BF16 = jnp.bfloat16

D_MODEL = 1024
A_GROUPS = 8
A_WIDTH = 512
A_CHUNK = 128
HG_HEADS = 8
HG_DK = 128
HG_WIDTH = HG_HEADS * HG_DK
N_IN = 2 * A_WIDTH + 5 * HG_WIDTH + 2 * D_MODEL
N_EXPERTS = 64
TOP_K = 6
N_GROUPS = 8
TOPK_GROUPS = 4
GROUP_SIZE = N_EXPERTS // N_GROUPS
EXPERT_DIM = 256
SHARED_DIM = 256
ROUTED_SCALE = 2.5
DEPTH = 1
ALPHA = (2.0 * DEPTH) ** 0.25
LN_EPS = 1e-5
RMS_EPS = 1e-6

LANES = 128
SUBLANES = 8
ROUTE_ROWS = 3 * SUBLANES
TOKEN_TILE = 512
FINAL_TILE = 1024
INPROJ_TILE = 256
HG_CHUNK = 128
HG_CHUNKS_PER_STEP = 2
HG_SAFE_LOGDECAY = 80.0
EXPERT_BLOCK = 1024
SC_CORES = 2
SC_SUBCORES = 16
SC_LANES = 16
SC_WINDOW = 32
COMBINE_PARTS = 2
SC_COMBINE_WINDOW = 8
VMEM_MIB = dict(inproj=48, hgrn=32, mix=40, experts=32, final=40)

_O_U, _O_V, _O_Q, _O_FF, _O_FB, _O_I, _O_G, _O_GA, _O_GB = (
    0, 512, 1024, 2048, 3072, 4096, 5120, 6144, 7168)


def _layer_norm(x, g, b):
    mu = jnp.mean(x, axis=-1, keepdims=True)
    xc = x - mu
    var = jnp.mean(xc * xc, axis=-1, keepdims=True)
    return xc * lax.rsqrt(var + LN_EPS) * g + b


def _bdot(a, b):
    return jnp.dot(a.astype(BF16), b.astype(BF16), preferred_element_type=F32)


def _pack_rows(x):
    w = x.shape[1] // 2
    lo = lax.bitcast_convert_type(x[:, :w].astype(BF16).astype(F32), jnp.uint32)
    hi = lax.bitcast_convert_type(x[:, w:].astype(BF16).astype(F32), jnp.uint32)
    return hi | (lo >> 16)


def _unpack_rows(p):
    lo = lax.bitcast_convert_type(p << 16, F32)
    hi = lax.bitcast_convert_type(p & jnp.uint32(0xFFFF0000), F32)
    return lo, hi


def _const_spec(shape):
    nd = len(shape)
    return pl.BlockSpec(shape, lambda *_: (0,) * nd, pipeline_mode=pl.Buffered(1))


def _inproj_kernel(x_ref, lng_ref, lnb_ref, win_ref, alng_ref, alnb_ref, wsp_ref, sbf_ref, lb_ref,
                   wpa_ref, *rest):
    if len(rest) > 8:
        for src, dst in zip(rest[:3], rest[11:]):
            dst[...] = src[...].astype(BF16)
        rest = rest[3:11]
    ag_ref, q_ref, gf_ref, gb_ref, iv_ref, so_ref, sgb_ref, gmin_ref = rest
    tm = x_ref.shape[0]
    xb = _layer_norm(x_ref[...], lng_ref[...], lnb_ref[...]).astype(BF16)

    def sec(lo, width):
        return jnp.dot(xb, win_ref[:, lo:lo + width], preferred_element_type=F32)

    v = _layer_norm(jax.nn.gelu(sec(_O_V, A_WIDTH)), alng_ref[...], alnb_ref[...]).astype(BF16)
    u = jax.nn.gelu(sec(_O_U, A_WIDTH))

    q_ref[...] = jax.nn.silu(sec(_O_Q, HG_WIDTH)).astype(BF16)
    mins = []
    for d, (off, g_ref) in enumerate(((_O_FF, gf_ref), (_O_FB, gb_ref))):
        lb = lb_ref[d:d + 1, :]
        f = lb + (1.0 - lb) * jax.nn.sigmoid(sec(off, HG_WIDTH))
        g = jnp.log(f)
        g_ref[...] = g
        half = jnp.sum(g.reshape(tm // (HG_CHUNK // 2), HG_CHUNK // 2, HG_WIDTH), axis=1)
        mins.append(jnp.min(half, axis=-1, keepdims=True))
    gmin_ref[0] = jnp.broadcast_to(jnp.concatenate(mins, axis=0), gmin_ref.shape[1:])
    so_ref[...] = jax.nn.silu(sec(_O_G, HG_WIDTH)).astype(BF16)
    sgb_ref[...] = jax.nn.sigmoid(sec(_O_GB, D_MODEL)).astype(BF16)

    lane = lax.broadcasted_iota(jnp.int32, (A_CHUNK, LANES), 1)
    left = lane < (A_WIDTH // A_GROUPS)
    zero = jnp.zeros((A_CHUNK, LANES), BF16)
    chunks = []
    for c in range(tm // A_CHUNK):
        vc = v[c * A_CHUNK:(c + 1) * A_CHUNK]
        cols = []
        for p in range(A_GROUPS // 2):
            vp = vc[:, p * LANES:(p + 1) * LANES]
            rhs = jnp.concatenate([jnp.where(left, vp, zero), jnp.where(left, zero, vp)], axis=0)
            cols.append(jnp.dot(wsp_ref[p], rhs, preferred_element_type=F32))
        chunks.append(jnp.concatenate(cols, axis=1) + sbf_ref[...])
    mixed = jnp.concatenate(chunks, axis=0)
    a = _bdot(u * mixed, wpa_ref[...])
    ag_ref[...] = (jax.nn.sigmoid(sec(_O_GA, D_MODEL)) * a).astype(BF16)
    iv_ref[...] = sec(_O_I, HG_WIDTH).astype(BF16)


def _inproj(x, p, expert_weights=None):
    T = x.shape[0]
    tm = INPROJ_TILE
    nt = T // tm
    nh = 2 * tm // (HG_CHUNK // 2)
    tok = lambda w: pl.BlockSpec((tm, w), lambda i: (i, 0))
    outs = (
        jax.ShapeDtypeStruct((T, D_MODEL), BF16),
        jax.ShapeDtypeStruct((T, HG_WIDTH), BF16),
        jax.ShapeDtypeStruct((T, HG_WIDTH), F32),
        jax.ShapeDtypeStruct((T, HG_WIDTH), F32),
        jax.ShapeDtypeStruct((T, HG_WIDTH), BF16),
        jax.ShapeDtypeStruct((T, HG_WIDTH), BF16),
        jax.ShapeDtypeStruct((T, D_MODEL), BF16),
        jax.ShapeDtypeStruct((nt, nh, LANES), F32),
    )
    in_specs = [
        tok(D_MODEL),
        _const_spec((1, D_MODEL)), _const_spec((1, D_MODEL)),
        _const_spec((D_MODEL, N_IN)),
        _const_spec((1, A_WIDTH)), _const_spec((1, A_WIDTH)),
        _const_spec((A_GROUPS // 2, A_CHUNK, 2 * A_CHUNK)),
        _const_spec((A_CHUNK, A_WIDTH)),
        _const_spec((2, HG_WIDTH)),
        _const_spec((A_WIDTH, D_MODEL)),
    ]
    out_specs = [tok(D_MODEL), tok(HG_WIDTH), tok(HG_WIDTH), tok(HG_WIDTH), tok(HG_WIDTH),
                 tok(HG_WIDTH), tok(D_MODEL), pl.BlockSpec((1, nh, LANES), lambda i: (i, 0, 0))]
    args = [x, p["ln_in_g"], p["ln_in_b"], p["w_in"], p["a_ln_g"], p["a_ln_b"], p["wsp"], p["sbf"],
            p["lb"], p["w_pa"]]
    if expert_weights is not None:
        slabs = [w.reshape(nt, w.shape[0] * w.shape[1] // nt, w.shape[2]) for w in expert_weights]
        w_specs = [pl.BlockSpec((1,) + w.shape[1:], lambda i: (i, 0, 0)) for w in slabs]
        in_specs += w_specs
        out_specs += w_specs
        outs += tuple(jax.ShapeDtypeStruct(w.shape, BF16) for w in slabs)
        args += slabs
    res = pl.pallas_call(
        _inproj_kernel,
        grid=(nt,),
        in_specs=in_specs,
        out_specs=tuple(out_specs),
        out_shape=outs,
        compiler_params=pltpu.CompilerParams(
            dimension_semantics=("parallel",), vmem_limit_bytes=VMEM_MIB["inproj"] << 20),
        name="inproj",
    )(*args)
    return res[:8], tuple(r.reshape(w.shape) for r, w in zip(res[8:], expert_weights or ()))


_NT = (((1,), (1,)), ((), ()))
_TN = (((0,), (0,)), ((), ()))


def _lockstep(*phased):
    phased = list(phased)
    while phased:
        phased = [g for g in phased if next(g, None) is not None]


def _hgrn_direction(q_ref, g_ref, v_ref, st_ref, o_ref, tri, mask, fwd):
    C = q_ref.shape[0]
    g = g_ref[...]
    ghi = g.astype(BF16)
    glo = (g - ghi.astype(F32)).astype(BF16)
    b = jnp.dot(jnp.concatenate([tri, tri], axis=1), jnp.concatenate([ghi, glo], axis=0),
                preferred_element_type=F32)
    mid = C // 2 - 1 if fwd else C // 2
    end = C - 1 if fwd else 0
    r = b[mid:mid + 1, :]
    b_end = b[end:end + 1, :]
    qt = q_ref[...].astype(F32) * jnp.exp(b - r)
    kt = (1.0 - jnp.exp(g)) * jnp.exp(r - b)
    qtb = qt.astype(BF16)
    ktb = kt.astype(BF16)
    qhb = (qt * jnp.exp(r)).astype(BF16)
    khb = (kt * jnp.exp(b_end - r)).astype(BF16)
    decay = jnp.exp(b_end)
    v = v_ref[...]
    heads =[slice(h * HG_DK, (h + 1) * HG_DK) for h in range(HG_HEADS)]
    zero = jnp.zeros((C, HG_DK), BF16)
    mask2 = jnp.concatenate([mask, mask], axis=1)
    yield True
    scores = []
    for p in range(HG_HEADS // 2):
        k1, k2 = ktb[:, heads[2 * p]], ktb[:, heads[2 * p + 1]]
        kk = jnp.concatenate([jnp.concatenate([k1, zero], axis=1),
                              jnp.concatenate([zero, k2], axis=1)], axis=0)
        s2 = lax.dot_general(qtb[:, 2 * p * HG_DK:(2 * p + 2) * HG_DK], kk, _NT,
                             preferred_element_type=F32)
        s2 = jnp.where(mask2, s2, 0.0).astype(BF16)
        scores += [s2[:, :C], s2[:, C:]]
    yield True
    for h, sl in enumerate(heads):
        st = st_ref[h]
        vt = v[:, sl].T
        o_ref[:, sl] = lax.dot_general(
            jnp.concatenate([scores[h], qhb[:, sl]], axis=1),
            jnp.concatenate([vt, st.astype(BF16)], axis=1), _NT,
            preferred_element_type=F32).astype(o_ref.dtype)
        st_ref[h] = st * decay[:, sl] + jnp.dot(vt, khb[:, sl], preferred_element_type=F32)


def _hgrn_direction_stepwise(q_ref, g_ref, v_ref, st_ref, o_ref, q32_ref, v32_ref, o32_ref, fwd):
    C = q_ref.shape[0]
    q32_ref[...] = q_ref[...].astype(F32)
    v32_ref[...] = v_ref[...].astype(F32)
    sub = SUBLANES
    rows = lax.broadcasted_iota(jnp.int32, (sub, HG_DK), 0)

    def group(i, carry):
        base = pl.multiple_of((i if fwd else C // sub - 1 - i) * sub, sub)
        f = jnp.exp(g_ref[pl.ds(base, sub), :])
        k = 1.0 - f
        q = q32_ref[pl.ds(base, sub), :]
        v = v32_ref[pl.ds(base, sub), :]
        for h in range(HG_HEADS):
            sl = slice(h * HG_DK, (h + 1) * HG_DK)
            st = st_ref[h]
            out = jnp.zeros((sub, HG_DK), F32)
            for r in (range(sub) if fwd else range(sub - 1, -1, -1)):
                v_t = jnp.where(rows == 0, v[r:r + 1, sl], 0.0).astype(BF16)
                k_t = jnp.broadcast_to(k[r:r + 1, sl], (sub, HG_DK)).astype(BF16)
                st = st * f[r:r + 1, sl] + lax.dot_general(v_t, k_t, _TN, preferred_element_type=F32)
                q_t = jnp.broadcast_to(q[r:r + 1, sl], (sub, HG_DK)).astype(BF16)
                o_t = lax.dot_general(q_t, st.astype(BF16), _NT, preferred_element_type=F32)
                out = jnp.where(rows == r, o_t, out)
            st_ref[h] = st
            o32_ref[pl.ds(base, sub), sl] = out
        return carry

    lax.fori_loop(0, C // sub, group, 0)
    o_ref[...] = o32_ref[...].astype(o_ref.dtype)


def _hgrn_kernel(safe_ref, qf_ref, qb_ref, gf_ref, gb_ref, vf_ref, vb_ref, tril_ref, triu_ref,
                 of_ref, ob_ref, sf_ref, sb_ref, q32_ref, v32_ref, o32_ref):
    b, j = pl.program_id(0), pl.program_id(1)
    ns = pl.num_programs(1)
    n = HG_CHUNKS_PER_STEP
    C = qf_ref.shape[0] // n

    @pl.when(j == 0)
    def _():
        sf_ref[...] = jnp.zeros_like(sf_ref)
        sb_ref[...] = jnp.zeros_like(sb_ref)

    row = lax.broadcasted_iota(jnp.int32, (C, C), 0)
    col = lax.broadcasted_iota(jnp.int32, (C, C), 1)
    safe_f = [safe_ref[0, (b * ns + j) * n + u] != 0 for u in range(n)]
    safe_b = [safe_ref[1, (b * ns + ns - 1 - j) * n + u] != 0 for u in range(n)]
    part = lambda ref, u: ref.at[pl.ds(u * C, C), :]

    def forward(u, stepwise):
        refs = (part(qf_ref, u), part(gf_ref, u), part(vf_ref, u), sf_ref, part(of_ref, u))
        if stepwise:
            _hgrn_direction_stepwise(*refs, q32_ref, v32_ref, o32_ref, True)
            return iter(())
        return _hgrn_direction(*refs, tril_ref[...], row >= col, True)

    def backward(u, stepwise):
        refs = (part(qb_ref, u), part(gb_ref, u), part(vb_ref, u), sb_ref, part(ob_ref, u))
        if stepwise:
            _hgrn_direction_stepwise(*refs, q32_ref, v32_ref, o32_ref, False)
            return iter(())
        return _hgrn_direction(*refs, triu_ref[...], row <= col, False)

    all_safe = functools.reduce(jnp.logical_and, safe_f + safe_b)

    @pl.when(all_safe)
    def _():
        _lockstep(*[d for u in range(n) for d in (forward(u, False), backward(n - 1 - u, False))])

    @pl.when(jnp.logical_not(all_safe))
    def _():
        run = lambda direction, u, stepwise: lambda: _lockstep(direction(u, stepwise))
        for u in range(n):
            pl.when(safe_f[u])(run(forward, u, False))
            pl.when(jnp.logical_not(safe_f[u]))(run(forward, u, True))
        for u in reversed(range(n)):
            pl.when(safe_b[u])(run(backward, u, False))
            pl.when(jnp.logical_not(safe_b[u]))(run(backward, u, True))


def _hgrn(q, gf, gb, iv, gmin, batch):
    T = q.shape[0]
    C = HG_CHUNK
    nc = T // batch // C
    nt, nh, _ = gmin.shape
    halves = gmin[:, :, 0].reshape(nt, 2, nh // 4, 2)
    safe = (jnp.min(halves, axis=-1) > -HG_SAFE_LOGDECAY).astype(jnp.int32)
    safe = safe.transpose(1, 0, 2).reshape(2, T // C)
    n = HG_CHUNKS_PER_STEP
    ns = nc // n
    fwd = pl.BlockSpec((n * C, HG_WIDTH), lambda b, j, s: (b * ns + j, 0))
    bwd = pl.BlockSpec((n * C, HG_WIDTH), lambda b, j, s: (b * ns + ns - 1 - j, 0))
    const = lambda shape: pl.BlockSpec(shape, lambda b, j, s: (0,) * len(shape),
                                       pipeline_mode=pl.Buffered(1))
    row = lax.broadcasted_iota(jnp.int32, (C, C), 0)
    col = lax.broadcasted_iota(jnp.int32, (C, C), 1)
    tril = (row >= col).astype(BF16)
    triu = (row <= col).astype(BF16)
    grid_spec = pltpu.PrefetchScalarGridSpec(
        num_scalar_prefetch=1,
        grid=(batch, ns),
        in_specs=[fwd, bwd, fwd, bwd, fwd, bwd, const((C, C)), const((C, C))],
        out_specs=(fwd, bwd),
        scratch_shapes=[pltpu.VMEM((HG_HEADS, HG_DK, HG_DK), F32),
                        pltpu.VMEM((HG_HEADS, HG_DK, HG_DK), F32),
                        pltpu.VMEM((C, HG_WIDTH), F32), pltpu.VMEM((C, HG_WIDTH), F32),
                        pltpu.VMEM((C, HG_WIDTH), F32)],
    )
    return pl.pallas_call(
        _hgrn_kernel,
        grid_spec=grid_spec,
        out_shape=(jax.ShapeDtypeStruct((T, HG_WIDTH), BF16), jax.ShapeDtypeStruct((T, HG_WIDTH), BF16)),
        compiler_params=pltpu.CompilerParams(
            dimension_semantics=("parallel", "arbitrary"), vmem_limit_bytes=VMEM_MIB["hgrn"] << 20),
        name="hgrn",
    )(safe, q, q, gf, gb, iv, iv, tril, triu)


def _mix_kernel(x_ref, lng_ref, lnb_ref, of_ref, ob_ref, so_ref, sgb_ref, ag_ref, ng_ref, wpb_ref,
                wo_ref, l1g_ref, l1b_ref, wr_ref, rb_ref, tri_ref, x1_ref, x1p_ref, route_ref,
                cnt_ref, carry_ref):
    tm = x_ref.shape[0]

    @pl.when(pl.program_id(0) == 0)
    def _():
        carry_ref[...] = jnp.zeros_like(carry_ref)

    o = of_ref[...].astype(F32) + ob_ref[...].astype(F32)
    heads = []
    for h in range(HG_HEADS):
        oh = o[:, h * HG_DK:(h + 1) * HG_DK]
        heads.append(oh * lax.rsqrt(jnp.mean(oh * oh, axis=-1, keepdims=True) + RMS_EPS))
    rn = jnp.concatenate(heads, axis=1) * ng_ref[...] * so_ref[...].astype(F32)
    r = _bdot(rn, wpb_ref[...])
    mixed = ag_ref[...].astype(F32) + sgb_ref[...].astype(F32) * r
    y = _bdot(mixed, wo_ref[...])
    xn = _layer_norm(x_ref[...], lng_ref[...], lnb_ref[...])
    x1 = _layer_norm(ALPHA * xn + y, l1g_ref[...], l1b_ref[...])
    x1_ref[...] = x1
    x1p_ref[...] = _pack_rows(x1)

    neg = jnp.float32(-jnp.inf)
    reps = tm // LANES
    scores = jax.nn.sigmoid(lax.dot_general(wr_ref[...], x1.astype(BF16), _NT,
                                            preferred_element_type=F32))
    biased = (scores + jnp.concatenate([rb_ref[...]] * reps, axis=1)).reshape(
        N_GROUPS, GROUP_SIZE, tm)
    sub = lax.broadcasted_iota(jnp.int32, biased.shape, 1).astype(F32)
    m1 = jnp.max(biased, axis=1, keepdims=True)
    first = jnp.min(jnp.where(biased == m1, sub, float(GROUP_SIZE)), axis=1, keepdims=True)
    m2 = jnp.max(jnp.where(sub == first, neg, biased), axis=1, keepdims=True)
    gs = (m1 + m2).reshape(N_GROUPS, tm)
    grp = lax.broadcasted_iota(jnp.int32, (N_GROUPS, tm), 0)
    ahead = jnp.zeros((N_GROUPS, tm), F32)
    for d in range(1, N_GROUPS):
        other = pltpu.roll(gs, d, 0)
        tie = jnp.where(grp >= d, 1.0, 0.0)
        ahead = ahead + jnp.where(other > gs, 1.0, jnp.where(other == gs, tie, 0.0))
    keep = (ahead < TOPK_GROUPS).reshape(N_GROUPS, 1, tm)
    allowed = jnp.where(keep, biased, neg).reshape(N_EXPERTS, tm)
    row = lax.broadcasted_iota(jnp.int32, (N_EXPERTS, tm), 0).astype(F32)
    sel = jnp.zeros((N_EXPERTS, tm), F32)
    picks = []
    for _ in range(TOP_K):
        m = jnp.max(allowed, axis=0, keepdims=True)
        first = jnp.min(jnp.where(allowed == m, row, float(N_EXPERTS)), axis=0, keepdims=True)
        hit = row == first
        picks.append((first, hit, jnp.sum(jnp.where(hit, scores, 0.0), axis=0, keepdims=True)))
        sel = jnp.where(hit, 1.0, sel)
        allowed = jnp.where(hit, neg, allowed)
    wsum = picks[0][2]
    for pk in picks[1:]:
        wsum = wsum + pk[2]
    selb = sel.astype(BF16)
    carry = carry_ref[...]
    before = (jnp.dot(selb, tri_ref[...], preferred_element_type=F32)
              + jnp.concatenate([carry] * reps, axis=1))
    total = carry + jnp.dot(selb, jnp.ones((tm, LANES), BF16), preferred_element_type=F32)
    carry_ref[...] = total
    cnt_ref[...] = total
    blank = [jnp.zeros((1, tm), F32)] * (SUBLANES - TOP_K)
    route_ref[...] = jnp.concatenate(
        [pk[0] for pk in picks] + blank
        + [jnp.sum(jnp.where(pk[1], before, 0.0), axis=0, keepdims=True) for pk in picks] + blank
        + [pk[2] / wsum * ROUTED_SCALE for pk in picks] + blank, axis=0)


def _mix(x, of, ob, so, sgb, ag, p):
    T = x.shape[0]
    tm = TOKEN_TILE
    tok = lambda w: pl.BlockSpec((tm, w), lambda i: (i, 0))
    row = lax.broadcasted_iota(jnp.int32, (tm, tm), 0)
    col = lax.broadcasted_iota(jnp.int32, (tm, tm), 1)
    tri = (row < col).astype(BF16)
    return pl.pallas_call(
        _mix_kernel,
        grid=(T // tm,),
        in_specs=[
            tok(D_MODEL), _const_spec((1, D_MODEL)), _const_spec((1, D_MODEL)),
            tok(HG_WIDTH), tok(HG_WIDTH), tok(HG_WIDTH), tok(D_MODEL), tok(D_MODEL),
            _const_spec((1, HG_WIDTH)),
            _const_spec((HG_WIDTH, D_MODEL)), _const_spec((D_MODEL, D_MODEL)),
            _const_spec((1, D_MODEL)), _const_spec((1, D_MODEL)),
            _const_spec((N_EXPERTS, D_MODEL)), _const_spec((N_EXPERTS, LANES)),
            _const_spec((tm, tm)),
        ],
        out_specs=(tok(D_MODEL), tok(D_MODEL // 2), pl.BlockSpec((ROUTE_ROWS, tm), lambda i: (0, i)),
                   pl.BlockSpec((N_EXPERTS, LANES), lambda i: (0, 0))),
        out_shape=(jax.ShapeDtypeStruct((T, D_MODEL), F32),
                   jax.ShapeDtypeStruct((T, D_MODEL // 2), jnp.uint32),
                   jax.ShapeDtypeStruct((ROUTE_ROWS, T), F32),
                   jax.ShapeDtypeStruct((N_EXPERTS, LANES), F32)),
        scratch_shapes=[pltpu.VMEM((N_EXPERTS, LANES), F32)],
        compiler_params=pltpu.CompilerParams(
            dimension_semantics=("arbitrary",), vmem_limit_bytes=VMEM_MIB["mix"] << 20),
        name="mix",
    )(x, p["ln_in_g"], p["ln_in_b"], of, ob, so, sgb, ag, p["hg_norm_g"], p["w_pb"], p["w_o"],
      p["ln1_g"], p["ln1_b"], p["w_router"], p["router_bias"], tri)


def _sc_mesh():
    return plsc.VectorSubcoreMesh(core_axis_name="c", subcore_axis_name="s",
                                  num_cores=SC_CORES, num_subcores=SC_SUBCORES)


def _sc_worker():
    return lax.axis_index("s") * SC_CORES + lax.axis_index("c")


def _sc_dispatch(x1p, dest, n_rows):
    T, w = x1p.shape
    W = SC_WINDOW
    per_worker = T // W // (SC_CORES * SC_SUBCORES)

    def body(x_hbm, d_hbm, o_hbm, rows_v, idx_v, sem):
        first = _sc_worker() * per_worker
        for k in range(TOP_K):
            pltpu.sync_copy(d_hbm.at[k, pl.ds(first, per_worker)], idx_v.at[k])

        @pl.loop(0, per_worker)
        def _(j):
            pltpu.sync_copy(x_hbm.at[pl.ds((first + j) * W, W)], rows_v)
            copies = [pltpu.async_copy(rows_v, o_hbm.at[idx_v.at[k, j]], sem) for k in range(TOP_K)]
            for c in copies:
                c.wait()

    return pl.kernel(
        body,
        out_type=jax.ShapeDtypeStruct((n_rows, w), jnp.uint32),
        mesh=_sc_mesh(),
        scratch_types=[pltpu.VMEM((W, w), jnp.uint32), pltpu.VMEM((TOP_K, per_worker, W), jnp.int32),
                       pltpu.SemaphoreType.DMA],
        name="sc_dispatch",
    )(x1p, dest.reshape(TOP_K, T // W, W))


def _sc_combine(ys, dest, gates):
    T = dest.shape[1]
    W = SC_COMBINE_WINDOW
    w = ys.shape[1]
    per_worker = T // W // (SC_CORES * SC_SUBCORES)
    assert per_worker % 2 == 0 and per_worker * W * SC_CORES * SC_SUBCORES == T
    assert 2 * W == SC_LANES

    def body(y_hbm, d_hbm, g_hbm, o_hbm, idx_v, gate_v, *scratch):
        slots = [dict(rows=scratch[s], out=scratch[2 + s], gsem=scratch[4 + s], wsem=scratch[6 + s],
                      lane0=s * W) for s in range(2)]
        first = _sc_worker() * per_worker
        for k in range(TOP_K):
            pltpu.sync_copy(d_hbm.at[k, pl.ds(first * W, per_worker * W)], idx_v.at[k])
            pltpu.sync_copy(g_hbm.at[k, pl.ds(first * W, per_worker * W)], gate_v.at[k])

        def loads(b, win):
            i = win - first
            return [pltpu.make_async_copy(y_hbm.at[idx_v.at[k, pl.ds(i * W, W)]],
                                          b["rows"].at[k], b["gsem"]) for k in range(TOP_K)]

        def write_back(b, win):
            return pltpu.make_async_copy(b["out"], o_hbm.at[pl.ds(win * W, W)], b["wsem"])

        def fetch(b, win):
            for c in loads(b, win):
                c.start()

        def rounded(x):
            u = plsc.bitcast(x, jnp.uint32)
            return u + (jnp.uint32(0x7FFF) + ((u >> 16) & jnp.uint32(1)))

        def reduce_rows(b, pair_gates):
            rows, out = b["rows"], b["out"]
            lane = lax.iota(jnp.int32, SC_LANES)

            @pl.loop(0, W)
            def _(j):
                mine = lane == b["lane0"] + j
                gate = [jnp.broadcast_to(jnp.sum(jnp.where(mine, g, 0.0)), (SC_LANES,))
                        for g in pair_gates]

                @plsc.parallel_loop(0, w, step=SC_LANES, unroll=4)
                def _(col):
                    lo, hi = [], []
                    for k in range(TOP_K):
                        p = rows[k, j, pl.ds(col, SC_LANES)]
                        lo.append(plsc.bitcast(p << 16, F32) * gate[k])
                        hi.append(plsc.bitcast(p & jnp.uint32(0xFFFF0000), F32) * gate[k])
                    lo = rounded((lo[0] + lo[1]) + (lo[2] + lo[3]) + (lo[4] + lo[5]))
                    hi = rounded((hi[0] + hi[1]) + (hi[2] + hi[3]) + (hi[4] + hi[5]))
                    out[j, pl.ds(col, SC_LANES)] = (hi & jnp.uint32(0xFFFF0000)) | (lo >> 16)

        def process(b, win, pair_gates, not_first):
            for c in loads(b, win):
                c.wait()

            @pl.when(not_first)
            def _():
                write_back(b, win).wait()

            reduce_rows(b, pair_gates)
            write_back(b, win).start()

        fetch(slots[0], first)

        @pl.loop(0, per_worker, step=2)
        def _(i):
            win = first + i
            pair_gates = [gate_v[k, pl.ds(i * W, SC_LANES)] for k in range(TOP_K)]
            fetch(slots[1], win + 1)
            process(slots[0], win, pair_gates, i > 0)

            @pl.when(i + 2 < per_worker)
            def _():
                fetch(slots[0], win + 2)

            process(slots[1], win + 1, pair_gates, i > 0)

        for b in slots:
            write_back(b, first).wait()

    slot_types = ([pltpu.VMEM((TOP_K, per_worker * W), jnp.int32),
                   pltpu.VMEM((TOP_K, per_worker * W), F32)]
                  + [pltpu.VMEM((TOP_K, W, w), jnp.uint32)] * 2
                  + [pltpu.VMEM((W, w), jnp.uint32)] * 2
                  + [pltpu.SemaphoreType.DMA] * 4)
    return pl.kernel(
        body,
        out_type=jax.ShapeDtypeStruct((T, w), jnp.uint32),
        mesh=_sc_mesh(),
        scratch_types=slot_types,
        compiler_params=pltpu.CompilerParams(needs_layout_passes=False),
        name="sc_combine",
    )(ys, dest, gates)


def _experts_kernel(be_ref, nv_ref, io_ref, xs_ref, wg_ref, wu_ref, wd_ref, ys_ref):
    del be_ref, io_ref
    n_valid = nv_ref[pl.program_id(0)]
    mb, w = xs_ref.shape

    @pl.when(n_valid > 0)
    def _():
        keep = lax.broadcasted_iota(jnp.int32, (mb, w), 0) < n_valid
        lo, hi = _unpack_rows(jnp.where(keep, xs_ref[...], jnp.uint32(0)))
        lo, hi = lo.astype(BF16), hi.astype(BF16)

        def proj(w_ref):
            return (jnp.dot(lo, w_ref[0, :w, :], preferred_element_type=F32)
                    + jnp.dot(hi, w_ref[0, w:, :], preferred_element_type=F32))

        hb = jax.nn.silu(proj(wg_ref)) * proj(wu_ref)
        ys_ref[...] = _pack_rows(jnp.dot(hb.astype(BF16), wd_ref[0], preferred_element_type=F32))


def _experts(xs, blk_exp, n_valid, blk_io, weights):
    n_rows, w = xs.shape
    mb = n_rows // blk_exp.shape[0]
    by_expert = lambda shape: pl.BlockSpec((1,) + shape, lambda i, be, nv, io: (be[i], 0, 0))
    w_specs = [by_expert((D_MODEL, EXPERT_DIM)), by_expert((D_MODEL, EXPERT_DIM)),
               by_expert((EXPERT_DIM, D_MODEL))]
    ys_spec = pl.BlockSpec((mb, w), lambda i, be, nv, io: (io[i], 0))
    ys_shape = jax.ShapeDtypeStruct((n_rows, w), jnp.uint32)
    grid_spec = pltpu.PrefetchScalarGridSpec(
        num_scalar_prefetch=3,
        grid=(n_rows // mb,),
        in_specs=[pl.BlockSpec((mb, w), lambda i, be, nv, io: (io[i], 0))] + w_specs,
        out_specs=ys_spec,
    )
    return pl.pallas_call(
        _experts_kernel,
        grid_spec=grid_spec,
        out_shape=ys_shape,
        compiler_params=pltpu.CompilerParams(
            dimension_semantics=("arbitrary",), vmem_limit_bytes=VMEM_MIB["experts"] << 20),
        name="experts",
    )(blk_exp, n_valid, blk_io, xs, *weights)


def _final_kernel(x1_ref, routed_ref, wsg_ref, wsu_ref, wsd_ref, l2g_ref, l2b_ref, *rest):
    out_ref = rest[-1]
    x1 = x1_ref[...]
    xb = x1.astype(BF16)
    hs = (jax.nn.silu(jnp.dot(xb, wsg_ref[...], preferred_element_type=F32))
          * jnp.dot(xb, wsu_ref[...], preferred_element_type=F32))
    shared = jnp.dot(hs.astype(BF16), wsd_ref[...], preferred_element_type=F32)
    routed = jnp.concatenate(_unpack_rows(routed_ref[...]), axis=1)
    out_ref[...] = _layer_norm(ALPHA * x1 + (routed + shared), l2g_ref[...], l2b_ref[...])


def _final(x1, routed, p, first_row, out=None):
    T = x1.shape[0]
    tm = FINAL_TILE
    first = first_row // tm
    here = pl.BlockSpec((tm, D_MODEL), lambda i: (first + i, 0))
    in_specs = [
        here, pl.BlockSpec((tm, D_MODEL // 2), lambda i: (i, 0)),
        _const_spec((D_MODEL, SHARED_DIM)), _const_spec((D_MODEL, SHARED_DIM)),
        _const_spec((SHARED_DIM, D_MODEL)),
        _const_spec((1, D_MODEL)), _const_spec((1, D_MODEL)),
    ]
    args = [x1, routed, p["w_sh_gate"], p["w_sh_up"], p["w_sh_down"], p["ln2_g"], p["ln2_b"]]
    if out is not None:
        in_specs.append(pl.BlockSpec(memory_space=pl.ANY))
        args.append(out)
    return pl.pallas_call(
        _final_kernel,
        grid=(routed.shape[0] // tm,),
        in_specs=in_specs,
        out_specs=here,
        out_shape=jax.ShapeDtypeStruct((T, D_MODEL), F32),
        input_output_aliases={} if out is None else {len(args) - 1: 0},
        compiler_params=pltpu.CompilerParams(
            dimension_semantics=("parallel",), vmem_limit_bytes=VMEM_MIB["final"] << 20),
        name="final",
    )(*args)


def _routing_layout(route, counts, n_tokens):
    mb = EXPERT_BLOCK
    n_blocks = -(-n_tokens * TOP_K // mb) + N_EXPERTS
    idx = route[0:TOP_K].astype(jnp.int32)
    rank = route[SUBLANES:SUBLANES + TOP_K].astype(jnp.int32)
    counts = counts.astype(jnp.int32)
    padded = (counts + mb - 1) // mb * mb
    pad_end = jnp.cumsum(padded)
    pad_start = pad_end - padded
    experts = jnp.arange(N_EXPERTS, dtype=jnp.int32)
    dest = rank + jnp.sum(jnp.where(idx[:, :, None] == experts, pad_start, 0), axis=-1)
    blk_start = jnp.arange(n_blocks, dtype=jnp.int32) * mb
    blk_exp = jnp.minimum(
        jnp.sum((pad_end[None, :] <= blk_start[:, None]).astype(jnp.int32), axis=1), N_EXPERTS - 1)
    valid_end = jnp.sum(jnp.where(blk_exp[:, None] == experts, pad_start + counts, 0), axis=-1)
    n_valid = jnp.clip(valid_end - blk_start, 0, mb).astype(jnp.int32)
    blk_io = jnp.minimum(jnp.arange(n_blocks, dtype=jnp.int32), pad_end[-1] // mb - 1)
    gates = route[2 * SUBLANES:2 * SUBLANES + TOP_K]
    return dest, gates, blk_exp, n_valid, blk_io, n_blocks * mb


def _encode(x, p, expert_weights):
    batch, seq, _ = x.shape
    T = batch * seq
    xt = x.reshape(T, D_MODEL)
    if expert_weights[0].dtype == BF16:
        (ag, q, gf, gb, iv, so, sgb, gmin), _ = _inproj(xt, p)
    else:
        (ag, q, gf, gb, iv, so, sgb, gmin), expert_weights = _inproj(xt, p, expert_weights)
    of, ob = _hgrn(q, gf, gb, iv, gmin, batch)
    x1, x1p, route, cnt = _mix(xt, of, ob, so, sgb, ag, p)
    dest, gates, blk_exp, n_valid, blk_io, n_rows = _routing_layout(route, cnt[:, 0], T)
    xs = _sc_dispatch(x1p, dest, n_rows)
    ys = _experts(xs, blk_exp, n_valid, blk_io, expert_weights)
    out = None
    for lo in range(0, T, T // COMBINE_PARTS):
        part = slice(lo, lo + T // COMBINE_PARTS)
        out = _final(x1, _sc_combine(ys, dest[:, part], gates[:, part]), p, lo, out)
    return out.reshape(batch, seq, D_MODEL), expert_weights


def _prepare_params(ln_in_g, ln_in_b, w_in, a_ln_g, a_ln_b, a_ws, a_sb, hg_lb_logits, hg_norm_g,
                    w_pa, w_pb, w_o, ln1_g, ln1_b, w_router, router_bias, w_sh_gate, w_sh_up,
                    w_sh_down, ln2_g, ln2_b):
    l = 0
    row = lambda v: v.reshape(1, -1).astype(F32)
    ws = a_ws[l].astype(BF16)
    wsp = jnp.concatenate([ws[0::2], ws[1::2]], axis=2)
    sbf = jnp.repeat(a_sb[l].astype(F32), A_WIDTH // A_GROUPS, axis=1)
    lb = jnp.cumsum(jax.nn.softmax(hg_lb_logits.astype(F32), axis=1), axis=1)[:, l]
    return dict(
        ln_in_g=row(ln_in_g), ln_in_b=row(ln_in_b), w_in=w_in[l].astype(BF16),
        a_ln_g=row(a_ln_g[l]), a_ln_b=row(a_ln_b[l]), wsp=wsp, sbf=sbf, lb=lb,
        w_pa=w_pa[l].astype(BF16), hg_norm_g=row(hg_norm_g[l]),
        w_pb=w_pb[l].astype(BF16), w_o=w_o[l].astype(BF16),
        ln1_g=row(ln1_g[l]), ln1_b=row(ln1_b[l]),
        w_router=w_router[l].T.astype(BF16),
        router_bias=jnp.broadcast_to(router_bias[l].astype(F32)[:, None], (N_EXPERTS, LANES)),
        w_sh_gate=w_sh_gate[l].astype(BF16), w_sh_up=w_sh_up[l].astype(BF16),
        w_sh_down=w_sh_down[l].astype(BF16),
        ln2_g=row(ln2_g[l]), ln2_b=row(ln2_b[l]),
    )


def kernel(x_prompt, x_sample, ln_in_g, ln_in_b, w_in, a_ln_g, a_ln_b, a_ws, a_sb, hg_lb_logits,
           hg_norm_g, w_pa, w_pb, w_o, ln1_g, ln1_b, w_router, router_bias, w_e_gate, w_e_up,
           w_e_down, w_sh_gate, w_sh_up, w_sh_down, ln2_g, ln2_b):
    p = _prepare_params(ln_in_g, ln_in_b, w_in, a_ln_g, a_ln_b, a_ws, a_sb, hg_lb_logits, hg_norm_g,
                        w_pa, w_pb, w_o, ln1_g, ln1_b, w_router, router_bias, w_sh_gate, w_sh_up,
                        w_sh_down, ln2_g, ln2_b)
    y_prompt, expert_weights = _encode(x_prompt, p, (w_e_gate[0], w_e_up[0], w_e_down[0]))
    y_sample, _ = _encode(x_sample, p, expert_weights)
    return y_prompt, y_sample
```

```python
import functools

import jax
import jax.numpy as jnp
from jax import lax
from jax.experimental import pallas as pl
from jax.experimental.pallas import tpu as pltpu
from jax.experimental.pallas import tpu_sc as plsc

F32 = jnp.float32
BF16 = jnp.bfloat16

D_MODEL = 1024
A_GROUPS = 8
A_WIDTH = 512
A_CHUNK = 128
HG_HEADS = 8
HG_DK = 128
HG_WIDTH = HG_HEADS * HG_DK
N_IN = 2 * A_WIDTH + 5 * HG_WIDTH + 2 * D_MODEL
N_EXPERTS = 64
TOP_K = 6
N_GROUPS = 8
TOPK_GROUPS = 4
GROUP_SIZE = N_EXPERTS // N_GROUPS
EXPERT_DIM = 256
SHARED_DIM = 256
ROUTED_SCALE = 2.5
DEPTH = 1
ALPHA = (2.0 * DEPTH) ** 0.25
LN_EPS = 1e-5
RMS_EPS = 1e-6

LANES = 128
SUBLANES = 8
ROUTE_ROWS = 3 * SUBLANES
TOKEN_TILE = 512
FINAL_TILE = 1024
INPROJ_TILE = 256
HG_CHUNK = 128
HG_CHUNKS_PER_STEP = 2
HG_SAFE_LOGDECAY = 80.0
EXPERT_BLOCK = 1024
SC_CORES = 2
SC_SUBCORES = 16
SC_LANES = 16
SC_WINDOW = 32
SC_COMBINE_WINDOW = 8
VMEM_MIB = dict(inproj=48, hgrn=32, mix=40, experts=32, final=40)

_O_U, _O_V, _O_Q, _O_FF, _O_FB, _O_I, _O_G, _O_GA, _O_GB = (
    0, 512, 1024, 2048, 3072, 4096, 5120, 6144, 7168)


def _layer_norm(x, g, b):
    mu = jnp.mean(x, axis=-1, keepdims=True)
    xc = x - mu
    var = jnp.mean(xc * xc, axis=-1, keepdims=True)
    return xc * lax.rsqrt(var + LN_EPS) * g + b


def _bdot(a, b):
    return jnp.dot(a.astype(BF16), b.astype(BF16), preferred_element_type=F32)


def _pack_rows(x):
    w = x.shape[1] // 2
    lo = lax.bitcast_convert_type(x[:, :w].astype(BF16).astype(F32), jnp.uint32)
    hi = lax.bitcast_convert_type(x[:, w:].astype(BF16).astype(F32), jnp.uint32)
    return hi | (lo >> 16)


def _unpack_rows(p):
    lo = lax.bitcast_convert_type(p << 16, F32)
    hi = lax.bitcast_convert_type(p & jnp.uint32(0xFFFF0000), F32)
    return lo, hi


def _const_spec(shape):
    nd = len(shape)
    return pl.BlockSpec(shape, lambda *_: (0,) * nd, pipeline_mode=pl.Buffered(1))


def _inproj_kernel(x_ref, lng_ref, lnb_ref, win_ref, alng_ref, alnb_ref, wsp_ref, sbf_ref, lb_ref,
                   wpa_ref, *rest):
    if len(rest) > 8:
        for src, dst in zip(rest[:3], rest[11:]):
            dst[...] = src[...].astype(BF16)
        rest = rest[3:11]
    ag_ref, q_ref, gf_ref, gb_ref, iv_ref, so_ref, sgb_ref, gmin_ref = rest
    tm = x_ref.shape[0]
    xb = _layer_norm(x_ref[...], lng_ref[...], lnb_ref[...]).astype(BF16)

    def sec(lo, width):
        return jnp.dot(xb, win_ref[:, lo:lo + width], preferred_element_type=F32)

    v = _layer_norm(jax.nn.gelu(sec(_O_V, A_WIDTH)), alng_ref[...], alnb_ref[...]).astype(BF16)
    u = jax.nn.gelu(sec(_O_U, A_WIDTH))

    q_ref[...] = jax.nn.silu(sec(_O_Q, HG_WIDTH)).astype(BF16)
    mins = []
    for d, (off, g_ref) in enumerate(((_O_FF, gf_ref), (_O_FB, gb_ref))):
        lb = lb_ref[d:d + 1, :]
        f = lb + (1.0 - lb) * jax.nn.sigmoid(sec(off, HG_WIDTH))
        g = jnp.log(f)
        g_ref[...] = g
        half = jnp.sum(g.reshape(tm // (HG_CHUNK // 2), HG_CHUNK // 2, HG_WIDTH), axis=1)
        mins.append(jnp.min(half, axis=-1, keepdims=True))
    gmin_ref[0] = jnp.broadcast_to(jnp.concatenate(mins, axis=0), gmin_ref.shape[1:])
    so_ref[...] = jax.nn.silu(sec(_O_G, HG_WIDTH)).astype(BF16)
    sgb_ref[...] = jax.nn.sigmoid(sec(_O_GB, D_MODEL)).astype(BF16)

    lane = lax.broadcasted_iota(jnp.int32, (A_CHUNK, LANES), 1)
    left = lane < (A_WIDTH // A_GROUPS)
    zero = jnp.zeros((A_CHUNK, LANES), BF16)
    chunks = []
    for c in range(tm // A_CHUNK):
        vc = v[c * A_CHUNK:(c + 1) * A_CHUNK]
        cols = []
        for p in range(A_GROUPS // 2):
            vp = vc[:, p * LANES:(p + 1) * LANES]
            rhs = jnp.concatenate([jnp.where(left, vp, zero), jnp.where(left, zero, vp)], axis=0)
            cols.append(jnp.dot(wsp_ref[p], rhs, preferred_element_type=F32))
        chunks.append(jnp.concatenate(cols, axis=1) + sbf_ref[...])
    mixed = jnp.concatenate(chunks, axis=0)
    a = _bdot(u * mixed, wpa_ref[...])
    ag_ref[...] = (jax.nn.sigmoid(sec(_O_GA, D_MODEL)) * a).astype(BF16)
    iv_ref[...] = sec(_O_I, HG_WIDTH).astype(BF16)


def _inproj(x, p, expert_weights=None):
    T = x.shape[0]
    tm = INPROJ_TILE
    nt = T // tm
    nh = 2 * tm // (HG_CHUNK // 2)
    tok = lambda w: pl.BlockSpec((tm, w), lambda i: (i, 0))
    outs = (
        jax.ShapeDtypeStruct((T, D_MODEL), BF16),
        jax.ShapeDtypeStruct((T, HG_WIDTH), BF16),
        jax.ShapeDtypeStruct((T, HG_WIDTH), F32),
        jax.ShapeDtypeStruct((T, HG_WIDTH), F32),
        jax.ShapeDtypeStruct((T, HG_WIDTH), BF16),
        jax.ShapeDtypeStruct((T, HG_WIDTH), BF16),
        jax.ShapeDtypeStruct((T, D_MODEL), BF16),
        jax.ShapeDtypeStruct((nt, nh, LANES), F32),
    )
    in_specs = [
        tok(D_MODEL),
        _const_spec((1, D_MODEL)), _const_spec((1, D_MODEL)),
        _const_spec((D_MODEL, N_IN)),
        _const_spec((1, A_WIDTH)), _const_spec((1, A_WIDTH)),
        _const_spec((A_GROUPS // 2, A_CHUNK, 2 * A_CHUNK)),
        _const_spec((A_CHUNK, A_WIDTH)),
        _const_spec((2, HG_WIDTH)),
        _const_spec((A_WIDTH, D_MODEL)),
    ]
    out_specs = [tok(D_MODEL), tok(HG_WIDTH), tok(HG_WIDTH), tok(HG_WIDTH), tok(HG_WIDTH),
                 tok(HG_WIDTH), tok(D_MODEL), pl.BlockSpec((1, nh, LANES), lambda i: (i, 0, 0))]
    args = [x, p["ln_in_g"], p["ln_in_b"], p["w_in"], p["a_ln_g"], p["a_ln_b"], p["wsp"], p["sbf"],
            p["lb"], p["w_pa"]]
    if expert_weights is not None:
        slabs = [w.reshape(nt, w.shape[0] * w.shape[1] // nt, w.shape[2]) for w in expert_weights]
        w_specs = [pl.BlockSpec((1,) + w.shape[1:], lambda i: (i, 0, 0)) for w in slabs]
        in_specs += w_specs
        out_specs += w_specs
        outs += tuple(jax.ShapeDtypeStruct(w.shape, BF16) for w in slabs)
        args += slabs
    res = pl.pallas_call(
        _inproj_kernel,
        grid=(nt,),
        in_specs=in_specs,
        out_specs=tuple(out_specs),
        out_shape=outs,
        compiler_params=pltpu.CompilerParams(
            dimension_semantics=("parallel",), vmem_limit_bytes=VMEM_MIB["inproj"] << 20),
        name="inproj",
    )(*args)
    return res[:8], tuple(r.reshape(w.shape) for r, w in zip(res[8:], expert_weights or ()))


_NT = (((1,), (1,)), ((), ()))
_TN = (((0,), (0,)), ((), ()))


def _lockstep(*phased):
    phased = list(phased)
    while phased:
        phased = [g for g in phased if next(g, None) is not None]


def _hgrn_direction(q_ref, g_ref, v_ref, st_ref, o_ref, tri, mask, fwd):
    C = q_ref.shape[0]
    g = g_ref[...]
    ghi = g.astype(BF16)
    glo = (g - ghi.astype(F32)).astype(BF16)
    b = jnp.dot(jnp.concatenate([tri, tri], axis=1), jnp.concatenate([ghi, glo], axis=0),
                preferred_element_type=F32)
    mid = C // 2 - 1 if fwd else C // 2
    end = C - 1 if fwd else 0
    r = b[mid:mid + 1, :]
    b_end = b[end:end + 1, :]
    qt = q_ref[...].astype(F32) * jnp.exp(b - r)
    kt = (1.0 - jnp.exp(g)) * jnp.exp(r - b)
    qtb = qt.astype(BF16)
    ktb = kt.astype(BF16)
    qhb = (qt * jnp.exp(r)).astype(BF16)
    khb = (kt * jnp.exp(b_end - r)).astype(BF16)
    decay = jnp.exp(b_end)
    v = v_ref[...]
    heads =[slice(h * HG_DK, (h + 1) * HG_DK) for h in range(HG_HEADS)]
    zero = jnp.zeros((C, HG_DK), BF16)
    mask2 = jnp.concatenate([mask, mask], axis=1)
    yield True
    scores = []
    for p in range(HG_HEADS // 2):
        k1, k2 = ktb[:, heads[2 * p]], ktb[:, heads[2 * p + 1]]
        kk = jnp.concatenate([jnp.concatenate([k1, zero], axis=1),
                              jnp.concatenate([zero, k2], axis=1)], axis=0)
        s2 = lax.dot_general(qtb[:, 2 * p * HG_DK:(2 * p + 2) * HG_DK], kk, _NT,
                             preferred_element_type=F32)
        s2 = jnp.where(mask2, s2, 0.0).astype(BF16)
        scores += [s2[:, :C], s2[:, C:]]
    yield True
    for h, sl in enumerate(heads):
        st = st_ref[h]
        vt = v[:, sl].T
        o_ref[:, sl] = lax.dot_general(
            jnp.concatenate([scores[h], qhb[:, sl]], axis=1),
            jnp.concatenate([vt, st.astype(BF16)], axis=1), _NT,
            preferred_element_type=F32).astype(o_ref.dtype)
        st_ref[h] = st * decay[:, sl] + jnp.dot(vt, khb[:, sl], preferred_element_type=F32)


def _hgrn_direction_stepwise(q_ref, g_ref, v_ref, st_ref, o_ref, q32_ref, v32_ref, o32_ref, fwd):
    C = q_ref.shape[0]
    q32_ref[...] = q_ref[...].astype(F32)
    v32_ref[...] = v_ref[...].astype(F32)
    sub = SUBLANES
    rows = lax.broadcasted_iota(jnp.int32, (sub, HG_DK), 0)

    def group(i, carry):
        base = pl.multiple_of((i if fwd else C // sub - 1 - i) * sub, sub)
        f = jnp.exp(g_ref[pl.ds(base, sub), :])
        k = 1.0 - f
        q = q32_ref[pl.ds(base, sub), :]
        v = v32_ref[pl.ds(base, sub), :]
        for h in range(HG_HEADS):
            sl = slice(h * HG_DK, (h + 1) * HG_DK)
            st = st_ref[h]
            out = jnp.zeros((sub, HG_DK), F32)
            for r in (range(sub) if fwd else range(sub - 1, -1, -1)):
                v_t = jnp.where(rows == 0, v[r:r + 1, sl], 0.0).astype(BF16)
                k_t = jnp.broadcast_to(k[r:r + 1, sl], (sub, HG_DK)).astype(BF16)
                st = st * f[r:r + 1, sl] + lax.dot_general(v_t, k_t, _TN, preferred_element_type=F32)
                q_t = jnp.broadcast_to(q[r:r + 1, sl], (sub, HG_DK)).astype(BF16)
                o_t = lax.dot_general(q_t, st.astype(BF16), _NT, preferred_element_type=F32)
                out = jnp.where(rows == r, o_t, out)
            st_ref[h] = st
            o32_ref[pl.ds(base, sub), sl] = out
        return carry

    lax.fori_loop(0, C // sub, group, 0)
    o_ref[...] = o32_ref[...].astype(o_ref.dtype)


def _hgrn_kernel(safe_ref, qf_ref, qb_ref, gf_ref, gb_ref, vf_ref, vb_ref, tril_ref, triu_ref,
                 of_ref, ob_ref, sf_ref, sb_ref, q32_ref, v32_ref, o32_ref):
    b, j = pl.program_id(0), pl.program_id(1)
    ns = pl.num_programs(1)
    n = HG_CHUNKS_PER_STEP
    C = qf_ref.shape[0] // n

    @pl.when(j == 0)
    def _():
        sf_ref[...] = jnp.zeros_like(sf_ref)
        sb_ref[...] = jnp.zeros_like(sb_ref)

    row = lax.broadcasted_iota(jnp.int32, (C, C), 0)
    col = lax.broadcasted_iota(jnp.int32, (C, C), 1)
    safe_f = [safe_ref[0, (b * ns + j) * n + u] != 0 for u in range(n)]
    safe_b = [safe_ref[1, (b * ns + ns - 1 - j) * n + u] != 0 for u in range(n)]
    part = lambda ref, u: ref.at[pl.ds(u * C, C), :]

    def forward(u, stepwise):
        refs = (part(qf_ref, u), part(gf_ref, u), part(vf_ref, u), sf_ref, part(of_ref, u))
        if stepwise:
            _hgrn_direction_stepwise(*refs, q32_ref, v32_ref, o32_ref, True)
            return iter(())
        return _hgrn_direction(*refs, tril_ref[...], row >= col, True)

    def backward(u, stepwise):
        refs = (part(qb_ref, u), part(gb_ref, u), part(vb_ref, u), sb_ref, part(ob_ref, u))
        if stepwise:
            _hgrn_direction_stepwise(*refs, q32_ref, v32_ref, o32_ref, False)
            return iter(())
        return _hgrn_direction(*refs, triu_ref[...], row <= col, False)

    all_safe = functools.reduce(jnp.logical_and, safe_f + safe_b)

    @pl.when(all_safe)
    def _():
        _lockstep(*[d for u in range(n) for d in (forward(u, False), backward(n - 1 - u, False))])

    @pl.when(jnp.logical_not(all_safe))
    def _():
        run = lambda direction, u, stepwise: lambda: _lockstep(direction(u, stepwise))
        for u in range(n):
            pl.when(safe_f[u])(run(forward, u, False))
            pl.when(jnp.logical_not(safe_f[u]))(run(forward, u, True))
        for u in reversed(range(n)):
            pl.when(safe_b[u])(run(backward, u, False))
            pl.when(jnp.logical_not(safe_b[u]))(run(backward, u, True))


def _hgrn(q, gf, gb, iv, gmin, batch):
    T = q.shape[0]
    C = HG_CHUNK
    nc = T // batch // C
    nt, nh, _ = gmin.shape
    halves = gmin[:, :, 0].reshape(nt, 2, nh // 4, 2)
    safe = (jnp.min(halves, axis=-1) > -HG_SAFE_LOGDECAY).astype(jnp.int32)
    safe = safe.transpose(1, 0, 2).reshape(2, T // C)
    n = HG_CHUNKS_PER_STEP
    ns = nc // n
    fwd = pl.BlockSpec((n * C, HG_WIDTH), lambda b, j, s: (b * ns + j, 0))
    bwd = pl.BlockSpec((n * C, HG_WIDTH), lambda b, j, s: (b * ns + ns - 1 - j, 0))
    const = lambda shape: pl.BlockSpec(shape, lambda b, j, s: (0,) * len(shape),
                                       pipeline_mode=pl.Buffered(1))
    row = lax.broadcasted_iota(jnp.int32, (C, C), 0)
    col = lax.broadcasted_iota(jnp.int32, (C, C), 1)
    tril = (row >= col).astype(BF16)
    triu = (row <= col).astype(BF16)
    grid_spec = pltpu.PrefetchScalarGridSpec(
        num_scalar_prefetch=1,
        grid=(batch, ns),
        in_specs=[fwd, bwd, fwd, bwd, fwd, bwd, const((C, C)), const((C, C))],
        out_specs=(fwd, bwd),
        scratch_shapes=[pltpu.VMEM((HG_HEADS, HG_DK, HG_DK), F32),
                        pltpu.VMEM((HG_HEADS, HG_DK, HG_DK), F32),
                        pltpu.VMEM((C, HG_WIDTH), F32), pltpu.VMEM((C, HG_WIDTH), F32),
                        pltpu.VMEM((C, HG_WIDTH), F32)],
    )
    return pl.pallas_call(
        _hgrn_kernel,
        grid_spec=grid_spec,
        out_shape=(jax.ShapeDtypeStruct((T, HG_WIDTH), BF16), jax.ShapeDtypeStruct((T, HG_WIDTH), BF16)),
        compiler_params=pltpu.CompilerParams(
            dimension_semantics=("parallel", "arbitrary"), vmem_limit_bytes=VMEM_MIB["hgrn"] << 20),
        name="hgrn",
    )(safe, q, q, gf, gb, iv, iv, tril, triu)


def _mix_kernel(x_ref, lng_ref, lnb_ref, of_ref, ob_ref, so_ref, sgb_ref, ag_ref, ng_ref, wpb_ref,
                wo_ref, l1g_ref, l1b_ref, wr_ref, rb_ref, tri_ref, x1_ref, x1p_ref, route_ref,
                cnt_ref, carry_ref):
    tm = x_ref.shape[0]

    @pl.when(pl.program_id(0) == 0)
    def _():
        carry_ref[...] = jnp.zeros_like(carry_ref)

    o = of_ref[...].astype(F32) + ob_ref[...].astype(F32)
    heads = []
    for h in range(HG_HEADS):
        oh = o[:, h * HG_DK:(h + 1) * HG_DK]
        heads.append(oh * lax.rsqrt(jnp.mean(oh * oh, axis=-1, keepdims=True) + RMS_EPS))
    rn = jnp.concatenate(heads, axis=1) * ng_ref[...] * so_ref[...].astype(F32)
    r = _bdot(rn, wpb_ref[...])
    mixed = ag_ref[...].astype(F32) + sgb_ref[...].astype(F32) * r
    y = _bdot(mixed, wo_ref[...])
    xn = _layer_norm(x_ref[...], lng_ref[...], lnb_ref[...])
    x1 = _layer_norm(ALPHA * xn + y, l1g_ref[...], l1b_ref[...])
    x1_ref[...] = x1
    x1p_ref[...] = _pack_rows(x1)

    neg = jnp.float32(-jnp.inf)
    reps = tm // LANES
    scores = jax.nn.sigmoid(lax.dot_general(wr_ref[...], x1.astype(BF16), _NT,
                                            preferred_element_type=F32))
    biased = (scores + jnp.concatenate([rb_ref[...]] * reps, axis=1)).reshape(
        N_GROUPS, GROUP_SIZE, tm)
    sub = lax.broadcasted_iota(jnp.int32, biased.shape, 1).astype(F32)
    m1 = jnp.max(biased, axis=1, keepdims=True)
    first = jnp.min(jnp.where(biased == m1, sub, float(GROUP_SIZE)), axis=1, keepdims=True)
    m2 = jnp.max(jnp.where(sub == first, neg, biased), axis=1, keepdims=True)
    gs = (m1 + m2).reshape(N_GROUPS, tm)
    grp = lax.broadcasted_iota(jnp.int32, (N_GROUPS, tm), 0)
    ahead = jnp.zeros((N_GROUPS, tm), F32)
    for d in range(1, N_GROUPS):
        other = pltpu.roll(gs, d, 0)
        tie = jnp.where(grp >= d, 1.0, 0.0)
        ahead = ahead + jnp.where(other > gs, 1.0, jnp.where(other == gs, tie, 0.0))
    keep = (ahead < TOPK_GROUPS).reshape(N_GROUPS, 1, tm)
    allowed = jnp.where(keep, biased, neg).reshape(N_EXPERTS, tm)
    row = lax.broadcasted_iota(jnp.int32, (N_EXPERTS, tm), 0).astype(F32)
    sel = jnp.zeros((N_EXPERTS, tm), F32)
    picks = []
    for _ in range(TOP_K):
        m = jnp.max(allowed, axis=0, keepdims=True)
        first = jnp.min(jnp.where(allowed == m, row, float(N_EXPERTS)), axis=0, keepdims=True)
        hit = row == first
        picks.append((first, hit, jnp.sum(jnp.where(hit, scores, 0.0), axis=0, keepdims=True)))
        sel = jnp.where(hit, 1.0, sel)
        allowed = jnp.where(hit, neg, allowed)
    wsum = picks[0][2]
    for pk in picks[1:]:
        wsum = wsum + pk[2]
    selb = sel.astype(BF16)
    carry = carry_ref[...]
    before = (jnp.dot(selb, tri_ref[...], preferred_element_type=F32)
              + jnp.concatenate([carry] * reps, axis=1))
    total = carry + jnp.dot(selb, jnp.ones((tm, LANES), BF16), preferred_element_type=F32)
    carry_ref[...] = total
    cnt_ref[...] = total
    blank = [jnp.zeros((1, tm), F32)] * (SUBLANES - TOP_K)
    route_ref[...] = jnp.concatenate(
        [pk[0] for pk in picks] + blank
        + [jnp.sum(jnp.where(pk[1], before, 0.0), axis=0, keepdims=True) for pk in picks] + blank
        + [pk[2] / wsum * ROUTED_SCALE for pk in picks] + blank, axis=0)


def _mix(x, of, ob, so, sgb, ag, p):
    T = x.shape[0]
    tm = TOKEN_TILE
    tok = lambda w: pl.BlockSpec((tm, w), lambda i: (i, 0))
    row = lax.broadcasted_iota(jnp.int32, (tm, tm), 0)
    col = lax.broadcasted_iota(jnp.int32, (tm, tm), 1)
    tri = (row < col).astype(BF16)
    return pl.pallas_call(
        _mix_kernel,
        grid=(T // tm,),
        in_specs=[
            tok(D_MODEL), _const_spec((1, D_MODEL)), _const_spec((1, D_MODEL)),
            tok(HG_WIDTH), tok(HG_WIDTH), tok(HG_WIDTH), tok(D_MODEL), tok(D_MODEL),
            _const_spec((1, HG_WIDTH)),
            _const_spec((HG_WIDTH, D_MODEL)), _const_spec((D_MODEL, D_MODEL)),
            _const_spec((1, D_MODEL)), _const_spec((1, D_MODEL)),
            _const_spec((N_EXPERTS, D_MODEL)), _const_spec((N_EXPERTS, LANES)),
            _const_spec((tm, tm)),
        ],
        out_specs=(tok(D_MODEL), tok(D_MODEL // 2), pl.BlockSpec((ROUTE_ROWS, tm), lambda i: (0, i)),
                   pl.BlockSpec((N_EXPERTS, LANES), lambda i: (0, 0))),
        out_shape=(jax.ShapeDtypeStruct((T, D_MODEL), F32),
                   jax.ShapeDtypeStruct((T, D_MODEL // 2), jnp.uint32),
                   jax.ShapeDtypeStruct((ROUTE_ROWS, T), F32),
                   jax.ShapeDtypeStruct((N_EXPERTS, LANES), F32)),
        scratch_shapes=[pltpu.VMEM((N_EXPERTS, LANES), F32)],
        compiler_params=pltpu.CompilerParams(
            dimension_semantics=("arbitrary",), vmem_limit_bytes=VMEM_MIB["mix"] << 20),
        name="mix",
    )(x, p["ln_in_g"], p["ln_in_b"], of, ob, so, sgb, ag, p["hg_norm_g"], p["w_pb"], p["w_o"],
      p["ln1_g"], p["ln1_b"], p["w_router"], p["router_bias"], tri)


def _sc_mesh():
    return plsc.VectorSubcoreMesh(core_axis_name="c", subcore_axis_name="s",
                                  num_cores=SC_CORES, num_subcores=SC_SUBCORES)


def _sc_worker():
    return lax.axis_index("s") * SC_CORES + lax.axis_index("c")


def _sc_dispatch(x1p, dest, n_rows):
    T, w = x1p.shape
    W = SC_WINDOW
    per_worker = T // W // (SC_CORES * SC_SUBCORES)

    def body(x_hbm, d_hbm, o_hbm, rows_v, idx_v, sem):
        first = _sc_worker() * per_worker
        for k in range(TOP_K):
            pltpu.sync_copy(d_hbm.at[k, pl.ds(first, per_worker)], idx_v.at[k])

        @pl.loop(0, per_worker)
        def _(j):
            pltpu.sync_copy(x_hbm.at[pl.ds((first + j) * W, W)], rows_v)
            copies = [pltpu.async_copy(rows_v, o_hbm.at[idx_v.at[k, j]], sem) for k in range(TOP_K)]
            for c in copies:
                c.wait()

    return pl.kernel(
        body,
        out_type=jax.ShapeDtypeStruct((n_rows, w), jnp.uint32),
        mesh=_sc_mesh(),
        scratch_types=[pltpu.VMEM((W, w), jnp.uint32), pltpu.VMEM((TOP_K, per_worker, W), jnp.int32),
                       pltpu.SemaphoreType.DMA],
        name="sc_dispatch",
    )(x1p, dest.reshape(TOP_K, T // W, W))


def _sc_combine(ys, dest, gates):
    T = dest.shape[1]
    W = SC_COMBINE_WINDOW
    w = ys.shape[1]
    per_worker = T // W // (SC_CORES * SC_SUBCORES)
    assert per_worker % 2 == 0 and per_worker * W * SC_CORES * SC_SUBCORES == T
    assert 2 * W == SC_LANES

    def body(y_hbm, d_hbm, g_hbm, o_hbm, idx_v, gate_v, *scratch):
        slots = [dict(rows=scratch[s], out=scratch[2 + s], gsem=scratch[4 + s], wsem=scratch[6 + s],
                      lane0=s * W) for s in range(2)]
        first = _sc_worker() * per_worker
        for k in range(TOP_K):
            pltpu.sync_copy(d_hbm.at[k, pl.ds(first * W, per_worker * W)], idx_v.at[k])
            pltpu.sync_copy(g_hbm.at[k, pl.ds(first * W, per_worker * W)], gate_v.at[k])

        def loads(b, win):
            i = win - first
            return [pltpu.make_async_copy(y_hbm.at[idx_v.at[k, pl.ds(i * W, W)]],
                                          b["rows"].at[k], b["gsem"]) for k in range(TOP_K)]

        def write_back(b, win):
            return pltpu.make_async_copy(b["out"], o_hbm.at[pl.ds(win * W, W)], b["wsem"])

        def fetch(b, win):
            for c in loads(b, win):
                c.start()

        def rounded(x):
            u = plsc.bitcast(x, jnp.uint32)
            return u + (jnp.uint32(0x7FFF) + ((u >> 16) & jnp.uint32(1)))

        def reduce_rows(b, pair_gates):
            rows, out = b["rows"], b["out"]
            lane = lax.iota(jnp.int32, SC_LANES)

            @pl.loop(0, W)
            def _(j):
                mine = lane == b["lane0"] + j
                gate = [jnp.broadcast_to(jnp.sum(jnp.where(mine, g, 0.0)), (SC_LANES,))
                        for g in pair_gates]

                @plsc.parallel_loop(0, w, step=SC_LANES, unroll=4)
                def _(col):
                    lo, hi = [], []
                    for k in range(TOP_K):
                        p = rows[k, j, pl.ds(col, SC_LANES)]
                        lo.append(plsc.bitcast(p << 16, F32) * gate[k])
                        hi.append(plsc.bitcast(p & jnp.uint32(0xFFFF0000), F32) * gate[k])
                    lo = rounded((lo[0] + lo[1]) + (lo[2] + lo[3]) + (lo[4] + lo[5]))
                    hi = rounded((hi[0] + hi[1]) + (hi[2] + hi[3]) + (hi[4] + hi[5]))
                    out[j, pl.ds(col, SC_LANES)] = (hi & jnp.uint32(0xFFFF0000)) | (lo >> 16)

        def process(b, win, pair_gates, not_first):
            for c in loads(b, win):
                c.wait()

            @pl.when(not_first)
            def _():
                write_back(b, win).wait()

            reduce_rows(b, pair_gates)
            write_back(b, win).start()

        fetch(slots[0], first)

        @pl.loop(0, per_worker, step=2)
        def _(i):
            win = first + i
            pair_gates = [gate_v[k, pl.ds(i * W, SC_LANES)] for k in range(TOP_K)]
            fetch(slots[1], win + 1)
            process(slots[0], win, pair_gates, i > 0)

            @pl.when(i + 2 < per_worker)
            def _():
                fetch(slots[0], win + 2)

            process(slots[1], win + 1, pair_gates, i > 0)

        for b in slots:
            write_back(b, first).wait()

    slot_types = ([pltpu.VMEM((TOP_K, per_worker * W), jnp.int32),
                   pltpu.VMEM((TOP_K, per_worker * W), F32)]
                  + [pltpu.VMEM((TOP_K, W, w), jnp.uint32)] * 2
                  + [pltpu.VMEM((W, w), jnp.uint32)] * 2
                  + [pltpu.SemaphoreType.DMA] * 4)
    return pl.kernel(
        body,
        out_type=jax.ShapeDtypeStruct((T, w), jnp.uint32),
        mesh=_sc_mesh(),
        scratch_types=slot_types,
        compiler_params=pltpu.CompilerParams(needs_layout_passes=False),
        name="sc_combine",
    )(ys, dest, gates)


def _experts_kernel(be_ref, nv_ref, io_ref, xs_ref, wg_ref, wu_ref, wd_ref, ys_ref):
    del be_ref, io_ref
    n_valid = nv_ref[pl.program_id(0)]
    mb, w = xs_ref.shape

    @pl.when(n_valid > 0)
    def _():
        keep = lax.broadcasted_iota(jnp.int32, (mb, w), 0) < n_valid
        lo, hi = _unpack_rows(jnp.where(keep, xs_ref[...], jnp.uint32(0)))
        lo, hi = lo.astype(BF16), hi.astype(BF16)

        def proj(w_ref):
            return (jnp.dot(lo, w_ref[0, :w, :], preferred_element_type=F32)
                    + jnp.dot(hi, w_ref[0, w:, :], preferred_element_type=F32))

        hb = jax.nn.silu(proj(wg_ref)) * proj(wu_ref)
        ys_ref[...] = _pack_rows(jnp.dot(hb.astype(BF16), wd_ref[0], preferred_element_type=F32))


def _experts(xs, blk_exp, n_valid, blk_io, weights):
    n_rows, w = xs.shape
    mb = n_rows // blk_exp.shape[0]
    by_expert = lambda shape: pl.BlockSpec((1,) + shape, lambda i, be, nv, io: (be[i], 0, 0))
    w_specs = [by_expert((D_MODEL, EXPERT_DIM)), by_expert((D_MODEL, EXPERT_DIM)),
               by_expert((EXPERT_DIM, D_MODEL))]
    ys_spec = pl.BlockSpec((mb, w), lambda i, be, nv, io: (io[i], 0))
    ys_shape = jax.ShapeDtypeStruct((n_rows, w), jnp.uint32)
    grid_spec = pltpu.PrefetchScalarGridSpec(
        num_scalar_prefetch=3,
        grid=(n_rows // mb,),
        in_specs=[pl.BlockSpec((mb, w), lambda i, be, nv, io: (io[i], 0))] + w_specs,
        out_specs=ys_spec,
    )
    return pl.pallas_call(
        _experts_kernel,
        grid_spec=grid_spec,
        out_shape=ys_shape,
        compiler_params=pltpu.CompilerParams(
            dimension_semantics=("arbitrary",), vmem_limit_bytes=VMEM_MIB["experts"] << 20),
        name="experts",
    )(blk_exp, n_valid, blk_io, xs, *weights)


def _final_kernel(x1_ref, routed_ref, wsg_ref, wsu_ref, wsd_ref, l2g_ref, l2b_ref, out_ref):
    x1 = x1_ref[...]
    xb = x1.astype(BF16)
    hs = (jax.nn.silu(jnp.dot(xb, wsg_ref[...], preferred_element_type=F32))
          * jnp.dot(xb, wsu_ref[...], preferred_element_type=F32))
    shared = jnp.dot(hs.astype(BF16), wsd_ref[...], preferred_element_type=F32)
    routed = jnp.concatenate(_unpack_rows(routed_ref[...]), axis=1)
    out_ref[...] = _layer_norm(ALPHA * x1 + (routed + shared), l2g_ref[...], l2b_ref[...])


def _final(x1, routed, p):
    T = x1.shape[0]
    tm = FINAL_TILE
    tok = lambda w: pl.BlockSpec((tm, w), lambda i: (i, 0))
    return pl.pallas_call(
        _final_kernel,
        grid=(T // tm,),
        in_specs=[
            tok(D_MODEL), tok(D_MODEL // 2),
            _const_spec((D_MODEL, SHARED_DIM)), _const_spec((D_MODEL, SHARED_DIM)),
            _const_spec((SHARED_DIM, D_MODEL)),
            _const_spec((1, D_MODEL)), _const_spec((1, D_MODEL)),
        ],
        out_specs=tok(D_MODEL),
        out_shape=jax.ShapeDtypeStruct((T, D_MODEL), F32),
        compiler_params=pltpu.CompilerParams(
            dimension_semantics=("parallel",), vmem_limit_bytes=VMEM_MIB["final"] << 20),
        name="final",
    )(x1, routed, p["w_sh_gate"], p["w_sh_up"], p["w_sh_down"], p["ln2_g"], p["ln2_b"])


def _routing_layout(route, counts, n_tokens):
    mb = EXPERT_BLOCK
    n_blocks = -(-n_tokens * TOP_K // mb) + N_EXPERTS
    idx = route[0:TOP_K].astype(jnp.int32)
    rank = route[SUBLANES:SUBLANES + TOP_K].astype(jnp.int32)
    counts = counts.astype(jnp.int32)
    padded = (counts + mb - 1) // mb * mb
    pad_end = jnp.cumsum(padded)
    pad_start = pad_end - padded
    experts = jnp.arange(N_EXPERTS, dtype=jnp.int32)
    dest = rank + jnp.sum(jnp.where(idx[:, :, None] == experts, pad_start, 0), axis=-1)
    blk_start = jnp.arange(n_blocks, dtype=jnp.int32) * mb
    blk_exp = jnp.minimum(
        jnp.sum((pad_end[None, :] <= blk_start[:, None]).astype(jnp.int32), axis=1), N_EXPERTS - 1)
    valid_end = jnp.sum(jnp.where(blk_exp[:, None] == experts, pad_start + counts, 0), axis=-1)
    n_valid = jnp.clip(valid_end - blk_start, 0, mb).astype(jnp.int32)
    blk_io = jnp.minimum(jnp.arange(n_blocks, dtype=jnp.int32), pad_end[-1] // mb - 1)
    gates = route[2 * SUBLANES:2 * SUBLANES + TOP_K]
    return dest, gates, blk_exp, n_valid, blk_io, n_blocks * mb


def _encode(x, p, expert_weights):
    batch, seq, _ = x.shape
    T = batch * seq
    xt = x.reshape(T, D_MODEL)
    if expert_weights[0].dtype == BF16:
        (ag, q, gf, gb, iv, so, sgb, gmin), _ = _inproj(xt, p)
    else:
        (ag, q, gf, gb, iv, so, sgb, gmin), expert_weights = _inproj(xt, p, expert_weights)
    of, ob = _hgrn(q, gf, gb, iv, gmin, batch)
    x1, x1p, route, cnt = _mix(xt, of, ob, so, sgb, ag, p)
    dest, gates, blk_exp, n_valid, blk_io, n_rows = _routing_layout(route, cnt[:, 0], T)
    xs = _sc_dispatch(x1p, dest, n_rows)
    ys = _experts(xs, blk_exp, n_valid, blk_io, expert_weights)
    out = _final(x1, _sc_combine(ys, dest, gates), p)
    return out.reshape(batch, seq, D_MODEL), expert_weights


def _prepare_params(ln_in_g, ln_in_b, w_in, a_ln_g, a_ln_b, a_ws, a_sb, hg_lb_logits, hg_norm_g,
                    w_pa, w_pb, w_o, ln1_g, ln1_b, w_router, router_bias, w_sh_gate, w_sh_up,
                    w_sh_down, ln2_g, ln2_b):
    l = 0
    row = lambda v: v.reshape(1, -1).astype(F32)
    ws = a_ws[l].astype(BF16)
    wsp = jnp.concatenate([ws[0::2], ws[1::2]], axis=2)
    sbf = jnp.repeat(a_sb[l].astype(F32), A_WIDTH // A_GROUPS, axis=1)
    lb = jnp.cumsum(jax.nn.softmax(hg_lb_logits.astype(F32), axis=1), axis=1)[:, l]
    return dict(
        ln_in_g=row(ln_in_g), ln_in_b=row(ln_in_b), w_in=w_in[l].astype(BF16),
        a_ln_g=row(a_ln_g[l]), a_ln_b=row(a_ln_b[l]), wsp=wsp, sbf=sbf, lb=lb,
        w_pa=w_pa[l].astype(BF16), hg_norm_g=row(hg_norm_g[l]),
        w_pb=w_pb[l].astype(BF16), w_o=w_o[l].astype(BF16),
        ln1_g=row(ln1_g[l]), ln1_b=row(ln1_b[l]),
        w_router=w_router[l].T.astype(BF16),
        router_bias=jnp.broadcast_to(router_bias[l].astype(F32)[:, None], (N_EXPERTS, LANES)),
        w_sh_gate=w_sh_gate[l].astype(BF16), w_sh_up=w_sh_up[l].astype(BF16),
        w_sh_down=w_sh_down[l].astype(BF16),
        ln2_g=row(ln2_g[l]), ln2_b=row(ln2_b[l]),
    )


def kernel(x_prompt, x_sample, ln_in_g, ln_in_b, w_in, a_ln_g, a_ln_b, a_ws, a_sb, hg_lb_logits,
           hg_norm_g, w_pa, w_pb, w_o, ln1_g, ln1_b, w_router, router_bias, w_e_gate, w_e_up,
           w_e_down, w_sh_gate, w_sh_up, w_sh_down, ln2_g, ln2_b):
    p = _prepare_params(ln_in_g, ln_in_b, w_in, a_ln_g, a_ln_b, a_ws, a_sb, hg_lb_logits, hg_norm_g,
                        w_pa, w_pb, w_o, ln1_g, ln1_b, w_router, router_bias, w_sh_gate, w_sh_up,
                        w_sh_down, ln2_g, ln2_b)
    y_prompt, expert_weights = _encode(x_prompt, p, (w_e_gate[0], w_e_up[0], w_e_down[0]))
    y_sample, _ = _encode(x_sample, p, expert_weights)
    return y_prompt, y_sample
```

```python
import functools

import jax
import jax.numpy as jnp
from jax import lax
from jax.experimental import pallas as pl
from jax.experimental.pallas import tpu as pltpu
from jax.experimental.pallas import tpu_sc as plsc

F32 = jnp.float32
BF16 = jnp.bfloat16

D_MODEL = 1024
A_GROUPS = 8
A_WIDTH = 512
A_CHUNK = 128
HG_HEADS = 8
HG_DK = 128
HG_WIDTH = HG_HEADS * HG_DK
N_IN = 2 * A_WIDTH + 5 * HG_WIDTH + 2 * D_MODEL
N_EXPERTS = 64
TOP_K = 6
N_GROUPS = 8
TOPK_GROUPS = 4
GROUP_SIZE = N_EXPERTS // N_GROUPS
EXPERT_DIM = 256
SHARED_DIM = 256
ROUTED_SCALE = 2.5
DEPTH = 1
ALPHA = (2.0 * DEPTH) ** 0.25
LN_EPS = 1e-5
RMS_EPS = 1e-6

LANES = 128
SUBLANES = 8
ROUTE_ROWS = 3 * SUBLANES
TOKEN_TILE = 512
FINAL_TILE = 1024
INPROJ_TILE = 256
HG_CHUNK = 128
HG_CHUNKS_PER_STEP = 2
HG_SAFE_LOGDECAY = 80.0
EXPERT_BLOCK = 1024
SC_CORES = 2
SC_SUBCORES = 16
SC_LANES = 16
SC_WINDOW = 32
SC_COMBINE_WINDOW = 8
VMEM_MIB = dict(inproj=48, hgrn=32, mix=40, experts=32, final=40)

_O_U, _O_V, _O_Q, _O_FF, _O_FB, _O_I, _O_G, _O_GA, _O_GB = (
    0, 512, 1024, 2048, 3072, 4096, 5120, 6144, 7168)


def _layer_norm(x, g, b):
    mu = jnp.mean(x, axis=-1, keepdims=True)
    xc = x - mu
    var = jnp.mean(xc * xc, axis=-1, keepdims=True)
    return xc * lax.rsqrt(var + LN_EPS) * g + b


def _bdot(a, b):
    return jnp.dot(a.astype(BF16), b.astype(BF16), preferred_element_type=F32)


def _pack_rows(x):
    w = x.shape[1] // 2
    lo = lax.bitcast_convert_type(x[:, :w].astype(BF16).astype(F32), jnp.uint32)
    hi = lax.bitcast_convert_type(x[:, w:].astype(BF16).astype(F32), jnp.uint32)
    return hi | (lo >> 16)


def _unpack_rows(p):
    lo = lax.bitcast_convert_type(p << 16, F32)
    hi = lax.bitcast_convert_type(p & jnp.uint32(0xFFFF0000), F32)
    return lo, hi


def _const_spec(shape):
    nd = len(shape)
    return pl.BlockSpec(shape, lambda *_: (0,) * nd, pipeline_mode=pl.Buffered(1))


def _inproj_kernel(x_ref, lng_ref, lnb_ref, win_ref, alng_ref, alnb_ref, wsp_ref, sbf_ref, lb_ref,
                   wpa_ref, *rest):
    if len(rest) > 8:
        for src, dst in zip(rest[:3], rest[11:]):
            dst[...] = src[...].astype(BF16)
        rest = rest[3:11]
    ag_ref, q_ref, gf_ref, gb_ref, iv_ref, so_ref, sgb_ref, gmin_ref = rest
    tm = x_ref.shape[0]
    xb = _layer_norm(x_ref[...], lng_ref[...], lnb_ref[...]).astype(BF16)

    def sec(lo, width):
        return jnp.dot(xb, win_ref[:, lo:lo + width], preferred_element_type=F32)

    v = _layer_norm(jax.nn.gelu(sec(_O_V, A_WIDTH)), alng_ref[...], alnb_ref[...]).astype(BF16)
    u = jax.nn.gelu(sec(_O_U, A_WIDTH))

    q_ref[...] = jax.nn.silu(sec(_O_Q, HG_WIDTH)).astype(BF16)
    mins = []
    for d, (off, g_ref) in enumerate(((_O_FF, gf_ref), (_O_FB, gb_ref))):
        lb = lb_ref[d:d + 1, :]
        f = lb + (1.0 - lb) * jax.nn.sigmoid(sec(off, HG_WIDTH))
        g = jnp.log(f)
        g_ref[...] = g
        half = jnp.sum(g.reshape(tm // (HG_CHUNK // 2), HG_CHUNK // 2, HG_WIDTH), axis=1)
        mins.append(jnp.min(half, axis=-1, keepdims=True))
    gmin_ref[0] = jnp.broadcast_to(jnp.concatenate(mins, axis=0), gmin_ref.shape[1:])
    so_ref[...] = jax.nn.silu(sec(_O_G, HG_WIDTH)).astype(BF16)
    sgb_ref[...] = jax.nn.sigmoid(sec(_O_GB, D_MODEL)).astype(BF16)

    lane = lax.broadcasted_iota(jnp.int32, (A_CHUNK, LANES), 1)
    left = lane < (A_WIDTH // A_GROUPS)
    zero = jnp.zeros((A_CHUNK, LANES), BF16)
    chunks = []
    for c in range(tm // A_CHUNK):
        vc = v[c * A_CHUNK:(c + 1) * A_CHUNK]
        cols = []
        for p in range(A_GROUPS // 2):
            vp = vc[:, p * LANES:(p + 1) * LANES]
            rhs = jnp.concatenate([jnp.where(left, vp, zero), jnp.where(left, zero, vp)], axis=0)
            cols.append(jnp.dot(wsp_ref[p], rhs, preferred_element_type=F32))
        chunks.append(jnp.concatenate(cols, axis=1) + sbf_ref[...])
    mixed = jnp.concatenate(chunks, axis=0)
    a = _bdot(u * mixed, wpa_ref[...])
    ag_ref[...] = (jax.nn.sigmoid(sec(_O_GA, D_MODEL)) * a).astype(BF16)
    iv_ref[...] = sec(_O_I, HG_WIDTH).astype(BF16)


def _inproj(x, p, expert_weights=None):
    T = x.shape[0]
    tm = INPROJ_TILE
    nt = T // tm
    nh = 2 * tm // (HG_CHUNK // 2)
    tok = lambda w: pl.BlockSpec((tm, w), lambda i: (i, 0))
    outs = (
        jax.ShapeDtypeStruct((T, D_MODEL), BF16),
        jax.ShapeDtypeStruct((T, HG_WIDTH), BF16),
        jax.ShapeDtypeStruct((T, HG_WIDTH), F32),
        jax.ShapeDtypeStruct((T, HG_WIDTH), F32),
        jax.ShapeDtypeStruct((T, HG_WIDTH), BF16),
        jax.ShapeDtypeStruct((T, HG_WIDTH), BF16),
        jax.ShapeDtypeStruct((T, D_MODEL), BF16),
        jax.ShapeDtypeStruct((nt, nh, LANES), F32),
    )
    in_specs = [
        tok(D_MODEL),
        _const_spec((1, D_MODEL)), _const_spec((1, D_MODEL)),
        _const_spec((D_MODEL, N_IN)),
        _const_spec((1, A_WIDTH)), _const_spec((1, A_WIDTH)),
        _const_spec((A_GROUPS // 2, A_CHUNK, 2 * A_CHUNK)),
        _const_spec((A_CHUNK, A_WIDTH)),
        _const_spec((2, HG_WIDTH)),
        _const_spec((A_WIDTH, D_MODEL)),
    ]
    out_specs = [tok(D_MODEL), tok(HG_WIDTH), tok(HG_WIDTH), tok(HG_WIDTH), tok(HG_WIDTH),
                 tok(HG_WIDTH), tok(D_MODEL), pl.BlockSpec((1, nh, LANES), lambda i: (i, 0, 0))]
    args = [x, p["ln_in_g"], p["ln_in_b"], p["w_in"], p["a_ln_g"], p["a_ln_b"], p["wsp"], p["sbf"],
            p["lb"], p["w_pa"]]
    if expert_weights is not None:
        slabs = [w.reshape(nt, w.shape[0] * w.shape[1] // nt, w.shape[2]) for w in expert_weights]
        w_specs = [pl.BlockSpec((1,) + w.shape[1:], lambda i: (i, 0, 0)) for w in slabs]
        in_specs += w_specs
        out_specs += w_specs
        outs += tuple(jax.ShapeDtypeStruct(w.shape, BF16) for w in slabs)
        args += slabs
    res = pl.pallas_call(
        _inproj_kernel,
        grid=(nt,),
        in_specs=in_specs,
        out_specs=tuple(out_specs),
        out_shape=outs,
        compiler_params=pltpu.CompilerParams(
            dimension_semantics=("parallel",), vmem_limit_bytes=VMEM_MIB["inproj"] << 20),
        name="inproj",
    )(*args)
    return res[:8], tuple(r.reshape(w.shape) for r, w in zip(res[8:], expert_weights or ()))


_NT = (((1,), (1,)), ((), ()))
_TN = (((0,), (0,)), ((), ()))


def _lockstep(*phased):
    phased = list(phased)
    while phased:
        phased = [g for g in phased if next(g, None) is not None]


def _hgrn_direction(q_ref, g_ref, v_ref, st_ref, o_ref, tri, mask, fwd):
    C = q_ref.shape[0]
    g = g_ref[...]
    ghi = g.astype(BF16)
    glo = (g - ghi.astype(F32)).astype(BF16)
    b = jnp.dot(jnp.concatenate([tri, tri], axis=1), jnp.concatenate([ghi, glo], axis=0),
                preferred_element_type=F32)
    mid = C // 2 - 1 if fwd else C // 2
    end = C - 1 if fwd else 0
    r = b[mid:mid + 1, :]
    b_end = b[end:end + 1, :]
    qt = q_ref[...].astype(F32) * jnp.exp(b - r)
    kt = (1.0 - jnp.exp(g)) * jnp.exp(r - b)
    qtb = qt.astype(BF16)
    ktb = kt.astype(BF16)
    qhb = (qt * jnp.exp(r)).astype(BF16)
    khb = (kt * jnp.exp(b_end - r)).astype(BF16)
    decay = jnp.exp(b_end)
    v = v_ref[...]
    heads =[slice(h * HG_DK, (h + 1) * HG_DK) for h in range(HG_HEADS)]
    zero = jnp.zeros((C, HG_DK), BF16)
    mask2 = jnp.concatenate([mask, mask], axis=1)
    yield True
    scores = []
    for p in range(HG_HEADS // 2):
        k1, k2 = ktb[:, heads[2 * p]], ktb[:, heads[2 * p + 1]]
        kk = jnp.concatenate([jnp.concatenate([k1, zero], axis=1),
                              jnp.concatenate([zero, k2], axis=1)], axis=0)
        s2 = lax.dot_general(qtb[:, 2 * p * HG_DK:(2 * p + 2) * HG_DK], kk, _NT,
                             preferred_element_type=F32)
        s2 = jnp.where(mask2, s2, 0.0).astype(BF16)
        scores += [s2[:, :C], s2[:, C:]]
    yield True
    for h, sl in enumerate(heads):
        st = st_ref[h]
        vt = v[:, sl].T
        o_ref[:, sl] = lax.dot_general(
            jnp.concatenate([scores[h], qhb[:, sl]], axis=1),
            jnp.concatenate([vt, st.astype(BF16)], axis=1), _NT,
            preferred_element_type=F32).astype(o_ref.dtype)
        st_ref[h] = st * decay[:, sl] + jnp.dot(vt, khb[:, sl], preferred_element_type=F32)


def _hgrn_direction_stepwise(q_ref, g_ref, v_ref, st_ref, o_ref, q32_ref, v32_ref, o32_ref, fwd):
    C = q_ref.shape[0]
    q32_ref[...] = q_ref[...].astype(F32)
    v32_ref[...] = v_ref[...].astype(F32)
    sub = SUBLANES
    rows = lax.broadcasted_iota(jnp.int32, (sub, HG_DK), 0)

    def group(i, carry):
        base = pl.multiple_of((i if fwd else C // sub - 1 - i) * sub, sub)
        f = jnp.exp(g_ref[pl.ds(base, sub), :])
        k = 1.0 - f
        q = q32_ref[pl.ds(base, sub), :]
        v = v32_ref[pl.ds(base, sub), :]
        for h in range(HG_HEADS):
            sl = slice(h * HG_DK, (h + 1) * HG_DK)
            st = st_ref[h]
            out = jnp.zeros((sub, HG_DK), F32)
            for r in (range(sub) if fwd else range(sub - 1, -1, -1)):
                v_t = jnp.where(rows == 0, v[r:r + 1, sl], 0.0).astype(BF16)
                k_t = jnp.broadcast_to(k[r:r + 1, sl], (sub, HG_DK)).astype(BF16)
                st = st * f[r:r + 1, sl] + lax.dot_general(v_t, k_t, _TN, preferred_element_type=F32)
                q_t = jnp.broadcast_to(q[r:r + 1, sl], (sub, HG_DK)).astype(BF16)
                o_t = lax.dot_general(q_t, st.astype(BF16), _NT, preferred_element_type=F32)
                out = jnp.where(rows == r, o_t, out)
            st_ref[h] = st
            o32_ref[pl.ds(base, sub), sl] = out
        return carry

    lax.fori_loop(0, C // sub, group, 0)
    o_ref[...] = o32_ref[...].astype(o_ref.dtype)


def _hgrn_kernel(safe_ref, qf_ref, qb_ref, gf_ref, gb_ref, vf_ref, vb_ref, tril_ref, triu_ref,
                 of_ref, ob_ref, sf_ref, sb_ref, q32_ref, v32_ref, o32_ref):
    b, j = pl.program_id(0), pl.program_id(1)
    ns = pl.num_programs(1)
    n = HG_CHUNKS_PER_STEP
    C = qf_ref.shape[0] // n

    @pl.when(j == 0)
    def _():
        sf_ref[...] = jnp.zeros_like(sf_ref)
        sb_ref[...] = jnp.zeros_like(sb_ref)

    row = lax.broadcasted_iota(jnp.int32, (C, C), 0)
    col = lax.broadcasted_iota(jnp.int32, (C, C), 1)
    safe_f = [safe_ref[0, (b * ns + j) * n + u] != 0 for u in range(n)]
    safe_b = [safe_ref[1, (b * ns + ns - 1 - j) * n + u] != 0 for u in range(n)]
    part = lambda ref, u: ref.at[pl.ds(u * C, C), :]

    def forward(u, stepwise):
        refs = (part(qf_ref, u), part(gf_ref, u), part(vf_ref, u), sf_ref, part(of_ref, u))
        if stepwise:
            _hgrn_direction_stepwise(*refs, q32_ref, v32_ref, o32_ref, True)
            return iter(())
        return _hgrn_direction(*refs, tril_ref[...], row >= col, True)

    def backward(u, stepwise):
        refs = (part(qb_ref, u), part(gb_ref, u), part(vb_ref, u), sb_ref, part(ob_ref, u))
        if stepwise:
            _hgrn_direction_stepwise(*refs, q32_ref, v32_ref, o32_ref, False)
            return iter(())
        return _hgrn_direction(*refs, triu_ref[...], row <= col, False)

    all_safe = functools.reduce(jnp.logical_and, safe_f + safe_b)

    @pl.when(all_safe)
    def _():
        _lockstep(*[d for u in range(n) for d in (forward(u, False), backward(n - 1 - u, False))])

    @pl.when(jnp.logical_not(all_safe))
    def _():
        run = lambda direction, u, stepwise: lambda: _lockstep(direction(u, stepwise))
        for u in range(n):
            pl.when(safe_f[u])(run(forward, u, False))
            pl.when(jnp.logical_not(safe_f[u]))(run(forward, u, True))
        for u in reversed(range(n)):
            pl.when(safe_b[u])(run(backward, u, False))
            pl.when(jnp.logical_not(safe_b[u]))(run(backward, u, True))


def _hgrn(q, gf, gb, iv, gmin, batch):
    T = q.shape[0]
    C = HG_CHUNK
    nc = T // batch // C
    nt, nh, _ = gmin.shape
    halves = gmin[:, :, 0].reshape(nt, 2, nh // 4, 2)
    safe = (jnp.min(halves, axis=-1) > -HG_SAFE_LOGDECAY).astype(jnp.int32)
    safe = safe.transpose(1, 0, 2).reshape(2, T // C)
    n = HG_CHUNKS_PER_STEP
    ns = nc // n
    fwd = pl.BlockSpec((n * C, HG_WIDTH), lambda b, j, s: (b * ns + j, 0))
    bwd = pl.BlockSpec((n * C, HG_WIDTH), lambda b, j, s: (b * ns + ns - 1 - j, 0))
    const = lambda shape: pl.BlockSpec(shape, lambda b, j, s: (0,) * len(shape),
                                       pipeline_mode=pl.Buffered(1))
    row = lax.broadcasted_iota(jnp.int32, (C, C), 0)
    col = lax.broadcasted_iota(jnp.int32, (C, C), 1)
    tril = (row >= col).astype(BF16)
    triu = (row <= col).astype(BF16)
    grid_spec = pltpu.PrefetchScalarGridSpec(
        num_scalar_prefetch=1,
        grid=(batch, ns),
        in_specs=[fwd, bwd, fwd, bwd, fwd, bwd, const((C, C)), const((C, C))],
        out_specs=(fwd, bwd),
        scratch_shapes=[pltpu.VMEM((HG_HEADS, HG_DK, HG_DK), F32),
                        pltpu.VMEM((HG_HEADS, HG_DK, HG_DK), F32),
                        pltpu.VMEM((C, HG_WIDTH), F32), pltpu.VMEM((C, HG_WIDTH), F32),
                        pltpu.VMEM((C, HG_WIDTH), F32)],
    )
    return pl.pallas_call(
        _hgrn_kernel,
        grid_spec=grid_spec,
        out_shape=(jax.ShapeDtypeStruct((T, HG_WIDTH), BF16), jax.ShapeDtypeStruct((T, HG_WIDTH), BF16)),
        compiler_params=pltpu.CompilerParams(
            dimension_semantics=("parallel", "arbitrary"), vmem_limit_bytes=VMEM_MIB["hgrn"] << 20),
        name="hgrn",
    )(safe, q, q, gf, gb, iv, iv, tril, triu)


def _mix_kernel(x_ref, lng_ref, lnb_ref, of_ref, ob_ref, so_ref, sgb_ref, ag_ref, ng_ref, wpb_ref,
                wo_ref, l1g_ref, l1b_ref, wr_ref, rb_ref, tri_ref, x1_ref, x1p_ref, route_ref,
                cnt_ref, carry_ref):
    tm = x_ref.shape[0]

    @pl.when(pl.program_id(0) == 0)
    def _():
        carry_ref[...] = jnp.zeros_like(carry_ref)

    o = of_ref[...].astype(F32) + ob_ref[...].astype(F32)
    heads = []
    for h in range(HG_HEADS):
        oh = o[:, h * HG_DK:(h + 1) * HG_DK]
        heads.append(oh * lax.rsqrt(jnp.mean(oh * oh, axis=-1, keepdims=True) + RMS_EPS))
    rn = jnp.concatenate(heads, axis=1) * ng_ref[...] * so_ref[...].astype(F32)
    r = _bdot(rn, wpb_ref[...])
    mixed = ag_ref[...].astype(F32) + sgb_ref[...].astype(F32) * r
    y = _bdot(mixed, wo_ref[...])
    xn = _layer_norm(x_ref[...], lng_ref[...], lnb_ref[...])
    x1 = _layer_norm(ALPHA * xn + y, l1g_ref[...], l1b_ref[...])
    x1_ref[...] = x1
    x1p_ref[...] = _pack_rows(x1)

    neg = jnp.float32(-jnp.inf)
    reps = tm // LANES
    scores = jax.nn.sigmoid(lax.dot_general(wr_ref[...], x1.astype(BF16), _NT,
                                            preferred_element_type=F32))
    biased = (scores + jnp.concatenate([rb_ref[...]] * reps, axis=1)).reshape(
        N_GROUPS, GROUP_SIZE, tm)
    sub = lax.broadcasted_iota(jnp.int32, biased.shape, 1).astype(F32)
    m1 = jnp.max(biased, axis=1, keepdims=True)
    first = jnp.min(jnp.where(biased == m1, sub, float(GROUP_SIZE)), axis=1, keepdims=True)
    m2 = jnp.max(jnp.where(sub == first, neg, biased), axis=1, keepdims=True)
    gs = (m1 + m2).reshape(N_GROUPS, tm)
    grp = lax.broadcasted_iota(jnp.int32, (N_GROUPS, tm), 0)
    ahead = jnp.zeros((N_GROUPS, tm), F32)
    for d in range(1, N_GROUPS):
        other = pltpu.roll(gs, d, 0)
        tie = jnp.where(grp >= d, 1.0, 0.0)
        ahead = ahead + jnp.where(other > gs, 1.0, jnp.where(other == gs, tie, 0.0))
    keep = (ahead < TOPK_GROUPS).reshape(N_GROUPS, 1, tm)
    allowed = jnp.where(keep, biased, neg).reshape(N_EXPERTS, tm)
    row = lax.broadcasted_iota(jnp.int32, (N_EXPERTS, tm), 0).astype(F32)
    sel = jnp.zeros((N_EXPERTS, tm), F32)
    picks = []
    for _ in range(TOP_K):
        m = jnp.max(allowed, axis=0, keepdims=True)
        first = jnp.min(jnp.where(allowed == m, row, float(N_EXPERTS)), axis=0, keepdims=True)
        hit = row == first
        picks.append((first, hit, jnp.sum(jnp.where(hit, scores, 0.0), axis=0, keepdims=True)))
        sel = jnp.where(hit, 1.0, sel)
        allowed = jnp.where(hit, neg, allowed)
    wsum = picks[0][2]
    for pk in picks[1:]:
        wsum = wsum + pk[2]
    selb = sel.astype(BF16)
    carry = carry_ref[...]
    before = (jnp.dot(selb, tri_ref[...], preferred_element_type=F32)
              + jnp.concatenate([carry] * reps, axis=1))
    total = carry + jnp.dot(selb, jnp.ones((tm, LANES), BF16), preferred_element_type=F32)
    carry_ref[...] = total
    cnt_ref[...] = total
    blank = [jnp.zeros((1, tm), F32)] * (SUBLANES - TOP_K)
    route_ref[...] = jnp.concatenate(
        [pk[0] for pk in picks] + blank
        + [jnp.sum(jnp.where(pk[1], before, 0.0), axis=0, keepdims=True) for pk in picks] + blank
        + [pk[2] / wsum * ROUTED_SCALE for pk in picks] + blank, axis=0)


def _mix(x, of, ob, so, sgb, ag, p):
    T = x.shape[0]
    tm = TOKEN_TILE
    tok = lambda w: pl.BlockSpec((tm, w), lambda i: (i, 0))
    row = lax.broadcasted_iota(jnp.int32, (tm, tm), 0)
    col = lax.broadcasted_iota(jnp.int32, (tm, tm), 1)
    tri = (row < col).astype(BF16)
    return pl.pallas_call(
        _mix_kernel,
        grid=(T // tm,),
        in_specs=[
            tok(D_MODEL), _const_spec((1, D_MODEL)), _const_spec((1, D_MODEL)),
            tok(HG_WIDTH), tok(HG_WIDTH), tok(HG_WIDTH), tok(D_MODEL), tok(D_MODEL),
            _const_spec((1, HG_WIDTH)),
            _const_spec((HG_WIDTH, D_MODEL)), _const_spec((D_MODEL, D_MODEL)),
            _const_spec((1, D_MODEL)), _const_spec((1, D_MODEL)),
            _const_spec((N_EXPERTS, D_MODEL)), _const_spec((N_EXPERTS, LANES)),
            _const_spec((tm, tm)),
        ],
        out_specs=(tok(D_MODEL), tok(D_MODEL // 2), pl.BlockSpec((ROUTE_ROWS, tm), lambda i: (0, i)),
                   pl.BlockSpec((N_EXPERTS, LANES), lambda i: (0, 0))),
        out_shape=(jax.ShapeDtypeStruct((T, D_MODEL), F32),
                   jax.ShapeDtypeStruct((T, D_MODEL // 2), jnp.uint32),
                   jax.ShapeDtypeStruct((ROUTE_ROWS, T), F32),
                   jax.ShapeDtypeStruct((N_EXPERTS, LANES), F32)),
        scratch_shapes=[pltpu.VMEM((N_EXPERTS, LANES), F32)],
        compiler_params=pltpu.CompilerParams(
            dimension_semantics=("arbitrary",), vmem_limit_bytes=VMEM_MIB["mix"] << 20),
        name="mix",
    )(x, p["ln_in_g"], p["ln_in_b"], of, ob, so, sgb, ag, p["hg_norm_g"], p["w_pb"], p["w_o"],
      p["ln1_g"], p["ln1_b"], p["w_router"], p["router_bias"], tri)


def _sc_mesh():
    return plsc.VectorSubcoreMesh(core_axis_name="c", subcore_axis_name="s",
                                  num_cores=SC_CORES, num_subcores=SC_SUBCORES)


def _sc_worker():
    return lax.axis_index("s") * SC_CORES + lax.axis_index("c")


def _sc_dispatch(x1p, dest, n_rows):
    T, w = x1p.shape
    W = SC_WINDOW
    per_worker = T // W // (SC_CORES * SC_SUBCORES)

    def body(x_hbm, d_hbm, o_hbm, rows_v, idx_v, sem):
        first = _sc_worker() * per_worker
        for k in range(TOP_K):
            pltpu.sync_copy(d_hbm.at[k, pl.ds(first, per_worker)], idx_v.at[k])

        @pl.loop(0, per_worker)
        def _(j):
            pltpu.sync_copy(x_hbm.at[pl.ds((first + j) * W, W)], rows_v)
            copies = [pltpu.async_copy(rows_v, o_hbm.at[idx_v.at[k, j]], sem) for k in range(TOP_K)]
            for c in copies:
                c.wait()

    return pl.kernel(
        body,
        out_type=jax.ShapeDtypeStruct((n_rows, w), jnp.uint32),
        mesh=_sc_mesh(),
        scratch_types=[pltpu.VMEM((W, w), jnp.uint32), pltpu.VMEM((TOP_K, per_worker, W), jnp.int32),
                       pltpu.SemaphoreType.DMA],
        name="sc_dispatch",
    )(x1p, dest.reshape(TOP_K, T // W, W))


def _sc_combine(ys, dest, gates):
    T = dest.shape[1]
    W = SC_COMBINE_WINDOW
    w = ys.shape[1]
    per_worker = T // W // (SC_CORES * SC_SUBCORES)
    assert per_worker % 2 == 0 and per_worker * W * SC_CORES * SC_SUBCORES == T
    assert 2 * W == SC_LANES

    def body(y_hbm, d_hbm, g_hbm, o_hbm, idx_v, gate_v, *scratch):
        slots = [dict(rows=scratch[s], out=scratch[2 + s], gsem=scratch[4 + s], wsem=scratch[6 + s],
                      lane0=s * W) for s in range(2)]
        first = _sc_worker() * per_worker
        for k in range(TOP_K):
            pltpu.sync_copy(d_hbm.at[k, pl.ds(first * W, per_worker * W)], idx_v.at[k])
            pltpu.sync_copy(g_hbm.at[k, pl.ds(first * W, per_worker * W)], gate_v.at[k])

        def loads(b, win):
            i = win - first
            return [pltpu.make_async_copy(y_hbm.at[idx_v.at[k, pl.ds(i * W, W)]],
                                          b["rows"].at[k], b["gsem"]) for k in range(TOP_K)]

        def write_back(b, win):
            return pltpu.make_async_copy(b["out"], o_hbm.at[pl.ds(win * W, W)], b["wsem"])

        def fetch(b, win):
            for c in loads(b, win):
                c.start()

        def rounded(x):
            u = plsc.bitcast(x, jnp.uint32)
            r = u + (jnp.uint32(0x7FFF) + ((u >> 16) & jnp.uint32(1)))
            return jnp.where(x != x, jnp.uint32(0x7FC00000), r)

        def reduce_rows(b, pair_gates):
            rows, out = b["rows"], b["out"]
            lane = lax.iota(jnp.int32, SC_LANES)

            @pl.loop(0, W)
            def _(j):
                mine = lane == b["lane0"] + j
                gate = [jnp.broadcast_to(jnp.sum(jnp.where(mine, g, 0.0)), (SC_LANES,))
                        for g in pair_gates]

                @plsc.parallel_loop(0, w, step=SC_LANES, unroll=4)
                def _(col):
                    lo, hi = [], []
                    for k in range(TOP_K):
                        p = rows[k, j, pl.ds(col, SC_LANES)]
                        lo.append(plsc.bitcast(p << 16, F32) * gate[k])
                        hi.append(plsc.bitcast(p & jnp.uint32(0xFFFF0000), F32) * gate[k])
                    lo = rounded((lo[0] + lo[1]) + (lo[2] + lo[3]) + (lo[4] + lo[5]))
                    hi = rounded((hi[0] + hi[1]) + (hi[2] + hi[3]) + (hi[4] + hi[5]))
                    out[j, pl.ds(col, SC_LANES)] = (hi & jnp.uint32(0xFFFF0000)) | (lo >> 16)

        def process(b, win, pair_gates, not_first):
            for c in loads(b, win):
                c.wait()

            @pl.when(not_first)
            def _():
                write_back(b, win).wait()

            reduce_rows(b, pair_gates)
            write_back(b, win).start()

        fetch(slots[0], first)

        @pl.loop(0, per_worker, step=2)
        def _(i):
            win = first + i
            pair_gates = [gate_v[k, pl.ds(i * W, SC_LANES)] for k in range(TOP_K)]
            fetch(slots[1], win + 1)
            process(slots[0], win, pair_gates, i > 0)

            @pl.when(i + 2 < per_worker)
            def _():
                fetch(slots[0], win + 2)

            process(slots[1], win + 1, pair_gates, i > 0)

        for b in slots:
            write_back(b, first).wait()

    slot_types = ([pltpu.VMEM((TOP_K, per_worker * W), jnp.int32),
                   pltpu.VMEM((TOP_K, per_worker * W), F32)]
                  + [pltpu.VMEM((TOP_K, W, w), jnp.uint32)] * 2
                  + [pltpu.VMEM((W, w), jnp.uint32)] * 2
                  + [pltpu.SemaphoreType.DMA] * 4)
    return pl.kernel(
        body,
        out_type=jax.ShapeDtypeStruct((T, w), jnp.uint32),
        mesh=_sc_mesh(),
        scratch_types=slot_types,
        compiler_params=pltpu.CompilerParams(needs_layout_passes=False),
        name="sc_combine",
    )(ys, dest, gates)


def _experts_kernel(be_ref, nv_ref, io_ref, xs_ref, wg_ref, wu_ref, wd_ref, ys_ref):
    del be_ref, io_ref
    n_valid = nv_ref[pl.program_id(0)]
    mb, w = xs_ref.shape

    @pl.when(n_valid > 0)
    def _():
        keep = lax.broadcasted_iota(jnp.int32, (mb, w), 0) < n_valid
        lo, hi = _unpack_rows(jnp.where(keep, xs_ref[...], jnp.uint32(0)))
        lo, hi = lo.astype(BF16), hi.astype(BF16)

        def proj(w_ref):
            return (jnp.dot(lo, w_ref[0, :w, :], preferred_element_type=F32)
                    + jnp.dot(hi, w_ref[0, w:, :], preferred_element_type=F32))

        hb = jax.nn.silu(proj(wg_ref)) * proj(wu_ref)
        ys_ref[...] = _pack_rows(jnp.dot(hb.astype(BF16), wd_ref[0], preferred_element_type=F32))


def _experts(xs, blk_exp, n_valid, blk_io, weights):
    n_rows, w = xs.shape
    mb = n_rows // blk_exp.shape[0]
    by_expert = lambda shape: pl.BlockSpec((1,) + shape, lambda i, be, nv, io: (be[i], 0, 0))
    w_specs = [by_expert((D_MODEL, EXPERT_DIM)), by_expert((D_MODEL, EXPERT_DIM)),
               by_expert((EXPERT_DIM, D_MODEL))]
    ys_spec = pl.BlockSpec((mb, w), lambda i, be, nv, io: (io[i], 0))
    ys_shape = jax.ShapeDtypeStruct((n_rows, w), jnp.uint32)
    grid_spec = pltpu.PrefetchScalarGridSpec(
        num_scalar_prefetch=3,
        grid=(n_rows // mb,),
        in_specs=[pl.BlockSpec((mb, w), lambda i, be, nv, io: (io[i], 0))] + w_specs,
        out_specs=ys_spec,
    )
    return pl.pallas_call(
        _experts_kernel,
        grid_spec=grid_spec,
        out_shape=ys_shape,
        compiler_params=pltpu.CompilerParams(
            dimension_semantics=("arbitrary",), vmem_limit_bytes=VMEM_MIB["experts"] << 20),
        name="experts",
    )(blk_exp, n_valid, blk_io, xs, *weights)


def _final_kernel(x1_ref, routed_ref, wsg_ref, wsu_ref, wsd_ref, l2g_ref, l2b_ref, out_ref):
    x1 = x1_ref[...]
    xb = x1.astype(BF16)
    hs = (jax.nn.silu(jnp.dot(xb, wsg_ref[...], preferred_element_type=F32))
          * jnp.dot(xb, wsu_ref[...], preferred_element_type=F32))
    shared = jnp.dot(hs.astype(BF16), wsd_ref[...], preferred_element_type=F32)
    routed = jnp.concatenate(_unpack_rows(routed_ref[...]), axis=1)
    out_ref[...] = _layer_norm(ALPHA * x1 + (routed + shared), l2g_ref[...], l2b_ref[...])


def _final(x1, routed, p):
    T = x1.shape[0]
    tm = FINAL_TILE
    tok = lambda w: pl.BlockSpec((tm, w), lambda i: (i, 0))
    return pl.pallas_call(
        _final_kernel,
        grid=(T // tm,),
        in_specs=[
            tok(D_MODEL), tok(D_MODEL // 2),
            _const_spec((D_MODEL, SHARED_DIM)), _const_spec((D_MODEL, SHARED_DIM)),
            _const_spec((SHARED_DIM, D_MODEL)),
            _const_spec((1, D_MODEL)), _const_spec((1, D_MODEL)),
        ],
        out_specs=tok(D_MODEL),
        out_shape=jax.ShapeDtypeStruct((T, D_MODEL), F32),
        compiler_params=pltpu.CompilerParams(
            dimension_semantics=("parallel",), vmem_limit_bytes=VMEM_MIB["final"] << 20),
        name="final",
    )(x1, routed, p["w_sh_gate"], p["w_sh_up"], p["w_sh_down"], p["ln2_g"], p["ln2_b"])


def _routing_layout(route, counts, n_tokens):
    mb = EXPERT_BLOCK
    n_blocks = -(-n_tokens * TOP_K // mb) + N_EXPERTS
    idx = route[0:TOP_K].astype(jnp.int32)
    rank = route[SUBLANES:SUBLANES + TOP_K].astype(jnp.int32)
    counts = counts.astype(jnp.int32)
    padded = (counts + mb - 1) // mb * mb
    pad_end = jnp.cumsum(padded)
    pad_start = pad_end - padded
    experts = jnp.arange(N_EXPERTS, dtype=jnp.int32)
    dest = rank + jnp.sum(jnp.where(idx[:, :, None] == experts, pad_start, 0), axis=-1)
    blk_start = jnp.arange(n_blocks, dtype=jnp.int32) * mb
    blk_exp = jnp.minimum(
        jnp.sum((pad_end[None, :] <= blk_start[:, None]).astype(jnp.int32), axis=1), N_EXPERTS - 1)
    valid_end = jnp.sum(jnp.where(blk_exp[:, None] == experts, pad_start + counts, 0), axis=-1)
    n_valid = jnp.clip(valid_end - blk_start, 0, mb).astype(jnp.int32)
    blk_io = jnp.minimum(jnp.arange(n_blocks, dtype=jnp.int32), pad_end[-1] // mb - 1)
    gates = route[2 * SUBLANES:2 * SUBLANES + TOP_K]
    return dest, gates, blk_exp, n_valid, blk_io, n_blocks * mb


def _encode(x, p, expert_weights):
    batch, seq, _ = x.shape
    T = batch * seq
    xt = x.reshape(T, D_MODEL)
    if expert_weights[0].dtype == BF16:
        (ag, q, gf, gb, iv, so, sgb, gmin), _ = _inproj(xt, p)
    else:
        (ag, q, gf, gb, iv, so, sgb, gmin), expert_weights = _inproj(xt, p, expert_weights)
    of, ob = _hgrn(q, gf, gb, iv, gmin, batch)
    x1, x1p, route, cnt = _mix(xt, of, ob, so, sgb, ag, p)
    dest, gates, blk_exp, n_valid, blk_io, n_rows = _routing_layout(route, cnt[:, 0], T)
    xs = _sc_dispatch(x1p, dest, n_rows)
    ys = _experts(xs, blk_exp, n_valid, blk_io, expert_weights)
    out = _final(x1, _sc_combine(ys, dest, gates), p)
    return out.reshape(batch, seq, D_MODEL), expert_weights


def _prepare_params(ln_in_g, ln_in_b, w_in, a_ln_g, a_ln_b, a_ws, a_sb, hg_lb_logits, hg_norm_g,
                    w_pa, w_pb, w_o, ln1_g, ln1_b, w_router, router_bias, w_sh_gate, w_sh_up,
                    w_sh_down, ln2_g, ln2_b):
    l = 0
    row = lambda v: v.reshape(1, -1).astype(F32)
    ws = a_ws[l].astype(BF16)
    wsp = jnp.concatenate([ws[0::2], ws[1::2]], axis=2)
    sbf = jnp.repeat(a_sb[l].astype(F32), A_WIDTH // A_GROUPS, axis=1)
    lb = jnp.cumsum(jax.nn.softmax(hg_lb_logits.astype(F32), axis=1), axis=1)[:, l]
    return dict(
        ln_in_g=row(ln_in_g), ln_in_b=row(ln_in_b), w_in=w_in[l].astype(BF16),
        a_ln_g=row(a_ln_g[l]), a_ln_b=row(a_ln_b[l]), wsp=wsp, sbf=sbf, lb=lb,
        w_pa=w_pa[l].astype(BF16), hg_norm_g=row(hg_norm_g[l]),
        w_pb=w_pb[l].astype(BF16), w_o=w_o[l].astype(BF16),
        ln1_g=row(ln1_g[l]), ln1_b=row(ln1_b[l]),
        w_router=w_router[l].T.astype(BF16),
        router_bias=jnp.broadcast_to(router_bias[l].astype(F32)[:, None], (N_EXPERTS, LANES)),
        w_sh_gate=w_sh_gate[l].astype(BF16), w_sh_up=w_sh_up[l].astype(BF16),
        w_sh_down=w_sh_down[l].astype(BF16),
        ln2_g=row(ln2_g[l]), ln2_b=row(ln2_b[l]),
    )


def kernel(x_prompt, x_sample, ln_in_g, ln_in_b, w_in, a_ln_g, a_ln_b, a_ws, a_sb, hg_lb_logits,
           hg_norm_g, w_pa, w_pb, w_o, ln1_g, ln1_b, w_router, router_bias, w_e_gate, w_e_up,
           w_e_down, w_sh_gate, w_sh_up, w_sh_down, ln2_g, ln2_b):
    p = _prepare_params(ln_in_g, ln_in_b, w_in, a_ln_g, a_ln_b, a_ws, a_sb, hg_lb_logits, hg_norm_g,
                        w_pa, w_pb, w_o, ln1_g, ln1_b, w_router, router_bias, w_sh_gate, w_sh_up,
                        w_sh_down, ln2_g, ln2_b)
    y_prompt, expert_weights = _encode(x_prompt, p, (w_e_gate[0], w_e_up[0], w_e_down[0]))
    y_sample, _ = _encode(x_sample, p, expert_weights)
    return y_prompt, y_sample
```

```python
import functools

import jax
import jax.numpy as jnp
from jax import lax
from jax.experimental import pallas as pl
from jax.experimental.pallas import tpu as pltpu
from jax.experimental.pallas import tpu_sc as plsc

F32 = jnp.float32
BF16 = jnp.bfloat16

D_MODEL = 1024
A_GROUPS = 8
A_WIDTH = 512
A_CHUNK = 128
HG_HEADS = 8
HG_DK = 128
HG_WIDTH = HG_HEADS * HG_DK
N_IN = 2 * A_WIDTH + 5 * HG_WIDTH + 2 * D_MODEL
N_EXPERTS = 64
TOP_K = 6
N_GROUPS = 8
TOPK_GROUPS = 4
GROUP_SIZE = N_EXPERTS // N_GROUPS
EXPERT_DIM = 256
SHARED_DIM = 256
ROUTED_SCALE = 2.5
DEPTH = 1
ALPHA = (2.0 * DEPTH) ** 0.25
LN_EPS = 1e-5
RMS_EPS = 1e-6

LANES = 128
SUBLANES = 8
ROUTE_ROWS = 3 * SUBLANES
TOKEN_TILE = 512
FINAL_TILE = 1024
INPROJ_TILE = 256
HG_CHUNK = 128
HG_CHUNKS_PER_STEP = 2
HG_SAFE_LOGDECAY = 80.0
EXPERT_BLOCK = 1024
SC_CORES = 2
SC_SUBCORES = 16
SC_LANES = 16
SC_WINDOW = 32
SC_COMBINE_WINDOW = 8
VMEM_MIB = dict(inproj=48, hgrn=32, mix=40, experts=32, final=40)

_O_U, _O_V, _O_Q, _O_FF, _O_FB, _O_I, _O_G, _O_GA, _O_GB = (
    0, 512, 1024, 2048, 3072, 4096, 5120, 6144, 7168)


def _layer_norm(x, g, b):
    mu = jnp.mean(x, axis=-1, keepdims=True)
    xc = x - mu
    var = jnp.mean(xc * xc, axis=-1, keepdims=True)
    return xc * lax.rsqrt(var + LN_EPS) * g + b


def _bdot(a, b):
    return jnp.dot(a.astype(BF16), b.astype(BF16), preferred_element_type=F32)


def _pack_rows(x):
    w = x.shape[1] // 2
    lo = lax.bitcast_convert_type(x[:, :w].astype(BF16).astype(F32), jnp.uint32)
    hi = lax.bitcast_convert_type(x[:, w:].astype(BF16).astype(F32), jnp.uint32)
    return hi | (lo >> 16)


def _unpack_rows(p):
    lo = lax.bitcast_convert_type(p << 16, F32)
    hi = lax.bitcast_convert_type(p & jnp.uint32(0xFFFF0000), F32)
    return lo, hi


def _const_spec(shape):
    nd = len(shape)
    return pl.BlockSpec(shape, lambda *_: (0,) * nd, pipeline_mode=pl.Buffered(1))


def _inproj_kernel(x_ref, lng_ref, lnb_ref, win_ref, alng_ref, alnb_ref, wsp_ref, sbf_ref, lb_ref,
                   wpa_ref, *rest):
    if len(rest) > 8:
        for src, dst in zip(rest[:3], rest[11:]):
            dst[...] = src[...].astype(BF16)
        rest = rest[3:11]
    ag_ref, q_ref, gf_ref, gb_ref, iv_ref, so_ref, sgb_ref, gmin_ref = rest
    tm = x_ref.shape[0]
    xb = _layer_norm(x_ref[...], lng_ref[...], lnb_ref[...]).astype(BF16)

    def sec(lo, width):
        return jnp.dot(xb, win_ref[:, lo:lo + width], preferred_element_type=F32)

    v = _layer_norm(jax.nn.gelu(sec(_O_V, A_WIDTH)), alng_ref[...], alnb_ref[...]).astype(BF16)
    u = jax.nn.gelu(sec(_O_U, A_WIDTH))

    q_ref[...] = jax.nn.silu(sec(_O_Q, HG_WIDTH)).astype(BF16)
    mins = []
    for d, (off, g_ref) in enumerate(((_O_FF, gf_ref), (_O_FB, gb_ref))):
        lb = lb_ref[d:d + 1, :]
        f = lb + (1.0 - lb) * jax.nn.sigmoid(sec(off, HG_WIDTH))
        g = jnp.log(f)
        g_ref[...] = g
        half = jnp.sum(g.reshape(tm // (HG_CHUNK // 2), HG_CHUNK // 2, HG_WIDTH), axis=1)
        mins.append(jnp.min(half, axis=-1, keepdims=True))
    gmin_ref[0] = jnp.broadcast_to(jnp.concatenate(mins, axis=0), gmin_ref.shape[1:])
    so_ref[...] = jax.nn.silu(sec(_O_G, HG_WIDTH)).astype(BF16)
    sgb_ref[...] = jax.nn.sigmoid(sec(_O_GB, D_MODEL)).astype(BF16)

    lane = lax.broadcasted_iota(jnp.int32, (A_CHUNK, LANES), 1)
    left = lane < (A_WIDTH // A_GROUPS)
    zero = jnp.zeros((A_CHUNK, LANES), BF16)
    chunks = []
    for c in range(tm // A_CHUNK):
        vc = v[c * A_CHUNK:(c + 1) * A_CHUNK]
        cols = []
        for p in range(A_GROUPS // 2):
            vp = vc[:, p * LANES:(p + 1) * LANES]
            rhs = jnp.concatenate([jnp.where(left, vp, zero), jnp.where(left, zero, vp)], axis=0)
            cols.append(jnp.dot(wsp_ref[p], rhs, preferred_element_type=F32))
        chunks.append(jnp.concatenate(cols, axis=1) + sbf_ref[...])
    mixed = jnp.concatenate(chunks, axis=0)
    a = _bdot(u * mixed, wpa_ref[...])
    ag_ref[...] = (jax.nn.sigmoid(sec(_O_GA, D_MODEL)) * a).astype(BF16)
    iv_ref[...] = sec(_O_I, HG_WIDTH).astype(BF16)


def _inproj(x, p, expert_weights=None):
    T = x.shape[0]
    tm = INPROJ_TILE
    nt = T // tm
    nh = 2 * tm // (HG_CHUNK // 2)
    tok = lambda w: pl.BlockSpec((tm, w), lambda i: (i, 0))
    outs = (
        jax.ShapeDtypeStruct((T, D_MODEL), BF16),
        jax.ShapeDtypeStruct((T, HG_WIDTH), BF16),
        jax.ShapeDtypeStruct((T, HG_WIDTH), F32),
        jax.ShapeDtypeStruct((T, HG_WIDTH), F32),
        jax.ShapeDtypeStruct((T, HG_WIDTH), BF16),
        jax.ShapeDtypeStruct((T, HG_WIDTH), BF16),
        jax.ShapeDtypeStruct((T, D_MODEL), BF16),
        jax.ShapeDtypeStruct((nt, nh, LANES), F32),
    )
    in_specs = [
        tok(D_MODEL),
        _const_spec((1, D_MODEL)), _const_spec((1, D_MODEL)),
        _const_spec((D_MODEL, N_IN)),
        _const_spec((1, A_WIDTH)), _const_spec((1, A_WIDTH)),
        _const_spec((A_GROUPS // 2, A_CHUNK, 2 * A_CHUNK)),
        _const_spec((A_CHUNK, A_WIDTH)),
        _const_spec((2, HG_WIDTH)),
        _const_spec((A_WIDTH, D_MODEL)),
    ]
    out_specs = [tok(D_MODEL), tok(HG_WIDTH), tok(HG_WIDTH), tok(HG_WIDTH), tok(HG_WIDTH),
                 tok(HG_WIDTH), tok(D_MODEL), pl.BlockSpec((1, nh, LANES), lambda i: (i, 0, 0))]
    args = [x, p["ln_in_g"], p["ln_in_b"], p["w_in"], p["a_ln_g"], p["a_ln_b"], p["wsp"], p["sbf"],
            p["lb"], p["w_pa"]]
    if expert_weights is not None:
        slabs = [w.reshape(nt, w.shape[0] * w.shape[1] // nt, w.shape[2]) for w in expert_weights]
        w_specs = [pl.BlockSpec((1,) + w.shape[1:], lambda i: (i, 0, 0)) for w in slabs]
        in_specs += w_specs
        out_specs += w_specs
        outs += tuple(jax.ShapeDtypeStruct(w.shape, BF16) for w in slabs)
        args += slabs
    res = pl.pallas_call(
        _inproj_kernel,
        grid=(nt,),
        in_specs=in_specs,
        out_specs=tuple(out_specs),
        out_shape=outs,
        compiler_params=pltpu.CompilerParams(
            dimension_semantics=("parallel",), vmem_limit_bytes=VMEM_MIB["inproj"] << 20),
        name="inproj",
    )(*args)
    return res[:8], tuple(r.reshape(w.shape) for r, w in zip(res[8:], expert_weights or ()))


_NT = (((1,), (1,)), ((), ()))
_TN = (((0,), (0,)), ((), ()))


def _lockstep(*phased):
    phased = list(phased)
    while phased:
        phased = [g for g in phased if next(g, None) is not None]


def _hgrn_direction(q_ref, g_ref, v_ref, st_ref, o_ref, tri, mask, fwd):
    C = q_ref.shape[0]
    g = g_ref[...]
    ghi = g.astype(BF16)
    glo = (g - ghi.astype(F32)).astype(BF16)
    b = jnp.dot(jnp.concatenate([tri, tri], axis=1), jnp.concatenate([ghi, glo], axis=0),
                preferred_element_type=F32)
    mid = C // 2 - 1 if fwd else C // 2
    end = C - 1 if fwd else 0
    r = b[mid:mid + 1, :]
    b_end = b[end:end + 1, :]
    qt = q_ref[...].astype(F32) * jnp.exp(b - r)
    kt = (1.0 - jnp.exp(g)) * jnp.exp(r - b)
    qtb = qt.astype(BF16)
    ktb = kt.astype(BF16)
    qhb = (qt * jnp.exp(r)).astype(BF16)
    khb = (kt * jnp.exp(b_end - r)).astype(BF16)
    decay = jnp.exp(b_end)
    v = v_ref[...]
    heads =[slice(h * HG_DK, (h + 1) * HG_DK) for h in range(HG_HEADS)]
    zero = jnp.zeros((C, HG_DK), BF16)
    mask2 = jnp.concatenate([mask, mask], axis=1)
    yield True
    scores = []
    for p in range(HG_HEADS // 2):
        k1, k2 = ktb[:, heads[2 * p]], ktb[:, heads[2 * p + 1]]
        kk = jnp.concatenate([jnp.concatenate([k1, zero], axis=1),
                              jnp.concatenate([zero, k2], axis=1)], axis=0)
        s2 = lax.dot_general(qtb[:, 2 * p * HG_DK:(2 * p + 2) * HG_DK], kk, _NT,
                             preferred_element_type=F32)
        s2 = jnp.where(mask2, s2, 0.0).astype(BF16)
        scores += [s2[:, :C], s2[:, C:]]
    yield True
    for h, sl in enumerate(heads):
        st = st_ref[h]
        vt = v[:, sl].T
        o_ref[:, sl] = lax.dot_general(
            jnp.concatenate([scores[h], qhb[:, sl]], axis=1),
            jnp.concatenate([vt, st.astype(BF16)], axis=1), _NT,
            preferred_element_type=F32).astype(o_ref.dtype)
        st_ref[h] = st * decay[:, sl] + jnp.dot(vt, khb[:, sl], preferred_element_type=F32)


def _hgrn_direction_stepwise(q_ref, g_ref, v_ref, st_ref, o_ref, q32_ref, v32_ref, o32_ref, fwd):
    C = q_ref.shape[0]
    q32_ref[...] = q_ref[...].astype(F32)
    v32_ref[...] = v_ref[...].astype(F32)
    sub = SUBLANES
    rows = lax.broadcasted_iota(jnp.int32, (sub, HG_DK), 0)

    def group(i, carry):
        base = pl.multiple_of((i if fwd else C // sub - 1 - i) * sub, sub)
        f = jnp.exp(g_ref[pl.ds(base, sub), :])
        k = 1.0 - f
        q = q32_ref[pl.ds(base, sub), :]
        v = v32_ref[pl.ds(base, sub), :]
        for h in range(HG_HEADS):
            sl = slice(h * HG_DK, (h + 1) * HG_DK)
            st = st_ref[h]
            out = jnp.zeros((sub, HG_DK), F32)
            for r in (range(sub) if fwd else range(sub - 1, -1, -1)):
                v_t = jnp.where(rows == 0, v[r:r + 1, sl], 0.0).astype(BF16)
                k_t = jnp.broadcast_to(k[r:r + 1, sl], (sub, HG_DK)).astype(BF16)
                st = st * f[r:r + 1, sl] + lax.dot_general(v_t, k_t, _TN, preferred_element_type=F32)
                q_t = jnp.broadcast_to(q[r:r + 1, sl], (sub, HG_DK)).astype(BF16)
                o_t = lax.dot_general(q_t, st.astype(BF16), _NT, preferred_element_type=F32)
                out = jnp.where(rows == r, o_t, out)
            st_ref[h] = st
            o32_ref[pl.ds(base, sub), sl] = out
        return carry

    lax.fori_loop(0, C // sub, group, 0)
    o_ref[...] = o32_ref[...].astype(o_ref.dtype)


def _hgrn_kernel(safe_ref, qf_ref, qb_ref, gf_ref, gb_ref, vf_ref, vb_ref, tril_ref, triu_ref,
                 of_ref, ob_ref, sf_ref, sb_ref, q32_ref, v32_ref, o32_ref):
    b, j = pl.program_id(0), pl.program_id(1)
    ns = pl.num_programs(1)
    n = HG_CHUNKS_PER_STEP
    C = qf_ref.shape[0] // n

    @pl.when(j == 0)
    def _():
        sf_ref[...] = jnp.zeros_like(sf_ref)
        sb_ref[...] = jnp.zeros_like(sb_ref)

    row = lax.broadcasted_iota(jnp.int32, (C, C), 0)
    col = lax.broadcasted_iota(jnp.int32, (C, C), 1)
    safe_f = [safe_ref[0, (b * ns + j) * n + u] != 0 for u in range(n)]
    safe_b = [safe_ref[1, (b * ns + ns - 1 - j) * n + u] != 0 for u in range(n)]
    part = lambda ref, u: ref.at[pl.ds(u * C, C), :]

    def forward(u, stepwise):
        refs = (part(qf_ref, u), part(gf_ref, u), part(vf_ref, u), sf_ref, part(of_ref, u))
        if stepwise:
            _hgrn_direction_stepwise(*refs, q32_ref, v32_ref, o32_ref, True)
            return iter(())
        return _hgrn_direction(*refs, tril_ref[...], row >= col, True)

    def backward(u, stepwise):
        refs = (part(qb_ref, u), part(gb_ref, u), part(vb_ref, u), sb_ref, part(ob_ref, u))
        if stepwise:
            _hgrn_direction_stepwise(*refs, q32_ref, v32_ref, o32_ref, False)
            return iter(())
        return _hgrn_direction(*refs, triu_ref[...], row <= col, False)

    all_safe = functools.reduce(jnp.logical_and, safe_f + safe_b)

    @pl.when(all_safe)
    def _():
        _lockstep(*[d for u in range(n) for d in (forward(u, False), backward(n - 1 - u, False))])

    @pl.when(jnp.logical_not(all_safe))
    def _():
        run = lambda direction, u, stepwise: lambda: _lockstep(direction(u, stepwise))
        for u in range(n):
            pl.when(safe_f[u])(run(forward, u, False))
            pl.when(jnp.logical_not(safe_f[u]))(run(forward, u, True))
        for u in reversed(range(n)):
            pl.when(safe_b[u])(run(backward, u, False))
            pl.when(jnp.logical_not(safe_b[u]))(run(backward, u, True))


def _hgrn(q, gf, gb, iv, gmin, batch):
    T = q.shape[0]
    C = HG_CHUNK
    nc = T // batch // C
    nt, nh, _ = gmin.shape
    halves = gmin[:, :, 0].reshape(nt, 2, nh // 4, 2)
    safe = (jnp.min(halves, axis=-1) > -HG_SAFE_LOGDECAY).astype(jnp.int32)
    safe = safe.transpose(1, 0, 2).reshape(2, T // C)
    n = HG_CHUNKS_PER_STEP
    ns = nc // n
    fwd = pl.BlockSpec((n * C, HG_WIDTH), lambda b, j, s: (b * ns + j, 0))
    bwd = pl.BlockSpec((n * C, HG_WIDTH), lambda b, j, s: (b * ns + ns - 1 - j, 0))
    const = lambda shape: pl.BlockSpec(shape, lambda b, j, s: (0,) * len(shape),
                                       pipeline_mode=pl.Buffered(1))
    row = lax.broadcasted_iota(jnp.int32, (C, C), 0)
    col = lax.broadcasted_iota(jnp.int32, (C, C), 1)
    tril = (row >= col).astype(BF16)
    triu = (row <= col).astype(BF16)
    grid_spec = pltpu.PrefetchScalarGridSpec(
        num_scalar_prefetch=1,
        grid=(batch, ns),
        in_specs=[fwd, bwd, fwd, bwd, fwd, bwd, const((C, C)), const((C, C))],
        out_specs=(fwd, bwd),
        scratch_shapes=[pltpu.VMEM((HG_HEADS, HG_DK, HG_DK), F32),
                        pltpu.VMEM((HG_HEADS, HG_DK, HG_DK), F32),
                        pltpu.VMEM((C, HG_WIDTH), F32), pltpu.VMEM((C, HG_WIDTH), F32),
                        pltpu.VMEM((C, HG_WIDTH), F32)],
    )
    return pl.pallas_call(
        _hgrn_kernel,
        grid_spec=grid_spec,
        out_shape=(jax.ShapeDtypeStruct((T, HG_WIDTH), BF16), jax.ShapeDtypeStruct((T, HG_WIDTH), BF16)),
        compiler_params=pltpu.CompilerParams(
            dimension_semantics=("parallel", "arbitrary"), vmem_limit_bytes=VMEM_MIB["hgrn"] << 20),
        name="hgrn",
    )(safe, q, q, gf, gb, iv, iv, tril, triu)


def _mix_kernel(x_ref, lng_ref, lnb_ref, of_ref, ob_ref, so_ref, sgb_ref, ag_ref, ng_ref, wpb_ref,
                wo_ref, l1g_ref, l1b_ref, wr_ref, rb_ref, tri_ref, x1_ref, x1p_ref, route_ref,
                cnt_ref, carry_ref):
    tm = x_ref.shape[0]

    @pl.when(pl.program_id(0) == 0)
    def _():
        carry_ref[...] = jnp.zeros_like(carry_ref)

    o = of_ref[...].astype(F32) + ob_ref[...].astype(F32)
    heads = []
    for h in range(HG_HEADS):
        oh = o[:, h * HG_DK:(h + 1) * HG_DK]
        heads.append(oh * lax.rsqrt(jnp.mean(oh * oh, axis=-1, keepdims=True) + RMS_EPS))
    rn = jnp.concatenate(heads, axis=1) * ng_ref[...] * so_ref[...].astype(F32)
    r = _bdot(rn, wpb_ref[...])
    mixed = ag_ref[...].astype(F32) + sgb_ref[...].astype(F32) * r
    y = _bdot(mixed, wo_ref[...])
    xn = _layer_norm(x_ref[...], lng_ref[...], lnb_ref[...])
    x1 = _layer_norm(ALPHA * xn + y, l1g_ref[...], l1b_ref[...])
    x1_ref[...] = x1
    x1p_ref[...] = _pack_rows(x1)

    neg = jnp.float32(-jnp.inf)
    reps = tm // LANES
    scores = jax.nn.sigmoid(lax.dot_general(wr_ref[...], x1.astype(BF16), _NT,
                                            preferred_element_type=F32))
    biased = (scores + jnp.concatenate([rb_ref[...]] * reps, axis=1)).reshape(
        N_GROUPS, GROUP_SIZE, tm)
    sub = lax.broadcasted_iota(jnp.int32, biased.shape, 1).astype(F32)
    m1 = jnp.max(biased, axis=1, keepdims=True)
    first = jnp.min(jnp.where(biased == m1, sub, float(GROUP_SIZE)), axis=1, keepdims=True)
    m2 = jnp.max(jnp.where(sub == first, neg, biased), axis=1, keepdims=True)
    gs = (m1 + m2).reshape(N_GROUPS, tm)
    grp = lax.broadcasted_iota(jnp.int32, (N_GROUPS, tm), 0)
    ahead = jnp.zeros((N_GROUPS, tm), F32)
    for d in range(1, N_GROUPS):
        other = pltpu.roll(gs, d, 0)
        tie = jnp.where(grp >= d, 1.0, 0.0)
        ahead = ahead + jnp.where(other > gs, 1.0, jnp.where(other == gs, tie, 0.0))
    keep = (ahead < TOPK_GROUPS).reshape(N_GROUPS, 1, tm)
    allowed = jnp.where(keep, biased, neg).reshape(N_EXPERTS, tm)
    row = lax.broadcasted_iota(jnp.int32, (N_EXPERTS, tm), 0).astype(F32)
    sel = jnp.zeros((N_EXPERTS, tm), F32)
    picks = []
    for _ in range(TOP_K):
        m = jnp.max(allowed, axis=0, keepdims=True)
        first = jnp.min(jnp.where(allowed == m, row, float(N_EXPERTS)), axis=0, keepdims=True)
        hit = row == first
        picks.append((first, hit, jnp.sum(jnp.where(hit, scores, 0.0), axis=0, keepdims=True)))
        sel = jnp.where(hit, 1.0, sel)
        allowed = jnp.where(hit, neg, allowed)
    wsum = picks[0][2]
    for pk in picks[1:]:
        wsum = wsum + pk[2]
    selb = sel.astype(BF16)
    carry = carry_ref[...]
    before = (jnp.dot(selb, tri_ref[...], preferred_element_type=F32)
              + jnp.concatenate([carry] * reps, axis=1))
    total = carry + jnp.dot(selb, jnp.ones((tm, LANES), BF16), preferred_element_type=F32)
    carry_ref[...] = total
    cnt_ref[...] = total
    blank = [jnp.zeros((1, tm), F32)] * (SUBLANES - TOP_K)
    route_ref[...] = jnp.concatenate(
        [pk[0] for pk in picks] + blank
        + [jnp.sum(jnp.where(pk[1], before, 0.0), axis=0, keepdims=True) for pk in picks] + blank
        + [pk[2] / wsum * ROUTED_SCALE for pk in picks] + blank, axis=0)


def _mix(x, of, ob, so, sgb, ag, p):
    T = x.shape[0]
    tm = TOKEN_TILE
    tok = lambda w: pl.BlockSpec((tm, w), lambda i: (i, 0))
    row = lax.broadcasted_iota(jnp.int32, (tm, tm), 0)
    col = lax.broadcasted_iota(jnp.int32, (tm, tm), 1)
    tri = (row < col).astype(BF16)
    return pl.pallas_call(
        _mix_kernel,
        grid=(T // tm,),
        in_specs=[
            tok(D_MODEL), _const_spec((1, D_MODEL)), _const_spec((1, D_MODEL)),
            tok(HG_WIDTH), tok(HG_WIDTH), tok(HG_WIDTH), tok(D_MODEL), tok(D_MODEL),
            _const_spec((1, HG_WIDTH)),
            _const_spec((HG_WIDTH, D_MODEL)), _const_spec((D_MODEL, D_MODEL)),
            _const_spec((1, D_MODEL)), _const_spec((1, D_MODEL)),
            _const_spec((N_EXPERTS, D_MODEL)), _const_spec((N_EXPERTS, LANES)),
            _const_spec((tm, tm)),
        ],
        out_specs=(tok(D_MODEL), tok(D_MODEL // 2), pl.BlockSpec((ROUTE_ROWS, tm), lambda i: (0, i)),
                   pl.BlockSpec((N_EXPERTS, LANES), lambda i: (0, 0))),
        out_shape=(jax.ShapeDtypeStruct((T, D_MODEL), F32),
                   jax.ShapeDtypeStruct((T, D_MODEL // 2), jnp.uint32),
                   jax.ShapeDtypeStruct((ROUTE_ROWS, T), F32),
                   jax.ShapeDtypeStruct((N_EXPERTS, LANES), F32)),
        scratch_shapes=[pltpu.VMEM((N_EXPERTS, LANES), F32)],
        compiler_params=pltpu.CompilerParams(
            dimension_semantics=("arbitrary",), vmem_limit_bytes=VMEM_MIB["mix"] << 20),
        name="mix",
    )(x, p["ln_in_g"], p["ln_in_b"], of, ob, so, sgb, ag, p["hg_norm_g"], p["w_pb"], p["w_o"],
      p["ln1_g"], p["ln1_b"], p["w_router"], p["router_bias"], tri)


def _sc_mesh():
    return plsc.VectorSubcoreMesh(core_axis_name="c", subcore_axis_name="s",
                                  num_cores=SC_CORES, num_subcores=SC_SUBCORES)


def _sc_worker():
    return lax.axis_index("s") * SC_CORES + lax.axis_index("c")


def _sc_dispatch(x1p, dest, n_rows):
    T, w = x1p.shape
    W = SC_WINDOW
    per_worker = T // W // (SC_CORES * SC_SUBCORES)

    def body(x_hbm, d_hbm, o_hbm, rows_v, idx_v, sem):
        first = _sc_worker() * per_worker
        for k in range(TOP_K):
            pltpu.sync_copy(d_hbm.at[k, pl.ds(first, per_worker)], idx_v.at[k])

        @pl.loop(0, per_worker)
        def _(j):
            pltpu.sync_copy(x_hbm.at[pl.ds((first + j) * W, W)], rows_v)
            copies = [pltpu.async_copy(rows_v, o_hbm.at[idx_v.at[k, j]], sem) for k in range(TOP_K)]
            for c in copies:
                c.wait()

    return pl.kernel(
        body,
        out_type=jax.ShapeDtypeStruct((n_rows, w), jnp.uint32),
        mesh=_sc_mesh(),
        scratch_types=[pltpu.VMEM((W, w), jnp.uint32), pltpu.VMEM((TOP_K, per_worker, W), jnp.int32),
                       pltpu.SemaphoreType.DMA],
        name="sc_dispatch",
    )(x1p, dest.reshape(TOP_K, T // W, W))


def _sc_combine(ys, dest, gates):
    T = dest.shape[1]
    W = SC_COMBINE_WINDOW
    w = ys.shape[1]
    per_worker = T // W // (SC_CORES * SC_SUBCORES)
    assert per_worker % 2 == 0 and per_worker * W * SC_CORES * SC_SUBCORES == T
    assert 2 * W == SC_LANES

    def body(y_hbm, d_hbm, g_hbm, o_hbm, idx_v, gate_v, *scratch):
        slots = [dict(rows=scratch[s], out=scratch[2 + s], gsem=scratch[4 + s], wsem=scratch[6 + s],
                      lane0=s * W) for s in range(2)]
        first = _sc_worker() * per_worker
        for k in range(TOP_K):
            pltpu.sync_copy(d_hbm.at[k, pl.ds(first * W, per_worker * W)], idx_v.at[k])
            pltpu.sync_copy(g_hbm.at[k, pl.ds(first * W, per_worker * W)], gate_v.at[k])

        def loads(b, win):
            i = win - first
            return [pltpu.make_async_copy(y_hbm.at[idx_v.at[k, pl.ds(i * W, W)]],
                                          b["rows"].at[k], b["gsem"]) for k in range(TOP_K)]

        def write_back(b, win):
            return pltpu.make_async_copy(b["out"], o_hbm.at[pl.ds(win * W, W)], b["wsem"])

        def fetch(b, win):
            for k, c in enumerate(loads(b, win)):
                c.start(priority=k % 2)

        def rounded(x):
            u = plsc.bitcast(x, jnp.uint32)
            r = u + (jnp.uint32(0x7FFF) + ((u >> 16) & jnp.uint32(1)))
            return jnp.where(x != x, jnp.uint32(0x7FC00000), r)

        def reduce_rows(b, pair_gates):
            rows, out = b["rows"], b["out"]
            lane = lax.iota(jnp.int32, SC_LANES)

            @pl.loop(0, W)
            def _(j):
                mine = lane == b["lane0"] + j
                gate = [jnp.broadcast_to(jnp.sum(jnp.where(mine, g, 0.0)), (SC_LANES,))
                        for g in pair_gates]

                @plsc.parallel_loop(0, w, step=SC_LANES, unroll=4)
                def _(col):
                    lo, hi = [], []
                    for k in range(TOP_K):
                        p = rows[k, j, pl.ds(col, SC_LANES)]
                        lo.append(plsc.bitcast(p << 16, F32) * gate[k])
                        hi.append(plsc.bitcast(p & jnp.uint32(0xFFFF0000), F32) * gate[k])
                    lo = rounded((lo[0] + lo[1]) + (lo[2] + lo[3]) + (lo[4] + lo[5]))
                    hi = rounded((hi[0] + hi[1]) + (hi[2] + hi[3]) + (hi[4] + hi[5]))
                    out[j, pl.ds(col, SC_LANES)] = (hi & jnp.uint32(0xFFFF0000)) | (lo >> 16)

        def process(b, win, pair_gates, not_first):
            for c in loads(b, win):
                c.wait()

            @pl.when(not_first)
            def _():
                write_back(b, win).wait()

            reduce_rows(b, pair_gates)
            write_back(b, win).start()

        fetch(slots[0], first)

        @pl.loop(0, per_worker, step=2)
        def _(i):
            win = first + i
            pair_gates = [gate_v[k, pl.ds(i * W, SC_LANES)] for k in range(TOP_K)]
            fetch(slots[1], win + 1)
            process(slots[0], win, pair_gates, i > 0)

            @pl.when(i + 2 < per_worker)
            def _():
                fetch(slots[0], win + 2)

            process(slots[1], win + 1, pair_gates, i > 0)

        for b in slots:
            write_back(b, first).wait()

    slot_types = ([pltpu.VMEM((TOP_K, per_worker * W), jnp.int32),
                   pltpu.VMEM((TOP_K, per_worker * W), F32)]
                  + [pltpu.VMEM((TOP_K, W, w), jnp.uint32)] * 2
                  + [pltpu.VMEM((W, w), jnp.uint32)] * 2
                  + [pltpu.SemaphoreType.DMA] * 4)
    return pl.kernel(
        body,
        out_type=jax.ShapeDtypeStruct((T, w), jnp.uint32),
        mesh=_sc_mesh(),
        scratch_types=slot_types,
        compiler_params=pltpu.CompilerParams(needs_layout_passes=False),
        name="sc_combine",
    )(ys, dest, gates)


def _experts_kernel(be_ref, nv_ref, io_ref, xs_ref, wg_ref, wu_ref, wd_ref, ys_ref):
    del be_ref, io_ref
    n_valid = nv_ref[pl.program_id(0)]
    mb, w = xs_ref.shape

    @pl.when(n_valid > 0)
    def _():
        keep = lax.broadcasted_iota(jnp.int32, (mb, w), 0) < n_valid
        lo, hi = _unpack_rows(jnp.where(keep, xs_ref[...], jnp.uint32(0)))
        lo, hi = lo.astype(BF16), hi.astype(BF16)

        def proj(w_ref):
            return (jnp.dot(lo, w_ref[0, :w, :], preferred_element_type=F32)
                    + jnp.dot(hi, w_ref[0, w:, :], preferred_element_type=F32))

        hb = jax.nn.silu(proj(wg_ref)) * proj(wu_ref)
        ys_ref[...] = _pack_rows(jnp.dot(hb.astype(BF16), wd_ref[0], preferred_element_type=F32))


def _experts(xs, blk_exp, n_valid, blk_io, weights):
    n_rows, w = xs.shape
    mb = n_rows // blk_exp.shape[0]
    by_expert = lambda shape: pl.BlockSpec((1,) + shape, lambda i, be, nv, io: (be[i], 0, 0))
    w_specs = [by_expert((D_MODEL, EXPERT_DIM)), by_expert((D_MODEL, EXPERT_DIM)),
               by_expert((EXPERT_DIM, D_MODEL))]
    ys_spec = pl.BlockSpec((mb, w), lambda i, be, nv, io: (io[i], 0))
    ys_shape = jax.ShapeDtypeStruct((n_rows, w), jnp.uint32)
    grid_spec = pltpu.PrefetchScalarGridSpec(
        num_scalar_prefetch=3,
        grid=(n_rows // mb,),
        in_specs=[pl.BlockSpec((mb, w), lambda i, be, nv, io: (io[i], 0))] + w_specs,
        out_specs=ys_spec,
    )
    return pl.pallas_call(
        _experts_kernel,
        grid_spec=grid_spec,
        out_shape=ys_shape,
        compiler_params=pltpu.CompilerParams(
            dimension_semantics=("arbitrary",), vmem_limit_bytes=VMEM_MIB["experts"] << 20),
        name="experts",
    )(blk_exp, n_valid, blk_io, xs, *weights)


def _final_kernel(x1_ref, routed_ref, wsg_ref, wsu_ref, wsd_ref, l2g_ref, l2b_ref, out_ref):
    x1 = x1_ref[...]
    xb = x1.astype(BF16)
    hs = (jax.nn.silu(jnp.dot(xb, wsg_ref[...], preferred_element_type=F32))
          * jnp.dot(xb, wsu_ref[...], preferred_element_type=F32))
    shared = jnp.dot(hs.astype(BF16), wsd_ref[...], preferred_element_type=F32)
    routed = jnp.concatenate(_unpack_rows(routed_ref[...]), axis=1)
    out_ref[...] = _layer_norm(ALPHA * x1 + (routed + shared), l2g_ref[...], l2b_ref[...])


def _final(x1, routed, p):
    T = x1.shape[0]
    tm = FINAL_TILE
    tok = lambda w: pl.BlockSpec((tm, w), lambda i: (i, 0))
    return pl.pallas_call(
        _final_kernel,
        grid=(T // tm,),
        in_specs=[
            tok(D_MODEL), tok(D_MODEL // 2),
            _const_spec((D_MODEL, SHARED_DIM)), _const_spec((D_MODEL, SHARED_DIM)),
            _const_spec((SHARED_DIM, D_MODEL)),
            _const_spec((1, D_MODEL)), _const_spec((1, D_MODEL)),
        ],
        out_specs=tok(D_MODEL),
        out_shape=jax.ShapeDtypeStruct((T, D_MODEL), F32),
        compiler_params=pltpu.CompilerParams(
            dimension_semantics=("parallel",), vmem_limit_bytes=VMEM_MIB["final"] << 20),
        name="final",
    )(x1, routed, p["w_sh_gate"], p["w_sh_up"], p["w_sh_down"], p["ln2_g"], p["ln2_b"])


def _routing_layout(route, counts, n_tokens):
    mb = EXPERT_BLOCK
    n_blocks = -(-n_tokens * TOP_K // mb) + N_EXPERTS
    idx = route[0:TOP_K].astype(jnp.int32)
    rank = route[SUBLANES:SUBLANES + TOP_K].astype(jnp.int32)
    counts = counts.astype(jnp.int32)
    padded = (counts + mb - 1) // mb * mb
    pad_end = jnp.cumsum(padded)
    pad_start = pad_end - padded
    experts = jnp.arange(N_EXPERTS, dtype=jnp.int32)
    dest = rank + jnp.sum(jnp.where(idx[:, :, None] == experts, pad_start, 0), axis=-1)
    blk_start = jnp.arange(n_blocks, dtype=jnp.int32) * mb
    blk_exp = jnp.minimum(
        jnp.sum((pad_end[None, :] <= blk_start[:, None]).astype(jnp.int32), axis=1), N_EXPERTS - 1)
    valid_end = jnp.sum(jnp.where(blk_exp[:, None] == experts, pad_start + counts, 0), axis=-1)
    n_valid = jnp.clip(valid_end - blk_start, 0, mb).astype(jnp.int32)
    blk_io = jnp.minimum(jnp.arange(n_blocks, dtype=jnp.int32), pad_end[-1] // mb - 1)
    gates = route[2 * SUBLANES:2 * SUBLANES + TOP_K]
    return dest, gates, blk_exp, n_valid, blk_io, n_blocks * mb


def _encode(x, p, expert_weights):
    batch, seq, _ = x.shape
    T = batch * seq
    xt = x.reshape(T, D_MODEL)
    if expert_weights[0].dtype == BF16:
        (ag, q, gf, gb, iv, so, sgb, gmin), _ = _inproj(xt, p)
    else:
        (ag, q, gf, gb, iv, so, sgb, gmin), expert_weights = _inproj(xt, p, expert_weights)
    of, ob = _hgrn(q, gf, gb, iv, gmin, batch)
    x1, x1p, route, cnt = _mix(xt, of, ob, so, sgb, ag, p)
    dest, gates, blk_exp, n_valid, blk_io, n_rows = _routing_layout(route, cnt[:, 0], T)
    xs = _sc_dispatch(x1p, dest, n_rows)
    ys = _experts(xs, blk_exp, n_valid, blk_io, expert_weights)
    out = _final(x1, _sc_combine(ys, dest, gates), p)
    return out.reshape(batch, seq, D_MODEL), expert_weights


def _prepare_params(ln_in_g, ln_in_b, w_in, a_ln_g, a_ln_b, a_ws, a_sb, hg_lb_logits, hg_norm_g,
                    w_pa, w_pb, w_o, ln1_g, ln1_b, w_router, router_bias, w_sh_gate, w_sh_up,
                    w_sh_down, ln2_g, ln2_b):
    l = 0
    row = lambda v: v.reshape(1, -1).astype(F32)
    ws = a_ws[l].astype(BF16)
    wsp = jnp.concatenate([ws[0::2], ws[1::2]], axis=2)
    sbf = jnp.repeat(a_sb[l].astype(F32), A_WIDTH // A_GROUPS, axis=1)
    lb = jnp.cumsum(jax.nn.softmax(hg_lb_logits.astype(F32), axis=1), axis=1)[:, l]
    return dict(
        ln_in_g=row(ln_in_g), ln_in_b=row(ln_in_b), w_in=w_in[l].astype(BF16),
        a_ln_g=row(a_ln_g[l]), a_ln_b=row(a_ln_b[l]), wsp=wsp, sbf=sbf, lb=lb,
        w_pa=w_pa[l].astype(BF16), hg_norm_g=row(hg_norm_g[l]),
        w_pb=w_pb[l].astype(BF16), w_o=w_o[l].astype(BF16),
        ln1_g=row(ln1_g[l]), ln1_b=row(ln1_b[l]),
        w_router=w_router[l].T.astype(BF16),
        router_bias=jnp.broadcast_to(router_bias[l].astype(F32)[:, None], (N_EXPERTS, LANES)),
        w_sh_gate=w_sh_gate[l].astype(BF16), w_sh_up=w_sh_up[l].astype(BF16),
        w_sh_down=w_sh_down[l].astype(BF16),
        ln2_g=row(ln2_g[l]), ln2_b=row(ln2_b[l]),
    )


def kernel(x_prompt, x_sample, ln_in_g, ln_in_b, w_in, a_ln_g, a_ln_b, a_ws, a_sb, hg_lb_logits,
           hg_norm_g, w_pa, w_pb, w_o, ln1_g, ln1_b, w_router, router_bias, w_e_gate, w_e_up,
           w_e_down, w_sh_gate, w_sh_up, w_sh_down, ln2_g, ln2_b):
    p = _prepare_params(ln_in_g, ln_in_b, w_in, a_ln_g, a_ln_b, a_ws, a_sb, hg_lb_logits, hg_norm_g,
                        w_pa, w_pb, w_o, ln1_g, ln1_b, w_router, router_bias, w_sh_gate, w_sh_up,
                        w_sh_down, ln2_g, ln2_b)
    y_prompt, expert_weights = _encode(x_prompt, p, (w_e_gate[0], w_e_up[0], w_e_down[0]))
    y_sample, _ = _encode(x_sample, p, expert_weights)
    return y_prompt, y_sample
```
